```python
import math
import jax
import jax.numpy as jnp
from jax import lax
import numpy as np

D_MODEL = 1024
BATCH = 8
SEQ = 8192
DEPTH = 2

GRID_W = 64
CTX_LEN = 256
EPS = 1e-6
ROPE_BASE = 10000.0

SSM_HEADS = 16
SSM_HEAD_DIM = 64
SSM_INNER = SSM_HEADS * SSM_HEAD_DIM
SSM_GROUPS = 2
SSM_STATE = 128
SSM_CONV = 5
SSM_CHUNK = 128
SSM_CONV_DIM = SSM_INNER + 2 * SSM_GROUPS * SSM_STATE

SWA_Q_HEADS = 8
SWA_KV_HEADS = 2
SWA_HEAD_DIM = 128
SWA_WINDOW = 128
SWA_BLOCK = 128

MLA_HEADS = 8
MLA_Q_RANK = 384
MLA_KV_RANK = 256
MLA_NOPE = 128
MLA_ROPE = 64
MLA_QK = MLA_NOPE + MLA_ROPE
MLA_V = 128
MLA_Q_BLOCK = 128

N_BRANCH = 3
FFN_HIDDEN = -(-8 * D_MODEL // (3 * 256)) * 256

IN_WIDTHS = (
    SSM_CONV_DIM,
    2 * SSM_HEADS,
    SWA_KV_HEADS * SWA_HEAD_DIM,
    SWA_KV_HEADS * SWA_HEAD_DIM,
    MLA_KV_RANK,
    MLA_ROPE,
    SSM_INNER,
    SWA_Q_HEADS * SWA_HEAD_DIM,
    MLA_Q_RANK,
    N_BRANCH * D_MODEL,
)
N_KV_SPLITS = 6
KV_COLS = sum(IN_WIDTHS[:N_KV_SPLITS])
IN_COLS = sum(IN_WIDTHS)

kernel_name = 'hybrid_ssd_swa_mla_dit_block'

F32 = jnp.float32


def rms_norm(x, g):
    xf = x.astype(F32)
    y = xf * lax.rsqrt(jnp.mean(xf * xf, axis=-1, keepdims=True) + EPS)
    return (y * g.astype(F32)).astype(x.dtype)


def modulate(x, g, shift, scale):
    return rms_norm(x, g) * (1.0 + scale[:, None, :]) + shift[:, None, :]


def split_cols(u, widths):
    offs = np.cumsum(widths)[:-1].tolist()
    return jnp.split(u, offs, axis=-1)


def axial_rope_tables(rows, rot_dim):
    n_freq = rot_dim // 4
    inv = jnp.power(ROPE_BASE, -jnp.arange(n_freq, dtype=F32) / n_freq)
    r, col = jnp.meshgrid(jnp.arange(rows, dtype=F32), jnp.arange(GRID_W, dtype=F32), indexing='ij')
    ang = jnp.stack([r.reshape(-1)[:, None] * inv, col.reshape(-1)[:, None] * inv], axis=1)
    return jnp.cos(ang), jnp.sin(ang)


def apply_axial_rope(x, cos, sin):
    shp = x.shape
    xr = x.astype(F32).reshape(shp[:-1] + (2, 2, shp[-1] // 4))
    x1, x2 = xr[..., 0, :], xr[..., 1, :]
    c = cos[None, :, None]
    s = sin[None, :, None]
    out = jnp.stack([x1 * c - x2 * s, x2 * c + x1 * s], axis=-2)
    return out.reshape(shp).astype(x.dtype)


def centred_dwconv(u, w, b):
    k, ch = w.shape
    out = lax.conv_general_dilated(u, w[:, None, :].astype(u.dtype), window_strides=(1,),
                                   padding=[(k // 2, k // 2)], dimension_numbers=('NWC', 'WIO', 'NWC'),
                                   feature_group_count=ch)
    return out + b


def ssd_scan(xh, dt, a, bm, cm, h0, with_y):
    bsz, L, H, P = xh.shape
    G, N = bm.shape[-2:]
    hpg = H // G
    Q = SSM_CHUNK
    nc = L // Q
    x = xh.astype(F32).reshape(bsz, nc, Q, G, hpg, P)
    dtc = dt.astype(F32).reshape(bsz, nc, Q, G, hpg)
    bc = bm.astype(F32).reshape(bsz, nc, Q, G, N)
    cc = cm.astype(F32).reshape(bsz, nc, Q, G, N)
    acs = jnp.cumsum(dtc * a.astype(F32).reshape(G, hpg), axis=2)
    xdt = x * dtc[..., None]
    decay_end = jnp.exp(acs[:, :, -1:] - acs)
    states = jnp.einsum('bcjgn,bcjghp->bcghpn', bc, xdt * decay_end[..., None])
    chunk_decay = jnp.exp(acs[:, :, -1])

    def step(h, inp):
        s, d = inp
        return h * d[..., None, None] + s, h

    h_t, h_in = lax.scan(step, h0.astype(F32).reshape(bsz, G, hpg, P, N),
                         (jnp.moveaxis(states, 1, 0), jnp.moveaxis(chunk_decay, 1, 0)))
    h_t = h_t.reshape(bsz, H, P, N)
    if not with_y:
        return None, h_t
    h_in = jnp.moveaxis(h_in, 0, 1)
    acs_t = jnp.moveaxis(acs, 2, -1)
    lower = jnp.tril(jnp.ones((Q, Q), dtype=bool))
    seg = jnp.exp(jnp.where(lower, acs_t[..., :, None] - acs_t[..., None, :], -jnp.inf))
    cb = jnp.einsum('bcign,bcjgn->bcgij', cc, bc)
    y_diag = jnp.einsum('bcghij,bcjghp->bcighp', cb[:, :, :, None] * seg, xdt)
    y_off = jnp.einsum('bcign,bcghpn->bcighp', cc, h_in) * jnp.exp(acs)[..., None]
    y = (y_diag + y_off).reshape(bsz, L, H, P)
    return y.astype(xh.dtype), h_t


def ssd_branch(xbc_l, dt_l, z_l, xbc_c, dt_c, z_c, conv_w, conv_b, dt_bias, a_log, d_skip, norm_g):
    a = -jnp.exp(a_log.astype(F32))

    def prep(xbc, dt):
        bsz, n = xbc.shape[:2]
        u = jax.nn.silu(centred_dwconv(xbc, conv_w, conv_b))
        xs, bm, cm = jnp.split(u, [SSM_INNER, SSM_INNER + SSM_GROUPS * SSM_STATE], axis=-1)
        dts = jax.nn.softplus(dt.reshape(bsz, n, 2, SSM_HEADS).astype(F32) + dt_bias.astype(F32))
        return (xs.reshape(bsz, n, SSM_HEADS, SSM_HEAD_DIM), bm.reshape(bsz, n, SSM_GROUPS, SSM_STATE),
                cm.reshape(bsz, n, SSM_GROUPS, SSM_STATE), dts)

    def rev(t):
        return jnp.flip(t, axis=1)

    def bidir(xs, bm, cm, dts, h_f, h_b, with_y):
        y_f, hf = ssd_scan(xs, dts[:, :, 0], a[0], bm, cm, h_f, with_y)
        y_b, hb = ssd_scan(rev(xs), rev(dts[:, :, 1]), a[1], rev(bm), rev(cm), h_b, with_y)
        y = y_f + rev(y_b) + d_skip[:, None] * xs if with_y else None
        return y, hf, hb

    def gated_out(y, z):
        return rms_norm(y.reshape(z.shape) * jax.nn.silu(z), norm_g)

    with_ctx = z_c is not None
    xs_c, b_c, c_c, dts_c = prep(xbc_c, dt_c)
    h0 = jnp.zeros((xs_c.shape[0], SSM_HEADS, SSM_HEAD_DIM, SSM_STATE), F32)
    y_c, hc_f, hc_b = bidir(xs_c, b_c, c_c, dts_c, h0, h0, with_ctx)
    xs_l, b_l, c_l, dts_l = prep(xbc_l, dt_l)
    y_l, _, _ = bidir(xs_l, b_l, c_l, dts_l, hc_f, hc_b, True)
    out_l = gated_out(y_l, z_l)
    out_c = gated_out(y_c, z_c) if with_ctx else None
    return out_l, out_c


def swa_branch(q_l, k_l, v_l, q_c, k_c, v_c, q_g, k_g, sink, rope):
    bsz, L = k_l.shape[:2]
    n_ctx = k_c.shape[1]
    grp = SWA_Q_HEADS // SWA_KV_HEADS
    blk = SWA_BLOCK
    nb = L // blk
    scale = SWA_HEAD_DIM ** -0.5

    def q_heads(q):
        return rms_norm(q.reshape(q.shape[0], q.shape[1], SWA_Q_HEADS, SWA_HEAD_DIM), q_g)

    def kv_heads(k, v):
        shp = (k.shape[0], k.shape[1], SWA_KV_HEADS, SWA_HEAD_DIM)
        return rms_norm(k.reshape(shp), k_g), v.reshape(shp)

    kc, vc = kv_heads(k_c, v_c)
    kl, vl = kv_heads(k_l, v_l)
    kl = apply_axial_rope(kl, *rope)
    ql = apply_axial_rope(q_heads(q_l), *rope)
    sink_f = sink.astype(F32).reshape(SWA_KV_HEADS, grp)

    qb = ql.reshape(bsz, nb, blk, SWA_KV_HEADS, grp, SWA_HEAD_DIM)

    def band(t):
        tp = jnp.pad(t, ((0, 0), (blk, blk), (0, 0), (0, 0))).reshape(bsz, nb + 2, blk, SWA_KV_HEADS, SWA_HEAD_DIM)
        return jnp.concatenate([tp[:, :-2], tp[:, 1:-1], tp[:, 2:]], axis=2)

    kb, vb = band(kl), band(vl)
    s_win = jnp.einsum('bnqhgd,bnkhd->bnhgqk', qb, kb).astype(F32) * scale
    q_idx = jnp.arange(blk)[:, None] + blk
    k_idx = jnp.arange(3 * blk)[None, :]
    k_pos = (jnp.arange(nb)[:, None, None] - 1) * blk + k_idx
    mask = (jnp.abs(k_idx - q_idx) <= SWA_WINDOW)[None] & (k_pos >= 0) & (k_pos < L)
    s_win = jnp.where(mask[None, :, None, None], s_win, -jnp.inf)
    s_ctx = jnp.einsum('bnqhgd,bkhd->bnhgqk', qb, kc).astype(F32) * scale
    s_sink = jnp.broadcast_to(sink_f[None, None, :, :, None, None], s_win.shape[:-1] + (1,))
    p = jax.nn.softmax(jnp.concatenate([s_win, s_ctx, s_sink], axis=-1), axis=-1).astype(vb.dtype)
    o = (jnp.einsum('bnhgqk,bnkhd->bnqhgd', p[..., :3 * blk], vb)
         + jnp.einsum('bnhgqk,bkhd->bnqhgd', p[..., 3 * blk:3 * blk + n_ctx], vc))
    out_l = o.reshape(bsz, L, SWA_Q_HEADS * SWA_HEAD_DIM)
    if q_c is None:
        return out_l, None
    qc = q_heads(q_c).reshape(bsz, n_ctx, SWA_KV_HEADS, grp, SWA_HEAD_DIM)
    s_c = jnp.einsum('bqhgd,bkhd->bhgqk', qc, kc).astype(F32) * scale
    s_cs = jnp.broadcast_to(sink_f[None, :, :, None, None], s_c.shape[:-1] + (1,))
    pc = jax.nn.softmax(jnp.concatenate([s_c, s_cs], axis=-1), axis=-1).astype(vc.dtype)
    out_c = jnp.einsum('bhgqk,bkhd->bqhgd', pc[..., :n_ctx], vc).reshape(bsz, n_ctx, SWA_Q_HEADS * SWA_HEAD_DIM)
    return out_l, out_c


def mla_keys(ckv, kr, kv_lat_g, w_ukv, k_g, rope):
    bsz, n = ckv.shape[:2]
    kv = (rms_norm(ckv, kv_lat_g) @ w_ukv).reshape(bsz, n, MLA_HEADS, MLA_NOPE + MLA_V)
    k_nope = rms_norm(kv[..., :MLA_NOPE], k_g[:MLA_NOPE])
    k_rope = rms_norm(kr[:, :, None, :], k_g[MLA_NOPE:])
    if rope is not None:
        k_rope = apply_axial_rope(k_rope, *rope)
    k = jnp.concatenate([k_nope, jnp.broadcast_to(k_rope, (bsz, n, MLA_HEADS, MLA_ROPE))], axis=-1)
    return k, kv[..., MLA_NOPE:]


def mla_queries(cq, q_lat_g, w_uq, q_g, rope):
    bsz, n = cq.shape[:2]
    q = (rms_norm(cq, q_lat_g) @ w_uq).reshape(bsz, n, MLA_HEADS, MLA_QK)
    q_nope = rms_norm(q[..., :MLA_NOPE], q_g[:MLA_NOPE])
    q_rope = rms_norm(q[..., MLA_NOPE:], q_g[MLA_NOPE:])
    if rope is not None:
        q_rope = apply_axial_rope(q_rope, *rope)
    return jnp.concatenate([q_nope, q_rope], axis=-1)


def mla_branch(cq_l, ckv_l, kr_l, cq_c, ckv_c, kr_c, q_lat_g, kv_lat_g, w_uq, w_ukv, q_g, k_g, rope):
    bsz, L = cq_l.shape[:2]
    n_ctx = ckv_c.shape[1]
    scale = MLA_QK ** -0.5
    k_c, v_c = mla_keys(ckv_c, kr_c, kv_lat_g, w_ukv, k_g, None)
    k_l, v_l = mla_keys(ckv_l, kr_l, kv_lat_g, w_ukv, k_g, rope)
    q_l = mla_queries(cq_l, q_lat_g, w_uq, q_g, rope)
    k_all = jnp.concatenate([k_l, k_c], axis=1)
    v_all = jnp.concatenate([v_l, v_c], axis=1)
    nb = L // MLA_Q_BLOCK
    qb = jnp.moveaxis(q_l.reshape(bsz, nb, MLA_Q_BLOCK, MLA_HEADS, MLA_QK), 1, 0)

    def attend(qblk):
        s = jnp.einsum('bqhd,bkhd->bhqk', qblk, k_all).astype(F32) * scale
        p = jax.nn.softmax(s, axis=-1).astype(v_all.dtype)
        return jnp.einsum('bhqk,bkhd->bqhd', p, v_all)

    out_l = jnp.moveaxis(lax.map(attend, qb), 0, 1).reshape(bsz, L, MLA_HEADS * MLA_V)
    if cq_c is None:
        return out_l, None
    q_c = mla_queries(cq_c, q_lat_g, w_uq, q_g, None)
    s = jnp.einsum('bqhd,bkhd->bhqk', q_c, k_c).astype(F32) * scale
    p = jax.nn.softmax(s, axis=-1).astype(v_c.dtype)
    out_c = jnp.einsum('bhqk,bkhd->bqhd', p, v_c).reshape(bsz, n_ctx, MLA_HEADS * MLA_V)
    return out_l, out_c


def merge_branches(gates, y_ssm, y_swa, y_mla, w_p_ssm, w_p_swa, w_p_mla, w_o):
    g = jax.nn.sigmoid(gates.astype(F32)).astype(gates.dtype)
    g_ssm, g_swa, g_mla = jnp.split(g, N_BRANCH, axis=-1)
    merged = g_ssm * (y_ssm @ w_p_ssm) + g_swa * (y_swa @ w_p_swa) + g_mla * (y_mla @ w_p_mla)
    return merged @ w_o


def swiglu(h, w_in, w_out):
    g, u = jnp.split(h @ w_in, 2, axis=-1)
    return (jax.nn.silu(g) * u) @ w_out


def _fwd_setup_inputs(seed: int = 0) -> dict:
    key = jax.random.key(seed)
    ks = iter(jax.random.split(key, 40))

    def nrm(shape, s):
        return jax.random.normal(next(ks), shape, F32) * s

    def gain(shape):
        return 1.0 + nrm(shape, 0.02)

    x = nrm((BATCH, SEQ, D_MODEL), 1.0)
    c = nrm((BATCH, D_MODEL), 1.0)
    ctx = nrm((BATCH, CTX_LEN, D_MODEL), 1.0)
    c_ctx = nrm((D_MODEL,), 1.0)
    w_mod = nrm((DEPTH, D_MODEL, 6 * D_MODEL), 0.5 * D_MODEL ** -0.5)
    b_mod = nrm((DEPTH, 6 * D_MODEL), 0.01)
    norm1_g = gain((DEPTH, D_MODEL))
    norm2_g = gain((DEPTH, D_MODEL))
    w_in = nrm((DEPTH, D_MODEL, IN_COLS), D_MODEL ** -0.5)
    ssm_conv_w = nrm((DEPTH, SSM_CONV, SSM_CONV_DIM), SSM_CONV ** -0.5)
    ssm_conv_b = nrm((DEPTH, SSM_CONV_DIM), 0.01)
    dt0 = jnp.exp(jax.random.uniform(next(ks), (DEPTH, 2, SSM_HEADS), F32, math.log(1e-3), math.log(1e-1)))
    ssm_dt_bias = dt0 + jnp.log(-jnp.expm1(-dt0))
    ssm_a_log = jnp.log(jax.random.uniform(next(ks), (DEPTH, 2, SSM_HEADS), F32, 1.0, 16.0))
    ssm_d = gain((DEPTH, SSM_HEADS))
    ssm_norm_g = gain((DEPTH, SSM_INNER))
    swa_q_norm_g = gain((DEPTH, SWA_HEAD_DIM))
    swa_k_norm_g = gain((DEPTH, SWA_HEAD_DIM))
    swa_sink = nrm((DEPTH, SWA_Q_HEADS), 0.5)
    mla_q_lat_g = gain((DEPTH, MLA_Q_RANK))
    mla_kv_lat_g = gain((DEPTH, MLA_KV_RANK))
    w_mla_uq = nrm((DEPTH, MLA_Q_RANK, MLA_HEADS * MLA_QK), MLA_Q_RANK ** -0.5)
    w_mla_ukv = nrm((DEPTH, MLA_KV_RANK, MLA_HEADS * (MLA_NOPE + MLA_V)), MLA_KV_RANK ** -0.5)
    mla_q_norm_g = gain((DEPTH, MLA_QK))
    mla_k_norm_g = gain((DEPTH, MLA_QK))
    w_p_ssm = nrm((DEPTH, SSM_INNER, D_MODEL), SSM_INNER ** -0.5)
    w_p_swa = nrm((DEPTH, SWA_Q_HEADS * SWA_HEAD_DIM, D_MODEL), (SWA_Q_HEADS * SWA_HEAD_DIM) ** -0.5)
    w_p_mla = nrm((DEPTH, MLA_HEADS * MLA_V, D_MODEL), (MLA_HEADS * MLA_V) ** -0.5)
    w_out = nrm((DEPTH, D_MODEL, D_MODEL), D_MODEL ** -0.5)
    w_ffn_in = nrm((DEPTH, D_MODEL, 2 * FFN_HIDDEN), D_MODEL ** -0.5)
    w_ffn_out = nrm((DEPTH, FFN_HIDDEN, D_MODEL), FFN_HIDDEN ** -0.5)
    return {'x': x, 'c': c, 'ctx': ctx, 'c_ctx': c_ctx, 'w_mod': w_mod, 'b_mod': b_mod,
            'norm1_g': norm1_g, 'norm2_g': norm2_g, 'w_in': w_in,
            'ssm_conv_w': ssm_conv_w, 'ssm_conv_b': ssm_conv_b, 'ssm_dt_bias': ssm_dt_bias,
            'ssm_a_log': ssm_a_log, 'ssm_d': ssm_d, 'ssm_norm_g': ssm_norm_g,
            'swa_q_norm_g': swa_q_norm_g, 'swa_k_norm_g': swa_k_norm_g, 'swa_sink': swa_sink,
            'mla_q_lat_g': mla_q_lat_g, 'mla_kv_lat_g': mla_kv_lat_g, 'w_mla_uq': w_mla_uq,
            'w_mla_ukv': w_mla_ukv, 'mla_q_norm_g': mla_q_norm_g, 'mla_k_norm_g': mla_k_norm_g,
            'w_p_ssm': w_p_ssm, 'w_p_swa': w_p_swa, 'w_p_mla': w_p_mla, 'w_out': w_out,
            'w_ffn_in': w_ffn_in, 'w_ffn_out': w_ffn_out}


def _fwd_reference(x, c, ctx, c_ctx, w_mod, b_mod, norm1_g, norm2_g, w_in,
              ssm_conv_w, ssm_conv_b, ssm_dt_bias, ssm_a_log, ssm_d, ssm_norm_g,
              swa_q_norm_g, swa_k_norm_g, swa_sink,
              mla_q_lat_g, mla_kv_lat_g, w_mla_uq, w_mla_ukv, mla_q_norm_g, mla_k_norm_g,
              w_p_ssm, w_p_swa, w_p_mla, w_out, w_ffn_in, w_ffn_out):
    n_lat = x.shape[1]
    rows = n_lat // GRID_W
    rope_swa = axial_rope_tables(rows, SWA_HEAD_DIM)
    rope_mla = axial_rope_tables(rows, MLA_ROPE)
    silu_c = jax.nn.silu(c)
    silu_cc = jax.nn.silu(c_ctx)[None]
    x_l, x_c = x, ctx
    for i in range(DEPTH):
        last = i == DEPTH - 1
        sh1, sc1, gt1, sh2, sc2, gt2 = jnp.split(silu_c @ w_mod[i] + b_mod[i], 6, axis=-1)
        csh1, csc1, cgt1, csh2, csc2, cgt2 = jnp.split(silu_cc @ w_mod[i] + b_mod[i], 6, axis=-1)

        h_l = modulate(x_l, norm1_g[i], sh1, sc1)
        h_c = modulate(x_c, norm1_g[i], csh1, csc1)
        (xbc_l, dt_l, k_l, v_l, ckv_l, kr_l, z_l, q_l, cq_l, gates_l) = split_cols(h_l @ w_in[i], IN_WIDTHS)
        if last:
            (xbc_c, dt_c, k_c, v_c, ckv_c, kr_c) = split_cols(h_c @ w_in[i][:, :KV_COLS], IN_WIDTHS[:N_KV_SPLITS])
            z_c = q_c = cq_c = gates_c = None
        else:
            (xbc_c, dt_c, k_c, v_c, ckv_c, kr_c, z_c, q_c, cq_c, gates_c) = split_cols(h_c @ w_in[i], IN_WIDTHS)

        y_ssm_l, y_ssm_c = ssd_branch(xbc_l, dt_l, z_l, xbc_c, dt_c, z_c, ssm_conv_w[i], ssm_conv_b[i],
                                      ssm_dt_bias[i], ssm_a_log[i], ssm_d[i], ssm_norm_g[i])
        y_swa_l, y_swa_c = swa_branch(q_l, k_l, v_l, q_c, k_c, v_c, swa_q_norm_g[i], swa_k_norm_g[i],
                                      swa_sink[i], rope_swa)
        y_mla_l, y_mla_c = mla_branch(cq_l, ckv_l, kr_l, cq_c, ckv_c, kr_c, mla_q_lat_g[i], mla_kv_lat_g[i],
                                      w_mla_uq[i], w_mla_ukv[i], mla_q_norm_g[i], mla_k_norm_g[i], rope_mla)

        x_l = x_l + gt1[:, None] * merge_branches(gates_l, y_ssm_l, y_swa_l, y_mla_l,
                                                  w_p_ssm[i], w_p_swa[i], w_p_mla[i], w_out[i])
        x_l = x_l + gt2[:, None] * swiglu(modulate(x_l, norm2_g[i], sh2, sc2), w_ffn_in[i], w_ffn_out[i])

        if not last:
            x_c = x_c + cgt1[:, None] * merge_branches(gates_c, y_ssm_c, y_swa_c, y_mla_c,
                                                      w_p_ssm[i], w_p_swa[i], w_p_mla[i], w_out[i])
            x_c = x_c + cgt2[:, None] * swiglu(modulate(x_c, norm2_g[i], csh2, csc2), w_ffn_in[i], w_ffn_out[i])
    return x_l


import jax as _jax
import jax.numpy as _jnp

TWIN_FORMAT = 'train_step'
FWD_PARAMS = ['x', 'c', 'ctx', 'c_ctx', 'w_mod', 'b_mod', 'norm1_g', 'norm2_g', 'w_in', 'ssm_conv_w', 'ssm_conv_b', 'ssm_dt_bias', 'ssm_a_log', 'ssm_d', 'ssm_norm_g', 'swa_q_norm_g', 'swa_k_norm_g', 'swa_sink', 'mla_q_lat_g', 'mla_kv_lat_g', 'w_mla_uq', 'w_mla_ukv', 'mla_q_norm_g', 'mla_k_norm_g', 'w_p_ssm', 'w_p_swa', 'w_p_mla', 'w_out', 'w_ffn_in', 'w_ffn_out']
TWIN_WEIGHTS = ['c_ctx', 'w_mod', 'b_mod', 'norm1_g', 'norm2_g', 'w_in', 'ssm_conv_w', 'ssm_conv_b', 'ssm_dt_bias', 'ssm_a_log', 'ssm_d', 'ssm_norm_g', 'swa_q_norm_g', 'swa_k_norm_g', 'swa_sink', 'mla_q_lat_g', 'mla_kv_lat_g', 'w_mla_uq', 'w_mla_ukv', 'mla_q_norm_g', 'mla_k_norm_g', 'w_p_ssm', 'w_p_swa', 'w_p_mla', 'w_out', 'w_ffn_in', 'w_ffn_out']
TWIN_DIFF_INPUT = 'x'
TWIN_INPUTS = ['x', 'c', 'ctx', 'c_ctx', 'w_mod', 'b_mod', 'norm1_g', 'norm2_g', 'w_in', 'ssm_conv_w', 'ssm_conv_b', 'ssm_dt_bias', 'ssm_a_log', 'ssm_d', 'ssm_norm_g', 'swa_q_norm_g', 'swa_k_norm_g', 'swa_sink', 'mla_q_lat_g', 'mla_kv_lat_g', 'w_mla_uq', 'w_mla_ukv', 'mla_q_norm_g', 'mla_k_norm_g', 'w_p_ssm', 'w_p_swa', 'w_p_mla', 'w_out', 'w_ffn_in', 'w_ffn_out', 'loss_target', 'm_c_ctx', 'm_w_mod', 'm_b_mod', 'm_norm1_g', 'm_norm2_g', 'm_w_in', 'm_ssm_conv_w', 'm_ssm_conv_b', 'm_ssm_dt_bias', 'm_ssm_a_log', 'm_ssm_d', 'm_ssm_norm_g', 'm_swa_q_norm_g', 'm_swa_k_norm_g', 'm_swa_sink', 'm_mla_q_lat_g', 'm_mla_kv_lat_g', 'm_w_mla_uq', 'm_w_mla_ukv', 'm_mla_q_norm_g', 'm_mla_k_norm_g', 'm_w_p_ssm', 'm_w_p_swa', 'm_w_p_mla', 'm_w_out', 'm_w_ffn_in', 'm_w_ffn_out', 'v_c_ctx', 'v_w_mod', 'v_b_mod', 'v_norm1_g', 'v_norm2_g', 'v_w_in', 'v_ssm_conv_w', 'v_ssm_conv_b', 'v_ssm_dt_bias', 'v_ssm_a_log', 'v_ssm_d', 'v_ssm_norm_g', 'v_swa_q_norm_g', 'v_swa_k_norm_g', 'v_swa_sink', 'v_mla_q_lat_g', 'v_mla_kv_lat_g', 'v_w_mla_uq', 'v_w_mla_ukv', 'v_mla_q_norm_g', 'v_mla_k_norm_g', 'v_w_p_ssm', 'v_w_p_swa', 'v_w_p_mla', 'v_w_out', 'v_w_ffn_in', 'v_w_ffn_out']
TWIN_OUTPUTS = ['loss', 'grad_x', 'grad_c_ctx', 'grad_w_mod', 'grad_b_mod', 'grad_norm1_g', 'grad_norm2_g', 'grad_w_in', 'grad_ssm_conv_w', 'grad_ssm_conv_b', 'grad_ssm_dt_bias', 'grad_ssm_a_log', 'grad_ssm_d', 'grad_ssm_norm_g', 'grad_swa_q_norm_g', 'grad_swa_k_norm_g', 'grad_swa_sink', 'grad_mla_q_lat_g', 'grad_mla_kv_lat_g', 'grad_w_mla_uq', 'grad_w_mla_ukv', 'grad_mla_q_norm_g', 'grad_mla_k_norm_g', 'grad_w_p_ssm', 'grad_w_p_swa', 'grad_w_p_mla', 'grad_w_out', 'grad_w_ffn_in', 'grad_w_ffn_out', 'delta_c_ctx', 'delta_w_mod', 'delta_b_mod', 'delta_norm1_g', 'delta_norm2_g', 'delta_w_in', 'delta_ssm_conv_w', 'delta_ssm_conv_b', 'delta_ssm_dt_bias', 'delta_ssm_a_log', 'delta_ssm_d', 'delta_ssm_norm_g', 'delta_swa_q_norm_g', 'delta_swa_k_norm_g', 'delta_swa_sink', 'delta_mla_q_lat_g', 'delta_mla_kv_lat_g', 'delta_w_mla_uq', 'delta_w_mla_ukv', 'delta_mla_q_norm_g', 'delta_mla_k_norm_g', 'delta_w_p_ssm', 'delta_w_p_swa', 'delta_w_p_mla', 'delta_w_out', 'delta_w_ffn_in', 'delta_w_ffn_out', 'new_m_c_ctx', 'new_m_w_mod', 'new_m_b_mod', 'new_m_norm1_g', 'new_m_norm2_g', 'new_m_w_in', 'new_m_ssm_conv_w', 'new_m_ssm_conv_b', 'new_m_ssm_dt_bias', 'new_m_ssm_a_log', 'new_m_ssm_d', 'new_m_ssm_norm_g', 'new_m_swa_q_norm_g', 'new_m_swa_k_norm_g', 'new_m_swa_sink', 'new_m_mla_q_lat_g', 'new_m_mla_kv_lat_g', 'new_m_w_mla_uq', 'new_m_w_mla_ukv', 'new_m_mla_q_norm_g', 'new_m_mla_k_norm_g', 'new_m_w_p_ssm', 'new_m_w_p_swa', 'new_m_w_p_mla', 'new_m_w_out', 'new_m_w_ffn_in', 'new_m_w_ffn_out', 'new_v_c_ctx', 'new_v_w_mod', 'new_v_b_mod', 'new_v_norm1_g', 'new_v_norm2_g', 'new_v_w_in', 'new_v_ssm_conv_w', 'new_v_ssm_conv_b', 'new_v_ssm_dt_bias', 'new_v_ssm_a_log', 'new_v_ssm_d', 'new_v_ssm_norm_g', 'new_v_swa_q_norm_g', 'new_v_swa_k_norm_g', 'new_v_swa_sink', 'new_v_mla_q_lat_g', 'new_v_mla_kv_lat_g', 'new_v_w_mla_uq', 'new_v_w_mla_ukv', 'new_v_mla_q_norm_g', 'new_v_mla_k_norm_g', 'new_v_w_p_ssm', 'new_v_w_p_swa', 'new_v_w_p_mla', 'new_v_w_out', 'new_v_w_ffn_in', 'new_v_w_ffn_out']
TWIN_LEAF_KINDS = {'loss': 'loss', 'grad_x': 'grad_x', 'grad_c_ctx': 'grad_w', 'grad_w_mod': 'grad_w', 'grad_b_mod': 'grad_w', 'grad_norm1_g': 'grad_w', 'grad_norm2_g': 'grad_w', 'grad_w_in': 'grad_w', 'grad_ssm_conv_w': 'grad_w', 'grad_ssm_conv_b': 'grad_w', 'grad_ssm_dt_bias': 'grad_w', 'grad_ssm_a_log': 'grad_w', 'grad_ssm_d': 'grad_w', 'grad_ssm_norm_g': 'grad_w', 'grad_swa_q_norm_g': 'grad_w', 'grad_swa_k_norm_g': 'grad_w', 'grad_swa_sink': 'grad_w', 'grad_mla_q_lat_g': 'grad_w', 'grad_mla_kv_lat_g': 'grad_w', 'grad_w_mla_uq': 'grad_w', 'grad_w_mla_ukv': 'grad_w', 'grad_mla_q_norm_g': 'grad_w', 'grad_mla_k_norm_g': 'grad_w', 'grad_w_p_ssm': 'grad_w', 'grad_w_p_swa': 'grad_w', 'grad_w_p_mla': 'grad_w', 'grad_w_out': 'grad_w', 'grad_w_ffn_in': 'grad_w', 'grad_w_ffn_out': 'grad_w', 'delta_c_ctx': 'delta_w', 'delta_w_mod': 'delta_w', 'delta_b_mod': 'delta_w', 'delta_norm1_g': 'delta_w', 'delta_norm2_g': 'delta_w', 'delta_w_in': 'delta_w', 'delta_ssm_conv_w': 'delta_w', 'delta_ssm_conv_b': 'delta_w', 'delta_ssm_dt_bias': 'delta_w', 'delta_ssm_a_log': 'delta_w', 'delta_ssm_d': 'delta_w', 'delta_ssm_norm_g': 'delta_w', 'delta_swa_q_norm_g': 'delta_w', 'delta_swa_k_norm_g': 'delta_w', 'delta_swa_sink': 'delta_w', 'delta_mla_q_lat_g': 'delta_w', 'delta_mla_kv_lat_g': 'delta_w', 'delta_w_mla_uq': 'delta_w', 'delta_w_mla_ukv': 'delta_w', 'delta_mla_q_norm_g': 'delta_w', 'delta_mla_k_norm_g': 'delta_w', 'delta_w_p_ssm': 'delta_w', 'delta_w_p_swa': 'delta_w', 'delta_w_p_mla': 'delta_w', 'delta_w_out': 'delta_w', 'delta_w_ffn_in': 'delta_w', 'delta_w_ffn_out': 'delta_w', 'new_m_c_ctx': 'new_m', 'new_m_w_mod': 'new_m', 'new_m_b_mod': 'new_m', 'new_m_norm1_g': 'new_m', 'new_m_norm2_g': 'new_m', 'new_m_w_in': 'new_m', 'new_m_ssm_conv_w': 'new_m', 'new_m_ssm_conv_b': 'new_m', 'new_m_ssm_dt_bias': 'new_m', 'new_m_ssm_a_log': 'new_m', 'new_m_ssm_d': 'new_m', 'new_m_ssm_norm_g': 'new_m', 'new_m_swa_q_norm_g': 'new_m', 'new_m_swa_k_norm_g': 'new_m', 'new_m_swa_sink': 'new_m', 'new_m_mla_q_lat_g': 'new_m', 'new_m_mla_kv_lat_g': 'new_m', 'new_m_w_mla_uq': 'new_m', 'new_m_w_mla_ukv': 'new_m', 'new_m_mla_q_norm_g': 'new_m', 'new_m_mla_k_norm_g': 'new_m', 'new_m_w_p_ssm': 'new_m', 'new_m_w_p_swa': 'new_m', 'new_m_w_p_mla': 'new_m', 'new_m_w_out': 'new_m', 'new_m_w_ffn_in': 'new_m', 'new_m_w_ffn_out': 'new_m', 'new_v_c_ctx': 'new_v', 'new_v_w_mod': 'new_v', 'new_v_b_mod': 'new_v', 'new_v_norm1_g': 'new_v', 'new_v_norm2_g': 'new_v', 'new_v_w_in': 'new_v', 'new_v_ssm_conv_w': 'new_v', 'new_v_ssm_conv_b': 'new_v', 'new_v_ssm_dt_bias': 'new_v', 'new_v_ssm_a_log': 'new_v', 'new_v_ssm_d': 'new_v', 'new_v_ssm_norm_g': 'new_v', 'new_v_swa_q_norm_g': 'new_v', 'new_v_swa_k_norm_g': 'new_v', 'new_v_swa_sink': 'new_v', 'new_v_mla_q_lat_g': 'new_v', 'new_v_mla_kv_lat_g': 'new_v', 'new_v_w_mla_uq': 'new_v', 'new_v_w_mla_ukv': 'new_v', 'new_v_mla_q_norm_g': 'new_v', 'new_v_mla_k_norm_g': 'new_v', 'new_v_w_p_ssm': 'new_v', 'new_v_w_p_swa': 'new_v', 'new_v_w_p_mla': 'new_v', 'new_v_w_out': 'new_v', 'new_v_w_ffn_in': 'new_v', 'new_v_w_ffn_out': 'new_v'}


def _forward(args):
    return _fwd_reference(*[args[k] for k in FWD_PARAMS])


def _output_shape():
    def fwd():
        inp = _fwd_setup_inputs(0)
        return _fwd_reference(*[inp[k] for k in FWD_PARAMS])
    out = _jax.eval_shape(fwd)
    return out.shape, out.dtype

N_MICROBATCH = 1
ADAM_LR = 0.001
ADAM_B1 = 0.9
ADAM_B2 = 0.999
ADAM_EPS = 1e-08
ADAM_WD = 0.01
ADAM_STEP = 10
PER_EXAMPLE_BATCH_AXIS = {'x': 0, 'c': 0, 'ctx': 0, 'loss_target': 0}
SHARED_INPUTS = []
_WEIGHT_DTYPES = {'c_ctx': _jnp.float32, 'w_mod': _jnp.float32, 'b_mod': _jnp.float32, 'norm1_g': _jnp.float32, 'norm2_g': _jnp.float32, 'w_in': _jnp.float32, 'ssm_conv_w': _jnp.float32, 'ssm_conv_b': _jnp.float32, 'ssm_dt_bias': _jnp.float32, 'ssm_a_log': _jnp.float32, 'ssm_d': _jnp.float32, 'ssm_norm_g': _jnp.float32, 'swa_q_norm_g': _jnp.float32, 'swa_k_norm_g': _jnp.float32, 'swa_sink': _jnp.float32, 'mla_q_lat_g': _jnp.float32, 'mla_kv_lat_g': _jnp.float32, 'w_mla_uq': _jnp.float32, 'w_mla_ukv': _jnp.float32, 'mla_q_norm_g': _jnp.float32, 'mla_k_norm_g': _jnp.float32, 'w_p_ssm': _jnp.float32, 'w_p_swa': _jnp.float32, 'w_p_mla': _jnp.float32, 'w_out': _jnp.float32, 'w_ffn_in': _jnp.float32, 'w_ffn_out': _jnp.float32}
MOMENT_SCALE = {'c_ctx': 2.004470e-01, 'w_mod': 1.270525e+00, 'b_mod': 3.341235e+00, 'norm1_g': 1.341449e-01, 'norm2_g': 6.295871e+00, 'w_in': 8.911489e-02, 'ssm_conv_w': 1.201412e-01, 'ssm_conv_b': 2.947083e-01, 'ssm_dt_bias': 2.744419e-01, 'ssm_a_log': 7.221322e-01, 'ssm_d': 4.075934e-01, 'ssm_norm_g': 2.795180e+00, 'swa_q_norm_g': 7.445151e-02, 'swa_k_norm_g': 7.102320e-02, 'swa_sink': 1.787594e-02, 'mla_q_lat_g': 1.559806e-02, 'mla_kv_lat_g': 6.413609e-01, 'w_mla_uq': 7.151893e-03, 'w_mla_ukv': 1.032262e-01, 'mla_q_norm_g': 2.357422e-02, 'mla_k_norm_g': 2.373455e-02, 'w_p_ssm': 2.114275e-01, 'w_p_swa': 1.539257e-01, 'w_p_mla': 1.440291e-01, 'w_out': 2.336330e-01, 'w_ffn_in': 9.003715e-02, 'w_ffn_out': 1.167769e-01}


def _to_microbatches(a, axis):
    t = _jnp.moveaxis(a, axis, 0)
    t = t.reshape((N_MICROBATCH, t.shape[0] // N_MICROBATCH) + t.shape[1:])
    return _jnp.moveaxis(t, 1, axis + 1)


def setup_inputs(seed: int = 0) -> dict:
    inp = _fwd_setup_inputs(seed)
    key = _jax.random.fold_in(_jax.random.key(seed), 7919)
    shape, _ = _output_shape()
    out = dict(inp)
    out["loss_target"] = _jax.random.normal(_jax.random.fold_in(key, 0), shape, _jnp.float32)
    for i, name in enumerate(TWIN_WEIGHTS):
        w = inp[name].astype(_jnp.float32)
        if MOMENT_SCALE is None:
            s = _jnp.sqrt(_jnp.mean(_jnp.square(w)) + 1e-30)
        else:
            s = MOMENT_SCALE[name]
        km, kv = _jax.random.split(_jax.random.fold_in(key, i + 1))
        out[name] = w
        out["m_" + name] = s * _jax.random.normal(km, w.shape, _jnp.float32)
        out["v_" + name] = (s * s) * _jax.random.uniform(kv, w.shape, _jnp.float32, 0.5, 1.5)
    if N_MICROBATCH > 1:
        for name, axis in PER_EXAMPLE_BATCH_AXIS.items():
            out[name] = _to_microbatches(out[name], axis)
    return {'x': out['x'], 'c': out['c'], 'ctx': out['ctx'], 'c_ctx': out['c_ctx'], 'w_mod': out['w_mod'], 'b_mod': out['b_mod'], 'norm1_g': out['norm1_g'], 'norm2_g': out['norm2_g'], 'w_in': out['w_in'], 'ssm_conv_w': out['ssm_conv_w'], 'ssm_conv_b': out['ssm_conv_b'], 'ssm_dt_bias': out['ssm_dt_bias'], 'ssm_a_log': out['ssm_a_log'], 'ssm_d': out['ssm_d'], 'ssm_norm_g': out['ssm_norm_g'], 'swa_q_norm_g': out['swa_q_norm_g'], 'swa_k_norm_g': out['swa_k_norm_g'], 'swa_sink': out['swa_sink'], 'mla_q_lat_g': out['mla_q_lat_g'], 'mla_kv_lat_g': out['mla_kv_lat_g'], 'w_mla_uq': out['w_mla_uq'], 'w_mla_ukv': out['w_mla_ukv'], 'mla_q_norm_g': out['mla_q_norm_g'], 'mla_k_norm_g': out['mla_k_norm_g'], 'w_p_ssm': out['w_p_ssm'], 'w_p_swa': out['w_p_swa'], 'w_p_mla': out['w_p_mla'], 'w_out': out['w_out'], 'w_ffn_in': out['w_ffn_in'], 'w_ffn_out': out['w_ffn_out'], 'loss_target': out['loss_target'], 'm_c_ctx': out['m_c_ctx'], 'm_w_mod': out['m_w_mod'], 'm_b_mod': out['m_b_mod'], 'm_norm1_g': out['m_norm1_g'], 'm_norm2_g': out['m_norm2_g'], 'm_w_in': out['m_w_in'], 'm_ssm_conv_w': out['m_ssm_conv_w'], 'm_ssm_conv_b': out['m_ssm_conv_b'], 'm_ssm_dt_bias': out['m_ssm_dt_bias'], 'm_ssm_a_log': out['m_ssm_a_log'], 'm_ssm_d': out['m_ssm_d'], 'm_ssm_norm_g': out['m_ssm_norm_g'], 'm_swa_q_norm_g': out['m_swa_q_norm_g'], 'm_swa_k_norm_g': out['m_swa_k_norm_g'], 'm_swa_sink': out['m_swa_sink'], 'm_mla_q_lat_g': out['m_mla_q_lat_g'], 'm_mla_kv_lat_g': out['m_mla_kv_lat_g'], 'm_w_mla_uq': out['m_w_mla_uq'], 'm_w_mla_ukv': out['m_w_mla_ukv'], 'm_mla_q_norm_g': out['m_mla_q_norm_g'], 'm_mla_k_norm_g': out['m_mla_k_norm_g'], 'm_w_p_ssm': out['m_w_p_ssm'], 'm_w_p_swa': out['m_w_p_swa'], 'm_w_p_mla': out['m_w_p_mla'], 'm_w_out': out['m_w_out'], 'm_w_ffn_in': out['m_w_ffn_in'], 'm_w_ffn_out': out['m_w_ffn_out'], 'v_c_ctx': out['v_c_ctx'], 'v_w_mod': out['v_w_mod'], 'v_b_mod': out['v_b_mod'], 'v_norm1_g': out['v_norm1_g'], 'v_norm2_g': out['v_norm2_g'], 'v_w_in': out['v_w_in'], 'v_ssm_conv_w': out['v_ssm_conv_w'], 'v_ssm_conv_b': out['v_ssm_conv_b'], 'v_ssm_dt_bias': out['v_ssm_dt_bias'], 'v_ssm_a_log': out['v_ssm_a_log'], 'v_ssm_d': out['v_ssm_d'], 'v_ssm_norm_g': out['v_ssm_norm_g'], 'v_swa_q_norm_g': out['v_swa_q_norm_g'], 'v_swa_k_norm_g': out['v_swa_k_norm_g'], 'v_swa_sink': out['v_swa_sink'], 'v_mla_q_lat_g': out['v_mla_q_lat_g'], 'v_mla_kv_lat_g': out['v_mla_kv_lat_g'], 'v_w_mla_uq': out['v_w_mla_uq'], 'v_w_mla_ukv': out['v_w_mla_ukv'], 'v_mla_q_norm_g': out['v_mla_q_norm_g'], 'v_mla_k_norm_g': out['v_mla_k_norm_g'], 'v_w_p_ssm': out['v_w_p_ssm'], 'v_w_p_swa': out['v_w_p_swa'], 'v_w_p_mla': out['v_w_p_mla'], 'v_w_out': out['v_w_out'], 'v_w_ffn_in': out['v_w_ffn_in'], 'v_w_ffn_out': out['v_w_ffn_out']}


def _loss(weights, diff, rest, loss_target):
    with _jax.named_scope("forward"):
        args = {**rest, TWIN_DIFF_INPUT: diff, **{k: w.astype(_WEIGHT_DTYPES[k]) for k, w in weights.items()}}
        y = _forward(args)
    with _jax.named_scope("loss_head"):
        err = _jnp.square(y.astype(_jnp.float32) - loss_target)
        return 0.5 * _jnp.sum(_jnp.mean(err, axis=-1)) if err.ndim else 0.5 * err


def _adamw(w, g, m, v):
    m = ADAM_B1 * m + (1.0 - ADAM_B1) * g
    v = ADAM_B2 * v + (1.0 - ADAM_B2) * _jnp.square(g)
    m_hat = m / (1.0 - ADAM_B1 ** ADAM_STEP)
    v_hat = v / (1.0 - ADAM_B2 ** ADAM_STEP)
    delta = -ADAM_LR * (m_hat / (_jnp.sqrt(v_hat) + ADAM_EPS) + ADAM_WD * w)
    return delta, m, v


def reference(x, c, ctx, c_ctx, w_mod, b_mod, norm1_g, norm2_g, w_in, ssm_conv_w, ssm_conv_b, ssm_dt_bias, ssm_a_log, ssm_d, ssm_norm_g, swa_q_norm_g, swa_k_norm_g, swa_sink, mla_q_lat_g, mla_kv_lat_g, w_mla_uq, w_mla_ukv, mla_q_norm_g, mla_k_norm_g, w_p_ssm, w_p_swa, w_p_mla, w_out, w_ffn_in, w_ffn_out, loss_target, m_c_ctx, m_w_mod, m_b_mod, m_norm1_g, m_norm2_g, m_w_in, m_ssm_conv_w, m_ssm_conv_b, m_ssm_dt_bias, m_ssm_a_log, m_ssm_d, m_ssm_norm_g, m_swa_q_norm_g, m_swa_k_norm_g, m_swa_sink, m_mla_q_lat_g, m_mla_kv_lat_g, m_w_mla_uq, m_w_mla_ukv, m_mla_q_norm_g, m_mla_k_norm_g, m_w_p_ssm, m_w_p_swa, m_w_p_mla, m_w_out, m_w_ffn_in, m_w_ffn_out, v_c_ctx, v_w_mod, v_b_mod, v_norm1_g, v_norm2_g, v_w_in, v_ssm_conv_w, v_ssm_conv_b, v_ssm_dt_bias, v_ssm_a_log, v_ssm_d, v_ssm_norm_g, v_swa_q_norm_g, v_swa_k_norm_g, v_swa_sink, v_mla_q_lat_g, v_mla_kv_lat_g, v_w_mla_uq, v_w_mla_ukv, v_mla_q_norm_g, v_mla_k_norm_g, v_w_p_ssm, v_w_p_swa, v_w_p_mla, v_w_out, v_w_ffn_in, v_w_ffn_out):
    given = dict(x=x, c=c, ctx=ctx, c_ctx=c_ctx, w_mod=w_mod, b_mod=b_mod, norm1_g=norm1_g, norm2_g=norm2_g, w_in=w_in, ssm_conv_w=ssm_conv_w, ssm_conv_b=ssm_conv_b, ssm_dt_bias=ssm_dt_bias, ssm_a_log=ssm_a_log, ssm_d=ssm_d, ssm_norm_g=ssm_norm_g, swa_q_norm_g=swa_q_norm_g, swa_k_norm_g=swa_k_norm_g, swa_sink=swa_sink, mla_q_lat_g=mla_q_lat_g, mla_kv_lat_g=mla_kv_lat_g, w_mla_uq=w_mla_uq, w_mla_ukv=w_mla_ukv, mla_q_norm_g=mla_q_norm_g, mla_k_norm_g=mla_k_norm_g, w_p_ssm=w_p_ssm, w_p_swa=w_p_swa, w_p_mla=w_p_mla, w_out=w_out, w_ffn_in=w_ffn_in, w_ffn_out=w_ffn_out, loss_target=loss_target, m_c_ctx=m_c_ctx, m_w_mod=m_w_mod, m_b_mod=m_b_mod, m_norm1_g=m_norm1_g, m_norm2_g=m_norm2_g, m_w_in=m_w_in, m_ssm_conv_w=m_ssm_conv_w, m_ssm_conv_b=m_ssm_conv_b, m_ssm_dt_bias=m_ssm_dt_bias, m_ssm_a_log=m_ssm_a_log, m_ssm_d=m_ssm_d, m_ssm_norm_g=m_ssm_norm_g, m_swa_q_norm_g=m_swa_q_norm_g, m_swa_k_norm_g=m_swa_k_norm_g, m_swa_sink=m_swa_sink, m_mla_q_lat_g=m_mla_q_lat_g, m_mla_kv_lat_g=m_mla_kv_lat_g, m_w_mla_uq=m_w_mla_uq, m_w_mla_ukv=m_w_mla_ukv, m_mla_q_norm_g=m_mla_q_norm_g, m_mla_k_norm_g=m_mla_k_norm_g, m_w_p_ssm=m_w_p_ssm, m_w_p_swa=m_w_p_swa, m_w_p_mla=m_w_p_mla, m_w_out=m_w_out, m_w_ffn_in=m_w_ffn_in, m_w_ffn_out=m_w_ffn_out, v_c_ctx=v_c_ctx, v_w_mod=v_w_mod, v_b_mod=v_b_mod, v_norm1_g=v_norm1_g, v_norm2_g=v_norm2_g, v_w_in=v_w_in, v_ssm_conv_w=v_ssm_conv_w, v_ssm_conv_b=v_ssm_conv_b, v_ssm_dt_bias=v_ssm_dt_bias, v_ssm_a_log=v_ssm_a_log, v_ssm_d=v_ssm_d, v_ssm_norm_g=v_ssm_norm_g, v_swa_q_norm_g=v_swa_q_norm_g, v_swa_k_norm_g=v_swa_k_norm_g, v_swa_sink=v_swa_sink, v_mla_q_lat_g=v_mla_q_lat_g, v_mla_kv_lat_g=v_mla_kv_lat_g, v_w_mla_uq=v_w_mla_uq, v_w_mla_ukv=v_w_mla_ukv, v_mla_q_norm_g=v_mla_q_norm_g, v_mla_k_norm_g=v_mla_k_norm_g, v_w_p_ssm=v_w_p_ssm, v_w_p_swa=v_w_p_swa, v_w_p_mla=v_w_p_mla, v_w_out=v_w_out, v_w_ffn_in=v_w_ffn_in, v_w_ffn_out=v_w_ffn_out)
    weights = {n: given[n] for n in TWIN_WEIGHTS}
    shared = {n: given[n] for n in SHARED_INPUTS}
    per_example = {n: given[n] for n in ['x', 'c', 'ctx']}
    grad_fn = _jax.value_and_grad(_loss, argnums=(0, 1))

    def one_microbatch(ex, loss_target):
        ex = dict(ex)
        diff = ex.pop(TWIN_DIFF_INPUT)
        return grad_fn(weights, diff, {**shared, **ex}, loss_target)

    if N_MICROBATCH == 1:
        loss, (grad_w, grad_x) = one_microbatch(per_example, given["loss_target"])
    else:
        def body(carry, xs):
            loss_sum, grad_sum = carry
            l_k, (gw_k, gx_k) = one_microbatch(xs[0], xs[1])
            with _jax.named_scope("update"):
                return (loss_sum + l_k, _jax.tree.map(_jnp.add, grad_sum, gw_k)), gx_k

        init = (_jnp.zeros((), _jnp.float32), _jax.tree.map(_jnp.zeros_like, weights))
        (loss, grad_w), grad_x = _jax.lax.scan(body, init, (per_example, given["loss_target"]))
    with _jax.named_scope("update"):
        delta_w, new_m, new_v = {}, {}, {}
        for n in TWIN_WEIGHTS:
            delta_w[n], new_m[n], new_v[n] = _adamw(weights[n], grad_w[n], given["m_" + n], given["v_" + n])
    return (loss, grad_x, *[grad_w[n] for n in TWIN_WEIGHTS], *[delta_w[n] for n in TWIN_WEIGHTS],
            *[new_m[n] for n in TWIN_WEIGHTS], *[new_v[n] for n in TWIN_WEIGHTS])
```

```python
import functools
import math

import numpy as np
import jax
import jax.numpy as jnp
from jax import lax
from jax.experimental import pallas as pl
from jax.experimental.pallas import tpu as pltpu

F32 = jnp.float32
BF16 = jnp.bfloat16

N_DEV = 8
V7X_VMEM_BYTES = 64 * 1024 * 1024
VMEM_LIMIT_BYTES = V7X_VMEM_BYTES - 8 * 1024 * 1024
LANES = 128

EPS = 1e-6
ROPE_BASE = 10000.0
GRID_W = 64
SSM_HEADS, SSM_HEAD_DIM, SSM_GROUPS, SSM_STATE, SSM_CONV, SSM_CHUNK = 16, 64, 2, 128, 5, 128
SSM_INNER = SSM_HEADS * SSM_HEAD_DIM
SSM_CONV_DIM = SSM_INNER + 2 * SSM_GROUPS * SSM_STATE
SWA_Q_HEADS, SWA_KV_HEADS, SWA_HEAD_DIM, SWA_WINDOW = 8, 2, 128, 128
MLA_HEADS, MLA_Q_RANK, MLA_KV_RANK, MLA_NOPE, MLA_ROPE, MLA_V = 8, 384, 256, 128, 64, 128
MLA_QK = MLA_NOPE + MLA_ROPE
MLA_QK_PAD = 2 * LANES
ADAM_LR, ADAM_B1, ADAM_B2, ADAM_EPS, ADAM_WD, ADAM_STEP = 0.001, 0.9, 0.999, 1e-08, 0.01, 10

ROW_TILE = 256


def _cparams(sem, **kw):
    return pltpu.CompilerParams(dimension_semantics=sem, vmem_limit_bytes=VMEM_LIMIT_BYTES, **kw)


def _pick(dim, prefs):
    for p in prefs:
        if dim % p == 0:
            return p
    return dim


def matmul(a, b, mode, name, out_dtype=F32, add=None):
    if mode == "nn":
        (M, K), (K2, N) = a.shape, b.shape
    elif mode == "nt":
        (M, K), (N, K2) = a.shape, b.shape
    else:
        (K, M), (K2, N) = a.shape, b.shape
    assert K == K2, (name, a.shape, b.shape)
    tm = _pick(M, (768, 512, 384, 256, 128))
    tn = _pick(N, (512, 384, 256, 128))
    tk = K if K <= 1536 else _pick(K, (1408, 768, 1024, 512, 256))
    nk = K // tk
    dims = {"nn": (((1,), (0,)), ((), ())), "nt": (((1,), (1,)), ((), ())), "tn": (((0,), (0,)), ((), ()))}[mode]
    a_spec = pl.BlockSpec((tk, tm), lambda i, j, k: (k, i)) if mode == "tn" else pl.BlockSpec((tm, tk), lambda i, j, k: (i, k))
    b_spec = pl.BlockSpec((tn, tk), lambda i, j, k: (j, k)) if mode == "nt" else pl.BlockSpec((tk, tn), lambda i, j, k: (k, j))
    o_spec = pl.BlockSpec((tm, tn), lambda i, j, k: (i, j))
    has_add = add is not None

    def body(*refs):
        if has_add:
            a_ref, b_ref, c_ref, o_ref, acc_ref = refs
        else:
            a_ref, b_ref, o_ref, acc_ref = refs
        k = pl.program_id(2)
        part = lax.dot_general(a_ref[...].astype(BF16), b_ref[...].astype(BF16), dims, preferred_element_type=F32)

        @pl.when(k == 0)
        def _():
            acc_ref[...] = part + c_ref[...] if has_add else part

        @pl.when(k > 0)
        def _():
            acc_ref[...] += part

        @pl.when(k == nk - 1)
        def _():
            o_ref[...] = acc_ref[...].astype(o_ref.dtype)

    ins = [a, b] + ([add] if has_add else [])
    in_specs = [a_spec, b_spec] + ([o_spec] if has_add else [])
    return pl.pallas_call(
        body, name=name, grid=(M // tm, N // tn, nk), in_specs=in_specs, out_specs=o_spec,
        out_shape=jax.ShapeDtypeStruct((M, N), out_dtype),
        scratch_shapes=[pltpu.VMEM((tm, tn), F32)],
        input_output_aliases=({2: 0} if has_add else {}),
        compiler_params=_cparams(("parallel", "parallel", "arbitrary")),
    )(*ins)


def _row_specs(descs, arrays, tm, nct):
    specs = []
    for d, arr in zip(descs, arrays):
        if d[0] == "row":
            _, w, stride, off, _ = d
            specs.append(pl.BlockSpec((tm, w), lambda i, h, off=off, stride=stride: (i, off + h * stride)))
        elif d[0] == "par":
            specs.append(pl.BlockSpec(arr.shape, lambda i, h, nd=arr.ndim: (0,) * nd))
        else:
            specs.append(pl.BlockSpec((1,) + arr.shape[1:], lambda i, h, nd=arr.ndim: (jnp.where(i >= nct, 1, 0),) + (0,) * (nd - 1)))
    return specs


def _load(d, ref):
    return ref[0] if d[0] == "grp" else ref[...]


def rowop_fwd(name, fn, descs, arrays, outs, T, n_ctx, heads=1, tm=ROW_TILE):
    nct = n_ctx // tm
    n_in = len(descs)

    def body(*refs):
        vals = [_load(d, r) for d, r in zip(descs, refs[:n_in])]
        res = fn(*vals)
        for o_ref, r in zip(refs[n_in:], res):
            o_ref[...] = r.astype(o_ref.dtype)

    out_specs = [pl.BlockSpec((tm, w), (lambda i, h: (i, h)) if ph else (lambda i, h: (i, 0))) for (w, ph, _) in outs]
    out_shape = [jax.ShapeDtypeStruct((T, w * (heads if ph else 1)), dt) for (w, ph, dt) in outs]
    return pl.pallas_call(
        body, name=name, grid=(T // tm, heads), in_specs=_row_specs(descs, arrays, tm, nct), out_specs=out_specs,
        out_shape=out_shape, compiler_params=_cparams(("parallel", "arbitrary")),
    )(*arrays)


def rowop_bwd(name, fn, descs, arrays, outs, cts, T, n_ctx, heads=1, tm=ROW_TILE, add=None):
    nct = n_ctx // tm
    n_in, n_ct = len(descs), len(cts)
    add = add or {}
    diff_idx = [k for k, d in enumerate(descs) if d[-1]]
    add_idx = [k for k in diff_idx if k in add]

    def body(*refs):
        in_refs, ct_refs = refs[:n_in], refs[n_in:n_in + n_ct]
        add_refs = dict(zip(add_idx, refs[n_in + n_ct:n_in + n_ct + len(add_idx)]))
        g_refs = refs[n_in + n_ct + len(add_idx):]
        i, h = pl.program_id(0), pl.program_id(1)
        vals = [_load(d, r) for d, r in zip(descs, in_refs)]

        def f(*dvals):
            full = list(vals)
            for k, v in zip(diff_idx, dvals):
                full[k] = v
            return tuple(fn(*full))

        _, vjp = jax.vjp(f, *[vals[k] for k in diff_idx])
        grads = vjp(tuple(c[...] for c in ct_refs))
        first_all = jnp.logical_and(i == 0, h == 0)
        first_grp = jnp.logical_and(jnp.logical_or(i == 0, i == nct), h == 0)
        for k, g_ref, g in zip(diff_idx, g_refs, grads):
            d = descs[k]
            if d[0] == "row":
                if k in add_refs:
                    g = g + add_refs[k][...]
                if d[2] or heads == 1:
                    g_ref[...] = g.astype(g_ref.dtype)
                else:
                    _accumulate(g_ref, g, h == 0)
            elif d[0] == "par":
                _accumulate(g_ref, g, first_all)
            else:
                _accumulate(g_ref, g[None], first_grp)

    in_specs = _row_specs(descs, arrays, tm, nct)
    ct_specs = [pl.BlockSpec((tm, w), (lambda i, h: (i, h)) if ph else (lambda i, h: (i, 0))) for (w, ph, _) in outs]
    all_specs = _row_specs(descs, arrays, tm, nct)
    g_specs, g_shape = [], []
    for k in diff_idx:
        d = descs[k]
        if d[0] == "row":
            g_specs.append(pl.BlockSpec((tm, d[1]), (lambda i, h: (i, h)) if d[2] else (lambda i, h: (i, 0))))
            g_shape.append(jax.ShapeDtypeStruct((T, d[1] * (heads if d[2] else 1)), F32))
        else:
            g_specs.append(all_specs[k])
            g_shape.append(jax.ShapeDtypeStruct(arrays[k].shape, F32))
    add_specs = [g_specs[diff_idx.index(k)] for k in add_idx]
    return pl.pallas_call(
        body, name=name, grid=(T // tm, heads), in_specs=in_specs + ct_specs + add_specs, out_specs=g_specs,
        out_shape=g_shape, compiler_params=_cparams(("arbitrary", "arbitrary")),
    )(*arrays, *cts, *[add[k] for k in add_idx])


def _accumulate(ref, val, first):
    @pl.when(first)
    def _():
        ref[...] = val.astype(ref.dtype)

    @pl.when(jnp.logical_not(first))
    def _():
        ref[...] += val.astype(ref.dtype)


def _rms(x, count=None):
    n = x.shape[-1] if count is None else count
    return x * lax.rsqrt(jnp.sum(x * x, axis=-1, keepdims=True) * (1.0 / n) + EPS)


def _swap_halves(x, nf):
    w = x.shape[-1]
    lane = lax.broadcasted_iota(jnp.int32, x.shape, x.ndim - 1)
    return jnp.where((lane % (2 * nf)) < nf, pltpu.roll(x, w - nf, x.ndim - 1), pltpu.roll(x, nf, x.ndim - 1))


def _make_rope(nf):
    @jax.custom_vjp
    def rope(x, c, s):
        return x * c + _swap_halves(x, nf) * s

    def fwd(x, c, s):
        return rope(x, c, s), (c, s)

    def bwd(res, g):
        c, s = res
        return g * c + _swap_halves(g * s, nf), jnp.zeros_like(c), jnp.zeros_like(s)

    rope.defvjp(fwd, bwd)
    return rope


_rope_swa = _make_rope(SWA_HEAD_DIM // 4)
_rope_mla = _make_rope(MLA_ROPE // 4)


@jax.custom_vjp
def _softplus(x):
    e = jnp.exp(-jnp.abs(x))
    u = 1.0 + e
    log1p_e = jnp.where(u == 1.0, e, jnp.log(u) * e / jnp.where(u == 1.0, 1.0, u - 1.0))
    return jnp.maximum(x, 0.0) + log1p_e


_softplus.defvjp(lambda x: (_softplus(x), x), lambda x, g: (g * jax.nn.sigmoid(x),))


def fn_norm_mod(x, g, shift, scale):
    return (_rms(x) * g * (1.0 + scale) + shift,)


def fn_rms(x, g):
    return (_rms(x) * g,)


def fn_resid(x, a, gate):
    return (x + gate * a,)


def fn_softplus(dt, bias):
    return (_softplus(dt + bias),)


def fn_ssd_out(yf, yb, xs, z, d_lane, g):
    y = yf + yb + d_lane * xs
    return (_rms(y * (z * jax.nn.sigmoid(z))) * g,)


def fn_swa_q(q, g, c, s):
    return (_rope_swa(_rms(q) * g, c, s),)


def fn_swa_kv(k, v, g, c, s):
    return (_rope_swa(_rms(k) * g, c, s), v)


def fn_mla_q(qn, qr, gn, gr, c, s):
    return (jnp.concatenate([_rms(qn) * gn, _rope_mla(_rms(qr, MLA_ROPE) * gr, c, s)], axis=-1),)


def fn_mla_kv(kn, v, kr, gn, gr, c, s):
    return (jnp.concatenate([_rms(kn) * gn, _rope_mla(_rms(kr, MLA_ROPE) * gr, c, s)], axis=-1), v)


def fn_merge(g1, g2, g3, p1, p2, p3):
    return (jax.nn.sigmoid(g1) * p1 + jax.nn.sigmoid(g2) * p2 + jax.nn.sigmoid(g3) * p3,)


def fn_swiglu(g, u):
    return (g * jax.nn.sigmoid(g) * u,)


ATTN_TILE = 256
NT_DIMS = (((1,), (1,)), ((), ()))


class AttnCfg:
    def __init__(self, hq, group, dq, dv, scale, window, has_sink, L, T, chunk):
        self.hq, self.group, self.dq, self.dv, self.scale = hq, group, dq, dv, scale
        self.window, self.has_sink, self.L, self.T, self.chunk = window, has_sink, L, T, chunk
        assert L % ATTN_TILE == 0 and (T - L) % ATTN_TILE == 0 and L % chunk == 0 and (T - L) % chunk == 0
        assert window is None or (window % chunk == 0 and ATTN_TILE % chunk == 0)


def _kv_ranges(cfg, q0, tq):
    c = cfg.chunk
    is_ctx = q0 >= cfg.L
    if cfg.window is None:
        lat_lo = 0
        lat_n = jnp.where(is_ctx, 0, cfg.L // c)
    else:
        lo = jnp.maximum(q0 - cfg.window, 0)
        hi = jnp.minimum(q0 + tq + cfg.window, cfg.L)
        lat_lo = lo
        lat_n = jnp.where(is_ctx, 0, (hi - lo) // c)
    return lat_lo, lat_n, cfg.L, (cfg.T - cfg.L) // c


def _window_mask(cfg, rows_q, rows_k):
    return jnp.abs(rows_k - rows_q) <= cfg.window


def flash_fwd(name, cfg, q, k, v, sink):
    T, tq, c = cfg.T, ATTN_TILE, cfg.chunk
    hq, g, dq, dv = cfg.hq, cfg.group, cfg.dq, cfg.dv

    def body(*refs):
        if cfg.has_sink:
            q_ref, k_ref, v_ref, sink_ref, o_ref, lse_ref = refs
        else:
            q_ref, k_ref, v_ref, o_ref, lse_ref = refs
        q0 = pl.program_id(1) * tq
        qq = q_ref[...]
        lat_lo, lat_n, ctx_lo, ctx_n = _kv_ranges(cfg, q0, tq)
        rows_q = q0 + lax.broadcasted_iota(jnp.int32, (tq, 1), 0)

        def make_step(base, masked):
            def step(t, carry):
                m, l, acc = carry
                ks = pl.multiple_of(base + t * c, c)
                s = lax.dot_general(qq, k_ref[pl.ds(ks, c), :], NT_DIMS, preferred_element_type=F32) * cfg.scale
                if masked:
                    rows_k = ks + lax.broadcasted_iota(jnp.int32, (1, c), 1)
                    s = jnp.where(_window_mask(cfg, rows_q, rows_k), s, -jnp.inf)
                m_new = jnp.maximum(m, jnp.max(s, axis=-1, keepdims=True))
                alpha = jnp.exp(m - m_new)
                p = jnp.exp(s - m_new)
                l = alpha * l + jnp.sum(p, axis=-1, keepdims=True)
                acc = alpha * acc + jnp.dot(p.astype(BF16), v_ref[pl.ds(ks, c), :], preferred_element_type=F32)
                return m_new, l, acc
            return step

        if cfg.has_sink:
            m0 = jnp.zeros((tq, 1), F32) + sink_ref[0]
            l0 = jnp.ones((tq, 1), F32)
        else:
            m0 = jnp.full((tq, 1), -jnp.inf, F32)
            l0 = jnp.zeros((tq, 1), F32)
        carry = (m0, l0, jnp.zeros((tq, dv), F32))
        carry = lax.fori_loop(0, ctx_n, make_step(ctx_lo, False), carry)
        m, l, acc = lax.fori_loop(0, lat_n, make_step(lat_lo, cfg.window is not None), carry)
        o_ref[...] = acc / l
        lse_ref[0] = m + jnp.log(l)

    in_specs = [pl.BlockSpec((tq, dq), lambda h, i: (i, h)),
                pl.BlockSpec((T, dq), lambda h, i: (0, h // g)),
                pl.BlockSpec((T, dv), lambda h, i: (0, h // g))]
    ins = [q, k, v]
    if cfg.has_sink:
        in_specs.append(pl.BlockSpec((1, 1, 1), lambda h, i: (h, 0, 0)))
        ins.append(sink)
    return pl.pallas_call(
        body, name=name, grid=(hq, T // tq), in_specs=in_specs,
        out_specs=[pl.BlockSpec((tq, dv), lambda h, i: (i, h)), pl.BlockSpec((1, tq, 1), lambda h, i: (h, i, 0))],
        out_shape=[jax.ShapeDtypeStruct((T, hq * dv), F32), jax.ShapeDtypeStruct((hq, T, 1), F32)],
        compiler_params=_cparams(("parallel", "parallel")),
    )(*ins)


def attn_delta(name, cfg, o, do, lse, sink):
    T, tm, hq, dv = cfg.T, ATTN_TILE, cfg.hq, cfg.dv

    def body(*refs):
        if cfg.has_sink:
            o_ref, do_ref, lse_ref, sink_ref, delta_ref, dob_ref, dsink_ref = refs
        else:
            o_ref, do_ref, delta_ref, dob_ref = refs
        d = do_ref[...]
        delta = jnp.sum(d * o_ref[...], axis=-1, keepdims=True)
        delta_ref[0] = delta
        dob_ref[...] = d.astype(BF16)
        if cfg.has_sink:
            part = -jnp.sum(jnp.exp(sink_ref[0] - lse_ref[0]) * delta, axis=0, keepdims=True)
            _accumulate(dsink_ref, part[None], pl.program_id(1) == 0)

    head_tile = pl.BlockSpec((tm, dv), lambda h, i: (i, h))
    col = pl.BlockSpec((1, tm, 1), lambda h, i: (h, i, 0))
    one = pl.BlockSpec((1, 1, 1), lambda h, i: (h, 0, 0))
    in_specs, ins = [head_tile, head_tile], [o, do]
    out_specs = [col, head_tile]
    out_shape = [jax.ShapeDtypeStruct((hq, T, 1), F32), jax.ShapeDtypeStruct((T, hq * dv), BF16)]
    if cfg.has_sink:
        in_specs += [col, one]
        ins += [lse, sink]
        out_specs.append(one)
        out_shape.append(jax.ShapeDtypeStruct((hq, 1, 1), F32))
    return pl.pallas_call(body, name=name, grid=(hq, T // tm), in_specs=in_specs, out_specs=out_specs, out_shape=out_shape,
                          compiler_params=_cparams(("arbitrary", "arbitrary")))(*ins)


def flash_dq(name, cfg, q, k, v, dob, lse, delta):
    T, tq, c = cfg.T, ATTN_TILE, cfg.chunk
    hq, g, dq, dv = cfg.hq, cfg.group, cfg.dq, cfg.dv

    def body(q_ref, k_ref, v_ref, do_ref, lse_ref, delta_ref, dq_ref):
        q0 = pl.program_id(1) * tq
        qq, dd = q_ref[...], do_ref[...]
        lse_c, delta_c = lse_ref[0], delta_ref[0]
        lat_lo, lat_n, ctx_lo, ctx_n = _kv_ranges(cfg, q0, tq)
        rows_q = q0 + lax.broadcasted_iota(jnp.int32, (tq, 1), 0)

        def make_step(base, masked):
            def step(t, acc):
                ks = pl.multiple_of(base + t * c, c)
                kk = k_ref[pl.ds(ks, c), :]
                s = lax.dot_general(qq, kk, NT_DIMS, preferred_element_type=F32) * cfg.scale
                p = jnp.exp(s - lse_c)
                if masked:
                    rows_k = ks + lax.broadcasted_iota(jnp.int32, (1, c), 1)
                    p = jnp.where(_window_mask(cfg, rows_q, rows_k), p, 0.0)
                dp = lax.dot_general(dd, v_ref[pl.ds(ks, c), :], NT_DIMS, preferred_element_type=F32)
                ds = p * (dp - delta_c) * cfg.scale
                return acc + jnp.dot(ds.astype(BF16), kk, preferred_element_type=F32)
            return step

        acc = lax.fori_loop(0, ctx_n, make_step(ctx_lo, False), jnp.zeros((tq, dq), F32))
        dq_ref[...] = lax.fori_loop(0, lat_n, make_step(lat_lo, cfg.window is not None), acc)

    col = pl.BlockSpec((1, tq, 1), lambda h, i: (h, i, 0))
    return pl.pallas_call(
        body, name=name, grid=(hq, T // tq),
        in_specs=[pl.BlockSpec((tq, dq), lambda h, i: (i, h)),
                  pl.BlockSpec((T, dq), lambda h, i: (0, h // g)),
                  pl.BlockSpec((T, dv), lambda h, i: (0, h // g)),
                  pl.BlockSpec((tq, dv), lambda h, i: (i, h)), col, col],
        out_specs=pl.BlockSpec((tq, dq), lambda h, i: (i, h)),
        out_shape=jax.ShapeDtypeStruct((T, hq * dq), F32),
        compiler_params=_cparams(("parallel", "parallel")),
    )(q, k, v, dob, lse, delta)


def flash_dkv(name, cfg, q, k, v, dob, lse_t, delta_t):
    T, tk, c = cfg.T, ATTN_TILE, cfg.chunk
    hq, g, dq, dv = cfg.hq, cfg.group, cfg.dq, cfg.dv
    hk = hq // g

    def body(k_ref, v_ref, q_ref, do_ref, lse_ref, delta_ref, dk_ref, dv_ref):
        k0 = pl.program_id(1) * tk
        kk, vv = k_ref[...], v_ref[...]
        is_ctx = k0 >= cfg.L
        rows_k = k0 + lax.broadcasted_iota(jnp.int32, (tk, 1), 0)
        if cfg.window is None:
            lat_lo = 0
            lat_n = cfg.L // c
        else:
            lo = jnp.maximum(k0 - cfg.window, 0)
            hi = jnp.minimum(k0 + tk + cfg.window, cfg.L)
            lat_lo = jnp.where(is_ctx, 0, lo)
            lat_n = jnp.where(is_ctx, cfg.L, hi - lo) // c
        ctx_n = jnp.where(is_ctx, (cfg.T - cfg.L) // c, 0)

        def make_step(base, masked, gi):
            def step(t, carry):
                dk_acc, dv_acc = carry
                qs = pl.multiple_of(base + t * c, c)
                ci = qs // c
                qc = q_ref[pl.ds(qs, c), gi * dq:(gi + 1) * dq]
                dc = do_ref[pl.ds(qs, c), gi * dv:(gi + 1) * dv]
                st = lax.dot_general(kk, qc, NT_DIMS, preferred_element_type=F32) * cfg.scale
                pt = jnp.exp(st - lse_ref[gi, ci])
                if masked:
                    rows_q = qs + lax.broadcasted_iota(jnp.int32, (1, c), 1)
                    pt = jnp.where(jnp.logical_or(is_ctx, _window_mask(cfg, rows_q, rows_k)), pt, 0.0)
                dv_acc = dv_acc + jnp.dot(pt.astype(BF16), dc, preferred_element_type=F32)
                dpt = lax.dot_general(vv, dc, NT_DIMS, preferred_element_type=F32)
                dst = pt * (dpt - delta_ref[gi, ci]) * cfg.scale
                dk_acc = dk_acc + jnp.dot(dst.astype(BF16), qc, preferred_element_type=F32)
                return dk_acc, dv_acc
            return step

        carry = (jnp.zeros((tk, dq), F32), jnp.zeros((tk, dv), F32))
        for gi in range(g):
            carry = lax.fori_loop(0, lat_n, make_step(lat_lo, cfg.window is not None, gi), carry)
            carry = lax.fori_loop(0, ctx_n, make_step(cfg.L, False, gi), carry)
        dk_ref[...], dv_ref[...] = carry

    nch = T // c
    return pl.pallas_call(
        body, name=name, grid=(hk, T // tk),
        in_specs=[pl.BlockSpec((tk, dq), lambda h, j: (j, h)),
                  pl.BlockSpec((tk, dv), lambda h, j: (j, h)),
                  pl.BlockSpec((T, g * dq), lambda h, j: (0, h)),
                  pl.BlockSpec((T, g * dv), lambda h, j: (0, h)),
                  pl.BlockSpec((g, nch, 1, c), lambda h, j: (h, 0, 0, 0)),
                  pl.BlockSpec((g, nch, 1, c), lambda h, j: (h, 0, 0, 0))],
        out_specs=[pl.BlockSpec((tk, dq), lambda h, j: (j, h)), pl.BlockSpec((tk, dv), lambda h, j: (j, h))],
        out_shape=[jax.ShapeDtypeStruct((T, hk * dq), F32), jax.ShapeDtypeStruct((T, hk * dv), F32)],
        compiler_params=_cparams(("parallel", "parallel")),
    )(k, v, q, dob, lse_t, delta_t)


HALO = 8


def _conv_specs(tm, C, T):
    nb = tm // HALO
    last = T // HALO - 1
    return [pl.BlockSpec((HALO, C), lambda i: (jnp.maximum(i * nb - 1, 0), 0)),
            pl.BlockSpec((tm, C), lambda i: (i, 0)),
            pl.BlockSpec((HALO, C), lambda i: (jnp.minimum((i + 1) * nb, last), 0))]


def _extended(prev_ref, cur_ref, next_ref, i, tm, L, T):
    r0 = i * tm
    keep_prev = jnp.logical_and(r0 != 0, r0 != L).astype(F32)
    keep_next = jnp.logical_and(r0 + tm != L, r0 + tm != T).astype(F32)
    return jnp.concatenate([prev_ref[...] * keep_prev, cur_ref[...], next_ref[...] * keep_next], axis=0)


def _shift_rows(xe, d):
    n = xe.shape[0]
    return xe if d == 0 else pltpu.roll(xe, (-d) % n, 0)


def _conv_pre(xe, w_ref, b_ref):
    acc = b_ref[...] + w_ref[SSM_CONV // 2:SSM_CONV // 2 + 1, :] * xe
    for k in range(SSM_CONV):
        if k != SSM_CONV // 2:
            acc = acc + w_ref[k:k + 1, :] * _shift_rows(xe, k - SSM_CONV // 2)
    return acc


def conv_fwd(name, x, w, b, L, tm=ROW_TILE):
    T, C = x.shape

    def body(xp, xc, xn, w_ref, b_ref, o_ref):
        xe = _extended(xp, xc, xn, pl.program_id(0), tm, L, T)
        pre = _conv_pre(xe, w_ref, b_ref)[HALO:HALO + tm]
        o_ref[...] = pre * jax.nn.sigmoid(pre)

    full = lambda a: pl.BlockSpec(a.shape, lambda i: (0, 0))
    return pl.pallas_call(body, name=name, grid=(T // tm,), in_specs=_conv_specs(tm, C, T) + [full(w), full(b)],
                          out_specs=pl.BlockSpec((tm, C), lambda i: (i, 0)), out_shape=jax.ShapeDtypeStruct((T, C), F32),
                          compiler_params=_cparams(("parallel",)))(x, x, x, w, b)


def conv_bwd(name, x, w, b, gu, L, tm=ROW_TILE):
    T, C = x.shape

    def body(xp, xc, xn, gp, gc, gn, w_ref, b_ref, dx_ref, dw_ref, db_ref):
        i = pl.program_id(0)
        xe = _extended(xp, xc, xn, i, tm, L, T)
        ge = _extended(gp, gc, gn, i, tm, L, T)
        pre = _conv_pre(xe, w_ref, b_ref)
        sg = jax.nn.sigmoid(pre)
        gpre = ge * (sg * (1.0 + pre * (1.0 - sg)))
        half = SSM_CONV // 2
        dx = jnp.zeros((tm, C), F32)
        rows = []
        for k in range(SSM_CONV):
            dx = dx + w_ref[k:k + 1, :] * _shift_rows(gpre, half - k)[HALO:HALO + tm]
            rows.append(jnp.sum(gpre[HALO:HALO + tm] * _shift_rows(xe, k - half)[HALO:HALO + tm], axis=0, keepdims=True))
        dx_ref[...] = dx
        rows += [jnp.zeros((1, C), F32)] * (8 - SSM_CONV)
        _accumulate(dw_ref, jnp.concatenate(rows, axis=0), i == 0)
        _accumulate(db_ref, jnp.sum(gpre[HALO:HALO + tm], axis=0, keepdims=True), i == 0)

    full = lambda a: pl.BlockSpec(a.shape, lambda i: (0, 0))
    return pl.pallas_call(
        body, name=name, grid=(T // tm,), in_specs=_conv_specs(tm, C, T) * 2 + [full(w), full(b)],
        out_specs=[pl.BlockSpec((tm, C), lambda i: (i, 0)), pl.BlockSpec((8, C), lambda i: (0, 0)), pl.BlockSpec((1, C), lambda i: (0, 0))],
        out_shape=[jax.ShapeDtypeStruct((T, C), F32), jax.ShapeDtypeStruct((8, C), F32), jax.ShapeDtypeStruct((1, C), F32)],
        compiler_params=_cparams(("arbitrary",)))(x, x, x, gu, gu, gu, w, b)


SSM_PAIRS = SSM_HEADS // 2
TN_DIMS = (((0,), (0,)), ((), ()))
HIGHEST = lax.Precision.HIGHEST


def _ssd_chunk(direction, xps, bs, cs, dt_col, dt_row, alog_row, alog_col, hps):
    Q = SSM_CHUNK
    da_col = dt_col * (-jnp.exp(alog_row))
    da_row = dt_row * (-jnp.exp(alog_col))
    ii = lax.broadcasted_iota(jnp.int32, (Q, Q), 0)
    jj = lax.broadcasted_iota(jnp.int32, (Q, Q), 1)
    tri = (ii >= jj) if direction == 0 else (ii <= jj)
    trif = tri.astype(F32)
    acs_col = jnp.dot(trif, da_col, precision=HIGHEST, preferred_element_type=F32)
    acs_row = lax.dot_general(da_row, trif, NT_DIMS, precision=HIGHEST, preferred_element_type=F32)
    tot_col = jnp.sum(da_col, axis=0, keepdims=True)
    lane16 = lax.broadcasted_iota(jnp.int32, (1, SSM_HEADS), 1)
    sub16 = lax.broadcasted_iota(jnp.int32, (SSM_HEADS, 1), 0)
    low = lax.broadcasted_iota(jnp.int32, (1, 2 * SSM_HEAD_DIM), 1) < SSM_HEAD_DIM

    def col(v, h):
        return jnp.sum(v * (lane16 == h).astype(F32), axis=1, keepdims=True)

    def row(v, h):
        return jnp.sum(v * (sub16 == h).astype(F32), axis=0, keepdims=True)

    ys, hos = [], []
    pairs_per_group = SSM_PAIRS // SSM_GROUPS
    for g in range(SSM_GROUPS):
        bb, cb16 = bs[g].astype(BF16), cs[g].astype(BF16)
        cb = lax.dot_general(cb16, bb, NT_DIMS, preferred_element_type=F32)
        for pp in range(pairs_per_group):
            p = g * pairs_per_group + pp
            h0, h1 = 2 * p, 2 * p + 1
            ac0, ac1 = col(acs_col, h0), col(acs_col, h1)
            seg0 = jnp.exp(jnp.where(tri, ac0 - row(acs_row, h0), -jnp.inf))
            seg1 = jnp.exp(jnp.where(tri, ac1 - row(acs_row, h1), -jnp.inf))
            dt_l = jnp.where(low, col(dt_col, h0), col(dt_col, h1))
            ac_l = jnp.where(low, ac0, ac1)
            tot_l = jnp.where(low, col(tot_col, h0), col(tot_col, h1))
            xdt = xps[p] * dt_l
            y = (jnp.dot((cb * seg0).astype(BF16), jnp.where(low, xdt, 0.0).astype(BF16), preferred_element_type=F32)
                 + jnp.dot((cb * seg1).astype(BF16), jnp.where(low, 0.0, xdt).astype(BF16), preferred_element_type=F32))
            y = y + jnp.dot(cb16, hps[p].astype(BF16), preferred_element_type=F32) * jnp.exp(ac_l)
            st = lax.dot_general(bb, (xdt * jnp.exp(tot_l - ac_l)).astype(BF16), TN_DIMS, preferred_element_type=F32)
            ys.append(y)
            hos.append(hps[p] * jnp.exp(tot_l) + st)
    return tuple(ys), tuple(hos)


def _ssd_chunk_of(direction, step, ncl, ncc):
    if direction == 0:
        return jnp.where(step < ncc, ncl + step, step - ncc)
    return jnp.where(step < ncc, ncl + ncc - 1 - step, ncl - 1 - (step - ncc))


def _ssd_load(u_ref):
    Q = SSM_CHUNK
    xps = tuple(u_ref[:, LANES * p:LANES * (p + 1)] for p in range(SSM_PAIRS))
    bs = tuple(u_ref[:, SSM_INNER + SSM_STATE * g:SSM_INNER + SSM_STATE * (g + 1)] for g in range(SSM_GROUPS))
    c0 = SSM_INNER + SSM_GROUPS * SSM_STATE
    cs = tuple(u_ref[:, c0 + SSM_STATE * g:c0 + SSM_STATE * (g + 1)] for g in range(SSM_GROUPS))
    return xps, bs, cs


def ssd_fwd(name, direction, u, dt, dt_t, alog_row, alog_col, L):
    T = u.shape[0]
    Q, N = SSM_CHUNK, SSM_STATE
    ncl, ncc = L // Q, (T - L) // Q
    nc = ncl + ncc
    cm = lambda s: _ssd_chunk_of(direction, s, ncl, ncc)

    def body(u_ref, dt_ref, dtt_ref, ar_ref, ac_ref, y_ref, hin_ref, state):
        @pl.when(pl.program_id(0) == 0)
        def _():
            state[...] = jnp.zeros_like(state)

        xps, bs, cs = _ssd_load(u_ref)
        hps = tuple(state[p] for p in range(SSM_PAIRS))
        for p in range(SSM_PAIRS):
            hin_ref[0, p] = hps[p]
        ys, hos = _ssd_chunk(direction, xps, bs, cs, dt_ref[...], dtt_ref[...], ar_ref[...], ac_ref[...], hps)
        for p in range(SSM_PAIRS):
            y_ref[:, LANES * p:LANES * (p + 1)] = ys[p]
            state[p] = hos[p]

    return pl.pallas_call(
        body, name=name, grid=(nc,),
        in_specs=[pl.BlockSpec((Q, SSM_CONV_DIM), lambda s: (cm(s), 0)),
                  pl.BlockSpec((Q, SSM_HEADS), lambda s: (cm(s), 0)),
                  pl.BlockSpec((SSM_HEADS, Q), lambda s: (0, cm(s))),
                  pl.BlockSpec((1, SSM_HEADS), lambda s: (0, 0)),
                  pl.BlockSpec((SSM_HEADS, 1), lambda s: (0, 0))],
        out_specs=[pl.BlockSpec((Q, SSM_INNER), lambda s: (cm(s), 0)),
                   pl.BlockSpec((1, SSM_PAIRS, N, LANES), lambda s: (cm(s), 0, 0, 0))],
        out_shape=[jax.ShapeDtypeStruct((T, SSM_INNER), F32), jax.ShapeDtypeStruct((nc, SSM_PAIRS, N, LANES), F32)],
        scratch_shapes=[pltpu.VMEM((SSM_PAIRS, N, LANES), F32)],
        compiler_params=_cparams(("arbitrary",)),
    )(u, dt, dt_t, alog_row, alog_col)


def ssd_bwd(name, direction, u, dt, dt_t, alog_row, alog_col, hin, dy, L, add_x=None, add_u=None):
    T = u.shape[0]
    Q, N = SSM_CHUNK, SSM_STATE
    ncl, ncc = L // Q, (T - L) // Q
    nc = ncl + ncc
    cm = lambda s: _ssd_chunk_of(direction, nc - 1 - s, ncl, ncc)
    n_add = (add_x is not None) + (add_u is not None)

    def body(*refs):
        u_ref, dt_ref, dtt_ref, ar_ref, ac_ref, hin_ref, dy_ref = refs[:7]
        add_refs = refs[7:7 + n_add]
        du_ref, ddt_ref, ddtt_ref, dar_ref, dac_ref, dstate = refs[7 + n_add:]
        first = pl.program_id(0) == 0

        @pl.when(first)
        def _():
            dstate[...] = jnp.zeros_like(dstate)

        xps, bs, cs = _ssd_load(u_ref)
        hps = tuple(hin_ref[0, p] for p in range(SSM_PAIRS))
        _, vjp = jax.vjp(functools.partial(_ssd_chunk, direction), xps, bs, cs, dt_ref[...], dtt_ref[...], ar_ref[...],
                         ac_ref[...], hps)
        dys = tuple(dy_ref[:, LANES * p:LANES * (p + 1)] for p in range(SSM_PAIRS))
        dhs = tuple(dstate[p] for p in range(SSM_PAIRS))
        gx, gb, gc, gdt, gdtt, gar, gac, ghp = vjp((dys, dhs))
        parts = list(gx) + list(gb) + list(gc)
        du = jnp.concatenate(parts, axis=1)
        k = 0
        if add_x is not None:
            du = du + jnp.concatenate([add_refs[k][...], jnp.zeros((Q, SSM_CONV_DIM - SSM_INNER), F32)], axis=1)
            k += 1
        if add_u is not None:
            du = du + add_refs[k][...]
        du_ref[...] = du
        ddt_ref[...] = gdt
        ddtt_ref[...] = gdtt
        _accumulate(dar_ref, gar, first)
        _accumulate(dac_ref, gac, first)
        for p in range(SSM_PAIRS):
            dstate[p] = ghp[p]

    in_specs = [pl.BlockSpec((Q, SSM_CONV_DIM), lambda s: (cm(s), 0)),
                pl.BlockSpec((Q, SSM_HEADS), lambda s: (cm(s), 0)),
                pl.BlockSpec((SSM_HEADS, Q), lambda s: (0, cm(s))),
                pl.BlockSpec((1, SSM_HEADS), lambda s: (0, 0)),
                pl.BlockSpec((SSM_HEADS, 1), lambda s: (0, 0)),
                pl.BlockSpec((1, SSM_PAIRS, N, LANES), lambda s: (cm(s), 0, 0, 0)),
                pl.BlockSpec((Q, SSM_INNER), lambda s: (cm(s), 0))]
    ins = [u, dt, dt_t, alog_row, alog_col, hin, dy]
    if add_x is not None:
        in_specs.append(pl.BlockSpec((Q, SSM_INNER), lambda s: (cm(s), 0)))
        ins.append(add_x)
    if add_u is not None:
        in_specs.append(pl.BlockSpec((Q, SSM_CONV_DIM), lambda s: (cm(s), 0)))
        ins.append(add_u)
    return pl.pallas_call(
        body, name=name, grid=(nc,), in_specs=in_specs,
        out_specs=[pl.BlockSpec((Q, SSM_CONV_DIM), lambda s: (cm(s), 0)),
                   pl.BlockSpec((Q, SSM_HEADS), lambda s: (cm(s), 0)),
                   pl.BlockSpec((SSM_HEADS, Q), lambda s: (0, cm(s))),
                   pl.BlockSpec((1, SSM_HEADS), lambda s: (0, 0)),
                   pl.BlockSpec((SSM_HEADS, 1), lambda s: (0, 0))],
        out_shape=[jax.ShapeDtypeStruct((T, SSM_CONV_DIM), F32), jax.ShapeDtypeStruct((T, SSM_HEADS), F32),
                   jax.ShapeDtypeStruct((SSM_HEADS, T), F32), jax.ShapeDtypeStruct((1, SSM_HEADS), F32),
                   jax.ShapeDtypeStruct((SSM_HEADS, 1), F32)],
        scratch_shapes=[pltpu.VMEM((SSM_PAIRS, N, LANES), F32)],
        compiler_params=_cparams(("arbitrary",)),
    )(*ins)


PEER_MASKS = (1, 2, 4, 3, 5, 6, 7)
N_PEERS = len(PEER_MASKS)
MESH_IDS = pl.DeviceIdType.MESH


def _my_index():
    return lax.axis_index("x") * 4 + lax.axis_index("y") * 2 + lax.axis_index("c")


def _coords(idx):
    return (idx // 4, (idx // 2) % 2, idx % 2)


def all_gather_hbm(name, arrays):
    n = len(arrays)

    def body(*refs):
        ins, outs = refs[:n], refs[n:2 * n]
        send_sems, recv_sems, local_sems = refs[2 * n:]
        me = _my_index()
        copies = []
        for a in range(n):
            local = pltpu.make_async_copy(ins[a], outs[a].at[me], local_sems.at[a])
            local.start()
            copies.append(local)
            for k, mask in enumerate(PEER_MASKS):
                peer = me ^ mask
                cp = pltpu.make_async_remote_copy(src_ref=ins[a], dst_ref=outs[a].at[me], send_sem=send_sems.at[a * N_PEERS + k],
                                                  recv_sem=recv_sems.at[a * N_PEERS + k], device_id=_coords(peer),
                                                  device_id_type=MESH_IDS)
                cp.start()
                copies.append(cp)
        for cp in copies:
            cp.wait()

    any_spec = pl.BlockSpec(memory_space=pl.ANY)
    return pl.pallas_call(
        body, name=name, in_specs=[any_spec] * n, out_specs=[any_spec] * n,
        out_shape=[jax.ShapeDtypeStruct((N_DEV,) + a.shape, a.dtype) for a in arrays],
        scratch_shapes=[pltpu.SemaphoreType.DMA((n * N_PEERS,)), pltpu.SemaphoreType.DMA((n * N_PEERS,)),
                        pltpu.SemaphoreType.DMA((n,))],
    )(*arrays)


def exchange_hbm(name, arrays):
    n = len(arrays)

    def body(*refs):
        ins, outs = refs[:n], refs[n:2 * n]
        send_sems, recv_sems, local_sems = refs[2 * n:]
        me = _my_index()
        copies = []
        for a in range(n):
            local = pltpu.make_async_copy(ins[a].at[me], outs[a].at[me], local_sems.at[a])
            local.start()
            copies.append(local)
            for k, mask in enumerate(PEER_MASKS):
                peer = me ^ mask
                cp = pltpu.make_async_remote_copy(src_ref=ins[a].at[peer], dst_ref=outs[a].at[me],
                                                  send_sem=send_sems.at[a * N_PEERS + k], recv_sem=recv_sems.at[a * N_PEERS + k],
                                                  device_id=_coords(peer), device_id_type=MESH_IDS)
                cp.start()
                copies.append(cp)
        for cp in copies:
            cp.wait()

    any_spec = pl.BlockSpec(memory_space=pl.ANY)
    return pl.pallas_call(
        body, name=name, in_specs=[any_spec] * n, out_specs=[any_spec] * n,
        out_shape=[jax.ShapeDtypeStruct(a.shape, a.dtype) for a in arrays],
        scratch_shapes=[pltpu.SemaphoreType.DMA((n * N_PEERS,)), pltpu.SemaphoreType.DMA((n * N_PEERS,)),
                        pltpu.SemaphoreType.DMA((n,))],
    )(*arrays)


def all_gather_vmem(name, v):
    def body(v_ref, out_ref, send_sems, recv_sems):
        me = _my_index()
        out_ref[me] = v_ref[...]
        copies = []
        for k, mask in enumerate(PEER_MASKS):
            cp = pltpu.make_async_remote_copy(src_ref=v_ref, dst_ref=out_ref.at[me], send_sem=send_sems.at[k],
                                              recv_sem=recv_sems.at[k], device_id=_coords(me ^ mask), device_id_type=MESH_IDS)
            cp.start()
            copies.append(cp)
        for cp in copies:
            cp.wait()

    vm = pl.BlockSpec(memory_space=pltpu.VMEM)
    return pl.pallas_call(
        body, name=name, in_specs=[vm], out_specs=vm, out_shape=jax.ShapeDtypeStruct((N_DEV,) + v.shape, v.dtype),
        scratch_shapes=[pltpu.SemaphoreType.DMA((N_PEERS,)), pltpu.SemaphoreType.DMA((N_PEERS,))],
    )(v)


def _row_tile(rows, cols, bufs):
    budget = 24 * 1024 * 1024 // (bufs * 2 * 4 * max(cols, LANES))
    if rows <= budget:
        return rows
    for t in range(budget - budget % 8, 7, -8):
        if rows % t == 0:
            return t
    return rows


def sum_parts(name, parts):
    P, R, C = parts.shape
    tr = _row_tile(R, C, P + 1)

    def body(p_ref, o_ref):
        acc = p_ref[0]
        for s in range(1, P):
            acc = acc + p_ref[s]
        o_ref[...] = acc

    return pl.pallas_call(body, name=name, grid=(R // tr,), in_specs=[pl.BlockSpec((P, tr, C), lambda i: (0, i, 0))],
                          out_specs=pl.BlockSpec((tr, C), lambda i: (i, 0)), out_shape=jax.ShapeDtypeStruct((R, C), F32),
                          compiler_params=_cparams(("parallel",)))(parts)


def adamw(name, w, g, m, v):
    R, C = w.shape
    tr = _row_tile(R, C, 7)

    def body(w_ref, g_ref, m_ref, v_ref, d_ref, nm_ref, nv_ref):
        g = g_ref[...]
        nm = ADAM_B1 * m_ref[...] + (1.0 - ADAM_B1) * g
        nv = ADAM_B2 * v_ref[...] + (1.0 - ADAM_B2) * (g * g)
        m_hat = nm / (1.0 - ADAM_B1 ** ADAM_STEP)
        v_hat = nv / (1.0 - ADAM_B2 ** ADAM_STEP)
        d_ref[...] = -ADAM_LR * (m_hat / (jnp.sqrt(v_hat) + ADAM_EPS) + ADAM_WD * w_ref[...])
        nm_ref[...] = nm
        nv_ref[...] = nv

    spec = pl.BlockSpec((tr, C), lambda i: (i, 0))
    return pl.pallas_call(body, name=name, grid=(R // tr,), in_specs=[spec] * 4, out_specs=[spec] * 3,
                          out_shape=[jax.ShapeDtypeStruct((R, C), F32)] * 3, compiler_params=_cparams(("parallel",)))(w, g, m, v)


def loss_and_grad(name, x, target, L, tm=ROW_TILE):
    T, D = x.shape
    nlt = L // tm

    def body(x_ref, t_ref, loss_ref, dx_ref):
        i = pl.program_id(0)
        err = jnp.where(i < nlt, x_ref[...] - t_ref[...], 0.0)
        dx_ref[...] = err * (1.0 / D)
        part = 0.5 * jnp.sum(jnp.sum(err * err, axis=1, keepdims=True), axis=0, keepdims=True) * (1.0 / D)
        _accumulate(loss_ref, part, i == 0)

    return pl.pallas_call(
        body, name=name, grid=(T // tm,),
        in_specs=[pl.BlockSpec((tm, D), lambda i: (i, 0)), pl.BlockSpec((tm, D), lambda i: (jnp.minimum(i, nlt - 1), 0))],
        out_specs=[pl.BlockSpec((1, 1), lambda i: (0, 0)), pl.BlockSpec((tm, D), lambda i: (i, 0))],
        out_shape=[jax.ShapeDtypeStruct((1, 1), F32), jax.ShapeDtypeStruct((T, D), F32)],
        compiler_params=_cparams(("arbitrary",)))(x, target)


def small_fwd(name, fn, arrays, out_shapes):
    def body(*refs):
        res = fn(*[r[...] for r in refs[:len(arrays)]])
        for o_ref, r in zip(refs[len(arrays):], res):
            o_ref[...] = r

    return pl.pallas_call(body, name=name, out_shape=[jax.ShapeDtypeStruct(s, F32) for s in out_shapes])(*arrays)


def small_bwd(name, fn, arrays, cts):
    n = len(arrays)

    def body(*refs):
        _, vjp = jax.vjp(lambda *a: tuple(fn(*a)), *[r[...] for r in refs[:n]])
        grads = vjp(tuple(r[...] for r in refs[n:n + len(cts)]))
        for o_ref, g in zip(refs[n + len(cts):], grads):
            o_ref[...] = g

    return pl.pallas_call(body, name=name, out_shape=[jax.ShapeDtypeStruct(a.shape, F32) for a in arrays])(*arrays, *cts)


def fn_silu(x):
    return (x * jax.nn.sigmoid(x),)


FWD_NAMES = ["x", "c", "ctx", "c_ctx", "w_mod", "b_mod", "norm1_g", "norm2_g", "w_in", "ssm_conv_w", "ssm_conv_b",
             "ssm_dt_bias", "ssm_a_log", "ssm_d", "ssm_norm_g", "swa_q_norm_g", "swa_k_norm_g", "swa_sink", "mla_q_lat_g",
             "mla_kv_lat_g", "w_mla_uq", "w_mla_ukv", "mla_q_norm_g", "mla_k_norm_g", "w_p_ssm", "w_p_swa", "w_p_mla",
             "w_out", "w_ffn_in", "w_ffn_out"]
WEIGHT_NAMES = FWD_NAMES[3:]
GATHERED = ["w_in", "w_mla_uq", "w_mla_ukv", "w_p_ssm", "w_p_swa", "w_p_mla", "w_out", "w_ffn_in", "w_ffn_out"]
COLUMN_SHARDED = ("w_in", "w_mla_uq", "w_mla_ukv", "w_ffn_in")
REPLICATED = ["c_ctx", "b_mod", "norm1_g", "norm2_g", "ssm_conv_b", "ssm_dt_bias", "ssm_a_log", "ssm_d", "ssm_norm_g",
              "swa_q_norm_g", "swa_k_norm_g", "swa_sink", "mla_q_lat_g", "mla_kv_lat_g", "mla_q_norm_g", "mla_k_norm_g"]
IN_SEGS = [("xbc", SSM_CONV_DIM), ("dt", 2 * SSM_HEADS), ("ks", SWA_KV_HEADS * SWA_HEAD_DIM), ("vs", SWA_KV_HEADS * SWA_HEAD_DIM),
           ("ckv", MLA_KV_RANK), ("kr", MLA_ROPE), ("z", SSM_INNER), ("qs", SWA_Q_HEADS * SWA_HEAD_DIM), ("cq", MLA_Q_RANK),
           ("g1", None), ("g2", None), ("g3", None)]


def _pack(vectors, multiple):
    flat = jnp.concatenate([v.reshape(-1) for v in vectors])
    pad = (-flat.shape[0]) % multiple
    return jnp.pad(flat, (0, pad)).reshape(-1, LANES)


def _unpack(packed, shapes):
    flat, out, off = packed.reshape(-1), [], 0
    for s in shapes:
        n = int(np.prod(s))
        out.append(flat[off:off + n].reshape(s))
        off += n
    return out


def _rope_tables(L, T, rot_dim):
    nf = rot_dim // 4
    inv = jnp.power(ROPE_BASE, -jnp.arange(nf, dtype=F32) / nf)
    r, col = jnp.meshgrid(jnp.arange(L // GRID_W, dtype=F32), jnp.arange(GRID_W, dtype=F32), indexing="ij")
    ang = jnp.stack([r.reshape(-1)[:, None] * inv, col.reshape(-1)[:, None] * inv], axis=1)
    cos, sin = jnp.cos(ang), jnp.sin(ang)
    c = jnp.concatenate([cos[:, 0], cos[:, 0], cos[:, 1], cos[:, 1]], axis=1)
    s = jnp.concatenate([-sin[:, 0], sin[:, 0], -sin[:, 1], sin[:, 1]], axis=1)
    c = jnp.pad(c, ((0, T - L), (0, LANES - rot_dim)), constant_values=1.0)
    s = jnp.pad(s, ((0, T - L), (0, LANES - rot_dim)))
    return c, s


def _pad_rows(a, rows):
    return jnp.pad(a, ((0, rows - a.shape[0]), (0, 0)))


def _row(w, per_head=0, off=0, diff=True):
    return ("row", w, per_head, off, diff)


PAR, PAR_ND = ("par", True), ("par", False)
GRP = ("grp", True)


def kernel(*args):
    n_fwd, n_w = len(FWD_NAMES), len(WEIGHT_NAMES)
    inp = dict(zip(FWD_NAMES, args[:n_fwd]))
    loss_target = args[n_fwd]
    mom_m = dict(zip(WEIGHT_NAMES, args[n_fwd + 1:n_fwd + 1 + n_w]))
    mom_v = dict(zip(WEIGHT_NAMES, args[n_fwd + 1 + n_w:]))

    x, ctx = inp["x"][0], inp["ctx"][0]
    L, D = x.shape
    n_ctx = ctx.shape[0]
    T = L + n_ctx
    depth = inp["w_in"].shape[0]
    me = _my_index()
    in_widths = [w if w is not None else D for _, w in IN_SEGS]
    in_offs = np.concatenate([[0], np.cumsum(in_widths)]).tolist()
    ffn_h = inp["w_ffn_out"].shape[1] * N_DEV
    cfg_swa = AttnCfg(SWA_Q_HEADS, SWA_Q_HEADS // SWA_KV_HEADS, SWA_HEAD_DIM, SWA_HEAD_DIM, SWA_HEAD_DIM ** -0.5, SWA_WINDOW,
                      True, L, T, 128)
    cfg_mla = AttnCfg(MLA_HEADS, 1, MLA_QK_PAD, MLA_V, MLA_QK ** -0.5, None, False, L, T, 256)

    def rf(name, fn, descs, arrays, outs, heads=1):
        return rowop_fwd(name, fn, descs, arrays, outs, T, L, heads=heads)

    def rb(name, fn, descs, arrays, outs, cts, heads=1, add=None):
        return rowop_bwd(name, fn, descs, arrays, outs, cts, T, L, heads=heads, add=add)

    local = []
    for n in GATHERED:
        w = inp[n]
        local.append((jnp.swapaxes(w, 1, 2) if n in COLUMN_SHARDED else w).astype(BF16))
    gathered = dict(zip(GATHERED, all_gather_hbm("gather_weights", local)))

    def full(n, l):
        g = gathered[n][:, l]
        return g.reshape(g.shape[0] * g.shape[1], g.shape[2])

    def layer_weights(l):
        wt = {}
        w_in_t = full("w_in", l)
        for (sn, _), o, w in zip(IN_SEGS, in_offs, in_widths):
            seg = w_in_t[o:o + w]
            wt[sn] = _pad_rows(seg, LANES) if sn == "kr" else seg
        uq = full("w_mla_uq", l).reshape(MLA_HEADS, MLA_QK, MLA_Q_RANK)
        wt["uqn"] = uq[:, :MLA_NOPE].reshape(MLA_HEADS * MLA_NOPE, MLA_Q_RANK)
        wt["uqr"] = jnp.pad(uq[:, MLA_NOPE:], ((0, 0), (0, LANES - MLA_ROPE), (0, 0))).reshape(MLA_HEADS * LANES, MLA_Q_RANK)
        ukv = full("w_mla_ukv", l).reshape(MLA_HEADS, MLA_NOPE + MLA_V, MLA_KV_RANK)
        wt["uk"] = ukv[:, :MLA_NOPE].reshape(MLA_HEADS * MLA_NOPE, MLA_KV_RANK)
        wt["uv"] = ukv[:, MLA_NOPE:].reshape(MLA_HEADS * MLA_V, MLA_KV_RANK)
        for n in ("w_p_ssm", "w_p_swa", "w_p_mla", "w_out", "w_ffn_out"):
            wt[n] = full(n, l)
        ffn_in_t = full("w_ffn_in", l)
        wt["fg"], wt["fu"] = ffn_in_t[:ffn_h], ffn_in_t[ffn_h:]
        return wt

    def layer_params(l):
        p = {}
        for n in ("norm1_g", "norm2_g", "ssm_conv_b", "ssm_norm_g", "swa_q_norm_g", "swa_k_norm_g", "mla_q_lat_g", "mla_kv_lat_g"):
            p[n] = inp[n][l][None]
        p["dt_bias"] = inp["ssm_dt_bias"][l].reshape(1, 2 * SSM_HEADS)
        p["alog_row"] = [inp["ssm_a_log"][l][d][None] for d in range(2)]
        p["alog_col"] = [inp["ssm_a_log"][l][d][:, None] for d in range(2)]
        p["d_lane"] = jnp.repeat(inp["ssm_d"][l], SSM_HEAD_DIM)[None]
        p["sink"] = inp["swa_sink"][l].reshape(SWA_Q_HEADS, 1, 1)
        for n, key in (("mla_q_norm_g", "gq"), ("mla_k_norm_g", "gk")):
            g = inp[n][l]
            p[key + "n"] = g[:MLA_NOPE][None]
            p[key + "r"] = jnp.pad(g[MLA_NOPE:], (0, LANES - MLA_ROPE))[None]
        return p

    conv_local = _pack([inp["ssm_conv_w"]], 8 * LANES)
    conv_all = all_gather_vmem("gather_conv_w", conv_local)
    cw = inp["ssm_conv_w"].shape
    conv_full = conv_all.reshape(N_DEV, -1)[:, :cw[0] * cw[1] * cw[2]].reshape(N_DEV, cw[0], cw[1], cw[2])
    conv_full = jnp.moveaxis(conv_full, 0, 2).reshape(cw[0], cw[1], N_DEV * cw[2])
    conv_w8 = jnp.pad(conv_full, ((0, 0), (0, 8 - cw[1]), (0, 0)))

    silu_c, silu_cc = small_fwd("silu_c", lambda a, b: fn_silu(a) + fn_silu(b), [inp["c"], inp["c_ctx"][None]], [(1, D), (1, D)])
    silu_all = all_gather_vmem("gather_silu_c", silu_c.reshape(D // LANES, LANES)).reshape(N_DEV, D)
    S_rows = 2 * N_DEV
    S_mat = jnp.concatenate([silu_all, silu_cc, jnp.zeros((S_rows - N_DEV - 1, D), F32)], axis=0)
    mod_cols = inp["w_mod"].shape[2]
    mods_local = []
    for l in range(depth):
        bias = lax.dynamic_slice(inp["b_mod"][l], (me * mod_cols,), (mod_cols,))
        mods_local.append(matmul(S_mat, inp["w_mod"][l], "nn", f"mod{l}", add=jnp.broadcast_to(bias[None], (S_rows, mod_cols))))
    mods_all = all_gather_vmem("gather_mods", jnp.stack(mods_local).reshape(-1, LANES))
    mods_all = jnp.moveaxis(mods_all.reshape(N_DEV, depth, S_rows, mod_cols), 0, 2).reshape(depth, S_rows, N_DEV * mod_cols)
    mods_lat = lax.dynamic_slice(mods_all, (0, me, 0), (depth, 1, N_DEV * mod_cols))[:, 0]
    mods_ctx = mods_all[:, N_DEV]

    def layer_mods(l):
        return [jnp.stack([mods_lat[l, j * D:(j + 1) * D], mods_ctx[l, j * D:(j + 1) * D]])[:, None] for j in range(6)]

    cs_swa = _rope_tables(L, T, SWA_HEAD_DIM)
    cs_mla = _rope_tables(L, T, MLA_ROPE)
    nm_descs = [_row(D), PAR, GRP, GRP]
    resid_descs = [_row(D, diff=False), _row(D), GRP]
    tab = [_row(LANES, diff=False), _row(LANES, diff=False)]
    swaq_descs = [_row(SWA_HEAD_DIM, 1), PAR] + tab
    swakv_descs = [_row(SWA_HEAD_DIM, 1), _row(SWA_HEAD_DIM, 1), PAR] + tab
    mlaq_descs = [_row(LANES, 1), _row(LANES, 1), PAR, PAR] + tab
    mlakv_descs = [_row(LANES, 1), _row(LANES, 1), _row(LANES), PAR, PAR] + tab
    ssdout_descs = [_row(SSM_INNER), _row(SSM_INNER, diff=False), _row(SSM_INNER), _row(SSM_INNER), PAR, PAR]
    merge_descs = [_row(D)] * 6
    swiglu_descs = [_row(ffn_h), _row(ffn_h)]
    seg_names = [sn for sn, _ in IN_SEGS]

    def layer_fwd(l, X, wt, p, mods):
        sh1, sc1, gt1, sh2, sc2, gt2 = mods
        r = {"X": X}
        r["h1"] = rf(f"l{l}_norm1", fn_norm_mod, nm_descs, [X, p["norm1_g"], sh1, sc1], [(D, 0, BF16)])[0]
        for sn in seg_names:
            r[sn] = matmul(r["h1"], wt[sn], "nt", f"l{l}_in_{sn}")
        r["u"] = conv_fwd(f"l{l}_conv", r["xbc"], conv_w8[l], p["ssm_conv_b"], L)
        r["dts"] = rf(f"l{l}_softplus", fn_softplus, [_row(2 * SSM_HEADS), PAR], [r["dt"], p["dt_bias"]], [(2 * SSM_HEADS, 0, F32)])[0]
        for d in range(2):
            dt_d = r["dts"][:, d * SSM_HEADS:(d + 1) * SSM_HEADS]
            r[f"dt{d}"], r[f"dtt{d}"] = dt_d, dt_d.T
            r[f"y{d}"], r[f"hin{d}"] = ssd_fwd(f"l{l}_ssd{d}", d, r["u"], dt_d, dt_d.T, p["alog_row"][d], p["alog_col"][d], L)
        r["ys"] = rf(f"l{l}_ssd_out", fn_ssd_out, ssdout_descs, [r["y0"], r["y1"], r["u"], r["z"], p["d_lane"], p["ssm_norm_g"]],
                     [(SSM_INNER, 0, F32)])[0]
        r["Qs"] = rf(f"l{l}_swa_q", fn_swa_q, swaq_descs, [r["qs"], p["swa_q_norm_g"], *cs_swa], [(SWA_HEAD_DIM, 1, BF16)], SWA_Q_HEADS)[0]
        r["Ks"], r["Vs"] = rf(f"l{l}_swa_kv", fn_swa_kv, swakv_descs, [r["ks"], r["vs"], p["swa_k_norm_g"], *cs_swa],
                              [(SWA_HEAD_DIM, 1, BF16), (SWA_HEAD_DIM, 1, BF16)], SWA_KV_HEADS)
        r["Os"], r["lse_s"] = flash_fwd(f"l{l}_swa_fwd", cfg_swa, r["Qs"], r["Ks"], r["Vs"], p["sink"])
        r["cqn"] = rf(f"l{l}_q_lat", fn_rms, [_row(MLA_Q_RANK), PAR], [r["cq"], p["mla_q_lat_g"]], [(MLA_Q_RANK, 0, BF16)])[0]
        r["qn"] = matmul(r["cqn"], wt["uqn"], "nt", f"l{l}_uqn")
        r["qr"] = matmul(r["cqn"], wt["uqr"], "nt", f"l{l}_uqr")
        r["Qm"] = rf(f"l{l}_mla_q", fn_mla_q, mlaq_descs, [r["qn"], r["qr"], p["gqn"], p["gqr"], *cs_mla], [(MLA_QK_PAD, 1, BF16)], MLA_HEADS)[0]
        r["ckvn"] = rf(f"l{l}_kv_lat", fn_rms, [_row(MLA_KV_RANK), PAR], [r["ckv"], p["mla_kv_lat_g"]], [(MLA_KV_RANK, 0, BF16)])[0]
        r["kn"] = matmul(r["ckvn"], wt["uk"], "nt", f"l{l}_uk")
        r["vp"] = matmul(r["ckvn"], wt["uv"], "nt", f"l{l}_uv")
        r["Km"], r["Vm"] = rf(f"l{l}_mla_kv", fn_mla_kv, mlakv_descs, [r["kn"], r["vp"], r["kr"], p["gkn"], p["gkr"], *cs_mla],
                              [(MLA_QK_PAD, 1, BF16), (MLA_V, 1, BF16)], MLA_HEADS)
        r["Om"], r["lse_m"] = flash_fwd(f"l{l}_mla_fwd", cfg_mla, r["Qm"], r["Km"], r["Vm"], None)
        r["P1"] = matmul(r["ys"], wt["w_p_ssm"], "nn", f"l{l}_p_ssm")
        r["P2"] = matmul(r["Os"], wt["w_p_swa"], "nn", f"l{l}_p_swa")
        r["P3"] = matmul(r["Om"], wt["w_p_mla"], "nn", f"l{l}_p_mla")
        r["mg"] = rf(f"l{l}_merge", fn_merge, merge_descs, [r["g1"], r["g2"], r["g3"], r["P1"], r["P2"], r["P3"]], [(D, 0, BF16)])[0]
        r["A"] = matmul(r["mg"], wt["w_out"], "nn", f"l{l}_out")
        r["X1"] = rf(f"l{l}_resid1", fn_resid, resid_descs, [X, r["A"], gt1], [(D, 0, F32)])[0]
        r["h2"] = rf(f"l{l}_norm2", fn_norm_mod, nm_descs, [r["X1"], p["norm2_g"], sh2, sc2], [(D, 0, BF16)])[0]
        r["Fg"] = matmul(r["h2"], wt["fg"], "nt", f"l{l}_ffn_g")
        r["Fu"] = matmul(r["h2"], wt["fu"], "nt", f"l{l}_ffn_u")
        r["sg"] = rf(f"l{l}_swiglu", fn_swiglu, swiglu_descs, [r["Fg"], r["Fu"]], [(ffn_h, 0, BF16)])[0]
        r["B"] = matmul(r["sg"], wt["w_ffn_out"], "nn", f"l{l}_ffn_out")
        X2 = rf(f"l{l}_resid2", fn_resid, resid_descs, [r["X1"], r["B"], gt2], [(D, 0, F32)])[0]
        return X2, r

    def attn_bwd(tag, cfg, q, k, v, o, lse, do, sink):
        res = attn_delta(f"{tag}_delta", cfg, o, do, lse, sink)
        delta, dob = res[0], res[1]
        dsink = res[2] if cfg.has_sink else None
        dq = flash_dq(f"{tag}_dq", cfg, q, k, v, dob, lse, delta)
        shp = (cfg.hq, T // cfg.chunk, 1, cfg.chunk)
        dk, dv = flash_dkv(f"{tag}_dkv", cfg, q, k, v, dob, lse.reshape(shp), delta.reshape(shp))
        return dq, dk, dv, dsink

    def layer_bwd(l, dX2, r, wt, p, mods):
        sh1, sc1, gt1, sh2, sc2, gt2 = mods
        g, gw = {}, {}
        dmod = [None] * 6
        dB, dmod[5] = rb(f"l{l}_resid2_b", fn_resid, resid_descs, [r["X1"], r["B"], gt2], [(D, 0, F32)], [dX2])
        dsg = matmul(dB, wt["w_ffn_out"], "nt", f"l{l}_ffn_out_da")
        gw["w_ffn_out"] = matmul(r["sg"], dB, "tn", f"l{l}_ffn_out_dw")
        dFg, dFu = rb(f"l{l}_swiglu_b", fn_swiglu, swiglu_descs, [r["Fg"], r["Fu"]], [(ffn_h, 0, BF16)], [dsg])
        dh2 = matmul(dFg, wt["fg"], "nn", f"l{l}_ffn_g_da")
        dh2 = matmul(dFu, wt["fu"], "nn", f"l{l}_ffn_u_da", add=dh2)
        gw["w_ffn_in"] = jnp.concatenate([matmul(r["h2"], dFg, "tn", f"l{l}_ffn_g_dw"), matmul(r["h2"], dFu, "tn", f"l{l}_ffn_u_dw")], axis=1)
        dX1, g["norm2_g"], dmod[3], dmod[4] = rb(f"l{l}_norm2_b", fn_norm_mod, nm_descs, [r["X1"], p["norm2_g"], sh2, sc2],
                                                [(D, 0, BF16)], [dh2], add={0: dX2})
        dA, dmod[2] = rb(f"l{l}_resid1_b", fn_resid, resid_descs, [r["X"], r["A"], gt1], [(D, 0, F32)], [dX1])
        dmg = matmul(dA, wt["w_out"], "nt", f"l{l}_out_da")
        gw["w_out"] = matmul(r["mg"], dA, "tn", f"l{l}_out_dw")
        dsegs = {}
        dsegs["g1"], dsegs["g2"], dsegs["g3"], dP1, dP2, dP3 = rb(
            f"l{l}_merge_b", fn_merge, merge_descs, [r["g1"], r["g2"], r["g3"], r["P1"], r["P2"], r["P3"]], [(D, 0, BF16)], [dmg])
        dys = matmul(dP1, wt["w_p_ssm"], "nt", f"l{l}_p_ssm_da")
        dOs = matmul(dP2, wt["w_p_swa"], "nt", f"l{l}_p_swa_da")
        dOm = matmul(dP3, wt["w_p_mla"], "nt", f"l{l}_p_mla_da")
        gw["w_p_ssm"] = matmul(r["ys"], dP1, "tn", f"l{l}_p_ssm_dw")
        gw["w_p_swa"] = matmul(r["Os"], dP2, "tn", f"l{l}_p_swa_dw")
        gw["w_p_mla"] = matmul(r["Om"], dP3, "tn", f"l{l}_p_mla_dw")
        dQm, dKm, dVm, _ = attn_bwd(f"l{l}_mla", cfg_mla, r["Qm"], r["Km"], r["Vm"], r["Om"], r["lse_m"], dOm, None)
        dkn, dvp, dsegs["kr"], dgkn, dgkr = rb(f"l{l}_mla_kv_b", fn_mla_kv, mlakv_descs,
                                               [r["kn"], r["vp"], r["kr"], p["gkn"], p["gkr"], *cs_mla],
                                               [(MLA_QK_PAD, 1, BF16), (MLA_V, 1, BF16)], [dKm, dVm], MLA_HEADS)
        dckvn = matmul(dkn, wt["uk"], "nn", f"l{l}_uk_da")
        dckvn = matmul(dvp, wt["uv"], "nn", f"l{l}_uv_da", add=dckvn)
        dw_uk = matmul(r["ckvn"], dkn, "tn", f"l{l}_uk_dw").reshape(MLA_KV_RANK, MLA_HEADS, MLA_NOPE)
        dw_uv = matmul(r["ckvn"], dvp, "tn", f"l{l}_uv_dw").reshape(MLA_KV_RANK, MLA_HEADS, MLA_V)
        gw["w_mla_ukv"] = jnp.concatenate([dw_uk, dw_uv], axis=2).reshape(MLA_KV_RANK, -1)
        dsegs["ckv"], g["mla_kv_lat_g"] = rb(f"l{l}_kv_lat_b", fn_rms, [_row(MLA_KV_RANK), PAR], [r["ckv"], p["mla_kv_lat_g"]],
                                             [(MLA_KV_RANK, 0, BF16)], [dckvn])
        dqn, dqr, dgqn, dgqr = rb(f"l{l}_mla_q_b", fn_mla_q, mlaq_descs, [r["qn"], r["qr"], p["gqn"], p["gqr"], *cs_mla],
                                  [(MLA_QK_PAD, 1, BF16)], [dQm], MLA_HEADS)
        dcqn = matmul(dqn, wt["uqn"], "nn", f"l{l}_uqn_da")
        dcqn = matmul(dqr, wt["uqr"], "nn", f"l{l}_uqr_da", add=dcqn)
        dw_uqn = matmul(r["cqn"], dqn, "tn", f"l{l}_uqn_dw").reshape(MLA_Q_RANK, MLA_HEADS, MLA_NOPE)
        dw_uqr = matmul(r["cqn"], dqr, "tn", f"l{l}_uqr_dw").reshape(MLA_Q_RANK, MLA_HEADS, LANES)[:, :, :MLA_ROPE]
        gw["w_mla_uq"] = jnp.concatenate([dw_uqn, dw_uqr], axis=2).reshape(MLA_Q_RANK, -1)
        dsegs["cq"], g["mla_q_lat_g"] = rb(f"l{l}_q_lat_b", fn_rms, [_row(MLA_Q_RANK), PAR], [r["cq"], p["mla_q_lat_g"]],
                                           [(MLA_Q_RANK, 0, BF16)], [dcqn])
        g["mla_q_norm_g"] = jnp.concatenate([dgqn[0], dgqr[0, :MLA_ROPE]])
        g["mla_k_norm_g"] = jnp.concatenate([dgkn[0], dgkr[0, :MLA_ROPE]])
        dQs, dKs, dVs, dsink = attn_bwd(f"l{l}_swa", cfg_swa, r["Qs"], r["Ks"], r["Vs"], r["Os"], r["lse_s"], dOs, p["sink"])
        g["swa_sink"] = dsink.reshape(SWA_Q_HEADS)
        dsegs["qs"], g["swa_q_norm_g"] = rb(f"l{l}_swa_q_b", fn_swa_q, swaq_descs, [r["qs"], p["swa_q_norm_g"], *cs_swa],
                                            [(SWA_HEAD_DIM, 1, BF16)], [dQs], SWA_Q_HEADS)
        dsegs["ks"], dsegs["vs"], g["swa_k_norm_g"] = rb(f"l{l}_swa_kv_b", fn_swa_kv, swakv_descs,
                                                         [r["ks"], r["vs"], p["swa_k_norm_g"], *cs_swa],
                                                         [(SWA_HEAD_DIM, 1, BF16), (SWA_HEAD_DIM, 1, BF16)], [dKs, dVs], SWA_KV_HEADS)
        dy, dxs, dsegs["z"], dd_lane, g["ssm_norm_g"] = rb(
            f"l{l}_ssd_out_b", fn_ssd_out, ssdout_descs, [r["y0"], r["y1"], r["u"], r["z"], p["d_lane"], p["ssm_norm_g"]],
            [(SSM_INNER, 0, F32)], [dys])
        g["ssm_d"] = dd_lane.reshape(SSM_HEADS, SSM_HEAD_DIM).sum(axis=1)
        du, ddts, dalog = None, [], []
        for d in range(2):
            du, ddt, ddtt, dar, dac = ssd_bwd(f"l{l}_ssd{d}_b", d, r["u"], r[f"dt{d}"], r[f"dtt{d}"], p["alog_row"][d], p["alog_col"][d],
                                              r[f"hin{d}"], dy, L, add_x=dxs if d == 0 else None, add_u=du)
            ddts.append(ddt + ddtt.T)
            dalog.append(dar[0] + dac[:, 0])
        g["ssm_a_log"] = jnp.stack(dalog)
        dsegs["xbc"], dconv_w, g["ssm_conv_b"] = conv_bwd(f"l{l}_conv_b", r["xbc"], conv_w8[l], p["ssm_conv_b"], du, L)
        dsegs["dt"], ddt_bias = rb(f"l{l}_softplus_b", fn_softplus, [_row(2 * SSM_HEADS), PAR], [r["dt"], p["dt_bias"]],
                                   [(2 * SSM_HEADS, 0, F32)], [jnp.concatenate(ddts, axis=1)])
        g["ssm_dt_bias"] = ddt_bias.reshape(2, SSM_HEADS)
        g["ssm_conv_w"] = dconv_w[:SSM_CONV]
        dh1, dws = None, []
        for sn, w in zip(seg_names, in_widths):
            dh1 = matmul(dsegs[sn], wt[sn], "nn", f"l{l}_in_{sn}_da", add=dh1)
            dws.append(matmul(r["h1"], dsegs[sn], "tn", f"l{l}_in_{sn}_dw")[:, :w])
        gw["w_in"] = jnp.concatenate(dws, axis=1)
        dX, g["norm1_g"], dmod[0], dmod[1] = rb(f"l{l}_norm1_b", fn_norm_mod, nm_descs, [r["X"], p["norm1_g"], sh1, sc1],
                                               [(D, 0, BF16)], [dh1], add={0: dX1})
        for n in ("norm1_g", "norm2_g", "ssm_conv_b", "ssm_norm_g", "swa_q_norm_g", "swa_k_norm_g", "mla_q_lat_g", "mla_kv_lat_g"):
            g[n] = g[n][0]
        dmod_lat = jnp.concatenate([dm[0, 0] for dm in dmod])
        dmod_ctx = jnp.concatenate([dm[1, 0] for dm in dmod])
        return dX, g, gw, dmod_lat, dmod_ctx

    X = jnp.concatenate([x, ctx], axis=0)
    saved = []
    for l in range(depth):
        wt, p, mods = layer_weights(l), layer_params(l), layer_mods(l)
        X, r = layer_fwd(l, X, wt, p, mods)
        saved.append((r, wt, p, mods))
    loss_part, dX = loss_and_grad("loss", X, loss_target[0], L)
    loss = lax.psum(loss_part[0, 0], ("x", "y", "c"))
    small_g = [None] * depth
    big_g = [None] * depth
    dmods = [None] * depth
    for l in reversed(range(depth)):
        r, wt, p, mods = saved[l]
        dX, small_g[l], big_g[l], dm_lat, dm_ctx = layer_bwd(l, dX, r, wt, p, mods)
        dmods[l] = jnp.stack([dm_lat, dm_ctx])
    grad_x = dX[:L][None]

    dm_all = all_gather_vmem("gather_dmods", jnp.stack(dmods).reshape(-1, LANES)).reshape(N_DEV, depth, 2, N_DEV * mod_cols)
    dm_rows = jnp.concatenate([jnp.moveaxis(dm_all[:, :, 0], 0, 1), dm_all[:, :, 1].sum(axis=0)[:, None],
                               jnp.zeros((depth, S_rows - N_DEV - 1, N_DEV * mod_cols), F32)], axis=1)
    dm_mine = lax.dynamic_slice(dm_rows, (0, 0, me * mod_cols), (depth, S_rows, mod_cols))
    grads = {}
    grads["w_mod"] = jnp.stack([matmul(S_mat, dm_mine[l], "tn", f"mod{l}_dw") for l in range(depth)])
    d_silu = None
    for l in range(depth):
        d_silu = matmul(dm_mine[l], inp["w_mod"][l], "nt", f"mod{l}_da", add=d_silu)
    small = {n: jnp.stack([small_g[l][n] for l in range(depth)]) for n in small_g[0]}
    small["c_ctx"] = small_bwd("silu_c_b", fn_silu, [inp["c_ctx"][None]], [d_silu[N_DEV:N_DEV + 1]])[0][0]
    small["b_mod"] = jnp.stack(dmods).sum(axis=1)

    rep_shapes = [inp[n].shape for n in REPLICATED]
    conv_shape = (depth, SSM_CONV, SSM_CONV_DIM)
    packed = _pack([small[n] for n in REPLICATED] + [small["ssm_conv_w"]], 8 * LANES)
    small_sum = sum_parts("sum_small", all_gather_vmem("gather_small", packed))
    summed = _unpack(small_sum, rep_shapes + [conv_shape])
    for n, gsum in zip(REPLICATED, summed):
        grads[n] = gsum
    grads["ssm_conv_w"] = lax.dynamic_slice(summed[-1], (0, 0, me * cw[2]), cw)

    slabs = []
    for n in GATHERED:
        gfull = jnp.stack([big_g[l][n] for l in range(depth)])
        if n in COLUMN_SHARDED:
            k_dim, n_dim = gfull.shape[1], gfull.shape[2]
            slabs.append(jnp.moveaxis(gfull.reshape(depth, k_dim, N_DEV, n_dim // N_DEV), 2, 0))
        else:
            k_dim, n_dim = gfull.shape[1], gfull.shape[2]
            slabs.append(jnp.moveaxis(gfull.reshape(depth, N_DEV, k_dim // N_DEV, n_dim), 1, 0))
    for n, parts in zip(GATHERED, exchange_hbm("exchange_grads", slabs)):
        shp = inp[n].shape
        grads[n] = sum_parts(f"sum_{n}", parts.reshape(N_DEV, shp[0] * shp[1], shp[2])).reshape(shp)

    delta, new_m, new_v = {}, {}, {}
    rep_pack = lambda d: _pack([d[n] for n in REPLICATED], 8 * LANES)
    rep_out = adamw("adamw_small", rep_pack(inp), rep_pack(grads), rep_pack(mom_m), rep_pack(mom_v))
    for out, res in zip((delta, new_m, new_v), rep_out):
        for n, a in zip(REPLICATED, _unpack(res, rep_shapes)):
            out[n] = a
    for n in ["w_mod", "ssm_conv_w"] + GATHERED:
        shp = inp[n].shape
        two_d = (shp[0] * shp[1], shp[2])
        res = adamw(f"adamw_{n}", inp[n].reshape(two_d), grads[n].reshape(two_d), mom_m[n].reshape(two_d), mom_v[n].reshape(two_d))
        delta[n], new_m[n], new_v[n] = [a.reshape(shp) for a in res]

    return (loss, grad_x, *[grads[n] for n in WEIGHT_NAMES], *[delta[n] for n in WEIGHT_NAMES],
            *[new_m[n] for n in WEIGHT_NAMES], *[new_v[n] for n in WEIGHT_NAMES])
```

```python
import functools
import math

import numpy as np
import jax
import jax.numpy as jnp
from jax import lax
from jax.experimental import pallas as pl
from jax.experimental.pallas import tpu as pltpu

F32 = jnp.float32
BF16 = jnp.bfloat16

N_DEV = 8
V7X_VMEM_BYTES = 64 * 1024 * 1024
VMEM_LIMIT_BYTES = V7X_VMEM_BYTES - 8 * 1024 * 1024
LANES = 128

EPS = 1e-6
ROPE_BASE = 10000.0
GRID_W = 64
SSM_HEADS, SSM_HEAD_DIM, SSM_GROUPS, SSM_STATE, SSM_CONV, SSM_CHUNK = 16, 64, 2, 128, 5, 128
SSM_INNER = SSM_HEADS * SSM_HEAD_DIM
SSM_CONV_DIM = SSM_INNER + 2 * SSM_GROUPS * SSM_STATE
SWA_Q_HEADS, SWA_KV_HEADS, SWA_HEAD_DIM, SWA_WINDOW = 8, 2, 128, 128
MLA_HEADS, MLA_Q_RANK, MLA_KV_RANK, MLA_NOPE, MLA_ROPE, MLA_V = 8, 384, 256, 128, 64, 128
MLA_QK = MLA_NOPE + MLA_ROPE
MLA_QK_PAD = 2 * LANES
ADAM_LR, ADAM_B1, ADAM_B2, ADAM_EPS, ADAM_WD, ADAM_STEP = 0.001, 0.9, 0.999, 1e-08, 0.01, 10

ROW_TILE = 256


def _cparams(sem, **kw):
    return pltpu.CompilerParams(dimension_semantics=sem, vmem_limit_bytes=VMEM_LIMIT_BYTES, **kw)


def _pick(dim, prefs):
    for p in prefs:
        if dim % p == 0:
            return p
    return dim


def matmul(a, b, mode, name, out_dtype=F32, add=None):
    if mode == "nn":
        (M, K), (K2, N) = a.shape, b.shape
    elif mode == "nt":
        (M, K), (N, K2) = a.shape, b.shape
    else:
        (K, M), (K2, N) = a.shape, b.shape
    assert K == K2, (name, a.shape, b.shape)
    tm = _pick(M, (768, 512, 384, 256, 128))
    tn = _pick(N, (512, 384, 256, 128))
    tk = K if K <= 1536 else _pick(K, (1408, 768, 1024, 512, 256))
    nk = K // tk
    dims = {"nn": (((1,), (0,)), ((), ())), "nt": (((1,), (1,)), ((), ())), "tn": (((0,), (0,)), ((), ()))}[mode]
    a_spec = pl.BlockSpec((tk, tm), lambda i, j, k: (k, i)) if mode == "tn" else pl.BlockSpec((tm, tk), lambda i, j, k: (i, k))
    b_spec = pl.BlockSpec((tn, tk), lambda i, j, k: (j, k)) if mode == "nt" else pl.BlockSpec((tk, tn), lambda i, j, k: (k, j))
    o_spec = pl.BlockSpec((tm, tn), lambda i, j, k: (i, j))
    has_add = add is not None

    def body(*refs):
        if has_add:
            a_ref, b_ref, c_ref, o_ref, acc_ref = refs
        else:
            a_ref, b_ref, o_ref, acc_ref = refs
        k = pl.program_id(2)
        part = lax.dot_general(a_ref[...].astype(BF16), b_ref[...].astype(BF16), dims, preferred_element_type=F32)

        @pl.when(k == 0)
        def _():
            acc_ref[...] = part + c_ref[...] if has_add else part

        @pl.when(k > 0)
        def _():
            acc_ref[...] += part

        @pl.when(k == nk - 1)
        def _():
            o_ref[...] = acc_ref[...].astype(o_ref.dtype)

    ins = [a, b] + ([add] if has_add else [])
    in_specs = [a_spec, b_spec] + ([o_spec] if has_add else [])
    return pl.pallas_call(
        body, name=name, grid=(M // tm, N // tn, nk), in_specs=in_specs, out_specs=o_spec,
        out_shape=jax.ShapeDtypeStruct((M, N), out_dtype),
        scratch_shapes=[pltpu.VMEM((tm, tn), F32)],
        input_output_aliases=({2: 0} if has_add else {}),
        compiler_params=_cparams(("parallel", "parallel", "arbitrary")),
    )(*ins)


def _row_specs(descs, arrays, tm, nct, heads):
    specs = []
    for d, arr in zip(descs, arrays):
        if d[0] == "row":
            _, w, per_head, off, _ = d
            specs.append(pl.BlockSpec((tm, w * (heads if per_head else 1)), lambda i, off=off: (i, off)))
        elif d[0] == "par":
            specs.append(pl.BlockSpec(arr.shape, lambda i, nd=arr.ndim: (0,) * nd))
        else:
            specs.append(pl.BlockSpec((1,) + arr.shape[1:], lambda i, nd=arr.ndim: (jnp.where(i >= nct, 1, 0),) + (0,) * (nd - 1)))
    return specs


def _load(d, ref, h):
    if d[0] == "grp":
        return ref[0]
    if d[0] == "row" and d[2]:
        return ref[:, h * d[1]:(h + 1) * d[1]]
    return ref[...]


def _out_specs(outs, tm, heads):
    return [pl.BlockSpec((tm, w * (heads if ph else 1)), lambda i: (i, 0)) for (w, ph, _) in outs]


def rowop_fwd(name, fn, descs, arrays, outs, T, n_ctx, heads=1, tm=ROW_TILE):
    nct = n_ctx // tm
    n_in = len(descs)

    def body(*refs):
        for h in range(heads):
            res = fn(*[_load(d, r, h) for d, r in zip(descs, refs[:n_in])])
            for o_ref, r, (w, ph, _) in zip(refs[n_in:], res, outs):
                if ph:
                    o_ref[:, h * w:(h + 1) * w] = r.astype(o_ref.dtype)
                else:
                    o_ref[...] = r.astype(o_ref.dtype)

    out_shape = [jax.ShapeDtypeStruct((T, w * (heads if ph else 1)), dt) for (w, ph, dt) in outs]
    return pl.pallas_call(
        body, name=name, grid=(T // tm,), in_specs=_row_specs(descs, arrays, tm, nct, heads), out_specs=_out_specs(outs, tm, heads),
        out_shape=out_shape, compiler_params=_cparams(("parallel",)),
    )(*arrays)


def rowop_bwd(name, fn, descs, arrays, outs, cts, T, n_ctx, heads=1, tm=ROW_TILE, add=None):
    nct = n_ctx // tm
    n_in, n_ct = len(descs), len(cts)
    add = add or {}
    diff_idx = [k for k, d in enumerate(descs) if d[-1]]
    add_idx = [k for k in diff_idx if k in add]

    def body(*refs):
        in_refs, ct_refs = refs[:n_in], refs[n_in:n_in + n_ct]
        add_refs = dict(zip(add_idx, refs[n_in + n_ct:n_in + n_ct + len(add_idx)]))
        g_refs = refs[n_in + n_ct + len(add_idx):]
        i = pl.program_id(0)
        shared = {}
        for h in range(heads):
            vals = [_load(d, r, h) for d, r in zip(descs, in_refs)]

            def f(*dvals, vals=vals):
                full = list(vals)
                for k, v in zip(diff_idx, dvals):
                    full[k] = v
                return tuple(fn(*full))

            _, vjp = jax.vjp(f, *[vals[k] for k in diff_idx])
            cts_h = tuple(c[:, h * w:(h + 1) * w] if ph else c[...] for c, (w, ph, _) in zip(ct_refs, outs))
            for k, g_ref, g in zip(diff_idx, g_refs, vjp(cts_h)):
                d = descs[k]
                if d[0] == "row" and d[2]:
                    g_ref[:, h * d[1]:(h + 1) * d[1]] = g.astype(g_ref.dtype)
                else:
                    shared[k] = g if k not in shared else shared[k] + g
        for k, g_ref in zip(diff_idx, g_refs):
            d = descs[k]
            if k not in shared:
                continue
            g = shared[k]
            if d[0] == "row":
                if k in add_refs:
                    g = g + add_refs[k][...]
                g_ref[...] = g.astype(g_ref.dtype)
            elif d[0] == "par":
                _accumulate(g_ref, g, i == 0)
            else:
                _accumulate(g_ref, g[None], jnp.logical_or(i == 0, i == nct))

    in_specs = _row_specs(descs, arrays, tm, nct, heads)
    g_specs, g_shape = [], []
    for k in diff_idx:
        d = descs[k]
        if d[0] == "row":
            g_specs.append(pl.BlockSpec((tm, d[1] * (heads if d[2] else 1)), lambda i: (i, 0)))
            g_shape.append(jax.ShapeDtypeStruct((T, d[1] * (heads if d[2] else 1)), F32))
        else:
            g_specs.append(in_specs[k])
            g_shape.append(jax.ShapeDtypeStruct(arrays[k].shape, F32))
    add_specs = [g_specs[diff_idx.index(k)] for k in add_idx]
    return pl.pallas_call(
        body, name=name, grid=(T // tm,), in_specs=in_specs + _out_specs(outs, tm, heads) + add_specs, out_specs=g_specs,
        out_shape=g_shape, compiler_params=_cparams(("arbitrary",)),
    )(*arrays, *cts, *[add[k] for k in add_idx])


def _accumulate(ref, val, first):
    @pl.when(first)
    def _():
        ref[...] = val.astype(ref.dtype)

    @pl.when(jnp.logical_not(first))
    def _():
        ref[...] += val.astype(ref.dtype)


def _rms(x, count=None):
    n = x.shape[-1] if count is None else count
    return x * lax.rsqrt(jnp.sum(x * x, axis=-1, keepdims=True) * (1.0 / n) + EPS)


def _swap_halves(x, nf):
    w = x.shape[-1]
    lane = lax.broadcasted_iota(jnp.int32, x.shape, x.ndim - 1)
    return jnp.where((lane % (2 * nf)) < nf, pltpu.roll(x, w - nf, x.ndim - 1), pltpu.roll(x, nf, x.ndim - 1))


def _make_rope(nf):
    @jax.custom_vjp
    def rope(x, c, s):
        return x * c + _swap_halves(x, nf) * s

    def fwd(x, c, s):
        return rope(x, c, s), (c, s)

    def bwd(res, g):
        c, s = res
        return g * c + _swap_halves(g * s, nf), jnp.zeros_like(c), jnp.zeros_like(s)

    rope.defvjp(fwd, bwd)
    return rope


_rope_swa = _make_rope(SWA_HEAD_DIM // 4)
_rope_mla = _make_rope(MLA_ROPE // 4)


@jax.custom_vjp
def _softplus(x):
    e = jnp.exp(-jnp.abs(x))
    u = 1.0 + e
    log1p_e = jnp.where(u == 1.0, e, jnp.log(u) * e / jnp.where(u == 1.0, 1.0, u - 1.0))
    return jnp.maximum(x, 0.0) + log1p_e


_softplus.defvjp(lambda x: (_softplus(x), x), lambda x, g: (g * jax.nn.sigmoid(x),))


def fn_norm_mod(x, g, shift, scale):
    return (_rms(x) * g * (1.0 + scale) + shift,)


def fn_rms(x, g):
    return (_rms(x) * g,)


def fn_resid(x, a, gate):
    return (x + gate * a,)


def fn_softplus(dt, bias):
    return (_softplus(dt + bias),)


def fn_ssd_out(yf, yb, xs, z, d_lane, g):
    y = yf + yb + d_lane * xs
    return (_rms(y * (z * jax.nn.sigmoid(z))) * g,)


def fn_swa_q(q, g, c, s):
    return (_rope_swa(_rms(q) * g, c, s),)


def fn_swa_kv(k, v, g, c, s):
    return (_rope_swa(_rms(k) * g, c, s), v)


def fn_mla_q(qn, qr, gn, gr, c, s):
    return (jnp.concatenate([_rms(qn) * gn, _rope_mla(_rms(qr, MLA_ROPE) * gr, c, s)], axis=-1),)


def fn_mla_kv(kn, v, kr, gn, gr, c, s):
    return (jnp.concatenate([_rms(kn) * gn, _rope_mla(_rms(kr, MLA_ROPE) * gr, c, s)], axis=-1), v)


def fn_merge(g1, g2, g3, p1, p2, p3):
    return (jax.nn.sigmoid(g1) * p1 + jax.nn.sigmoid(g2) * p2 + jax.nn.sigmoid(g3) * p3,)


def fn_swiglu(g, u):
    return (g * jax.nn.sigmoid(g) * u,)


ATTN_TILE = 256
NT_DIMS = (((1,), (1,)), ((), ()))


class AttnCfg:
    def __init__(self, hq, group, dq, dv, scale, window, has_sink, L, T, chunk, kv_block):
        self.hq, self.group, self.dq, self.dv, self.scale = hq, group, dq, dv, scale
        self.window, self.has_sink, self.L, self.T, self.chunk = window, has_sink, L, T, chunk
        self.kv_block = kv_block
        self.q_block = kv_block * group
        assert L % ATTN_TILE == 0 and (T - L) % ATTN_TILE == 0 and L % chunk == 0 and (T - L) % chunk == 0
        assert window is None or (window % chunk == 0 and ATTN_TILE % chunk == 0)
        assert (hq // group) % kv_block == 0


LOG2E = math.log2(math.e)


def _key_chunks(cfg, q0, tq):
    c = cfg.chunk
    is_ctx = q0 >= cfg.L
    ctx_n = (cfg.T - cfg.L) // c
    if cfg.window is None:
        lat_lo = 0
        lat_n = jnp.where(is_ctx, 0, cfg.L // c)
    else:
        lat_lo = jnp.maximum(q0 - cfg.window, 0)
        lat_n = jnp.where(is_ctx, 0, (jnp.minimum(q0 + tq + cfg.window, cfg.L) - lat_lo) // c)
    n = ctx_n + lat_n

    def start(t):
        t = jnp.minimum(t, n - 1)
        return pl.multiple_of(jnp.where(t < ctx_n, cfg.L + t * c, lat_lo + (t - ctx_n) * c), c)

    return n, start


def _visible(cfg, rows_q, rows_k):
    return jnp.logical_or(rows_k >= cfg.L, jnp.abs(rows_k - rows_q) <= cfg.window)


def flash_fwd(name, cfg, q, k, v, sink):
    T, tq, c = cfg.T, ATTN_TILE, cfg.chunk
    hq, g, dq, dv, hb, kb = cfg.hq, cfg.group, cfg.dq, cfg.dv, cfg.q_block, cfg.kv_block
    to_log2 = cfg.scale * LOG2E

    def body(*refs):
        if cfg.has_sink:
            q_ref, k_ref, v_ref, sink_ref, o_ref, lse_ref = refs
        else:
            q_ref, k_ref, v_ref, o_ref, lse_ref = refs
        q0 = pl.program_id(1) * tq
        qs = [q_ref[:, hh * dq:(hh + 1) * dq] for hh in range(hb)]
        n, start = _key_chunks(cfg, q0, tq)
        rows_q = q0 + lax.broadcasted_iota(jnp.int32, (tq, 1), 0)

        def keys(t, hh):
            kh = hh // g
            return k_ref[pl.ds(start(t), c), kh * dq:(kh + 1) * dq]

        def logits(t):
            return tuple(lax.dot_general(qs[hh], keys(t, hh), NT_DIMS, preferred_element_type=F32) for hh in range(hb))

        def step(t, carry):
            state, s_all = carry
            s_next = logits(t + 1)
            ks = start(t)
            rows_k = ks + lax.broadcasted_iota(jnp.int32, (1, c), 1)
            new_state = []
            for hh in range(hb):
                m, l, acc = state[hh]
                s = s_all[hh]
                if cfg.window is not None:
                    s = jnp.where(_visible(cfg, rows_q, rows_k), s, -jnp.inf)
                m_new = jnp.maximum(m, jnp.max(s, axis=-1, keepdims=True) * to_log2)
                alpha = jnp.exp2(m - m_new)
                p = jnp.exp2(s * to_log2 - m_new)
                l = alpha * l + jnp.sum(p, axis=-1, keepdims=True)
                kh = hh // g
                acc = alpha * acc + jnp.dot(p.astype(BF16), v_ref[pl.ds(ks, c), kh * dv:(kh + 1) * dv], preferred_element_type=F32)
                new_state.append((m_new, l, acc))
            return tuple(new_state), s_next

        state0 = []
        for hh in range(hb):
            if cfg.has_sink:
                m0 = jnp.zeros((tq, 1), F32) + sink_ref[hh] * LOG2E
                l0 = jnp.ones((tq, 1), F32)
            else:
                m0 = jnp.full((tq, 1), -jnp.inf, F32)
                l0 = jnp.zeros((tq, 1), F32)
            state0.append((m0, l0, jnp.zeros((tq, dv), F32)))
        state, _ = lax.fori_loop(0, n, step, (tuple(state0), logits(0)))
        for hh in range(hb):
            m, l, acc = state[hh]
            o_ref[:, hh * dv:(hh + 1) * dv] = acc / l
            lse_ref[hh] = m + jnp.log2(l)

    in_specs = [pl.BlockSpec((tq, hb * dq), lambda h, i: (i, h)),
                pl.BlockSpec((T, kb * dq), lambda h, i: (0, h)),
                pl.BlockSpec((T, kb * dv), lambda h, i: (0, h))]
    ins = [q, k, v]
    if cfg.has_sink:
        in_specs.append(pl.BlockSpec((hb, 1, 1), lambda h, i: (h, 0, 0)))
        ins.append(sink)
    return pl.pallas_call(
        body, name=name, grid=(hq // hb, T // tq), in_specs=in_specs,
        out_specs=[pl.BlockSpec((tq, hb * dv), lambda h, i: (i, h)), pl.BlockSpec((hb, tq, 1), lambda h, i: (h, i, 0))],
        out_shape=[jax.ShapeDtypeStruct((T, hq * dv), F32), jax.ShapeDtypeStruct((hq, T, 1), F32)],
        compiler_params=_cparams(("parallel", "parallel")),
    )(*ins)


def attn_delta(name, cfg, o, do, lse, sink):
    T, tm, hq, dv = cfg.T, ATTN_TILE, cfg.hq, cfg.dv

    def body(*refs):
        if cfg.has_sink:
            o_ref, do_ref, lse_ref, sink_ref, delta_ref, dob_ref, dsink_ref = refs
        else:
            o_ref, do_ref, delta_ref, dob_ref = refs
        dob_ref[...] = do_ref[...].astype(BF16)
        parts = []
        for h in range(hq):
            delta = jnp.sum(do_ref[:, h * dv:(h + 1) * dv] * o_ref[:, h * dv:(h + 1) * dv], axis=-1, keepdims=True)
            delta_ref[h] = delta
            if cfg.has_sink:
                parts.append(-jnp.sum(jnp.exp2(sink_ref[h] * LOG2E - lse_ref[h]) * delta, axis=0, keepdims=True)[None])
        if cfg.has_sink:
            _accumulate(dsink_ref, jnp.concatenate(parts, axis=0), pl.program_id(0) == 0)

    head_tile = pl.BlockSpec((tm, hq * dv), lambda i: (i, 0))
    col = pl.BlockSpec((hq, tm, 1), lambda i: (0, i, 0))
    one = pl.BlockSpec((hq, 1, 1), lambda i: (0, 0, 0))
    in_specs, ins = [head_tile, head_tile], [o, do]
    out_specs = [col, head_tile]
    out_shape = [jax.ShapeDtypeStruct((hq, T, 1), F32), jax.ShapeDtypeStruct((T, hq * dv), BF16)]
    if cfg.has_sink:
        in_specs += [col, one]
        ins += [lse, sink]
        out_specs.append(one)
        out_shape.append(jax.ShapeDtypeStruct((hq, 1, 1), F32))
    return pl.pallas_call(body, name=name, grid=(T // tm,), in_specs=in_specs, out_specs=out_specs, out_shape=out_shape,
                          compiler_params=_cparams(("arbitrary",)))(*ins)


def flash_dq(name, cfg, q, k, v, dob, lse, delta):
    T, tq, c = cfg.T, ATTN_TILE, cfg.chunk
    hq, g, dq, dv, hb, kb = cfg.hq, cfg.group, cfg.dq, cfg.dv, cfg.q_block, cfg.kv_block

    def body(q_ref, k_ref, v_ref, do_ref, lse_ref, delta_ref, dq_ref):
        q0 = pl.program_id(1) * tq
        qs = [q_ref[:, hh * dq:(hh + 1) * dq] for hh in range(hb)]
        dos = [do_ref[:, hh * dv:(hh + 1) * dv] for hh in range(hb)]
        n, start = _key_chunks(cfg, q0, tq)
        rows_q = q0 + lax.broadcasted_iota(jnp.int32, (tq, 1), 0)
        to_log2 = cfg.scale * LOG2E

        def keys(t, hh):
            kh = hh // g
            return k_ref[pl.ds(start(t), c), kh * dq:(kh + 1) * dq]

        def products(t):
            out = []
            for hh in range(hb):
                kh = hh // g
                out.append((lax.dot_general(qs[hh], keys(t, hh), NT_DIMS, preferred_element_type=F32),
                            lax.dot_general(dos[hh], v_ref[pl.ds(start(t), c), kh * dv:(kh + 1) * dv], NT_DIMS,
                                            preferred_element_type=F32)))
            return tuple(out)

        def step(t, carry):
            accs, prods = carry
            nxt = products(t + 1)
            rows_k = start(t) + lax.broadcasted_iota(jnp.int32, (1, c), 1)
            new_accs = []
            for hh in range(hb):
                s, dp = prods[hh]
                p = jnp.exp2(s * to_log2 - lse_ref[hh])
                if cfg.window is not None:
                    p = jnp.where(_visible(cfg, rows_q, rows_k), p, 0.0)
                ds = (p * (dp - delta_ref[hh])).astype(BF16)
                new_accs.append(accs[hh] + jnp.dot(ds, keys(t, hh), preferred_element_type=F32))
            return tuple(new_accs), nxt

        accs, _ = lax.fori_loop(0, n, step, (tuple(jnp.zeros((tq, dq), F32) for _ in range(hb)), products(0)))
        for hh in range(hb):
            dq_ref[:, hh * dq:(hh + 1) * dq] = accs[hh] * cfg.scale

    col = pl.BlockSpec((hb, tq, 1), lambda h, i: (h, i, 0))
    return pl.pallas_call(
        body, name=name, grid=(hq // hb, T // tq),
        in_specs=[pl.BlockSpec((tq, hb * dq), lambda h, i: (i, h)),
                  pl.BlockSpec((T, kb * dq), lambda h, i: (0, h)),
                  pl.BlockSpec((T, kb * dv), lambda h, i: (0, h)),
                  pl.BlockSpec((tq, hb * dv), lambda h, i: (i, h)), col, col],
        out_specs=pl.BlockSpec((tq, hb * dq), lambda h, i: (i, h)),
        out_shape=jax.ShapeDtypeStruct((T, hq * dq), F32),
        compiler_params=_cparams(("parallel", "parallel")),
    )(q, k, v, dob, lse, delta)


def flash_dkv(name, cfg, q, k, v, dob, lse_t, delta_t):
    T, tk, c = cfg.T, ATTN_TILE, cfg.chunk
    hq, g, dq, dv, hb, kb = cfg.hq, cfg.group, cfg.dq, cfg.dv, cfg.q_block, cfg.kv_block
    hk = hq // g

    def body(k_ref, v_ref, q_ref, do_ref, lse_ref, delta_ref, dk_ref, dv_ref):
        k0 = pl.program_id(1) * tk
        kk = [k_ref[:, kh * dq:(kh + 1) * dq] for kh in range(kb)]
        vv = [v_ref[:, kh * dv:(kh + 1) * dv] for kh in range(kb)]
        is_ctx = k0 >= cfg.L
        rows_k = k0 + lax.broadcasted_iota(jnp.int32, (tk, 1), 0)
        to_log2 = cfg.scale * LOG2E
        if cfg.window is None:
            lat_lo = 0
            lat_n = cfg.L // c
        else:
            lo = jnp.maximum(k0 - cfg.window, 0)
            lat_lo = jnp.where(is_ctx, 0, lo)
            lat_n = jnp.where(is_ctx, cfg.L, jnp.minimum(k0 + tk + cfg.window, cfg.L) - lo) // c
        n = lat_n + jnp.where(is_ctx, (cfg.T - cfg.L) // c, 0)

        def start(t):
            t = jnp.minimum(t, n - 1)
            return pl.multiple_of(jnp.where(t < lat_n, lat_lo + t * c, cfg.L + (t - lat_n) * c), c)

        def operands(t, hh):
            qs = start(t)
            return q_ref[pl.ds(qs, c), hh * dq:(hh + 1) * dq], do_ref[pl.ds(qs, c), hh * dv:(hh + 1) * dv]

        def products(t):
            out = []
            for hh in range(hb):
                kh = hh // g
                qc, dc = operands(t, hh)
                out.append((lax.dot_general(kk[kh], qc, NT_DIMS, preferred_element_type=F32),
                            lax.dot_general(vv[kh], dc, NT_DIMS, preferred_element_type=F32)))
            return tuple(out)

        def step(t, carry):
            accs, prods = carry
            nxt = products(t + 1)
            qs = start(t)
            ci = qs // c
            rows_q = qs + lax.broadcasted_iota(jnp.int32, (1, c), 1)
            accs = [list(a) for a in accs]
            for hh in range(hb):
                kh = hh // g
                st, dpt = prods[hh]
                qc, dc = operands(t, hh)
                pt = jnp.exp2(st * to_log2 - lse_ref[hh, ci])
                if cfg.window is not None:
                    pt = jnp.where(_visible(cfg, rows_q, rows_k), pt, 0.0)
                accs[kh][1] = accs[kh][1] + jnp.dot(pt.astype(BF16), dc, preferred_element_type=F32)
                dst = (pt * (dpt - delta_ref[hh, ci])).astype(BF16)
                accs[kh][0] = accs[kh][0] + jnp.dot(dst, qc, preferred_element_type=F32)
            return tuple(tuple(a) for a in accs), nxt

        accs0 = tuple((jnp.zeros((tk, dq), F32), jnp.zeros((tk, dv), F32)) for _ in range(kb))
        accs, _ = lax.fori_loop(0, n, step, (accs0, products(0)))
        for kh in range(kb):
            dk_ref[:, kh * dq:(kh + 1) * dq] = accs[kh][0] * cfg.scale
            dv_ref[:, kh * dv:(kh + 1) * dv] = accs[kh][1]

    nch = T // c
    return pl.pallas_call(
        body, name=name, grid=(hk // kb, T // tk),
        in_specs=[pl.BlockSpec((tk, kb * dq), lambda h, j: (j, h)),
                  pl.BlockSpec((tk, kb * dv), lambda h, j: (j, h)),
                  pl.BlockSpec((T, hb * dq), lambda h, j: (0, h)),
                  pl.BlockSpec((T, hb * dv), lambda h, j: (0, h)),
                  pl.BlockSpec((hb, nch, 1, c), lambda h, j: (h, 0, 0, 0)),
                  pl.BlockSpec((hb, nch, 1, c), lambda h, j: (h, 0, 0, 0))],
        out_specs=[pl.BlockSpec((tk, kb * dq), lambda h, j: (j, h)), pl.BlockSpec((tk, kb * dv), lambda h, j: (j, h))],
        out_shape=[jax.ShapeDtypeStruct((T, hk * dq), F32), jax.ShapeDtypeStruct((T, hk * dv), F32)],
        compiler_params=_cparams(("parallel", "parallel")),
    )(k, v, q, dob, lse_t, delta_t)


HALO = 8


def _conv_specs(tm, C, T):
    nb = tm // HALO
    last = T // HALO - 1
    return [pl.BlockSpec((HALO, C), lambda i: (jnp.maximum(i * nb - 1, 0), 0)),
            pl.BlockSpec((tm, C), lambda i: (i, 0)),
            pl.BlockSpec((HALO, C), lambda i: (jnp.minimum((i + 1) * nb, last), 0))]


def _extended(prev_ref, cur_ref, next_ref, i, tm, L, T):
    r0 = i * tm
    keep_prev = jnp.logical_and(r0 != 0, r0 != L).astype(F32)
    keep_next = jnp.logical_and(r0 + tm != L, r0 + tm != T).astype(F32)
    return jnp.concatenate([prev_ref[...] * keep_prev, cur_ref[...], next_ref[...] * keep_next], axis=0)


def _shift_rows(xe, d):
    n = xe.shape[0]
    return xe if d == 0 else pltpu.roll(xe, (-d) % n, 0)


def _conv_pre(xe, w_ref, b_ref):
    acc = b_ref[...] + w_ref[SSM_CONV // 2:SSM_CONV // 2 + 1, :] * xe
    for k in range(SSM_CONV):
        if k != SSM_CONV // 2:
            acc = acc + w_ref[k:k + 1, :] * _shift_rows(xe, k - SSM_CONV // 2)
    return acc


def conv_fwd(name, x, w, b, L, tm=ROW_TILE):
    T, C = x.shape

    def body(xp, xc, xn, w_ref, b_ref, o_ref):
        xe = _extended(xp, xc, xn, pl.program_id(0), tm, L, T)
        pre = _conv_pre(xe, w_ref, b_ref)[HALO:HALO + tm]
        o_ref[...] = pre * jax.nn.sigmoid(pre)

    full = lambda a: pl.BlockSpec(a.shape, lambda i: (0, 0))
    return pl.pallas_call(body, name=name, grid=(T // tm,), in_specs=_conv_specs(tm, C, T) + [full(w), full(b)],
                          out_specs=pl.BlockSpec((tm, C), lambda i: (i, 0)), out_shape=jax.ShapeDtypeStruct((T, C), F32),
                          compiler_params=_cparams(("parallel",)))(x, x, x, w, b)


def conv_bwd(name, x, w, b, gu, L, tm=ROW_TILE):
    T, C = x.shape

    def body(xp, xc, xn, gp, gc, gn, w_ref, b_ref, dx_ref, dw_ref, db_ref):
        i = pl.program_id(0)
        xe = _extended(xp, xc, xn, i, tm, L, T)
        ge = _extended(gp, gc, gn, i, tm, L, T)
        pre = _conv_pre(xe, w_ref, b_ref)
        sg = jax.nn.sigmoid(pre)
        gpre = ge * (sg * (1.0 + pre * (1.0 - sg)))
        half = SSM_CONV // 2
        dx = jnp.zeros((tm, C), F32)
        rows = []
        for k in range(SSM_CONV):
            dx = dx + w_ref[k:k + 1, :] * _shift_rows(gpre, half - k)[HALO:HALO + tm]
            rows.append(jnp.sum(gpre[HALO:HALO + tm] * _shift_rows(xe, k - half)[HALO:HALO + tm], axis=0, keepdims=True))
        dx_ref[...] = dx
        rows += [jnp.zeros((1, C), F32)] * (8 - SSM_CONV)
        _accumulate(dw_ref, jnp.concatenate(rows, axis=0), i == 0)
        _accumulate(db_ref, jnp.sum(gpre[HALO:HALO + tm], axis=0, keepdims=True), i == 0)

    full = lambda a: pl.BlockSpec(a.shape, lambda i: (0, 0))
    return pl.pallas_call(
        body, name=name, grid=(T // tm,), in_specs=_conv_specs(tm, C, T) * 2 + [full(w), full(b)],
        out_specs=[pl.BlockSpec((tm, C), lambda i: (i, 0)), pl.BlockSpec((8, C), lambda i: (0, 0)), pl.BlockSpec((1, C), lambda i: (0, 0))],
        out_shape=[jax.ShapeDtypeStruct((T, C), F32), jax.ShapeDtypeStruct((8, C), F32), jax.ShapeDtypeStruct((1, C), F32)],
        compiler_params=_cparams(("arbitrary",)))(x, x, x, gu, gu, gu, w, b)


SSM_PAIRS = SSM_HEADS // 2
TN_DIMS = (((0,), (0,)), ((), ()))
HIGHEST = lax.Precision.HIGHEST


def _ssd_chunk(direction, xps, bs, cs, dt_col, dt_row, alog_row, alog_col, hps):
    Q = SSM_CHUNK
    da_col = dt_col * (-jnp.exp(alog_row))
    da_row = dt_row * (-jnp.exp(alog_col))
    ii = lax.broadcasted_iota(jnp.int32, (Q, Q), 0)
    jj = lax.broadcasted_iota(jnp.int32, (Q, Q), 1)
    tri = (ii >= jj) if direction == 0 else (ii <= jj)
    trif = tri.astype(F32)
    acs_col = jnp.dot(trif, da_col, precision=HIGHEST, preferred_element_type=F32)
    acs_row = lax.dot_general(da_row, trif, NT_DIMS, precision=HIGHEST, preferred_element_type=F32)
    tot_col = jnp.sum(da_col, axis=0, keepdims=True)
    lane16 = lax.broadcasted_iota(jnp.int32, (1, SSM_HEADS), 1)
    sub16 = lax.broadcasted_iota(jnp.int32, (SSM_HEADS, 1), 0)
    low = lax.broadcasted_iota(jnp.int32, (1, 2 * SSM_HEAD_DIM), 1) < SSM_HEAD_DIM

    def col(v, h):
        return jnp.sum(v * (lane16 == h).astype(F32), axis=1, keepdims=True)

    def row(v, h):
        return jnp.sum(v * (sub16 == h).astype(F32), axis=0, keepdims=True)

    ys, hos = [], []
    pairs_per_group = SSM_PAIRS // SSM_GROUPS
    for g in range(SSM_GROUPS):
        bb, cb16 = bs[g].astype(BF16), cs[g].astype(BF16)
        cb = lax.dot_general(cb16, bb, NT_DIMS, preferred_element_type=F32)
        for pp in range(pairs_per_group):
            p = g * pairs_per_group + pp
            h0, h1 = 2 * p, 2 * p + 1
            ac0, ac1 = col(acs_col, h0), col(acs_col, h1)
            seg0 = jnp.exp(jnp.where(tri, ac0 - row(acs_row, h0), -jnp.inf))
            seg1 = jnp.exp(jnp.where(tri, ac1 - row(acs_row, h1), -jnp.inf))
            dt_l = jnp.where(low, col(dt_col, h0), col(dt_col, h1))
            ac_l = jnp.where(low, ac0, ac1)
            tot_l = jnp.where(low, col(tot_col, h0), col(tot_col, h1))
            xdt = xps[p] * dt_l
            y = (jnp.dot((cb * seg0).astype(BF16), jnp.where(low, xdt, 0.0).astype(BF16), preferred_element_type=F32)
                 + jnp.dot((cb * seg1).astype(BF16), jnp.where(low, 0.0, xdt).astype(BF16), preferred_element_type=F32))
            y = y + jnp.dot(cb16, hps[p].astype(BF16), preferred_element_type=F32) * jnp.exp(ac_l)
            st = lax.dot_general(bb, (xdt * jnp.exp(tot_l - ac_l)).astype(BF16), TN_DIMS, preferred_element_type=F32)
            ys.append(y)
            hos.append(hps[p] * jnp.exp(tot_l) + st)
    return tuple(ys), tuple(hos)


def _ssd_chunk_of(direction, step, ncl, ncc):
    if direction == 0:
        return jnp.where(step < ncc, ncl + step, step - ncc)
    return jnp.where(step < ncc, ncl + ncc - 1 - step, ncl - 1 - (step - ncc))


def _ssd_load(u_ref):
    Q = SSM_CHUNK
    xps = tuple(u_ref[:, LANES * p:LANES * (p + 1)] for p in range(SSM_PAIRS))
    bs = tuple(u_ref[:, SSM_INNER + SSM_STATE * g:SSM_INNER + SSM_STATE * (g + 1)] for g in range(SSM_GROUPS))
    c0 = SSM_INNER + SSM_GROUPS * SSM_STATE
    cs = tuple(u_ref[:, c0 + SSM_STATE * g:c0 + SSM_STATE * (g + 1)] for g in range(SSM_GROUPS))
    return xps, bs, cs


def ssd_fwd(name, direction, u, dt, dt_t, alog_row, alog_col, L):
    T = u.shape[0]
    Q, N = SSM_CHUNK, SSM_STATE
    ncl, ncc = L // Q, (T - L) // Q
    nc = ncl + ncc
    cm = lambda s: _ssd_chunk_of(direction, s, ncl, ncc)

    def body(u_ref, dt_ref, dtt_ref, ar_ref, ac_ref, y_ref, hin_ref, state):
        @pl.when(pl.program_id(0) == 0)
        def _():
            state[...] = jnp.zeros_like(state)

        xps, bs, cs = _ssd_load(u_ref)
        hps = tuple(state[p] for p in range(SSM_PAIRS))
        for p in range(SSM_PAIRS):
            hin_ref[0, p] = hps[p]
        ys, hos = _ssd_chunk(direction, xps, bs, cs, dt_ref[...], dtt_ref[...], ar_ref[...], ac_ref[...], hps)
        for p in range(SSM_PAIRS):
            y_ref[:, LANES * p:LANES * (p + 1)] = ys[p]
            state[p] = hos[p]

    return pl.pallas_call(
        body, name=name, grid=(nc,),
        in_specs=[pl.BlockSpec((Q, SSM_CONV_DIM), lambda s: (cm(s), 0)),
                  pl.BlockSpec((Q, SSM_HEADS), lambda s: (cm(s), 0)),
                  pl.BlockSpec((SSM_HEADS, Q), lambda s: (0, cm(s))),
                  pl.BlockSpec((1, SSM_HEADS), lambda s: (0, 0)),
                  pl.BlockSpec((SSM_HEADS, 1), lambda s: (0, 0))],
        out_specs=[pl.BlockSpec((Q, SSM_INNER), lambda s: (cm(s), 0)),
                   pl.BlockSpec((1, SSM_PAIRS, N, LANES), lambda s: (cm(s), 0, 0, 0))],
        out_shape=[jax.ShapeDtypeStruct((T, SSM_INNER), F32), jax.ShapeDtypeStruct((nc, SSM_PAIRS, N, LANES), F32)],
        scratch_shapes=[pltpu.VMEM((SSM_PAIRS, N, LANES), F32)],
        compiler_params=_cparams(("arbitrary",)),
    )(u, dt, dt_t, alog_row, alog_col)


def ssd_bwd(name, direction, u, dt, dt_t, alog_row, alog_col, hin, dy, L, add_x=None, add_u=None):
    T = u.shape[0]
    Q, N = SSM_CHUNK, SSM_STATE
    ncl, ncc = L // Q, (T - L) // Q
    nc = ncl + ncc
    cm = lambda s: _ssd_chunk_of(direction, nc - 1 - s, ncl, ncc)
    n_add = (add_x is not None) + (add_u is not None)

    def body(*refs):
        u_ref, dt_ref, dtt_ref, ar_ref, ac_ref, hin_ref, dy_ref = refs[:7]
        add_refs = refs[7:7 + n_add]
        du_ref, ddt_ref, ddtt_ref, dar_ref, dac_ref, dstate = refs[7 + n_add:]
        first = pl.program_id(0) == 0

        @pl.when(first)
        def _():
            dstate[...] = jnp.zeros_like(dstate)

        xps, bs, cs = _ssd_load(u_ref)
        hps = tuple(hin_ref[0, p] for p in range(SSM_PAIRS))
        _, vjp = jax.vjp(functools.partial(_ssd_chunk, direction), xps, bs, cs, dt_ref[...], dtt_ref[...], ar_ref[...],
                         ac_ref[...], hps)
        dys = tuple(dy_ref[:, LANES * p:LANES * (p + 1)] for p in range(SSM_PAIRS))
        dhs = tuple(dstate[p] for p in range(SSM_PAIRS))
        gx, gb, gc, gdt, gdtt, gar, gac, ghp = vjp((dys, dhs))
        parts = list(gx) + list(gb) + list(gc)
        du = jnp.concatenate(parts, axis=1)
        k = 0
        if add_x is not None:
            du = du + jnp.concatenate([add_refs[k][...], jnp.zeros((Q, SSM_CONV_DIM - SSM_INNER), F32)], axis=1)
            k += 1
        if add_u is not None:
            du = du + add_refs[k][...]
        du_ref[...] = du
        ddt_ref[...] = gdt
        ddtt_ref[...] = gdtt
        _accumulate(dar_ref, gar, first)
        _accumulate(dac_ref, gac, first)
        for p in range(SSM_PAIRS):
            dstate[p] = ghp[p]

    in_specs = [pl.BlockSpec((Q, SSM_CONV_DIM), lambda s: (cm(s), 0)),
                pl.BlockSpec((Q, SSM_HEADS), lambda s: (cm(s), 0)),
                pl.BlockSpec((SSM_HEADS, Q), lambda s: (0, cm(s))),
                pl.BlockSpec((1, SSM_HEADS), lambda s: (0, 0)),
                pl.BlockSpec((SSM_HEADS, 1), lambda s: (0, 0)),
                pl.BlockSpec((1, SSM_PAIRS, N, LANES), lambda s: (cm(s), 0, 0, 0)),
                pl.BlockSpec((Q, SSM_INNER), lambda s: (cm(s), 0))]
    ins = [u, dt, dt_t, alog_row, alog_col, hin, dy]
    if add_x is not None:
        in_specs.append(pl.BlockSpec((Q, SSM_INNER), lambda s: (cm(s), 0)))
        ins.append(add_x)
    if add_u is not None:
        in_specs.append(pl.BlockSpec((Q, SSM_CONV_DIM), lambda s: (cm(s), 0)))
        ins.append(add_u)
    return pl.pallas_call(
        body, name=name, grid=(nc,), in_specs=in_specs,
        out_specs=[pl.BlockSpec((Q, SSM_CONV_DIM), lambda s: (cm(s), 0)),
                   pl.BlockSpec((Q, SSM_HEADS), lambda s: (cm(s), 0)),
                   pl.BlockSpec((SSM_HEADS, Q), lambda s: (0, cm(s))),
                   pl.BlockSpec((1, SSM_HEADS), lambda s: (0, 0)),
                   pl.BlockSpec((SSM_HEADS, 1), lambda s: (0, 0))],
        out_shape=[jax.ShapeDtypeStruct((T, SSM_CONV_DIM), F32), jax.ShapeDtypeStruct((T, SSM_HEADS), F32),
                   jax.ShapeDtypeStruct((SSM_HEADS, T), F32), jax.ShapeDtypeStruct((1, SSM_HEADS), F32),
                   jax.ShapeDtypeStruct((SSM_HEADS, 1), F32)],
        scratch_shapes=[pltpu.VMEM((SSM_PAIRS, N, LANES), F32)],
        compiler_params=_cparams(("arbitrary",)),
    )(*ins)


PEER_MASKS = (1, 2, 4, 3, 5, 6, 7)
N_PEERS = len(PEER_MASKS)
MESH_IDS = pl.DeviceIdType.MESH


def _my_index():
    return lax.axis_index("x") * 4 + lax.axis_index("y") * 2 + lax.axis_index("c")


def _coords(idx):
    return (idx // 4, (idx // 2) % 2, idx % 2)


def all_gather_hbm(name, arrays):
    n = len(arrays)

    def body(*refs):
        ins, outs = refs[:n], refs[n:2 * n]
        send_sems, recv_sems, local_sems = refs[2 * n:]
        me = _my_index()
        copies = []
        for a in range(n):
            local = pltpu.make_async_copy(ins[a], outs[a].at[me], local_sems.at[a])
            local.start()
            copies.append(local)
            for k, mask in enumerate(PEER_MASKS):
                peer = me ^ mask
                cp = pltpu.make_async_remote_copy(src_ref=ins[a], dst_ref=outs[a].at[me], send_sem=send_sems.at[a * N_PEERS + k],
                                                  recv_sem=recv_sems.at[a * N_PEERS + k], device_id=_coords(peer),
                                                  device_id_type=MESH_IDS)
                cp.start()
                copies.append(cp)
        for cp in copies:
            cp.wait()

    any_spec = pl.BlockSpec(memory_space=pl.ANY)
    return pl.pallas_call(
        body, name=name, in_specs=[any_spec] * n, out_specs=[any_spec] * n,
        out_shape=[jax.ShapeDtypeStruct((N_DEV,) + a.shape, a.dtype) for a in arrays],
        scratch_shapes=[pltpu.SemaphoreType.DMA((n * N_PEERS,)), pltpu.SemaphoreType.DMA((n * N_PEERS,)),
                        pltpu.SemaphoreType.DMA((n,))],
    )(*arrays)


def exchange_hbm(name, arrays):
    n = len(arrays)

    def body(*refs):
        ins, outs = refs[:n], refs[n:2 * n]
        send_sems, recv_sems, local_sems = refs[2 * n:]
        me = _my_index()
        copies = []
        for a in range(n):
            local = pltpu.make_async_copy(ins[a].at[me], outs[a].at[me], local_sems.at[a])
            local.start()
            copies.append(local)
            for k, mask in enumerate(PEER_MASKS):
                peer = me ^ mask
                cp = pltpu.make_async_remote_copy(src_ref=ins[a].at[peer], dst_ref=outs[a].at[me],
                                                  send_sem=send_sems.at[a * N_PEERS + k], recv_sem=recv_sems.at[a * N_PEERS + k],
                                                  device_id=_coords(peer), device_id_type=MESH_IDS)
                cp.start()
                copies.append(cp)
        for cp in copies:
            cp.wait()

    any_spec = pl.BlockSpec(memory_space=pl.ANY)
    return pl.pallas_call(
        body, name=name, in_specs=[any_spec] * n, out_specs=[any_spec] * n,
        out_shape=[jax.ShapeDtypeStruct(a.shape, a.dtype) for a in arrays],
        scratch_shapes=[pltpu.SemaphoreType.DMA((n * N_PEERS,)), pltpu.SemaphoreType.DMA((n * N_PEERS,)),
                        pltpu.SemaphoreType.DMA((n,))],
    )(*arrays)


def all_gather_vmem(name, v):
    def body(v_ref, out_ref, send_sems, recv_sems):
        me = _my_index()
        out_ref[me] = v_ref[...]
        copies = []
        for k, mask in enumerate(PEER_MASKS):
            cp = pltpu.make_async_remote_copy(src_ref=v_ref, dst_ref=out_ref.at[me], send_sem=send_sems.at[k],
                                              recv_sem=recv_sems.at[k], device_id=_coords(me ^ mask), device_id_type=MESH_IDS)
            cp.start()
            copies.append(cp)
        for cp in copies:
            cp.wait()

    vm = pl.BlockSpec(memory_space=pltpu.VMEM)
    return pl.pallas_call(
        body, name=name, in_specs=[vm], out_specs=vm, out_shape=jax.ShapeDtypeStruct((N_DEV,) + v.shape, v.dtype),
        scratch_shapes=[pltpu.SemaphoreType.DMA((N_PEERS,)), pltpu.SemaphoreType.DMA((N_PEERS,))],
    )(v)


def _row_tile(rows, cols, bufs):
    budget = 24 * 1024 * 1024 // (bufs * 2 * 4 * max(cols, LANES))
    if rows <= budget:
        return rows
    for t in range(budget - budget % 8, 7, -8):
        if rows % t == 0:
            return t
    return rows


def sum_parts(name, parts):
    P, R, C = parts.shape
    tr = _row_tile(R, C, P + 1)

    def body(p_ref, o_ref):
        acc = p_ref[0]
        for s in range(1, P):
            acc = acc + p_ref[s]
        o_ref[...] = acc

    return pl.pallas_call(body, name=name, grid=(R // tr,), in_specs=[pl.BlockSpec((P, tr, C), lambda i: (0, i, 0))],
                          out_specs=pl.BlockSpec((tr, C), lambda i: (i, 0)), out_shape=jax.ShapeDtypeStruct((R, C), F32),
                          compiler_params=_cparams(("parallel",)))(parts)


def adamw(name, w, g, m, v):
    R, C = w.shape
    tr = _row_tile(R, C, 7)

    def body(w_ref, g_ref, m_ref, v_ref, d_ref, nm_ref, nv_ref):
        g = g_ref[...]
        nm = ADAM_B1 * m_ref[...] + (1.0 - ADAM_B1) * g
        nv = ADAM_B2 * v_ref[...] + (1.0 - ADAM_B2) * (g * g)
        m_hat = nm / (1.0 - ADAM_B1 ** ADAM_STEP)
        v_hat = nv / (1.0 - ADAM_B2 ** ADAM_STEP)
        d_ref[...] = -ADAM_LR * (m_hat / (jnp.sqrt(v_hat) + ADAM_EPS) + ADAM_WD * w_ref[...])
        nm_ref[...] = nm
        nv_ref[...] = nv

    spec = pl.BlockSpec((tr, C), lambda i: (i, 0))
    return pl.pallas_call(body, name=name, grid=(R // tr,), in_specs=[spec] * 4, out_specs=[spec] * 3,
                          out_shape=[jax.ShapeDtypeStruct((R, C), F32)] * 3, compiler_params=_cparams(("parallel",)))(w, g, m, v)


def loss_and_grad(name, x, target, L, tm=ROW_TILE):
    T, D = x.shape
    nlt = L // tm

    def body(x_ref, t_ref, loss_ref, dx_ref):
        i = pl.program_id(0)
        err = jnp.where(i < nlt, x_ref[...] - t_ref[...], 0.0)
        dx_ref[...] = err * (1.0 / D)
        part = 0.5 * jnp.sum(jnp.sum(err * err, axis=1, keepdims=True), axis=0, keepdims=True) * (1.0 / D)
        _accumulate(loss_ref, part, i == 0)

    return pl.pallas_call(
        body, name=name, grid=(T // tm,),
        in_specs=[pl.BlockSpec((tm, D), lambda i: (i, 0)), pl.BlockSpec((tm, D), lambda i: (jnp.minimum(i, nlt - 1), 0))],
        out_specs=[pl.BlockSpec((1, 1), lambda i: (0, 0)), pl.BlockSpec((tm, D), lambda i: (i, 0))],
        out_shape=[jax.ShapeDtypeStruct((1, 1), F32), jax.ShapeDtypeStruct((T, D), F32)],
        compiler_params=_cparams(("arbitrary",)))(x, target)


def small_fwd(name, fn, arrays, out_shapes):
    def body(*refs):
        res = fn(*[r[...] for r in refs[:len(arrays)]])
        for o_ref, r in zip(refs[len(arrays):], res):
            o_ref[...] = r

    return pl.pallas_call(body, name=name, out_shape=[jax.ShapeDtypeStruct(s, F32) for s in out_shapes])(*arrays)


def small_bwd(name, fn, arrays, cts):
    n = len(arrays)

    def body(*refs):
        _, vjp = jax.vjp(lambda *a: tuple(fn(*a)), *[r[...] for r in refs[:n]])
        grads = vjp(tuple(r[...] for r in refs[n:n + len(cts)]))
        for o_ref, g in zip(refs[n + len(cts):], grads):
            o_ref[...] = g

    return pl.pallas_call(body, name=name, out_shape=[jax.ShapeDtypeStruct(a.shape, F32) for a in arrays])(*arrays, *cts)


def fn_silu(x):
    return (x * jax.nn.sigmoid(x),)


FWD_NAMES = ["x", "c", "ctx", "c_ctx", "w_mod", "b_mod", "norm1_g", "norm2_g", "w_in", "ssm_conv_w", "ssm_conv_b",
             "ssm_dt_bias", "ssm_a_log", "ssm_d", "ssm_norm_g", "swa_q_norm_g", "swa_k_norm_g", "swa_sink", "mla_q_lat_g",
             "mla_kv_lat_g", "w_mla_uq", "w_mla_ukv", "mla_q_norm_g", "mla_k_norm_g", "w_p_ssm", "w_p_swa", "w_p_mla",
             "w_out", "w_ffn_in", "w_ffn_out"]
WEIGHT_NAMES = FWD_NAMES[3:]
GATHERED = ["w_in", "w_mla_uq", "w_mla_ukv", "w_p_ssm", "w_p_swa", "w_p_mla", "w_out", "w_ffn_in", "w_ffn_out"]
COLUMN_SHARDED = ("w_in", "w_mla_uq", "w_mla_ukv", "w_ffn_in")
REPLICATED = ["c_ctx", "b_mod", "norm1_g", "norm2_g", "ssm_conv_b", "ssm_dt_bias", "ssm_a_log", "ssm_d", "ssm_norm_g",
              "swa_q_norm_g", "swa_k_norm_g", "swa_sink", "mla_q_lat_g", "mla_kv_lat_g", "mla_q_norm_g", "mla_k_norm_g"]
IN_SEGS = [("xbc", SSM_CONV_DIM), ("dt", 2 * SSM_HEADS), ("ks", SWA_KV_HEADS * SWA_HEAD_DIM), ("vs", SWA_KV_HEADS * SWA_HEAD_DIM),
           ("ckv", MLA_KV_RANK), ("kr", MLA_ROPE), ("z", SSM_INNER), ("qs", SWA_Q_HEADS * SWA_HEAD_DIM), ("cq", MLA_Q_RANK),
           ("g1", None), ("g2", None), ("g3", None)]


def _pack(vectors, multiple):
    flat = jnp.concatenate([v.reshape(-1) for v in vectors])
    pad = (-flat.shape[0]) % multiple
    return jnp.pad(flat, (0, pad)).reshape(-1, LANES)


def _unpack(packed, shapes):
    flat, out, off = packed.reshape(-1), [], 0
    for s in shapes:
        n = int(np.prod(s))
        out.append(flat[off:off + n].reshape(s))
        off += n
    return out


def _rope_tables(L, T, rot_dim):
    nf = rot_dim // 4
    inv = jnp.power(ROPE_BASE, -jnp.arange(nf, dtype=F32) / nf)
    r, col = jnp.meshgrid(jnp.arange(L // GRID_W, dtype=F32), jnp.arange(GRID_W, dtype=F32), indexing="ij")
    ang = jnp.stack([r.reshape(-1)[:, None] * inv, col.reshape(-1)[:, None] * inv], axis=1)
    cos, sin = jnp.cos(ang), jnp.sin(ang)
    c = jnp.concatenate([cos[:, 0], cos[:, 0], cos[:, 1], cos[:, 1]], axis=1)
    s = jnp.concatenate([-sin[:, 0], sin[:, 0], -sin[:, 1], sin[:, 1]], axis=1)
    c = jnp.pad(c, ((0, T - L), (0, LANES - rot_dim)), constant_values=1.0)
    s = jnp.pad(s, ((0, T - L), (0, LANES - rot_dim)))
    return c, s


def _pad_rows(a, rows):
    return jnp.pad(a, ((0, rows - a.shape[0]), (0, 0)))


def _row(w, per_head=0, off=0, diff=True):
    return ("row", w, per_head, off, diff)


PAR, PAR_ND = ("par", True), ("par", False)
GRP = ("grp", True)


def kernel(*args):
    n_fwd, n_w = len(FWD_NAMES), len(WEIGHT_NAMES)
    inp = dict(zip(FWD_NAMES, args[:n_fwd]))
    loss_target = args[n_fwd]
    mom_m = dict(zip(WEIGHT_NAMES, args[n_fwd + 1:n_fwd + 1 + n_w]))
    mom_v = dict(zip(WEIGHT_NAMES, args[n_fwd + 1 + n_w:]))

    x, ctx = inp["x"][0], inp["ctx"][0]
    L, D = x.shape
    n_ctx = ctx.shape[0]
    T = L + n_ctx
    depth = inp["w_in"].shape[0]
    me = _my_index()
    in_widths = [w if w is not None else D for _, w in IN_SEGS]
    in_offs = np.concatenate([[0], np.cumsum(in_widths)]).tolist()
    ffn_h = inp["w_ffn_out"].shape[1] * N_DEV
    cfg_swa = AttnCfg(SWA_Q_HEADS, SWA_Q_HEADS // SWA_KV_HEADS, SWA_HEAD_DIM, SWA_HEAD_DIM, SWA_HEAD_DIM ** -0.5, SWA_WINDOW,
                      True, L, T, 128, 1)
    cfg_mla = AttnCfg(MLA_HEADS, 1, MLA_QK_PAD, MLA_V, MLA_QK ** -0.5, None, False, L, T, 256, 2)

    def rf(name, fn, descs, arrays, outs, heads=1):
        return rowop_fwd(name, fn, descs, arrays, outs, T, L, heads=heads)

    def rb(name, fn, descs, arrays, outs, cts, heads=1, add=None):
        return rowop_bwd(name, fn, descs, arrays, outs, cts, T, L, heads=heads, add=add)

    local = []
    for n in GATHERED:
        w = inp[n]
        local.append((jnp.swapaxes(w, 1, 2) if n in COLUMN_SHARDED else w).astype(BF16))
    gathered = dict(zip(GATHERED, all_gather_hbm("gather_weights", local)))

    def full(n, l):
        g = gathered[n][:, l]
        return g.reshape(g.shape[0] * g.shape[1], g.shape[2])

    def layer_weights(l):
        wt = {}
        w_in_t = full("w_in", l)
        for (sn, _), o, w in zip(IN_SEGS, in_offs, in_widths):
            seg = w_in_t[o:o + w]
            wt[sn] = _pad_rows(seg, LANES) if sn == "kr" else seg
        uq = full("w_mla_uq", l).reshape(MLA_HEADS, MLA_QK, MLA_Q_RANK)
        wt["uqn"] = uq[:, :MLA_NOPE].reshape(MLA_HEADS * MLA_NOPE, MLA_Q_RANK)
        wt["uqr"] = jnp.pad(uq[:, MLA_NOPE:], ((0, 0), (0, LANES - MLA_ROPE), (0, 0))).reshape(MLA_HEADS * LANES, MLA_Q_RANK)
        ukv = full("w_mla_ukv", l).reshape(MLA_HEADS, MLA_NOPE + MLA_V, MLA_KV_RANK)
        wt["uk"] = ukv[:, :MLA_NOPE].reshape(MLA_HEADS * MLA_NOPE, MLA_KV_RANK)
        wt["uv"] = ukv[:, MLA_NOPE:].reshape(MLA_HEADS * MLA_V, MLA_KV_RANK)
        for n in ("w_p_ssm", "w_p_swa", "w_p_mla", "w_out", "w_ffn_out"):
            wt[n] = full(n, l)
        ffn_in_t = full("w_ffn_in", l)
        wt["fg"], wt["fu"] = ffn_in_t[:ffn_h], ffn_in_t[ffn_h:]
        return wt

    def layer_params(l):
        p = {}
        for n in ("norm1_g", "norm2_g", "ssm_conv_b", "ssm_norm_g", "swa_q_norm_g", "swa_k_norm_g", "mla_q_lat_g", "mla_kv_lat_g"):
            p[n] = inp[n][l][None]
        p["dt_bias"] = inp["ssm_dt_bias"][l].reshape(1, 2 * SSM_HEADS)
        p["alog_row"] = [inp["ssm_a_log"][l][d][None] for d in range(2)]
        p["alog_col"] = [inp["ssm_a_log"][l][d][:, None] for d in range(2)]
        p["d_lane"] = jnp.repeat(inp["ssm_d"][l], SSM_HEAD_DIM)[None]
        p["sink"] = inp["swa_sink"][l].reshape(SWA_Q_HEADS, 1, 1)
        for n, key in (("mla_q_norm_g", "gq"), ("mla_k_norm_g", "gk")):
            g = inp[n][l]
            p[key + "n"] = g[:MLA_NOPE][None]
            p[key + "r"] = jnp.pad(g[MLA_NOPE:], (0, LANES - MLA_ROPE))[None]
        return p

    conv_local = _pack([inp["ssm_conv_w"]], 8 * LANES)
    conv_all = all_gather_vmem("gather_conv_w", conv_local)
    cw = inp["ssm_conv_w"].shape
    conv_full = conv_all.reshape(N_DEV, -1)[:, :cw[0] * cw[1] * cw[2]].reshape(N_DEV, cw[0], cw[1], cw[2])
    conv_full = jnp.moveaxis(conv_full, 0, 2).reshape(cw[0], cw[1], N_DEV * cw[2])
    conv_w8 = jnp.pad(conv_full, ((0, 0), (0, 8 - cw[1]), (0, 0)))

    silu_c, silu_cc = small_fwd("silu_c", lambda a, b: fn_silu(a) + fn_silu(b), [inp["c"], inp["c_ctx"][None]], [(1, D), (1, D)])
    silu_all = all_gather_vmem("gather_silu_c", silu_c.reshape(D // LANES, LANES)).reshape(N_DEV, D)
    S_rows = 2 * N_DEV
    S_mat = jnp.concatenate([silu_all, silu_cc, jnp.zeros((S_rows - N_DEV - 1, D), F32)], axis=0)
    mod_cols = inp["w_mod"].shape[2]
    mods_local = []
    for l in range(depth):
        bias = lax.dynamic_slice(inp["b_mod"][l], (me * mod_cols,), (mod_cols,))
        mods_local.append(matmul(S_mat, inp["w_mod"][l], "nn", f"mod{l}", add=jnp.broadcast_to(bias[None], (S_rows, mod_cols))))
    mods_all = all_gather_vmem("gather_mods", jnp.stack(mods_local).reshape(-1, LANES))
    mods_all = jnp.moveaxis(mods_all.reshape(N_DEV, depth, S_rows, mod_cols), 0, 2).reshape(depth, S_rows, N_DEV * mod_cols)
    mods_lat = lax.dynamic_slice(mods_all, (0, me, 0), (depth, 1, N_DEV * mod_cols))[:, 0]
    mods_ctx = mods_all[:, N_DEV]

    def layer_mods(l):
        return [jnp.stack([mods_lat[l, j * D:(j + 1) * D], mods_ctx[l, j * D:(j + 1) * D]])[:, None] for j in range(6)]

    cs_swa = _rope_tables(L, T, SWA_HEAD_DIM)
    cs_mla = _rope_tables(L, T, MLA_ROPE)
    nm_descs = [_row(D), PAR, GRP, GRP]
    resid_descs = [_row(D, diff=False), _row(D), GRP]
    tab = [_row(LANES, diff=False), _row(LANES, diff=False)]
    swaq_descs = [_row(SWA_HEAD_DIM, 1), PAR] + tab
    swakv_descs = [_row(SWA_HEAD_DIM, 1), _row(SWA_HEAD_DIM, 1), PAR] + tab
    mlaq_descs = [_row(LANES, 1), _row(LANES, 1), PAR, PAR] + tab
    mlakv_descs = [_row(LANES, 1), _row(LANES, 1), _row(LANES), PAR, PAR] + tab
    ssdout_descs = [_row(SSM_INNER), _row(SSM_INNER, diff=False), _row(SSM_INNER), _row(SSM_INNER), PAR, PAR]
    merge_descs = [_row(D)] * 6
    swiglu_descs = [_row(ffn_h), _row(ffn_h)]
    seg_names = [sn for sn, _ in IN_SEGS]

    def layer_fwd(l, X, wt, p, mods):
        sh1, sc1, gt1, sh2, sc2, gt2 = mods
        r = {"X": X}
        r["h1"] = rf(f"l{l}_norm1", fn_norm_mod, nm_descs, [X, p["norm1_g"], sh1, sc1], [(D, 0, BF16)])[0]
        for sn in seg_names:
            r[sn] = matmul(r["h1"], wt[sn], "nt", f"l{l}_in_{sn}")
        r["u"] = conv_fwd(f"l{l}_conv", r["xbc"], conv_w8[l], p["ssm_conv_b"], L)
        r["dts"] = rf(f"l{l}_softplus", fn_softplus, [_row(2 * SSM_HEADS), PAR], [r["dt"], p["dt_bias"]], [(2 * SSM_HEADS, 0, F32)])[0]
        for d in range(2):
            dt_d = r["dts"][:, d * SSM_HEADS:(d + 1) * SSM_HEADS]
            r[f"dt{d}"], r[f"dtt{d}"] = dt_d, dt_d.T
            r[f"y{d}"], r[f"hin{d}"] = ssd_fwd(f"l{l}_ssd{d}", d, r["u"], dt_d, dt_d.T, p["alog_row"][d], p["alog_col"][d], L)
        r["ys"] = rf(f"l{l}_ssd_out", fn_ssd_out, ssdout_descs, [r["y0"], r["y1"], r["u"], r["z"], p["d_lane"], p["ssm_norm_g"]],
                     [(SSM_INNER, 0, F32)])[0]
        r["Qs"] = rf(f"l{l}_swa_q", fn_swa_q, swaq_descs, [r["qs"], p["swa_q_norm_g"], *cs_swa], [(SWA_HEAD_DIM, 1, BF16)], SWA_Q_HEADS)[0]
        r["Ks"], r["Vs"] = rf(f"l{l}_swa_kv", fn_swa_kv, swakv_descs, [r["ks"], r["vs"], p["swa_k_norm_g"], *cs_swa],
                              [(SWA_HEAD_DIM, 1, BF16), (SWA_HEAD_DIM, 1, BF16)], SWA_KV_HEADS)
        r["Os"], r["lse_s"] = flash_fwd(f"l{l}_swa_fwd", cfg_swa, r["Qs"], r["Ks"], r["Vs"], p["sink"])
        r["cqn"] = rf(f"l{l}_q_lat", fn_rms, [_row(MLA_Q_RANK), PAR], [r["cq"], p["mla_q_lat_g"]], [(MLA_Q_RANK, 0, BF16)])[0]
        r["qn"] = matmul(r["cqn"], wt["uqn"], "nt", f"l{l}_uqn")
        r["qr"] = matmul(r["cqn"], wt["uqr"], "nt", f"l{l}_uqr")
        r["Qm"] = rf(f"l{l}_mla_q", fn_mla_q, mlaq_descs, [r["qn"], r["qr"], p["gqn"], p["gqr"], *cs_mla], [(MLA_QK_PAD, 1, BF16)], MLA_HEADS)[0]
        r["ckvn"] = rf(f"l{l}_kv_lat", fn_rms, [_row(MLA_KV_RANK), PAR], [r["ckv"], p["mla_kv_lat_g"]], [(MLA_KV_RANK, 0, BF16)])[0]
        r["kn"] = matmul(r["ckvn"], wt["uk"], "nt", f"l{l}_uk")
        r["vp"] = matmul(r["ckvn"], wt["uv"], "nt", f"l{l}_uv")
        r["Km"], r["Vm"] = rf(f"l{l}_mla_kv", fn_mla_kv, mlakv_descs, [r["kn"], r["vp"], r["kr"], p["gkn"], p["gkr"], *cs_mla],
                              [(MLA_QK_PAD, 1, BF16), (MLA_V, 1, BF16)], MLA_HEADS)
        r["Om"], r["lse_m"] = flash_fwd(f"l{l}_mla_fwd", cfg_mla, r["Qm"], r["Km"], r["Vm"], None)
        r["P1"] = matmul(r["ys"], wt["w_p_ssm"], "nn", f"l{l}_p_ssm")
        r["P2"] = matmul(r["Os"], wt["w_p_swa"], "nn", f"l{l}_p_swa")
        r["P3"] = matmul(r["Om"], wt["w_p_mla"], "nn", f"l{l}_p_mla")
        r["mg"] = rf(f"l{l}_merge", fn_merge, merge_descs, [r["g1"], r["g2"], r["g3"], r["P1"], r["P2"], r["P3"]], [(D, 0, BF16)])[0]
        r["A"] = matmul(r["mg"], wt["w_out"], "nn", f"l{l}_out")
        r["X1"] = rf(f"l{l}_resid1", fn_resid, resid_descs, [X, r["A"], gt1], [(D, 0, F32)])[0]
        r["h2"] = rf(f"l{l}_norm2", fn_norm_mod, nm_descs, [r["X1"], p["norm2_g"], sh2, sc2], [(D, 0, BF16)])[0]
        r["Fg"] = matmul(r["h2"], wt["fg"], "nt", f"l{l}_ffn_g")
        r["Fu"] = matmul(r["h2"], wt["fu"], "nt", f"l{l}_ffn_u")
        r["sg"] = rf(f"l{l}_swiglu", fn_swiglu, swiglu_descs, [r["Fg"], r["Fu"]], [(ffn_h, 0, BF16)])[0]
        r["B"] = matmul(r["sg"], wt["w_ffn_out"], "nn", f"l{l}_ffn_out")
        X2 = rf(f"l{l}_resid2", fn_resid, resid_descs, [r["X1"], r["B"], gt2], [(D, 0, F32)])[0]
        return X2, r

    def attn_bwd(tag, cfg, q, k, v, o, lse, do, sink):
        res = attn_delta(f"{tag}_delta", cfg, o, do, lse, sink)
        delta, dob = res[0], res[1]
        dsink = res[2] if cfg.has_sink else None
        dq = flash_dq(f"{tag}_dq", cfg, q, k, v, dob, lse, delta)
        shp = (cfg.hq, T // cfg.chunk, 1, cfg.chunk)
        dk, dv = flash_dkv(f"{tag}_dkv", cfg, q, k, v, dob, lse.reshape(shp), delta.reshape(shp))
        return dq, dk, dv, dsink

    def layer_bwd(l, dX2, r, wt, p, mods):
        sh1, sc1, gt1, sh2, sc2, gt2 = mods
        g, gw = {}, {}
        dmod = [None] * 6
        dB, dmod[5] = rb(f"l{l}_resid2_b", fn_resid, resid_descs, [r["X1"], r["B"], gt2], [(D, 0, F32)], [dX2])
        dsg = matmul(dB, wt["w_ffn_out"], "nt", f"l{l}_ffn_out_da")
        gw["w_ffn_out"] = matmul(r["sg"], dB, "tn", f"l{l}_ffn_out_dw")
        dFg, dFu = rb(f"l{l}_swiglu_b", fn_swiglu, swiglu_descs, [r["Fg"], r["Fu"]], [(ffn_h, 0, BF16)], [dsg])
        dh2 = matmul(dFg, wt["fg"], "nn", f"l{l}_ffn_g_da")
        dh2 = matmul(dFu, wt["fu"], "nn", f"l{l}_ffn_u_da", add=dh2)
        gw["w_ffn_in"] = jnp.concatenate([matmul(r["h2"], dFg, "tn", f"l{l}_ffn_g_dw"), matmul(r["h2"], dFu, "tn", f"l{l}_ffn_u_dw")], axis=1)
        dX1, g["norm2_g"], dmod[3], dmod[4] = rb(f"l{l}_norm2_b", fn_norm_mod, nm_descs, [r["X1"], p["norm2_g"], sh2, sc2],
                                                [(D, 0, BF16)], [dh2], add={0: dX2})
        dA, dmod[2] = rb(f"l{l}_resid1_b", fn_resid, resid_descs, [r["X"], r["A"], gt1], [(D, 0, F32)], [dX1])
        dmg = matmul(dA, wt["w_out"], "nt", f"l{l}_out_da")
        gw["w_out"] = matmul(r["mg"], dA, "tn", f"l{l}_out_dw")
        dsegs = {}
        dsegs["g1"], dsegs["g2"], dsegs["g3"], dP1, dP2, dP3 = rb(
            f"l{l}_merge_b", fn_merge, merge_descs, [r["g1"], r["g2"], r["g3"], r["P1"], r["P2"], r["P3"]], [(D, 0, BF16)], [dmg])
        dys = matmul(dP1, wt["w_p_ssm"], "nt", f"l{l}_p_ssm_da")
        dOs = matmul(dP2, wt["w_p_swa"], "nt", f"l{l}_p_swa_da")
        dOm = matmul(dP3, wt["w_p_mla"], "nt", f"l{l}_p_mla_da")
        gw["w_p_ssm"] = matmul(r["ys"], dP1, "tn", f"l{l}_p_ssm_dw")
        gw["w_p_swa"] = matmul(r["Os"], dP2, "tn", f"l{l}_p_swa_dw")
        gw["w_p_mla"] = matmul(r["Om"], dP3, "tn", f"l{l}_p_mla_dw")
        dQm, dKm, dVm, _ = attn_bwd(f"l{l}_mla", cfg_mla, r["Qm"], r["Km"], r["Vm"], r["Om"], r["lse_m"], dOm, None)
        dkn, dvp, dsegs["kr"], dgkn, dgkr = rb(f"l{l}_mla_kv_b", fn_mla_kv, mlakv_descs,
                                               [r["kn"], r["vp"], r["kr"], p["gkn"], p["gkr"], *cs_mla],
                                               [(MLA_QK_PAD, 1, BF16), (MLA_V, 1, BF16)], [dKm, dVm], MLA_HEADS)
        dckvn = matmul(dkn, wt["uk"], "nn", f"l{l}_uk_da")
        dckvn = matmul(dvp, wt["uv"], "nn", f"l{l}_uv_da", add=dckvn)
        dw_uk = matmul(r["ckvn"], dkn, "tn", f"l{l}_uk_dw").reshape(MLA_KV_RANK, MLA_HEADS, MLA_NOPE)
        dw_uv = matmul(r["ckvn"], dvp, "tn", f"l{l}_uv_dw").reshape(MLA_KV_RANK, MLA_HEADS, MLA_V)
        gw["w_mla_ukv"] = jnp.concatenate([dw_uk, dw_uv], axis=2).reshape(MLA_KV_RANK, -1)
        dsegs["ckv"], g["mla_kv_lat_g"] = rb(f"l{l}_kv_lat_b", fn_rms, [_row(MLA_KV_RANK), PAR], [r["ckv"], p["mla_kv_lat_g"]],
                                             [(MLA_KV_RANK, 0, BF16)], [dckvn])
        dqn, dqr, dgqn, dgqr = rb(f"l{l}_mla_q_b", fn_mla_q, mlaq_descs, [r["qn"], r["qr"], p["gqn"], p["gqr"], *cs_mla],
                                  [(MLA_QK_PAD, 1, BF16)], [dQm], MLA_HEADS)
        dcqn = matmul(dqn, wt["uqn"], "nn", f"l{l}_uqn_da")
        dcqn = matmul(dqr, wt["uqr"], "nn", f"l{l}_uqr_da", add=dcqn)
        dw_uqn = matmul(r["cqn"], dqn, "tn", f"l{l}_uqn_dw").reshape(MLA_Q_RANK, MLA_HEADS, MLA_NOPE)
        dw_uqr = matmul(r["cqn"], dqr, "tn", f"l{l}_uqr_dw").reshape(MLA_Q_RANK, MLA_HEADS, LANES)[:, :, :MLA_ROPE]
        gw["w_mla_uq"] = jnp.concatenate([dw_uqn, dw_uqr], axis=2).reshape(MLA_Q_RANK, -1)
        dsegs["cq"], g["mla_q_lat_g"] = rb(f"l{l}_q_lat_b", fn_rms, [_row(MLA_Q_RANK), PAR], [r["cq"], p["mla_q_lat_g"]],
                                           [(MLA_Q_RANK, 0, BF16)], [dcqn])
        g["mla_q_norm_g"] = jnp.concatenate([dgqn[0], dgqr[0, :MLA_ROPE]])
        g["mla_k_norm_g"] = jnp.concatenate([dgkn[0], dgkr[0, :MLA_ROPE]])
        dQs, dKs, dVs, dsink = attn_bwd(f"l{l}_swa", cfg_swa, r["Qs"], r["Ks"], r["Vs"], r["Os"], r["lse_s"], dOs, p["sink"])
        g["swa_sink"] = dsink.reshape(SWA_Q_HEADS)
        dsegs["qs"], g["swa_q_norm_g"] = rb(f"l{l}_swa_q_b", fn_swa_q, swaq_descs, [r["qs"], p["swa_q_norm_g"], *cs_swa],
                                            [(SWA_HEAD_DIM, 1, BF16)], [dQs], SWA_Q_HEADS)
        dsegs["ks"], dsegs["vs"], g["swa_k_norm_g"] = rb(f"l{l}_swa_kv_b", fn_swa_kv, swakv_descs,
                                                         [r["ks"], r["vs"], p["swa_k_norm_g"], *cs_swa],
                                                         [(SWA_HEAD_DIM, 1, BF16), (SWA_HEAD_DIM, 1, BF16)], [dKs, dVs], SWA_KV_HEADS)
        dy, dxs, dsegs["z"], dd_lane, g["ssm_norm_g"] = rb(
            f"l{l}_ssd_out_b", fn_ssd_out, ssdout_descs, [r["y0"], r["y1"], r["u"], r["z"], p["d_lane"], p["ssm_norm_g"]],
            [(SSM_INNER, 0, F32)], [dys])
        g["ssm_d"] = dd_lane.reshape(SSM_HEADS, SSM_HEAD_DIM).sum(axis=1)
        du, ddts, dalog = None, [], []
        for d in range(2):
            du, ddt, ddtt, dar, dac = ssd_bwd(f"l{l}_ssd{d}_b", d, r["u"], r[f"dt{d}"], r[f"dtt{d}"], p["alog_row"][d], p["alog_col"][d],
                                              r[f"hin{d}"], dy, L, add_x=dxs if d == 0 else None, add_u=du)
            ddts.append(ddt + ddtt.T)
            dalog.append(dar[0] + dac[:, 0])
        g["ssm_a_log"] = jnp.stack(dalog)
        dsegs["xbc"], dconv_w, g["ssm_conv_b"] = conv_bwd(f"l{l}_conv_b", r["xbc"], conv_w8[l], p["ssm_conv_b"], du, L)
        dsegs["dt"], ddt_bias = rb(f"l{l}_softplus_b", fn_softplus, [_row(2 * SSM_HEADS), PAR], [r["dt"], p["dt_bias"]],
                                   [(2 * SSM_HEADS, 0, F32)], [jnp.concatenate(ddts, axis=1)])
        g["ssm_dt_bias"] = ddt_bias.reshape(2, SSM_HEADS)
        g["ssm_conv_w"] = dconv_w[:SSM_CONV]
        dh1, dws = None, []
        for sn, w in zip(seg_names, in_widths):
            dh1 = matmul(dsegs[sn], wt[sn], "nn", f"l{l}_in_{sn}_da", add=dh1)
            dws.append(matmul(r["h1"], dsegs[sn], "tn", f"l{l}_in_{sn}_dw")[:, :w])
        gw["w_in"] = jnp.concatenate(dws, axis=1)
        dX, g["norm1_g"], dmod[0], dmod[1] = rb(f"l{l}_norm1_b", fn_norm_mod, nm_descs, [r["X"], p["norm1_g"], sh1, sc1],
                                               [(D, 0, BF16)], [dh1], add={0: dX1})
        for n in ("norm1_g", "norm2_g", "ssm_conv_b", "ssm_norm_g", "swa_q_norm_g", "swa_k_norm_g", "mla_q_lat_g", "mla_kv_lat_g"):
            g[n] = g[n][0]
        dmod_lat = jnp.concatenate([dm[0, 0] for dm in dmod])
        dmod_ctx = jnp.concatenate([dm[1, 0] for dm in dmod])
        return dX, g, gw, dmod_lat, dmod_ctx

    X = jnp.concatenate([x, ctx], axis=0)
    saved = []
    for l in range(depth):
        wt, p, mods = layer_weights(l), layer_params(l), layer_mods(l)
        X, r = layer_fwd(l, X, wt, p, mods)
        saved.append((r, wt, p, mods))
    loss_part, dX = loss_and_grad("loss", X, loss_target[0], L)
    loss = lax.psum(loss_part[0, 0], ("x", "y", "c"))
    small_g = [None] * depth
    big_g = [None] * depth
    dmods = [None] * depth
    for l in reversed(range(depth)):
        r, wt, p, mods = saved[l]
        dX, small_g[l], big_g[l], dm_lat, dm_ctx = layer_bwd(l, dX, r, wt, p, mods)
        dmods[l] = jnp.stack([dm_lat, dm_ctx])
    grad_x = dX[:L][None]

    dm_all = all_gather_vmem("gather_dmods", jnp.stack(dmods).reshape(-1, LANES)).reshape(N_DEV, depth, 2, N_DEV * mod_cols)
    dm_rows = jnp.concatenate([jnp.moveaxis(dm_all[:, :, 0], 0, 1), dm_all[:, :, 1].sum(axis=0)[:, None],
                               jnp.zeros((depth, S_rows - N_DEV - 1, N_DEV * mod_cols), F32)], axis=1)
    dm_mine = lax.dynamic_slice(dm_rows, (0, 0, me * mod_cols), (depth, S_rows, mod_cols))
    grads = {}
    grads["w_mod"] = jnp.stack([matmul(S_mat, dm_mine[l], "tn", f"mod{l}_dw") for l in range(depth)])
    d_silu = None
    for l in range(depth):
        d_silu = matmul(dm_mine[l], inp["w_mod"][l], "nt", f"mod{l}_da", add=d_silu)
    small = {n: jnp.stack([small_g[l][n] for l in range(depth)]) for n in small_g[0]}
    small["c_ctx"] = small_bwd("silu_c_b", fn_silu, [inp["c_ctx"][None]], [d_silu[N_DEV:N_DEV + 1]])[0][0]
    small["b_mod"] = jnp.stack(dmods).sum(axis=1)

    rep_shapes = [inp[n].shape for n in REPLICATED]
    conv_shape = (depth, SSM_CONV, SSM_CONV_DIM)
    packed = _pack([small[n] for n in REPLICATED] + [small["ssm_conv_w"]], 8 * LANES)
    small_sum = sum_parts("sum_small", all_gather_vmem("gather_small", packed))
    summed = _unpack(small_sum, rep_shapes + [conv_shape])
    for n, gsum in zip(REPLICATED, summed):
        grads[n] = gsum
    grads["ssm_conv_w"] = lax.dynamic_slice(summed[-1], (0, 0, me * cw[2]), cw)

    slabs = []
    for n in GATHERED:
        gfull = jnp.stack([big_g[l][n] for l in range(depth)])
        if n in COLUMN_SHARDED:
            k_dim, n_dim = gfull.shape[1], gfull.shape[2]
            slabs.append(jnp.moveaxis(gfull.reshape(depth, k_dim, N_DEV, n_dim // N_DEV), 2, 0))
        else:
            k_dim, n_dim = gfull.shape[1], gfull.shape[2]
            slabs.append(jnp.moveaxis(gfull.reshape(depth, N_DEV, k_dim // N_DEV, n_dim), 1, 0))
    for n, parts in zip(GATHERED, exchange_hbm("exchange_grads", slabs)):
        shp = inp[n].shape
        grads[n] = sum_parts(f"sum_{n}", parts.reshape(N_DEV, shp[0] * shp[1], shp[2])).reshape(shp)

    delta, new_m, new_v = {}, {}, {}
    rep_pack = lambda d: _pack([d[n] for n in REPLICATED], 8 * LANES)
    rep_out = adamw("adamw_small", rep_pack(inp), rep_pack(grads), rep_pack(mom_m), rep_pack(mom_v))
    for out, res in zip((delta, new_m, new_v), rep_out):
        for n, a in zip(REPLICATED, _unpack(res, rep_shapes)):
            out[n] = a
    for n in ["w_mod", "ssm_conv_w"] + GATHERED:
        shp = inp[n].shape
        two_d = (shp[0] * shp[1], shp[2])
        res = adamw(f"adamw_{n}", inp[n].reshape(two_d), grads[n].reshape(two_d), mom_m[n].reshape(two_d), mom_v[n].reshape(two_d))
        delta[n], new_m[n], new_v[n] = [a.reshape(shp) for a in res]

    return (loss, grad_x, *[grads[n] for n in WEIGHT_NAMES], *[delta[n] for n in WEIGHT_NAMES],
            *[new_m[n] for n in WEIGHT_NAMES], *[new_v[n] for n in WEIGHT_NAMES])
```

```python
import functools
import math

import numpy as np
import jax
import jax.numpy as jnp
from jax import lax
from jax.experimental import pallas as pl
from jax.experimental.pallas import tpu as pltpu

F32 = jnp.float32
BF16 = jnp.bfloat16

N_DEV = 8
V7X_VMEM_BYTES = 64 * 1024 * 1024
VMEM_LIMIT_BYTES = V7X_VMEM_BYTES - 8 * 1024 * 1024
LANES = 128

EPS = 1e-6
ROPE_BASE = 10000.0
GRID_W = 64
SSM_HEADS, SSM_HEAD_DIM, SSM_GROUPS, SSM_STATE, SSM_CONV, SSM_CHUNK = 16, 64, 2, 128, 5, 128
SSM_INNER = SSM_HEADS * SSM_HEAD_DIM
SSM_CONV_DIM = SSM_INNER + 2 * SSM_GROUPS * SSM_STATE
SWA_Q_HEADS, SWA_KV_HEADS, SWA_HEAD_DIM, SWA_WINDOW = 8, 2, 128, 128
MLA_HEADS, MLA_Q_RANK, MLA_KV_RANK, MLA_NOPE, MLA_ROPE, MLA_V = 8, 384, 256, 128, 64, 128
MLA_QK = MLA_NOPE + MLA_ROPE
MLA_QK_PAD = 2 * LANES
ADAM_LR, ADAM_B1, ADAM_B2, ADAM_EPS, ADAM_WD, ADAM_STEP = 0.001, 0.9, 0.999, 1e-08, 0.01, 10

ROW_TILE = 256


def _cparams(sem, **kw):
    return pltpu.CompilerParams(dimension_semantics=sem, vmem_limit_bytes=VMEM_LIMIT_BYTES, **kw)


def _pick(dim, prefs):
    for p in prefs:
        if dim % p == 0:
            return p
    return dim


def matmul(a, b, mode, name, out_dtype=F32, add=None):
    if mode == "nn":
        (M, K), (K2, N) = a.shape, b.shape
    elif mode == "nt":
        (M, K), (N, K2) = a.shape, b.shape
    else:
        (K, M), (K2, N) = a.shape, b.shape
    assert K == K2, (name, a.shape, b.shape)
    has_add = add is not None
    tm, tn, tk = _matmul_tiles(M, N, K, a.dtype.itemsize, b.dtype.itemsize, jnp.dtype(out_dtype).itemsize, has_add)
    nk = K // tk
    dims = {"nn": (((1,), (0,)), ((), ())), "nt": (((1,), (1,)), ((), ())), "tn": (((0,), (0,)), ((), ()))}[mode]
    a_spec = pl.BlockSpec((tk, tm), lambda i, j, k: (k, i)) if mode == "tn" else pl.BlockSpec((tm, tk), lambda i, j, k: (i, k))
    b_spec = pl.BlockSpec((tn, tk), lambda i, j, k: (j, k)) if mode == "nt" else pl.BlockSpec((tk, tn), lambda i, j, k: (k, j))
    o_spec = pl.BlockSpec((tm, tn), lambda i, j, k: (i, j))

    def body(*refs):
        a_ref, b_ref = refs[:2]
        c_ref = refs[2] if has_add else None
        o_ref = refs[3] if has_add else refs[2]
        part = lax.dot_general(a_ref[...].astype(BF16), b_ref[...].astype(BF16), dims, preferred_element_type=F32)
        if nk == 1:
            o_ref[...] = (part + c_ref[...] if has_add else part).astype(o_ref.dtype)
            return
        acc_ref = refs[-1]
        k = pl.program_id(2)

        @pl.when(k == 0)
        def _():
            acc_ref[...] = part + c_ref[...] if has_add else part

        @pl.when(k > 0)
        def _():
            acc_ref[...] += part

        @pl.when(k == nk - 1)
        def _():
            o_ref[...] = acc_ref[...].astype(o_ref.dtype)

    ins = [a, b] + ([add] if has_add else [])
    in_specs = [a_spec, b_spec] + ([o_spec] if has_add else [])
    return pl.pallas_call(
        body, name=name, grid=(M // tm, N // tn, nk), in_specs=in_specs, out_specs=o_spec,
        out_shape=jax.ShapeDtypeStruct((M, N), out_dtype),
        scratch_shapes=[pltpu.VMEM((tm, tn), F32)] if nk > 1 else [],
        input_output_aliases=({2: 0} if has_add else {}),
        compiler_params=_cparams(("parallel", "parallel", "arbitrary")),
    )(*ins)


MATMUL_VMEM_BUDGET = 36 * 1024 * 1024


def _matmul_tiles(M, N, K, a_bytes, b_bytes, o_bytes, has_add):
    tk = K if K <= 1536 else _pick(K, (1408, 1024, 768, 704, 512, 256))
    nk = K // tk
    m_cands = [t for t in (1024, 768, 512, 384, 256, 128) if M % t == 0] or [M]
    n_cands = [t for t in range(LANES, min(N, 2816) + 1, LANES) if N % t == 0] or [N]
    best = None
    for tm in m_cands:
        for tn in n_cands:
            pipeline = 2 * (tm * tk * a_bytes + tk * tn * b_bytes + tm * tn * o_bytes) + (2 * tm * tn * 4 if has_add else 0)
            temps = tm * tn * 4 * (2 if nk > 1 else 1) + (tm * tk * 2 if a_bytes == 4 else 0) + (tk * tn * 2 if b_bytes == 4 else 0)
            if pipeline + temps <= MATMUL_VMEM_BUDGET:
                score = (tm * tn, tn)
                if best is None or score > best[0]:
                    best = (score, tm, tn)
    if best is None:
        return m_cands[-1], n_cands[0], tk
    return best[1], best[2], tk


def _row_specs(descs, arrays, tm, nct, heads):
    specs = []
    for d, arr in zip(descs, arrays):
        if d[0] == "row":
            _, w, per_head, off, _ = d
            specs.append(pl.BlockSpec((tm, w * (heads if per_head else 1)), lambda i, off=off: (i, off)))
        elif d[0] == "par":
            specs.append(pl.BlockSpec(arr.shape, lambda i, nd=arr.ndim: (0,) * nd))
        else:
            specs.append(pl.BlockSpec((1,) + arr.shape[1:], lambda i, nd=arr.ndim: (jnp.where(i >= nct, 1, 0),) + (0,) * (nd - 1)))
    return specs


def _load(d, ref, h):
    if d[0] == "grp":
        return ref[0]
    if d[0] == "row" and d[2]:
        return ref[:, h * d[1]:(h + 1) * d[1]]
    return ref[...]


def _out_specs(outs, tm, heads):
    return [pl.BlockSpec((tm, w * (heads if ph else 1)), lambda i: (i, 0)) for (w, ph, _) in outs]


def rowop_fwd(name, fn, descs, arrays, outs, T, n_ctx, heads=1, tm=ROW_TILE):
    nct = n_ctx // tm
    n_in = len(descs)

    def body(*refs):
        for h in range(heads):
            res = fn(*[_load(d, r, h) for d, r in zip(descs, refs[:n_in])])
            for o_ref, r, (w, ph, _) in zip(refs[n_in:], res, outs):
                if ph:
                    o_ref[:, h * w:(h + 1) * w] = r.astype(o_ref.dtype)
                else:
                    o_ref[...] = r.astype(o_ref.dtype)

    out_shape = [jax.ShapeDtypeStruct((T, w * (heads if ph else 1)), dt) for (w, ph, dt) in outs]
    return pl.pallas_call(
        body, name=name, grid=(T // tm,), in_specs=_row_specs(descs, arrays, tm, nct, heads), out_specs=_out_specs(outs, tm, heads),
        out_shape=out_shape, compiler_params=_cparams(("parallel",)),
    )(*arrays)


def rowop_bwd(name, fn, descs, arrays, outs, cts, T, n_ctx, heads=1, tm=ROW_TILE, add=None):
    nct = n_ctx // tm
    n_in, n_ct = len(descs), len(cts)
    add = add or {}
    diff_idx = [k for k, d in enumerate(descs) if d[-1]]
    add_idx = [k for k in diff_idx if k in add]

    def body(*refs):
        in_refs, ct_refs = refs[:n_in], refs[n_in:n_in + n_ct]
        add_refs = dict(zip(add_idx, refs[n_in + n_ct:n_in + n_ct + len(add_idx)]))
        g_refs = refs[n_in + n_ct + len(add_idx):]
        i = pl.program_id(0)
        shared = {}
        for h in range(heads):
            vals = [_load(d, r, h) for d, r in zip(descs, in_refs)]

            def f(*dvals, vals=vals):
                full = list(vals)
                for k, v in zip(diff_idx, dvals):
                    full[k] = v
                return tuple(fn(*full))

            _, vjp = jax.vjp(f, *[vals[k] for k in diff_idx])
            cts_h = tuple(c[:, h * w:(h + 1) * w] if ph else c[...] for c, (w, ph, _) in zip(ct_refs, outs))
            for k, g_ref, g in zip(diff_idx, g_refs, vjp(cts_h)):
                d = descs[k]
                if d[0] == "row" and d[2]:
                    g_ref[:, h * d[1]:(h + 1) * d[1]] = g.astype(g_ref.dtype)
                else:
                    shared[k] = g if k not in shared else shared[k] + g
        for k, g_ref in zip(diff_idx, g_refs):
            d = descs[k]
            if k not in shared:
                continue
            g = shared[k]
            if d[0] == "row":
                if k in add_refs:
                    g = g + add_refs[k][...]
                g_ref[...] = g.astype(g_ref.dtype)
            elif d[0] == "par":
                _accumulate(g_ref, g, i == 0)
            else:
                _accumulate(g_ref, g[None], jnp.logical_or(i == 0, i == nct))

    in_specs = _row_specs(descs, arrays, tm, nct, heads)
    g_specs, g_shape = [], []
    for k in diff_idx:
        d = descs[k]
        if d[0] == "row":
            g_specs.append(pl.BlockSpec((tm, d[1] * (heads if d[2] else 1)), lambda i: (i, 0)))
            g_shape.append(jax.ShapeDtypeStruct((T, d[1] * (heads if d[2] else 1)), F32))
        else:
            g_specs.append(in_specs[k])
            g_shape.append(jax.ShapeDtypeStruct(arrays[k].shape, F32))
    add_specs = [g_specs[diff_idx.index(k)] for k in add_idx]
    return pl.pallas_call(
        body, name=name, grid=(T // tm,), in_specs=in_specs + _out_specs(outs, tm, heads) + add_specs, out_specs=g_specs,
        out_shape=g_shape, compiler_params=_cparams(("arbitrary",)),
    )(*arrays, *cts, *[add[k] for k in add_idx])


def _accumulate(ref, val, first):
    @pl.when(first)
    def _():
        ref[...] = val.astype(ref.dtype)

    @pl.when(jnp.logical_not(first))
    def _():
        ref[...] += val.astype(ref.dtype)


def _rms(x, count=None):
    n = x.shape[-1] if count is None else count
    return x * lax.rsqrt(jnp.sum(x * x, axis=-1, keepdims=True) * (1.0 / n) + EPS)


def _swap_halves(x, nf):
    w = x.shape[-1]
    lane = lax.broadcasted_iota(jnp.int32, x.shape, x.ndim - 1)
    return jnp.where((lane % (2 * nf)) < nf, pltpu.roll(x, w - nf, x.ndim - 1), pltpu.roll(x, nf, x.ndim - 1))


def _make_rope(nf):
    @jax.custom_vjp
    def rope(x, c, s):
        return x * c + _swap_halves(x, nf) * s

    def fwd(x, c, s):
        return rope(x, c, s), (c, s)

    def bwd(res, g):
        c, s = res
        return g * c + _swap_halves(g * s, nf), jnp.zeros_like(c), jnp.zeros_like(s)

    rope.defvjp(fwd, bwd)
    return rope


_rope_swa = _make_rope(SWA_HEAD_DIM // 4)
_rope_mla = _make_rope(MLA_ROPE // 4)


@jax.custom_vjp
def _softplus(x):
    e = jnp.exp(-jnp.abs(x))
    u = 1.0 + e
    log1p_e = jnp.where(u == 1.0, e, jnp.log(u) * e / jnp.where(u == 1.0, 1.0, u - 1.0))
    return jnp.maximum(x, 0.0) + log1p_e


_softplus.defvjp(lambda x: (_softplus(x), x), lambda x, g: (g * jax.nn.sigmoid(x),))


def fn_norm_mod(x, g, shift, scale):
    return (_rms(x) * g * (1.0 + scale) + shift,)


def fn_rms(x, g):
    return (_rms(x) * g,)


def fn_resid(x, a, gate):
    return (x + gate * a,)


def fn_softplus(dt, bias):
    return (_softplus(dt + bias),)


def fn_ssd_out(yf, yb, xs, z, d_lane, g):
    y = yf + yb + d_lane * xs
    return (_rms(y * (z * jax.nn.sigmoid(z))) * g,)


def fn_swa_q(q, g, c, s):
    return (_rope_swa(_rms(q) * g, c, s),)


def fn_swa_kv(k, v, g, c, s):
    return (_rope_swa(_rms(k) * g, c, s), v)


def fn_mla_q(qn, qr, gn, gr, c, s):
    return (jnp.concatenate([_rms(qn) * gn, _rope_mla(_rms(qr, MLA_ROPE) * gr, c, s)], axis=-1),)


def fn_mla_kv(kn, v, kr, gn, gr, c, s):
    return (jnp.concatenate([_rms(kn) * gn, _rope_mla(_rms(kr, MLA_ROPE) * gr, c, s)], axis=-1), v)


def fn_merge(g1, g2, g3, p1, p2, p3):
    return (jax.nn.sigmoid(g1) * p1 + jax.nn.sigmoid(g2) * p2 + jax.nn.sigmoid(g3) * p3,)


def fn_swiglu(g, u):
    return (g * jax.nn.sigmoid(g) * u,)


ATTN_TILE = 256
NT_DIMS = (((1,), (1,)), ((), ()))


class AttnCfg:
    def __init__(self, hq, group, dq, dv, scale, window, has_sink, L, T, chunk, kv_block):
        self.hq, self.group, self.dq, self.dv, self.scale = hq, group, dq, dv, scale
        self.window, self.has_sink, self.L, self.T = window, has_sink, L, T
        self.chunk = _pick(L, (chunk, ATTN_TILE))
        self.ctx_chunk = T - L
        self.kv_block = kv_block
        self.q_block = kv_block * group
        assert L % ATTN_TILE == 0 and (T - L) % ATTN_TILE == 0 and L % self.chunk == 0
        assert (hq // group) % kv_block == 0
        if window is not None:
            assert (ATTN_TILE + 2 * window) % self.chunk == 0
            self.window_chunks = min((ATTN_TILE + 2 * window) // self.chunk, L // self.chunk)
            self.align = math.gcd(self.chunk, window)
        else:
            self.align = self.chunk


LOG2E = math.log2(math.e)


def _latent_chunks(cfg, r0):
    c = cfg.chunk
    if cfg.window is None:
        lo, n = 0, cfg.L // c
    else:
        n = cfg.window_chunks
        lo = jnp.clip(r0 - cfg.window, 0, cfg.L - n * c)
    return lo, n


def _visible(cfg, rows_q, rows_k):
    return jnp.logical_or(rows_k >= cfg.L, jnp.abs(rows_k - rows_q) <= cfg.window)


def flash_fwd(name, cfg, q, k, v, sink):
    T, tq, c = cfg.T, ATTN_TILE, cfg.chunk
    hq, g, dq, dv, hb, kb = cfg.hq, cfg.group, cfg.dq, cfg.dv, cfg.q_block, cfg.kv_block
    to_log2 = cfg.scale * LOG2E

    def body(*refs):
        if cfg.has_sink:
            q_ref, k_ref, v_ref, sink_ref, o_ref, lse_ref = refs
        else:
            q_ref, k_ref, v_ref, o_ref, lse_ref = refs
        q0 = pl.program_id(1) * tq
        qs = [q_ref[:, hh * dq:(hh + 1) * dq] for hh in range(hb)]
        lat_lo, lat_n = _latent_chunks(cfg, q0)
        n = jnp.where(q0 >= cfg.L, 0, lat_n)
        rows_q = q0 + lax.broadcasted_iota(jnp.int32, (tq, 1), 0)

        def start(t):
            return pl.multiple_of(lat_lo + jnp.minimum(t, lat_n - 1) * c, cfg.align)

        def logits(ks, size):
            return tuple(lax.dot_general(qs[hh], k_ref[pl.ds(ks, size), (hh // g) * dq:(hh // g + 1) * dq], NT_DIMS,
                                         preferred_element_type=F32) for hh in range(hb))

        def update(state, s_all, ks, size, masked):
            new_state = []
            for hh in range(hb):
                m, l, acc = state[hh]
                s = s_all[hh]
                if masked:
                    rows_k = ks + lax.broadcasted_iota(jnp.int32, (1, size), 1)
                    s = jnp.where(_visible(cfg, rows_q, rows_k), s, -jnp.inf)
                m_new = jnp.maximum(m, jnp.max(s, axis=-1, keepdims=True) * to_log2)
                alpha = jnp.exp2(m - m_new)
                p = jnp.exp2(s * to_log2 - m_new)
                l = alpha * l + jnp.sum(p, axis=-1, keepdims=True)
                kh = hh // g
                acc = alpha * acc + jnp.dot(p.astype(BF16), v_ref[pl.ds(ks, size), kh * dv:(kh + 1) * dv], preferred_element_type=F32)
                new_state.append((m_new, l, acc))
            return tuple(new_state)

        def step(t, carry):
            state, s_all = carry
            s_next = logits(start(t + 1), c)
            return update(state, s_all, start(t), c, cfg.window is not None), s_next

        state = []
        for hh in range(hb):
            if cfg.has_sink:
                m0 = jnp.zeros((tq, 1), F32) + sink_ref[hh] * LOG2E
                l0 = jnp.ones((tq, 1), F32)
            else:
                m0 = jnp.full((tq, 1), -jnp.inf, F32)
                l0 = jnp.zeros((tq, 1), F32)
            state.append((m0, l0, jnp.zeros((tq, dv), F32)))
        state = update(tuple(state), logits(cfg.L, cfg.ctx_chunk), cfg.L, cfg.ctx_chunk, False)
        state, _ = lax.fori_loop(0, n, step, (state, logits(start(0), c)))
        for hh in range(hb):
            m, l, acc = state[hh]
            o_ref[:, hh * dv:(hh + 1) * dv] = acc / l
            lse_ref[hh] = m + jnp.log2(l)

    in_specs = [pl.BlockSpec((tq, hb * dq), lambda h, i: (i, h)),
                pl.BlockSpec((T, kb * dq), lambda h, i: (0, h)),
                pl.BlockSpec((T, kb * dv), lambda h, i: (0, h))]
    ins = [q, k, v]
    if cfg.has_sink:
        in_specs.append(pl.BlockSpec((hb, 1, 1), lambda h, i: (h, 0, 0)))
        ins.append(sink)
    return pl.pallas_call(
        body, name=name, grid=(hq // hb, T // tq), in_specs=in_specs,
        out_specs=[pl.BlockSpec((tq, hb * dv), lambda h, i: (i, h)), pl.BlockSpec((hb, tq, 1), lambda h, i: (h, i, 0))],
        out_shape=[jax.ShapeDtypeStruct((T, hq * dv), F32), jax.ShapeDtypeStruct((hq, T, 1), F32)],
        compiler_params=_cparams(("parallel", "parallel")),
    )(*ins)


def attn_delta(name, cfg, o, do, lse, sink):
    T, tm, hq, dv = cfg.T, ATTN_TILE, cfg.hq, cfg.dv

    def body(*refs):
        if cfg.has_sink:
            o_ref, do_ref, lse_ref, sink_ref, delta_ref, dob_ref, dsink_ref = refs
        else:
            o_ref, do_ref, delta_ref, dob_ref = refs
        dob_ref[...] = do_ref[...].astype(BF16)
        parts = []
        for h in range(hq):
            delta = jnp.sum(do_ref[:, h * dv:(h + 1) * dv] * o_ref[:, h * dv:(h + 1) * dv], axis=-1, keepdims=True)
            delta_ref[h] = delta
            if cfg.has_sink:
                parts.append(-jnp.sum(jnp.exp2(sink_ref[h] * LOG2E - lse_ref[h]) * delta, axis=0, keepdims=True)[None])
        if cfg.has_sink:
            _accumulate(dsink_ref, jnp.concatenate(parts, axis=0), pl.program_id(0) == 0)

    head_tile = pl.BlockSpec((tm, hq * dv), lambda i: (i, 0))
    col = pl.BlockSpec((hq, tm, 1), lambda i: (0, i, 0))
    one = pl.BlockSpec((hq, 1, 1), lambda i: (0, 0, 0))
    in_specs, ins = [head_tile, head_tile], [o, do]
    out_specs = [col, head_tile]
    out_shape = [jax.ShapeDtypeStruct((hq, T, 1), F32), jax.ShapeDtypeStruct((T, hq * dv), BF16)]
    if cfg.has_sink:
        in_specs += [col, one]
        ins += [lse, sink]
        out_specs.append(one)
        out_shape.append(jax.ShapeDtypeStruct((hq, 1, 1), F32))
    return pl.pallas_call(body, name=name, grid=(T // tm,), in_specs=in_specs, out_specs=out_specs, out_shape=out_shape,
                          compiler_params=_cparams(("arbitrary",)))(*ins)


def flash_dq(name, cfg, q, k, v, dob, lse, delta):
    T, tq, c = cfg.T, ATTN_TILE, cfg.chunk
    hq, g, dq, dv, hb, kb = cfg.hq, cfg.group, cfg.dq, cfg.dv, cfg.q_block, cfg.kv_block

    def body(q_ref, k_ref, v_ref, do_ref, lse_ref, delta_ref, dq_ref):
        q0 = pl.program_id(1) * tq
        qs = [q_ref[:, hh * dq:(hh + 1) * dq] for hh in range(hb)]
        dos = [do_ref[:, hh * dv:(hh + 1) * dv] for hh in range(hb)]
        lat_lo, lat_n = _latent_chunks(cfg, q0)
        n = jnp.where(q0 >= cfg.L, 0, lat_n)
        rows_q = q0 + lax.broadcasted_iota(jnp.int32, (tq, 1), 0)
        to_log2 = cfg.scale * LOG2E

        def start(t):
            return pl.multiple_of(lat_lo + jnp.minimum(t, lat_n - 1) * c, cfg.align)

        def keys(ks, size, hh):
            kh = hh // g
            return k_ref[pl.ds(ks, size), kh * dq:(kh + 1) * dq]

        def products(ks, size):
            out = []
            for hh in range(hb):
                kh = hh // g
                out.append((lax.dot_general(qs[hh], keys(ks, size, hh), NT_DIMS, preferred_element_type=F32),
                            lax.dot_general(dos[hh], v_ref[pl.ds(ks, size), kh * dv:(kh + 1) * dv], NT_DIMS,
                                            preferred_element_type=F32)))
            return tuple(out)

        def update(accs, prods, ks, size, masked):
            new_accs = []
            for hh in range(hb):
                s, dp = prods[hh]
                p = jnp.exp2(s * to_log2 - lse_ref[hh])
                if masked:
                    rows_k = ks + lax.broadcasted_iota(jnp.int32, (1, size), 1)
                    p = jnp.where(_visible(cfg, rows_q, rows_k), p, 0.0)
                ds = (p * (dp - delta_ref[hh])).astype(BF16)
                new_accs.append(accs[hh] + jnp.dot(ds, keys(ks, size, hh), preferred_element_type=F32))
            return tuple(new_accs)

        def step(t, carry):
            accs, prods = carry
            nxt = products(start(t + 1), c)
            return update(accs, prods, start(t), c, cfg.window is not None), nxt

        accs = tuple(jnp.zeros((tq, dq), F32) for _ in range(hb))
        accs = update(accs, products(cfg.L, cfg.ctx_chunk), cfg.L, cfg.ctx_chunk, False)
        accs, _ = lax.fori_loop(0, n, step, (accs, products(start(0), c)))
        for hh in range(hb):
            dq_ref[:, hh * dq:(hh + 1) * dq] = accs[hh] * cfg.scale

    col = pl.BlockSpec((hb, tq, 1), lambda h, i: (h, i, 0))
    return pl.pallas_call(
        body, name=name, grid=(hq // hb, T // tq),
        in_specs=[pl.BlockSpec((tq, hb * dq), lambda h, i: (i, h)),
                  pl.BlockSpec((T, kb * dq), lambda h, i: (0, h)),
                  pl.BlockSpec((T, kb * dv), lambda h, i: (0, h)),
                  pl.BlockSpec((tq, hb * dv), lambda h, i: (i, h)), col, col],
        out_specs=pl.BlockSpec((tq, hb * dq), lambda h, i: (i, h)),
        out_shape=jax.ShapeDtypeStruct((T, hq * dq), F32),
        compiler_params=_cparams(("parallel", "parallel")),
    )(q, k, v, dob, lse, delta)


def flash_dkv(name, cfg, q, k, v, dob, lse, delta):
    T, L, tk, c, al, cc = cfg.T, cfg.L, ATTN_TILE, cfg.chunk, cfg.align, cfg.ctx_chunk
    hq, g, dq, dv, hb, kb = cfg.hq, cfg.group, cfg.dq, cfg.dv, cfg.q_block, cfg.kv_block
    hk = hq // g
    sub = c // al

    def body(k_ref, v_ref, q_ref, do_ref, lse_lat, delta_lat, lse_ctx, delta_ctx, dk_ref, dv_ref):
        k0 = pl.program_id(1) * tk
        kk = [k_ref[:, kh * dq:(kh + 1) * dq] for kh in range(kb)]
        vv = [v_ref[:, kh * dv:(kh + 1) * dv] for kh in range(kb)]
        is_ctx = k0 >= L
        rows_k = k0 + lax.broadcasted_iota(jnp.int32, (tk, 1), 0)
        to_log2 = cfg.scale * LOG2E
        lat_lo, lat_n = _latent_chunks(cfg, k0)
        if cfg.window is not None:
            lat_lo = jnp.where(is_ctx, 0, lat_lo)
            lat_n = jnp.where(is_ctx, L // c, lat_n)

        def start(t):
            return pl.multiple_of(lat_lo + jnp.minimum(t, lat_n - 1) * c, al)

        def operands(qs, size, hh):
            return q_ref[pl.ds(qs, size), hh * dq:(hh + 1) * dq], do_ref[pl.ds(qs, size), hh * dv:(hh + 1) * dv]

        def products(qs, size):
            out = []
            for hh in range(hb):
                kh = hh // g
                qc, dc = operands(qs, size, hh)
                out.append((lax.dot_general(kk[kh], qc, NT_DIMS, preferred_element_type=F32),
                            lax.dot_general(vv[kh], dc, NT_DIMS, preferred_element_type=F32)))
            return tuple(out)

        def update(accs, prods, qs, size, rows, masked):
            accs = [list(a) for a in accs]
            for hh in range(hb):
                kh = hh // g
                st, dpt = prods[hh]
                qc, dc = operands(qs, size, hh)
                lse_row, delta_row = rows(hh)
                pt = jnp.exp2(st * to_log2 - lse_row)
                if masked:
                    rows_q = qs + lax.broadcasted_iota(jnp.int32, (1, size), 1)
                    pt = jnp.where(_visible(cfg, rows_q, rows_k), pt, 0.0)
                accs[kh][1] = accs[kh][1] + jnp.dot(pt.astype(BF16), dc, preferred_element_type=F32)
                dst = (pt * (dpt - delta_row)).astype(BF16)
                accs[kh][0] = accs[kh][0] + jnp.dot(dst, qc, preferred_element_type=F32)
            return tuple(tuple(a) for a in accs)

        def step(t, carry):
            accs, prods = carry
            nxt = products(start(t + 1), c)
            qs = start(t)
            ci = qs // al

            def rows(hh):
                return (jnp.concatenate([lse_lat[hh, ci + j] for j in range(sub)], axis=1),
                        jnp.concatenate([delta_lat[hh, ci + j] for j in range(sub)], axis=1))

            return update(accs, prods, qs, c, rows, cfg.window is not None), nxt

        def ctx_step(_, accs):
            return update(accs, products(L, cc), L, cc, lambda hh: (lse_ctx[hh, 0], delta_ctx[hh, 0]), False)

        accs = tuple((jnp.zeros((tk, dq), F32), jnp.zeros((tk, dv), F32)) for _ in range(kb))
        accs, _ = lax.fori_loop(0, lat_n, step, (accs, products(start(0), c)))
        accs = lax.fori_loop(0, jnp.where(is_ctx, 1, 0), ctx_step, accs)
        for kh in range(kb):
            dk_ref[:, kh * dq:(kh + 1) * dq] = accs[kh][0] * cfg.scale
            dv_ref[:, kh * dv:(kh + 1) * dv] = accs[kh][1]

    def lanes(a):
        return a[:, :L].reshape(hq, L // al, 1, al), a[:, L:].reshape(hq, 1, 1, cc)

    lse_lat, lse_ctx = lanes(lse)
    delta_lat, delta_ctx = lanes(delta)
    lat_spec = pl.BlockSpec((hb, L // al, 1, al), lambda h, j: (h, 0, 0, 0))
    ctx_spec = pl.BlockSpec((hb, 1, 1, cc), lambda h, j: (h, 0, 0, 0))
    return pl.pallas_call(
        body, name=name, grid=(hk // kb, T // tk),
        in_specs=[pl.BlockSpec((tk, kb * dq), lambda h, j: (j, h)),
                  pl.BlockSpec((tk, kb * dv), lambda h, j: (j, h)),
                  pl.BlockSpec((T, hb * dq), lambda h, j: (0, h)),
                  pl.BlockSpec((T, hb * dv), lambda h, j: (0, h)),
                  lat_spec, lat_spec, ctx_spec, ctx_spec],
        out_specs=[pl.BlockSpec((tk, kb * dq), lambda h, j: (j, h)), pl.BlockSpec((tk, kb * dv), lambda h, j: (j, h))],
        out_shape=[jax.ShapeDtypeStruct((T, hk * dq), F32), jax.ShapeDtypeStruct((T, hk * dv), F32)],
        compiler_params=_cparams(("parallel", "parallel")),
    )(k, v, q, dob, lse_lat, delta_lat, lse_ctx, delta_ctx)


HALO = 8


def _conv_specs(tm, C, T):
    nb = tm // HALO
    last = T // HALO - 1
    return [pl.BlockSpec((HALO, C), lambda i: (jnp.maximum(i * nb - 1, 0), 0)),
            pl.BlockSpec((tm, C), lambda i: (i, 0)),
            pl.BlockSpec((HALO, C), lambda i: (jnp.minimum((i + 1) * nb, last), 0))]


def _extended(prev_ref, cur_ref, next_ref, i, tm, L, T):
    r0 = i * tm
    keep_prev = jnp.logical_and(r0 != 0, r0 != L).astype(F32)
    keep_next = jnp.logical_and(r0 + tm != L, r0 + tm != T).astype(F32)
    return jnp.concatenate([prev_ref[...] * keep_prev, cur_ref[...], next_ref[...] * keep_next], axis=0)


def _shift_rows(xe, d):
    n = xe.shape[0]
    return xe if d == 0 else pltpu.roll(xe, (-d) % n, 0)


def _conv_pre(xe, w_ref, b_ref):
    acc = b_ref[...] + w_ref[SSM_CONV // 2:SSM_CONV // 2 + 1, :] * xe
    for k in range(SSM_CONV):
        if k != SSM_CONV // 2:
            acc = acc + w_ref[k:k + 1, :] * _shift_rows(xe, k - SSM_CONV // 2)
    return acc


def conv_fwd(name, x, w, b, L, tm=ROW_TILE):
    T, C = x.shape

    def body(xp, xc, xn, w_ref, b_ref, o_ref):
        xe = _extended(xp, xc, xn, pl.program_id(0), tm, L, T)
        pre = _conv_pre(xe, w_ref, b_ref)[HALO:HALO + tm]
        o_ref[...] = pre * jax.nn.sigmoid(pre)

    full = lambda a: pl.BlockSpec(a.shape, lambda i: (0, 0))
    return pl.pallas_call(body, name=name, grid=(T // tm,), in_specs=_conv_specs(tm, C, T) + [full(w), full(b)],
                          out_specs=pl.BlockSpec((tm, C), lambda i: (i, 0)), out_shape=jax.ShapeDtypeStruct((T, C), F32),
                          compiler_params=_cparams(("parallel",)))(x, x, x, w, b)


def conv_bwd(name, x, w, b, gu, L, tm=ROW_TILE):
    T, C = x.shape

    def body(xp, xc, xn, gp, gc, gn, w_ref, b_ref, dx_ref, dw_ref, db_ref):
        i = pl.program_id(0)
        xe = _extended(xp, xc, xn, i, tm, L, T)
        ge = _extended(gp, gc, gn, i, tm, L, T)
        pre = _conv_pre(xe, w_ref, b_ref)
        sg = jax.nn.sigmoid(pre)
        gpre = ge * (sg * (1.0 + pre * (1.0 - sg)))
        half = SSM_CONV // 2
        dx = jnp.zeros((tm, C), F32)
        rows = []
        for k in range(SSM_CONV):
            dx = dx + w_ref[k:k + 1, :] * _shift_rows(gpre, half - k)[HALO:HALO + tm]
            rows.append(jnp.sum(gpre[HALO:HALO + tm] * _shift_rows(xe, k - half)[HALO:HALO + tm], axis=0, keepdims=True))
        dx_ref[...] = dx
        rows += [jnp.zeros((1, C), F32)] * (8 - SSM_CONV)
        _accumulate(dw_ref, jnp.concatenate(rows, axis=0), i == 0)
        _accumulate(db_ref, jnp.sum(gpre[HALO:HALO + tm], axis=0, keepdims=True), i == 0)

    full = lambda a: pl.BlockSpec(a.shape, lambda i: (0, 0))
    return pl.pallas_call(
        body, name=name, grid=(T // tm,), in_specs=_conv_specs(tm, C, T) * 2 + [full(w), full(b)],
        out_specs=[pl.BlockSpec((tm, C), lambda i: (i, 0)), pl.BlockSpec((8, C), lambda i: (0, 0)), pl.BlockSpec((1, C), lambda i: (0, 0))],
        out_shape=[jax.ShapeDtypeStruct((T, C), F32), jax.ShapeDtypeStruct((8, C), F32), jax.ShapeDtypeStruct((1, C), F32)],
        compiler_params=_cparams(("arbitrary",)))(x, x, x, gu, gu, gu, w, b)


SSM_PAIRS = SSM_HEADS // 2
TN_DIMS = (((0,), (0,)), ((), ()))
HIGHEST = lax.Precision.HIGHEST


def _ssd_chunk(direction, xps, bs, cs, dt_col, dt_row, alog_row, alog_col, hps):
    Q = SSM_CHUNK
    da_col = dt_col * (-jnp.exp(alog_row))
    da_row = dt_row * (-jnp.exp(alog_col))
    ii = lax.broadcasted_iota(jnp.int32, (Q, Q), 0)
    jj = lax.broadcasted_iota(jnp.int32, (Q, Q), 1)
    tri = (ii >= jj) if direction == 0 else (ii <= jj)
    trif = tri.astype(F32)
    acs_col = jnp.dot(trif, da_col, precision=HIGHEST, preferred_element_type=F32)
    acs_row = lax.dot_general(da_row, trif, NT_DIMS, precision=HIGHEST, preferred_element_type=F32)
    tot_col = jnp.sum(da_col, axis=0, keepdims=True)
    lane16 = lax.broadcasted_iota(jnp.int32, (1, SSM_HEADS), 1)
    sub16 = lax.broadcasted_iota(jnp.int32, (SSM_HEADS, 1), 0)
    low = lax.broadcasted_iota(jnp.int32, (1, 2 * SSM_HEAD_DIM), 1) < SSM_HEAD_DIM

    def col(v, h):
        return jnp.sum(v * (lane16 == h).astype(F32), axis=1, keepdims=True)

    def row(v, h):
        return jnp.sum(v * (sub16 == h).astype(F32), axis=0, keepdims=True)

    ys, hos = [], []
    pairs_per_group = SSM_PAIRS // SSM_GROUPS
    for g in range(SSM_GROUPS):
        bb, cb16 = bs[g].astype(BF16), cs[g].astype(BF16)
        cb = lax.dot_general(cb16, bb, NT_DIMS, preferred_element_type=F32)
        for pp in range(pairs_per_group):
            p = g * pairs_per_group + pp
            h0, h1 = 2 * p, 2 * p + 1
            ac0, ac1 = col(acs_col, h0), col(acs_col, h1)
            seg0 = jnp.exp(jnp.where(tri, ac0 - row(acs_row, h0), -jnp.inf))
            seg1 = jnp.exp(jnp.where(tri, ac1 - row(acs_row, h1), -jnp.inf))
            dt_l = jnp.where(low, col(dt_col, h0), col(dt_col, h1))
            ac_l = jnp.where(low, ac0, ac1)
            tot_l = jnp.where(low, col(tot_col, h0), col(tot_col, h1))
            xdt = xps[p] * dt_l
            y = (jnp.dot((cb * seg0).astype(BF16), jnp.where(low, xdt, 0.0).astype(BF16), preferred_element_type=F32)
                 + jnp.dot((cb * seg1).astype(BF16), jnp.where(low, 0.0, xdt).astype(BF16), preferred_element_type=F32))
            y = y + jnp.dot(cb16, hps[p].astype(BF16), preferred_element_type=F32) * jnp.exp(ac_l)
            st = lax.dot_general(bb, (xdt * jnp.exp(tot_l - ac_l)).astype(BF16), TN_DIMS, preferred_element_type=F32)
            ys.append(y)
            hos.append(hps[p] * jnp.exp(tot_l) + st)
    return tuple(ys), tuple(hos)


def _ssd_chunk_of(direction, step, ncl, ncc):
    if direction == 0:
        return jnp.where(step < ncc, ncl + step, step - ncc)
    return jnp.where(step < ncc, ncl + ncc - 1 - step, ncl - 1 - (step - ncc))


def _ssd_load(u_ref):
    Q = SSM_CHUNK
    xps = tuple(u_ref[:, LANES * p:LANES * (p + 1)] for p in range(SSM_PAIRS))
    bs = tuple(u_ref[:, SSM_INNER + SSM_STATE * g:SSM_INNER + SSM_STATE * (g + 1)] for g in range(SSM_GROUPS))
    c0 = SSM_INNER + SSM_GROUPS * SSM_STATE
    cs = tuple(u_ref[:, c0 + SSM_STATE * g:c0 + SSM_STATE * (g + 1)] for g in range(SSM_GROUPS))
    return xps, bs, cs


def ssd_fwd(name, direction, u, dt, dt_t, alog_row, alog_col, L):
    T = u.shape[0]
    Q, N = SSM_CHUNK, SSM_STATE
    ncl, ncc = L // Q, (T - L) // Q
    nc = ncl + ncc
    cm = lambda s: _ssd_chunk_of(direction, s, ncl, ncc)

    def body(u_ref, dt_ref, dtt_ref, ar_ref, ac_ref, y_ref, hin_ref, state):
        @pl.when(pl.program_id(0) == 0)
        def _():
            state[...] = jnp.zeros_like(state)

        xps, bs, cs = _ssd_load(u_ref)
        hps = tuple(state[p] for p in range(SSM_PAIRS))
        for p in range(SSM_PAIRS):
            hin_ref[0, p] = hps[p]
        ys, hos = _ssd_chunk(direction, xps, bs, cs, dt_ref[...], dtt_ref[...], ar_ref[...], ac_ref[...], hps)
        for p in range(SSM_PAIRS):
            y_ref[:, LANES * p:LANES * (p + 1)] = ys[p]
            state[p] = hos[p]

    return pl.pallas_call(
        body, name=name, grid=(nc,),
        in_specs=[pl.BlockSpec((Q, SSM_CONV_DIM), lambda s: (cm(s), 0)),
                  pl.BlockSpec((Q, SSM_HEADS), lambda s: (cm(s), 0)),
                  pl.BlockSpec((SSM_HEADS, Q), lambda s: (0, cm(s))),
                  pl.BlockSpec((1, SSM_HEADS), lambda s: (0, 0)),
                  pl.BlockSpec((SSM_HEADS, 1), lambda s: (0, 0))],
        out_specs=[pl.BlockSpec((Q, SSM_INNER), lambda s: (cm(s), 0)),
                   pl.BlockSpec((1, SSM_PAIRS, N, LANES), lambda s: (cm(s), 0, 0, 0))],
        out_shape=[jax.ShapeDtypeStruct((T, SSM_INNER), F32), jax.ShapeDtypeStruct((nc, SSM_PAIRS, N, LANES), F32)],
        scratch_shapes=[pltpu.VMEM((SSM_PAIRS, N, LANES), F32)],
        compiler_params=_cparams(("arbitrary",)),
    )(u, dt, dt_t, alog_row, alog_col)


def ssd_bwd(name, direction, u, dt, dt_t, alog_row, alog_col, hin, dy, L, add_x=None, add_u=None):
    T = u.shape[0]
    Q, N = SSM_CHUNK, SSM_STATE
    ncl, ncc = L // Q, (T - L) // Q
    nc = ncl + ncc
    cm = lambda s: _ssd_chunk_of(direction, nc - 1 - s, ncl, ncc)
    n_add = (add_x is not None) + (add_u is not None)

    def body(*refs):
        u_ref, dt_ref, dtt_ref, ar_ref, ac_ref, hin_ref, dy_ref = refs[:7]
        add_refs = refs[7:7 + n_add]
        du_ref, ddt_ref, ddtt_ref, dar_ref, dac_ref, dstate = refs[7 + n_add:]
        first = pl.program_id(0) == 0

        @pl.when(first)
        def _():
            dstate[...] = jnp.zeros_like(dstate)

        xps, bs, cs = _ssd_load(u_ref)
        hps = tuple(hin_ref[0, p] for p in range(SSM_PAIRS))
        _, vjp = jax.vjp(functools.partial(_ssd_chunk, direction), xps, bs, cs, dt_ref[...], dtt_ref[...], ar_ref[...],
                         ac_ref[...], hps)
        dys = tuple(dy_ref[:, LANES * p:LANES * (p + 1)] for p in range(SSM_PAIRS))
        dhs = tuple(dstate[p] for p in range(SSM_PAIRS))
        gx, gb, gc, gdt, gdtt, gar, gac, ghp = vjp((dys, dhs))
        parts = list(gx) + list(gb) + list(gc)
        du = jnp.concatenate(parts, axis=1)
        k = 0
        if add_x is not None:
            du = du + jnp.concatenate([add_refs[k][...], jnp.zeros((Q, SSM_CONV_DIM - SSM_INNER), F32)], axis=1)
            k += 1
        if add_u is not None:
            du = du + add_refs[k][...]
        du_ref[...] = du
        ddt_ref[...] = gdt
        ddtt_ref[...] = gdtt
        _accumulate(dar_ref, gar, first)
        _accumulate(dac_ref, gac, first)
        for p in range(SSM_PAIRS):
            dstate[p] = ghp[p]

    in_specs = [pl.BlockSpec((Q, SSM_CONV_DIM), lambda s: (cm(s), 0)),
                pl.BlockSpec((Q, SSM_HEADS), lambda s: (cm(s), 0)),
                pl.BlockSpec((SSM_HEADS, Q), lambda s: (0, cm(s))),
                pl.BlockSpec((1, SSM_HEADS), lambda s: (0, 0)),
                pl.BlockSpec((SSM_HEADS, 1), lambda s: (0, 0)),
                pl.BlockSpec((1, SSM_PAIRS, N, LANES), lambda s: (cm(s), 0, 0, 0)),
                pl.BlockSpec((Q, SSM_INNER), lambda s: (cm(s), 0))]
    ins = [u, dt, dt_t, alog_row, alog_col, hin, dy]
    if add_x is not None:
        in_specs.append(pl.BlockSpec((Q, SSM_INNER), lambda s: (cm(s), 0)))
        ins.append(add_x)
    if add_u is not None:
        in_specs.append(pl.BlockSpec((Q, SSM_CONV_DIM), lambda s: (cm(s), 0)))
        ins.append(add_u)
    return pl.pallas_call(
        body, name=name, grid=(nc,), in_specs=in_specs,
        out_specs=[pl.BlockSpec((Q, SSM_CONV_DIM), lambda s: (cm(s), 0)),
                   pl.BlockSpec((Q, SSM_HEADS), lambda s: (cm(s), 0)),
                   pl.BlockSpec((SSM_HEADS, Q), lambda s: (0, cm(s))),
                   pl.BlockSpec((1, SSM_HEADS), lambda s: (0, 0)),
                   pl.BlockSpec((SSM_HEADS, 1), lambda s: (0, 0))],
        out_shape=[jax.ShapeDtypeStruct((T, SSM_CONV_DIM), F32), jax.ShapeDtypeStruct((T, SSM_HEADS), F32),
                   jax.ShapeDtypeStruct((SSM_HEADS, T), F32), jax.ShapeDtypeStruct((1, SSM_HEADS), F32),
                   jax.ShapeDtypeStruct((SSM_HEADS, 1), F32)],
        scratch_shapes=[pltpu.VMEM((SSM_PAIRS, N, LANES), F32)],
        compiler_params=_cparams(("arbitrary",)),
    )(*ins)


PEER_MASKS = (1, 2, 4, 3, 5, 6, 7)
N_PEERS = len(PEER_MASKS)
MESH_IDS = pl.DeviceIdType.MESH


def _my_index():
    return lax.axis_index("x") * 4 + lax.axis_index("y") * 2 + lax.axis_index("c")


def _coords(idx):
    return (idx // 4, (idx // 2) % 2, idx % 2)


def all_gather_hbm(name, arrays):
    n = len(arrays)

    def body(*refs):
        ins, outs = refs[:n], refs[n:2 * n]
        send_sems, recv_sems, local_sems = refs[2 * n:]
        me = _my_index()
        copies = []
        for a in range(n):
            local = pltpu.make_async_copy(ins[a], outs[a].at[me], local_sems.at[a])
            local.start()
            copies.append(local)
            for k, mask in enumerate(PEER_MASKS):
                peer = me ^ mask
                cp = pltpu.make_async_remote_copy(src_ref=ins[a], dst_ref=outs[a].at[me], send_sem=send_sems.at[a * N_PEERS + k],
                                                  recv_sem=recv_sems.at[a * N_PEERS + k], device_id=_coords(peer),
                                                  device_id_type=MESH_IDS)
                cp.start()
                copies.append(cp)
        for cp in copies:
            cp.wait()

    any_spec = pl.BlockSpec(memory_space=pl.ANY)
    return pl.pallas_call(
        body, name=name, in_specs=[any_spec] * n, out_specs=[any_spec] * n,
        out_shape=[jax.ShapeDtypeStruct((N_DEV,) + a.shape, a.dtype) for a in arrays],
        scratch_shapes=[pltpu.SemaphoreType.DMA((n * N_PEERS,)), pltpu.SemaphoreType.DMA((n * N_PEERS,)),
                        pltpu.SemaphoreType.DMA((n,))],
    )(*arrays)


def exchange_hbm(name, arrays):
    n = len(arrays)

    def body(*refs):
        ins, outs = refs[:n], refs[n:2 * n]
        send_sems, recv_sems, local_sems = refs[2 * n:]
        me = _my_index()
        copies = []
        for a in range(n):
            local = pltpu.make_async_copy(ins[a].at[me], outs[a].at[me], local_sems.at[a])
            local.start()
            copies.append(local)
            for k, mask in enumerate(PEER_MASKS):
                peer = me ^ mask
                cp = pltpu.make_async_remote_copy(src_ref=ins[a].at[peer], dst_ref=outs[a].at[me],
                                                  send_sem=send_sems.at[a * N_PEERS + k], recv_sem=recv_sems.at[a * N_PEERS + k],
                                                  device_id=_coords(peer), device_id_type=MESH_IDS)
                cp.start()
                copies.append(cp)
        for cp in copies:
            cp.wait()

    any_spec = pl.BlockSpec(memory_space=pl.ANY)
    return pl.pallas_call(
        body, name=name, in_specs=[any_spec] * n, out_specs=[any_spec] * n,
        out_shape=[jax.ShapeDtypeStruct(a.shape, a.dtype) for a in arrays],
        scratch_shapes=[pltpu.SemaphoreType.DMA((n * N_PEERS,)), pltpu.SemaphoreType.DMA((n * N_PEERS,)),
                        pltpu.SemaphoreType.DMA((n,))],
    )(*arrays)


def all_gather_vmem(name, v):
    def body(v_ref, out_ref, send_sems, recv_sems):
        me = _my_index()
        out_ref[me] = v_ref[...]
        copies = []
        for k, mask in enumerate(PEER_MASKS):
            cp = pltpu.make_async_remote_copy(src_ref=v_ref, dst_ref=out_ref.at[me], send_sem=send_sems.at[k],
                                              recv_sem=recv_sems.at[k], device_id=_coords(me ^ mask), device_id_type=MESH_IDS)
            cp.start()
            copies.append(cp)
        for cp in copies:
            cp.wait()

    vm = pl.BlockSpec(memory_space=pltpu.VMEM)
    return pl.pallas_call(
        body, name=name, in_specs=[vm], out_specs=vm, out_shape=jax.ShapeDtypeStruct((N_DEV,) + v.shape, v.dtype),
        scratch_shapes=[pltpu.SemaphoreType.DMA((N_PEERS,)), pltpu.SemaphoreType.DMA((N_PEERS,))],
    )(v)


def _row_tile(rows, cols, bufs):
    budget = 24 * 1024 * 1024 // (bufs * 2 * 4 * max(cols, LANES))
    if rows <= budget:
        return rows
    for t in range(budget - budget % 8, 7, -8):
        if rows % t == 0:
            return t
    return rows


def sum_parts(name, parts):
    P, R, C = parts.shape
    tr = _row_tile(R, C, P + 1)

    def body(p_ref, o_ref):
        acc = p_ref[0]
        for s in range(1, P):
            acc = acc + p_ref[s]
        o_ref[...] = acc

    return pl.pallas_call(body, name=name, grid=(R // tr,), in_specs=[pl.BlockSpec((P, tr, C), lambda i: (0, i, 0))],
                          out_specs=pl.BlockSpec((tr, C), lambda i: (i, 0)), out_shape=jax.ShapeDtypeStruct((R, C), F32),
                          compiler_params=_cparams(("parallel",)))(parts)


def adamw(name, w, g, m, v):
    R, C = w.shape
    tr = _row_tile(R, C, 7)

    def body(w_ref, g_ref, m_ref, v_ref, d_ref, nm_ref, nv_ref):
        g = g_ref[...]
        nm = ADAM_B1 * m_ref[...] + (1.0 - ADAM_B1) * g
        nv = ADAM_B2 * v_ref[...] + (1.0 - ADAM_B2) * (g * g)
        m_hat = nm / (1.0 - ADAM_B1 ** ADAM_STEP)
        v_hat = nv / (1.0 - ADAM_B2 ** ADAM_STEP)
        d_ref[...] = -ADAM_LR * (m_hat / (jnp.sqrt(v_hat) + ADAM_EPS) + ADAM_WD * w_ref[...])
        nm_ref[...] = nm
        nv_ref[...] = nv

    spec = pl.BlockSpec((tr, C), lambda i: (i, 0))
    return pl.pallas_call(body, name=name, grid=(R // tr,), in_specs=[spec] * 4, out_specs=[spec] * 3,
                          out_shape=[jax.ShapeDtypeStruct((R, C), F32)] * 3, compiler_params=_cparams(("parallel",)))(w, g, m, v)


def loss_and_grad(name, x, target, L, tm=ROW_TILE):
    T, D = x.shape
    nlt = L // tm

    def body(x_ref, t_ref, loss_ref, dx_ref):
        i = pl.program_id(0)
        err = jnp.where(i < nlt, x_ref[...] - t_ref[...], 0.0)
        dx_ref[...] = err * (1.0 / D)
        part = 0.5 * jnp.sum(jnp.sum(err * err, axis=1, keepdims=True), axis=0, keepdims=True) * (1.0 / D)
        _accumulate(loss_ref, part, i == 0)

    return pl.pallas_call(
        body, name=name, grid=(T // tm,),
        in_specs=[pl.BlockSpec((tm, D), lambda i: (i, 0)), pl.BlockSpec((tm, D), lambda i: (jnp.minimum(i, nlt - 1), 0))],
        out_specs=[pl.BlockSpec((1, 1), lambda i: (0, 0)), pl.BlockSpec((tm, D), lambda i: (i, 0))],
        out_shape=[jax.ShapeDtypeStruct((1, 1), F32), jax.ShapeDtypeStruct((T, D), F32)],
        compiler_params=_cparams(("arbitrary",)))(x, target)


def small_fwd(name, fn, arrays, out_shapes):
    def body(*refs):
        res = fn(*[r[...] for r in refs[:len(arrays)]])
        for o_ref, r in zip(refs[len(arrays):], res):
            o_ref[...] = r

    return pl.pallas_call(body, name=name, out_shape=[jax.ShapeDtypeStruct(s, F32) for s in out_shapes])(*arrays)


def small_bwd(name, fn, arrays, cts):
    n = len(arrays)

    def body(*refs):
        _, vjp = jax.vjp(lambda *a: tuple(fn(*a)), *[r[...] for r in refs[:n]])
        grads = vjp(tuple(r[...] for r in refs[n:n + len(cts)]))
        for o_ref, g in zip(refs[n + len(cts):], grads):
            o_ref[...] = g

    return pl.pallas_call(body, name=name, out_shape=[jax.ShapeDtypeStruct(a.shape, F32) for a in arrays])(*arrays, *cts)


def fn_silu(x):
    return (x * jax.nn.sigmoid(x),)


FWD_NAMES = ["x", "c", "ctx", "c_ctx", "w_mod", "b_mod", "norm1_g", "norm2_g", "w_in", "ssm_conv_w", "ssm_conv_b",
             "ssm_dt_bias", "ssm_a_log", "ssm_d", "ssm_norm_g", "swa_q_norm_g", "swa_k_norm_g", "swa_sink", "mla_q_lat_g",
             "mla_kv_lat_g", "w_mla_uq", "w_mla_ukv", "mla_q_norm_g", "mla_k_norm_g", "w_p_ssm", "w_p_swa", "w_p_mla",
             "w_out", "w_ffn_in", "w_ffn_out"]
WEIGHT_NAMES = FWD_NAMES[3:]
GATHERED = ["w_in", "w_mla_uq", "w_mla_ukv", "w_p_ssm", "w_p_swa", "w_p_mla", "w_out", "w_ffn_in", "w_ffn_out"]
COLUMN_SHARDED = ("w_in", "w_mla_uq", "w_mla_ukv", "w_ffn_in")
REPLICATED = ["c_ctx", "b_mod", "norm1_g", "norm2_g", "ssm_conv_b", "ssm_dt_bias", "ssm_a_log", "ssm_d", "ssm_norm_g",
              "swa_q_norm_g", "swa_k_norm_g", "swa_sink", "mla_q_lat_g", "mla_kv_lat_g", "mla_q_norm_g", "mla_k_norm_g"]
IN_SEGS = [("xbc", SSM_CONV_DIM), ("dt", 2 * SSM_HEADS), ("ks", SWA_KV_HEADS * SWA_HEAD_DIM), ("vs", SWA_KV_HEADS * SWA_HEAD_DIM),
           ("ckv", MLA_KV_RANK), ("kr", MLA_ROPE), ("z", SSM_INNER), ("qs", SWA_Q_HEADS * SWA_HEAD_DIM), ("cq", MLA_Q_RANK),
           ("g1", None), ("g2", None), ("g3", None)]


def _pack(vectors, multiple):
    flat = jnp.concatenate([v.reshape(-1) for v in vectors])
    pad = (-flat.shape[0]) % multiple
    return jnp.pad(flat, (0, pad)).reshape(-1, LANES)


def _unpack(packed, shapes):
    flat, out, off = packed.reshape(-1), [], 0
    for s in shapes:
        n = int(np.prod(s))
        out.append(flat[off:off + n].reshape(s))
        off += n
    return out


def _rope_tables(L, T, rot_dim):
    nf = rot_dim // 4
    inv = jnp.power(ROPE_BASE, -jnp.arange(nf, dtype=F32) / nf)
    r, col = jnp.meshgrid(jnp.arange(L // GRID_W, dtype=F32), jnp.arange(GRID_W, dtype=F32), indexing="ij")
    ang = jnp.stack([r.reshape(-1)[:, None] * inv, col.reshape(-1)[:, None] * inv], axis=1)
    cos, sin = jnp.cos(ang), jnp.sin(ang)
    c = jnp.concatenate([cos[:, 0], cos[:, 0], cos[:, 1], cos[:, 1]], axis=1)
    s = jnp.concatenate([-sin[:, 0], sin[:, 0], -sin[:, 1], sin[:, 1]], axis=1)
    c = jnp.pad(c, ((0, T - L), (0, LANES - rot_dim)), constant_values=1.0)
    s = jnp.pad(s, ((0, T - L), (0, LANES - rot_dim)))
    return c, s


def _pad_rows(a, rows):
    return jnp.pad(a, ((0, rows - a.shape[0]), (0, 0)))


def _row(w, per_head=0, off=0, diff=True):
    return ("row", w, per_head, off, diff)


PAR, PAR_ND = ("par", True), ("par", False)
GRP = ("grp", True)


def kernel(*args):
    n_fwd, n_w = len(FWD_NAMES), len(WEIGHT_NAMES)
    inp = dict(zip(FWD_NAMES, args[:n_fwd]))
    loss_target = args[n_fwd]
    mom_m = dict(zip(WEIGHT_NAMES, args[n_fwd + 1:n_fwd + 1 + n_w]))
    mom_v = dict(zip(WEIGHT_NAMES, args[n_fwd + 1 + n_w:]))

    x, ctx = inp["x"][0], inp["ctx"][0]
    L, D = x.shape
    n_ctx = ctx.shape[0]
    T = L + n_ctx
    depth = inp["w_in"].shape[0]
    me = _my_index()
    in_widths = [w if w is not None else D for _, w in IN_SEGS]
    in_offs = np.concatenate([[0], np.cumsum(in_widths)]).tolist()
    ffn_h = inp["w_ffn_out"].shape[1] * N_DEV
    cfg_swa = AttnCfg(SWA_Q_HEADS, SWA_Q_HEADS // SWA_KV_HEADS, SWA_HEAD_DIM, SWA_HEAD_DIM, SWA_HEAD_DIM ** -0.5, SWA_WINDOW,
                      True, L, T, 256, 1)
    cfg_mla = AttnCfg(MLA_HEADS, 1, MLA_QK_PAD, MLA_V, MLA_QK ** -0.5, None, False, L, T, 1024, 2)

    def rf(name, fn, descs, arrays, outs, heads=1):
        return rowop_fwd(name, fn, descs, arrays, outs, T, L, heads=heads)

    def rb(name, fn, descs, arrays, outs, cts, heads=1, add=None):
        return rowop_bwd(name, fn, descs, arrays, outs, cts, T, L, heads=heads, add=add)

    local = []
    for n in GATHERED:
        w = inp[n]
        local.append((jnp.swapaxes(w, 1, 2) if n in COLUMN_SHARDED else w).astype(BF16))
    gathered = dict(zip(GATHERED, all_gather_hbm("gather_weights", local)))

    def full(n, l):
        g = gathered[n][:, l]
        return g.reshape(g.shape[0] * g.shape[1], g.shape[2])

    def layer_weights(l):
        wt = {}
        w_in_t = full("w_in", l)
        for (sn, _), o, w in zip(IN_SEGS, in_offs, in_widths):
            seg = w_in_t[o:o + w]
            wt[sn] = _pad_rows(seg, LANES) if sn == "kr" else seg
        uq = full("w_mla_uq", l).reshape(MLA_HEADS, MLA_QK, MLA_Q_RANK)
        wt["uqn"] = uq[:, :MLA_NOPE].reshape(MLA_HEADS * MLA_NOPE, MLA_Q_RANK)
        wt["uqr"] = jnp.pad(uq[:, MLA_NOPE:], ((0, 0), (0, LANES - MLA_ROPE), (0, 0))).reshape(MLA_HEADS * LANES, MLA_Q_RANK)
        ukv = full("w_mla_ukv", l).reshape(MLA_HEADS, MLA_NOPE + MLA_V, MLA_KV_RANK)
        wt["uk"] = ukv[:, :MLA_NOPE].reshape(MLA_HEADS * MLA_NOPE, MLA_KV_RANK)
        wt["uv"] = ukv[:, MLA_NOPE:].reshape(MLA_HEADS * MLA_V, MLA_KV_RANK)
        for n in ("w_p_ssm", "w_p_swa", "w_p_mla", "w_out", "w_ffn_out"):
            wt[n] = full(n, l)
        ffn_in_t = full("w_ffn_in", l)
        wt["fg"], wt["fu"] = ffn_in_t[:ffn_h], ffn_in_t[ffn_h:]
        return wt

    def layer_params(l):
        p = {}
        for n in ("norm1_g", "norm2_g", "ssm_conv_b", "ssm_norm_g", "swa_q_norm_g", "swa_k_norm_g", "mla_q_lat_g", "mla_kv_lat_g"):
            p[n] = inp[n][l][None]
        p["dt_bias"] = inp["ssm_dt_bias"][l].reshape(1, 2 * SSM_HEADS)
        p["alog_row"] = [inp["ssm_a_log"][l][d][None] for d in range(2)]
        p["alog_col"] = [inp["ssm_a_log"][l][d][:, None] for d in range(2)]
        p["d_lane"] = jnp.repeat(inp["ssm_d"][l], SSM_HEAD_DIM)[None]
        p["sink"] = inp["swa_sink"][l].reshape(SWA_Q_HEADS, 1, 1)
        for n, key in (("mla_q_norm_g", "gq"), ("mla_k_norm_g", "gk")):
            g = inp[n][l]
            p[key + "n"] = g[:MLA_NOPE][None]
            p[key + "r"] = jnp.pad(g[MLA_NOPE:], (0, LANES - MLA_ROPE))[None]
        return p

    conv_local = _pack([inp["ssm_conv_w"]], 8 * LANES)
    conv_all = all_gather_vmem("gather_conv_w", conv_local)
    cw = inp["ssm_conv_w"].shape
    conv_full = conv_all.reshape(N_DEV, -1)[:, :cw[0] * cw[1] * cw[2]].reshape(N_DEV, cw[0], cw[1], cw[2])
    conv_full = jnp.moveaxis(conv_full, 0, 2).reshape(cw[0], cw[1], N_DEV * cw[2])
    conv_w8 = jnp.pad(conv_full, ((0, 0), (0, 8 - cw[1]), (0, 0)))

    silu_c, silu_cc = small_fwd("silu_c", lambda a, b: fn_silu(a) + fn_silu(b), [inp["c"], inp["c_ctx"][None]], [(1, D), (1, D)])
    silu_all = all_gather_vmem("gather_silu_c", silu_c.reshape(D // LANES, LANES)).reshape(N_DEV, D)
    S_rows = 2 * N_DEV
    S_mat = jnp.concatenate([silu_all, silu_cc, jnp.zeros((S_rows - N_DEV - 1, D), F32)], axis=0)
    mod_cols = inp["w_mod"].shape[2]
    mods_local = []
    for l in range(depth):
        bias = lax.dynamic_slice(inp["b_mod"][l], (me * mod_cols,), (mod_cols,))
        mods_local.append(matmul(S_mat, inp["w_mod"][l], "nn", f"mod{l}", add=jnp.broadcast_to(bias[None], (S_rows, mod_cols))))
    mods_all = all_gather_vmem("gather_mods", jnp.stack(mods_local).reshape(-1, LANES))
    mods_all = jnp.moveaxis(mods_all.reshape(N_DEV, depth, S_rows, mod_cols), 0, 2).reshape(depth, S_rows, N_DEV * mod_cols)
    mods_lat = lax.dynamic_slice(mods_all, (0, me, 0), (depth, 1, N_DEV * mod_cols))[:, 0]
    mods_ctx = mods_all[:, N_DEV]

    def layer_mods(l):
        return [jnp.stack([mods_lat[l, j * D:(j + 1) * D], mods_ctx[l, j * D:(j + 1) * D]])[:, None] for j in range(6)]

    cs_swa = _rope_tables(L, T, SWA_HEAD_DIM)
    cs_mla = _rope_tables(L, T, MLA_ROPE)
    nm_descs = [_row(D), PAR, GRP, GRP]
    resid_descs = [_row(D, diff=False), _row(D), GRP]
    tab = [_row(LANES, diff=False), _row(LANES, diff=False)]
    swaq_descs = [_row(SWA_HEAD_DIM, 1), PAR] + tab
    swakv_descs = [_row(SWA_HEAD_DIM, 1), _row(SWA_HEAD_DIM, 1), PAR] + tab
    mlaq_descs = [_row(LANES, 1), _row(LANES, 1), PAR, PAR] + tab
    mlakv_descs = [_row(LANES, 1), _row(LANES, 1), _row(LANES), PAR, PAR] + tab
    ssdout_descs = [_row(SSM_INNER), _row(SSM_INNER, diff=False), _row(SSM_INNER), _row(SSM_INNER), PAR, PAR]
    merge_descs = [_row(D)] * 6
    swiglu_descs = [_row(ffn_h), _row(ffn_h)]
    seg_names = [sn for sn, _ in IN_SEGS]

    def layer_fwd(l, X, wt, p, mods):
        sh1, sc1, gt1, sh2, sc2, gt2 = mods
        r = {"X": X}
        r["h1"] = rf(f"l{l}_norm1", fn_norm_mod, nm_descs, [X, p["norm1_g"], sh1, sc1], [(D, 0, BF16)])[0]
        for sn in seg_names:
            r[sn] = matmul(r["h1"], wt[sn], "nt", f"l{l}_in_{sn}")
        r["u"] = conv_fwd(f"l{l}_conv", r["xbc"], conv_w8[l], p["ssm_conv_b"], L)
        r["dts"] = rf(f"l{l}_softplus", fn_softplus, [_row(2 * SSM_HEADS), PAR], [r["dt"], p["dt_bias"]], [(2 * SSM_HEADS, 0, F32)])[0]
        for d in range(2):
            dt_d = r["dts"][:, d * SSM_HEADS:(d + 1) * SSM_HEADS]
            r[f"dt{d}"], r[f"dtt{d}"] = dt_d, dt_d.T
            r[f"y{d}"], r[f"hin{d}"] = ssd_fwd(f"l{l}_ssd{d}", d, r["u"], dt_d, dt_d.T, p["alog_row"][d], p["alog_col"][d], L)
        r["ys"] = rf(f"l{l}_ssd_out", fn_ssd_out, ssdout_descs, [r["y0"], r["y1"], r["u"], r["z"], p["d_lane"], p["ssm_norm_g"]],
                     [(SSM_INNER, 0, F32)])[0]
        r["Qs"] = rf(f"l{l}_swa_q", fn_swa_q, swaq_descs, [r["qs"], p["swa_q_norm_g"], *cs_swa], [(SWA_HEAD_DIM, 1, BF16)], SWA_Q_HEADS)[0]
        r["Ks"], r["Vs"] = rf(f"l{l}_swa_kv", fn_swa_kv, swakv_descs, [r["ks"], r["vs"], p["swa_k_norm_g"], *cs_swa],
                              [(SWA_HEAD_DIM, 1, BF16), (SWA_HEAD_DIM, 1, BF16)], SWA_KV_HEADS)
        r["Os"], r["lse_s"] = flash_fwd(f"l{l}_swa_fwd", cfg_swa, r["Qs"], r["Ks"], r["Vs"], p["sink"])
        r["cqn"] = rf(f"l{l}_q_lat", fn_rms, [_row(MLA_Q_RANK), PAR], [r["cq"], p["mla_q_lat_g"]], [(MLA_Q_RANK, 0, BF16)])[0]
        r["qn"] = matmul(r["cqn"], wt["uqn"], "nt", f"l{l}_uqn")
        r["qr"] = matmul(r["cqn"], wt["uqr"], "nt", f"l{l}_uqr")
        r["Qm"] = rf(f"l{l}_mla_q", fn_mla_q, mlaq_descs, [r["qn"], r["qr"], p["gqn"], p["gqr"], *cs_mla], [(MLA_QK_PAD, 1, BF16)], MLA_HEADS)[0]
        r["ckvn"] = rf(f"l{l}_kv_lat", fn_rms, [_row(MLA_KV_RANK), PAR], [r["ckv"], p["mla_kv_lat_g"]], [(MLA_KV_RANK, 0, BF16)])[0]
        r["kn"] = matmul(r["ckvn"], wt["uk"], "nt", f"l{l}_uk")
        r["vp"] = matmul(r["ckvn"], wt["uv"], "nt", f"l{l}_uv")
        r["Km"], r["Vm"] = rf(f"l{l}_mla_kv", fn_mla_kv, mlakv_descs, [r["kn"], r["vp"], r["kr"], p["gkn"], p["gkr"], *cs_mla],
                              [(MLA_QK_PAD, 1, BF16), (MLA_V, 1, BF16)], MLA_HEADS)
        r["Om"], r["lse_m"] = flash_fwd(f"l{l}_mla_fwd", cfg_mla, r["Qm"], r["Km"], r["Vm"], None)
        r["P1"] = matmul(r["ys"], wt["w_p_ssm"], "nn", f"l{l}_p_ssm")
        r["P2"] = matmul(r["Os"], wt["w_p_swa"], "nn", f"l{l}_p_swa")
        r["P3"] = matmul(r["Om"], wt["w_p_mla"], "nn", f"l{l}_p_mla")
        r["mg"] = rf(f"l{l}_merge", fn_merge, merge_descs, [r["g1"], r["g2"], r["g3"], r["P1"], r["P2"], r["P3"]], [(D, 0, BF16)])[0]
        r["A"] = matmul(r["mg"], wt["w_out"], "nn", f"l{l}_out")
        r["X1"] = rf(f"l{l}_resid1", fn_resid, resid_descs, [X, r["A"], gt1], [(D, 0, F32)])[0]
        r["h2"] = rf(f"l{l}_norm2", fn_norm_mod, nm_descs, [r["X1"], p["norm2_g"], sh2, sc2], [(D, 0, BF16)])[0]
        r["Fg"] = matmul(r["h2"], wt["fg"], "nt", f"l{l}_ffn_g")
        r["Fu"] = matmul(r["h2"], wt["fu"], "nt", f"l{l}_ffn_u")
        r["sg"] = rf(f"l{l}_swiglu", fn_swiglu, swiglu_descs, [r["Fg"], r["Fu"]], [(ffn_h, 0, BF16)])[0]
        r["B"] = matmul(r["sg"], wt["w_ffn_out"], "nn", f"l{l}_ffn_out")
        X2 = rf(f"l{l}_resid2", fn_resid, resid_descs, [r["X1"], r["B"], gt2], [(D, 0, F32)])[0]
        return X2, r

    def attn_bwd(tag, cfg, q, k, v, o, lse, do, sink):
        res = attn_delta(f"{tag}_delta", cfg, o, do, lse, sink)
        delta, dob = res[0], res[1]
        dsink = res[2] if cfg.has_sink else None
        dq = flash_dq(f"{tag}_dq", cfg, q, k, v, dob, lse, delta)
        dk, dv = flash_dkv(f"{tag}_dkv", cfg, q, k, v, dob, lse, delta)
        return dq, dk, dv, dsink

    def layer_bwd(l, dX2, r, wt, p, mods):
        sh1, sc1, gt1, sh2, sc2, gt2 = mods
        g, gw = {}, {}
        dmod = [None] * 6
        dB, dmod[5] = rb(f"l{l}_resid2_b", fn_resid, resid_descs, [r["X1"], r["B"], gt2], [(D, 0, F32)], [dX2])
        dsg = matmul(dB, wt["w_ffn_out"], "nt", f"l{l}_ffn_out_da")
        gw["w_ffn_out"] = matmul(r["sg"], dB, "tn", f"l{l}_ffn_out_dw")
        dFg, dFu = rb(f"l{l}_swiglu_b", fn_swiglu, swiglu_descs, [r["Fg"], r["Fu"]], [(ffn_h, 0, BF16)], [dsg])
        dh2 = matmul(dFg, wt["fg"], "nn", f"l{l}_ffn_g_da")
        dh2 = matmul(dFu, wt["fu"], "nn", f"l{l}_ffn_u_da", add=dh2)
        gw["w_ffn_in"] = jnp.concatenate([matmul(r["h2"], dFg, "tn", f"l{l}_ffn_g_dw"), matmul(r["h2"], dFu, "tn", f"l{l}_ffn_u_dw")], axis=1)
        dX1, g["norm2_g"], dmod[3], dmod[4] = rb(f"l{l}_norm2_b", fn_norm_mod, nm_descs, [r["X1"], p["norm2_g"], sh2, sc2],
                                                [(D, 0, BF16)], [dh2], add={0: dX2})
        dA, dmod[2] = rb(f"l{l}_resid1_b", fn_resid, resid_descs, [r["X"], r["A"], gt1], [(D, 0, F32)], [dX1])
        dmg = matmul(dA, wt["w_out"], "nt", f"l{l}_out_da")
        gw["w_out"] = matmul(r["mg"], dA, "tn", f"l{l}_out_dw")
        dsegs = {}
        dsegs["g1"], dsegs["g2"], dsegs["g3"], dP1, dP2, dP3 = rb(
            f"l{l}_merge_b", fn_merge, merge_descs, [r["g1"], r["g2"], r["g3"], r["P1"], r["P2"], r["P3"]], [(D, 0, BF16)], [dmg])
        dys = matmul(dP1, wt["w_p_ssm"], "nt", f"l{l}_p_ssm_da")
        dOs = matmul(dP2, wt["w_p_swa"], "nt", f"l{l}_p_swa_da")
        dOm = matmul(dP3, wt["w_p_mla"], "nt", f"l{l}_p_mla_da")
        gw["w_p_ssm"] = matmul(r["ys"], dP1, "tn", f"l{l}_p_ssm_dw")
        gw["w_p_swa"] = matmul(r["Os"], dP2, "tn", f"l{l}_p_swa_dw")
        gw["w_p_mla"] = matmul(r["Om"], dP3, "tn", f"l{l}_p_mla_dw")
        dQm, dKm, dVm, _ = attn_bwd(f"l{l}_mla", cfg_mla, r["Qm"], r["Km"], r["Vm"], r["Om"], r["lse_m"], dOm, None)
        dkn, dvp, dsegs["kr"], dgkn, dgkr = rb(f"l{l}_mla_kv_b", fn_mla_kv, mlakv_descs,
                                               [r["kn"], r["vp"], r["kr"], p["gkn"], p["gkr"], *cs_mla],
                                               [(MLA_QK_PAD, 1, BF16), (MLA_V, 1, BF16)], [dKm, dVm], MLA_HEADS)
        dckvn = matmul(dkn, wt["uk"], "nn", f"l{l}_uk_da")
        dckvn = matmul(dvp, wt["uv"], "nn", f"l{l}_uv_da", add=dckvn)
        dw_uk = matmul(r["ckvn"], dkn, "tn", f"l{l}_uk_dw").reshape(MLA_KV_RANK, MLA_HEADS, MLA_NOPE)
        dw_uv = matmul(r["ckvn"], dvp, "tn", f"l{l}_uv_dw").reshape(MLA_KV_RANK, MLA_HEADS, MLA_V)
        gw["w_mla_ukv"] = jnp.concatenate([dw_uk, dw_uv], axis=2).reshape(MLA_KV_RANK, -1)
        dsegs["ckv"], g["mla_kv_lat_g"] = rb(f"l{l}_kv_lat_b", fn_rms, [_row(MLA_KV_RANK), PAR], [r["ckv"], p["mla_kv_lat_g"]],
                                             [(MLA_KV_RANK, 0, BF16)], [dckvn])
        dqn, dqr, dgqn, dgqr = rb(f"l{l}_mla_q_b", fn_mla_q, mlaq_descs, [r["qn"], r["qr"], p["gqn"], p["gqr"], *cs_mla],
                                  [(MLA_QK_PAD, 1, BF16)], [dQm], MLA_HEADS)
        dcqn = matmul(dqn, wt["uqn"], "nn", f"l{l}_uqn_da")
        dcqn = matmul(dqr, wt["uqr"], "nn", f"l{l}_uqr_da", add=dcqn)
        dw_uqn = matmul(r["cqn"], dqn, "tn", f"l{l}_uqn_dw").reshape(MLA_Q_RANK, MLA_HEADS, MLA_NOPE)
        dw_uqr = matmul(r["cqn"], dqr, "tn", f"l{l}_uqr_dw").reshape(MLA_Q_RANK, MLA_HEADS, LANES)[:, :, :MLA_ROPE]
        gw["w_mla_uq"] = jnp.concatenate([dw_uqn, dw_uqr], axis=2).reshape(MLA_Q_RANK, -1)
        dsegs["cq"], g["mla_q_lat_g"] = rb(f"l{l}_q_lat_b", fn_rms, [_row(MLA_Q_RANK), PAR], [r["cq"], p["mla_q_lat_g"]],
                                           [(MLA_Q_RANK, 0, BF16)], [dcqn])
        g["mla_q_norm_g"] = jnp.concatenate([dgqn[0], dgqr[0, :MLA_ROPE]])
        g["mla_k_norm_g"] = jnp.concatenate([dgkn[0], dgkr[0, :MLA_ROPE]])
        dQs, dKs, dVs, dsink = attn_bwd(f"l{l}_swa", cfg_swa, r["Qs"], r["Ks"], r["Vs"], r["Os"], r["lse_s"], dOs, p["sink"])
        g["swa_sink"] = dsink.reshape(SWA_Q_HEADS)
        dsegs["qs"], g["swa_q_norm_g"] = rb(f"l{l}_swa_q_b", fn_swa_q, swaq_descs, [r["qs"], p["swa_q_norm_g"], *cs_swa],
                                            [(SWA_HEAD_DIM, 1, BF16)], [dQs], SWA_Q_HEADS)
        dsegs["ks"], dsegs["vs"], g["swa_k_norm_g"] = rb(f"l{l}_swa_kv_b", fn_swa_kv, swakv_descs,
                                                         [r["ks"], r["vs"], p["swa_k_norm_g"], *cs_swa],
                                                         [(SWA_HEAD_DIM, 1, BF16), (SWA_HEAD_DIM, 1, BF16)], [dKs, dVs], SWA_KV_HEADS)
        dy, dxs, dsegs["z"], dd_lane, g["ssm_norm_g"] = rb(
            f"l{l}_ssd_out_b", fn_ssd_out, ssdout_descs, [r["y0"], r["y1"], r["u"], r["z"], p["d_lane"], p["ssm_norm_g"]],
            [(SSM_INNER, 0, F32)], [dys])
        g["ssm_d"] = dd_lane.reshape(SSM_HEADS, SSM_HEAD_DIM).sum(axis=1)
        du, ddts, dalog = None, [], []
        for d in range(2):
            du, ddt, ddtt, dar, dac = ssd_bwd(f"l{l}_ssd{d}_b", d, r["u"], r[f"dt{d}"], r[f"dtt{d}"], p["alog_row"][d], p["alog_col"][d],
                                              r[f"hin{d}"], dy, L, add_x=dxs if d == 0 else None, add_u=du)
            ddts.append(ddt + ddtt.T)
            dalog.append(dar[0] + dac[:, 0])
        g["ssm_a_log"] = jnp.stack(dalog)
        dsegs["xbc"], dconv_w, g["ssm_conv_b"] = conv_bwd(f"l{l}_conv_b", r["xbc"], conv_w8[l], p["ssm_conv_b"], du, L)
        dsegs["dt"], ddt_bias = rb(f"l{l}_softplus_b", fn_softplus, [_row(2 * SSM_HEADS), PAR], [r["dt"], p["dt_bias"]],
                                   [(2 * SSM_HEADS, 0, F32)], [jnp.concatenate(ddts, axis=1)])
        g["ssm_dt_bias"] = ddt_bias.reshape(2, SSM_HEADS)
        g["ssm_conv_w"] = dconv_w[:SSM_CONV]
        dh1, dws = None, []
        for sn, w in zip(seg_names, in_widths):
            dh1 = matmul(dsegs[sn], wt[sn], "nn", f"l{l}_in_{sn}_da", add=dh1)
            dws.append(matmul(r["h1"], dsegs[sn], "tn", f"l{l}_in_{sn}_dw")[:, :w])
        gw["w_in"] = jnp.concatenate(dws, axis=1)
        dX, g["norm1_g"], dmod[0], dmod[1] = rb(f"l{l}_norm1_b", fn_norm_mod, nm_descs, [r["X"], p["norm1_g"], sh1, sc1],
                                               [(D, 0, BF16)], [dh1], add={0: dX1})
        for n in ("norm1_g", "norm2_g", "ssm_conv_b", "ssm_norm_g", "swa_q_norm_g", "swa_k_norm_g", "mla_q_lat_g", "mla_kv_lat_g"):
            g[n] = g[n][0]
        dmod_lat = jnp.concatenate([dm[0, 0] for dm in dmod])
        dmod_ctx = jnp.concatenate([dm[1, 0] for dm in dmod])
        return dX, g, gw, dmod_lat, dmod_ctx

    X = jnp.concatenate([x, ctx], axis=0)
    saved = []
    for l in range(depth):
        wt, p, mods = layer_weights(l), layer_params(l), layer_mods(l)
        X, r = layer_fwd(l, X, wt, p, mods)
        saved.append((r, wt, p, mods))
    loss_part, dX = loss_and_grad("loss", X, loss_target[0], L)
    loss = lax.psum(loss_part[0, 0], ("x", "y", "c"))
    small_g = [None] * depth
    big_g = [None] * depth
    dmods = [None] * depth
    for l in reversed(range(depth)):
        r, wt, p, mods = saved[l]
        dX, small_g[l], big_g[l], dm_lat, dm_ctx = layer_bwd(l, dX, r, wt, p, mods)
        dmods[l] = jnp.stack([dm_lat, dm_ctx])
    grad_x = dX[:L][None]

    dm_all = all_gather_vmem("gather_dmods", jnp.stack(dmods).reshape(-1, LANES)).reshape(N_DEV, depth, 2, N_DEV * mod_cols)
    dm_rows = jnp.concatenate([jnp.moveaxis(dm_all[:, :, 0], 0, 1), dm_all[:, :, 1].sum(axis=0)[:, None],
                               jnp.zeros((depth, S_rows - N_DEV - 1, N_DEV * mod_cols), F32)], axis=1)
    dm_mine = lax.dynamic_slice(dm_rows, (0, 0, me * mod_cols), (depth, S_rows, mod_cols))
    grads = {}
    grads["w_mod"] = jnp.stack([matmul(S_mat, dm_mine[l], "tn", f"mod{l}_dw") for l in range(depth)])
    d_silu = None
    for l in range(depth):
        d_silu = matmul(dm_mine[l], inp["w_mod"][l], "nt", f"mod{l}_da", add=d_silu)
    small = {n: jnp.stack([small_g[l][n] for l in range(depth)]) for n in small_g[0]}
    small["c_ctx"] = small_bwd("silu_c_b", fn_silu, [inp["c_ctx"][None]], [d_silu[N_DEV:N_DEV + 1]])[0][0]
    small["b_mod"] = jnp.stack(dmods).sum(axis=1)

    rep_shapes = [inp[n].shape for n in REPLICATED]
    conv_shape = (depth, SSM_CONV, SSM_CONV_DIM)
    packed = _pack([small[n] for n in REPLICATED] + [small["ssm_conv_w"]], 8 * LANES)
    small_sum = sum_parts("sum_small", all_gather_vmem("gather_small", packed))
    summed = _unpack(small_sum, rep_shapes + [conv_shape])
    for n, gsum in zip(REPLICATED, summed):
        grads[n] = gsum
    grads["ssm_conv_w"] = lax.dynamic_slice(summed[-1], (0, 0, me * cw[2]), cw)

    slabs = []
    for n in GATHERED:
        gfull = jnp.stack([big_g[l][n] for l in range(depth)])
        if n in COLUMN_SHARDED:
            k_dim, n_dim = gfull.shape[1], gfull.shape[2]
            slabs.append(jnp.moveaxis(gfull.reshape(depth, k_dim, N_DEV, n_dim // N_DEV), 2, 0))
        else:
            k_dim, n_dim = gfull.shape[1], gfull.shape[2]
            slabs.append(jnp.moveaxis(gfull.reshape(depth, N_DEV, k_dim // N_DEV, n_dim), 1, 0))
    for n, parts in zip(GATHERED, exchange_hbm("exchange_grads", slabs)):
        shp = inp[n].shape
        grads[n] = sum_parts(f"sum_{n}", parts.reshape(N_DEV, shp[0] * shp[1], shp[2])).reshape(shp)

    delta, new_m, new_v = {}, {}, {}
    rep_pack = lambda d: _pack([d[n] for n in REPLICATED], 8 * LANES)
    rep_out = adamw("adamw_small", rep_pack(inp), rep_pack(grads), rep_pack(mom_m), rep_pack(mom_v))
    for out, res in zip((delta, new_m, new_v), rep_out):
        for n, a in zip(REPLICATED, _unpack(res, rep_shapes)):
            out[n] = a
    for n in ["w_mod", "ssm_conv_w"] + GATHERED:
        shp = inp[n].shape
        two_d = (shp[0] * shp[1], shp[2])
        res = adamw(f"adamw_{n}", inp[n].reshape(two_d), grads[n].reshape(two_d), mom_m[n].reshape(two_d), mom_v[n].reshape(two_d))
        delta[n], new_m[n], new_v[n] = [a.reshape(shp) for a in res]

    return (loss, grad_x, *[grads[n] for n in WEIGHT_NAMES], *[delta[n] for n in WEIGHT_NAMES],
            *[new_m[n] for n in WEIGHT_NAMES], *[new_v[n] for n in WEIGHT_NAMES])
```

```python
import functools
import math

import numpy as np
import jax
import jax.numpy as jnp
from jax import lax
from jax.experimental import pallas as pl
from jax.experimental.pallas import tpu as pltpu

F32 = jnp.float32
BF16 = jnp.bfloat16

N_DEV = 8
V7X_VMEM_BYTES = 64 * 1024 * 1024
VMEM_LIMIT_BYTES = V7X_VMEM_BYTES - 8 * 1024 * 1024
LANES = 128

EPS = 1e-6
ROPE_BASE = 10000.0
GRID_W = 64
SSM_HEADS, SSM_HEAD_DIM, SSM_GROUPS, SSM_STATE, SSM_CONV, SSM_CHUNK = 16, 64, 2, 128, 5, 128
SSM_INNER = SSM_HEADS * SSM_HEAD_DIM
SSM_CONV_DIM = SSM_INNER + 2 * SSM_GROUPS * SSM_STATE
SWA_Q_HEADS, SWA_KV_HEADS, SWA_HEAD_DIM, SWA_WINDOW = 8, 2, 128, 128
MLA_HEADS, MLA_Q_RANK, MLA_KV_RANK, MLA_NOPE, MLA_ROPE, MLA_V = 8, 384, 256, 128, 64, 128
MLA_QK = MLA_NOPE + MLA_ROPE
MLA_QK_PAD = 2 * LANES
ADAM_LR, ADAM_B1, ADAM_B2, ADAM_EPS, ADAM_WD, ADAM_STEP = 0.001, 0.9, 0.999, 1e-08, 0.01, 10

ROW_TILE = 256


def _cparams(sem, **kw):
    return pltpu.CompilerParams(dimension_semantics=sem, vmem_limit_bytes=VMEM_LIMIT_BYTES, **kw)


def _pick(dim, prefs):
    for p in prefs:
        if dim % p == 0:
            return p
    return dim


def matmul(a, b, mode, name, out_dtype=F32, add=None):
    if mode == "nn":
        (M, K), (K2, N) = a.shape, b.shape
    elif mode == "nt":
        (M, K), (N, K2) = a.shape, b.shape
    else:
        (K, M), (K2, N) = a.shape, b.shape
    assert K == K2, (name, a.shape, b.shape)
    has_add = add is not None
    tm, tn, tk = _matmul_tiles(M, N, K, a.dtype.itemsize, b.dtype.itemsize, jnp.dtype(out_dtype).itemsize, has_add)
    nk = K // tk
    dims = {"nn": (((1,), (0,)), ((), ())), "nt": (((1,), (1,)), ((), ())), "tn": (((0,), (0,)), ((), ()))}[mode]
    a_spec = pl.BlockSpec((tk, tm), lambda i, j, k: (k, i)) if mode == "tn" else pl.BlockSpec((tm, tk), lambda i, j, k: (i, k))
    b_spec = pl.BlockSpec((tn, tk), lambda i, j, k: (j, k)) if mode == "nt" else pl.BlockSpec((tk, tn), lambda i, j, k: (k, j))
    o_spec = pl.BlockSpec((tm, tn), lambda i, j, k: (i, j))

    def body(*refs):
        a_ref, b_ref = refs[:2]
        c_ref = refs[2] if has_add else None
        o_ref = refs[3] if has_add else refs[2]
        part = lax.dot_general(a_ref[...].astype(BF16), b_ref[...].astype(BF16), dims, preferred_element_type=F32)
        if nk == 1:
            o_ref[...] = (part + c_ref[...] if has_add else part).astype(o_ref.dtype)
            return
        acc_ref = refs[-1]
        k = pl.program_id(2)

        @pl.when(k == 0)
        def _():
            acc_ref[...] = part + c_ref[...] if has_add else part

        @pl.when(k > 0)
        def _():
            acc_ref[...] += part

        @pl.when(k == nk - 1)
        def _():
            o_ref[...] = acc_ref[...].astype(o_ref.dtype)

    ins = [a, b] + ([add] if has_add else [])
    in_specs = [a_spec, b_spec] + ([o_spec] if has_add else [])
    return pl.pallas_call(
        body, name=name, grid=(M // tm, N // tn, nk), in_specs=in_specs, out_specs=o_spec,
        out_shape=jax.ShapeDtypeStruct((M, N), out_dtype),
        scratch_shapes=[pltpu.VMEM((tm, tn), F32)] if nk > 1 else [],
        input_output_aliases=({2: 0} if has_add else {}),
        compiler_params=_cparams(("parallel", "parallel", "arbitrary")),
    )(*ins)


MATMUL_VMEM_BUDGET = 36 * 1024 * 1024


def _matmul_tiles(M, N, K, a_bytes, b_bytes, o_bytes, has_add):
    tk = K if K <= 1536 else _pick(K, (1408, 1024, 768, 704, 512, 256))
    nk = K // tk
    m_cands = [t for t in (1024, 768, 512, 384, 256, 128) if M % t == 0] or [M]
    n_cands = [t for t in range(LANES, min(N, 2816) + 1, LANES) if N % t == 0] or [N]
    best = None
    for tm in m_cands:
        for tn in n_cands:
            pipeline = 2 * (tm * tk * a_bytes + tk * tn * b_bytes + tm * tn * o_bytes) + (2 * tm * tn * 4 if has_add else 0)
            temps = tm * tn * 4 * (2 if nk > 1 else 1) + (tm * tk * 2 if a_bytes == 4 else 0) + (tk * tn * 2 if b_bytes == 4 else 0)
            if pipeline + temps <= MATMUL_VMEM_BUDGET:
                score = (tm * tn, tn)
                if best is None or score > best[0]:
                    best = (score, tm, tn)
    if best is None:
        return m_cands[-1], n_cands[0], tk
    return best[1], best[2], tk


def _row_specs(descs, arrays, tm, nct, heads):
    specs = []
    for d, arr in zip(descs, arrays):
        if d[0] == "row":
            _, w, per_head, off, _ = d
            specs.append(pl.BlockSpec((tm, w * (heads if per_head else 1)), lambda i, off=off: (i, off)))
        elif d[0] == "par":
            specs.append(pl.BlockSpec(arr.shape, lambda i, nd=arr.ndim: (0,) * nd))
        else:
            specs.append(pl.BlockSpec((1,) + arr.shape[1:], lambda i, nd=arr.ndim: (jnp.where(i >= nct, 1, 0),) + (0,) * (nd - 1)))
    return specs


def _load(d, ref, h):
    if d[0] == "grp":
        return ref[0]
    if d[0] == "row" and d[2]:
        return ref[:, h * d[1]:(h + 1) * d[1]]
    return ref[...]


def _out_specs(outs, tm, heads):
    return [pl.BlockSpec((tm, w * (heads if ph else 1)), lambda i: (i, 0)) for (w, ph, _) in outs]


def rowop_fwd(name, fn, descs, arrays, outs, T, n_ctx, heads=1, tm=ROW_TILE):
    nct = n_ctx // tm
    n_in = len(descs)

    def body(*refs):
        for h in range(heads):
            res = fn(*[_load(d, r, h) for d, r in zip(descs, refs[:n_in])])
            for o_ref, r, (w, ph, _) in zip(refs[n_in:], res, outs):
                if ph:
                    o_ref[:, h * w:(h + 1) * w] = r.astype(o_ref.dtype)
                else:
                    o_ref[...] = r.astype(o_ref.dtype)

    out_shape = [jax.ShapeDtypeStruct((T, w * (heads if ph else 1)), dt) for (w, ph, dt) in outs]
    return pl.pallas_call(
        body, name=name, grid=(T // tm,), in_specs=_row_specs(descs, arrays, tm, nct, heads), out_specs=_out_specs(outs, tm, heads),
        out_shape=out_shape, compiler_params=_cparams(("parallel",)),
    )(*arrays)


def rowop_bwd(name, fn, descs, arrays, outs, cts, T, n_ctx, heads=1, tm=ROW_TILE, add=None):
    nct = n_ctx // tm
    n_in, n_ct = len(descs), len(cts)
    add = add or {}
    diff_idx = [k for k, d in enumerate(descs) if d[-1]]
    add_idx = [k for k in diff_idx if k in add]

    def body(*refs):
        in_refs, ct_refs = refs[:n_in], refs[n_in:n_in + n_ct]
        add_refs = dict(zip(add_idx, refs[n_in + n_ct:n_in + n_ct + len(add_idx)]))
        g_refs = refs[n_in + n_ct + len(add_idx):]
        i = pl.program_id(0)
        shared = {}
        for h in range(heads):
            vals = [_load(d, r, h) for d, r in zip(descs, in_refs)]

            def f(*dvals, vals=vals):
                full = list(vals)
                for k, v in zip(diff_idx, dvals):
                    full[k] = v
                return tuple(fn(*full))

            _, vjp = jax.vjp(f, *[vals[k] for k in diff_idx])
            cts_h = tuple(c[:, h * w:(h + 1) * w] if ph else c[...] for c, (w, ph, _) in zip(ct_refs, outs))
            for k, g_ref, g in zip(diff_idx, g_refs, vjp(cts_h)):
                d = descs[k]
                if d[0] == "row" and d[2]:
                    g_ref[:, h * d[1]:(h + 1) * d[1]] = g.astype(g_ref.dtype)
                else:
                    shared[k] = g if k not in shared else shared[k] + g
        for k, g_ref in zip(diff_idx, g_refs):
            d = descs[k]
            if k not in shared:
                continue
            g = shared[k]
            if d[0] == "row":
                if k in add_refs:
                    g = g + add_refs[k][...]
                g_ref[...] = g.astype(g_ref.dtype)
            elif d[0] == "par":
                _accumulate(g_ref, g, i == 0)
            else:
                _accumulate(g_ref, g[None], jnp.logical_or(i == 0, i == nct))

    in_specs = _row_specs(descs, arrays, tm, nct, heads)
    g_specs, g_shape = [], []
    for k in diff_idx:
        d = descs[k]
        if d[0] == "row":
            g_specs.append(pl.BlockSpec((tm, d[1] * (heads if d[2] else 1)), lambda i: (i, 0)))
            g_shape.append(jax.ShapeDtypeStruct((T, d[1] * (heads if d[2] else 1)), F32))
        else:
            g_specs.append(in_specs[k])
            g_shape.append(jax.ShapeDtypeStruct(arrays[k].shape, F32))
    add_specs = [g_specs[diff_idx.index(k)] for k in add_idx]
    return pl.pallas_call(
        body, name=name, grid=(T // tm,), in_specs=in_specs + _out_specs(outs, tm, heads) + add_specs, out_specs=g_specs,
        out_shape=g_shape, compiler_params=_cparams(("arbitrary",)),
    )(*arrays, *cts, *[add[k] for k in add_idx])


def _accumulate(ref, val, first):
    @pl.when(first)
    def _():
        ref[...] = val.astype(ref.dtype)

    @pl.when(jnp.logical_not(first))
    def _():
        ref[...] += val.astype(ref.dtype)


def _rms(x, count=None):
    n = x.shape[-1] if count is None else count
    return x * lax.rsqrt(jnp.sum(x * x, axis=-1, keepdims=True) * (1.0 / n) + EPS)


def _swap_halves(x, nf):
    w = x.shape[-1]
    lane = lax.broadcasted_iota(jnp.int32, x.shape, x.ndim - 1)
    return jnp.where((lane % (2 * nf)) < nf, pltpu.roll(x, w - nf, x.ndim - 1), pltpu.roll(x, nf, x.ndim - 1))


def _make_rope(nf):
    @jax.custom_vjp
    def rope(x, c, s):
        return x * c + _swap_halves(x, nf) * s

    def fwd(x, c, s):
        return rope(x, c, s), (c, s)

    def bwd(res, g):
        c, s = res
        return g * c + _swap_halves(g * s, nf), jnp.zeros_like(c), jnp.zeros_like(s)

    rope.defvjp(fwd, bwd)
    return rope


_rope_swa = _make_rope(SWA_HEAD_DIM // 4)
_rope_mla = _make_rope(MLA_ROPE // 4)


@jax.custom_vjp
def _softplus(x):
    e = jnp.exp(-jnp.abs(x))
    u = 1.0 + e
    log1p_e = jnp.where(u == 1.0, e, jnp.log(u) * e / jnp.where(u == 1.0, 1.0, u - 1.0))
    return jnp.maximum(x, 0.0) + log1p_e


_softplus.defvjp(lambda x: (_softplus(x), x), lambda x, g: (g * jax.nn.sigmoid(x),))


def fn_norm_mod(x, g, shift, scale):
    return (_rms(x) * g * (1.0 + scale) + shift,)


def fn_rms(x, g):
    return (_rms(x) * g,)


def fn_resid(x, a, gate):
    return (x + gate * a,)


def fn_softplus(dt, bias):
    return (_softplus(dt + bias),)


def fn_ssd_out(yf, yb, xs, z, d_lane, g):
    y = yf + yb + d_lane * xs
    return (_rms(y * (z * jax.nn.sigmoid(z))) * g,)


def fn_swa_q(q, g, c, s):
    return (_rope_swa(_rms(q) * g, c, s),)


def fn_swa_kv(k, v, g, c, s):
    return (_rope_swa(_rms(k) * g, c, s), v)


def fn_mla_q(qn, qr, gn, gr, c, s):
    return (jnp.concatenate([_rms(qn) * gn, _rope_mla(_rms(qr, MLA_ROPE) * gr, c, s)], axis=-1),)


def fn_mla_kv(kn, v, kr, gn, gr, c, s):
    return (jnp.concatenate([_rms(kn) * gn, _rope_mla(_rms(kr, MLA_ROPE) * gr, c, s)], axis=-1), v)


def fn_merge(g1, g2, g3, p1, p2, p3):
    return (jax.nn.sigmoid(g1) * p1 + jax.nn.sigmoid(g2) * p2 + jax.nn.sigmoid(g3) * p3,)


def fn_swiglu(g, u):
    return (g * jax.nn.sigmoid(g) * u,)


ATTN_TILE = 256
NT_DIMS = (((1,), (1,)), ((), ()))


class AttnCfg:
    def __init__(self, hq, group, dq, dv, scale, window, has_sink, L, T, chunk, kv_block):
        self.hq, self.group, self.dq, self.dv, self.scale = hq, group, dq, dv, scale
        self.window, self.has_sink, self.L, self.T = window, has_sink, L, T
        self.chunk = _pick(L, (chunk, ATTN_TILE))
        self.ctx_chunk = T - L
        self.kv_block = kv_block
        self.q_block = kv_block * group
        assert L % ATTN_TILE == 0 and (T - L) % ATTN_TILE == 0 and L % self.chunk == 0
        assert (hq // group) % kv_block == 0
        if window is not None:
            assert (ATTN_TILE + 2 * window) % self.chunk == 0
            self.window_chunks = min((ATTN_TILE + 2 * window) // self.chunk, L // self.chunk)
            self.align = math.gcd(self.chunk, window)
        else:
            self.align = self.chunk


LOG2E = math.log2(math.e)


def _latent_chunks(cfg, r0):
    c = cfg.chunk
    if cfg.window is None:
        lo, n = 0, cfg.L // c
    else:
        n = cfg.window_chunks
        lo = jnp.clip(r0 - cfg.window, 0, cfg.L - n * c)
    return lo, n


def _visible(cfg, rows_q, rows_k):
    return jnp.logical_or(rows_k >= cfg.L, jnp.abs(rows_k - rows_q) <= cfg.window)


def flash_fwd(name, cfg, q, k, v, sink):
    T, tq, c = cfg.T, ATTN_TILE, cfg.chunk
    hq, g, dq, dv, hb, kb = cfg.hq, cfg.group, cfg.dq, cfg.dv, cfg.q_block, cfg.kv_block
    to_log2 = cfg.scale * LOG2E

    def body(*refs):
        if cfg.has_sink:
            q_ref, k_ref, v_ref, sink_ref, o_ref, lse_ref = refs
        else:
            q_ref, k_ref, v_ref, o_ref, lse_ref = refs
        q0 = pl.program_id(1) * tq
        qs = [q_ref[:, hh * dq:(hh + 1) * dq] for hh in range(hb)]
        lat_lo, lat_n = _latent_chunks(cfg, q0)
        n = jnp.where(q0 >= cfg.L, 0, lat_n)
        rows_q = q0 + lax.broadcasted_iota(jnp.int32, (tq, 1), 0)

        def start(t):
            return pl.multiple_of(lat_lo + jnp.minimum(t, lat_n - 1) * c, cfg.align)

        def logits(ks, size):
            return tuple(lax.dot_general(qs[hh], k_ref[pl.ds(ks, size), (hh // g) * dq:(hh // g + 1) * dq], NT_DIMS,
                                         preferred_element_type=F32) for hh in range(hb))

        def update(state, s_all, ks, size, masked):
            new_state = []
            for hh in range(hb):
                m, l, acc = state[hh]
                s = s_all[hh]
                if masked:
                    rows_k = ks + lax.broadcasted_iota(jnp.int32, (1, size), 1)
                    s = jnp.where(_visible(cfg, rows_q, rows_k), s, -jnp.inf)
                m_new = jnp.maximum(m, jnp.max(s, axis=-1, keepdims=True) * to_log2)
                alpha = jnp.exp2(m - m_new)
                p = jnp.exp2(s * to_log2 - m_new)
                l = alpha * l + jnp.sum(p, axis=-1, keepdims=True)
                kh = hh // g
                acc = alpha * acc + jnp.dot(p.astype(BF16), v_ref[pl.ds(ks, size), kh * dv:(kh + 1) * dv], preferred_element_type=F32)
                new_state.append((m_new, l, acc))
            return tuple(new_state)

        def step(t, carry):
            state, s_all = carry
            s_next = logits(start(t + 1), c)
            return update(state, s_all, start(t), c, cfg.window is not None), s_next

        state = []
        for hh in range(hb):
            if cfg.has_sink:
                m0 = jnp.zeros((tq, 1), F32) + sink_ref[hh] * LOG2E
                l0 = jnp.ones((tq, 1), F32)
            else:
                m0 = jnp.full((tq, 1), -jnp.inf, F32)
                l0 = jnp.zeros((tq, 1), F32)
            state.append((m0, l0, jnp.zeros((tq, dv), F32)))
        state = update(tuple(state), logits(cfg.L, cfg.ctx_chunk), cfg.L, cfg.ctx_chunk, False)
        state, _ = lax.fori_loop(0, n, step, (state, logits(start(0), c)))
        for hh in range(hb):
            m, l, acc = state[hh]
            o_ref[:, hh * dv:(hh + 1) * dv] = acc / l
            lse_ref[hh] = m + jnp.log2(l)

    in_specs = [pl.BlockSpec((tq, hb * dq), lambda h, i: (i, h)),
                pl.BlockSpec((T, kb * dq), lambda h, i: (0, h)),
                pl.BlockSpec((T, kb * dv), lambda h, i: (0, h))]
    ins = [q, k, v]
    if cfg.has_sink:
        in_specs.append(pl.BlockSpec((hb, 1, 1), lambda h, i: (h, 0, 0)))
        ins.append(sink)
    return pl.pallas_call(
        body, name=name, grid=(hq // hb, T // tq), in_specs=in_specs,
        out_specs=[pl.BlockSpec((tq, hb * dv), lambda h, i: (i, h)), pl.BlockSpec((hb, tq, 1), lambda h, i: (h, i, 0))],
        out_shape=[jax.ShapeDtypeStruct((T, hq * dv), F32), jax.ShapeDtypeStruct((hq, T, 1), F32)],
        compiler_params=_cparams(("parallel", "parallel")),
    )(*ins)


def attn_delta(name, cfg, o, do, lse, sink):
    T, tm, hq, dv = cfg.T, ATTN_TILE, cfg.hq, cfg.dv

    def body(*refs):
        if cfg.has_sink:
            o_ref, do_ref, lse_ref, sink_ref, delta_ref, dob_ref, dsink_ref = refs
        else:
            o_ref, do_ref, delta_ref, dob_ref = refs
        dob_ref[...] = do_ref[...].astype(BF16)
        parts = []
        for h in range(hq):
            delta = jnp.sum(do_ref[:, h * dv:(h + 1) * dv] * o_ref[:, h * dv:(h + 1) * dv], axis=-1, keepdims=True)
            delta_ref[h] = delta
            if cfg.has_sink:
                parts.append(-jnp.sum(jnp.exp2(sink_ref[h] * LOG2E - lse_ref[h]) * delta, axis=0, keepdims=True)[None])
        if cfg.has_sink:
            _accumulate(dsink_ref, jnp.concatenate(parts, axis=0), pl.program_id(0) == 0)

    head_tile = pl.BlockSpec((tm, hq * dv), lambda i: (i, 0))
    col = pl.BlockSpec((hq, tm, 1), lambda i: (0, i, 0))
    one = pl.BlockSpec((hq, 1, 1), lambda i: (0, 0, 0))
    in_specs, ins = [head_tile, head_tile], [o, do]
    out_specs = [col, head_tile]
    out_shape = [jax.ShapeDtypeStruct((hq, T, 1), F32), jax.ShapeDtypeStruct((T, hq * dv), BF16)]
    if cfg.has_sink:
        in_specs += [col, one]
        ins += [lse, sink]
        out_specs.append(one)
        out_shape.append(jax.ShapeDtypeStruct((hq, 1, 1), F32))
    return pl.pallas_call(body, name=name, grid=(T // tm,), in_specs=in_specs, out_specs=out_specs, out_shape=out_shape,
                          compiler_params=_cparams(("arbitrary",)))(*ins)


def flash_dq(name, cfg, q, k, v, dob, lse, delta):
    T, tq, c = cfg.T, ATTN_TILE, cfg.chunk
    hq, g, dq, dv, hb, kb = cfg.hq, cfg.group, cfg.dq, cfg.dv, cfg.q_block, cfg.kv_block

    def body(q_ref, k_ref, v_ref, do_ref, lse_ref, delta_ref, dq_ref):
        q0 = pl.program_id(1) * tq
        qs = [q_ref[:, hh * dq:(hh + 1) * dq] for hh in range(hb)]
        dos = [do_ref[:, hh * dv:(hh + 1) * dv] for hh in range(hb)]
        lat_lo, lat_n = _latent_chunks(cfg, q0)
        n = jnp.where(q0 >= cfg.L, 0, lat_n)
        rows_q = q0 + lax.broadcasted_iota(jnp.int32, (tq, 1), 0)
        to_log2 = cfg.scale * LOG2E

        def start(t):
            return pl.multiple_of(lat_lo + jnp.minimum(t, lat_n - 1) * c, cfg.align)

        def keys(ks, size, hh):
            kh = hh // g
            return k_ref[pl.ds(ks, size), kh * dq:(kh + 1) * dq]

        def products(ks, size):
            out = []
            for hh in range(hb):
                kh = hh // g
                out.append((lax.dot_general(qs[hh], keys(ks, size, hh), NT_DIMS, preferred_element_type=F32),
                            lax.dot_general(dos[hh], v_ref[pl.ds(ks, size), kh * dv:(kh + 1) * dv], NT_DIMS,
                                            preferred_element_type=F32)))
            return tuple(out)

        def update(accs, prods, ks, size, masked):
            new_accs = []
            for hh in range(hb):
                s, dp = prods[hh]
                p = jnp.exp2(s * to_log2 - lse_ref[hh])
                if masked:
                    rows_k = ks + lax.broadcasted_iota(jnp.int32, (1, size), 1)
                    p = jnp.where(_visible(cfg, rows_q, rows_k), p, 0.0)
                ds = (p * (dp - delta_ref[hh])).astype(BF16)
                new_accs.append(accs[hh] + jnp.dot(ds, keys(ks, size, hh), preferred_element_type=F32))
            return tuple(new_accs)

        def step(t, carry):
            accs, prods = carry
            nxt = products(start(t + 1), c)
            return update(accs, prods, start(t), c, cfg.window is not None), nxt

        accs = tuple(jnp.zeros((tq, dq), F32) for _ in range(hb))
        accs = update(accs, products(cfg.L, cfg.ctx_chunk), cfg.L, cfg.ctx_chunk, False)
        accs, _ = lax.fori_loop(0, n, step, (accs, products(start(0), c)))
        for hh in range(hb):
            dq_ref[:, hh * dq:(hh + 1) * dq] = accs[hh] * cfg.scale

    col = pl.BlockSpec((hb, tq, 1), lambda h, i: (h, i, 0))
    return pl.pallas_call(
        body, name=name, grid=(hq // hb, T // tq),
        in_specs=[pl.BlockSpec((tq, hb * dq), lambda h, i: (i, h)),
                  pl.BlockSpec((T, kb * dq), lambda h, i: (0, h)),
                  pl.BlockSpec((T, kb * dv), lambda h, i: (0, h)),
                  pl.BlockSpec((tq, hb * dv), lambda h, i: (i, h)), col, col],
        out_specs=pl.BlockSpec((tq, hb * dq), lambda h, i: (i, h)),
        out_shape=jax.ShapeDtypeStruct((T, hq * dq), F32),
        compiler_params=_cparams(("parallel", "parallel")),
    )(q, k, v, dob, lse, delta)


def flash_bwd_fused(name, cfg, q, k, v, dob, lse, delta):
    assert cfg.window is None and cfg.group == 1
    T, L, tq, c, cc = cfg.T, cfg.L, ATTN_TILE, cfg.chunk, cfg.ctx_chunk
    hq, dq, dv = cfg.hq, cfg.dq, cfg.dv
    nq = T // tq
    to_log2 = cfg.scale * LOG2E

    def body(q_ref, k_ref, v_ref, do_ref, lse_ref, delta_ref, dq_ref, dk_ref, dv_ref):
        i = pl.program_id(1)
        q0 = i * tq

        @pl.when(i == 0)
        def _():
            dk_ref[...] = jnp.zeros_like(dk_ref)
            dv_ref[...] = jnp.zeros_like(dv_ref)

        qq, dd = q_ref[...], do_ref[...]
        lse_c, delta_c = lse_ref[0], delta_ref[0]
        n = jnp.where(q0 >= L, 0, L // c)

        def start(t):
            return pl.multiple_of(jnp.minimum(t, L // c - 1) * c, c)

        def products(ks, size):
            return (lax.dot_general(qq, k_ref[pl.ds(ks, size), :], NT_DIMS, preferred_element_type=F32),
                    lax.dot_general(dd, v_ref[pl.ds(ks, size), :], NT_DIMS, preferred_element_type=F32))

        def update(acc, prods, ks, size):
            s, dp = prods
            p = jnp.exp2(s * to_log2 - lse_c)
            ds = (p * (dp - delta_c)).astype(BF16)
            dv_ref[pl.ds(ks, size), :] += lax.dot_general(p.astype(BF16), dd, TN_DIMS, preferred_element_type=F32)
            dk_ref[pl.ds(ks, size), :] += lax.dot_general(ds, qq, TN_DIMS, preferred_element_type=F32)
            return acc + jnp.dot(ds, k_ref[pl.ds(ks, size), :], preferred_element_type=F32)

        def step(t, carry):
            acc, prods = carry
            nxt = products(start(t + 1), c)
            return update(acc, prods, start(t), c), nxt

        acc = update(jnp.zeros((tq, dq), F32), products(L, cc), L, cc)
        acc, _ = lax.fori_loop(0, n, step, (acc, products(start(0), c)))
        dq_ref[...] = acc * cfg.scale

        @pl.when(i == nq - 1)
        def _():
            dk_ref[...] = dk_ref[...] * cfg.scale

    col = pl.BlockSpec((1, tq, 1), lambda h, i: (h, i, 0))
    return pl.pallas_call(
        body, name=name, grid=(hq, nq),
        in_specs=[pl.BlockSpec((tq, dq), lambda h, i: (i, h)),
                  pl.BlockSpec((T, dq), lambda h, i: (0, h)),
                  pl.BlockSpec((T, dv), lambda h, i: (0, h)),
                  pl.BlockSpec((tq, dv), lambda h, i: (i, h)), col, col],
        out_specs=[pl.BlockSpec((tq, dq), lambda h, i: (i, h)),
                   pl.BlockSpec((T, dq), lambda h, i: (0, h)),
                   pl.BlockSpec((T, dv), lambda h, i: (0, h))],
        out_shape=[jax.ShapeDtypeStruct((T, hq * dq), F32), jax.ShapeDtypeStruct((T, hq * dq), F32),
                   jax.ShapeDtypeStruct((T, hq * dv), F32)],
        compiler_params=_cparams(("arbitrary", "arbitrary")),
    )(q, k, v, dob, lse, delta)


def flash_dkv(name, cfg, q, k, v, dob, lse, delta):
    T, L, tk, c, al, cc = cfg.T, cfg.L, ATTN_TILE, cfg.chunk, cfg.align, cfg.ctx_chunk
    hq, g, dq, dv, hb, kb = cfg.hq, cfg.group, cfg.dq, cfg.dv, cfg.q_block, cfg.kv_block
    hk = hq // g
    sub = c // al

    def body(k_ref, v_ref, q_ref, do_ref, lse_lat, delta_lat, lse_ctx, delta_ctx, dk_ref, dv_ref):
        k0 = pl.program_id(1) * tk
        kk = [k_ref[:, kh * dq:(kh + 1) * dq] for kh in range(kb)]
        vv = [v_ref[:, kh * dv:(kh + 1) * dv] for kh in range(kb)]
        is_ctx = k0 >= L
        rows_k = k0 + lax.broadcasted_iota(jnp.int32, (tk, 1), 0)
        to_log2 = cfg.scale * LOG2E
        lat_lo, lat_n = _latent_chunks(cfg, k0)
        if cfg.window is not None:
            lat_lo = jnp.where(is_ctx, 0, lat_lo)
            lat_n = jnp.where(is_ctx, L // c, lat_n)

        def start(t):
            return pl.multiple_of(lat_lo + jnp.minimum(t, lat_n - 1) * c, al)

        def operands(qs, size, hh):
            return q_ref[pl.ds(qs, size), hh * dq:(hh + 1) * dq], do_ref[pl.ds(qs, size), hh * dv:(hh + 1) * dv]

        def products(qs, size):
            out = []
            for hh in range(hb):
                kh = hh // g
                qc, dc = operands(qs, size, hh)
                out.append((lax.dot_general(kk[kh], qc, NT_DIMS, preferred_element_type=F32),
                            lax.dot_general(vv[kh], dc, NT_DIMS, preferred_element_type=F32)))
            return tuple(out)

        def update(accs, prods, qs, size, rows, masked):
            accs = [list(a) for a in accs]
            for hh in range(hb):
                kh = hh // g
                st, dpt = prods[hh]
                qc, dc = operands(qs, size, hh)
                lse_row, delta_row = rows(hh)
                pt = jnp.exp2(st * to_log2 - lse_row)
                if masked:
                    rows_q = qs + lax.broadcasted_iota(jnp.int32, (1, size), 1)
                    pt = jnp.where(_visible(cfg, rows_q, rows_k), pt, 0.0)
                accs[kh][1] = accs[kh][1] + jnp.dot(pt.astype(BF16), dc, preferred_element_type=F32)
                dst = (pt * (dpt - delta_row)).astype(BF16)
                accs[kh][0] = accs[kh][0] + jnp.dot(dst, qc, preferred_element_type=F32)
            return tuple(tuple(a) for a in accs)

        def step(t, carry):
            accs, prods = carry
            nxt = products(start(t + 1), c)
            qs = start(t)
            ci = qs // al

            def rows(hh):
                return (jnp.concatenate([lse_lat[hh, ci + j] for j in range(sub)], axis=1),
                        jnp.concatenate([delta_lat[hh, ci + j] for j in range(sub)], axis=1))

            return update(accs, prods, qs, c, rows, cfg.window is not None), nxt

        def ctx_step(_, accs):
            return update(accs, products(L, cc), L, cc, lambda hh: (lse_ctx[hh, 0], delta_ctx[hh, 0]), False)

        accs = tuple((jnp.zeros((tk, dq), F32), jnp.zeros((tk, dv), F32)) for _ in range(kb))
        accs, _ = lax.fori_loop(0, lat_n, step, (accs, products(start(0), c)))
        accs = lax.fori_loop(0, jnp.where(is_ctx, 1, 0), ctx_step, accs)
        for kh in range(kb):
            dk_ref[:, kh * dq:(kh + 1) * dq] = accs[kh][0] * cfg.scale
            dv_ref[:, kh * dv:(kh + 1) * dv] = accs[kh][1]

    def lanes(a):
        return a[:, :L].reshape(hq, L // al, 1, al), a[:, L:].reshape(hq, 1, 1, cc)

    lse_lat, lse_ctx = lanes(lse)
    delta_lat, delta_ctx = lanes(delta)
    lat_spec = pl.BlockSpec((hb, L // al, 1, al), lambda h, j: (h, 0, 0, 0))
    ctx_spec = pl.BlockSpec((hb, 1, 1, cc), lambda h, j: (h, 0, 0, 0))
    return pl.pallas_call(
        body, name=name, grid=(hk // kb, T // tk),
        in_specs=[pl.BlockSpec((tk, kb * dq), lambda h, j: (j, h)),
                  pl.BlockSpec((tk, kb * dv), lambda h, j: (j, h)),
                  pl.BlockSpec((T, hb * dq), lambda h, j: (0, h)),
                  pl.BlockSpec((T, hb * dv), lambda h, j: (0, h)),
                  lat_spec, lat_spec, ctx_spec, ctx_spec],
        out_specs=[pl.BlockSpec((tk, kb * dq), lambda h, j: (j, h)), pl.BlockSpec((tk, kb * dv), lambda h, j: (j, h))],
        out_shape=[jax.ShapeDtypeStruct((T, hk * dq), F32), jax.ShapeDtypeStruct((T, hk * dv), F32)],
        compiler_params=_cparams(("parallel", "parallel")),
    )(k, v, q, dob, lse_lat, delta_lat, lse_ctx, delta_ctx)


HALO = 8


def _conv_specs(tm, C, T):
    nb = tm // HALO
    last = T // HALO - 1
    return [pl.BlockSpec((HALO, C), lambda i: (jnp.maximum(i * nb - 1, 0), 0)),
            pl.BlockSpec((tm, C), lambda i: (i, 0)),
            pl.BlockSpec((HALO, C), lambda i: (jnp.minimum((i + 1) * nb, last), 0))]


def _extended(prev_ref, cur_ref, next_ref, i, tm, L, T):
    r0 = i * tm
    keep_prev = jnp.logical_and(r0 != 0, r0 != L).astype(F32)
    keep_next = jnp.logical_and(r0 + tm != L, r0 + tm != T).astype(F32)
    return jnp.concatenate([prev_ref[...] * keep_prev, cur_ref[...], next_ref[...] * keep_next], axis=0)


def _shift_rows(xe, d):
    n = xe.shape[0]
    return xe if d == 0 else pltpu.roll(xe, (-d) % n, 0)


def _conv_pre(xe, w_ref, b_ref):
    acc = b_ref[...] + w_ref[SSM_CONV // 2:SSM_CONV // 2 + 1, :] * xe
    for k in range(SSM_CONV):
        if k != SSM_CONV // 2:
            acc = acc + w_ref[k:k + 1, :] * _shift_rows(xe, k - SSM_CONV // 2)
    return acc


def conv_fwd(name, x, w, b, L, tm=ROW_TILE):
    T, C = x.shape

    def body(xp, xc, xn, w_ref, b_ref, o_ref):
        xe = _extended(xp, xc, xn, pl.program_id(0), tm, L, T)
        pre = _conv_pre(xe, w_ref, b_ref)[HALO:HALO + tm]
        o_ref[...] = pre * jax.nn.sigmoid(pre)

    full = lambda a: pl.BlockSpec(a.shape, lambda i: (0, 0))
    return pl.pallas_call(body, name=name, grid=(T // tm,), in_specs=_conv_specs(tm, C, T) + [full(w), full(b)],
                          out_specs=pl.BlockSpec((tm, C), lambda i: (i, 0)), out_shape=jax.ShapeDtypeStruct((T, C), F32),
                          compiler_params=_cparams(("parallel",)))(x, x, x, w, b)


def conv_bwd(name, x, w, b, gu, L, tm=ROW_TILE):
    T, C = x.shape

    def body(xp, xc, xn, gp, gc, gn, w_ref, b_ref, dx_ref, dw_ref, db_ref):
        i = pl.program_id(0)
        xe = _extended(xp, xc, xn, i, tm, L, T)
        ge = _extended(gp, gc, gn, i, tm, L, T)
        pre = _conv_pre(xe, w_ref, b_ref)
        sg = jax.nn.sigmoid(pre)
        gpre = ge * (sg * (1.0 + pre * (1.0 - sg)))
        half = SSM_CONV // 2
        dx = jnp.zeros((tm, C), F32)
        rows = []
        for k in range(SSM_CONV):
            dx = dx + w_ref[k:k + 1, :] * _shift_rows(gpre, half - k)[HALO:HALO + tm]
            rows.append(jnp.sum(gpre[HALO:HALO + tm] * _shift_rows(xe, k - half)[HALO:HALO + tm], axis=0, keepdims=True))
        dx_ref[...] = dx
        rows += [jnp.zeros((1, C), F32)] * (8 - SSM_CONV)
        _accumulate(dw_ref, jnp.concatenate(rows, axis=0), i == 0)
        _accumulate(db_ref, jnp.sum(gpre[HALO:HALO + tm], axis=0, keepdims=True), i == 0)

    full = lambda a: pl.BlockSpec(a.shape, lambda i: (0, 0))
    return pl.pallas_call(
        body, name=name, grid=(T // tm,), in_specs=_conv_specs(tm, C, T) * 2 + [full(w), full(b)],
        out_specs=[pl.BlockSpec((tm, C), lambda i: (i, 0)), pl.BlockSpec((8, C), lambda i: (0, 0)), pl.BlockSpec((1, C), lambda i: (0, 0))],
        out_shape=[jax.ShapeDtypeStruct((T, C), F32), jax.ShapeDtypeStruct((8, C), F32), jax.ShapeDtypeStruct((1, C), F32)],
        compiler_params=_cparams(("arbitrary",)))(x, x, x, gu, gu, gu, w, b)


SSM_PAIRS = SSM_HEADS // 2
TN_DIMS = (((0,), (0,)), ((), ()))
HIGHEST = lax.Precision.HIGHEST


def _ssd_chunk(direction, xps, bs, cs, dt_col, dt_row, alog_row, alog_col, hps):
    Q = SSM_CHUNK
    da_col = dt_col * (-jnp.exp(alog_row))
    da_row = dt_row * (-jnp.exp(alog_col))
    ii = lax.broadcasted_iota(jnp.int32, (Q, Q), 0)
    jj = lax.broadcasted_iota(jnp.int32, (Q, Q), 1)
    tri = (ii >= jj) if direction == 0 else (ii <= jj)
    trif = tri.astype(F32)
    acs_col = jnp.dot(trif, da_col, precision=HIGHEST, preferred_element_type=F32)
    acs_row = lax.dot_general(da_row, trif, NT_DIMS, precision=HIGHEST, preferred_element_type=F32)
    tot_col = jnp.sum(da_col, axis=0, keepdims=True)
    lane16 = lax.broadcasted_iota(jnp.int32, (1, SSM_HEADS), 1)
    sub16 = lax.broadcasted_iota(jnp.int32, (SSM_HEADS, 1), 0)
    low = lax.broadcasted_iota(jnp.int32, (1, 2 * SSM_HEAD_DIM), 1) < SSM_HEAD_DIM

    def col(v, h):
        return jnp.sum(v * (lane16 == h).astype(F32), axis=1, keepdims=True)

    def row(v, h):
        return jnp.sum(v * (sub16 == h).astype(F32), axis=0, keepdims=True)

    ys, hos = [], []
    pairs_per_group = SSM_PAIRS // SSM_GROUPS
    for g in range(SSM_GROUPS):
        bb, cb16 = bs[g].astype(BF16), cs[g].astype(BF16)
        cb = lax.dot_general(cb16, bb, NT_DIMS, preferred_element_type=F32)
        for pp in range(pairs_per_group):
            p = g * pairs_per_group + pp
            h0, h1 = 2 * p, 2 * p + 1
            ac0, ac1 = col(acs_col, h0), col(acs_col, h1)
            seg0 = jnp.exp(jnp.where(tri, ac0 - row(acs_row, h0), -jnp.inf))
            seg1 = jnp.exp(jnp.where(tri, ac1 - row(acs_row, h1), -jnp.inf))
            dt_l = jnp.where(low, col(dt_col, h0), col(dt_col, h1))
            ac_l = jnp.where(low, ac0, ac1)
            tot_l = jnp.where(low, col(tot_col, h0), col(tot_col, h1))
            xdt = xps[p] * dt_l
            y = (jnp.dot((cb * seg0).astype(BF16), jnp.where(low, xdt, 0.0).astype(BF16), preferred_element_type=F32)
                 + jnp.dot((cb * seg1).astype(BF16), jnp.where(low, 0.0, xdt).astype(BF16), preferred_element_type=F32))
            y = y + jnp.dot(cb16, hps[p].astype(BF16), preferred_element_type=F32) * jnp.exp(ac_l)
            st = lax.dot_general(bb, (xdt * jnp.exp(tot_l - ac_l)).astype(BF16), TN_DIMS, preferred_element_type=F32)
            ys.append(y)
            hos.append(hps[p] * jnp.exp(tot_l) + st)
    return tuple(ys), tuple(hos)


def _ssd_chunk_of(direction, step, ncl, ncc):
    if direction == 0:
        return jnp.where(step < ncc, ncl + step, step - ncc)
    return jnp.where(step < ncc, ncl + ncc - 1 - step, ncl - 1 - (step - ncc))


def _ssd_load(u_ref):
    Q = SSM_CHUNK
    xps = tuple(u_ref[:, LANES * p:LANES * (p + 1)] for p in range(SSM_PAIRS))
    bs = tuple(u_ref[:, SSM_INNER + SSM_STATE * g:SSM_INNER + SSM_STATE * (g + 1)] for g in range(SSM_GROUPS))
    c0 = SSM_INNER + SSM_GROUPS * SSM_STATE
    cs = tuple(u_ref[:, c0 + SSM_STATE * g:c0 + SSM_STATE * (g + 1)] for g in range(SSM_GROUPS))
    return xps, bs, cs


def ssd_fwd(name, direction, u, dt, dt_t, alog_row, alog_col, L):
    T = u.shape[0]
    Q, N = SSM_CHUNK, SSM_STATE
    ncl, ncc = L // Q, (T - L) // Q
    nc = ncl + ncc
    cm = lambda s: _ssd_chunk_of(direction, s, ncl, ncc)

    def body(u_ref, dt_ref, dtt_ref, ar_ref, ac_ref, y_ref, hin_ref, state):
        @pl.when(pl.program_id(0) == 0)
        def _():
            state[...] = jnp.zeros_like(state)

        xps, bs, cs = _ssd_load(u_ref)
        hps = tuple(state[p] for p in range(SSM_PAIRS))
        for p in range(SSM_PAIRS):
            hin_ref[0, p] = hps[p]
        ys, hos = _ssd_chunk(direction, xps, bs, cs, dt_ref[...], dtt_ref[...], ar_ref[...], ac_ref[...], hps)
        for p in range(SSM_PAIRS):
            y_ref[:, LANES * p:LANES * (p + 1)] = ys[p]
            state[p] = hos[p]

    return pl.pallas_call(
        body, name=name, grid=(nc,),
        in_specs=[pl.BlockSpec((Q, SSM_CONV_DIM), lambda s: (cm(s), 0)),
                  pl.BlockSpec((Q, SSM_HEADS), lambda s: (cm(s), 0)),
                  pl.BlockSpec((SSM_HEADS, Q), lambda s: (0, cm(s))),
                  pl.BlockSpec((1, SSM_HEADS), lambda s: (0, 0)),
                  pl.BlockSpec((SSM_HEADS, 1), lambda s: (0, 0))],
        out_specs=[pl.BlockSpec((Q, SSM_INNER), lambda s: (cm(s), 0)),
                   pl.BlockSpec((1, SSM_PAIRS, N, LANES), lambda s: (cm(s), 0, 0, 0))],
        out_shape=[jax.ShapeDtypeStruct((T, SSM_INNER), F32), jax.ShapeDtypeStruct((nc, SSM_PAIRS, N, LANES), F32)],
        scratch_shapes=[pltpu.VMEM((SSM_PAIRS, N, LANES), F32)],
        compiler_params=_cparams(("arbitrary",)),
    )(u, dt, dt_t, alog_row, alog_col)


def ssd_bwd(name, direction, u, dt, dt_t, alog_row, alog_col, hin, dy, L, add_x=None, add_u=None):
    T = u.shape[0]
    Q, N = SSM_CHUNK, SSM_STATE
    ncl, ncc = L // Q, (T - L) // Q
    nc = ncl + ncc
    cm = lambda s: _ssd_chunk_of(direction, nc - 1 - s, ncl, ncc)
    n_add = (add_x is not None) + (add_u is not None)

    def body(*refs):
        u_ref, dt_ref, dtt_ref, ar_ref, ac_ref, hin_ref, dy_ref = refs[:7]
        add_refs = refs[7:7 + n_add]
        du_ref, ddt_ref, ddtt_ref, dar_ref, dac_ref, dstate = refs[7 + n_add:]
        first = pl.program_id(0) == 0

        @pl.when(first)
        def _():
            dstate[...] = jnp.zeros_like(dstate)

        xps, bs, cs = _ssd_load(u_ref)
        hps = tuple(hin_ref[0, p] for p in range(SSM_PAIRS))
        _, vjp = jax.vjp(functools.partial(_ssd_chunk, direction), xps, bs, cs, dt_ref[...], dtt_ref[...], ar_ref[...],
                         ac_ref[...], hps)
        dys = tuple(dy_ref[:, LANES * p:LANES * (p + 1)] for p in range(SSM_PAIRS))
        dhs = tuple(dstate[p] for p in range(SSM_PAIRS))
        gx, gb, gc, gdt, gdtt, gar, gac, ghp = vjp((dys, dhs))
        parts = list(gx) + list(gb) + list(gc)
        du = jnp.concatenate(parts, axis=1)
        k = 0
        if add_x is not None:
            du = du + jnp.concatenate([add_refs[k][...], jnp.zeros((Q, SSM_CONV_DIM - SSM_INNER), F32)], axis=1)
            k += 1
        if add_u is not None:
            du = du + add_refs[k][...]
        du_ref[...] = du
        ddt_ref[...] = gdt
        ddtt_ref[...] = gdtt
        _accumulate(dar_ref, gar, first)
        _accumulate(dac_ref, gac, first)
        for p in range(SSM_PAIRS):
            dstate[p] = ghp[p]

    in_specs = [pl.BlockSpec((Q, SSM_CONV_DIM), lambda s: (cm(s), 0)),
                pl.BlockSpec((Q, SSM_HEADS), lambda s: (cm(s), 0)),
                pl.BlockSpec((SSM_HEADS, Q), lambda s: (0, cm(s))),
                pl.BlockSpec((1, SSM_HEADS), lambda s: (0, 0)),
                pl.BlockSpec((SSM_HEADS, 1), lambda s: (0, 0)),
                pl.BlockSpec((1, SSM_PAIRS, N, LANES), lambda s: (cm(s), 0, 0, 0)),
                pl.BlockSpec((Q, SSM_INNER), lambda s: (cm(s), 0))]
    ins = [u, dt, dt_t, alog_row, alog_col, hin, dy]
    if add_x is not None:
        in_specs.append(pl.BlockSpec((Q, SSM_INNER), lambda s: (cm(s), 0)))
        ins.append(add_x)
    if add_u is not None:
        in_specs.append(pl.BlockSpec((Q, SSM_CONV_DIM), lambda s: (cm(s), 0)))
        ins.append(add_u)
    return pl.pallas_call(
        body, name=name, grid=(nc,), in_specs=in_specs,
        out_specs=[pl.BlockSpec((Q, SSM_CONV_DIM), lambda s: (cm(s), 0)),
                   pl.BlockSpec((Q, SSM_HEADS), lambda s: (cm(s), 0)),
                   pl.BlockSpec((SSM_HEADS, Q), lambda s: (0, cm(s))),
                   pl.BlockSpec((1, SSM_HEADS), lambda s: (0, 0)),
                   pl.BlockSpec((SSM_HEADS, 1), lambda s: (0, 0))],
        out_shape=[jax.ShapeDtypeStruct((T, SSM_CONV_DIM), F32), jax.ShapeDtypeStruct((T, SSM_HEADS), F32),
                   jax.ShapeDtypeStruct((SSM_HEADS, T), F32), jax.ShapeDtypeStruct((1, SSM_HEADS), F32),
                   jax.ShapeDtypeStruct((SSM_HEADS, 1), F32)],
        scratch_shapes=[pltpu.VMEM((SSM_PAIRS, N, LANES), F32)],
        compiler_params=_cparams(("arbitrary",)),
    )(*ins)


PEER_MASKS = (1, 2, 4, 3, 5, 6, 7)
N_PEERS = len(PEER_MASKS)
MESH_IDS = pl.DeviceIdType.MESH


def _my_index():
    return lax.axis_index("x") * 4 + lax.axis_index("y") * 2 + lax.axis_index("c")


def _coords(idx):
    return (idx // 4, (idx // 2) % 2, idx % 2)


def all_gather_hbm(name, arrays):
    n = len(arrays)

    def body(*refs):
        ins, outs = refs[:n], refs[n:2 * n]
        send_sems, recv_sems, local_sems = refs[2 * n:]
        me = _my_index()
        copies = []
        for a in range(n):
            local = pltpu.make_async_copy(ins[a], outs[a].at[me], local_sems.at[a])
            local.start()
            copies.append(local)
            for k, mask in enumerate(PEER_MASKS):
                peer = me ^ mask
                cp = pltpu.make_async_remote_copy(src_ref=ins[a], dst_ref=outs[a].at[me], send_sem=send_sems.at[a * N_PEERS + k],
                                                  recv_sem=recv_sems.at[a * N_PEERS + k], device_id=_coords(peer),
                                                  device_id_type=MESH_IDS)
                cp.start()
                copies.append(cp)
        for cp in copies:
            cp.wait()

    any_spec = pl.BlockSpec(memory_space=pl.ANY)
    return pl.pallas_call(
        body, name=name, in_specs=[any_spec] * n, out_specs=[any_spec] * n,
        out_shape=[jax.ShapeDtypeStruct((N_DEV,) + a.shape, a.dtype) for a in arrays],
        scratch_shapes=[pltpu.SemaphoreType.DMA((n * N_PEERS,)), pltpu.SemaphoreType.DMA((n * N_PEERS,)),
                        pltpu.SemaphoreType.DMA((n,))],
    )(*arrays)


def exchange_hbm(name, arrays):
    n = len(arrays)

    def body(*refs):
        ins, outs = refs[:n], refs[n:2 * n]
        send_sems, recv_sems, local_sems = refs[2 * n:]
        me = _my_index()
        copies = []
        for a in range(n):
            local = pltpu.make_async_copy(ins[a].at[me], outs[a].at[me], local_sems.at[a])
            local.start()
            copies.append(local)
            for k, mask in enumerate(PEER_MASKS):
                peer = me ^ mask
                cp = pltpu.make_async_remote_copy(src_ref=ins[a].at[peer], dst_ref=outs[a].at[me],
                                                  send_sem=send_sems.at[a * N_PEERS + k], recv_sem=recv_sems.at[a * N_PEERS + k],
                                                  device_id=_coords(peer), device_id_type=MESH_IDS)
                cp.start()
                copies.append(cp)
        for cp in copies:
            cp.wait()

    any_spec = pl.BlockSpec(memory_space=pl.ANY)
    return pl.pallas_call(
        body, name=name, in_specs=[any_spec] * n, out_specs=[any_spec] * n,
        out_shape=[jax.ShapeDtypeStruct(a.shape, a.dtype) for a in arrays],
        scratch_shapes=[pltpu.SemaphoreType.DMA((n * N_PEERS,)), pltpu.SemaphoreType.DMA((n * N_PEERS,)),
                        pltpu.SemaphoreType.DMA((n,))],
    )(*arrays)


def all_gather_vmem(name, v):
    def body(v_ref, out_ref, send_sems, recv_sems):
        me = _my_index()
        out_ref[me] = v_ref[...]
        copies = []
        for k, mask in enumerate(PEER_MASKS):
            cp = pltpu.make_async_remote_copy(src_ref=v_ref, dst_ref=out_ref.at[me], send_sem=send_sems.at[k],
                                              recv_sem=recv_sems.at[k], device_id=_coords(me ^ mask), device_id_type=MESH_IDS)
            cp.start()
            copies.append(cp)
        for cp in copies:
            cp.wait()

    vm = pl.BlockSpec(memory_space=pltpu.VMEM)
    return pl.pallas_call(
        body, name=name, in_specs=[vm], out_specs=vm, out_shape=jax.ShapeDtypeStruct((N_DEV,) + v.shape, v.dtype),
        scratch_shapes=[pltpu.SemaphoreType.DMA((N_PEERS,)), pltpu.SemaphoreType.DMA((N_PEERS,))],
    )(v)


def _row_tile(rows, cols, bufs):
    budget = 24 * 1024 * 1024 // (bufs * 2 * 4 * max(cols, LANES))
    if rows <= budget:
        return rows
    for t in range(budget - budget % 16, 15, -16):
        if rows % t == 0:
            return t
    return rows


def sum_parts(name, parts):
    P, R, C = parts.shape
    tr = _row_tile(R, C, P + 1)

    def body(p_ref, o_ref):
        acc = p_ref[0].astype(F32)
        for s in range(1, P):
            acc = acc + p_ref[s].astype(F32)
        o_ref[...] = acc

    return pl.pallas_call(body, name=name, grid=(R // tr,), in_specs=[pl.BlockSpec((P, tr, C), lambda i: (0, i, 0))],
                          out_specs=pl.BlockSpec((tr, C), lambda i: (i, 0)), out_shape=jax.ShapeDtypeStruct((R, C), F32),
                          compiler_params=_cparams(("parallel",)))(parts)


def cast_bf16(name, x):
    R, C = x.shape
    tr = _row_tile(R, C, 2)

    def body(x_ref, o_ref):
        o_ref[...] = x_ref[...].astype(BF16)

    spec = pl.BlockSpec((tr, C), lambda i: (i, 0))
    return pl.pallas_call(body, name=name, grid=(R // tr,), in_specs=[spec], out_specs=spec,
                          out_shape=jax.ShapeDtypeStruct((R, C), BF16), compiler_params=_cparams(("parallel",)))(x)


def adamw(name, w, g, m, v):
    R, C = w.shape
    tr = _row_tile(R, C, 7)

    def body(w_ref, g_ref, m_ref, v_ref, d_ref, nm_ref, nv_ref):
        g = g_ref[...]
        nm = ADAM_B1 * m_ref[...] + (1.0 - ADAM_B1) * g
        nv = ADAM_B2 * v_ref[...] + (1.0 - ADAM_B2) * (g * g)
        m_hat = nm / (1.0 - ADAM_B1 ** ADAM_STEP)
        v_hat = nv / (1.0 - ADAM_B2 ** ADAM_STEP)
        d_ref[...] = -ADAM_LR * (m_hat / (jnp.sqrt(v_hat) + ADAM_EPS) + ADAM_WD * w_ref[...])
        nm_ref[...] = nm
        nv_ref[...] = nv

    spec = pl.BlockSpec((tr, C), lambda i: (i, 0))
    return pl.pallas_call(body, name=name, grid=(R // tr,), in_specs=[spec] * 4, out_specs=[spec] * 3,
                          out_shape=[jax.ShapeDtypeStruct((R, C), F32)] * 3, compiler_params=_cparams(("parallel",)))(w, g, m, v)


def loss_and_grad(name, x, target, L, tm=ROW_TILE):
    T, D = x.shape
    nlt = L // tm

    def body(x_ref, t_ref, loss_ref, dx_ref):
        i = pl.program_id(0)
        err = jnp.where(i < nlt, x_ref[...] - t_ref[...], 0.0)
        dx_ref[...] = err * (1.0 / D)
        part = 0.5 * jnp.sum(jnp.sum(err * err, axis=1, keepdims=True), axis=0, keepdims=True) * (1.0 / D)
        _accumulate(loss_ref, part, i == 0)

    return pl.pallas_call(
        body, name=name, grid=(T // tm,),
        in_specs=[pl.BlockSpec((tm, D), lambda i: (i, 0)), pl.BlockSpec((tm, D), lambda i: (jnp.minimum(i, nlt - 1), 0))],
        out_specs=[pl.BlockSpec((1, 1), lambda i: (0, 0)), pl.BlockSpec((tm, D), lambda i: (i, 0))],
        out_shape=[jax.ShapeDtypeStruct((1, 1), F32), jax.ShapeDtypeStruct((T, D), F32)],
        compiler_params=_cparams(("arbitrary",)))(x, target)


def small_fwd(name, fn, arrays, out_shapes):
    def body(*refs):
        res = fn(*[r[...] for r in refs[:len(arrays)]])
        for o_ref, r in zip(refs[len(arrays):], res):
            o_ref[...] = r

    return pl.pallas_call(body, name=name, out_shape=[jax.ShapeDtypeStruct(s, F32) for s in out_shapes])(*arrays)


def small_bwd(name, fn, arrays, cts):
    n = len(arrays)

    def body(*refs):
        _, vjp = jax.vjp(lambda *a: tuple(fn(*a)), *[r[...] for r in refs[:n]])
        grads = vjp(tuple(r[...] for r in refs[n:n + len(cts)]))
        for o_ref, g in zip(refs[n + len(cts):], grads):
            o_ref[...] = g

    return pl.pallas_call(body, name=name, out_shape=[jax.ShapeDtypeStruct(a.shape, F32) for a in arrays])(*arrays, *cts)


def fn_silu(x):
    return (x * jax.nn.sigmoid(x),)


FWD_NAMES = ["x", "c", "ctx", "c_ctx", "w_mod", "b_mod", "norm1_g", "norm2_g", "w_in", "ssm_conv_w", "ssm_conv_b",
             "ssm_dt_bias", "ssm_a_log", "ssm_d", "ssm_norm_g", "swa_q_norm_g", "swa_k_norm_g", "swa_sink", "mla_q_lat_g",
             "mla_kv_lat_g", "w_mla_uq", "w_mla_ukv", "mla_q_norm_g", "mla_k_norm_g", "w_p_ssm", "w_p_swa", "w_p_mla",
             "w_out", "w_ffn_in", "w_ffn_out"]
WEIGHT_NAMES = FWD_NAMES[3:]
GATHERED = ["w_in", "w_mla_uq", "w_mla_ukv", "w_p_ssm", "w_p_swa", "w_p_mla", "w_out", "w_ffn_in", "w_ffn_out"]
COLUMN_SHARDED = ("w_in", "w_mla_uq", "w_mla_ukv", "w_ffn_in")
REPLICATED = ["c_ctx", "b_mod", "norm1_g", "norm2_g", "ssm_conv_b", "ssm_dt_bias", "ssm_a_log", "ssm_d", "ssm_norm_g",
              "swa_q_norm_g", "swa_k_norm_g", "swa_sink", "mla_q_lat_g", "mla_kv_lat_g", "mla_q_norm_g", "mla_k_norm_g"]
IN_SEGS = [("xbc", SSM_CONV_DIM), ("dt", 2 * SSM_HEADS), ("ks", SWA_KV_HEADS * SWA_HEAD_DIM), ("vs", SWA_KV_HEADS * SWA_HEAD_DIM),
           ("ckv", MLA_KV_RANK), ("kr", MLA_ROPE), ("z", SSM_INNER), ("qs", SWA_Q_HEADS * SWA_HEAD_DIM), ("cq", MLA_Q_RANK),
           ("g1", None), ("g2", None), ("g3", None)]


def _pack(vectors, multiple):
    flat = jnp.concatenate([v.reshape(-1) for v in vectors])
    pad = (-flat.shape[0]) % multiple
    return jnp.pad(flat, (0, pad)).reshape(-1, LANES)


def _unpack(packed, shapes):
    flat, out, off = packed.reshape(-1), [], 0
    for s in shapes:
        n = int(np.prod(s))
        out.append(flat[off:off + n].reshape(s))
        off += n
    return out


def _rope_tables(L, T, rot_dim):
    nf = rot_dim // 4
    inv = jnp.power(ROPE_BASE, -jnp.arange(nf, dtype=F32) / nf)
    r, col = jnp.meshgrid(jnp.arange(L // GRID_W, dtype=F32), jnp.arange(GRID_W, dtype=F32), indexing="ij")
    ang = jnp.stack([r.reshape(-1)[:, None] * inv, col.reshape(-1)[:, None] * inv], axis=1)
    cos, sin = jnp.cos(ang), jnp.sin(ang)
    c = jnp.concatenate([cos[:, 0], cos[:, 0], cos[:, 1], cos[:, 1]], axis=1)
    s = jnp.concatenate([-sin[:, 0], sin[:, 0], -sin[:, 1], sin[:, 1]], axis=1)
    c = jnp.pad(c, ((0, T - L), (0, LANES - rot_dim)), constant_values=1.0)
    s = jnp.pad(s, ((0, T - L), (0, LANES - rot_dim)))
    return c, s


def _pad_rows(a, rows):
    return jnp.pad(a, ((0, rows - a.shape[0]), (0, 0)))


def _row(w, per_head=0, off=0, diff=True):
    return ("row", w, per_head, off, diff)


PAR, PAR_ND = ("par", True), ("par", False)
GRP = ("grp", True)


def kernel(*args):
    n_fwd, n_w = len(FWD_NAMES), len(WEIGHT_NAMES)
    inp = dict(zip(FWD_NAMES, args[:n_fwd]))
    loss_target = args[n_fwd]
    mom_m = dict(zip(WEIGHT_NAMES, args[n_fwd + 1:n_fwd + 1 + n_w]))
    mom_v = dict(zip(WEIGHT_NAMES, args[n_fwd + 1 + n_w:]))

    x, ctx = inp["x"][0], inp["ctx"][0]
    L, D = x.shape
    n_ctx = ctx.shape[0]
    T = L + n_ctx
    depth = inp["w_in"].shape[0]
    me = _my_index()
    in_widths = [w if w is not None else D for _, w in IN_SEGS]
    in_offs = np.concatenate([[0], np.cumsum(in_widths)]).tolist()
    ffn_h = inp["w_ffn_out"].shape[1] * N_DEV
    cfg_swa = AttnCfg(SWA_Q_HEADS, SWA_Q_HEADS // SWA_KV_HEADS, SWA_HEAD_DIM, SWA_HEAD_DIM, SWA_HEAD_DIM ** -0.5, SWA_WINDOW,
                      True, L, T, 256, 1)
    cfg_mla = AttnCfg(MLA_HEADS, 1, MLA_QK_PAD, MLA_V, MLA_QK ** -0.5, None, False, L, T, 1024, 2)

    def rf(name, fn, descs, arrays, outs, heads=1):
        return rowop_fwd(name, fn, descs, arrays, outs, T, L, heads=heads)

    def rb(name, fn, descs, arrays, outs, cts, heads=1, add=None):
        return rowop_bwd(name, fn, descs, arrays, outs, cts, T, L, heads=heads, add=add)

    local = []
    for n in GATHERED:
        w = inp[n]
        local.append((jnp.swapaxes(w, 1, 2) if n in COLUMN_SHARDED else w).astype(BF16))
    gathered = dict(zip(GATHERED, all_gather_hbm("gather_weights", local)))

    def full(n, l):
        g = gathered[n][:, l]
        return g.reshape(g.shape[0] * g.shape[1], g.shape[2])

    def layer_weights(l):
        wt = {}
        w_in_t = full("w_in", l)
        for (sn, _), o, w in zip(IN_SEGS, in_offs, in_widths):
            seg = w_in_t[o:o + w]
            wt[sn] = _pad_rows(seg, LANES) if sn == "kr" else seg
        uq = full("w_mla_uq", l).reshape(MLA_HEADS, MLA_QK, MLA_Q_RANK)
        wt["uqn"] = uq[:, :MLA_NOPE].reshape(MLA_HEADS * MLA_NOPE, MLA_Q_RANK)
        wt["uqr"] = jnp.pad(uq[:, MLA_NOPE:], ((0, 0), (0, LANES - MLA_ROPE), (0, 0))).reshape(MLA_HEADS * LANES, MLA_Q_RANK)
        ukv = full("w_mla_ukv", l).reshape(MLA_HEADS, MLA_NOPE + MLA_V, MLA_KV_RANK)
        wt["uk"] = ukv[:, :MLA_NOPE].reshape(MLA_HEADS * MLA_NOPE, MLA_KV_RANK)
        wt["uv"] = ukv[:, MLA_NOPE:].reshape(MLA_HEADS * MLA_V, MLA_KV_RANK)
        for n in ("w_p_ssm", "w_p_swa", "w_p_mla", "w_out", "w_ffn_out"):
            wt[n] = full(n, l)
        ffn_in_t = full("w_ffn_in", l)
        wt["fg"], wt["fu"] = ffn_in_t[:ffn_h], ffn_in_t[ffn_h:]
        return wt

    def layer_params(l):
        p = {}
        for n in ("norm1_g", "norm2_g", "ssm_conv_b", "ssm_norm_g", "swa_q_norm_g", "swa_k_norm_g", "mla_q_lat_g", "mla_kv_lat_g"):
            p[n] = inp[n][l][None]
        p["dt_bias"] = inp["ssm_dt_bias"][l].reshape(1, 2 * SSM_HEADS)
        p["alog_row"] = [inp["ssm_a_log"][l][d][None] for d in range(2)]
        p["alog_col"] = [inp["ssm_a_log"][l][d][:, None] for d in range(2)]
        p["d_lane"] = jnp.repeat(inp["ssm_d"][l], SSM_HEAD_DIM)[None]
        p["sink"] = inp["swa_sink"][l].reshape(SWA_Q_HEADS, 1, 1)
        for n, key in (("mla_q_norm_g", "gq"), ("mla_k_norm_g", "gk")):
            g = inp[n][l]
            p[key + "n"] = g[:MLA_NOPE][None]
            p[key + "r"] = jnp.pad(g[MLA_NOPE:], (0, LANES - MLA_ROPE))[None]
        return p

    conv_local = _pack([inp["ssm_conv_w"]], 8 * LANES)
    conv_all = all_gather_vmem("gather_conv_w", conv_local)
    cw = inp["ssm_conv_w"].shape
    conv_full = conv_all.reshape(N_DEV, -1)[:, :cw[0] * cw[1] * cw[2]].reshape(N_DEV, cw[0], cw[1], cw[2])
    conv_full = jnp.moveaxis(conv_full, 0, 2).reshape(cw[0], cw[1], N_DEV * cw[2])
    conv_w8 = jnp.pad(conv_full, ((0, 0), (0, 8 - cw[1]), (0, 0)))

    silu_c, silu_cc = small_fwd("silu_c", lambda a, b: fn_silu(a) + fn_silu(b), [inp["c"], inp["c_ctx"][None]], [(1, D), (1, D)])
    silu_all = all_gather_vmem("gather_silu_c", silu_c.reshape(D // LANES, LANES)).reshape(N_DEV, D)
    S_rows = 2 * N_DEV
    S_mat = jnp.concatenate([silu_all, silu_cc, jnp.zeros((S_rows - N_DEV - 1, D), F32)], axis=0)
    mod_cols = inp["w_mod"].shape[2]
    mods_local = []
    for l in range(depth):
        bias = lax.dynamic_slice(inp["b_mod"][l], (me * mod_cols,), (mod_cols,))
        mods_local.append(matmul(S_mat, inp["w_mod"][l], "nn", f"mod{l}", add=jnp.broadcast_to(bias[None], (S_rows, mod_cols))))
    mods_all = all_gather_vmem("gather_mods", jnp.stack(mods_local).reshape(-1, LANES))
    mods_all = jnp.moveaxis(mods_all.reshape(N_DEV, depth, S_rows, mod_cols), 0, 2).reshape(depth, S_rows, N_DEV * mod_cols)
    mods_lat = lax.dynamic_slice(mods_all, (0, me, 0), (depth, 1, N_DEV * mod_cols))[:, 0]
    mods_ctx = mods_all[:, N_DEV]

    def layer_mods(l):
        return [jnp.stack([mods_lat[l, j * D:(j + 1) * D], mods_ctx[l, j * D:(j + 1) * D]])[:, None] for j in range(6)]

    cs_swa = _rope_tables(L, T, SWA_HEAD_DIM)
    cs_mla = _rope_tables(L, T, MLA_ROPE)
    nm_descs = [_row(D), PAR, GRP, GRP]
    resid_descs = [_row(D, diff=False), _row(D), GRP]
    tab = [_row(LANES, diff=False), _row(LANES, diff=False)]
    swaq_descs = [_row(SWA_HEAD_DIM, 1), PAR] + tab
    swakv_descs = [_row(SWA_HEAD_DIM, 1), _row(SWA_HEAD_DIM, 1), PAR] + tab
    mlaq_descs = [_row(LANES, 1), _row(LANES, 1), PAR, PAR] + tab
    mlakv_descs = [_row(LANES, 1), _row(LANES, 1), _row(LANES), PAR, PAR] + tab
    ssdout_descs = [_row(SSM_INNER), _row(SSM_INNER, diff=False), _row(SSM_INNER), _row(SSM_INNER), PAR, PAR]
    merge_descs = [_row(D)] * 6
    swiglu_descs = [_row(ffn_h), _row(ffn_h)]
    seg_names = [sn for sn, _ in IN_SEGS]

    def layer_fwd(l, X, wt, p, mods):
        sh1, sc1, gt1, sh2, sc2, gt2 = mods
        r = {"X": X}
        r["h1"] = rf(f"l{l}_norm1", fn_norm_mod, nm_descs, [X, p["norm1_g"], sh1, sc1], [(D, 0, BF16)])[0]
        for sn in seg_names:
            r[sn] = matmul(r["h1"], wt[sn], "nt", f"l{l}_in_{sn}")
        r["u"] = conv_fwd(f"l{l}_conv", r["xbc"], conv_w8[l], p["ssm_conv_b"], L)
        r["dts"] = rf(f"l{l}_softplus", fn_softplus, [_row(2 * SSM_HEADS), PAR], [r["dt"], p["dt_bias"]], [(2 * SSM_HEADS, 0, F32)])[0]
        for d in range(2):
            dt_d = r["dts"][:, d * SSM_HEADS:(d + 1) * SSM_HEADS]
            r[f"dt{d}"], r[f"dtt{d}"] = dt_d, dt_d.T
            r[f"y{d}"], r[f"hin{d}"] = ssd_fwd(f"l{l}_ssd{d}", d, r["u"], dt_d, dt_d.T, p["alog_row"][d], p["alog_col"][d], L)
        r["ys"] = rf(f"l{l}_ssd_out", fn_ssd_out, ssdout_descs, [r["y0"], r["y1"], r["u"], r["z"], p["d_lane"], p["ssm_norm_g"]],
                     [(SSM_INNER, 0, F32)])[0]
        r["Qs"] = rf(f"l{l}_swa_q", fn_swa_q, swaq_descs, [r["qs"], p["swa_q_norm_g"], *cs_swa], [(SWA_HEAD_DIM, 1, BF16)], SWA_Q_HEADS)[0]
        r["Ks"], r["Vs"] = rf(f"l{l}_swa_kv", fn_swa_kv, swakv_descs, [r["ks"], r["vs"], p["swa_k_norm_g"], *cs_swa],
                              [(SWA_HEAD_DIM, 1, BF16), (SWA_HEAD_DIM, 1, BF16)], SWA_KV_HEADS)
        r["Os"], r["lse_s"] = flash_fwd(f"l{l}_swa_fwd", cfg_swa, r["Qs"], r["Ks"], r["Vs"], p["sink"])
        r["cqn"] = rf(f"l{l}_q_lat", fn_rms, [_row(MLA_Q_RANK), PAR], [r["cq"], p["mla_q_lat_g"]], [(MLA_Q_RANK, 0, BF16)])[0]
        r["qn"] = matmul(r["cqn"], wt["uqn"], "nt", f"l{l}_uqn")
        r["qr"] = matmul(r["cqn"], wt["uqr"], "nt", f"l{l}_uqr")
        r["Qm"] = rf(f"l{l}_mla_q", fn_mla_q, mlaq_descs, [r["qn"], r["qr"], p["gqn"], p["gqr"], *cs_mla], [(MLA_QK_PAD, 1, BF16)], MLA_HEADS)[0]
        r["ckvn"] = rf(f"l{l}_kv_lat", fn_rms, [_row(MLA_KV_RANK), PAR], [r["ckv"], p["mla_kv_lat_g"]], [(MLA_KV_RANK, 0, BF16)])[0]
        r["kn"] = matmul(r["ckvn"], wt["uk"], "nt", f"l{l}_uk")
        r["vp"] = matmul(r["ckvn"], wt["uv"], "nt", f"l{l}_uv")
        r["Km"], r["Vm"] = rf(f"l{l}_mla_kv", fn_mla_kv, mlakv_descs, [r["kn"], r["vp"], r["kr"], p["gkn"], p["gkr"], *cs_mla],
                              [(MLA_QK_PAD, 1, BF16), (MLA_V, 1, BF16)], MLA_HEADS)
        r["Om"], r["lse_m"] = flash_fwd(f"l{l}_mla_fwd", cfg_mla, r["Qm"], r["Km"], r["Vm"], None)
        r["P1"] = matmul(r["ys"], wt["w_p_ssm"], "nn", f"l{l}_p_ssm")
        r["P2"] = matmul(r["Os"], wt["w_p_swa"], "nn", f"l{l}_p_swa")
        r["P3"] = matmul(r["Om"], wt["w_p_mla"], "nn", f"l{l}_p_mla")
        r["mg"] = rf(f"l{l}_merge", fn_merge, merge_descs, [r["g1"], r["g2"], r["g3"], r["P1"], r["P2"], r["P3"]], [(D, 0, BF16)])[0]
        r["A"] = matmul(r["mg"], wt["w_out"], "nn", f"l{l}_out")
        r["X1"] = rf(f"l{l}_resid1", fn_resid, resid_descs, [X, r["A"], gt1], [(D, 0, F32)])[0]
        r["h2"] = rf(f"l{l}_norm2", fn_norm_mod, nm_descs, [r["X1"], p["norm2_g"], sh2, sc2], [(D, 0, BF16)])[0]
        r["Fg"] = matmul(r["h2"], wt["fg"], "nt", f"l{l}_ffn_g")
        r["Fu"] = matmul(r["h2"], wt["fu"], "nt", f"l{l}_ffn_u")
        r["sg"] = rf(f"l{l}_swiglu", fn_swiglu, swiglu_descs, [r["Fg"], r["Fu"]], [(ffn_h, 0, BF16)])[0]
        r["B"] = matmul(r["sg"], wt["w_ffn_out"], "nn", f"l{l}_ffn_out")
        X2 = rf(f"l{l}_resid2", fn_resid, resid_descs, [r["X1"], r["B"], gt2], [(D, 0, F32)])[0]
        return X2, r

    def attn_bwd(tag, cfg, q, k, v, o, lse, do, sink):
        res = attn_delta(f"{tag}_delta", cfg, o, do, lse, sink)
        delta, dob = res[0], res[1]
        dsink = res[2] if cfg.has_sink else None
        if cfg.window is None and cfg.group == 1:
            dq, dk, dv = flash_bwd_fused(f"{tag}_bwd", cfg, q, k, v, dob, lse, delta)
        else:
            dq = flash_dq(f"{tag}_dq", cfg, q, k, v, dob, lse, delta)
            dk, dv = flash_dkv(f"{tag}_dkv", cfg, q, k, v, dob, lse, delta)
        return dq, dk, dv, dsink

    def layer_bwd(l, dX2, r, wt, p, mods):
        sh1, sc1, gt1, sh2, sc2, gt2 = mods
        g, gw = {}, {}
        dmod = [None] * 6
        dB, dmod[5] = rb(f"l{l}_resid2_b", fn_resid, resid_descs, [r["X1"], r["B"], gt2], [(D, 0, F32)], [dX2])
        dsg = matmul(dB, wt["w_ffn_out"], "nt", f"l{l}_ffn_out_da")
        gw["w_ffn_out"] = matmul(r["sg"], dB, "tn", f"l{l}_ffn_out_dw")
        dFg, dFu = rb(f"l{l}_swiglu_b", fn_swiglu, swiglu_descs, [r["Fg"], r["Fu"]], [(ffn_h, 0, BF16)], [dsg])
        dh2 = matmul(dFg, wt["fg"], "nn", f"l{l}_ffn_g_da")
        dh2 = matmul(dFu, wt["fu"], "nn", f"l{l}_ffn_u_da", add=dh2)
        gw["w_ffn_in"] = jnp.concatenate([matmul(r["h2"], dFg, "tn", f"l{l}_ffn_g_dw"), matmul(r["h2"], dFu, "tn", f"l{l}_ffn_u_dw")], axis=1)
        dX1, g["norm2_g"], dmod[3], dmod[4] = rb(f"l{l}_norm2_b", fn_norm_mod, nm_descs, [r["X1"], p["norm2_g"], sh2, sc2],
                                                [(D, 0, BF16)], [dh2], add={0: dX2})
        dA, dmod[2] = rb(f"l{l}_resid1_b", fn_resid, resid_descs, [r["X"], r["A"], gt1], [(D, 0, F32)], [dX1])
        dmg = matmul(dA, wt["w_out"], "nt", f"l{l}_out_da")
        gw["w_out"] = matmul(r["mg"], dA, "tn", f"l{l}_out_dw")
        dsegs = {}
        dsegs["g1"], dsegs["g2"], dsegs["g3"], dP1, dP2, dP3 = rb(
            f"l{l}_merge_b", fn_merge, merge_descs, [r["g1"], r["g2"], r["g3"], r["P1"], r["P2"], r["P3"]], [(D, 0, BF16)], [dmg])
        dys = matmul(dP1, wt["w_p_ssm"], "nt", f"l{l}_p_ssm_da")
        dOs = matmul(dP2, wt["w_p_swa"], "nt", f"l{l}_p_swa_da")
        dOm = matmul(dP3, wt["w_p_mla"], "nt", f"l{l}_p_mla_da")
        gw["w_p_ssm"] = matmul(r["ys"], dP1, "tn", f"l{l}_p_ssm_dw")
        gw["w_p_swa"] = matmul(r["Os"], dP2, "tn", f"l{l}_p_swa_dw")
        gw["w_p_mla"] = matmul(r["Om"], dP3, "tn", f"l{l}_p_mla_dw")
        dQm, dKm, dVm, _ = attn_bwd(f"l{l}_mla", cfg_mla, r["Qm"], r["Km"], r["Vm"], r["Om"], r["lse_m"], dOm, None)
        dkn, dvp, dsegs["kr"], dgkn, dgkr = rb(f"l{l}_mla_kv_b", fn_mla_kv, mlakv_descs,
                                               [r["kn"], r["vp"], r["kr"], p["gkn"], p["gkr"], *cs_mla],
                                               [(MLA_QK_PAD, 1, BF16), (MLA_V, 1, BF16)], [dKm, dVm], MLA_HEADS)
        dckvn = matmul(dkn, wt["uk"], "nn", f"l{l}_uk_da")
        dckvn = matmul(dvp, wt["uv"], "nn", f"l{l}_uv_da", add=dckvn)
        dw_uk = matmul(r["ckvn"], dkn, "tn", f"l{l}_uk_dw").reshape(MLA_KV_RANK, MLA_HEADS, MLA_NOPE)
        dw_uv = matmul(r["ckvn"], dvp, "tn", f"l{l}_uv_dw").reshape(MLA_KV_RANK, MLA_HEADS, MLA_V)
        gw["w_mla_ukv"] = jnp.concatenate([dw_uk, dw_uv], axis=2).reshape(MLA_KV_RANK, -1)
        dsegs["ckv"], g["mla_kv_lat_g"] = rb(f"l{l}_kv_lat_b", fn_rms, [_row(MLA_KV_RANK), PAR], [r["ckv"], p["mla_kv_lat_g"]],
                                             [(MLA_KV_RANK, 0, BF16)], [dckvn])
        dqn, dqr, dgqn, dgqr = rb(f"l{l}_mla_q_b", fn_mla_q, mlaq_descs, [r["qn"], r["qr"], p["gqn"], p["gqr"], *cs_mla],
                                  [(MLA_QK_PAD, 1, BF16)], [dQm], MLA_HEADS)
        dcqn = matmul(dqn, wt["uqn"], "nn", f"l{l}_uqn_da")
        dcqn = matmul(dqr, wt["uqr"], "nn", f"l{l}_uqr_da", add=dcqn)
        dw_uqn = matmul(r["cqn"], dqn, "tn", f"l{l}_uqn_dw").reshape(MLA_Q_RANK, MLA_HEADS, MLA_NOPE)
        dw_uqr = matmul(r["cqn"], dqr, "tn", f"l{l}_uqr_dw").reshape(MLA_Q_RANK, MLA_HEADS, LANES)[:, :, :MLA_ROPE]
        gw["w_mla_uq"] = jnp.concatenate([dw_uqn, dw_uqr], axis=2).reshape(MLA_Q_RANK, -1)
        dsegs["cq"], g["mla_q_lat_g"] = rb(f"l{l}_q_lat_b", fn_rms, [_row(MLA_Q_RANK), PAR], [r["cq"], p["mla_q_lat_g"]],
                                           [(MLA_Q_RANK, 0, BF16)], [dcqn])
        g["mla_q_norm_g"] = jnp.concatenate([dgqn[0], dgqr[0, :MLA_ROPE]])
        g["mla_k_norm_g"] = jnp.concatenate([dgkn[0], dgkr[0, :MLA_ROPE]])
        dQs, dKs, dVs, dsink = attn_bwd(f"l{l}_swa", cfg_swa, r["Qs"], r["Ks"], r["Vs"], r["Os"], r["lse_s"], dOs, p["sink"])
        g["swa_sink"] = dsink.reshape(SWA_Q_HEADS)
        dsegs["qs"], g["swa_q_norm_g"] = rb(f"l{l}_swa_q_b", fn_swa_q, swaq_descs, [r["qs"], p["swa_q_norm_g"], *cs_swa],
                                            [(SWA_HEAD_DIM, 1, BF16)], [dQs], SWA_Q_HEADS)
        dsegs["ks"], dsegs["vs"], g["swa_k_norm_g"] = rb(f"l{l}_swa_kv_b", fn_swa_kv, swakv_descs,
                                                         [r["ks"], r["vs"], p["swa_k_norm_g"], *cs_swa],
                                                         [(SWA_HEAD_DIM, 1, BF16), (SWA_HEAD_DIM, 1, BF16)], [dKs, dVs], SWA_KV_HEADS)
        dy, dxs, dsegs["z"], dd_lane, g["ssm_norm_g"] = rb(
            f"l{l}_ssd_out_b", fn_ssd_out, ssdout_descs, [r["y0"], r["y1"], r["u"], r["z"], p["d_lane"], p["ssm_norm_g"]],
            [(SSM_INNER, 0, F32)], [dys])
        g["ssm_d"] = dd_lane.reshape(SSM_HEADS, SSM_HEAD_DIM).sum(axis=1)
        du, ddts, dalog = None, [], []
        for d in range(2):
            du, ddt, ddtt, dar, dac = ssd_bwd(f"l{l}_ssd{d}_b", d, r["u"], r[f"dt{d}"], r[f"dtt{d}"], p["alog_row"][d], p["alog_col"][d],
                                              r[f"hin{d}"], dy, L, add_x=dxs if d == 0 else None, add_u=du)
            ddts.append(ddt + ddtt.T)
            dalog.append(dar[0] + dac[:, 0])
        g["ssm_a_log"] = jnp.stack(dalog)
        dsegs["xbc"], dconv_w, g["ssm_conv_b"] = conv_bwd(f"l{l}_conv_b", r["xbc"], conv_w8[l], p["ssm_conv_b"], du, L)
        dsegs["dt"], ddt_bias = rb(f"l{l}_softplus_b", fn_softplus, [_row(2 * SSM_HEADS), PAR], [r["dt"], p["dt_bias"]],
                                   [(2 * SSM_HEADS, 0, F32)], [jnp.concatenate(ddts, axis=1)])
        g["ssm_dt_bias"] = ddt_bias.reshape(2, SSM_HEADS)
        g["ssm_conv_w"] = dconv_w[:SSM_CONV]
        dh1, dws = None, []
        for sn, w in zip(seg_names, in_widths):
            dh1 = matmul(dsegs[sn], wt[sn], "nn", f"l{l}_in_{sn}_da", add=dh1)
            dws.append(matmul(r["h1"], dsegs[sn], "tn", f"l{l}_in_{sn}_dw")[:, :w])
        gw["w_in"] = jnp.concatenate(dws, axis=1)
        dX, g["norm1_g"], dmod[0], dmod[1] = rb(f"l{l}_norm1_b", fn_norm_mod, nm_descs, [r["X"], p["norm1_g"], sh1, sc1],
                                               [(D, 0, BF16)], [dh1], add={0: dX1})
        for n in ("norm1_g", "norm2_g", "ssm_conv_b", "ssm_norm_g", "swa_q_norm_g", "swa_k_norm_g", "mla_q_lat_g", "mla_kv_lat_g"):
            g[n] = g[n][0]
        dmod_lat = jnp.concatenate([dm[0, 0] for dm in dmod])
        dmod_ctx = jnp.concatenate([dm[1, 0] for dm in dmod])
        return dX, g, gw, dmod_lat, dmod_ctx

    X = jnp.concatenate([x, ctx], axis=0)
    saved = []
    for l in range(depth):
        wt, p, mods = layer_weights(l), layer_params(l), layer_mods(l)
        X, r = layer_fwd(l, X, wt, p, mods)
        saved.append((r, wt, p, mods))
    loss_part, dX = loss_and_grad("loss", X, loss_target[0], L)
    loss = lax.psum(loss_part[0, 0], ("x", "y", "c"))
    small_g = [None] * depth
    big_g = [None] * depth
    dmods = [None] * depth
    for l in reversed(range(depth)):
        r, wt, p, mods = saved[l]
        dX, small_g[l], big_g[l], dm_lat, dm_ctx = layer_bwd(l, dX, r, wt, p, mods)
        dmods[l] = jnp.stack([dm_lat, dm_ctx])
    grad_x = dX[:L][None]

    dm_all = all_gather_vmem("gather_dmods", jnp.stack(dmods).reshape(-1, LANES)).reshape(N_DEV, depth, 2, N_DEV * mod_cols)
    dm_rows = jnp.concatenate([jnp.moveaxis(dm_all[:, :, 0], 0, 1), dm_all[:, :, 1].sum(axis=0)[:, None],
                               jnp.zeros((depth, S_rows - N_DEV - 1, N_DEV * mod_cols), F32)], axis=1)
    dm_mine = lax.dynamic_slice(dm_rows, (0, 0, me * mod_cols), (depth, S_rows, mod_cols))
    grads = {}
    grads["w_mod"] = jnp.stack([matmul(S_mat, dm_mine[l], "tn", f"mod{l}_dw") for l in range(depth)])
    d_silu = None
    for l in range(depth):
        d_silu = matmul(dm_mine[l], inp["w_mod"][l], "nt", f"mod{l}_da", add=d_silu)
    small = {n: jnp.stack([small_g[l][n] for l in range(depth)]) for n in small_g[0]}
    small["c_ctx"] = small_bwd("silu_c_b", fn_silu, [inp["c_ctx"][None]], [d_silu[N_DEV:N_DEV + 1]])[0][0]
    small["b_mod"] = jnp.stack(dmods).sum(axis=1)

    rep_shapes = [inp[n].shape for n in REPLICATED]
    conv_shape = (depth, SSM_CONV, SSM_CONV_DIM)
    packed = _pack([small[n] for n in REPLICATED] + [small["ssm_conv_w"]], 8 * LANES)
    small_sum = sum_parts("sum_small", all_gather_vmem("gather_small", packed))
    summed = _unpack(small_sum, rep_shapes + [conv_shape])
    for n, gsum in zip(REPLICATED, summed):
        grads[n] = gsum
    grads["ssm_conv_w"] = lax.dynamic_slice(summed[-1], (0, 0, me * cw[2]), cw)

    slabs = []
    for n in GATHERED:
        gfull = jnp.stack([big_g[l][n] for l in range(depth)])
        if n in COLUMN_SHARDED:
            k_dim, n_dim = gfull.shape[1], gfull.shape[2]
            slab = jnp.moveaxis(gfull.reshape(depth, k_dim, N_DEV, n_dim // N_DEV), 2, 0)
        else:
            k_dim, n_dim = gfull.shape[1], gfull.shape[2]
            slab = jnp.moveaxis(gfull.reshape(depth, N_DEV, k_dim // N_DEV, n_dim), 1, 0)
        slabs.append(cast_bf16(f"cast_{n}", slab.reshape(-1, slab.shape[-1])).reshape(slab.shape))
    for n, parts in zip(GATHERED, exchange_hbm("exchange_grads", slabs)):
        shp = inp[n].shape
        grads[n] = sum_parts(f"sum_{n}", parts.reshape(N_DEV, shp[0] * shp[1], shp[2])).reshape(shp)

    delta, new_m, new_v = {}, {}, {}
    rep_pack = lambda d: _pack([d[n] for n in REPLICATED], 8 * LANES)
    rep_out = adamw("adamw_small", rep_pack(inp), rep_pack(grads), rep_pack(mom_m), rep_pack(mom_v))
    for out, res in zip((delta, new_m, new_v), rep_out):
        for n, a in zip(REPLICATED, _unpack(res, rep_shapes)):
            out[n] = a
    for n in ["w_mod", "ssm_conv_w"] + GATHERED:
        shp = inp[n].shape
        two_d = (shp[0] * shp[1], shp[2])
        res = adamw(f"adamw_{n}", inp[n].reshape(two_d), grads[n].reshape(two_d), mom_m[n].reshape(two_d), mom_v[n].reshape(two_d))
        delta[n], new_m[n], new_v[n] = [a.reshape(shp) for a in res]

    return (loss, grad_x, *[grads[n] for n in WEIGHT_NAMES], *[delta[n] for n in WEIGHT_NAMES],
            *[new_m[n] for n in WEIGHT_NAMES], *[new_v[n] for n in WEIGHT_NAMES])
```

```python
import functools
import math

import numpy as np
import jax
import jax.numpy as jnp
from jax import lax
from jax.experimental import pallas as pl
from jax.experimental.pallas import tpu as pltpu

F32 = jnp.float32
BF16 = jnp.bfloat16

N_DEV = 8
V7X_VMEM_BYTES = 64 * 1024 * 1024
VMEM_LIMIT_BYTES = V7X_VMEM_BYTES - 8 * 1024 * 1024
LANES = 128

EPS = 1e-6
ROPE_BASE = 10000.0
GRID_W = 64
SSM_HEADS, SSM_HEAD_DIM, SSM_GROUPS, SSM_STATE, SSM_CONV, SSM_CHUNK = 16, 64, 2, 128, 5, 128
SSM_INNER = SSM_HEADS * SSM_HEAD_DIM
SSM_CONV_DIM = SSM_INNER + 2 * SSM_GROUPS * SSM_STATE
SWA_Q_HEADS, SWA_KV_HEADS, SWA_HEAD_DIM, SWA_WINDOW = 8, 2, 128, 128
MLA_HEADS, MLA_Q_RANK, MLA_KV_RANK, MLA_NOPE, MLA_ROPE, MLA_V = 8, 384, 256, 128, 64, 128
MLA_QK = MLA_NOPE + MLA_ROPE
MLA_QK_PAD = 2 * LANES
ADAM_LR, ADAM_B1, ADAM_B2, ADAM_EPS, ADAM_WD, ADAM_STEP = 0.001, 0.9, 0.999, 1e-08, 0.01, 10

ROW_TILE = 256


def _cparams(sem, **kw):
    return pltpu.CompilerParams(dimension_semantics=sem, vmem_limit_bytes=VMEM_LIMIT_BYTES, **kw)


def _pick(dim, prefs):
    for p in prefs:
        if dim % p == 0:
            return p
    return dim


def matmul(a, b, mode, name, out_dtype=F32, add=None):
    if mode == "nn":
        (M, K), (K2, N) = a.shape, b.shape
    elif mode == "nt":
        (M, K), (N, K2) = a.shape, b.shape
    else:
        (K, M), (K2, N) = a.shape, b.shape
    assert K == K2, (name, a.shape, b.shape)
    has_add = add is not None
    tm, tn, tk = _matmul_tiles(M, N, K, a.dtype.itemsize, b.dtype.itemsize, jnp.dtype(out_dtype).itemsize, has_add,
                                   m_on_lanes=(mode == "tn"))
    nk = K // tk
    dims = {"nn": (((1,), (0,)), ((), ())), "nt": (((1,), (1,)), ((), ())), "tn": (((0,), (0,)), ((), ()))}[mode]
    a_spec = pl.BlockSpec((tk, tm), lambda i, j, k: (k, i)) if mode == "tn" else pl.BlockSpec((tm, tk), lambda i, j, k: (i, k))
    b_spec = pl.BlockSpec((tn, tk), lambda i, j, k: (j, k)) if mode == "nt" else pl.BlockSpec((tk, tn), lambda i, j, k: (k, j))
    o_spec = pl.BlockSpec((tm, tn), lambda i, j, k: (i, j))

    def body(*refs):
        a_ref, b_ref = refs[:2]
        c_ref = refs[2] if has_add else None
        o_ref = refs[3] if has_add else refs[2]
        part = lax.dot_general(a_ref[...].astype(BF16), b_ref[...].astype(BF16), dims, preferred_element_type=F32)
        if nk == 1:
            o_ref[...] = (part + c_ref[...] if has_add else part).astype(o_ref.dtype)
            return
        acc_ref = refs[-1]
        k = pl.program_id(2)

        @pl.when(k == 0)
        def _():
            acc_ref[...] = part + c_ref[...] if has_add else part

        @pl.when(k > 0)
        def _():
            acc_ref[...] += part

        @pl.when(k == nk - 1)
        def _():
            o_ref[...] = acc_ref[...].astype(o_ref.dtype)

    ins = [a, b] + ([add] if has_add else [])
    in_specs = [a_spec, b_spec] + ([o_spec] if has_add else [])
    return pl.pallas_call(
        body, name=name, grid=(M // tm, N // tn, nk), in_specs=in_specs, out_specs=o_spec,
        out_shape=jax.ShapeDtypeStruct((M, N), out_dtype),
        scratch_shapes=[pltpu.VMEM((tm, tn), F32)] if nk > 1 else [],
        input_output_aliases=({2: 0} if has_add else {}),
        compiler_params=_cparams(("parallel", "parallel", "arbitrary")),
    )(*ins)


def matmul_sum(name, pairs, add=None):
    M, N = pairs[0][0].shape[0], pairs[0][1].shape[1]
    n = len(pairs)
    has_add = add is not None
    resident = sum(2 * b.shape[0] * N * b.dtype.itemsize for _, b in pairs)
    tm = next((t for t in (768, 512, 384, 256, 128) if M % t == 0 and resident + sum(
        2 * t * a.shape[1] * a.dtype.itemsize + t * a.shape[1] * 2 for a, _ in pairs) + 6 * t * N * 4 <= MATMUL_VMEM_BUDGET), None)
    assert tm is not None, name

    def body(*refs):
        acc = refs[2 * n][...] if has_add else None
        for s in range(n):
            part = jnp.dot(refs[2 * s][...].astype(BF16), refs[2 * s + 1][...].astype(BF16), preferred_element_type=F32)
            acc = part if acc is None else acc + part
        refs[-1][...] = acc

    o_spec = pl.BlockSpec((tm, N), lambda i: (i, 0))
    in_specs, ins = [], []
    for a, b in pairs:
        in_specs += [pl.BlockSpec((tm, a.shape[1]), lambda i: (i, 0)), pl.BlockSpec(b.shape, lambda i: (0, 0))]
        ins += [a, b]
    if has_add:
        in_specs.append(o_spec)
        ins.append(add)
    return pl.pallas_call(body, name=name, grid=(M // tm,), in_specs=in_specs, out_specs=o_spec,
                          out_shape=jax.ShapeDtypeStruct((M, N), F32), input_output_aliases=({2 * n: 0} if has_add else {}),
                          compiler_params=_cparams(("parallel",)))(*ins)


def matmul_multi(name, a, bs):
    M, K = a.shape
    n = len(bs)
    resident = sum(2 * b.shape[0] * K * b.dtype.itemsize for b in bs)
    n_total = sum(b.shape[0] for b in bs)
    tm = next((t for t in (768, 512, 384, 256, 128) if M % t == 0 and
               resident + 2 * t * K * a.dtype.itemsize + 3 * t * n_total * 4 <= MATMUL_VMEM_BUDGET), None)
    assert tm is not None, name

    def body(*refs):
        lhs = refs[0][...].astype(BF16)
        for s in range(n):
            refs[1 + n + s][...] = lax.dot_general(lhs, refs[1 + s][...].astype(BF16), NT_DIMS, preferred_element_type=F32)

    in_specs = [pl.BlockSpec((tm, K), lambda i: (i, 0))] + [pl.BlockSpec(b.shape, lambda i: (0, 0)) for b in bs]
    return pl.pallas_call(
        body, name=name, grid=(M // tm,), in_specs=in_specs,
        out_specs=[pl.BlockSpec((tm, b.shape[0]), lambda i: (i, 0)) for b in bs],
        out_shape=[jax.ShapeDtypeStruct((M, b.shape[0]), F32) for b in bs], compiler_params=_cparams(("parallel",)))(a, *bs)


MATMUL_VMEM_BUDGET = 36 * 1024 * 1024


def _matmul_tiles(M, N, K, a_bytes, b_bytes, o_bytes, has_add, m_on_lanes=False):
    tk = K if K <= 1536 else _pick(K, (1408, 1024, 768, 704, 512, 256))
    nk = K // tk
    m_cands = [t for t in (1024, 768, 512, 384, 256, 128) if M % t == 0] or [M]
    if M % 768 and M % 1024:
        m_cands += [t for t in (1408, 704, 352) if M % t == 0 and not (m_on_lanes and t % LANES)]
    n_cands = [t for t in range(LANES, min(N, 2816) + 1, LANES) if N % t == 0] or [N]
    best = None
    for tm in m_cands:
        for tn in n_cands:
            pipeline = 2 * (tm * tk * a_bytes + tk * tn * b_bytes + tm * tn * o_bytes) + (2 * tm * tn * 4 if has_add else 0)
            temps = tm * tn * 4 * (2 if nk > 1 else 1) + (tm * tk * 2 if a_bytes == 4 else 0) + (tk * tn * 2 if b_bytes == 4 else 0)
            if pipeline + temps <= MATMUL_VMEM_BUDGET:
                score = (tm * tn, tn)
                if best is None or score > best[0]:
                    best = (score, tm, tn)
    if best is None:
        return m_cands[-1], n_cands[0], tk
    return best[1], best[2], tk


def _row_specs(descs, arrays, tm, nct, heads):
    specs = []
    for d, arr in zip(descs, arrays):
        if d[0] == "row":
            _, w, per_head, off, _ = d
            specs.append(pl.BlockSpec((tm, w * (heads if per_head else 1)), lambda i, off=off: (i, off)))
        elif d[0] == "par":
            specs.append(pl.BlockSpec(arr.shape, lambda i, nd=arr.ndim: (0,) * nd))
        else:
            specs.append(pl.BlockSpec((1,) + arr.shape[1:], lambda i, nd=arr.ndim: (jnp.where(i >= nct, 1, 0),) + (0,) * (nd - 1)))
    return specs


def _load(d, ref, h):
    if d[0] == "grp":
        return ref[0]
    if d[0] == "row" and d[2]:
        return ref[:, h * d[1]:(h + 1) * d[1]]
    return ref[...]


def _out_specs(outs, tm, heads):
    return [pl.BlockSpec((tm, w * (heads if ph else 1)), lambda i: (i, 0)) for (w, ph, _) in outs]


def rowop_fwd(name, fn, descs, arrays, outs, T, n_ctx, heads=1, tm=ROW_TILE):
    nct = n_ctx // tm
    n_in = len(descs)

    def body(*refs):
        for h in range(heads):
            res = fn(*[_load(d, r, h) for d, r in zip(descs, refs[:n_in])])
            for o_ref, r, (w, ph, _) in zip(refs[n_in:], res, outs):
                if ph:
                    o_ref[:, h * w:(h + 1) * w] = r.astype(o_ref.dtype)
                else:
                    o_ref[...] = r.astype(o_ref.dtype)

    out_shape = [jax.ShapeDtypeStruct((T, w * (heads if ph else 1)), dt) for (w, ph, dt) in outs]
    return pl.pallas_call(
        body, name=name, grid=(T // tm,), in_specs=_row_specs(descs, arrays, tm, nct, heads), out_specs=_out_specs(outs, tm, heads),
        out_shape=out_shape, compiler_params=_cparams(("parallel",)),
    )(*arrays)


def rowop_bwd(name, fn, descs, arrays, outs, cts, T, n_ctx, heads=1, tm=ROW_TILE, add=None):
    nct = n_ctx // tm
    n_in, n_ct = len(descs), len(cts)
    add = add or {}
    diff_idx = [k for k, d in enumerate(descs) if d[-1]]
    add_idx = [k for k in diff_idx if k in add]

    def body(*refs):
        in_refs, ct_refs = refs[:n_in], refs[n_in:n_in + n_ct]
        add_refs = dict(zip(add_idx, refs[n_in + n_ct:n_in + n_ct + len(add_idx)]))
        g_refs = refs[n_in + n_ct + len(add_idx):]
        i = pl.program_id(0)
        shared = {}
        for h in range(heads):
            vals = [_load(d, r, h) for d, r in zip(descs, in_refs)]

            def f(*dvals, vals=vals):
                full = list(vals)
                for k, v in zip(diff_idx, dvals):
                    full[k] = v
                return tuple(fn(*full))

            _, vjp = jax.vjp(f, *[vals[k] for k in diff_idx])
            cts_h = tuple(c[:, h * w:(h + 1) * w] if ph else c[...] for c, (w, ph, _) in zip(ct_refs, outs))
            for k, g_ref, g in zip(diff_idx, g_refs, vjp(cts_h)):
                d = descs[k]
                if d[0] == "row" and d[2]:
                    g_ref[:, h * d[1]:(h + 1) * d[1]] = g.astype(g_ref.dtype)
                else:
                    shared[k] = g if k not in shared else shared[k] + g
        for k, g_ref in zip(diff_idx, g_refs):
            d = descs[k]
            if k not in shared:
                continue
            g = shared[k]
            if d[0] == "row":
                if k in add_refs:
                    g = g + add_refs[k][...]
                g_ref[...] = g.astype(g_ref.dtype)
            elif d[0] == "par":
                _accumulate(g_ref, g, i == 0)
            else:
                _accumulate(g_ref, g[None], jnp.logical_or(i == 0, i == nct))

    in_specs = _row_specs(descs, arrays, tm, nct, heads)
    g_specs, g_shape = [], []
    for k in diff_idx:
        d = descs[k]
        if d[0] == "row":
            g_specs.append(pl.BlockSpec((tm, d[1] * (heads if d[2] else 1)), lambda i: (i, 0)))
            g_shape.append(jax.ShapeDtypeStruct((T, d[1] * (heads if d[2] else 1)), F32))
        else:
            g_specs.append(in_specs[k])
            g_shape.append(jax.ShapeDtypeStruct(arrays[k].shape, F32))
    add_specs = [g_specs[diff_idx.index(k)] for k in add_idx]
    return pl.pallas_call(
        body, name=name, grid=(T // tm,), in_specs=in_specs + _out_specs(outs, tm, heads) + add_specs, out_specs=g_specs,
        out_shape=g_shape, compiler_params=_cparams(("arbitrary",)),
    )(*arrays, *cts, *[add[k] for k in add_idx])


def _accumulate(ref, val, first):
    @pl.when(first)
    def _():
        ref[...] = val.astype(ref.dtype)

    @pl.when(jnp.logical_not(first))
    def _():
        ref[...] += val.astype(ref.dtype)


def _rms(x, count=None):
    n = x.shape[-1] if count is None else count
    return x * lax.rsqrt(jnp.sum(x * x, axis=-1, keepdims=True) * (1.0 / n) + EPS)


def _swap_halves(x, nf):
    w = x.shape[-1]
    lane = lax.broadcasted_iota(jnp.int32, x.shape, x.ndim - 1)
    return jnp.where((lane % (2 * nf)) < nf, pltpu.roll(x, w - nf, x.ndim - 1), pltpu.roll(x, nf, x.ndim - 1))


def _make_rope(nf):
    @jax.custom_vjp
    def rope(x, c, s):
        return x * c + _swap_halves(x, nf) * s

    def fwd(x, c, s):
        return rope(x, c, s), (c, s)

    def bwd(res, g):
        c, s = res
        return g * c + _swap_halves(g * s, nf), jnp.zeros_like(c), jnp.zeros_like(s)

    rope.defvjp(fwd, bwd)
    return rope


_rope_swa = _make_rope(SWA_HEAD_DIM // 4)
_rope_mla = _make_rope(MLA_ROPE // 4)


@jax.custom_vjp
def _softplus(x):
    e = jnp.exp(-jnp.abs(x))
    u = 1.0 + e
    log1p_e = jnp.where(u == 1.0, e, jnp.log(u) * e / jnp.where(u == 1.0, 1.0, u - 1.0))
    return jnp.maximum(x, 0.0) + log1p_e


_softplus.defvjp(lambda x: (_softplus(x), x), lambda x, g: (g * jax.nn.sigmoid(x),))


def fn_norm_mod(x, g, shift, scale):
    return (_rms(x) * g * (1.0 + scale) + shift,)


def fn_rms(x, g):
    return (_rms(x) * g,)


def fn_resid(x, a, gate):
    return (x + gate * a,)


def fn_softplus(dt, bias):
    return (_softplus(dt + bias),)


def fn_ssd_out(yf, yb, xs, z, d_lane, g):
    y = yf + yb + d_lane * xs
    return (_rms(y * (z * jax.nn.sigmoid(z))) * g,)


def fn_swa_q(q, g, c, s):
    return (_rope_swa(_rms(q) * g, c, s),)


def fn_swa_kv(k, v, g, c, s):
    return (_rope_swa(_rms(k) * g, c, s), v)


def fn_mla_q(qn, qr, gn, gr, c, s):
    return (jnp.concatenate([_rms(qn) * gn, _rope_mla(_rms(qr, MLA_ROPE) * gr, c, s)], axis=-1),)


def fn_mla_kv(kn, v, kr, gn, gr, c, s):
    return (jnp.concatenate([_rms(kn) * gn, _rope_mla(_rms(kr, MLA_ROPE) * gr, c, s)], axis=-1), v)


def fn_merge(g1, g2, g3, p1, p2, p3):
    return (jax.nn.sigmoid(g1) * p1 + jax.nn.sigmoid(g2) * p2 + jax.nn.sigmoid(g3) * p3,)


def fn_swiglu(g, u):
    return (g * jax.nn.sigmoid(g) * u,)


ATTN_TILE = 256
NT_DIMS = (((1,), (1,)), ((), ()))


class AttnCfg:
    def __init__(self, hq, group, dq, dv, scale, window, has_sink, L, T, chunk, kv_block):
        self.hq, self.group, self.dq, self.dv, self.scale = hq, group, dq, dv, scale
        self.window, self.has_sink, self.L, self.T = window, has_sink, L, T
        self.chunk = _pick(L, (chunk, ATTN_TILE))
        self.ctx_chunk = T - L
        self.kv_block = kv_block
        self.q_block = kv_block * group
        assert L % ATTN_TILE == 0 and (T - L) % ATTN_TILE == 0 and L % self.chunk == 0
        assert (hq // group) % kv_block == 0
        if window is not None:
            assert (ATTN_TILE + 2 * window) % self.chunk == 0
            self.window_chunks = min((ATTN_TILE + 2 * window) // self.chunk, L // self.chunk)
            self.align = math.gcd(self.chunk, window)
        else:
            self.align = self.chunk


LOG2E = math.log2(math.e)


def _latent_chunks(cfg, r0):
    c = cfg.chunk
    if cfg.window is None:
        lo, n = 0, cfg.L // c
    else:
        n = cfg.window_chunks
        lo = jnp.clip(r0 - cfg.window, 0, cfg.L - n * c)
    return lo, n


def _visible(cfg, rows_q, rows_k):
    return jnp.logical_or(rows_k >= cfg.L, jnp.abs(rows_k - rows_q) <= cfg.window)


def flash_fwd(name, cfg, q, k, v, sink):
    T, tq, c = cfg.T, ATTN_TILE, cfg.chunk
    hq, g, dq, dv, hb, kb = cfg.hq, cfg.group, cfg.dq, cfg.dv, cfg.q_block, cfg.kv_block
    to_log2 = cfg.scale * LOG2E

    def body(*refs):
        if cfg.has_sink:
            q_ref, k_ref, v_ref, sink_ref, o_ref, lse_ref = refs
        else:
            q_ref, k_ref, v_ref, o_ref, lse_ref = refs
        q0 = pl.program_id(1) * tq
        qs = [q_ref[:, hh * dq:(hh + 1) * dq] for hh in range(hb)]
        lat_lo, lat_n = _latent_chunks(cfg, q0)
        n = jnp.where(q0 >= cfg.L, 0, lat_n)
        rows_q = q0 + lax.broadcasted_iota(jnp.int32, (tq, 1), 0)

        def start(t):
            return pl.multiple_of(lat_lo + jnp.minimum(t, lat_n - 1) * c, cfg.align)

        def logits(ks, size):
            return tuple(lax.dot_general(qs[hh], k_ref[pl.ds(ks, size), (hh // g) * dq:(hh // g + 1) * dq], NT_DIMS,
                                         preferred_element_type=F32) for hh in range(hb))

        def update(state, s_all, ks, size, masked):
            new_state = []
            for hh in range(hb):
                m, l, acc = state[hh]
                s = s_all[hh]
                if masked:
                    rows_k = ks + lax.broadcasted_iota(jnp.int32, (1, size), 1)
                    s = jnp.where(_visible(cfg, rows_q, rows_k), s, -jnp.inf)
                m_new = jnp.maximum(m, jnp.max(s, axis=-1, keepdims=True) * to_log2)
                alpha = jnp.exp2(m - m_new)
                p = jnp.exp2(s * to_log2 - m_new)
                l = alpha * l + jnp.sum(p, axis=-1, keepdims=True)
                kh = hh // g
                acc = alpha * acc + jnp.dot(p.astype(BF16), v_ref[pl.ds(ks, size), kh * dv:(kh + 1) * dv], preferred_element_type=F32)
                new_state.append((m_new, l, acc))
            return tuple(new_state)

        def step(t, carry):
            state, s_all = carry
            s_next = logits(start(t + 1), c)
            return update(state, s_all, start(t), c, cfg.window is not None), s_next

        state = []
        for hh in range(hb):
            if cfg.has_sink:
                m0 = jnp.zeros((tq, 1), F32) + sink_ref[hh] * LOG2E
                l0 = jnp.ones((tq, 1), F32)
            else:
                m0 = jnp.full((tq, 1), -jnp.inf, F32)
                l0 = jnp.zeros((tq, 1), F32)
            state.append((m0, l0, jnp.zeros((tq, dv), F32)))
        state = update(tuple(state), logits(cfg.L, cfg.ctx_chunk), cfg.L, cfg.ctx_chunk, False)
        state, _ = lax.fori_loop(0, n, step, (state, logits(start(0), c)))
        for hh in range(hb):
            m, l, acc = state[hh]
            o_ref[:, hh * dv:(hh + 1) * dv] = acc / l
            lse_ref[hh] = m + jnp.log2(l)

    in_specs = [pl.BlockSpec((tq, hb * dq), lambda h, i: (i, h)),
                pl.BlockSpec((T, kb * dq), lambda h, i: (0, h)),
                pl.BlockSpec((T, kb * dv), lambda h, i: (0, h))]
    ins = [q, k, v]
    if cfg.has_sink:
        in_specs.append(pl.BlockSpec((hb, 1, 1), lambda h, i: (h, 0, 0)))
        ins.append(sink)
    return pl.pallas_call(
        body, name=name, grid=(hq // hb, T // tq), in_specs=in_specs,
        out_specs=[pl.BlockSpec((tq, hb * dv), lambda h, i: (i, h)), pl.BlockSpec((hb, tq, 1), lambda h, i: (h, i, 0))],
        out_shape=[jax.ShapeDtypeStruct((T, hq * dv), F32), jax.ShapeDtypeStruct((hq, T, 1), F32)],
        compiler_params=_cparams(("parallel", "parallel")),
    )(*ins)


def attn_delta(name, cfg, o, do, lse, sink):
    T, tm, hq, dv = cfg.T, ATTN_TILE, cfg.hq, cfg.dv

    def body(*refs):
        if cfg.has_sink:
            o_ref, do_ref, lse_ref, sink_ref, delta_ref, dob_ref, dsink_ref = refs
        else:
            o_ref, do_ref, delta_ref, dob_ref = refs
        dob_ref[...] = do_ref[...].astype(BF16)
        parts = []
        for h in range(hq):
            delta = jnp.sum(do_ref[:, h * dv:(h + 1) * dv] * o_ref[:, h * dv:(h + 1) * dv], axis=-1, keepdims=True)
            delta_ref[h] = delta
            if cfg.has_sink:
                parts.append(-jnp.sum(jnp.exp2(sink_ref[h] * LOG2E - lse_ref[h]) * delta, axis=0, keepdims=True)[None])
        if cfg.has_sink:
            _accumulate(dsink_ref, jnp.concatenate(parts, axis=0), pl.program_id(0) == 0)

    head_tile = pl.BlockSpec((tm, hq * dv), lambda i: (i, 0))
    col = pl.BlockSpec((hq, tm, 1), lambda i: (0, i, 0))
    one = pl.BlockSpec((hq, 1, 1), lambda i: (0, 0, 0))
    in_specs, ins = [head_tile, head_tile], [o, do]
    out_specs = [col, head_tile]
    out_shape = [jax.ShapeDtypeStruct((hq, T, 1), F32), jax.ShapeDtypeStruct((T, hq * dv), BF16)]
    if cfg.has_sink:
        in_specs += [col, one]
        ins += [lse, sink]
        out_specs.append(one)
        out_shape.append(jax.ShapeDtypeStruct((hq, 1, 1), F32))
    return pl.pallas_call(body, name=name, grid=(T // tm,), in_specs=in_specs, out_specs=out_specs, out_shape=out_shape,
                          compiler_params=_cparams(("arbitrary",)))(*ins)


def flash_dq(name, cfg, q, k, v, dob, lse, delta):
    T, tq, c = cfg.T, ATTN_TILE, cfg.chunk
    hq, g, dq, dv, hb, kb = cfg.hq, cfg.group, cfg.dq, cfg.dv, cfg.q_block, cfg.kv_block

    def body(q_ref, k_ref, v_ref, do_ref, lse_ref, delta_ref, dq_ref):
        q0 = pl.program_id(1) * tq
        qs = [q_ref[:, hh * dq:(hh + 1) * dq] for hh in range(hb)]
        dos = [do_ref[:, hh * dv:(hh + 1) * dv] for hh in range(hb)]
        lat_lo, lat_n = _latent_chunks(cfg, q0)
        n = jnp.where(q0 >= cfg.L, 0, lat_n)
        rows_q = q0 + lax.broadcasted_iota(jnp.int32, (tq, 1), 0)
        to_log2 = cfg.scale * LOG2E

        def start(t):
            return pl.multiple_of(lat_lo + jnp.minimum(t, lat_n - 1) * c, cfg.align)

        def keys(ks, size, hh):
            kh = hh // g
            return k_ref[pl.ds(ks, size), kh * dq:(kh + 1) * dq]

        def products(ks, size):
            out = []
            for hh in range(hb):
                kh = hh // g
                out.append((lax.dot_general(qs[hh], keys(ks, size, hh), NT_DIMS, preferred_element_type=F32),
                            lax.dot_general(dos[hh], v_ref[pl.ds(ks, size), kh * dv:(kh + 1) * dv], NT_DIMS,
                                            preferred_element_type=F32)))
            return tuple(out)

        def update(accs, prods, ks, size, masked):
            new_accs = []
            for hh in range(hb):
                s, dp = prods[hh]
                p = jnp.exp2(s * to_log2 - lse_ref[hh])
                if masked:
                    rows_k = ks + lax.broadcasted_iota(jnp.int32, (1, size), 1)
                    p = jnp.where(_visible(cfg, rows_q, rows_k), p, 0.0)
                ds = (p * (dp - delta_ref[hh])).astype(BF16)
                new_accs.append(accs[hh] + jnp.dot(ds, keys(ks, size, hh), preferred_element_type=F32))
            return tuple(new_accs)

        def step(t, carry):
            accs, prods = carry
            nxt = products(start(t + 1), c)
            return update(accs, prods, start(t), c, cfg.window is not None), nxt

        accs = tuple(jnp.zeros((tq, dq), F32) for _ in range(hb))
        accs = update(accs, products(cfg.L, cfg.ctx_chunk), cfg.L, cfg.ctx_chunk, False)
        accs, _ = lax.fori_loop(0, n, step, (accs, products(start(0), c)))
        for hh in range(hb):
            dq_ref[:, hh * dq:(hh + 1) * dq] = accs[hh] * cfg.scale

    col = pl.BlockSpec((hb, tq, 1), lambda h, i: (h, i, 0))
    return pl.pallas_call(
        body, name=name, grid=(hq // hb, T // tq),
        in_specs=[pl.BlockSpec((tq, hb * dq), lambda h, i: (i, h)),
                  pl.BlockSpec((T, kb * dq), lambda h, i: (0, h)),
                  pl.BlockSpec((T, kb * dv), lambda h, i: (0, h)),
                  pl.BlockSpec((tq, hb * dv), lambda h, i: (i, h)), col, col],
        out_specs=pl.BlockSpec((tq, hb * dq), lambda h, i: (i, h)),
        out_shape=jax.ShapeDtypeStruct((T, hq * dq), F32),
        compiler_params=_cparams(("parallel", "parallel")),
    )(q, k, v, dob, lse, delta)


def flash_bwd_fused(name, cfg, q, k, v, dob, lse, delta):
    assert cfg.window is None and cfg.group == 1
    T, L, tq, c, cc = cfg.T, cfg.L, ATTN_TILE, cfg.chunk, cfg.ctx_chunk
    hq, dq, dv = cfg.hq, cfg.dq, cfg.dv
    nq = T // tq
    to_log2 = cfg.scale * LOG2E

    def body(q_ref, k_ref, v_ref, do_ref, lse_ref, delta_ref, dq_ref, dk_ref, dv_ref):
        i = pl.program_id(1)
        q0 = i * tq

        @pl.when(i == 0)
        def _():
            dk_ref[...] = jnp.zeros_like(dk_ref)
            dv_ref[...] = jnp.zeros_like(dv_ref)

        qq, dd = q_ref[...], do_ref[...]
        lse_c, delta_c = lse_ref[0], delta_ref[0]
        n = jnp.where(q0 >= L, 0, L // c)

        def start(t):
            return pl.multiple_of(jnp.minimum(t, L // c - 1) * c, c)

        def products(ks, size):
            return (lax.dot_general(qq, k_ref[pl.ds(ks, size), :], NT_DIMS, preferred_element_type=F32),
                    lax.dot_general(dd, v_ref[pl.ds(ks, size), :], NT_DIMS, preferred_element_type=F32))

        def update(acc, prods, ks, size):
            s, dp = prods
            p = jnp.exp2(s * to_log2 - lse_c)
            ds = (p * (dp - delta_c)).astype(BF16)
            dv_ref[pl.ds(ks, size), :] += lax.dot_general(p.astype(BF16), dd, TN_DIMS, preferred_element_type=F32)
            dk_ref[pl.ds(ks, size), :] += lax.dot_general(ds, qq, TN_DIMS, preferred_element_type=F32)
            return acc + jnp.dot(ds, k_ref[pl.ds(ks, size), :], preferred_element_type=F32)

        def step(t, carry):
            acc, prods = carry
            nxt = products(start(t + 1), c)
            return update(acc, prods, start(t), c), nxt

        acc = update(jnp.zeros((tq, dq), F32), products(L, cc), L, cc)
        acc, _ = lax.fori_loop(0, n, step, (acc, products(start(0), c)))
        dq_ref[...] = acc * cfg.scale

        @pl.when(i == nq - 1)
        def _():
            dk_ref[...] = dk_ref[...] * cfg.scale

    col = pl.BlockSpec((1, tq, 1), lambda h, i: (h, i, 0))
    return pl.pallas_call(
        body, name=name, grid=(hq, nq),
        in_specs=[pl.BlockSpec((tq, dq), lambda h, i: (i, h)),
                  pl.BlockSpec((T, dq), lambda h, i: (0, h)),
                  pl.BlockSpec((T, dv), lambda h, i: (0, h)),
                  pl.BlockSpec((tq, dv), lambda h, i: (i, h)), col, col],
        out_specs=[pl.BlockSpec((tq, dq), lambda h, i: (i, h)),
                   pl.BlockSpec((T, dq), lambda h, i: (0, h)),
                   pl.BlockSpec((T, dv), lambda h, i: (0, h))],
        out_shape=[jax.ShapeDtypeStruct((T, hq * dq), F32), jax.ShapeDtypeStruct((T, hq * dq), F32),
                   jax.ShapeDtypeStruct((T, hq * dv), F32)],
        compiler_params=_cparams(("arbitrary", "arbitrary")),
    )(q, k, v, dob, lse, delta)


def flash_dkv(name, cfg, q, k, v, dob, lse, delta):
    T, L, tk, c, al, cc = cfg.T, cfg.L, ATTN_TILE, cfg.chunk, cfg.align, cfg.ctx_chunk
    hq, g, dq, dv, hb, kb = cfg.hq, cfg.group, cfg.dq, cfg.dv, cfg.q_block, cfg.kv_block
    hk = hq // g
    sub = c // al

    def body(k_ref, v_ref, q_ref, do_ref, lse_lat, delta_lat, lse_ctx, delta_ctx, dk_ref, dv_ref):
        k0 = pl.program_id(1) * tk
        kk = [k_ref[:, kh * dq:(kh + 1) * dq] for kh in range(kb)]
        vv = [v_ref[:, kh * dv:(kh + 1) * dv] for kh in range(kb)]
        is_ctx = k0 >= L
        rows_k = k0 + lax.broadcasted_iota(jnp.int32, (tk, 1), 0)
        to_log2 = cfg.scale * LOG2E
        lat_lo, lat_n = _latent_chunks(cfg, k0)
        if cfg.window is not None:
            lat_lo = jnp.where(is_ctx, 0, lat_lo)
            lat_n = jnp.where(is_ctx, L // c, lat_n)

        def start(t):
            return pl.multiple_of(lat_lo + jnp.minimum(t, lat_n - 1) * c, al)

        def operands(qs, size, hh):
            return q_ref[pl.ds(qs, size), hh * dq:(hh + 1) * dq], do_ref[pl.ds(qs, size), hh * dv:(hh + 1) * dv]

        def products(qs, size):
            out = []
            for hh in range(hb):
                kh = hh // g
                qc, dc = operands(qs, size, hh)
                out.append((lax.dot_general(kk[kh], qc, NT_DIMS, preferred_element_type=F32),
                            lax.dot_general(vv[kh], dc, NT_DIMS, preferred_element_type=F32)))
            return tuple(out)

        def update(accs, prods, qs, size, rows, masked):
            accs = [list(a) for a in accs]
            for hh in range(hb):
                kh = hh // g
                st, dpt = prods[hh]
                qc, dc = operands(qs, size, hh)
                lse_row, delta_row = rows(hh)
                pt = jnp.exp2(st * to_log2 - lse_row)
                if masked:
                    rows_q = qs + lax.broadcasted_iota(jnp.int32, (1, size), 1)
                    pt = jnp.where(_visible(cfg, rows_q, rows_k), pt, 0.0)
                accs[kh][1] = accs[kh][1] + jnp.dot(pt.astype(BF16), dc, preferred_element_type=F32)
                dst = (pt * (dpt - delta_row)).astype(BF16)
                accs[kh][0] = accs[kh][0] + jnp.dot(dst, qc, preferred_element_type=F32)
            return tuple(tuple(a) for a in accs)

        def step(t, carry):
            accs, prods = carry
            nxt = products(start(t + 1), c)
            qs = start(t)
            ci = qs // al

            def rows(hh):
                return (jnp.concatenate([lse_lat[hh, ci + j] for j in range(sub)], axis=1),
                        jnp.concatenate([delta_lat[hh, ci + j] for j in range(sub)], axis=1))

            return update(accs, prods, qs, c, rows, cfg.window is not None), nxt

        def ctx_step(_, accs):
            return update(accs, products(L, cc), L, cc, lambda hh: (lse_ctx[hh, 0], delta_ctx[hh, 0]), False)

        accs = tuple((jnp.zeros((tk, dq), F32), jnp.zeros((tk, dv), F32)) for _ in range(kb))
        accs, _ = lax.fori_loop(0, lat_n, step, (accs, products(start(0), c)))
        accs = lax.fori_loop(0, jnp.where(is_ctx, 1, 0), ctx_step, accs)
        for kh in range(kb):
            dk_ref[:, kh * dq:(kh + 1) * dq] = accs[kh][0] * cfg.scale
            dv_ref[:, kh * dv:(kh + 1) * dv] = accs[kh][1]

    def lanes(a):
        return a[:, :L].reshape(hq, L // al, 1, al), a[:, L:].reshape(hq, 1, 1, cc)

    lse_lat, lse_ctx = lanes(lse)
    delta_lat, delta_ctx = lanes(delta)
    lat_spec = pl.BlockSpec((hb, L // al, 1, al), lambda h, j: (h, 0, 0, 0))
    ctx_spec = pl.BlockSpec((hb, 1, 1, cc), lambda h, j: (h, 0, 0, 0))
    return pl.pallas_call(
        body, name=name, grid=(hk // kb, T // tk),
        in_specs=[pl.BlockSpec((tk, kb * dq), lambda h, j: (j, h)),
                  pl.BlockSpec((tk, kb * dv), lambda h, j: (j, h)),
                  pl.BlockSpec((T, hb * dq), lambda h, j: (0, h)),
                  pl.BlockSpec((T, hb * dv), lambda h, j: (0, h)),
                  lat_spec, lat_spec, ctx_spec, ctx_spec],
        out_specs=[pl.BlockSpec((tk, kb * dq), lambda h, j: (j, h)), pl.BlockSpec((tk, kb * dv), lambda h, j: (j, h))],
        out_shape=[jax.ShapeDtypeStruct((T, hk * dq), F32), jax.ShapeDtypeStruct((T, hk * dv), F32)],
        compiler_params=_cparams(("parallel", "parallel")),
    )(k, v, q, dob, lse_lat, delta_lat, lse_ctx, delta_ctx)


HALO = 8


def _conv_specs(tm, C, T):
    nb = tm // HALO
    last = T // HALO - 1
    return [pl.BlockSpec((HALO, C), lambda i: (jnp.maximum(i * nb - 1, 0), 0)),
            pl.BlockSpec((tm, C), lambda i: (i, 0)),
            pl.BlockSpec((HALO, C), lambda i: (jnp.minimum((i + 1) * nb, last), 0))]


def _extended(prev_ref, cur_ref, next_ref, i, tm, L, T):
    r0 = i * tm
    keep_prev = jnp.logical_and(r0 != 0, r0 != L).astype(F32)
    keep_next = jnp.logical_and(r0 + tm != L, r0 + tm != T).astype(F32)
    return jnp.concatenate([prev_ref[...] * keep_prev, cur_ref[...], next_ref[...] * keep_next], axis=0)


def _shift_rows(xe, d):
    n = xe.shape[0]
    return xe if d == 0 else pltpu.roll(xe, (-d) % n, 0)


def _conv_pre(xe, w_ref, b_ref):
    acc = b_ref[...] + w_ref[SSM_CONV // 2:SSM_CONV // 2 + 1, :] * xe
    for k in range(SSM_CONV):
        if k != SSM_CONV // 2:
            acc = acc + w_ref[k:k + 1, :] * _shift_rows(xe, k - SSM_CONV // 2)
    return acc


def conv_fwd(name, x, w, b, L, tm=ROW_TILE):
    T, C = x.shape

    def body(xp, xc, xn, w_ref, b_ref, o_ref):
        xe = _extended(xp, xc, xn, pl.program_id(0), tm, L, T)
        pre = _conv_pre(xe, w_ref, b_ref)[HALO:HALO + tm]
        o_ref[...] = pre * jax.nn.sigmoid(pre)

    full = lambda a: pl.BlockSpec(a.shape, lambda i: (0, 0))
    return pl.pallas_call(body, name=name, grid=(T // tm,), in_specs=_conv_specs(tm, C, T) + [full(w), full(b)],
                          out_specs=pl.BlockSpec((tm, C), lambda i: (i, 0)), out_shape=jax.ShapeDtypeStruct((T, C), F32),
                          compiler_params=_cparams(("parallel",)))(x, x, x, w, b)


def conv_bwd(name, x, w, b, gu, L, tm=ROW_TILE):
    T, C = x.shape

    def body(xp, xc, xn, gp, gc, gn, w_ref, b_ref, dx_ref, dw_ref, db_ref):
        i = pl.program_id(0)
        xe = _extended(xp, xc, xn, i, tm, L, T)
        ge = _extended(gp, gc, gn, i, tm, L, T)
        pre = _conv_pre(xe, w_ref, b_ref)
        sg = jax.nn.sigmoid(pre)
        gpre = ge * (sg * (1.0 + pre * (1.0 - sg)))
        half = SSM_CONV // 2
        dx = jnp.zeros((tm, C), F32)
        rows = []
        for k in range(SSM_CONV):
            dx = dx + w_ref[k:k + 1, :] * _shift_rows(gpre, half - k)[HALO:HALO + tm]
            rows.append(jnp.sum(gpre[HALO:HALO + tm] * _shift_rows(xe, k - half)[HALO:HALO + tm], axis=0, keepdims=True))
        dx_ref[...] = dx
        rows += [jnp.zeros((1, C), F32)] * (8 - SSM_CONV)
        _accumulate(dw_ref, jnp.concatenate(rows, axis=0), i == 0)
        _accumulate(db_ref, jnp.sum(gpre[HALO:HALO + tm], axis=0, keepdims=True), i == 0)

    full = lambda a: pl.BlockSpec(a.shape, lambda i: (0, 0))
    return pl.pallas_call(
        body, name=name, grid=(T // tm,), in_specs=_conv_specs(tm, C, T) * 2 + [full(w), full(b)],
        out_specs=[pl.BlockSpec((tm, C), lambda i: (i, 0)), pl.BlockSpec((8, C), lambda i: (0, 0)), pl.BlockSpec((1, C), lambda i: (0, 0))],
        out_shape=[jax.ShapeDtypeStruct((T, C), F32), jax.ShapeDtypeStruct((8, C), F32), jax.ShapeDtypeStruct((1, C), F32)],
        compiler_params=_cparams(("arbitrary",)))(x, x, x, gu, gu, gu, w, b)


SSM_PAIRS = SSM_HEADS // 2
TN_DIMS = (((0,), (0,)), ((), ()))
HIGHEST = lax.Precision.HIGHEST


def _ssd_chunk(direction, xps, bs, cs, dt_col, dt_row, alog_row, alog_col, hps):
    Q = SSM_CHUNK
    da_col = dt_col * (-jnp.exp(alog_row))
    da_row = dt_row * (-jnp.exp(alog_col))
    ii = lax.broadcasted_iota(jnp.int32, (Q, Q), 0)
    jj = lax.broadcasted_iota(jnp.int32, (Q, Q), 1)
    tri = (ii >= jj) if direction == 0 else (ii <= jj)
    trif = tri.astype(F32)
    acs_col = jnp.dot(trif, da_col, precision=HIGHEST, preferred_element_type=F32)
    acs_row = lax.dot_general(da_row, trif, NT_DIMS, precision=HIGHEST, preferred_element_type=F32)
    tot_col = jnp.sum(da_col, axis=0, keepdims=True)
    lane16 = lax.broadcasted_iota(jnp.int32, (1, SSM_HEADS), 1)
    sub16 = lax.broadcasted_iota(jnp.int32, (SSM_HEADS, 1), 0)
    low = lax.broadcasted_iota(jnp.int32, (1, 2 * SSM_HEAD_DIM), 1) < SSM_HEAD_DIM

    def col(v, h):
        return jnp.sum(v * (lane16 == h).astype(F32), axis=1, keepdims=True)

    def row(v, h):
        return jnp.sum(v * (sub16 == h).astype(F32), axis=0, keepdims=True)

    ys, hos = [], []
    pairs_per_group = SSM_PAIRS // SSM_GROUPS
    for g in range(SSM_GROUPS):
        bb, cb16 = bs[g].astype(BF16), cs[g].astype(BF16)
        cb = lax.dot_general(cb16, bb, NT_DIMS, preferred_element_type=F32)
        for pp in range(pairs_per_group):
            p = g * pairs_per_group + pp
            h0, h1 = 2 * p, 2 * p + 1
            ac0, ac1 = col(acs_col, h0), col(acs_col, h1)
            seg0 = jnp.exp(jnp.where(tri, ac0 - row(acs_row, h0), -jnp.inf))
            seg1 = jnp.exp(jnp.where(tri, ac1 - row(acs_row, h1), -jnp.inf))
            dt_l = jnp.where(low, col(dt_col, h0), col(dt_col, h1))
            ac_l = jnp.where(low, ac0, ac1)
            tot_l = jnp.where(low, col(tot_col, h0), col(tot_col, h1))
            xdt = xps[p] * dt_l
            y = (jnp.dot((cb * seg0).astype(BF16), jnp.where(low, xdt, 0.0).astype(BF16), preferred_element_type=F32)
                 + jnp.dot((cb * seg1).astype(BF16), jnp.where(low, 0.0, xdt).astype(BF16), preferred_element_type=F32))
            y = y + jnp.dot(cb16, hps[p].astype(BF16), preferred_element_type=F32) * jnp.exp(ac_l)
            st = lax.dot_general(bb, (xdt * jnp.exp(tot_l - ac_l)).astype(BF16), TN_DIMS, preferred_element_type=F32)
            ys.append(y)
            hos.append(hps[p] * jnp.exp(tot_l) + st)
    return tuple(ys), tuple(hos)


def _ssd_chunk_of(direction, step, ncl, ncc):
    if direction == 0:
        return jnp.where(step < ncc, ncl + step, step - ncc)
    return jnp.where(step < ncc, ncl + ncc - 1 - step, ncl - 1 - (step - ncc))


def _ssd_load(u_ref):
    Q = SSM_CHUNK
    xps = tuple(u_ref[:, LANES * p:LANES * (p + 1)] for p in range(SSM_PAIRS))
    bs = tuple(u_ref[:, SSM_INNER + SSM_STATE * g:SSM_INNER + SSM_STATE * (g + 1)] for g in range(SSM_GROUPS))
    c0 = SSM_INNER + SSM_GROUPS * SSM_STATE
    cs = tuple(u_ref[:, c0 + SSM_STATE * g:c0 + SSM_STATE * (g + 1)] for g in range(SSM_GROUPS))
    return xps, bs, cs


def ssd_fwd(name, direction, u, dt, dt_t, alog_row, alog_col, L):
    T = u.shape[0]
    Q, N = SSM_CHUNK, SSM_STATE
    ncl, ncc = L // Q, (T - L) // Q
    nc = ncl + ncc
    cm = lambda s: _ssd_chunk_of(direction, s, ncl, ncc)

    def body(u_ref, dt_ref, dtt_ref, ar_ref, ac_ref, y_ref, hin_ref, state):
        @pl.when(pl.program_id(0) == 0)
        def _():
            state[...] = jnp.zeros_like(state)

        xps, bs, cs = _ssd_load(u_ref)
        hps = tuple(state[p] for p in range(SSM_PAIRS))
        for p in range(SSM_PAIRS):
            hin_ref[0, p] = hps[p]
        ys, hos = _ssd_chunk(direction, xps, bs, cs, dt_ref[...], dtt_ref[...], ar_ref[...], ac_ref[...], hps)
        for p in range(SSM_PAIRS):
            y_ref[:, LANES * p:LANES * (p + 1)] = ys[p]
            state[p] = hos[p]

    return pl.pallas_call(
        body, name=name, grid=(nc,),
        in_specs=[pl.BlockSpec((Q, SSM_CONV_DIM), lambda s: (cm(s), 0)),
                  pl.BlockSpec((Q, SSM_HEADS), lambda s: (cm(s), 0)),
                  pl.BlockSpec((SSM_HEADS, Q), lambda s: (0, cm(s))),
                  pl.BlockSpec((1, SSM_HEADS), lambda s: (0, 0)),
                  pl.BlockSpec((SSM_HEADS, 1), lambda s: (0, 0))],
        out_specs=[pl.BlockSpec((Q, SSM_INNER), lambda s: (cm(s), 0)),
                   pl.BlockSpec((1, SSM_PAIRS, N, LANES), lambda s: (cm(s), 0, 0, 0))],
        out_shape=[jax.ShapeDtypeStruct((T, SSM_INNER), F32), jax.ShapeDtypeStruct((nc, SSM_PAIRS, N, LANES), F32)],
        scratch_shapes=[pltpu.VMEM((SSM_PAIRS, N, LANES), F32)],
        compiler_params=_cparams(("arbitrary",)),
    )(u, dt, dt_t, alog_row, alog_col)


def ssd_bwd(name, direction, u, dt, dt_t, alog_row, alog_col, hin, dy, L, add_x=None, add_u=None):
    T = u.shape[0]
    Q, N = SSM_CHUNK, SSM_STATE
    ncl, ncc = L // Q, (T - L) // Q
    nc = ncl + ncc
    cm = lambda s: _ssd_chunk_of(direction, nc - 1 - s, ncl, ncc)
    n_add = (add_x is not None) + (add_u is not None)

    def body(*refs):
        u_ref, dt_ref, dtt_ref, ar_ref, ac_ref, hin_ref, dy_ref = refs[:7]
        add_refs = refs[7:7 + n_add]
        du_ref, ddt_ref, ddtt_ref, dar_ref, dac_ref, dstate = refs[7 + n_add:]
        first = pl.program_id(0) == 0

        @pl.when(first)
        def _():
            dstate[...] = jnp.zeros_like(dstate)

        xps, bs, cs = _ssd_load(u_ref)
        hps = tuple(hin_ref[0, p] for p in range(SSM_PAIRS))
        _, vjp = jax.vjp(functools.partial(_ssd_chunk, direction), xps, bs, cs, dt_ref[...], dtt_ref[...], ar_ref[...],
                         ac_ref[...], hps)
        dys = tuple(dy_ref[:, LANES * p:LANES * (p + 1)] for p in range(SSM_PAIRS))
        dhs = tuple(dstate[p] for p in range(SSM_PAIRS))
        gx, gb, gc, gdt, gdtt, gar, gac, ghp = vjp((dys, dhs))
        parts = list(gx) + list(gb) + list(gc)
        du = jnp.concatenate(parts, axis=1)
        k = 0
        if add_x is not None:
            du = du + jnp.concatenate([add_refs[k][...], jnp.zeros((Q, SSM_CONV_DIM - SSM_INNER), F32)], axis=1)
            k += 1
        if add_u is not None:
            du = du + add_refs[k][...]
        du_ref[...] = du
        ddt_ref[...] = gdt
        ddtt_ref[...] = gdtt
        _accumulate(dar_ref, gar, first)
        _accumulate(dac_ref, gac, first)
        for p in range(SSM_PAIRS):
            dstate[p] = ghp[p]

    in_specs = [pl.BlockSpec((Q, SSM_CONV_DIM), lambda s: (cm(s), 0)),
                pl.BlockSpec((Q, SSM_HEADS), lambda s: (cm(s), 0)),
                pl.BlockSpec((SSM_HEADS, Q), lambda s: (0, cm(s))),
                pl.BlockSpec((1, SSM_HEADS), lambda s: (0, 0)),
                pl.BlockSpec((SSM_HEADS, 1), lambda s: (0, 0)),
                pl.BlockSpec((1, SSM_PAIRS, N, LANES), lambda s: (cm(s), 0, 0, 0)),
                pl.BlockSpec((Q, SSM_INNER), lambda s: (cm(s), 0))]
    ins = [u, dt, dt_t, alog_row, alog_col, hin, dy]
    if add_x is not None:
        in_specs.append(pl.BlockSpec((Q, SSM_INNER), lambda s: (cm(s), 0)))
        ins.append(add_x)
    if add_u is not None:
        in_specs.append(pl.BlockSpec((Q, SSM_CONV_DIM), lambda s: (cm(s), 0)))
        ins.append(add_u)
    return pl.pallas_call(
        body, name=name, grid=(nc,), in_specs=in_specs,
        out_specs=[pl.BlockSpec((Q, SSM_CONV_DIM), lambda s: (cm(s), 0)),
                   pl.BlockSpec((Q, SSM_HEADS), lambda s: (cm(s), 0)),
                   pl.BlockSpec((SSM_HEADS, Q), lambda s: (0, cm(s))),
                   pl.BlockSpec((1, SSM_HEADS), lambda s: (0, 0)),
                   pl.BlockSpec((SSM_HEADS, 1), lambda s: (0, 0))],
        out_shape=[jax.ShapeDtypeStruct((T, SSM_CONV_DIM), F32), jax.ShapeDtypeStruct((T, SSM_HEADS), F32),
                   jax.ShapeDtypeStruct((SSM_HEADS, T), F32), jax.ShapeDtypeStruct((1, SSM_HEADS), F32),
                   jax.ShapeDtypeStruct((SSM_HEADS, 1), F32)],
        scratch_shapes=[pltpu.VMEM((SSM_PAIRS, N, LANES), F32)],
        compiler_params=_cparams(("arbitrary",)),
    )(*ins)


PEER_MASKS = (1, 2, 4, 3, 5, 6, 7)
N_PEERS = len(PEER_MASKS)
MESH_IDS = pl.DeviceIdType.MESH


def _my_index():
    return lax.axis_index("x") * 4 + lax.axis_index("y") * 2 + lax.axis_index("c")


def _coords(idx):
    return (idx // 4, (idx // 2) % 2, idx % 2)


def all_gather_hbm(name, arrays):
    n = len(arrays)
    chip_masks = (4, 2, 6)

    def body(*refs):
        ins, outs = refs[:n], refs[n:2 * n]
        send_sems, recv_sems, local_sems = refs[2 * n:]
        me = _my_index()
        sibling = me ^ 1

        def copy(a, k, block, to, src=None):
            return pltpu.make_async_remote_copy(
                src_ref=outs[a].at[block] if src is None else src, dst_ref=outs[a].at[block],
                send_sem=send_sems.at[a * N_PEERS + k], recv_sem=recv_sems.at[a * N_PEERS + k],
                device_id=_coords(to), device_id_type=MESH_IDS)

        started, own = [], []
        for a in range(n):
            local = pltpu.make_async_copy(ins[a], outs[a].at[me], local_sems.at[a])
            local.start()
            own.append(local)
            first = [copy(a, 0, me, sibling, src=ins[a])] + [copy(a, 1 + j, me, me ^ m, src=ins[a]) for j, m in enumerate(chip_masks)]
            for cp in first:
                cp.start()
            started += first
        for a in range(n):
            for j, m in enumerate(chip_masks):
                copy(a, 1 + j, me ^ m, me).wait_recv()
                fwd = copy(a, 4 + j, me ^ m, sibling)
                fwd.start()
                started.append(fwd)
        for a in range(n):
            copy(a, 0, sibling, me).wait_recv()
            for j, m in enumerate(chip_masks):
                copy(a, 4 + j, sibling ^ m, me).wait_recv()
        for cp in started:
            cp.wait_send()
        for cp in own:
            cp.wait()

    any_spec = pl.BlockSpec(memory_space=pl.ANY)
    return pl.pallas_call(
        body, name=name, in_specs=[any_spec] * n, out_specs=[any_spec] * n,
        out_shape=[jax.ShapeDtypeStruct((N_DEV,) + a.shape, a.dtype) for a in arrays],
        scratch_shapes=[pltpu.SemaphoreType.DMA((n * N_PEERS,)), pltpu.SemaphoreType.DMA((n * N_PEERS,)),
                        pltpu.SemaphoreType.DMA((n,))],
    )(*arrays)


def exchange_hbm(name, arrays):
    n = len(arrays)

    def body(*refs):
        ins, outs = refs[:n], refs[n:2 * n]
        send_sems, recv_sems, local_sems = refs[2 * n:]
        me = _my_index()
        copies = []
        for a in range(n):
            local = pltpu.make_async_copy(ins[a].at[me], outs[a].at[me], local_sems.at[a])
            local.start()
            copies.append(local)
            for k, mask in enumerate(PEER_MASKS):
                peer = me ^ mask
                cp = pltpu.make_async_remote_copy(src_ref=ins[a].at[peer], dst_ref=outs[a].at[me],
                                                  send_sem=send_sems.at[a * N_PEERS + k], recv_sem=recv_sems.at[a * N_PEERS + k],
                                                  device_id=_coords(peer), device_id_type=MESH_IDS)
                cp.start()
                copies.append(cp)
        for cp in copies:
            cp.wait()

    any_spec = pl.BlockSpec(memory_space=pl.ANY)
    return pl.pallas_call(
        body, name=name, in_specs=[any_spec] * n, out_specs=[any_spec] * n,
        out_shape=[jax.ShapeDtypeStruct(a.shape, a.dtype) for a in arrays],
        scratch_shapes=[pltpu.SemaphoreType.DMA((n * N_PEERS,)), pltpu.SemaphoreType.DMA((n * N_PEERS,)),
                        pltpu.SemaphoreType.DMA((n,))],
    )(*arrays)


def all_gather_vmem(name, v):
    def body(v_ref, out_ref, send_sems, recv_sems):
        me = _my_index()
        out_ref[me] = v_ref[...]
        copies = []
        for k, mask in enumerate(PEER_MASKS):
            cp = pltpu.make_async_remote_copy(src_ref=v_ref, dst_ref=out_ref.at[me], send_sem=send_sems.at[k],
                                              recv_sem=recv_sems.at[k], device_id=_coords(me ^ mask), device_id_type=MESH_IDS)
            cp.start()
            copies.append(cp)
        for cp in copies:
            cp.wait()

    vm = pl.BlockSpec(memory_space=pltpu.VMEM)
    return pl.pallas_call(
        body, name=name, in_specs=[vm], out_specs=vm, out_shape=jax.ShapeDtypeStruct((N_DEV,) + v.shape, v.dtype),
        scratch_shapes=[pltpu.SemaphoreType.DMA((N_PEERS,)), pltpu.SemaphoreType.DMA((N_PEERS,))],
    )(v)


def _row_tile(rows, cols, bufs):
    budget = 24 * 1024 * 1024 // (bufs * 2 * 4 * max(cols, LANES))
    if rows <= budget:
        return rows
    for t in range(budget - budget % 16, 15, -16):
        if rows % t == 0:
            return t
    return rows


def sum_parts(name, parts):
    P, R, C = parts.shape
    tr = _row_tile(R, C, P + 1)

    def body(p_ref, o_ref):
        acc = p_ref[0].astype(F32)
        for s in range(1, P):
            acc = acc + p_ref[s].astype(F32)
        o_ref[...] = acc

    return pl.pallas_call(body, name=name, grid=(R // tr,), in_specs=[pl.BlockSpec((P, tr, C), lambda i: (0, i, 0))],
                          out_specs=pl.BlockSpec((tr, C), lambda i: (i, 0)), out_shape=jax.ShapeDtypeStruct((R, C), F32),
                          compiler_params=_cparams(("parallel",)))(parts)


def cast_bf16(name, x):
    R, C = x.shape
    tr = _row_tile(R, C, 2)

    def body(x_ref, o_ref):
        o_ref[...] = x_ref[...].astype(BF16)

    spec = pl.BlockSpec((tr, C), lambda i: (i, 0))
    return pl.pallas_call(body, name=name, grid=(R // tr,), in_specs=[spec], out_specs=spec,
                          out_shape=jax.ShapeDtypeStruct((R, C), BF16), compiler_params=_cparams(("parallel",)))(x)


def adamw(name, w, g, m, v):
    R, C = w.shape
    tr = _row_tile(R, C, 7)

    def body(w_ref, g_ref, m_ref, v_ref, d_ref, nm_ref, nv_ref):
        g = g_ref[...]
        nm = ADAM_B1 * m_ref[...] + (1.0 - ADAM_B1) * g
        nv = ADAM_B2 * v_ref[...] + (1.0 - ADAM_B2) * (g * g)
        m_hat = nm / (1.0 - ADAM_B1 ** ADAM_STEP)
        v_hat = nv / (1.0 - ADAM_B2 ** ADAM_STEP)
        d_ref[...] = -ADAM_LR * (m_hat / (jnp.sqrt(v_hat) + ADAM_EPS) + ADAM_WD * w_ref[...])
        nm_ref[...] = nm
        nv_ref[...] = nv

    spec = pl.BlockSpec((tr, C), lambda i: (i, 0))
    return pl.pallas_call(body, name=name, grid=(R // tr,), in_specs=[spec] * 4, out_specs=[spec] * 3,
                          out_shape=[jax.ShapeDtypeStruct((R, C), F32)] * 3, compiler_params=_cparams(("parallel",)))(w, g, m, v)


def loss_and_grad(name, x, target, L, tm=ROW_TILE):
    T, D = x.shape
    nlt = L // tm

    def body(x_ref, t_ref, loss_ref, dx_ref):
        i = pl.program_id(0)
        err = jnp.where(i < nlt, x_ref[...] - t_ref[...], 0.0)
        dx_ref[...] = err * (1.0 / D)
        part = 0.5 * jnp.sum(jnp.sum(err * err, axis=1, keepdims=True), axis=0, keepdims=True) * (1.0 / D)
        _accumulate(loss_ref, part, i == 0)

    return pl.pallas_call(
        body, name=name, grid=(T // tm,),
        in_specs=[pl.BlockSpec((tm, D), lambda i: (i, 0)), pl.BlockSpec((tm, D), lambda i: (jnp.minimum(i, nlt - 1), 0))],
        out_specs=[pl.BlockSpec((1, 1), lambda i: (0, 0)), pl.BlockSpec((tm, D), lambda i: (i, 0))],
        out_shape=[jax.ShapeDtypeStruct((1, 1), F32), jax.ShapeDtypeStruct((T, D), F32)],
        compiler_params=_cparams(("arbitrary",)))(x, target)


def small_fwd(name, fn, arrays, out_shapes):
    def body(*refs):
        res = fn(*[r[...] for r in refs[:len(arrays)]])
        for o_ref, r in zip(refs[len(arrays):], res):
            o_ref[...] = r

    return pl.pallas_call(body, name=name, out_shape=[jax.ShapeDtypeStruct(s, F32) for s in out_shapes])(*arrays)


def small_bwd(name, fn, arrays, cts):
    n = len(arrays)

    def body(*refs):
        _, vjp = jax.vjp(lambda *a: tuple(fn(*a)), *[r[...] for r in refs[:n]])
        grads = vjp(tuple(r[...] for r in refs[n:n + len(cts)]))
        for o_ref, g in zip(refs[n + len(cts):], grads):
            o_ref[...] = g

    return pl.pallas_call(body, name=name, out_shape=[jax.ShapeDtypeStruct(a.shape, F32) for a in arrays])(*arrays, *cts)


def fn_silu(x):
    return (x * jax.nn.sigmoid(x),)


FWD_NAMES = ["x", "c", "ctx", "c_ctx", "w_mod", "b_mod", "norm1_g", "norm2_g", "w_in", "ssm_conv_w", "ssm_conv_b",
             "ssm_dt_bias", "ssm_a_log", "ssm_d", "ssm_norm_g", "swa_q_norm_g", "swa_k_norm_g", "swa_sink", "mla_q_lat_g",
             "mla_kv_lat_g", "w_mla_uq", "w_mla_ukv", "mla_q_norm_g", "mla_k_norm_g", "w_p_ssm", "w_p_swa", "w_p_mla",
             "w_out", "w_ffn_in", "w_ffn_out"]
WEIGHT_NAMES = FWD_NAMES[3:]
GATHERED = ["w_in", "w_mla_uq", "w_mla_ukv", "w_p_ssm", "w_p_swa", "w_p_mla", "w_out", "w_ffn_in", "w_ffn_out"]
COLUMN_SHARDED = ("w_in", "w_mla_uq", "w_mla_ukv", "w_ffn_in")
REPLICATED = ["c_ctx", "b_mod", "norm1_g", "norm2_g", "ssm_conv_b", "ssm_dt_bias", "ssm_a_log", "ssm_d", "ssm_norm_g",
              "swa_q_norm_g", "swa_k_norm_g", "swa_sink", "mla_q_lat_g", "mla_kv_lat_g", "mla_q_norm_g", "mla_k_norm_g"]
IN_SEGS = [("xbc", SSM_CONV_DIM), ("dt", 2 * SSM_HEADS), ("ks", SWA_KV_HEADS * SWA_HEAD_DIM), ("vs", SWA_KV_HEADS * SWA_HEAD_DIM),
           ("ckv", MLA_KV_RANK), ("kr", MLA_ROPE), ("z", SSM_INNER), ("qs", SWA_Q_HEADS * SWA_HEAD_DIM), ("cq", MLA_Q_RANK),
           ("g1", None), ("g2", None), ("g3", None)]


def _pack(vectors, multiple):
    flat = jnp.concatenate([v.reshape(-1) for v in vectors])
    pad = (-flat.shape[0]) % multiple
    return jnp.pad(flat, (0, pad)).reshape(-1, LANES)


def _unpack(packed, shapes):
    flat, out, off = packed.reshape(-1), [], 0
    for s in shapes:
        n = int(np.prod(s))
        out.append(flat[off:off + n].reshape(s))
        off += n
    return out


def _rope_tables(L, T, rot_dim):
    nf = rot_dim // 4
    inv = jnp.power(ROPE_BASE, -jnp.arange(nf, dtype=F32) / nf)
    r, col = jnp.meshgrid(jnp.arange(L // GRID_W, dtype=F32), jnp.arange(GRID_W, dtype=F32), indexing="ij")
    ang = jnp.stack([r.reshape(-1)[:, None] * inv, col.reshape(-1)[:, None] * inv], axis=1)
    cos, sin = jnp.cos(ang), jnp.sin(ang)
    c = jnp.concatenate([cos[:, 0], cos[:, 0], cos[:, 1], cos[:, 1]], axis=1)
    s = jnp.concatenate([-sin[:, 0], sin[:, 0], -sin[:, 1], sin[:, 1]], axis=1)
    c = jnp.pad(c, ((0, T - L), (0, LANES - rot_dim)), constant_values=1.0)
    s = jnp.pad(s, ((0, T - L), (0, LANES - rot_dim)))
    return c, s


def _pad_rows(a, rows):
    return jnp.pad(a, ((0, rows - a.shape[0]), (0, 0)))


def _row(w, per_head=0, off=0, diff=True):
    return ("row", w, per_head, off, diff)


PAR, PAR_ND = ("par", True), ("par", False)
GRP = ("grp", True)


def kernel(*args):
    n_fwd, n_w = len(FWD_NAMES), len(WEIGHT_NAMES)
    inp = dict(zip(FWD_NAMES, args[:n_fwd]))
    loss_target = args[n_fwd]
    mom_m = dict(zip(WEIGHT_NAMES, args[n_fwd + 1:n_fwd + 1 + n_w]))
    mom_v = dict(zip(WEIGHT_NAMES, args[n_fwd + 1 + n_w:]))

    x, ctx = inp["x"][0], inp["ctx"][0]
    L, D = x.shape
    n_ctx = ctx.shape[0]
    T = L + n_ctx
    depth = inp["w_in"].shape[0]
    me = _my_index()
    in_widths = [w if w is not None else D for _, w in IN_SEGS]
    in_offs = np.concatenate([[0], np.cumsum(in_widths)]).tolist()
    ffn_h = inp["w_ffn_out"].shape[1] * N_DEV
    cfg_swa = AttnCfg(SWA_Q_HEADS, SWA_Q_HEADS // SWA_KV_HEADS, SWA_HEAD_DIM, SWA_HEAD_DIM, SWA_HEAD_DIM ** -0.5, SWA_WINDOW,
                      True, L, T, 256, 1)
    cfg_mla = AttnCfg(MLA_HEADS, 1, MLA_QK_PAD, MLA_V, MLA_QK ** -0.5, None, False, L, T, 1024, 2)
    cfg_mla_bwd = AttnCfg(MLA_HEADS, 1, MLA_QK_PAD, MLA_V, MLA_QK ** -0.5, None, False, L, T, 2048, 1)

    def rf(name, fn, descs, arrays, outs, heads=1):
        return rowop_fwd(name, fn, descs, arrays, outs, T, L, heads=heads)

    def rb(name, fn, descs, arrays, outs, cts, heads=1, add=None):
        return rowop_bwd(name, fn, descs, arrays, outs, cts, T, L, heads=heads, add=add)

    local = []
    for n in GATHERED:
        w = inp[n]
        local.append((jnp.swapaxes(w, 1, 2) if n in COLUMN_SHARDED else w).astype(BF16))
    gathered = dict(zip(GATHERED, all_gather_hbm("gather_weights", local)))

    def full(n, l):
        g = gathered[n][:, l]
        return g.reshape(g.shape[0] * g.shape[1], g.shape[2])

    def layer_weights(l):
        wt = {}
        w_in_t = full("w_in", l)
        for (sn, _), o, w in zip(IN_SEGS, in_offs, in_widths):
            seg = w_in_t[o:o + w]
            wt[sn] = _pad_rows(seg, LANES) if sn == "kr" else seg
        uq = full("w_mla_uq", l).reshape(MLA_HEADS, MLA_QK, MLA_Q_RANK)
        wt["uqn"] = uq[:, :MLA_NOPE].reshape(MLA_HEADS * MLA_NOPE, MLA_Q_RANK)
        wt["uqr"] = jnp.pad(uq[:, MLA_NOPE:], ((0, 0), (0, LANES - MLA_ROPE), (0, 0))).reshape(MLA_HEADS * LANES, MLA_Q_RANK)
        ukv = full("w_mla_ukv", l).reshape(MLA_HEADS, MLA_NOPE + MLA_V, MLA_KV_RANK)
        wt["uk"] = ukv[:, :MLA_NOPE].reshape(MLA_HEADS * MLA_NOPE, MLA_KV_RANK)
        wt["uv"] = ukv[:, MLA_NOPE:].reshape(MLA_HEADS * MLA_V, MLA_KV_RANK)
        for n in ("w_p_ssm", "w_p_swa", "w_p_mla", "w_out", "w_ffn_out"):
            wt[n] = full(n, l)
        ffn_in_t = full("w_ffn_in", l)
        wt["fg"], wt["fu"] = ffn_in_t[:ffn_h], ffn_in_t[ffn_h:]
        return wt

    def layer_params(l):
        p = {}
        for n in ("norm1_g", "norm2_g", "ssm_conv_b", "ssm_norm_g", "swa_q_norm_g", "swa_k_norm_g", "mla_q_lat_g", "mla_kv_lat_g"):
            p[n] = inp[n][l][None]
        p["dt_bias"] = inp["ssm_dt_bias"][l].reshape(1, 2 * SSM_HEADS)
        p["alog_row"] = [inp["ssm_a_log"][l][d][None] for d in range(2)]
        p["alog_col"] = [inp["ssm_a_log"][l][d][:, None] for d in range(2)]
        p["d_lane"] = jnp.repeat(inp["ssm_d"][l], SSM_HEAD_DIM)[None]
        p["sink"] = inp["swa_sink"][l].reshape(SWA_Q_HEADS, 1, 1)
        for n, key in (("mla_q_norm_g", "gq"), ("mla_k_norm_g", "gk")):
            g = inp[n][l]
            p[key + "n"] = g[:MLA_NOPE][None]
            p[key + "r"] = jnp.pad(g[MLA_NOPE:], (0, LANES - MLA_ROPE))[None]
        return p

    conv_local = _pack([inp["ssm_conv_w"]], 8 * LANES)
    conv_all = all_gather_vmem("gather_conv_w", conv_local)
    cw = inp["ssm_conv_w"].shape
    conv_full = conv_all.reshape(N_DEV, -1)[:, :cw[0] * cw[1] * cw[2]].reshape(N_DEV, cw[0], cw[1], cw[2])
    conv_full = jnp.moveaxis(conv_full, 0, 2).reshape(cw[0], cw[1], N_DEV * cw[2])
    conv_w8 = jnp.pad(conv_full, ((0, 0), (0, 8 - cw[1]), (0, 0)))

    silu_c, silu_cc = small_fwd("silu_c", lambda a, b: fn_silu(a) + fn_silu(b), [inp["c"], inp["c_ctx"][None]], [(1, D), (1, D)])
    silu_all = all_gather_vmem("gather_silu_c", silu_c.reshape(D // LANES, LANES)).reshape(N_DEV, D)
    S_rows = 2 * N_DEV
    S_mat = jnp.concatenate([silu_all, silu_cc, jnp.zeros((S_rows - N_DEV - 1, D), F32)], axis=0)
    mod_cols = inp["w_mod"].shape[2]
    mods_local = []
    for l in range(depth):
        bias = lax.dynamic_slice(inp["b_mod"][l], (me * mod_cols,), (mod_cols,))
        mods_local.append(matmul(S_mat, inp["w_mod"][l], "nn", f"mod{l}", add=jnp.broadcast_to(bias[None], (S_rows, mod_cols))))
    mods_all = all_gather_vmem("gather_mods", jnp.stack(mods_local).reshape(-1, LANES))
    mods_all = jnp.moveaxis(mods_all.reshape(N_DEV, depth, S_rows, mod_cols), 0, 2).reshape(depth, S_rows, N_DEV * mod_cols)
    mods_lat = lax.dynamic_slice(mods_all, (0, me, 0), (depth, 1, N_DEV * mod_cols))[:, 0]
    mods_ctx = mods_all[:, N_DEV]

    def layer_mods(l):
        return [jnp.stack([mods_lat[l, j * D:(j + 1) * D], mods_ctx[l, j * D:(j + 1) * D]])[:, None] for j in range(6)]

    cs_swa = _rope_tables(L, T, SWA_HEAD_DIM)
    cs_mla = _rope_tables(L, T, MLA_ROPE)
    nm_descs = [_row(D), PAR, GRP, GRP]
    resid_descs = [_row(D, diff=False), _row(D), GRP]
    tab = [_row(LANES, diff=False), _row(LANES, diff=False)]
    swaq_descs = [_row(SWA_HEAD_DIM, 1), PAR] + tab
    swakv_descs = [_row(SWA_HEAD_DIM, 1), _row(SWA_HEAD_DIM, 1), PAR] + tab
    mlaq_descs = [_row(LANES, 1), _row(LANES, 1), PAR, PAR] + tab
    mlakv_descs = [_row(LANES, 1), _row(LANES, 1), _row(LANES), PAR, PAR] + tab
    ssdout_descs = [_row(SSM_INNER), _row(SSM_INNER, diff=False), _row(SSM_INNER), _row(SSM_INNER), PAR, PAR]
    merge_descs = [_row(D)] * 6
    swiglu_descs = [_row(ffn_h), _row(ffn_h)]
    seg_names = [sn for sn, _ in IN_SEGS]
    seg_groups = [seg_names[:7], seg_names[7:]]

    def layer_fwd(l, X, wt, p, mods):
        sh1, sc1, gt1, sh2, sc2, gt2 = mods
        r = {"X": X}
        r["h1"] = rf(f"l{l}_norm1", fn_norm_mod, nm_descs, [X, p["norm1_g"], sh1, sc1], [(D, 0, BF16)])[0]
        for gi, group in enumerate(seg_groups):
            r.update(zip(group, matmul_multi(f"l{l}_in{gi}", r["h1"], [wt[sn] for sn in group])))
        r["u"] = conv_fwd(f"l{l}_conv", r["xbc"], conv_w8[l], p["ssm_conv_b"], L)
        r["dts"] = rf(f"l{l}_softplus", fn_softplus, [_row(2 * SSM_HEADS), PAR], [r["dt"], p["dt_bias"]], [(2 * SSM_HEADS, 0, F32)])[0]
        for d in range(2):
            dt_d = r["dts"][:, d * SSM_HEADS:(d + 1) * SSM_HEADS]
            r[f"dt{d}"], r[f"dtt{d}"] = dt_d, dt_d.T
            r[f"y{d}"], r[f"hin{d}"] = ssd_fwd(f"l{l}_ssd{d}", d, r["u"], dt_d, dt_d.T, p["alog_row"][d], p["alog_col"][d], L)
        r["ys"] = rf(f"l{l}_ssd_out", fn_ssd_out, ssdout_descs, [r["y0"], r["y1"], r["u"], r["z"], p["d_lane"], p["ssm_norm_g"]],
                     [(SSM_INNER, 0, F32)])[0]
        r["Qs"] = rf(f"l{l}_swa_q", fn_swa_q, swaq_descs, [r["qs"], p["swa_q_norm_g"], *cs_swa], [(SWA_HEAD_DIM, 1, BF16)], SWA_Q_HEADS)[0]
        r["Ks"], r["Vs"] = rf(f"l{l}_swa_kv", fn_swa_kv, swakv_descs, [r["ks"], r["vs"], p["swa_k_norm_g"], *cs_swa],
                              [(SWA_HEAD_DIM, 1, BF16), (SWA_HEAD_DIM, 1, BF16)], SWA_KV_HEADS)
        r["Os"], r["lse_s"] = flash_fwd(f"l{l}_swa_fwd", cfg_swa, r["Qs"], r["Ks"], r["Vs"], p["sink"])
        r["cqn"] = rf(f"l{l}_q_lat", fn_rms, [_row(MLA_Q_RANK), PAR], [r["cq"], p["mla_q_lat_g"]], [(MLA_Q_RANK, 0, BF16)])[0]
        r["qn"], r["qr"] = matmul_multi(f"l{l}_uq", r["cqn"], [wt["uqn"], wt["uqr"]])
        r["Qm"] = rf(f"l{l}_mla_q", fn_mla_q, mlaq_descs, [r["qn"], r["qr"], p["gqn"], p["gqr"], *cs_mla], [(MLA_QK_PAD, 1, BF16)], MLA_HEADS)[0]
        r["ckvn"] = rf(f"l{l}_kv_lat", fn_rms, [_row(MLA_KV_RANK), PAR], [r["ckv"], p["mla_kv_lat_g"]], [(MLA_KV_RANK, 0, BF16)])[0]
        r["kn"], r["vp"] = matmul_multi(f"l{l}_ukv", r["ckvn"], [wt["uk"], wt["uv"]])
        r["Km"], r["Vm"] = rf(f"l{l}_mla_kv", fn_mla_kv, mlakv_descs, [r["kn"], r["vp"], r["kr"], p["gkn"], p["gkr"], *cs_mla],
                              [(MLA_QK_PAD, 1, BF16), (MLA_V, 1, BF16)], MLA_HEADS)
        r["Om"], r["lse_m"] = flash_fwd(f"l{l}_mla_fwd", cfg_mla, r["Qm"], r["Km"], r["Vm"], None)
        r["P1"] = matmul(r["ys"], wt["w_p_ssm"], "nn", f"l{l}_p_ssm")
        r["P2"] = matmul(r["Os"], wt["w_p_swa"], "nn", f"l{l}_p_swa")
        r["P3"] = matmul(r["Om"], wt["w_p_mla"], "nn", f"l{l}_p_mla")
        r["mg"] = rf(f"l{l}_merge", fn_merge, merge_descs, [r["g1"], r["g2"], r["g3"], r["P1"], r["P2"], r["P3"]], [(D, 0, BF16)])[0]
        r["A"] = matmul(r["mg"], wt["w_out"], "nn", f"l{l}_out")
        r["X1"] = rf(f"l{l}_resid1", fn_resid, resid_descs, [X, r["A"], gt1], [(D, 0, F32)])[0]
        r["h2"] = rf(f"l{l}_norm2", fn_norm_mod, nm_descs, [r["X1"], p["norm2_g"], sh2, sc2], [(D, 0, BF16)])[0]
        r["Fg"] = matmul(r["h2"], wt["fg"], "nt", f"l{l}_ffn_g")
        r["Fu"] = matmul(r["h2"], wt["fu"], "nt", f"l{l}_ffn_u")
        r["sg"] = rf(f"l{l}_swiglu", fn_swiglu, swiglu_descs, [r["Fg"], r["Fu"]], [(ffn_h, 0, BF16)])[0]
        r["B"] = matmul(r["sg"], wt["w_ffn_out"], "nn", f"l{l}_ffn_out")
        X2 = rf(f"l{l}_resid2", fn_resid, resid_descs, [r["X1"], r["B"], gt2], [(D, 0, F32)])[0]
        return X2, r

    def attn_bwd(tag, cfg, q, k, v, o, lse, do, sink):
        res = attn_delta(f"{tag}_delta", cfg, o, do, lse, sink)
        delta, dob = res[0], res[1]
        dsink = res[2] if cfg.has_sink else None
        if cfg.window is None and cfg.group == 1:
            dq, dk, dv = flash_bwd_fused(f"{tag}_bwd", cfg, q, k, v, dob, lse, delta)
        else:
            dq = flash_dq(f"{tag}_dq", cfg, q, k, v, dob, lse, delta)
            dk, dv = flash_dkv(f"{tag}_dkv", cfg, q, k, v, dob, lse, delta)
        return dq, dk, dv, dsink

    def layer_bwd(l, dX2, r, wt, p, mods):
        sh1, sc1, gt1, sh2, sc2, gt2 = mods
        g, gw = {}, {}
        dmod = [None] * 6
        dB, dmod[5] = rb(f"l{l}_resid2_b", fn_resid, resid_descs, [r["X1"], r["B"], gt2], [(D, 0, F32)], [dX2])
        dsg = matmul(dB, wt["w_ffn_out"], "nt", f"l{l}_ffn_out_da")
        gw["w_ffn_out"] = matmul(r["sg"], dB, "tn", f"l{l}_ffn_out_dw")
        dFg, dFu = rb(f"l{l}_swiglu_b", fn_swiglu, swiglu_descs, [r["Fg"], r["Fu"]], [(ffn_h, 0, BF16)], [dsg])
        dh2 = matmul(dFg, wt["fg"], "nn", f"l{l}_ffn_g_da")
        dh2 = matmul(dFu, wt["fu"], "nn", f"l{l}_ffn_u_da", add=dh2)
        gw["w_ffn_in"] = jnp.concatenate([matmul(r["h2"], dFg, "tn", f"l{l}_ffn_g_dw"), matmul(r["h2"], dFu, "tn", f"l{l}_ffn_u_dw")], axis=1)
        dX1, g["norm2_g"], dmod[3], dmod[4] = rb(f"l{l}_norm2_b", fn_norm_mod, nm_descs, [r["X1"], p["norm2_g"], sh2, sc2],
                                                [(D, 0, BF16)], [dh2], add={0: dX2})
        dA, dmod[2] = rb(f"l{l}_resid1_b", fn_resid, resid_descs, [r["X"], r["A"], gt1], [(D, 0, F32)], [dX1])
        dmg = matmul(dA, wt["w_out"], "nt", f"l{l}_out_da")
        gw["w_out"] = matmul(r["mg"], dA, "tn", f"l{l}_out_dw")
        dsegs = {}
        dsegs["g1"], dsegs["g2"], dsegs["g3"], dP1, dP2, dP3 = rb(
            f"l{l}_merge_b", fn_merge, merge_descs, [r["g1"], r["g2"], r["g3"], r["P1"], r["P2"], r["P3"]], [(D, 0, BF16)], [dmg])
        dys = matmul(dP1, wt["w_p_ssm"], "nt", f"l{l}_p_ssm_da")
        dOs = matmul(dP2, wt["w_p_swa"], "nt", f"l{l}_p_swa_da")
        dOm = matmul(dP3, wt["w_p_mla"], "nt", f"l{l}_p_mla_da")
        gw["w_p_ssm"] = matmul(r["ys"], dP1, "tn", f"l{l}_p_ssm_dw")
        gw["w_p_swa"] = matmul(r["Os"], dP2, "tn", f"l{l}_p_swa_dw")
        gw["w_p_mla"] = matmul(r["Om"], dP3, "tn", f"l{l}_p_mla_dw")
        dQm, dKm, dVm, _ = attn_bwd(f"l{l}_mla", cfg_mla_bwd, r["Qm"], r["Km"], r["Vm"], r["Om"], r["lse_m"], dOm, None)
        dkn, dvp, dsegs["kr"], dgkn, dgkr = rb(f"l{l}_mla_kv_b", fn_mla_kv, mlakv_descs,
                                               [r["kn"], r["vp"], r["kr"], p["gkn"], p["gkr"], *cs_mla],
                                               [(MLA_QK_PAD, 1, BF16), (MLA_V, 1, BF16)], [dKm, dVm], MLA_HEADS)
        dckvn = matmul_sum(f"l{l}_ukv_da", [(dkn, wt["uk"]), (dvp, wt["uv"])])
        dw_uk = matmul(r["ckvn"], dkn, "tn", f"l{l}_uk_dw").reshape(MLA_KV_RANK, MLA_HEADS, MLA_NOPE)
        dw_uv = matmul(r["ckvn"], dvp, "tn", f"l{l}_uv_dw").reshape(MLA_KV_RANK, MLA_HEADS, MLA_V)
        gw["w_mla_ukv"] = jnp.concatenate([dw_uk, dw_uv], axis=2).reshape(MLA_KV_RANK, -1)
        dsegs["ckv"], g["mla_kv_lat_g"] = rb(f"l{l}_kv_lat_b", fn_rms, [_row(MLA_KV_RANK), PAR], [r["ckv"], p["mla_kv_lat_g"]],
                                             [(MLA_KV_RANK, 0, BF16)], [dckvn])
        dqn, dqr, dgqn, dgqr = rb(f"l{l}_mla_q_b", fn_mla_q, mlaq_descs, [r["qn"], r["qr"], p["gqn"], p["gqr"], *cs_mla],
                                  [(MLA_QK_PAD, 1, BF16)], [dQm], MLA_HEADS)
        dcqn = matmul_sum(f"l{l}_uq_da", [(dqn, wt["uqn"]), (dqr, wt["uqr"])])
        dw_uqn = matmul(r["cqn"], dqn, "tn", f"l{l}_uqn_dw").reshape(MLA_Q_RANK, MLA_HEADS, MLA_NOPE)
        dw_uqr = matmul(r["cqn"], dqr, "tn", f"l{l}_uqr_dw").reshape(MLA_Q_RANK, MLA_HEADS, LANES)[:, :, :MLA_ROPE]
        gw["w_mla_uq"] = jnp.concatenate([dw_uqn, dw_uqr], axis=2).reshape(MLA_Q_RANK, -1)
        dsegs["cq"], g["mla_q_lat_g"] = rb(f"l{l}_q_lat_b", fn_rms, [_row(MLA_Q_RANK), PAR], [r["cq"], p["mla_q_lat_g"]],
                                           [(MLA_Q_RANK, 0, BF16)], [dcqn])
        g["mla_q_norm_g"] = jnp.concatenate([dgqn[0], dgqr[0, :MLA_ROPE]])
        g["mla_k_norm_g"] = jnp.concatenate([dgkn[0], dgkr[0, :MLA_ROPE]])
        dQs, dKs, dVs, dsink = attn_bwd(f"l{l}_swa", cfg_swa, r["Qs"], r["Ks"], r["Vs"], r["Os"], r["lse_s"], dOs, p["sink"])
        g["swa_sink"] = dsink.reshape(SWA_Q_HEADS)
        dsegs["qs"], g["swa_q_norm_g"] = rb(f"l{l}_swa_q_b", fn_swa_q, swaq_descs, [r["qs"], p["swa_q_norm_g"], *cs_swa],
                                            [(SWA_HEAD_DIM, 1, BF16)], [dQs], SWA_Q_HEADS)
        dsegs["ks"], dsegs["vs"], g["swa_k_norm_g"] = rb(f"l{l}_swa_kv_b", fn_swa_kv, swakv_descs,
                                                         [r["ks"], r["vs"], p["swa_k_norm_g"], *cs_swa],
                                                         [(SWA_HEAD_DIM, 1, BF16), (SWA_HEAD_DIM, 1, BF16)], [dKs, dVs], SWA_KV_HEADS)
        dy, dxs, dsegs["z"], dd_lane, g["ssm_norm_g"] = rb(
            f"l{l}_ssd_out_b", fn_ssd_out, ssdout_descs, [r["y0"], r["y1"], r["u"], r["z"], p["d_lane"], p["ssm_norm_g"]],
            [(SSM_INNER, 0, F32)], [dys])
        g["ssm_d"] = dd_lane.reshape(SSM_HEADS, SSM_HEAD_DIM).sum(axis=1)
        du, ddts, dalog = None, [], []
        for d in range(2):
            du, ddt, ddtt, dar, dac = ssd_bwd(f"l{l}_ssd{d}_b", d, r["u"], r[f"dt{d}"], r[f"dtt{d}"], p["alog_row"][d], p["alog_col"][d],
                                              r[f"hin{d}"], dy, L, add_x=dxs if d == 0 else None, add_u=du)
            ddts.append(ddt + ddtt.T)
            dalog.append(dar[0] + dac[:, 0])
        g["ssm_a_log"] = jnp.stack(dalog)
        dsegs["xbc"], dconv_w, g["ssm_conv_b"] = conv_bwd(f"l{l}_conv_b", r["xbc"], conv_w8[l], p["ssm_conv_b"], du, L)
        dsegs["dt"], ddt_bias = rb(f"l{l}_softplus_b", fn_softplus, [_row(2 * SSM_HEADS), PAR], [r["dt"], p["dt_bias"]],
                                   [(2 * SSM_HEADS, 0, F32)], [jnp.concatenate(ddts, axis=1)])
        g["ssm_dt_bias"] = ddt_bias.reshape(2, SSM_HEADS)
        g["ssm_conv_w"] = dconv_w[:SSM_CONV]
        dh1, dws = None, []
        for gi, group in enumerate(seg_groups):
            dh1 = matmul_sum(f"l{l}_in_da{gi}", [(dsegs[sn], wt[sn]) for sn in group], add=dh1)
        for sn, w in zip(seg_names, in_widths):
            dws.append(matmul(r["h1"], dsegs[sn], "tn", f"l{l}_in_{sn}_dw")[:, :w])
        gw["w_in"] = jnp.concatenate(dws, axis=1)
        dX, g["norm1_g"], dmod[0], dmod[1] = rb(f"l{l}_norm1_b", fn_norm_mod, nm_descs, [r["X"], p["norm1_g"], sh1, sc1],
                                               [(D, 0, BF16)], [dh1], add={0: dX1})
        for n in ("norm1_g", "norm2_g", "ssm_conv_b", "ssm_norm_g", "swa_q_norm_g", "swa_k_norm_g", "mla_q_lat_g", "mla_kv_lat_g"):
            g[n] = g[n][0]
        dmod_lat = jnp.concatenate([dm[0, 0] for dm in dmod])
        dmod_ctx = jnp.concatenate([dm[1, 0] for dm in dmod])
        return dX, g, gw, dmod_lat, dmod_ctx

    X = jnp.concatenate([x, ctx], axis=0)
    saved = []
    for l in range(depth):
        wt, p, mods = layer_weights(l), layer_params(l), layer_mods(l)
        X, r = layer_fwd(l, X, wt, p, mods)
        saved.append((r, wt, p, mods))
    loss_part, dX = loss_and_grad("loss", X, loss_target[0], L)
    loss = lax.psum(loss_part[0, 0], ("x", "y", "c"))
    small_g = [None] * depth
    big_g = [None] * depth
    dmods = [None] * depth
    for l in reversed(range(depth)):
        r, wt, p, mods = saved[l]
        dX, small_g[l], big_g[l], dm_lat, dm_ctx = layer_bwd(l, dX, r, wt, p, mods)
        dmods[l] = jnp.stack([dm_lat, dm_ctx])
    grad_x = dX[:L][None]

    dm_all = all_gather_vmem("gather_dmods", jnp.stack(dmods).reshape(-1, LANES)).reshape(N_DEV, depth, 2, N_DEV * mod_cols)
    dm_rows = jnp.concatenate([jnp.moveaxis(dm_all[:, :, 0], 0, 1), dm_all[:, :, 1].sum(axis=0)[:, None],
                               jnp.zeros((depth, S_rows - N_DEV - 1, N_DEV * mod_cols), F32)], axis=1)
    dm_mine = lax.dynamic_slice(dm_rows, (0, 0, me * mod_cols), (depth, S_rows, mod_cols))
    grads = {}
    grads["w_mod"] = jnp.stack([matmul(S_mat, dm_mine[l], "tn", f"mod{l}_dw") for l in range(depth)])
    d_silu = None
    for l in range(depth):
        d_silu = matmul(dm_mine[l], inp["w_mod"][l], "nt", f"mod{l}_da", add=d_silu)
    small = {n: jnp.stack([small_g[l][n] for l in range(depth)]) for n in small_g[0]}
    small["c_ctx"] = small_bwd("silu_c_b", fn_silu, [inp["c_ctx"][None]], [d_silu[N_DEV:N_DEV + 1]])[0][0]
    small["b_mod"] = jnp.stack(dmods).sum(axis=1)

    rep_shapes = [inp[n].shape for n in REPLICATED]
    conv_shape = (depth, SSM_CONV, SSM_CONV_DIM)
    packed = _pack([small[n] for n in REPLICATED] + [small["ssm_conv_w"]], 8 * LANES)
    small_sum = sum_parts("sum_small", all_gather_vmem("gather_small", packed))
    summed = _unpack(small_sum, rep_shapes + [conv_shape])
    for n, gsum in zip(REPLICATED, summed):
        grads[n] = gsum
    grads["ssm_conv_w"] = lax.dynamic_slice(summed[-1], (0, 0, me * cw[2]), cw)

    slabs = []
    for n in GATHERED:
        gfull = jnp.stack([big_g[l][n] for l in range(depth)])
        if n in COLUMN_SHARDED:
            k_dim, n_dim = gfull.shape[1], gfull.shape[2]
            slab = jnp.moveaxis(gfull.reshape(depth, k_dim, N_DEV, n_dim // N_DEV), 2, 0)
        else:
            k_dim, n_dim = gfull.shape[1], gfull.shape[2]
            slab = jnp.moveaxis(gfull.reshape(depth, N_DEV, k_dim // N_DEV, n_dim), 1, 0)
        slabs.append(cast_bf16(f"cast_{n}", slab.reshape(-1, slab.shape[-1])).reshape(slab.shape))
    for n, parts in zip(GATHERED, exchange_hbm("exchange_grads", slabs)):
        shp = inp[n].shape
        grads[n] = sum_parts(f"sum_{n}", parts.reshape(N_DEV, shp[0] * shp[1], shp[2])).reshape(shp)

    delta, new_m, new_v = {}, {}, {}
    rep_pack = lambda d: _pack([d[n] for n in REPLICATED], 8 * LANES)
    rep_out = adamw("adamw_small", rep_pack(inp), rep_pack(grads), rep_pack(mom_m), rep_pack(mom_v))
    for out, res in zip((delta, new_m, new_v), rep_out):
        for n, a in zip(REPLICATED, _unpack(res, rep_shapes)):
            out[n] = a
    for n in ["w_mod", "ssm_conv_w"] + GATHERED:
        shp = inp[n].shape
        two_d = (shp[0] * shp[1], shp[2])
        res = adamw(f"adamw_{n}", inp[n].reshape(two_d), grads[n].reshape(two_d), mom_m[n].reshape(two_d), mom_v[n].reshape(two_d))
        delta[n], new_m[n], new_v[n] = [a.reshape(shp) for a in res]

    return (loss, grad_x, *[grads[n] for n in WEIGHT_NAMES], *[delta[n] for n in WEIGHT_NAMES],
            *[new_m[n] for n in WEIGHT_NAMES], *[new_v[n] for n in WEIGHT_NAMES])
```

```python
import functools
import math

import numpy as np
import jax
import jax.numpy as jnp
from jax import lax
from jax.experimental import pallas as pl
from jax.experimental.pallas import tpu as pltpu

F32 = jnp.float32
BF16 = jnp.bfloat16

N_DEV = 8
V7X_VMEM_BYTES = 64 * 1024 * 1024
VMEM_LIMIT_BYTES = V7X_VMEM_BYTES - 8 * 1024 * 1024
LANES = 128

EPS = 1e-6
ROPE_BASE = 10000.0
GRID_W = 64
SSM_HEADS, SSM_HEAD_DIM, SSM_GROUPS, SSM_STATE, SSM_CONV, SSM_CHUNK = 16, 64, 2, 128, 5, 128
SSM_INNER = SSM_HEADS * SSM_HEAD_DIM
SSM_CONV_DIM = SSM_INNER + 2 * SSM_GROUPS * SSM_STATE
SWA_Q_HEADS, SWA_KV_HEADS, SWA_HEAD_DIM, SWA_WINDOW = 8, 2, 128, 128
MLA_HEADS, MLA_Q_RANK, MLA_KV_RANK, MLA_NOPE, MLA_ROPE, MLA_V = 8, 384, 256, 128, 64, 128
MLA_QK = MLA_NOPE + MLA_ROPE
MLA_QK_PAD = 2 * LANES
ADAM_LR, ADAM_B1, ADAM_B2, ADAM_EPS, ADAM_WD, ADAM_STEP = 0.001, 0.9, 0.999, 1e-08, 0.01, 10

ROW_TILE = 256


def _cparams(sem, **kw):
    return pltpu.CompilerParams(dimension_semantics=sem, vmem_limit_bytes=VMEM_LIMIT_BYTES, **kw)


def _pick(dim, prefs):
    for p in prefs:
        if dim % p == 0:
            return p
    return dim


def matmul(a, b, mode, name, out_dtype=F32, add=None):
    if mode == "nn":
        (M, K), (K2, N) = a.shape, b.shape
    elif mode == "nt":
        (M, K), (N, K2) = a.shape, b.shape
    else:
        (K, M), (K2, N) = a.shape, b.shape
    assert K == K2, (name, a.shape, b.shape)
    has_add = add is not None
    tm, tn, tk = _matmul_tiles(M, N, K, a.dtype.itemsize, b.dtype.itemsize, jnp.dtype(out_dtype).itemsize, has_add,
                                   m_on_lanes=(mode == "tn"))
    nk = K // tk
    dims = {"nn": (((1,), (0,)), ((), ())), "nt": (((1,), (1,)), ((), ())), "tn": (((0,), (0,)), ((), ()))}[mode]
    a_spec = pl.BlockSpec((tk, tm), lambda i, j, k: (k, i)) if mode == "tn" else pl.BlockSpec((tm, tk), lambda i, j, k: (i, k))
    b_spec = pl.BlockSpec((tn, tk), lambda i, j, k: (j, k)) if mode == "nt" else pl.BlockSpec((tk, tn), lambda i, j, k: (k, j))
    o_spec = pl.BlockSpec((tm, tn), lambda i, j, k: (i, j))

    def body(*refs):
        a_ref, b_ref = refs[:2]
        c_ref = refs[2] if has_add else None
        o_ref = refs[3] if has_add else refs[2]
        part = lax.dot_general(a_ref[...].astype(BF16), b_ref[...].astype(BF16), dims, preferred_element_type=F32)
        if nk == 1:
            o_ref[...] = (part + c_ref[...] if has_add else part).astype(o_ref.dtype)
            return
        acc_ref = refs[-1]
        k = pl.program_id(2)

        @pl.when(k == 0)
        def _():
            acc_ref[...] = part + c_ref[...] if has_add else part

        @pl.when(k > 0)
        def _():
            acc_ref[...] += part

        @pl.when(k == nk - 1)
        def _():
            o_ref[...] = acc_ref[...].astype(o_ref.dtype)

    ins = [a, b] + ([add] if has_add else [])
    in_specs = [a_spec, b_spec] + ([o_spec] if has_add else [])
    return pl.pallas_call(
        body, name=name, grid=(M // tm, N // tn, nk), in_specs=in_specs, out_specs=o_spec,
        out_shape=jax.ShapeDtypeStruct((M, N), out_dtype),
        scratch_shapes=[pltpu.VMEM((tm, tn), F32)] if nk > 1 else [],
        input_output_aliases=({2: 0} if has_add else {}),
        compiler_params=_cparams(("parallel", "parallel", "arbitrary")),
    )(*ins)


def matmul_sum(name, pairs, add=None):
    M, N = pairs[0][0].shape[0], pairs[0][1].shape[1]
    n = len(pairs)
    has_add = add is not None
    resident = sum(2 * b.shape[0] * N * b.dtype.itemsize for _, b in pairs)
    tm = next((t for t in (768, 512, 384, 256, 128) if M % t == 0 and resident + sum(
        2 * t * a.shape[1] * a.dtype.itemsize + t * a.shape[1] * 2 for a, _ in pairs) + 6 * t * N * 4 <= MATMUL_VMEM_BUDGET), None)
    assert tm is not None, name

    def body(*refs):
        acc = refs[2 * n][...] if has_add else None
        for s in range(n):
            part = jnp.dot(refs[2 * s][...].astype(BF16), refs[2 * s + 1][...].astype(BF16), preferred_element_type=F32)
            acc = part if acc is None else acc + part
        refs[-1][...] = acc

    o_spec = pl.BlockSpec((tm, N), lambda i: (i, 0))
    in_specs, ins = [], []
    for a, b in pairs:
        in_specs += [pl.BlockSpec((tm, a.shape[1]), lambda i: (i, 0)), pl.BlockSpec(b.shape, lambda i: (0, 0))]
        ins += [a, b]
    if has_add:
        in_specs.append(o_spec)
        ins.append(add)
    return pl.pallas_call(body, name=name, grid=(M // tm,), in_specs=in_specs, out_specs=o_spec,
                          out_shape=jax.ShapeDtypeStruct((M, N), F32), input_output_aliases=({2 * n: 0} if has_add else {}),
                          compiler_params=_cparams(("parallel",)))(*ins)


def matmul_multi(name, a, bs):
    M, K = a.shape
    n = len(bs)
    resident = sum(2 * b.shape[0] * K * b.dtype.itemsize for b in bs)
    n_total = sum(b.shape[0] for b in bs)
    tm = next((t for t in (768, 512, 384, 256, 128) if M % t == 0 and
               resident + 2 * t * K * a.dtype.itemsize + 3 * t * n_total * 4 <= MATMUL_VMEM_BUDGET), None)
    assert tm is not None, name

    def body(*refs):
        lhs = refs[0][...].astype(BF16)
        for s in range(n):
            refs[1 + n + s][...] = lax.dot_general(lhs, refs[1 + s][...].astype(BF16), NT_DIMS, preferred_element_type=F32)

    in_specs = [pl.BlockSpec((tm, K), lambda i: (i, 0))] + [pl.BlockSpec(b.shape, lambda i: (0, 0)) for b in bs]
    return pl.pallas_call(
        body, name=name, grid=(M // tm,), in_specs=in_specs,
        out_specs=[pl.BlockSpec((tm, b.shape[0]), lambda i: (i, 0)) for b in bs],
        out_shape=[jax.ShapeDtypeStruct((M, b.shape[0]), F32) for b in bs], compiler_params=_cparams(("parallel",)))(a, *bs)


MATMUL_VMEM_BUDGET = 36 * 1024 * 1024


def _matmul_tiles(M, N, K, a_bytes, b_bytes, o_bytes, has_add, m_on_lanes=False):
    tk = K if K <= 1536 else _pick(K, (1408, 1024, 768, 704, 512, 256))
    nk = K // tk
    m_cands = [t for t in (1024, 768, 512, 384, 256, 128) if M % t == 0] or [M]
    if M % 768 and M % 1024:
        m_cands += [t for t in (1408, 704, 352) if M % t == 0 and not (m_on_lanes and t % LANES)]
    n_cands = [t for t in range(LANES, min(N, 2816) + 1, LANES) if N % t == 0] or [N]
    best = None
    for tm in m_cands:
        for tn in n_cands:
            pipeline = 2 * (tm * tk * a_bytes + tk * tn * b_bytes + tm * tn * o_bytes) + (2 * tm * tn * 4 if has_add else 0)
            temps = tm * tn * 4 * (2 if nk > 1 else 1) + (tm * tk * 2 if a_bytes == 4 else 0) + (tk * tn * 2 if b_bytes == 4 else 0)
            if pipeline + temps <= MATMUL_VMEM_BUDGET:
                score = (tm * tn, tn)
                if best is None or score > best[0]:
                    best = (score, tm, tn)
    if best is None:
        return m_cands[-1], n_cands[0], tk
    return best[1], best[2], tk


def _row_specs(descs, arrays, tm, nct, heads):
    specs = []
    for d, arr in zip(descs, arrays):
        if d[0] == "row":
            _, w, per_head, off, _ = d
            specs.append(pl.BlockSpec((tm, w * (heads if per_head else 1)), lambda i, off=off: (i, off)))
        elif d[0] == "par":
            specs.append(pl.BlockSpec(arr.shape, lambda i, nd=arr.ndim: (0,) * nd))
        else:
            specs.append(pl.BlockSpec((1,) + arr.shape[1:], lambda i, nd=arr.ndim: (jnp.where(i >= nct, 1, 0),) + (0,) * (nd - 1)))
    return specs


def _load(d, ref, h):
    if d[0] == "grp":
        return ref[0]
    if d[0] == "row" and d[2]:
        return ref[:, h * d[1]:(h + 1) * d[1]]
    return ref[...]


def _out_specs(outs, tm, heads):
    return [pl.BlockSpec((tm, w * (heads if ph else 1)), lambda i: (i, 0)) for (w, ph, _) in outs]


def rowop_fwd(name, fn, descs, arrays, outs, T, n_ctx, heads=1, tm=ROW_TILE):
    nct = n_ctx // tm
    n_in = len(descs)

    def body(*refs):
        for h in range(heads):
            res = fn(*[_load(d, r, h) for d, r in zip(descs, refs[:n_in])])
            for o_ref, r, (w, ph, _) in zip(refs[n_in:], res, outs):
                if ph:
                    o_ref[:, h * w:(h + 1) * w] = r.astype(o_ref.dtype)
                else:
                    o_ref[...] = r.astype(o_ref.dtype)

    out_shape = [jax.ShapeDtypeStruct((T, w * (heads if ph else 1)), dt) for (w, ph, dt) in outs]
    return pl.pallas_call(
        body, name=name, grid=(T // tm,), in_specs=_row_specs(descs, arrays, tm, nct, heads), out_specs=_out_specs(outs, tm, heads),
        out_shape=out_shape, compiler_params=_cparams(("parallel",)),
    )(*arrays)


def rowop_bwd(name, fn, descs, arrays, outs, cts, T, n_ctx, heads=1, tm=ROW_TILE, add=None):
    nct = n_ctx // tm
    n_in, n_ct = len(descs), len(cts)
    add = add or {}
    diff_idx = [k for k, d in enumerate(descs) if d[-1]]
    add_idx = [k for k in diff_idx if k in add]

    def body(*refs):
        in_refs, ct_refs = refs[:n_in], refs[n_in:n_in + n_ct]
        add_refs = dict(zip(add_idx, refs[n_in + n_ct:n_in + n_ct + len(add_idx)]))
        g_refs = refs[n_in + n_ct + len(add_idx):]
        i = pl.program_id(0)
        shared = {}
        for h in range(heads):
            vals = [_load(d, r, h) for d, r in zip(descs, in_refs)]

            def f(*dvals, vals=vals):
                full = list(vals)
                for k, v in zip(diff_idx, dvals):
                    full[k] = v
                return tuple(fn(*full))

            _, vjp = jax.vjp(f, *[vals[k] for k in diff_idx])
            cts_h = tuple(c[:, h * w:(h + 1) * w] if ph else c[...] for c, (w, ph, _) in zip(ct_refs, outs))
            for k, g_ref, g in zip(diff_idx, g_refs, vjp(cts_h)):
                d = descs[k]
                if d[0] == "row" and d[2]:
                    g_ref[:, h * d[1]:(h + 1) * d[1]] = g.astype(g_ref.dtype)
                else:
                    shared[k] = g if k not in shared else shared[k] + g
        for k, g_ref in zip(diff_idx, g_refs):
            d = descs[k]
            if k not in shared:
                continue
            g = shared[k]
            if d[0] == "row":
                if k in add_refs:
                    g = g + add_refs[k][...]
                g_ref[...] = g.astype(g_ref.dtype)
            elif d[0] == "par":
                _accumulate(g_ref, g, i == 0)
            else:
                _accumulate(g_ref, g[None], jnp.logical_or(i == 0, i == nct))

    in_specs = _row_specs(descs, arrays, tm, nct, heads)
    g_specs, g_shape = [], []
    for k in diff_idx:
        d = descs[k]
        if d[0] == "row":
            g_specs.append(pl.BlockSpec((tm, d[1] * (heads if d[2] else 1)), lambda i: (i, 0)))
            g_shape.append(jax.ShapeDtypeStruct((T, d[1] * (heads if d[2] else 1)), F32))
        else:
            g_specs.append(in_specs[k])
            g_shape.append(jax.ShapeDtypeStruct(arrays[k].shape, F32))
    add_specs = [g_specs[diff_idx.index(k)] for k in add_idx]
    return pl.pallas_call(
        body, name=name, grid=(T // tm,), in_specs=in_specs + _out_specs(outs, tm, heads) + add_specs, out_specs=g_specs,
        out_shape=g_shape, compiler_params=_cparams(("arbitrary",)),
    )(*arrays, *cts, *[add[k] for k in add_idx])


def _accumulate(ref, val, first):
    @pl.when(first)
    def _():
        ref[...] = val.astype(ref.dtype)

    @pl.when(jnp.logical_not(first))
    def _():
        ref[...] += val.astype(ref.dtype)


def _rms(x, count=None):
    n = x.shape[-1] if count is None else count
    return x * lax.rsqrt(jnp.sum(x * x, axis=-1, keepdims=True) * (1.0 / n) + EPS)


def _swap_halves(x, nf):
    w = x.shape[-1]
    lane = lax.broadcasted_iota(jnp.int32, x.shape, x.ndim - 1)
    return jnp.where((lane % (2 * nf)) < nf, pltpu.roll(x, w - nf, x.ndim - 1), pltpu.roll(x, nf, x.ndim - 1))


def _make_rope(nf):
    @jax.custom_vjp
    def rope(x, c, s):
        return x * c + _swap_halves(x, nf) * s

    def fwd(x, c, s):
        return rope(x, c, s), (c, s)

    def bwd(res, g):
        c, s = res
        return g * c + _swap_halves(g * s, nf), jnp.zeros_like(c), jnp.zeros_like(s)

    rope.defvjp(fwd, bwd)
    return rope


_rope_swa = _make_rope(SWA_HEAD_DIM // 4)
_rope_mla = _make_rope(MLA_ROPE // 4)


@jax.custom_vjp
def _softplus(x):
    e = jnp.exp(-jnp.abs(x))
    u = 1.0 + e
    log1p_e = jnp.where(u == 1.0, e, jnp.log(u) * e / jnp.where(u == 1.0, 1.0, u - 1.0))
    return jnp.maximum(x, 0.0) + log1p_e


_softplus.defvjp(lambda x: (_softplus(x), x), lambda x, g: (g * jax.nn.sigmoid(x),))


def fn_norm_mod(x, g, shift, scale):
    return (_rms(x) * g * (1.0 + scale) + shift,)


def fn_rms(x, g):
    return (_rms(x) * g,)


def fn_resid(x, a, gate):
    return (x + gate * a,)


def fn_softplus(dt, bias):
    return (_softplus(dt + bias),)


def fn_ssd_out(yf, yb, xs, z, d_lane, g):
    y = yf + yb + d_lane * xs
    return (_rms(y * (z * jax.nn.sigmoid(z))) * g,)


def fn_swa_q(q, g, c, s):
    return (_rope_swa(_rms(q) * g, c, s),)


def fn_swa_kv(k, v, g, c, s):
    return (_rope_swa(_rms(k) * g, c, s), v)


def fn_mla_q(qn, qr, gn, gr, c, s):
    return (jnp.concatenate([_rms(qn) * gn, _rope_mla(_rms(qr, MLA_ROPE) * gr, c, s)], axis=-1),)


def fn_mla_kv(kn, v, kr, gn, gr, c, s):
    return (jnp.concatenate([_rms(kn) * gn, _rope_mla(_rms(kr, MLA_ROPE) * gr, c, s)], axis=-1), v)


def fn_merge(g1, g2, g3, p1, p2, p3):
    return (jax.nn.sigmoid(g1) * p1 + jax.nn.sigmoid(g2) * p2 + jax.nn.sigmoid(g3) * p3,)


def fn_swiglu(g, u):
    return (g * jax.nn.sigmoid(g) * u,)


ATTN_TILE = 256
NT_DIMS = (((1,), (1,)), ((), ()))


class AttnCfg:
    def __init__(self, hq, group, dq, dv, scale, window, has_sink, L, T, chunk, kv_block):
        self.hq, self.group, self.dq, self.dv, self.scale = hq, group, dq, dv, scale
        self.window, self.has_sink, self.L, self.T = window, has_sink, L, T
        self.chunk = _pick(L, (chunk, ATTN_TILE))
        self.ctx_chunk = T - L
        self.kv_block = kv_block
        self.q_block = kv_block * group
        assert L % ATTN_TILE == 0 and (T - L) % ATTN_TILE == 0 and L % self.chunk == 0
        assert (hq // group) % kv_block == 0
        if window is not None:
            assert (ATTN_TILE + 2 * window) % self.chunk == 0
            self.window_chunks = min((ATTN_TILE + 2 * window) // self.chunk, L // self.chunk)
            self.align = math.gcd(self.chunk, window)
        else:
            self.align = self.chunk


LOG2E = math.log2(math.e)


def _latent_chunks(cfg, r0):
    c = cfg.chunk
    if cfg.window is None:
        lo, n = 0, cfg.L // c
    else:
        n = cfg.window_chunks
        lo = jnp.clip(r0 - cfg.window, 0, cfg.L - n * c)
    return lo, n


def _visible(cfg, rows_q, rows_k):
    return jnp.logical_or(rows_k >= cfg.L, jnp.abs(rows_k - rows_q) <= cfg.window)


def flash_fwd(name, cfg, q, k, v, sink):
    T, tq, c = cfg.T, ATTN_TILE, cfg.chunk
    hq, g, dq, dv, hb, kb = cfg.hq, cfg.group, cfg.dq, cfg.dv, cfg.q_block, cfg.kv_block
    to_log2 = cfg.scale * LOG2E

    def body(*refs):
        if cfg.has_sink:
            q_ref, k_ref, v_ref, sink_ref, o_ref, lse_ref = refs
        else:
            q_ref, k_ref, v_ref, o_ref, lse_ref = refs
        q0 = pl.program_id(1) * tq
        qs = [q_ref[:, hh * dq:(hh + 1) * dq] for hh in range(hb)]
        lat_lo, lat_n = _latent_chunks(cfg, q0)
        n = jnp.where(q0 >= cfg.L, 0, lat_n)
        rows_q = q0 + lax.broadcasted_iota(jnp.int32, (tq, 1), 0)

        def start(t):
            return pl.multiple_of(lat_lo + jnp.minimum(t, lat_n - 1) * c, cfg.align)

        def logits(ks, size):
            return tuple(lax.dot_general(qs[hh], k_ref[pl.ds(ks, size), (hh // g) * dq:(hh // g + 1) * dq], NT_DIMS,
                                         preferred_element_type=F32) for hh in range(hb))

        def update(state, s_all, ks, size, masked):
            new_state = []
            for hh in range(hb):
                m, l, acc = state[hh]
                s = s_all[hh]
                if masked:
                    rows_k = ks + lax.broadcasted_iota(jnp.int32, (1, size), 1)
                    s = jnp.where(_visible(cfg, rows_q, rows_k), s, -jnp.inf)
                m_new = jnp.maximum(m, jnp.max(s, axis=-1, keepdims=True) * to_log2)
                alpha = jnp.exp2(m - m_new)
                p = jnp.exp2(s * to_log2 - m_new)
                l = alpha * l + jnp.sum(p, axis=-1, keepdims=True)
                kh = hh // g
                acc = alpha * acc + jnp.dot(p.astype(BF16), v_ref[pl.ds(ks, size), kh * dv:(kh + 1) * dv], preferred_element_type=F32)
                new_state.append((m_new, l, acc))
            return tuple(new_state)

        def step(t, carry):
            state, s_all = carry
            s_next = logits(start(t + 1), c)
            return update(state, s_all, start(t), c, cfg.window is not None), s_next

        state = []
        for hh in range(hb):
            if cfg.has_sink:
                m0 = jnp.zeros((tq, 1), F32) + sink_ref[hh] * LOG2E
                l0 = jnp.ones((tq, 1), F32)
            else:
                m0 = jnp.full((tq, 1), -jnp.inf, F32)
                l0 = jnp.zeros((tq, 1), F32)
            state.append((m0, l0, jnp.zeros((tq, dv), F32)))
        state = update(tuple(state), logits(cfg.L, cfg.ctx_chunk), cfg.L, cfg.ctx_chunk, False)
        state, _ = lax.fori_loop(0, n, step, (state, logits(start(0), c)))
        for hh in range(hb):
            m, l, acc = state[hh]
            o_ref[:, hh * dv:(hh + 1) * dv] = acc / l
            lse_ref[hh] = m + jnp.log2(l)

    in_specs = [pl.BlockSpec((tq, hb * dq), lambda h, i: (i, h)),
                pl.BlockSpec((T, kb * dq), lambda h, i: (0, h)),
                pl.BlockSpec((T, kb * dv), lambda h, i: (0, h))]
    ins = [q, k, v]
    if cfg.has_sink:
        in_specs.append(pl.BlockSpec((hb, 1, 1), lambda h, i: (h, 0, 0)))
        ins.append(sink)
    return pl.pallas_call(
        body, name=name, grid=(hq // hb, T // tq), in_specs=in_specs,
        out_specs=[pl.BlockSpec((tq, hb * dv), lambda h, i: (i, h)), pl.BlockSpec((hb, tq, 1), lambda h, i: (h, i, 0))],
        out_shape=[jax.ShapeDtypeStruct((T, hq * dv), F32), jax.ShapeDtypeStruct((hq, T, 1), F32)],
        compiler_params=_cparams(("parallel", "parallel")),
    )(*ins)


def attn_delta(name, cfg, o, do, lse, sink):
    T, tm, hq, dv = cfg.T, ATTN_TILE, cfg.hq, cfg.dv

    def body(*refs):
        if cfg.has_sink:
            o_ref, do_ref, lse_ref, sink_ref, delta_ref, dob_ref, dsink_ref = refs
        else:
            o_ref, do_ref, delta_ref, dob_ref = refs
        dob_ref[...] = do_ref[...].astype(BF16)
        parts = []
        for h in range(hq):
            delta = jnp.sum(do_ref[:, h * dv:(h + 1) * dv] * o_ref[:, h * dv:(h + 1) * dv], axis=-1, keepdims=True)
            delta_ref[h] = delta
            if cfg.has_sink:
                parts.append(-jnp.sum(jnp.exp2(sink_ref[h] * LOG2E - lse_ref[h]) * delta, axis=0, keepdims=True)[None])
        if cfg.has_sink:
            _accumulate(dsink_ref, jnp.concatenate(parts, axis=0), pl.program_id(0) == 0)

    head_tile = pl.BlockSpec((tm, hq * dv), lambda i: (i, 0))
    col = pl.BlockSpec((hq, tm, 1), lambda i: (0, i, 0))
    one = pl.BlockSpec((hq, 1, 1), lambda i: (0, 0, 0))
    in_specs, ins = [head_tile, head_tile], [o, do]
    out_specs = [col, head_tile]
    out_shape = [jax.ShapeDtypeStruct((hq, T, 1), F32), jax.ShapeDtypeStruct((T, hq * dv), BF16)]
    if cfg.has_sink:
        in_specs += [col, one]
        ins += [lse, sink]
        out_specs.append(one)
        out_shape.append(jax.ShapeDtypeStruct((hq, 1, 1), F32))
    return pl.pallas_call(body, name=name, grid=(T // tm,), in_specs=in_specs, out_specs=out_specs, out_shape=out_shape,
                          compiler_params=_cparams(("arbitrary",)))(*ins)


def flash_bwd_fused(name, cfg, q, k, v, dob, lse, delta):
    T, L, tq, c, cc = cfg.T, cfg.L, ATTN_TILE, cfg.chunk, cfg.ctx_chunk
    hq, g, dq, dv = cfg.hq, cfg.group, cfg.dq, cfg.dv
    hk = hq // g
    nq = T // tq
    to_log2 = cfg.scale * LOG2E
    masked = cfg.window is not None

    def body(q_ref, k_ref, v_ref, do_ref, lse_ref, delta_ref, dq_ref, dk_ref, dv_ref):
        i = pl.program_id(1)
        q0 = i * tq

        @pl.when(i == 0)
        def _():
            dk_ref[...] = jnp.zeros_like(dk_ref)
            dv_ref[...] = jnp.zeros_like(dv_ref)

        qs = [q_ref[:, hh * dq:(hh + 1) * dq] for hh in range(g)]
        dos = [do_ref[:, hh * dv:(hh + 1) * dv] for hh in range(g)]
        lat_lo, lat_n = _latent_chunks(cfg, q0)
        n = jnp.where(q0 >= L, 0, lat_n)
        rows_q = q0 + lax.broadcasted_iota(jnp.int32, (tq, 1), 0)

        def start(t):
            return pl.multiple_of(lat_lo + jnp.minimum(t, lat_n - 1) * c, cfg.align)

        def products(ks, size):
            kk, vv = k_ref[pl.ds(ks, size), :], v_ref[pl.ds(ks, size), :]
            return tuple((lax.dot_general(qs[hh], kk, NT_DIMS, preferred_element_type=F32),
                          lax.dot_general(dos[hh], vv, NT_DIMS, preferred_element_type=F32)) for hh in range(g))

        def update(accs, prods, ks, size, mask_it):
            new_accs, dv_part, dk_part = [], None, None
            for hh in range(g):
                s, dp = prods[hh]
                p = jnp.exp2(s * to_log2 - lse_ref[hh])
                if mask_it:
                    rows_k = ks + lax.broadcasted_iota(jnp.int32, (1, size), 1)
                    p = jnp.where(_visible(cfg, rows_q, rows_k), p, 0.0)
                ds = (p * (dp - delta_ref[hh])).astype(BF16)
                dv_h = lax.dot_general(p.astype(BF16), dos[hh], TN_DIMS, preferred_element_type=F32)
                dk_h = lax.dot_general(ds, qs[hh], TN_DIMS, preferred_element_type=F32)
                dv_part = dv_h if dv_part is None else dv_part + dv_h
                dk_part = dk_h if dk_part is None else dk_part + dk_h
                new_accs.append(accs[hh] + jnp.dot(ds, k_ref[pl.ds(ks, size), :], preferred_element_type=F32))
            dv_ref[pl.ds(ks, size), :] += dv_part
            dk_ref[pl.ds(ks, size), :] += dk_part
            return tuple(new_accs)

        def step(t, carry):
            accs, prods = carry
            nxt = products(start(t + 1), c)
            return update(accs, prods, start(t), c, masked), nxt

        accs = update(tuple(jnp.zeros((tq, dq), F32) for _ in range(g)), products(L, cc), L, cc, False)
        accs, _ = lax.fori_loop(0, n, step, (accs, products(start(0), c)))
        for hh in range(g):
            dq_ref[:, hh * dq:(hh + 1) * dq] = accs[hh] * cfg.scale

        @pl.when(i == nq - 1)
        def _():
            dk_ref[...] = dk_ref[...] * cfg.scale

    col = pl.BlockSpec((g, tq, 1), lambda h, i: (h, i, 0))
    return pl.pallas_call(
        body, name=name, grid=(hk, nq),
        in_specs=[pl.BlockSpec((tq, g * dq), lambda h, i: (i, h)),
                  pl.BlockSpec((T, dq), lambda h, i: (0, h)),
                  pl.BlockSpec((T, dv), lambda h, i: (0, h)),
                  pl.BlockSpec((tq, g * dv), lambda h, i: (i, h)), col, col],
        out_specs=[pl.BlockSpec((tq, g * dq), lambda h, i: (i, h)),
                   pl.BlockSpec((T, dq), lambda h, i: (0, h)),
                   pl.BlockSpec((T, dv), lambda h, i: (0, h))],
        out_shape=[jax.ShapeDtypeStruct((T, hq * dq), F32), jax.ShapeDtypeStruct((T, hk * dq), F32),
                   jax.ShapeDtypeStruct((T, hk * dv), F32)],
        compiler_params=_cparams(("arbitrary", "arbitrary")),
    )(q, k, v, dob, lse, delta)


HALO = 8


def _conv_specs(tm, C, T):
    nb = tm // HALO
    last = T // HALO - 1
    return [pl.BlockSpec((HALO, C), lambda i: (jnp.maximum(i * nb - 1, 0), 0)),
            pl.BlockSpec((tm, C), lambda i: (i, 0)),
            pl.BlockSpec((HALO, C), lambda i: (jnp.minimum((i + 1) * nb, last), 0))]


def _extended(prev_ref, cur_ref, next_ref, i, tm, L, T):
    r0 = i * tm
    keep_prev = jnp.logical_and(r0 != 0, r0 != L).astype(F32)
    keep_next = jnp.logical_and(r0 + tm != L, r0 + tm != T).astype(F32)
    return jnp.concatenate([prev_ref[...] * keep_prev, cur_ref[...], next_ref[...] * keep_next], axis=0)


def _shift_rows(xe, d):
    n = xe.shape[0]
    return xe if d == 0 else pltpu.roll(xe, (-d) % n, 0)


def _conv_pre(xe, w_ref, b_ref):
    acc = b_ref[...] + w_ref[SSM_CONV // 2:SSM_CONV // 2 + 1, :] * xe
    for k in range(SSM_CONV):
        if k != SSM_CONV // 2:
            acc = acc + w_ref[k:k + 1, :] * _shift_rows(xe, k - SSM_CONV // 2)
    return acc


def conv_fwd(name, x, w, b, L, tm=ROW_TILE):
    T, C = x.shape

    def body(xp, xc, xn, w_ref, b_ref, o_ref):
        xe = _extended(xp, xc, xn, pl.program_id(0), tm, L, T)
        pre = _conv_pre(xe, w_ref, b_ref)[HALO:HALO + tm]
        o_ref[...] = pre * jax.nn.sigmoid(pre)

    full = lambda a: pl.BlockSpec(a.shape, lambda i: (0, 0))
    return pl.pallas_call(body, name=name, grid=(T // tm,), in_specs=_conv_specs(tm, C, T) + [full(w), full(b)],
                          out_specs=pl.BlockSpec((tm, C), lambda i: (i, 0)), out_shape=jax.ShapeDtypeStruct((T, C), F32),
                          compiler_params=_cparams(("parallel",)))(x, x, x, w, b)


def conv_bwd(name, x, w, b, gu, L, tm=ROW_TILE):
    T, C = x.shape

    def body(xp, xc, xn, gp, gc, gn, w_ref, b_ref, dx_ref, dw_ref, db_ref):
        i = pl.program_id(0)
        xe = _extended(xp, xc, xn, i, tm, L, T)
        ge = _extended(gp, gc, gn, i, tm, L, T)
        pre = _conv_pre(xe, w_ref, b_ref)
        sg = jax.nn.sigmoid(pre)
        gpre = ge * (sg * (1.0 + pre * (1.0 - sg)))
        half = SSM_CONV // 2
        dx = jnp.zeros((tm, C), F32)
        rows = []
        for k in range(SSM_CONV):
            dx = dx + w_ref[k:k + 1, :] * _shift_rows(gpre, half - k)[HALO:HALO + tm]
            rows.append(jnp.sum(gpre[HALO:HALO + tm] * _shift_rows(xe, k - half)[HALO:HALO + tm], axis=0, keepdims=True))
        dx_ref[...] = dx
        rows += [jnp.zeros((1, C), F32)] * (8 - SSM_CONV)
        _accumulate(dw_ref, jnp.concatenate(rows, axis=0), i == 0)
        _accumulate(db_ref, jnp.sum(gpre[HALO:HALO + tm], axis=0, keepdims=True), i == 0)

    full = lambda a: pl.BlockSpec(a.shape, lambda i: (0, 0))
    return pl.pallas_call(
        body, name=name, grid=(T // tm,), in_specs=_conv_specs(tm, C, T) * 2 + [full(w), full(b)],
        out_specs=[pl.BlockSpec((tm, C), lambda i: (i, 0)), pl.BlockSpec((8, C), lambda i: (0, 0)), pl.BlockSpec((1, C), lambda i: (0, 0))],
        out_shape=[jax.ShapeDtypeStruct((T, C), F32), jax.ShapeDtypeStruct((8, C), F32), jax.ShapeDtypeStruct((1, C), F32)],
        compiler_params=_cparams(("arbitrary",)))(x, x, x, gu, gu, gu, w, b)


SSM_PAIRS = SSM_HEADS // 2
TN_DIMS = (((0,), (0,)), ((), ()))
HIGHEST = lax.Precision.HIGHEST


def _ssd_chunk(direction, xps, bs, cs, dt_col, dt_row, alog_row, alog_col, hps):
    Q = SSM_CHUNK
    da_col = dt_col * (-jnp.exp(alog_row))
    da_row = dt_row * (-jnp.exp(alog_col))
    ii = lax.broadcasted_iota(jnp.int32, (Q, Q), 0)
    jj = lax.broadcasted_iota(jnp.int32, (Q, Q), 1)
    tri = (ii >= jj) if direction == 0 else (ii <= jj)
    trif = tri.astype(F32)
    acs_col = jnp.dot(trif, da_col, precision=HIGHEST, preferred_element_type=F32)
    acs_row = lax.dot_general(da_row, trif, NT_DIMS, precision=HIGHEST, preferred_element_type=F32)
    tot_col = jnp.sum(da_col, axis=0, keepdims=True)
    lane16 = lax.broadcasted_iota(jnp.int32, (1, SSM_HEADS), 1)
    sub16 = lax.broadcasted_iota(jnp.int32, (SSM_HEADS, 1), 0)
    low = lax.broadcasted_iota(jnp.int32, (1, 2 * SSM_HEAD_DIM), 1) < SSM_HEAD_DIM

    def col(v, h):
        return jnp.sum(v * (lane16 == h).astype(F32), axis=1, keepdims=True)

    def row(v, h):
        return jnp.sum(v * (sub16 == h).astype(F32), axis=0, keepdims=True)

    ys, hos = [], []
    pairs_per_group = SSM_PAIRS // SSM_GROUPS
    for g in range(SSM_GROUPS):
        bb, cb16 = bs[g].astype(BF16), cs[g].astype(BF16)
        cb = lax.dot_general(cb16, bb, NT_DIMS, preferred_element_type=F32)
        for pp in range(pairs_per_group):
            p = g * pairs_per_group + pp
            h0, h1 = 2 * p, 2 * p + 1
            ac0, ac1 = col(acs_col, h0), col(acs_col, h1)
            seg0 = jnp.exp(jnp.where(tri, ac0 - row(acs_row, h0), -jnp.inf))
            seg1 = jnp.exp(jnp.where(tri, ac1 - row(acs_row, h1), -jnp.inf))
            dt_l = jnp.where(low, col(dt_col, h0), col(dt_col, h1))
            ac_l = jnp.where(low, ac0, ac1)
            tot_l = jnp.where(low, col(tot_col, h0), col(tot_col, h1))
            xdt = xps[p] * dt_l
            y = (jnp.dot((cb * seg0).astype(BF16), jnp.where(low, xdt, 0.0).astype(BF16), preferred_element_type=F32)
                 + jnp.dot((cb * seg1).astype(BF16), jnp.where(low, 0.0, xdt).astype(BF16), preferred_element_type=F32))
            y = y + jnp.dot(cb16, hps[p].astype(BF16), preferred_element_type=F32) * jnp.exp(ac_l)
            st = lax.dot_general(bb, (xdt * jnp.exp(tot_l - ac_l)).astype(BF16), TN_DIMS, preferred_element_type=F32)
            ys.append(y)
            hos.append(hps[p] * jnp.exp(tot_l) + st)
    return tuple(ys), tuple(hos)


def _ssd_chunk_of(direction, step, ncl, ncc):
    if direction == 0:
        return jnp.where(step < ncc, ncl + step, step - ncc)
    return jnp.where(step < ncc, ncl + ncc - 1 - step, ncl - 1 - (step - ncc))


def _ssd_load(u_ref):
    Q = SSM_CHUNK
    xps = tuple(u_ref[:, LANES * p:LANES * (p + 1)] for p in range(SSM_PAIRS))
    bs = tuple(u_ref[:, SSM_INNER + SSM_STATE * g:SSM_INNER + SSM_STATE * (g + 1)] for g in range(SSM_GROUPS))
    c0 = SSM_INNER + SSM_GROUPS * SSM_STATE
    cs = tuple(u_ref[:, c0 + SSM_STATE * g:c0 + SSM_STATE * (g + 1)] for g in range(SSM_GROUPS))
    return xps, bs, cs


def ssd_fwd(name, direction, u, dt, dt_t, alog_row, alog_col, L):
    T = u.shape[0]
    Q, N = SSM_CHUNK, SSM_STATE
    ncl, ncc = L // Q, (T - L) // Q
    nc = ncl + ncc
    cm = lambda s: _ssd_chunk_of(direction, s, ncl, ncc)

    def body(u_ref, dt_ref, dtt_ref, ar_ref, ac_ref, y_ref, hin_ref, state):
        @pl.when(pl.program_id(0) == 0)
        def _():
            state[...] = jnp.zeros_like(state)

        xps, bs, cs = _ssd_load(u_ref)
        hps = tuple(state[p] for p in range(SSM_PAIRS))
        for p in range(SSM_PAIRS):
            hin_ref[0, p] = hps[p]
        ys, hos = _ssd_chunk(direction, xps, bs, cs, dt_ref[...], dtt_ref[...], ar_ref[...], ac_ref[...], hps)
        for p in range(SSM_PAIRS):
            y_ref[:, LANES * p:LANES * (p + 1)] = ys[p]
            state[p] = hos[p]

    return pl.pallas_call(
        body, name=name, grid=(nc,),
        in_specs=[pl.BlockSpec((Q, SSM_CONV_DIM), lambda s: (cm(s), 0)),
                  pl.BlockSpec((Q, SSM_HEADS), lambda s: (cm(s), 0)),
                  pl.BlockSpec((SSM_HEADS, Q), lambda s: (0, cm(s))),
                  pl.BlockSpec((1, SSM_HEADS), lambda s: (0, 0)),
                  pl.BlockSpec((SSM_HEADS, 1), lambda s: (0, 0))],
        out_specs=[pl.BlockSpec((Q, SSM_INNER), lambda s: (cm(s), 0)),
                   pl.BlockSpec((1, SSM_PAIRS, N, LANES), lambda s: (cm(s), 0, 0, 0))],
        out_shape=[jax.ShapeDtypeStruct((T, SSM_INNER), F32), jax.ShapeDtypeStruct((nc, SSM_PAIRS, N, LANES), F32)],
        scratch_shapes=[pltpu.VMEM((SSM_PAIRS, N, LANES), F32)],
        compiler_params=_cparams(("arbitrary",)),
    )(u, dt, dt_t, alog_row, alog_col)


def ssd_bwd(name, direction, u, dt, dt_t, alog_row, alog_col, hin, dy, L, add_x=None, add_u=None):
    T = u.shape[0]
    Q, N = SSM_CHUNK, SSM_STATE
    ncl, ncc = L // Q, (T - L) // Q
    nc = ncl + ncc
    cm = lambda s: _ssd_chunk_of(direction, nc - 1 - s, ncl, ncc)
    n_add = (add_x is not None) + (add_u is not None)

    def body(*refs):
        u_ref, dt_ref, dtt_ref, ar_ref, ac_ref, hin_ref, dy_ref = refs[:7]
        add_refs = refs[7:7 + n_add]
        du_ref, ddt_ref, ddtt_ref, dar_ref, dac_ref, dstate = refs[7 + n_add:]
        first = pl.program_id(0) == 0

        @pl.when(first)
        def _():
            dstate[...] = jnp.zeros_like(dstate)

        xps, bs, cs = _ssd_load(u_ref)
        hps = tuple(hin_ref[0, p] for p in range(SSM_PAIRS))
        _, vjp = jax.vjp(functools.partial(_ssd_chunk, direction), xps, bs, cs, dt_ref[...], dtt_ref[...], ar_ref[...],
                         ac_ref[...], hps)
        dys = tuple(dy_ref[:, LANES * p:LANES * (p + 1)] for p in range(SSM_PAIRS))
        dhs = tuple(dstate[p] for p in range(SSM_PAIRS))
        gx, gb, gc, gdt, gdtt, gar, gac, ghp = vjp((dys, dhs))
        parts = list(gx) + list(gb) + list(gc)
        du = jnp.concatenate(parts, axis=1)
        k = 0
        if add_x is not None:
            du = du + jnp.concatenate([add_refs[k][...], jnp.zeros((Q, SSM_CONV_DIM - SSM_INNER), F32)], axis=1)
            k += 1
        if add_u is not None:
            du = du + add_refs[k][...]
        du_ref[...] = du
        ddt_ref[...] = gdt
        ddtt_ref[...] = gdtt
        _accumulate(dar_ref, gar, first)
        _accumulate(dac_ref, gac, first)
        for p in range(SSM_PAIRS):
            dstate[p] = ghp[p]

    in_specs = [pl.BlockSpec((Q, SSM_CONV_DIM), lambda s: (cm(s), 0)),
                pl.BlockSpec((Q, SSM_HEADS), lambda s: (cm(s), 0)),
                pl.BlockSpec((SSM_HEADS, Q), lambda s: (0, cm(s))),
                pl.BlockSpec((1, SSM_HEADS), lambda s: (0, 0)),
                pl.BlockSpec((SSM_HEADS, 1), lambda s: (0, 0)),
                pl.BlockSpec((1, SSM_PAIRS, N, LANES), lambda s: (cm(s), 0, 0, 0)),
                pl.BlockSpec((Q, SSM_INNER), lambda s: (cm(s), 0))]
    ins = [u, dt, dt_t, alog_row, alog_col, hin, dy]
    if add_x is not None:
        in_specs.append(pl.BlockSpec((Q, SSM_INNER), lambda s: (cm(s), 0)))
        ins.append(add_x)
    if add_u is not None:
        in_specs.append(pl.BlockSpec((Q, SSM_CONV_DIM), lambda s: (cm(s), 0)))
        ins.append(add_u)
    return pl.pallas_call(
        body, name=name, grid=(nc,), in_specs=in_specs,
        out_specs=[pl.BlockSpec((Q, SSM_CONV_DIM), lambda s: (cm(s), 0)),
                   pl.BlockSpec((Q, SSM_HEADS), lambda s: (cm(s), 0)),
                   pl.BlockSpec((SSM_HEADS, Q), lambda s: (0, cm(s))),
                   pl.BlockSpec((1, SSM_HEADS), lambda s: (0, 0)),
                   pl.BlockSpec((SSM_HEADS, 1), lambda s: (0, 0))],
        out_shape=[jax.ShapeDtypeStruct((T, SSM_CONV_DIM), F32), jax.ShapeDtypeStruct((T, SSM_HEADS), F32),
                   jax.ShapeDtypeStruct((SSM_HEADS, T), F32), jax.ShapeDtypeStruct((1, SSM_HEADS), F32),
                   jax.ShapeDtypeStruct((SSM_HEADS, 1), F32)],
        scratch_shapes=[pltpu.VMEM((SSM_PAIRS, N, LANES), F32)],
        compiler_params=_cparams(("arbitrary",)),
    )(*ins)


PEER_MASKS = (1, 2, 4, 3, 5, 6, 7)
N_PEERS = len(PEER_MASKS)
MESH_IDS = pl.DeviceIdType.MESH


def _my_index():
    return lax.axis_index("x") * 4 + lax.axis_index("y") * 2 + lax.axis_index("c")


def _coords(idx):
    return (idx // 4, (idx // 2) % 2, idx % 2)


def all_gather_hbm(name, arrays):
    n = len(arrays)
    chip_masks = (4, 2, 6)

    def body(*refs):
        ins, outs = refs[:n], refs[n:2 * n]
        send_sems, recv_sems, local_sems = refs[2 * n:]
        me = _my_index()
        sibling = me ^ 1

        def copy(a, k, block, to, src=None):
            return pltpu.make_async_remote_copy(
                src_ref=outs[a].at[block] if src is None else src, dst_ref=outs[a].at[block],
                send_sem=send_sems.at[a * N_PEERS + k], recv_sem=recv_sems.at[a * N_PEERS + k],
                device_id=_coords(to), device_id_type=MESH_IDS)

        started, own = [], []
        for a in range(n):
            local = pltpu.make_async_copy(ins[a], outs[a].at[me], local_sems.at[a])
            local.start()
            own.append(local)
            first = [copy(a, 0, me, sibling, src=ins[a])] + [copy(a, 1 + j, me, me ^ m, src=ins[a]) for j, m in enumerate(chip_masks)]
            for cp in first:
                cp.start()
            started += first
        for a in range(n):
            for j, m in enumerate(chip_masks):
                copy(a, 1 + j, me ^ m, me).wait_recv()
                fwd = copy(a, 4 + j, me ^ m, sibling)
                fwd.start()
                started.append(fwd)
        for a in range(n):
            copy(a, 0, sibling, me).wait_recv()
            for j, m in enumerate(chip_masks):
                copy(a, 4 + j, sibling ^ m, me).wait_recv()
        for cp in started:
            cp.wait_send()
        for cp in own:
            cp.wait()

    any_spec = pl.BlockSpec(memory_space=pl.ANY)
    return pl.pallas_call(
        body, name=name, in_specs=[any_spec] * n, out_specs=[any_spec] * n,
        out_shape=[jax.ShapeDtypeStruct((N_DEV,) + a.shape, a.dtype) for a in arrays],
        scratch_shapes=[pltpu.SemaphoreType.DMA((n * N_PEERS,)), pltpu.SemaphoreType.DMA((n * N_PEERS,)),
                        pltpu.SemaphoreType.DMA((n,))],
    )(*arrays)


def exchange_hbm(name, arrays):
    n = len(arrays)

    def body(*refs):
        ins, outs = refs[:n], refs[n:2 * n]
        send_sems, recv_sems, local_sems = refs[2 * n:]
        me = _my_index()
        copies = []
        for a in range(n):
            local = pltpu.make_async_copy(ins[a].at[me], outs[a].at[me], local_sems.at[a])
            local.start()
            copies.append(local)
            for k, mask in enumerate(PEER_MASKS):
                peer = me ^ mask
                cp = pltpu.make_async_remote_copy(src_ref=ins[a].at[peer], dst_ref=outs[a].at[me],
                                                  send_sem=send_sems.at[a * N_PEERS + k], recv_sem=recv_sems.at[a * N_PEERS + k],
                                                  device_id=_coords(peer), device_id_type=MESH_IDS)
                cp.start()
                copies.append(cp)
        for cp in copies:
            cp.wait()

    any_spec = pl.BlockSpec(memory_space=pl.ANY)
    return pl.pallas_call(
        body, name=name, in_specs=[any_spec] * n, out_specs=[any_spec] * n,
        out_shape=[jax.ShapeDtypeStruct(a.shape, a.dtype) for a in arrays],
        scratch_shapes=[pltpu.SemaphoreType.DMA((n * N_PEERS,)), pltpu.SemaphoreType.DMA((n * N_PEERS,)),
                        pltpu.SemaphoreType.DMA((n,))],
    )(*arrays)


def all_gather_vmem(name, v):
    def body(v_ref, out_ref, send_sems, recv_sems):
        me = _my_index()
        out_ref[me] = v_ref[...]
        copies = []
        for k, mask in enumerate(PEER_MASKS):
            cp = pltpu.make_async_remote_copy(src_ref=v_ref, dst_ref=out_ref.at[me], send_sem=send_sems.at[k],
                                              recv_sem=recv_sems.at[k], device_id=_coords(me ^ mask), device_id_type=MESH_IDS)
            cp.start()
            copies.append(cp)
        for cp in copies:
            cp.wait()

    vm = pl.BlockSpec(memory_space=pltpu.VMEM)
    return pl.pallas_call(
        body, name=name, in_specs=[vm], out_specs=vm, out_shape=jax.ShapeDtypeStruct((N_DEV,) + v.shape, v.dtype),
        scratch_shapes=[pltpu.SemaphoreType.DMA((N_PEERS,)), pltpu.SemaphoreType.DMA((N_PEERS,))],
    )(v)


def _row_tile(rows, cols, bufs):
    budget = 24 * 1024 * 1024 // (bufs * 2 * 4 * max(cols, LANES))
    if rows <= budget:
        return rows
    for t in range(budget - budget % 16, 15, -16):
        if rows % t == 0:
            return t
    return rows


def sum_parts(name, parts):
    P, R, C = parts.shape
    tr = _row_tile(R, C, P + 1)

    def body(p_ref, o_ref):
        acc = p_ref[0].astype(F32)
        for s in range(1, P):
            acc = acc + p_ref[s].astype(F32)
        o_ref[...] = acc

    return pl.pallas_call(body, name=name, grid=(R // tr,), in_specs=[pl.BlockSpec((P, tr, C), lambda i: (0, i, 0))],
                          out_specs=pl.BlockSpec((tr, C), lambda i: (i, 0)), out_shape=jax.ShapeDtypeStruct((R, C), F32),
                          compiler_params=_cparams(("parallel",)))(parts)


def cast_bf16(name, x):
    R, C = x.shape
    tr = _row_tile(R, C, 2)

    def body(x_ref, o_ref):
        o_ref[...] = x_ref[...].astype(BF16)

    spec = pl.BlockSpec((tr, C), lambda i: (i, 0))
    return pl.pallas_call(body, name=name, grid=(R // tr,), in_specs=[spec], out_specs=spec,
                          out_shape=jax.ShapeDtypeStruct((R, C), BF16), compiler_params=_cparams(("parallel",)))(x)


def adamw(name, w, g, m, v):
    R, C = w.shape
    tr = _row_tile(R, C, 7)

    def body(w_ref, g_ref, m_ref, v_ref, d_ref, nm_ref, nv_ref):
        g = g_ref[...]
        nm = ADAM_B1 * m_ref[...] + (1.0 - ADAM_B1) * g
        nv = ADAM_B2 * v_ref[...] + (1.0 - ADAM_B2) * (g * g)
        m_hat = nm / (1.0 - ADAM_B1 ** ADAM_STEP)
        v_hat = nv / (1.0 - ADAM_B2 ** ADAM_STEP)
        d_ref[...] = -ADAM_LR * (m_hat / (jnp.sqrt(v_hat) + ADAM_EPS) + ADAM_WD * w_ref[...])
        nm_ref[...] = nm
        nv_ref[...] = nv

    spec = pl.BlockSpec((tr, C), lambda i: (i, 0))
    return pl.pallas_call(body, name=name, grid=(R // tr,), in_specs=[spec] * 4, out_specs=[spec] * 3,
                          out_shape=[jax.ShapeDtypeStruct((R, C), F32)] * 3, compiler_params=_cparams(("parallel",)))(w, g, m, v)


def loss_and_grad(name, x, target, L, tm=ROW_TILE):
    T, D = x.shape
    nlt = L // tm

    def body(x_ref, t_ref, loss_ref, dx_ref):
        i = pl.program_id(0)
        err = jnp.where(i < nlt, x_ref[...] - t_ref[...], 0.0)
        dx_ref[...] = err * (1.0 / D)
        part = 0.5 * jnp.sum(jnp.sum(err * err, axis=1, keepdims=True), axis=0, keepdims=True) * (1.0 / D)
        _accumulate(loss_ref, part, i == 0)

    return pl.pallas_call(
        body, name=name, grid=(T // tm,),
        in_specs=[pl.BlockSpec((tm, D), lambda i: (i, 0)), pl.BlockSpec((tm, D), lambda i: (jnp.minimum(i, nlt - 1), 0))],
        out_specs=[pl.BlockSpec((1, 1), lambda i: (0, 0)), pl.BlockSpec((tm, D), lambda i: (i, 0))],
        out_shape=[jax.ShapeDtypeStruct((1, 1), F32), jax.ShapeDtypeStruct((T, D), F32)],
        compiler_params=_cparams(("arbitrary",)))(x, target)


def small_fwd(name, fn, arrays, out_shapes):
    def body(*refs):
        res = fn(*[r[...] for r in refs[:len(arrays)]])
        for o_ref, r in zip(refs[len(arrays):], res):
            o_ref[...] = r

    return pl.pallas_call(body, name=name, out_shape=[jax.ShapeDtypeStruct(s, F32) for s in out_shapes])(*arrays)


def small_bwd(name, fn, arrays, cts):
    n = len(arrays)

    def body(*refs):
        _, vjp = jax.vjp(lambda *a: tuple(fn(*a)), *[r[...] for r in refs[:n]])
        grads = vjp(tuple(r[...] for r in refs[n:n + len(cts)]))
        for o_ref, g in zip(refs[n + len(cts):], grads):
            o_ref[...] = g

    return pl.pallas_call(body, name=name, out_shape=[jax.ShapeDtypeStruct(a.shape, F32) for a in arrays])(*arrays, *cts)


def fn_silu(x):
    return (x * jax.nn.sigmoid(x),)


FWD_NAMES = ["x", "c", "ctx", "c_ctx", "w_mod", "b_mod", "norm1_g", "norm2_g", "w_in", "ssm_conv_w", "ssm_conv_b",
             "ssm_dt_bias", "ssm_a_log", "ssm_d", "ssm_norm_g", "swa_q_norm_g", "swa_k_norm_g", "swa_sink", "mla_q_lat_g",
             "mla_kv_lat_g", "w_mla_uq", "w_mla_ukv", "mla_q_norm_g", "mla_k_norm_g", "w_p_ssm", "w_p_swa", "w_p_mla",
             "w_out", "w_ffn_in", "w_ffn_out"]
WEIGHT_NAMES = FWD_NAMES[3:]
GATHERED = ["w_in", "w_mla_uq", "w_mla_ukv", "w_p_ssm", "w_p_swa", "w_p_mla", "w_out", "w_ffn_in", "w_ffn_out"]
COLUMN_SHARDED = ("w_in", "w_mla_uq", "w_mla_ukv", "w_ffn_in")
REPLICATED = ["c_ctx", "b_mod", "norm1_g", "norm2_g", "ssm_conv_b", "ssm_dt_bias", "ssm_a_log", "ssm_d", "ssm_norm_g",
              "swa_q_norm_g", "swa_k_norm_g", "swa_sink", "mla_q_lat_g", "mla_kv_lat_g", "mla_q_norm_g", "mla_k_norm_g"]
IN_SEGS = [("xbc", SSM_CONV_DIM), ("dt", 2 * SSM_HEADS), ("ks", SWA_KV_HEADS * SWA_HEAD_DIM), ("vs", SWA_KV_HEADS * SWA_HEAD_DIM),
           ("ckv", MLA_KV_RANK), ("kr", MLA_ROPE), ("z", SSM_INNER), ("qs", SWA_Q_HEADS * SWA_HEAD_DIM), ("cq", MLA_Q_RANK),
           ("g1", None), ("g2", None), ("g3", None)]


def _pack(vectors, multiple):
    flat = jnp.concatenate([v.reshape(-1) for v in vectors])
    pad = (-flat.shape[0]) % multiple
    return jnp.pad(flat, (0, pad)).reshape(-1, LANES)


def _unpack(packed, shapes):
    flat, out, off = packed.reshape(-1), [], 0
    for s in shapes:
        n = int(np.prod(s))
        out.append(flat[off:off + n].reshape(s))
        off += n
    return out


def _rope_tables(L, T, rot_dim):
    nf = rot_dim // 4
    inv = jnp.power(ROPE_BASE, -jnp.arange(nf, dtype=F32) / nf)
    r, col = jnp.meshgrid(jnp.arange(L // GRID_W, dtype=F32), jnp.arange(GRID_W, dtype=F32), indexing="ij")
    ang = jnp.stack([r.reshape(-1)[:, None] * inv, col.reshape(-1)[:, None] * inv], axis=1)
    cos, sin = jnp.cos(ang), jnp.sin(ang)
    c = jnp.concatenate([cos[:, 0], cos[:, 0], cos[:, 1], cos[:, 1]], axis=1)
    s = jnp.concatenate([-sin[:, 0], sin[:, 0], -sin[:, 1], sin[:, 1]], axis=1)
    c = jnp.pad(c, ((0, T - L), (0, LANES - rot_dim)), constant_values=1.0)
    s = jnp.pad(s, ((0, T - L), (0, LANES - rot_dim)))
    return c, s


def _pad_rows(a, rows):
    return jnp.pad(a, ((0, rows - a.shape[0]), (0, 0)))


def _row(w, per_head=0, off=0, diff=True):
    return ("row", w, per_head, off, diff)


PAR, PAR_ND = ("par", True), ("par", False)
GRP = ("grp", True)


def kernel(*args):
    n_fwd, n_w = len(FWD_NAMES), len(WEIGHT_NAMES)
    inp = dict(zip(FWD_NAMES, args[:n_fwd]))
    loss_target = args[n_fwd]
    mom_m = dict(zip(WEIGHT_NAMES, args[n_fwd + 1:n_fwd + 1 + n_w]))
    mom_v = dict(zip(WEIGHT_NAMES, args[n_fwd + 1 + n_w:]))

    x, ctx = inp["x"][0], inp["ctx"][0]
    L, D = x.shape
    n_ctx = ctx.shape[0]
    T = L + n_ctx
    depth = inp["w_in"].shape[0]
    me = _my_index()
    in_widths = [w if w is not None else D for _, w in IN_SEGS]
    in_offs = np.concatenate([[0], np.cumsum(in_widths)]).tolist()
    ffn_h = inp["w_ffn_out"].shape[1] * N_DEV
    cfg_swa = AttnCfg(SWA_Q_HEADS, SWA_Q_HEADS // SWA_KV_HEADS, SWA_HEAD_DIM, SWA_HEAD_DIM, SWA_HEAD_DIM ** -0.5, SWA_WINDOW,
                      True, L, T, 256, 1)
    cfg_mla = AttnCfg(MLA_HEADS, 1, MLA_QK_PAD, MLA_V, MLA_QK ** -0.5, None, False, L, T, 1024, 2)
    cfg_mla_bwd = AttnCfg(MLA_HEADS, 1, MLA_QK_PAD, MLA_V, MLA_QK ** -0.5, None, False, L, T, 2048, 1)

    def rf(name, fn, descs, arrays, outs, heads=1):
        return rowop_fwd(name, fn, descs, arrays, outs, T, L, heads=heads)

    def rb(name, fn, descs, arrays, outs, cts, heads=1, add=None):
        return rowop_bwd(name, fn, descs, arrays, outs, cts, T, L, heads=heads, add=add)

    local = []
    for n in GATHERED:
        w = inp[n]
        local.append((jnp.swapaxes(w, 1, 2) if n in COLUMN_SHARDED else w).astype(BF16))
    gathered = dict(zip(GATHERED, all_gather_hbm("gather_weights", local)))

    def full(n, l):
        g = gathered[n][:, l]
        return g.reshape(g.shape[0] * g.shape[1], g.shape[2])

    def layer_weights(l):
        wt = {}
        w_in_t = full("w_in", l)
        for (sn, _), o, w in zip(IN_SEGS, in_offs, in_widths):
            seg = w_in_t[o:o + w]
            wt[sn] = _pad_rows(seg, LANES) if sn == "kr" else seg
        uq = full("w_mla_uq", l).reshape(MLA_HEADS, MLA_QK, MLA_Q_RANK)
        wt["uqn"] = uq[:, :MLA_NOPE].reshape(MLA_HEADS * MLA_NOPE, MLA_Q_RANK)
        wt["uqr"] = jnp.pad(uq[:, MLA_NOPE:], ((0, 0), (0, LANES - MLA_ROPE), (0, 0))).reshape(MLA_HEADS * LANES, MLA_Q_RANK)
        ukv = full("w_mla_ukv", l).reshape(MLA_HEADS, MLA_NOPE + MLA_V, MLA_KV_RANK)
        wt["uk"] = ukv[:, :MLA_NOPE].reshape(MLA_HEADS * MLA_NOPE, MLA_KV_RANK)
        wt["uv"] = ukv[:, MLA_NOPE:].reshape(MLA_HEADS * MLA_V, MLA_KV_RANK)
        for n in ("w_p_ssm", "w_p_swa", "w_p_mla", "w_out", "w_ffn_out"):
            wt[n] = full(n, l)
        ffn_in_t = full("w_ffn_in", l)
        wt["fg"], wt["fu"] = ffn_in_t[:ffn_h], ffn_in_t[ffn_h:]
        return wt

    def layer_params(l):
        p = {}
        for n in ("norm1_g", "norm2_g", "ssm_conv_b", "ssm_norm_g", "swa_q_norm_g", "swa_k_norm_g", "mla_q_lat_g", "mla_kv_lat_g"):
            p[n] = inp[n][l][None]
        p["dt_bias"] = inp["ssm_dt_bias"][l].reshape(1, 2 * SSM_HEADS)
        p["alog_row"] = [inp["ssm_a_log"][l][d][None] for d in range(2)]
        p["alog_col"] = [inp["ssm_a_log"][l][d][:, None] for d in range(2)]
        p["d_lane"] = jnp.repeat(inp["ssm_d"][l], SSM_HEAD_DIM)[None]
        p["sink"] = inp["swa_sink"][l].reshape(SWA_Q_HEADS, 1, 1)
        for n, key in (("mla_q_norm_g", "gq"), ("mla_k_norm_g", "gk")):
            g = inp[n][l]
            p[key + "n"] = g[:MLA_NOPE][None]
            p[key + "r"] = jnp.pad(g[MLA_NOPE:], (0, LANES - MLA_ROPE))[None]
        return p

    conv_local = _pack([inp["ssm_conv_w"]], 8 * LANES)
    conv_all = all_gather_vmem("gather_conv_w", conv_local)
    cw = inp["ssm_conv_w"].shape
    conv_full = conv_all.reshape(N_DEV, -1)[:, :cw[0] * cw[1] * cw[2]].reshape(N_DEV, cw[0], cw[1], cw[2])
    conv_full = jnp.moveaxis(conv_full, 0, 2).reshape(cw[0], cw[1], N_DEV * cw[2])
    conv_w8 = jnp.pad(conv_full, ((0, 0), (0, 8 - cw[1]), (0, 0)))

    silu_c, silu_cc = small_fwd("silu_c", lambda a, b: fn_silu(a) + fn_silu(b), [inp["c"], inp["c_ctx"][None]], [(1, D), (1, D)])
    silu_all = all_gather_vmem("gather_silu_c", silu_c.reshape(D // LANES, LANES)).reshape(N_DEV, D)
    S_rows = 2 * N_DEV
    S_mat = jnp.concatenate([silu_all, silu_cc, jnp.zeros((S_rows - N_DEV - 1, D), F32)], axis=0)
    mod_cols = inp["w_mod"].shape[2]
    mods_local = []
    for l in range(depth):
        bias = lax.dynamic_slice(inp["b_mod"][l], (me * mod_cols,), (mod_cols,))
        mods_local.append(matmul(S_mat, inp["w_mod"][l], "nn", f"mod{l}", add=jnp.broadcast_to(bias[None], (S_rows, mod_cols))))
    mods_all = all_gather_vmem("gather_mods", jnp.stack(mods_local).reshape(-1, LANES))
    mods_all = jnp.moveaxis(mods_all.reshape(N_DEV, depth, S_rows, mod_cols), 0, 2).reshape(depth, S_rows, N_DEV * mod_cols)
    mods_lat = lax.dynamic_slice(mods_all, (0, me, 0), (depth, 1, N_DEV * mod_cols))[:, 0]
    mods_ctx = mods_all[:, N_DEV]

    def layer_mods(l):
        return [jnp.stack([mods_lat[l, j * D:(j + 1) * D], mods_ctx[l, j * D:(j + 1) * D]])[:, None] for j in range(6)]

    cs_swa = _rope_tables(L, T, SWA_HEAD_DIM)
    cs_mla = _rope_tables(L, T, MLA_ROPE)
    nm_descs = [_row(D), PAR, GRP, GRP]
    resid_descs = [_row(D, diff=False), _row(D), GRP]
    tab = [_row(LANES, diff=False), _row(LANES, diff=False)]
    swaq_descs = [_row(SWA_HEAD_DIM, 1), PAR] + tab
    swakv_descs = [_row(SWA_HEAD_DIM, 1), _row(SWA_HEAD_DIM, 1), PAR] + tab
    mlaq_descs = [_row(LANES, 1), _row(LANES, 1), PAR, PAR] + tab
    mlakv_descs = [_row(LANES, 1), _row(LANES, 1), _row(LANES), PAR, PAR] + tab
    ssdout_descs = [_row(SSM_INNER), _row(SSM_INNER, diff=False), _row(SSM_INNER), _row(SSM_INNER), PAR, PAR]
    merge_descs = [_row(D)] * 6
    swiglu_descs = [_row(ffn_h), _row(ffn_h)]
    seg_names = [sn for sn, _ in IN_SEGS]
    seg_groups = [seg_names[:7], seg_names[7:]]

    def layer_fwd(l, X, wt, p, mods):
        sh1, sc1, gt1, sh2, sc2, gt2 = mods
        r = {"X": X}
        r["h1"] = rf(f"l{l}_norm1", fn_norm_mod, nm_descs, [X, p["norm1_g"], sh1, sc1], [(D, 0, BF16)])[0]
        for gi, group in enumerate(seg_groups):
            r.update(zip(group, matmul_multi(f"l{l}_in{gi}", r["h1"], [wt[sn] for sn in group])))
        r["u"] = conv_fwd(f"l{l}_conv", r["xbc"], conv_w8[l], p["ssm_conv_b"], L)
        r["dts"] = rf(f"l{l}_softplus", fn_softplus, [_row(2 * SSM_HEADS), PAR], [r["dt"], p["dt_bias"]], [(2 * SSM_HEADS, 0, F32)])[0]
        for d in range(2):
            dt_d = r["dts"][:, d * SSM_HEADS:(d + 1) * SSM_HEADS]
            r[f"dt{d}"], r[f"dtt{d}"] = dt_d, dt_d.T
            r[f"y{d}"], r[f"hin{d}"] = ssd_fwd(f"l{l}_ssd{d}", d, r["u"], dt_d, dt_d.T, p["alog_row"][d], p["alog_col"][d], L)
        r["ys"] = rf(f"l{l}_ssd_out", fn_ssd_out, ssdout_descs, [r["y0"], r["y1"], r["u"], r["z"], p["d_lane"], p["ssm_norm_g"]],
                     [(SSM_INNER, 0, F32)])[0]
        r["Qs"] = rf(f"l{l}_swa_q", fn_swa_q, swaq_descs, [r["qs"], p["swa_q_norm_g"], *cs_swa], [(SWA_HEAD_DIM, 1, BF16)], SWA_Q_HEADS)[0]
        r["Ks"], r["Vs"] = rf(f"l{l}_swa_kv", fn_swa_kv, swakv_descs, [r["ks"], r["vs"], p["swa_k_norm_g"], *cs_swa],
                              [(SWA_HEAD_DIM, 1, BF16), (SWA_HEAD_DIM, 1, BF16)], SWA_KV_HEADS)
        r["Os"], r["lse_s"] = flash_fwd(f"l{l}_swa_fwd", cfg_swa, r["Qs"], r["Ks"], r["Vs"], p["sink"])
        r["cqn"] = rf(f"l{l}_q_lat", fn_rms, [_row(MLA_Q_RANK), PAR], [r["cq"], p["mla_q_lat_g"]], [(MLA_Q_RANK, 0, BF16)])[0]
        r["qn"], r["qr"] = matmul_multi(f"l{l}_uq", r["cqn"], [wt["uqn"], wt["uqr"]])
        r["Qm"] = rf(f"l{l}_mla_q", fn_mla_q, mlaq_descs, [r["qn"], r["qr"], p["gqn"], p["gqr"], *cs_mla], [(MLA_QK_PAD, 1, BF16)], MLA_HEADS)[0]
        r["ckvn"] = rf(f"l{l}_kv_lat", fn_rms, [_row(MLA_KV_RANK), PAR], [r["ckv"], p["mla_kv_lat_g"]], [(MLA_KV_RANK, 0, BF16)])[0]
        r["kn"], r["vp"] = matmul_multi(f"l{l}_ukv", r["ckvn"], [wt["uk"], wt["uv"]])
        r["Km"], r["Vm"] = rf(f"l{l}_mla_kv", fn_mla_kv, mlakv_descs, [r["kn"], r["vp"], r["kr"], p["gkn"], p["gkr"], *cs_mla],
                              [(MLA_QK_PAD, 1, BF16), (MLA_V, 1, BF16)], MLA_HEADS)
        r["Om"], r["lse_m"] = flash_fwd(f"l{l}_mla_fwd", cfg_mla, r["Qm"], r["Km"], r["Vm"], None)
        r["P1"] = matmul(r["ys"], wt["w_p_ssm"], "nn", f"l{l}_p_ssm")
        r["P2"] = matmul(r["Os"], wt["w_p_swa"], "nn", f"l{l}_p_swa")
        r["P3"] = matmul(r["Om"], wt["w_p_mla"], "nn", f"l{l}_p_mla")
        r["mg"] = rf(f"l{l}_merge", fn_merge, merge_descs, [r["g1"], r["g2"], r["g3"], r["P1"], r["P2"], r["P3"]], [(D, 0, BF16)])[0]
        r["A"] = matmul(r["mg"], wt["w_out"], "nn", f"l{l}_out")
        r["X1"] = rf(f"l{l}_resid1", fn_resid, resid_descs, [X, r["A"], gt1], [(D, 0, F32)])[0]
        r["h2"] = rf(f"l{l}_norm2", fn_norm_mod, nm_descs, [r["X1"], p["norm2_g"], sh2, sc2], [(D, 0, BF16)])[0]
        r["Fg"] = matmul(r["h2"], wt["fg"], "nt", f"l{l}_ffn_g")
        r["Fu"] = matmul(r["h2"], wt["fu"], "nt", f"l{l}_ffn_u")
        r["sg"] = rf(f"l{l}_swiglu", fn_swiglu, swiglu_descs, [r["Fg"], r["Fu"]], [(ffn_h, 0, BF16)])[0]
        r["B"] = matmul(r["sg"], wt["w_ffn_out"], "nn", f"l{l}_ffn_out")
        X2 = rf(f"l{l}_resid2", fn_resid, resid_descs, [r["X1"], r["B"], gt2], [(D, 0, F32)])[0]
        return X2, r

    def attn_bwd(tag, cfg, q, k, v, o, lse, do, sink):
        res = attn_delta(f"{tag}_delta", cfg, o, do, lse, sink)
        delta, dob = res[0], res[1]
        dsink = res[2] if cfg.has_sink else None
        dq, dk, dv = flash_bwd_fused(f"{tag}_bwd", cfg, q, k, v, dob, lse, delta)
        return dq, dk, dv, dsink

    def layer_bwd(l, dX2, r, wt, p, mods):
        sh1, sc1, gt1, sh2, sc2, gt2 = mods
        g, gw = {}, {}
        dmod = [None] * 6
        dB, dmod[5] = rb(f"l{l}_resid2_b", fn_resid, resid_descs, [r["X1"], r["B"], gt2], [(D, 0, F32)], [dX2])
        dsg = matmul(dB, wt["w_ffn_out"], "nt", f"l{l}_ffn_out_da")
        gw["w_ffn_out"] = matmul(r["sg"], dB, "tn", f"l{l}_ffn_out_dw")
        dFg, dFu = rb(f"l{l}_swiglu_b", fn_swiglu, swiglu_descs, [r["Fg"], r["Fu"]], [(ffn_h, 0, BF16)], [dsg])
        dh2 = matmul(dFg, wt["fg"], "nn", f"l{l}_ffn_g_da")
        dh2 = matmul(dFu, wt["fu"], "nn", f"l{l}_ffn_u_da", add=dh2)
        gw["w_ffn_in"] = jnp.concatenate([matmul(r["h2"], dFg, "tn", f"l{l}_ffn_g_dw"), matmul(r["h2"], dFu, "tn", f"l{l}_ffn_u_dw")], axis=1)
        dX1, g["norm2_g"], dmod[3], dmod[4] = rb(f"l{l}_norm2_b", fn_norm_mod, nm_descs, [r["X1"], p["norm2_g"], sh2, sc2],
                                                [(D, 0, BF16)], [dh2], add={0: dX2})
        dA, dmod[2] = rb(f"l{l}_resid1_b", fn_resid, resid_descs, [r["X"], r["A"], gt1], [(D, 0, F32)], [dX1])
        dmg = matmul(dA, wt["w_out"], "nt", f"l{l}_out_da")
        gw["w_out"] = matmul(r["mg"], dA, "tn", f"l{l}_out_dw")
        dsegs = {}
        dsegs["g1"], dsegs["g2"], dsegs["g3"], dP1, dP2, dP3 = rb(
            f"l{l}_merge_b", fn_merge, merge_descs, [r["g1"], r["g2"], r["g3"], r["P1"], r["P2"], r["P3"]], [(D, 0, BF16)], [dmg])
        dys = matmul(dP1, wt["w_p_ssm"], "nt", f"l{l}_p_ssm_da")
        dOs = matmul(dP2, wt["w_p_swa"], "nt", f"l{l}_p_swa_da")
        dOm = matmul(dP3, wt["w_p_mla"], "nt", f"l{l}_p_mla_da")
        gw["w_p_ssm"] = matmul(r["ys"], dP1, "tn", f"l{l}_p_ssm_dw")
        gw["w_p_swa"] = matmul(r["Os"], dP2, "tn", f"l{l}_p_swa_dw")
        gw["w_p_mla"] = matmul(r["Om"], dP3, "tn", f"l{l}_p_mla_dw")
        dQm, dKm, dVm, _ = attn_bwd(f"l{l}_mla", cfg_mla_bwd, r["Qm"], r["Km"], r["Vm"], r["Om"], r["lse_m"], dOm, None)
        dkn, dvp, dsegs["kr"], dgkn, dgkr = rb(f"l{l}_mla_kv_b", fn_mla_kv, mlakv_descs,
                                               [r["kn"], r["vp"], r["kr"], p["gkn"], p["gkr"], *cs_mla],
                                               [(MLA_QK_PAD, 1, BF16), (MLA_V, 1, BF16)], [dKm, dVm], MLA_HEADS)
        dckvn = matmul_sum(f"l{l}_ukv_da", [(dkn, wt["uk"]), (dvp, wt["uv"])])
        dw_uk = matmul(r["ckvn"], dkn, "tn", f"l{l}_uk_dw").reshape(MLA_KV_RANK, MLA_HEADS, MLA_NOPE)
        dw_uv = matmul(r["ckvn"], dvp, "tn", f"l{l}_uv_dw").reshape(MLA_KV_RANK, MLA_HEADS, MLA_V)
        gw["w_mla_ukv"] = jnp.concatenate([dw_uk, dw_uv], axis=2).reshape(MLA_KV_RANK, -1)
        dsegs["ckv"], g["mla_kv_lat_g"] = rb(f"l{l}_kv_lat_b", fn_rms, [_row(MLA_KV_RANK), PAR], [r["ckv"], p["mla_kv_lat_g"]],
                                             [(MLA_KV_RANK, 0, BF16)], [dckvn])
        dqn, dqr, dgqn, dgqr = rb(f"l{l}_mla_q_b", fn_mla_q, mlaq_descs, [r["qn"], r["qr"], p["gqn"], p["gqr"], *cs_mla],
                                  [(MLA_QK_PAD, 1, BF16)], [dQm], MLA_HEADS)
        dcqn = matmul_sum(f"l{l}_uq_da", [(dqn, wt["uqn"]), (dqr, wt["uqr"])])
        dw_uqn = matmul(r["cqn"], dqn, "tn", f"l{l}_uqn_dw").reshape(MLA_Q_RANK, MLA_HEADS, MLA_NOPE)
        dw_uqr = matmul(r["cqn"], dqr, "tn", f"l{l}_uqr_dw").reshape(MLA_Q_RANK, MLA_HEADS, LANES)[:, :, :MLA_ROPE]
        gw["w_mla_uq"] = jnp.concatenate([dw_uqn, dw_uqr], axis=2).reshape(MLA_Q_RANK, -1)
        dsegs["cq"], g["mla_q_lat_g"] = rb(f"l{l}_q_lat_b", fn_rms, [_row(MLA_Q_RANK), PAR], [r["cq"], p["mla_q_lat_g"]],
                                           [(MLA_Q_RANK, 0, BF16)], [dcqn])
        g["mla_q_norm_g"] = jnp.concatenate([dgqn[0], dgqr[0, :MLA_ROPE]])
        g["mla_k_norm_g"] = jnp.concatenate([dgkn[0], dgkr[0, :MLA_ROPE]])
        dQs, dKs, dVs, dsink = attn_bwd(f"l{l}_swa", cfg_swa, r["Qs"], r["Ks"], r["Vs"], r["Os"], r["lse_s"], dOs, p["sink"])
        g["swa_sink"] = dsink.reshape(SWA_Q_HEADS)
        dsegs["qs"], g["swa_q_norm_g"] = rb(f"l{l}_swa_q_b", fn_swa_q, swaq_descs, [r["qs"], p["swa_q_norm_g"], *cs_swa],
                                            [(SWA_HEAD_DIM, 1, BF16)], [dQs], SWA_Q_HEADS)
        dsegs["ks"], dsegs["vs"], g["swa_k_norm_g"] = rb(f"l{l}_swa_kv_b", fn_swa_kv, swakv_descs,
                                                         [r["ks"], r["vs"], p["swa_k_norm_g"], *cs_swa],
                                                         [(SWA_HEAD_DIM, 1, BF16), (SWA_HEAD_DIM, 1, BF16)], [dKs, dVs], SWA_KV_HEADS)
        dy, dxs, dsegs["z"], dd_lane, g["ssm_norm_g"] = rb(
            f"l{l}_ssd_out_b", fn_ssd_out, ssdout_descs, [r["y0"], r["y1"], r["u"], r["z"], p["d_lane"], p["ssm_norm_g"]],
            [(SSM_INNER, 0, F32)], [dys])
        g["ssm_d"] = dd_lane.reshape(SSM_HEADS, SSM_HEAD_DIM).sum(axis=1)
        du, ddts, dalog = None, [], []
        for d in range(2):
            du, ddt, ddtt, dar, dac = ssd_bwd(f"l{l}_ssd{d}_b", d, r["u"], r[f"dt{d}"], r[f"dtt{d}"], p["alog_row"][d], p["alog_col"][d],
                                              r[f"hin{d}"], dy, L, add_x=dxs if d == 0 else None, add_u=du)
            ddts.append(ddt + ddtt.T)
            dalog.append(dar[0] + dac[:, 0])
        g["ssm_a_log"] = jnp.stack(dalog)
        dsegs["xbc"], dconv_w, g["ssm_conv_b"] = conv_bwd(f"l{l}_conv_b", r["xbc"], conv_w8[l], p["ssm_conv_b"], du, L)
        dsegs["dt"], ddt_bias = rb(f"l{l}_softplus_b", fn_softplus, [_row(2 * SSM_HEADS), PAR], [r["dt"], p["dt_bias"]],
                                   [(2 * SSM_HEADS, 0, F32)], [jnp.concatenate(ddts, axis=1)])
        g["ssm_dt_bias"] = ddt_bias.reshape(2, SSM_HEADS)
        g["ssm_conv_w"] = dconv_w[:SSM_CONV]
        dh1, dws = None, []
        for gi, group in enumerate(seg_groups):
            dh1 = matmul_sum(f"l{l}_in_da{gi}", [(dsegs[sn], wt[sn]) for sn in group], add=dh1)
        for sn, w in zip(seg_names, in_widths):
            dws.append(matmul(r["h1"], dsegs[sn], "tn", f"l{l}_in_{sn}_dw")[:, :w])
        gw["w_in"] = jnp.concatenate(dws, axis=1)
        dX, g["norm1_g"], dmod[0], dmod[1] = rb(f"l{l}_norm1_b", fn_norm_mod, nm_descs, [r["X"], p["norm1_g"], sh1, sc1],
                                               [(D, 0, BF16)], [dh1], add={0: dX1})
        for n in ("norm1_g", "norm2_g", "ssm_conv_b", "ssm_norm_g", "swa_q_norm_g", "swa_k_norm_g", "mla_q_lat_g", "mla_kv_lat_g"):
            g[n] = g[n][0]
        dmod_lat = jnp.concatenate([dm[0, 0] for dm in dmod])
        dmod_ctx = jnp.concatenate([dm[1, 0] for dm in dmod])
        return dX, g, gw, dmod_lat, dmod_ctx

    X = jnp.concatenate([x, ctx], axis=0)
    saved = []
    for l in range(depth):
        wt, p, mods = layer_weights(l), layer_params(l), layer_mods(l)
        X, r = layer_fwd(l, X, wt, p, mods)
        saved.append((r, wt, p, mods))
    loss_part, dX = loss_and_grad("loss", X, loss_target[0], L)
    loss = lax.psum(loss_part[0, 0], ("x", "y", "c"))
    small_g = [None] * depth
    big_g = [None] * depth
    dmods = [None] * depth
    for l in reversed(range(depth)):
        r, wt, p, mods = saved[l]
        dX, small_g[l], big_g[l], dm_lat, dm_ctx = layer_bwd(l, dX, r, wt, p, mods)
        dmods[l] = jnp.stack([dm_lat, dm_ctx])
    grad_x = dX[:L][None]

    dm_all = all_gather_vmem("gather_dmods", jnp.stack(dmods).reshape(-1, LANES)).reshape(N_DEV, depth, 2, N_DEV * mod_cols)
    dm_rows = jnp.concatenate([jnp.moveaxis(dm_all[:, :, 0], 0, 1), dm_all[:, :, 1].sum(axis=0)[:, None],
                               jnp.zeros((depth, S_rows - N_DEV - 1, N_DEV * mod_cols), F32)], axis=1)
    dm_mine = lax.dynamic_slice(dm_rows, (0, 0, me * mod_cols), (depth, S_rows, mod_cols))
    grads = {}
    grads["w_mod"] = jnp.stack([matmul(S_mat, dm_mine[l], "tn", f"mod{l}_dw") for l in range(depth)])
    d_silu = None
    for l in range(depth):
        d_silu = matmul(dm_mine[l], inp["w_mod"][l], "nt", f"mod{l}_da", add=d_silu)
    small = {n: jnp.stack([small_g[l][n] for l in range(depth)]) for n in small_g[0]}
    small["c_ctx"] = small_bwd("silu_c_b", fn_silu, [inp["c_ctx"][None]], [d_silu[N_DEV:N_DEV + 1]])[0][0]
    small["b_mod"] = jnp.stack(dmods).sum(axis=1)

    rep_shapes = [inp[n].shape for n in REPLICATED]
    conv_shape = (depth, SSM_CONV, SSM_CONV_DIM)
    packed = _pack([small[n] for n in REPLICATED] + [small["ssm_conv_w"]], 8 * LANES)
    small_sum = sum_parts("sum_small", all_gather_vmem("gather_small", packed))
    summed = _unpack(small_sum, rep_shapes + [conv_shape])
    for n, gsum in zip(REPLICATED, summed):
        grads[n] = gsum
    grads["ssm_conv_w"] = lax.dynamic_slice(summed[-1], (0, 0, me * cw[2]), cw)

    slabs = []
    for n in GATHERED:
        gfull = jnp.stack([big_g[l][n] for l in range(depth)])
        if n in COLUMN_SHARDED:
            k_dim, n_dim = gfull.shape[1], gfull.shape[2]
            slab = jnp.moveaxis(gfull.reshape(depth, k_dim, N_DEV, n_dim // N_DEV), 2, 0)
        else:
            k_dim, n_dim = gfull.shape[1], gfull.shape[2]
            slab = jnp.moveaxis(gfull.reshape(depth, N_DEV, k_dim // N_DEV, n_dim), 1, 0)
        slabs.append(cast_bf16(f"cast_{n}", slab.reshape(-1, slab.shape[-1])).reshape(slab.shape))
    for n, parts in zip(GATHERED, exchange_hbm("exchange_grads", slabs)):
        shp = inp[n].shape
        grads[n] = sum_parts(f"sum_{n}", parts.reshape(N_DEV, shp[0] * shp[1], shp[2])).reshape(shp)

    delta, new_m, new_v = {}, {}, {}
    rep_pack = lambda d: _pack([d[n] for n in REPLICATED], 8 * LANES)
    rep_out = adamw("adamw_small", rep_pack(inp), rep_pack(grads), rep_pack(mom_m), rep_pack(mom_v))
    for out, res in zip((delta, new_m, new_v), rep_out):
        for n, a in zip(REPLICATED, _unpack(res, rep_shapes)):
            out[n] = a
    for n in ["w_mod", "ssm_conv_w"] + GATHERED:
        shp = inp[n].shape
        two_d = (shp[0] * shp[1], shp[2])
        res = adamw(f"adamw_{n}", inp[n].reshape(two_d), grads[n].reshape(two_d), mom_m[n].reshape(two_d), mom_v[n].reshape(two_d))
        delta[n], new_m[n], new_v[n] = [a.reshape(shp) for a in res]

    return (loss, grad_x, *[grads[n] for n in WEIGHT_NAMES], *[delta[n] for n in WEIGHT_NAMES],
            *[new_m[n] for n in WEIGHT_NAMES], *[new_v[n] for n in WEIGHT_NAMES])
```

```python
import functools
import math

import numpy as np
import jax
import jax.numpy as jnp
from jax import lax
from jax.experimental import pallas as pl
from jax.experimental.pallas import tpu as pltpu

F32 = jnp.float32
BF16 = jnp.bfloat16

N_DEV = 8
V7X_VMEM_BYTES = 64 * 1024 * 1024
VMEM_LIMIT_BYTES = V7X_VMEM_BYTES - 8 * 1024 * 1024
LANES = 128

EPS = 1e-6
ROPE_BASE = 10000.0
GRID_W = 64
SSM_HEADS, SSM_HEAD_DIM, SSM_GROUPS, SSM_STATE, SSM_CONV, SSM_CHUNK = 16, 64, 2, 128, 5, 128
SSM_INNER = SSM_HEADS * SSM_HEAD_DIM
SSM_CONV_DIM = SSM_INNER + 2 * SSM_GROUPS * SSM_STATE
SWA_Q_HEADS, SWA_KV_HEADS, SWA_HEAD_DIM, SWA_WINDOW = 8, 2, 128, 128
MLA_HEADS, MLA_Q_RANK, MLA_KV_RANK, MLA_NOPE, MLA_ROPE, MLA_V = 8, 384, 256, 128, 64, 128
MLA_QK = MLA_NOPE + MLA_ROPE
MLA_QK_PAD = 2 * LANES
ADAM_LR, ADAM_B1, ADAM_B2, ADAM_EPS, ADAM_WD, ADAM_STEP = 0.001, 0.9, 0.999, 1e-08, 0.01, 10

ROW_TILE = 256


def _cparams(sem, **kw):
    return pltpu.CompilerParams(dimension_semantics=sem, vmem_limit_bytes=VMEM_LIMIT_BYTES, **kw)


def _pick(dim, prefs):
    for p in prefs:
        if dim % p == 0:
            return p
    return dim


def matmul(a, b, mode, name, out_dtype=F32, add=None):
    if mode == "nn":
        (M, K), (K2, N) = a.shape, b.shape
    elif mode == "nt":
        (M, K), (N, K2) = a.shape, b.shape
    else:
        (K, M), (K2, N) = a.shape, b.shape
    assert K == K2, (name, a.shape, b.shape)
    has_add = add is not None
    tm, tn, tk = _matmul_tiles(M, N, K, a.dtype.itemsize, b.dtype.itemsize, jnp.dtype(out_dtype).itemsize, has_add,
                                   m_on_lanes=(mode == "tn"))
    nk = K // tk
    dims = {"nn": (((1,), (0,)), ((), ())), "nt": (((1,), (1,)), ((), ())), "tn": (((0,), (0,)), ((), ()))}[mode]
    a_spec = pl.BlockSpec((tk, tm), lambda i, j, k: (k, i)) if mode == "tn" else pl.BlockSpec((tm, tk), lambda i, j, k: (i, k))
    b_spec = pl.BlockSpec((tn, tk), lambda i, j, k: (j, k)) if mode == "nt" else pl.BlockSpec((tk, tn), lambda i, j, k: (k, j))
    o_spec = pl.BlockSpec((tm, tn), lambda i, j, k: (i, j))

    def body(*refs):
        a_ref, b_ref = refs[:2]
        c_ref = refs[2] if has_add else None
        o_ref = refs[3] if has_add else refs[2]
        part = lax.dot_general(a_ref[...].astype(BF16), b_ref[...].astype(BF16), dims, preferred_element_type=F32)
        if nk == 1:
            o_ref[...] = (part + c_ref[...] if has_add else part).astype(o_ref.dtype)
            return
        acc_ref = refs[-1]
        k = pl.program_id(2)

        @pl.when(k == 0)
        def _():
            acc_ref[...] = part + c_ref[...] if has_add else part

        @pl.when(k > 0)
        def _():
            acc_ref[...] += part

        @pl.when(k == nk - 1)
        def _():
            o_ref[...] = acc_ref[...].astype(o_ref.dtype)

    ins = [a, b] + ([add] if has_add else [])
    in_specs = [a_spec, b_spec] + ([o_spec] if has_add else [])
    return pl.pallas_call(
        body, name=name, grid=(M // tm, N // tn, nk), in_specs=in_specs, out_specs=o_spec,
        out_shape=jax.ShapeDtypeStruct((M, N), out_dtype),
        scratch_shapes=[pltpu.VMEM((tm, tn), F32)] if nk > 1 else [],
        input_output_aliases=({2: 0} if has_add else {}),
        compiler_params=_cparams(("parallel", "parallel", "arbitrary")),
    )(*ins)


def matmul_sum(name, pairs, add=None):
    M, N = pairs[0][0].shape[0], pairs[0][1].shape[1]
    n = len(pairs)
    has_add = add is not None
    resident = sum(2 * b.shape[0] * N * b.dtype.itemsize for _, b in pairs)
    tm = next((t for t in (768, 512, 384, 256, 128) if M % t == 0 and resident + sum(
        2 * t * a.shape[1] * a.dtype.itemsize + t * a.shape[1] * 2 for a, _ in pairs) + 6 * t * N * 4 <= MATMUL_VMEM_BUDGET), None)
    assert tm is not None, name

    def body(*refs):
        acc = refs[2 * n][...] if has_add else None
        for s in range(n):
            part = jnp.dot(refs[2 * s][...].astype(BF16), refs[2 * s + 1][...].astype(BF16), preferred_element_type=F32)
            acc = part if acc is None else acc + part
        refs[-1][...] = acc

    o_spec = pl.BlockSpec((tm, N), lambda i: (i, 0))
    in_specs, ins = [], []
    for a, b in pairs:
        in_specs += [pl.BlockSpec((tm, a.shape[1]), lambda i: (i, 0)), pl.BlockSpec(b.shape, lambda i: (0, 0))]
        ins += [a, b]
    if has_add:
        in_specs.append(o_spec)
        ins.append(add)
    return pl.pallas_call(body, name=name, grid=(M // tm,), in_specs=in_specs, out_specs=o_spec,
                          out_shape=jax.ShapeDtypeStruct((M, N), F32), input_output_aliases=({2 * n: 0} if has_add else {}),
                          compiler_params=_cparams(("parallel",)))(*ins)


def matmul_multi(name, a, bs):
    M, K = a.shape
    n = len(bs)
    resident = sum(2 * b.shape[0] * K * b.dtype.itemsize for b in bs)
    n_total = sum(b.shape[0] for b in bs)
    tm = next((t for t in (768, 512, 384, 256, 128) if M % t == 0 and
               resident + 2 * t * K * a.dtype.itemsize + 3 * t * n_total * 4 <= MATMUL_VMEM_BUDGET), None)
    assert tm is not None, name

    def body(*refs):
        lhs = refs[0][...].astype(BF16)
        for s in range(n):
            refs[1 + n + s][...] = lax.dot_general(lhs, refs[1 + s][...].astype(BF16), NT_DIMS, preferred_element_type=F32)

    in_specs = [pl.BlockSpec((tm, K), lambda i: (i, 0))] + [pl.BlockSpec(b.shape, lambda i: (0, 0)) for b in bs]
    return pl.pallas_call(
        body, name=name, grid=(M // tm,), in_specs=in_specs,
        out_specs=[pl.BlockSpec((tm, b.shape[0]), lambda i: (i, 0)) for b in bs],
        out_shape=[jax.ShapeDtypeStruct((M, b.shape[0]), F32) for b in bs], compiler_params=_cparams(("parallel",)))(a, *bs)


MATMUL_VMEM_BUDGET = 36 * 1024 * 1024


def _matmul_tiles(M, N, K, a_bytes, b_bytes, o_bytes, has_add, m_on_lanes=False):
    tk = K if K <= 1536 else _pick(K, (1408, 1024, 768, 704, 512, 256))
    nk = K // tk
    m_cands = [t for t in (1024, 768, 512, 384, 256, 128) if M % t == 0] or [M]
    if M % 768 and M % 1024:
        m_cands += [t for t in (1408, 704, 352) if M % t == 0 and not (m_on_lanes and t % LANES)]
    n_cands = [t for t in range(LANES, min(N, 2816) + 1, LANES) if N % t == 0] or [N]
    best = None
    for tm in m_cands:
        for tn in n_cands:
            pipeline = 2 * (tm * tk * a_bytes + tk * tn * b_bytes + tm * tn * o_bytes) + (2 * tm * tn * 4 if has_add else 0)
            temps = tm * tn * 4 * (2 if nk > 1 else 1) + (tm * tk * 2 if a_bytes == 4 else 0) + (tk * tn * 2 if b_bytes == 4 else 0)
            if pipeline + temps <= MATMUL_VMEM_BUDGET:
                score = (tm * tn, tn)
                if best is None or score > best[0]:
                    best = (score, tm, tn)
    if best is None:
        return m_cands[-1], n_cands[0], tk
    return best[1], best[2], tk


def _row_specs(descs, arrays, tm, nct, heads):
    specs = []
    for d, arr in zip(descs, arrays):
        if d[0] == "row":
            _, w, per_head, off, _ = d
            specs.append(pl.BlockSpec((tm, w * (heads if per_head else 1)), lambda i, off=off: (i, off)))
        elif d[0] == "par":
            specs.append(pl.BlockSpec(arr.shape, lambda i, nd=arr.ndim: (0,) * nd))
        else:
            specs.append(pl.BlockSpec((1,) + arr.shape[1:], lambda i, nd=arr.ndim: (jnp.where(i >= nct, 1, 0),) + (0,) * (nd - 1)))
    return specs


def _load(d, ref, h):
    if d[0] == "grp":
        return ref[0]
    if d[0] == "row" and d[2]:
        return ref[:, h * d[1]:(h + 1) * d[1]]
    return ref[...]


def _out_specs(outs, tm, heads):
    return [pl.BlockSpec((tm, w * (heads if ph else 1)), lambda i: (i, 0)) for (w, ph, _) in outs]


def rowop_fwd(name, fn, descs, arrays, outs, T, n_ctx, heads=1, tm=ROW_TILE):
    nct = n_ctx // tm
    n_in = len(descs)

    def body(*refs):
        for h in range(heads):
            res = fn(*[_load(d, r, h) for d, r in zip(descs, refs[:n_in])])
            for o_ref, r, (w, ph, _) in zip(refs[n_in:], res, outs):
                if ph:
                    o_ref[:, h * w:(h + 1) * w] = r.astype(o_ref.dtype)
                else:
                    o_ref[...] = r.astype(o_ref.dtype)

    out_shape = [jax.ShapeDtypeStruct((T, w * (heads if ph else 1)), dt) for (w, ph, dt) in outs]
    return pl.pallas_call(
        body, name=name, grid=(T // tm,), in_specs=_row_specs(descs, arrays, tm, nct, heads), out_specs=_out_specs(outs, tm, heads),
        out_shape=out_shape, compiler_params=_cparams(("parallel",)),
    )(*arrays)


def rowop_bwd(name, fn, descs, arrays, outs, cts, T, n_ctx, heads=1, tm=ROW_TILE, add=None):
    nct = n_ctx // tm
    n_in, n_ct = len(descs), len(cts)
    add = add or {}
    diff_idx = [k for k, d in enumerate(descs) if d[-1]]
    add_idx = [k for k in diff_idx if k in add]

    def body(*refs):
        in_refs, ct_refs = refs[:n_in], refs[n_in:n_in + n_ct]
        add_refs = dict(zip(add_idx, refs[n_in + n_ct:n_in + n_ct + len(add_idx)]))
        g_refs = refs[n_in + n_ct + len(add_idx):]
        i = pl.program_id(0)
        shared = {}
        for h in range(heads):
            vals = [_load(d, r, h) for d, r in zip(descs, in_refs)]

            def f(*dvals, vals=vals):
                full = list(vals)
                for k, v in zip(diff_idx, dvals):
                    full[k] = v
                return tuple(fn(*full))

            _, vjp = jax.vjp(f, *[vals[k] for k in diff_idx])
            cts_h = tuple(c[:, h * w:(h + 1) * w] if ph else c[...] for c, (w, ph, _) in zip(ct_refs, outs))
            for k, g_ref, g in zip(diff_idx, g_refs, vjp(cts_h)):
                d = descs[k]
                if d[0] == "row" and d[2]:
                    g_ref[:, h * d[1]:(h + 1) * d[1]] = g.astype(g_ref.dtype)
                else:
                    shared[k] = g if k not in shared else shared[k] + g
        for k, g_ref in zip(diff_idx, g_refs):
            d = descs[k]
            if k not in shared:
                continue
            g = shared[k]
            if d[0] == "row":
                if k in add_refs:
                    g = g + add_refs[k][...]
                g_ref[...] = g.astype(g_ref.dtype)
            elif d[0] == "par":
                _accumulate(g_ref, g, i == 0)
            else:
                _accumulate(g_ref, g[None], jnp.logical_or(i == 0, i == nct))

    in_specs = _row_specs(descs, arrays, tm, nct, heads)
    g_specs, g_shape = [], []
    for k in diff_idx:
        d = descs[k]
        if d[0] == "row":
            g_specs.append(pl.BlockSpec((tm, d[1] * (heads if d[2] else 1)), lambda i: (i, 0)))
            g_shape.append(jax.ShapeDtypeStruct((T, d[1] * (heads if d[2] else 1)), F32))
        else:
            g_specs.append(in_specs[k])
            g_shape.append(jax.ShapeDtypeStruct(arrays[k].shape, F32))
    add_specs = [g_specs[diff_idx.index(k)] for k in add_idx]
    return pl.pallas_call(
        body, name=name, grid=(T // tm,), in_specs=in_specs + _out_specs(outs, tm, heads) + add_specs, out_specs=g_specs,
        out_shape=g_shape, compiler_params=_cparams(("arbitrary",)),
    )(*arrays, *cts, *[add[k] for k in add_idx])


def _accumulate(ref, val, first):
    @pl.when(first)
    def _():
        ref[...] = val.astype(ref.dtype)

    @pl.when(jnp.logical_not(first))
    def _():
        ref[...] += val.astype(ref.dtype)


def _rms(x, count=None):
    n = x.shape[-1] if count is None else count
    return x * lax.rsqrt(jnp.sum(x * x, axis=-1, keepdims=True) * (1.0 / n) + EPS)


def _swap_halves(x, nf):
    w = x.shape[-1]
    lane = lax.broadcasted_iota(jnp.int32, x.shape, x.ndim - 1)
    return jnp.where((lane % (2 * nf)) < nf, pltpu.roll(x, w - nf, x.ndim - 1), pltpu.roll(x, nf, x.ndim - 1))


def _make_rope(nf):
    @jax.custom_vjp
    def rope(x, c, s):
        return x * c + _swap_halves(x, nf) * s

    def fwd(x, c, s):
        return rope(x, c, s), (c, s)

    def bwd(res, g):
        c, s = res
        return g * c + _swap_halves(g * s, nf), jnp.zeros_like(c), jnp.zeros_like(s)

    rope.defvjp(fwd, bwd)
    return rope


_rope_swa = _make_rope(SWA_HEAD_DIM // 4)
_rope_mla = _make_rope(MLA_ROPE // 4)


@jax.custom_vjp
def _softplus(x):
    e = jnp.exp(-jnp.abs(x))
    u = 1.0 + e
    log1p_e = jnp.where(u == 1.0, e, jnp.log(u) * e / jnp.where(u == 1.0, 1.0, u - 1.0))
    return jnp.maximum(x, 0.0) + log1p_e


_softplus.defvjp(lambda x: (_softplus(x), x), lambda x, g: (g * jax.nn.sigmoid(x),))


def fn_norm_mod(x, g, shift, scale):
    return (_rms(x) * g * (1.0 + scale) + shift,)


def fn_rms(x, g):
    return (_rms(x) * g,)


def fn_resid(x, a, gate):
    return (x + gate * a,)


def fn_softplus(dt, bias):
    return (_softplus(dt + bias),)


def fn_ssd_out(yf, yb, xs, z, d_lane, g):
    y = yf + yb + d_lane * xs
    return (_rms(y * (z * jax.nn.sigmoid(z))) * g,)


def fn_swa_q(q, g, c, s):
    return (_rope_swa(_rms(q) * g, c, s),)


def fn_swa_kv(k, v, g, c, s):
    return (_rope_swa(_rms(k) * g, c, s), v)


def fn_mla_q(qn, qr, gn, gr, c, s):
    return (jnp.concatenate([_rms(qn) * gn, _rope_mla(_rms(qr, MLA_ROPE) * gr, c, s)], axis=-1),)


def fn_mla_kv(kn, v, kr, gn, gr, c, s):
    return (jnp.concatenate([_rms(kn) * gn, _rope_mla(_rms(kr, MLA_ROPE) * gr, c, s)], axis=-1), v)


def fn_merge(g1, g2, g3, p1, p2, p3):
    return (jax.nn.sigmoid(g1) * p1 + jax.nn.sigmoid(g2) * p2 + jax.nn.sigmoid(g3) * p3,)


def fn_swiglu(g, u):
    return (g * jax.nn.sigmoid(g) * u,)


ATTN_TILE = 256
NT_DIMS = (((1,), (1,)), ((), ()))


class AttnCfg:
    def __init__(self, hq, group, dq, dv, scale, window, has_sink, L, T, chunk, kv_block):
        self.hq, self.group, self.dq, self.dv, self.scale = hq, group, dq, dv, scale
        self.window, self.has_sink, self.L, self.T = window, has_sink, L, T
        self.chunk = _pick(L, (chunk, ATTN_TILE))
        self.ctx_chunk = T - L
        self.kv_block = kv_block
        self.q_block = kv_block * group
        assert L % ATTN_TILE == 0 and (T - L) % ATTN_TILE == 0 and L % self.chunk == 0
        assert (hq // group) % kv_block == 0
        if window is not None:
            assert (ATTN_TILE + 2 * window) % self.chunk == 0
            self.window_chunks = min((ATTN_TILE + 2 * window) // self.chunk, L // self.chunk)
            self.align = math.gcd(self.chunk, window)
        else:
            self.align = self.chunk


LOG2E = math.log2(math.e)


def _latent_chunks(cfg, r0):
    c = cfg.chunk
    if cfg.window is None:
        lo, n = 0, cfg.L // c
    else:
        n = cfg.window_chunks
        lo = jnp.clip(r0 - cfg.window, 0, cfg.L - n * c)
    return lo, n


def _visible(cfg, rows_q, rows_k):
    return jnp.logical_or(rows_k >= cfg.L, jnp.abs(rows_k - rows_q) <= cfg.window)


def flash_fwd(name, cfg, q, k, v, sink):
    T, tq, c = cfg.T, ATTN_TILE, cfg.chunk
    hq, g, dq, dv, hb, kb = cfg.hq, cfg.group, cfg.dq, cfg.dv, cfg.q_block, cfg.kv_block
    to_log2 = cfg.scale * LOG2E

    def body(*refs):
        if cfg.has_sink:
            q_ref, k_ref, v_ref, sink_ref, o_ref, lse_ref = refs
        else:
            q_ref, k_ref, v_ref, o_ref, lse_ref = refs
        q0 = pl.program_id(1) * tq
        qs = [q_ref[:, hh * dq:(hh + 1) * dq] for hh in range(hb)]
        lat_lo, lat_n = _latent_chunks(cfg, q0)
        n = jnp.where(q0 >= cfg.L, 0, lat_n)
        rows_q = q0 + lax.broadcasted_iota(jnp.int32, (tq, 1), 0)

        def start(t):
            return pl.multiple_of(lat_lo + jnp.minimum(t, lat_n - 1) * c, cfg.align)

        def logits(ks, size):
            return tuple(lax.dot_general(qs[hh], k_ref[pl.ds(ks, size), (hh // g) * dq:(hh // g + 1) * dq], NT_DIMS,
                                         preferred_element_type=F32) for hh in range(hb))

        def update(state, s_all, ks, size, masked):
            new_state = []
            for hh in range(hb):
                m, acc = state[hh]
                s = s_all[hh]
                if masked:
                    rows_k = ks + lax.broadcasted_iota(jnp.int32, (1, size), 1)
                    s = jnp.where(_visible(cfg, rows_q, rows_k), s, -jnp.inf)
                m_new = jnp.maximum(m, jnp.max(s, axis=-1, keepdims=True) * to_log2)
                alpha = jnp.exp2(m - m_new)
                p = jnp.exp2(s * to_log2 - m_new).astype(BF16)
                kh = hh // g
                v_ones = jnp.concatenate([v_ref[pl.ds(ks, size), kh * dv:(kh + 1) * dv], jnp.ones((size, dv), BF16)], axis=1)
                acc = alpha * acc + jnp.dot(p, v_ones, preferred_element_type=F32)
                new_state.append((m_new, acc))
            return tuple(new_state)

        def step(t, carry):
            state, s_all = carry
            s_next = logits(start(t + 1), c)
            return update(state, s_all, start(t), c, cfg.window is not None), s_next

        state = []
        for hh in range(hb):
            if cfg.has_sink:
                m0 = jnp.zeros((tq, 1), F32) + sink_ref[hh] * LOG2E
                l0 = jnp.ones((tq, dv), F32)
            else:
                m0 = jnp.full((tq, 1), -jnp.inf, F32)
                l0 = jnp.zeros((tq, dv), F32)
            state.append((m0, jnp.concatenate([jnp.zeros((tq, dv), F32), l0], axis=1)))
        state = update(tuple(state), logits(cfg.L, cfg.ctx_chunk), cfg.L, cfg.ctx_chunk, False)
        state, _ = lax.fori_loop(0, n, step, (state, logits(start(0), c)))
        for hh in range(hb):
            m, acc = state[hh]
            o_ref[:, hh * dv:(hh + 1) * dv] = acc[:, :dv] / acc[:, dv:]
            lse_ref[hh] = m + jnp.log2(acc[:, dv:dv + 1])

    in_specs = [pl.BlockSpec((tq, hb * dq), lambda h, i: (i, h)),
                pl.BlockSpec((T, kb * dq), lambda h, i: (0, h)),
                pl.BlockSpec((T, kb * dv), lambda h, i: (0, h))]
    ins = [q, k, v]
    if cfg.has_sink:
        in_specs.append(pl.BlockSpec((hb, 1, 1), lambda h, i: (h, 0, 0)))
        ins.append(sink)
    return pl.pallas_call(
        body, name=name, grid=(hq // hb, T // tq), in_specs=in_specs,
        out_specs=[pl.BlockSpec((tq, hb * dv), lambda h, i: (i, h)), pl.BlockSpec((hb, tq, 1), lambda h, i: (h, i, 0))],
        out_shape=[jax.ShapeDtypeStruct((T, hq * dv), F32), jax.ShapeDtypeStruct((hq, T, 1), F32)],
        compiler_params=_cparams(("parallel", "parallel")),
    )(*ins)


def attn_delta(name, cfg, o, do, lse, sink):
    T, tm, hq, dv = cfg.T, ATTN_TILE, cfg.hq, cfg.dv

    def body(*refs):
        if cfg.has_sink:
            o_ref, do_ref, lse_ref, sink_ref, delta_ref, dob_ref, dsink_ref = refs
        else:
            o_ref, do_ref, delta_ref, dob_ref = refs
        dob_ref[...] = do_ref[...].astype(BF16)
        parts = []
        for h in range(hq):
            delta = jnp.sum(do_ref[:, h * dv:(h + 1) * dv] * o_ref[:, h * dv:(h + 1) * dv], axis=-1, keepdims=True)
            delta_ref[h] = delta
            if cfg.has_sink:
                parts.append(-jnp.sum(jnp.exp2(sink_ref[h] * LOG2E - lse_ref[h]) * delta, axis=0, keepdims=True)[None])
        if cfg.has_sink:
            _accumulate(dsink_ref, jnp.concatenate(parts, axis=0), pl.program_id(0) == 0)

    head_tile = pl.BlockSpec((tm, hq * dv), lambda i: (i, 0))
    col = pl.BlockSpec((hq, tm, 1), lambda i: (0, i, 0))
    one = pl.BlockSpec((hq, 1, 1), lambda i: (0, 0, 0))
    in_specs, ins = [head_tile, head_tile], [o, do]
    out_specs = [col, head_tile]
    out_shape = [jax.ShapeDtypeStruct((hq, T, 1), F32), jax.ShapeDtypeStruct((T, hq * dv), BF16)]
    if cfg.has_sink:
        in_specs += [col, one]
        ins += [lse, sink]
        out_specs.append(one)
        out_shape.append(jax.ShapeDtypeStruct((hq, 1, 1), F32))
    return pl.pallas_call(body, name=name, grid=(T // tm,), in_specs=in_specs, out_specs=out_specs, out_shape=out_shape,
                          compiler_params=_cparams(("arbitrary",)))(*ins)


def flash_bwd_fused(name, cfg, q, k, v, dob, lse, delta):
    T, L, tq, c, cc = cfg.T, cfg.L, ATTN_TILE, cfg.chunk, cfg.ctx_chunk
    hq, g, dq, dv = cfg.hq, cfg.group, cfg.dq, cfg.dv
    hk = hq // g
    nq = T // tq
    to_log2 = cfg.scale * LOG2E
    masked = cfg.window is not None

    def body(q_ref, k_ref, v_ref, do_ref, lse_ref, delta_ref, dq_ref, dk_ref, dv_ref):
        i = pl.program_id(1)
        q0 = i * tq

        @pl.when(i == 0)
        def _():
            dk_ref[...] = jnp.zeros_like(dk_ref)
            dv_ref[...] = jnp.zeros_like(dv_ref)

        qs = [q_ref[:, hh * dq:(hh + 1) * dq] for hh in range(g)]
        dos = [do_ref[:, hh * dv:(hh + 1) * dv] for hh in range(g)]
        lses = [lse_ref[hh] for hh in range(g)]
        deltas = [delta_ref[hh] for hh in range(g)]
        lat_lo, lat_n = _latent_chunks(cfg, q0)
        n = jnp.where(q0 >= L, 0, lat_n)
        rows_q = q0 + lax.broadcasted_iota(jnp.int32, (tq, 1), 0)

        def start(t):
            return pl.multiple_of(lat_lo + jnp.minimum(t, lat_n - 1) * c, cfg.align)

        def products(ks, size):
            kk, vv = k_ref[pl.ds(ks, size), :], v_ref[pl.ds(ks, size), :]
            return tuple((lax.dot_general(qs[hh], kk, NT_DIMS, preferred_element_type=F32),
                          lax.dot_general(dos[hh], vv, NT_DIMS, preferred_element_type=F32)) for hh in range(g))

        def update(accs, prods, ks, size, mask_it):
            new_accs, dv_part, dk_part = [], None, None
            for hh in range(g):
                s, dp = prods[hh]
                p = jnp.exp2(s * to_log2 - lses[hh])
                if mask_it:
                    rows_k = ks + lax.broadcasted_iota(jnp.int32, (1, size), 1)
                    p = jnp.where(_visible(cfg, rows_q, rows_k), p, 0.0)
                ds = (p * (dp - deltas[hh])).astype(BF16)
                dv_h = lax.dot_general(p.astype(BF16), dos[hh], TN_DIMS, preferred_element_type=F32)
                dk_h = lax.dot_general(ds, qs[hh], TN_DIMS, preferred_element_type=F32)
                dv_part = dv_h if dv_part is None else dv_part + dv_h
                dk_part = dk_h if dk_part is None else dk_part + dk_h
                new_accs.append(accs[hh] + jnp.dot(ds, k_ref[pl.ds(ks, size), :], preferred_element_type=F32))
            dv_ref[pl.ds(ks, size), :] += dv_part
            dk_ref[pl.ds(ks, size), :] += dk_part
            return tuple(new_accs)

        def step(t, carry):
            accs, prods = carry
            nxt = products(start(t + 1), c)
            return update(accs, prods, start(t), c, masked), nxt

        accs = update(tuple(jnp.zeros((tq, dq), F32) for _ in range(g)), products(L, cc), L, cc, False)
        accs, _ = lax.fori_loop(0, n, step, (accs, products(start(0), c)))
        for hh in range(g):
            dq_ref[:, hh * dq:(hh + 1) * dq] = accs[hh] * cfg.scale

        @pl.when(i == nq - 1)
        def _():
            dk_ref[...] = dk_ref[...] * cfg.scale

    col = pl.BlockSpec((g, tq, 1), lambda h, i: (h, i, 0))
    return pl.pallas_call(
        body, name=name, grid=(hk, nq),
        in_specs=[pl.BlockSpec((tq, g * dq), lambda h, i: (i, h)),
                  pl.BlockSpec((T, dq), lambda h, i: (0, h)),
                  pl.BlockSpec((T, dv), lambda h, i: (0, h)),
                  pl.BlockSpec((tq, g * dv), lambda h, i: (i, h)), col, col],
        out_specs=[pl.BlockSpec((tq, g * dq), lambda h, i: (i, h)),
                   pl.BlockSpec((T, dq), lambda h, i: (0, h)),
                   pl.BlockSpec((T, dv), lambda h, i: (0, h))],
        out_shape=[jax.ShapeDtypeStruct((T, hq * dq), F32), jax.ShapeDtypeStruct((T, hk * dq), F32),
                   jax.ShapeDtypeStruct((T, hk * dv), F32)],
        compiler_params=_cparams(("arbitrary", "arbitrary")),
    )(q, k, v, dob, lse, delta)


HALO = 8


def _conv_specs(tm, C, T):
    nb = tm // HALO
    last = T // HALO - 1
    return [pl.BlockSpec((HALO, C), lambda i: (jnp.maximum(i * nb - 1, 0), 0)),
            pl.BlockSpec((tm, C), lambda i: (i, 0)),
            pl.BlockSpec((HALO, C), lambda i: (jnp.minimum((i + 1) * nb, last), 0))]


def _extended(prev_ref, cur_ref, next_ref, i, tm, L, T):
    r0 = i * tm
    keep_prev = jnp.logical_and(r0 != 0, r0 != L).astype(F32)
    keep_next = jnp.logical_and(r0 + tm != L, r0 + tm != T).astype(F32)
    return jnp.concatenate([prev_ref[...] * keep_prev, cur_ref[...], next_ref[...] * keep_next], axis=0)


def _shift_rows(xe, d):
    n = xe.shape[0]
    return xe if d == 0 else pltpu.roll(xe, (-d) % n, 0)


def _conv_pre(xe, w_ref, b_ref):
    acc = b_ref[...] + w_ref[SSM_CONV // 2:SSM_CONV // 2 + 1, :] * xe
    for k in range(SSM_CONV):
        if k != SSM_CONV // 2:
            acc = acc + w_ref[k:k + 1, :] * _shift_rows(xe, k - SSM_CONV // 2)
    return acc


def conv_fwd(name, x, w, b, L, tm=ROW_TILE):
    T, C = x.shape

    def body(xp, xc, xn, w_ref, b_ref, o_ref):
        xe = _extended(xp, xc, xn, pl.program_id(0), tm, L, T)
        pre = _conv_pre(xe, w_ref, b_ref)[HALO:HALO + tm]
        o_ref[...] = pre * jax.nn.sigmoid(pre)

    full = lambda a: pl.BlockSpec(a.shape, lambda i: (0, 0))
    return pl.pallas_call(body, name=name, grid=(T // tm,), in_specs=_conv_specs(tm, C, T) + [full(w), full(b)],
                          out_specs=pl.BlockSpec((tm, C), lambda i: (i, 0)), out_shape=jax.ShapeDtypeStruct((T, C), F32),
                          compiler_params=_cparams(("parallel",)))(x, x, x, w, b)


def conv_bwd(name, x, w, b, gu, L, tm=ROW_TILE):
    T, C = x.shape

    def body(xp, xc, xn, gp, gc, gn, w_ref, b_ref, dx_ref, dw_ref, db_ref):
        i = pl.program_id(0)
        xe = _extended(xp, xc, xn, i, tm, L, T)
        ge = _extended(gp, gc, gn, i, tm, L, T)
        pre = _conv_pre(xe, w_ref, b_ref)
        sg = jax.nn.sigmoid(pre)
        gpre = ge * (sg * (1.0 + pre * (1.0 - sg)))
        half = SSM_CONV // 2
        dx = jnp.zeros((tm, C), F32)
        rows = []
        for k in range(SSM_CONV):
            dx = dx + w_ref[k:k + 1, :] * _shift_rows(gpre, half - k)[HALO:HALO + tm]
            rows.append(jnp.sum(gpre[HALO:HALO + tm] * _shift_rows(xe, k - half)[HALO:HALO + tm], axis=0, keepdims=True))
        dx_ref[...] = dx
        rows += [jnp.zeros((1, C), F32)] * (8 - SSM_CONV)
        _accumulate(dw_ref, jnp.concatenate(rows, axis=0), i == 0)
        _accumulate(db_ref, jnp.sum(gpre[HALO:HALO + tm], axis=0, keepdims=True), i == 0)

    full = lambda a: pl.BlockSpec(a.shape, lambda i: (0, 0))
    return pl.pallas_call(
        body, name=name, grid=(T // tm,), in_specs=_conv_specs(tm, C, T) * 2 + [full(w), full(b)],
        out_specs=[pl.BlockSpec((tm, C), lambda i: (i, 0)), pl.BlockSpec((8, C), lambda i: (0, 0)), pl.BlockSpec((1, C), lambda i: (0, 0))],
        out_shape=[jax.ShapeDtypeStruct((T, C), F32), jax.ShapeDtypeStruct((8, C), F32), jax.ShapeDtypeStruct((1, C), F32)],
        compiler_params=_cparams(("arbitrary",)))(x, x, x, gu, gu, gu, w, b)


SSM_PAIRS = SSM_HEADS // 2
TN_DIMS = (((0,), (0,)), ((), ()))
HIGHEST = lax.Precision.HIGHEST


def _ssd_chunk(direction, xps, bs, cs, dt_col, dt_row, alog_row, alog_col, hps):
    Q = SSM_CHUNK
    da_col = dt_col * (-jnp.exp(alog_row))
    da_row = dt_row * (-jnp.exp(alog_col))
    ii = lax.broadcasted_iota(jnp.int32, (Q, Q), 0)
    jj = lax.broadcasted_iota(jnp.int32, (Q, Q), 1)
    tri = (ii >= jj) if direction == 0 else (ii <= jj)
    trif = tri.astype(F32)
    acs_col = jnp.dot(trif, da_col, precision=HIGHEST, preferred_element_type=F32)
    acs_row = lax.dot_general(da_row, trif, NT_DIMS, precision=HIGHEST, preferred_element_type=F32)
    tot_col = jnp.sum(da_col, axis=0, keepdims=True)
    lane16 = lax.broadcasted_iota(jnp.int32, (1, SSM_HEADS), 1)
    sub16 = lax.broadcasted_iota(jnp.int32, (SSM_HEADS, 1), 0)
    low = lax.broadcasted_iota(jnp.int32, (1, 2 * SSM_HEAD_DIM), 1) < SSM_HEAD_DIM

    def col(v, h):
        return jnp.sum(v * (lane16 == h).astype(F32), axis=1, keepdims=True)

    def row(v, h):
        return jnp.sum(v * (sub16 == h).astype(F32), axis=0, keepdims=True)

    ys, hos = [], []
    pairs_per_group = SSM_PAIRS // SSM_GROUPS
    for g in range(SSM_GROUPS):
        bb, cb16 = bs[g].astype(BF16), cs[g].astype(BF16)
        cb = lax.dot_general(cb16, bb, NT_DIMS, preferred_element_type=F32)
        for pp in range(pairs_per_group):
            p = g * pairs_per_group + pp
            h0, h1 = 2 * p, 2 * p + 1
            ac0, ac1 = col(acs_col, h0), col(acs_col, h1)
            seg0 = jnp.exp(jnp.where(tri, ac0 - row(acs_row, h0), -jnp.inf))
            seg1 = jnp.exp(jnp.where(tri, ac1 - row(acs_row, h1), -jnp.inf))
            dt_l = jnp.where(low, col(dt_col, h0), col(dt_col, h1))
            ac_l = jnp.where(low, ac0, ac1)
            tot_l = jnp.where(low, col(tot_col, h0), col(tot_col, h1))
            xdt = xps[p] * dt_l
            y = (jnp.dot((cb * seg0).astype(BF16), jnp.where(low, xdt, 0.0).astype(BF16), preferred_element_type=F32)
                 + jnp.dot((cb * seg1).astype(BF16), jnp.where(low, 0.0, xdt).astype(BF16), preferred_element_type=F32))
            y = y + jnp.dot(cb16, hps[p].astype(BF16), preferred_element_type=F32) * jnp.exp(ac_l)
            st = lax.dot_general(bb, (xdt * jnp.exp(tot_l - ac_l)).astype(BF16), TN_DIMS, preferred_element_type=F32)
            ys.append(y)
            hos.append(hps[p] * jnp.exp(tot_l) + st)
    return tuple(ys), tuple(hos)


def _ssd_chunk_of(direction, step, ncl, ncc):
    if direction == 0:
        return jnp.where(step < ncc, ncl + step, step - ncc)
    return jnp.where(step < ncc, ncl + ncc - 1 - step, ncl - 1 - (step - ncc))


def _ssd_load(u_ref):
    Q = SSM_CHUNK
    xps = tuple(u_ref[:, LANES * p:LANES * (p + 1)] for p in range(SSM_PAIRS))
    bs = tuple(u_ref[:, SSM_INNER + SSM_STATE * g:SSM_INNER + SSM_STATE * (g + 1)] for g in range(SSM_GROUPS))
    c0 = SSM_INNER + SSM_GROUPS * SSM_STATE
    cs = tuple(u_ref[:, c0 + SSM_STATE * g:c0 + SSM_STATE * (g + 1)] for g in range(SSM_GROUPS))
    return xps, bs, cs


def ssd_fwd(name, direction, u, dt, dt_t, alog_row, alog_col, L):
    T = u.shape[0]
    Q, N = SSM_CHUNK, SSM_STATE
    ncl, ncc = L // Q, (T - L) // Q
    nc = ncl + ncc
    cm = lambda s: _ssd_chunk_of(direction, s, ncl, ncc)

    def body(u_ref, dt_ref, dtt_ref, ar_ref, ac_ref, y_ref, hin_ref, state):
        @pl.when(pl.program_id(0) == 0)
        def _():
            state[...] = jnp.zeros_like(state)

        xps, bs, cs = _ssd_load(u_ref)
        hps = tuple(state[p] for p in range(SSM_PAIRS))
        for p in range(SSM_PAIRS):
            hin_ref[0, p] = hps[p]
        ys, hos = _ssd_chunk(direction, xps, bs, cs, dt_ref[...], dtt_ref[...], ar_ref[...], ac_ref[...], hps)
        for p in range(SSM_PAIRS):
            y_ref[:, LANES * p:LANES * (p + 1)] = ys[p]
            state[p] = hos[p]

    return pl.pallas_call(
        body, name=name, grid=(nc,),
        in_specs=[pl.BlockSpec((Q, SSM_CONV_DIM), lambda s: (cm(s), 0)),
                  pl.BlockSpec((Q, SSM_HEADS), lambda s: (cm(s), 0)),
                  pl.BlockSpec((SSM_HEADS, Q), lambda s: (0, cm(s))),
                  pl.BlockSpec((1, SSM_HEADS), lambda s: (0, 0)),
                  pl.BlockSpec((SSM_HEADS, 1), lambda s: (0, 0))],
        out_specs=[pl.BlockSpec((Q, SSM_INNER), lambda s: (cm(s), 0)),
                   pl.BlockSpec((1, SSM_PAIRS, N, LANES), lambda s: (cm(s), 0, 0, 0))],
        out_shape=[jax.ShapeDtypeStruct((T, SSM_INNER), F32), jax.ShapeDtypeStruct((nc, SSM_PAIRS, N, LANES), F32)],
        scratch_shapes=[pltpu.VMEM((SSM_PAIRS, N, LANES), F32)],
        compiler_params=_cparams(("arbitrary",)),
    )(u, dt, dt_t, alog_row, alog_col)


def ssd_bwd(name, direction, u, dt, dt_t, alog_row, alog_col, hin, dy, L, add_x=None, add_u=None):
    T = u.shape[0]
    Q, N = SSM_CHUNK, SSM_STATE
    ncl, ncc = L // Q, (T - L) // Q
    nc = ncl + ncc
    cm = lambda s: _ssd_chunk_of(direction, nc - 1 - s, ncl, ncc)
    n_add = (add_x is not None) + (add_u is not None)

    def body(*refs):
        u_ref, dt_ref, dtt_ref, ar_ref, ac_ref, hin_ref, dy_ref = refs[:7]
        add_refs = refs[7:7 + n_add]
        du_ref, ddt_ref, ddtt_ref, dar_ref, dac_ref, dstate = refs[7 + n_add:]
        first = pl.program_id(0) == 0

        @pl.when(first)
        def _():
            dstate[...] = jnp.zeros_like(dstate)

        xps, bs, cs = _ssd_load(u_ref)
        hps = tuple(hin_ref[0, p] for p in range(SSM_PAIRS))
        _, vjp = jax.vjp(functools.partial(_ssd_chunk, direction), xps, bs, cs, dt_ref[...], dtt_ref[...], ar_ref[...],
                         ac_ref[...], hps)
        dys = tuple(dy_ref[:, LANES * p:LANES * (p + 1)] for p in range(SSM_PAIRS))
        dhs = tuple(dstate[p] for p in range(SSM_PAIRS))
        gx, gb, gc, gdt, gdtt, gar, gac, ghp = vjp((dys, dhs))
        parts = list(gx) + list(gb) + list(gc)
        du = jnp.concatenate(parts, axis=1)
        k = 0
        if add_x is not None:
            du = du + jnp.concatenate([add_refs[k][...], jnp.zeros((Q, SSM_CONV_DIM - SSM_INNER), F32)], axis=1)
            k += 1
        if add_u is not None:
            du = du + add_refs[k][...]
        du_ref[...] = du
        ddt_ref[...] = gdt
        ddtt_ref[...] = gdtt
        _accumulate(dar_ref, gar, first)
        _accumulate(dac_ref, gac, first)
        for p in range(SSM_PAIRS):
            dstate[p] = ghp[p]

    in_specs = [pl.BlockSpec((Q, SSM_CONV_DIM), lambda s: (cm(s), 0)),
                pl.BlockSpec((Q, SSM_HEADS), lambda s: (cm(s), 0)),
                pl.BlockSpec((SSM_HEADS, Q), lambda s: (0, cm(s))),
                pl.BlockSpec((1, SSM_HEADS), lambda s: (0, 0)),
                pl.BlockSpec((SSM_HEADS, 1), lambda s: (0, 0)),
                pl.BlockSpec((1, SSM_PAIRS, N, LANES), lambda s: (cm(s), 0, 0, 0)),
                pl.BlockSpec((Q, SSM_INNER), lambda s: (cm(s), 0))]
    ins = [u, dt, dt_t, alog_row, alog_col, hin, dy]
    if add_x is not None:
        in_specs.append(pl.BlockSpec((Q, SSM_INNER), lambda s: (cm(s), 0)))
        ins.append(add_x)
    if add_u is not None:
        in_specs.append(pl.BlockSpec((Q, SSM_CONV_DIM), lambda s: (cm(s), 0)))
        ins.append(add_u)
    return pl.pallas_call(
        body, name=name, grid=(nc,), in_specs=in_specs,
        out_specs=[pl.BlockSpec((Q, SSM_CONV_DIM), lambda s: (cm(s), 0)),
                   pl.BlockSpec((Q, SSM_HEADS), lambda s: (cm(s), 0)),
                   pl.BlockSpec((SSM_HEADS, Q), lambda s: (0, cm(s))),
                   pl.BlockSpec((1, SSM_HEADS), lambda s: (0, 0)),
                   pl.BlockSpec((SSM_HEADS, 1), lambda s: (0, 0))],
        out_shape=[jax.ShapeDtypeStruct((T, SSM_CONV_DIM), F32), jax.ShapeDtypeStruct((T, SSM_HEADS), F32),
                   jax.ShapeDtypeStruct((SSM_HEADS, T), F32), jax.ShapeDtypeStruct((1, SSM_HEADS), F32),
                   jax.ShapeDtypeStruct((SSM_HEADS, 1), F32)],
        scratch_shapes=[pltpu.VMEM((SSM_PAIRS, N, LANES), F32)],
        compiler_params=_cparams(("arbitrary",)),
    )(*ins)


PEER_MASKS = (1, 2, 4, 3, 5, 6, 7)
N_PEERS = len(PEER_MASKS)
MESH_IDS = pl.DeviceIdType.MESH


def _my_index():
    return lax.axis_index("x") * 4 + lax.axis_index("y") * 2 + lax.axis_index("c")


def _coords(idx):
    return (idx // 4, (idx // 2) % 2, idx % 2)


def all_gather_hbm(name, arrays):
    n = len(arrays)
    chip_masks = (4, 2, 6)

    def body(*refs):
        ins, outs = refs[:n], refs[n:2 * n]
        send_sems, recv_sems, local_sems = refs[2 * n:]
        me = _my_index()
        sibling = me ^ 1

        def copy(a, k, block, to, src=None):
            return pltpu.make_async_remote_copy(
                src_ref=outs[a].at[block] if src is None else src, dst_ref=outs[a].at[block],
                send_sem=send_sems.at[a * N_PEERS + k], recv_sem=recv_sems.at[a * N_PEERS + k],
                device_id=_coords(to), device_id_type=MESH_IDS)

        started, own = [], []
        for a in range(n):
            local = pltpu.make_async_copy(ins[a], outs[a].at[me], local_sems.at[a])
            local.start()
            own.append(local)
            first = [copy(a, 0, me, sibling, src=ins[a])] + [copy(a, 1 + j, me, me ^ m, src=ins[a]) for j, m in enumerate(chip_masks)]
            for cp in first:
                cp.start()
            started += first
        for a in range(n):
            for j, m in enumerate(chip_masks):
                copy(a, 1 + j, me ^ m, me).wait_recv()
                fwd = copy(a, 4 + j, me ^ m, sibling)
                fwd.start()
                started.append(fwd)
        for a in range(n):
            copy(a, 0, sibling, me).wait_recv()
            for j, m in enumerate(chip_masks):
                copy(a, 4 + j, sibling ^ m, me).wait_recv()
        for cp in started:
            cp.wait_send()
        for cp in own:
            cp.wait()

    any_spec = pl.BlockSpec(memory_space=pl.ANY)
    return pl.pallas_call(
        body, name=name, in_specs=[any_spec] * n, out_specs=[any_spec] * n,
        out_shape=[jax.ShapeDtypeStruct((N_DEV,) + a.shape, a.dtype) for a in arrays],
        scratch_shapes=[pltpu.SemaphoreType.DMA((n * N_PEERS,)), pltpu.SemaphoreType.DMA((n * N_PEERS,)),
                        pltpu.SemaphoreType.DMA((n,))],
    )(*arrays)


def exchange_hbm(name, arrays):
    n = len(arrays)

    def body(*refs):
        ins, outs = refs[:n], refs[n:2 * n]
        send_sems, recv_sems, local_sems = refs[2 * n:]
        me = _my_index()
        copies = []
        for a in range(n):
            local = pltpu.make_async_copy(ins[a].at[me], outs[a].at[me], local_sems.at[a])
            local.start()
            copies.append(local)
            for k, mask in enumerate(PEER_MASKS):
                peer = me ^ mask
                cp = pltpu.make_async_remote_copy(src_ref=ins[a].at[peer], dst_ref=outs[a].at[me],
                                                  send_sem=send_sems.at[a * N_PEERS + k], recv_sem=recv_sems.at[a * N_PEERS + k],
                                                  device_id=_coords(peer), device_id_type=MESH_IDS)
                cp.start()
                copies.append(cp)
        for cp in copies:
            cp.wait()

    any_spec = pl.BlockSpec(memory_space=pl.ANY)
    return pl.pallas_call(
        body, name=name, in_specs=[any_spec] * n, out_specs=[any_spec] * n,
        out_shape=[jax.ShapeDtypeStruct(a.shape, a.dtype) for a in arrays],
        scratch_shapes=[pltpu.SemaphoreType.DMA((n * N_PEERS,)), pltpu.SemaphoreType.DMA((n * N_PEERS,)),
                        pltpu.SemaphoreType.DMA((n,))],
    )(*arrays)


def all_gather_vmem(name, v):
    def body(v_ref, out_ref, send_sems, recv_sems):
        me = _my_index()
        out_ref[me] = v_ref[...]
        copies = []
        for k, mask in enumerate(PEER_MASKS):
            cp = pltpu.make_async_remote_copy(src_ref=v_ref, dst_ref=out_ref.at[me], send_sem=send_sems.at[k],
                                              recv_sem=recv_sems.at[k], device_id=_coords(me ^ mask), device_id_type=MESH_IDS)
            cp.start()
            copies.append(cp)
        for cp in copies:
            cp.wait()

    vm = pl.BlockSpec(memory_space=pltpu.VMEM)
    return pl.pallas_call(
        body, name=name, in_specs=[vm], out_specs=vm, out_shape=jax.ShapeDtypeStruct((N_DEV,) + v.shape, v.dtype),
        scratch_shapes=[pltpu.SemaphoreType.DMA((N_PEERS,)), pltpu.SemaphoreType.DMA((N_PEERS,))],
    )(v)


def _row_tile(rows, cols, bufs):
    budget = 24 * 1024 * 1024 // (bufs * 2 * 4 * max(cols, LANES))
    if rows <= budget:
        return rows
    for t in range(budget - budget % 16, 15, -16):
        if rows % t == 0:
            return t
    return rows


def sum_parts(name, parts):
    P, R, C = parts.shape
    tr = _row_tile(R, C, P + 1)

    def body(p_ref, o_ref):
        acc = p_ref[0].astype(F32)
        for s in range(1, P):
            acc = acc + p_ref[s].astype(F32)
        o_ref[...] = acc

    return pl.pallas_call(body, name=name, grid=(R // tr,), in_specs=[pl.BlockSpec((P, tr, C), lambda i: (0, i, 0))],
                          out_specs=pl.BlockSpec((tr, C), lambda i: (i, 0)), out_shape=jax.ShapeDtypeStruct((R, C), F32),
                          compiler_params=_cparams(("parallel",)))(parts)


def cast_bf16(name, x):
    R, C = x.shape
    tr = _row_tile(R, C, 2)

    def body(x_ref, o_ref):
        o_ref[...] = x_ref[...].astype(BF16)

    spec = pl.BlockSpec((tr, C), lambda i: (i, 0))
    return pl.pallas_call(body, name=name, grid=(R // tr,), in_specs=[spec], out_specs=spec,
                          out_shape=jax.ShapeDtypeStruct((R, C), BF16), compiler_params=_cparams(("parallel",)))(x)


def adamw(name, w, g, m, v):
    R, C = w.shape
    tr = _row_tile(R, C, 7)

    def body(w_ref, g_ref, m_ref, v_ref, d_ref, nm_ref, nv_ref):
        g = g_ref[...]
        nm = ADAM_B1 * m_ref[...] + (1.0 - ADAM_B1) * g
        nv = ADAM_B2 * v_ref[...] + (1.0 - ADAM_B2) * (g * g)
        m_hat = nm / (1.0 - ADAM_B1 ** ADAM_STEP)
        v_hat = nv / (1.0 - ADAM_B2 ** ADAM_STEP)
        d_ref[...] = -ADAM_LR * (m_hat / (jnp.sqrt(v_hat) + ADAM_EPS) + ADAM_WD * w_ref[...])
        nm_ref[...] = nm
        nv_ref[...] = nv

    spec = pl.BlockSpec((tr, C), lambda i: (i, 0))
    return pl.pallas_call(body, name=name, grid=(R // tr,), in_specs=[spec] * 4, out_specs=[spec] * 3,
                          out_shape=[jax.ShapeDtypeStruct((R, C), F32)] * 3, compiler_params=_cparams(("parallel",)))(w, g, m, v)


def loss_and_grad(name, x, target, L, tm=ROW_TILE):
    T, D = x.shape
    nlt = L // tm

    def body(x_ref, t_ref, loss_ref, dx_ref):
        i = pl.program_id(0)
        err = jnp.where(i < nlt, x_ref[...] - t_ref[...], 0.0)
        dx_ref[...] = err * (1.0 / D)
        part = 0.5 * jnp.sum(jnp.sum(err * err, axis=1, keepdims=True), axis=0, keepdims=True) * (1.0 / D)
        _accumulate(loss_ref, part, i == 0)

    return pl.pallas_call(
        body, name=name, grid=(T // tm,),
        in_specs=[pl.BlockSpec((tm, D), lambda i: (i, 0)), pl.BlockSpec((tm, D), lambda i: (jnp.minimum(i, nlt - 1), 0))],
        out_specs=[pl.BlockSpec((1, 1), lambda i: (0, 0)), pl.BlockSpec((tm, D), lambda i: (i, 0))],
        out_shape=[jax.ShapeDtypeStruct((1, 1), F32), jax.ShapeDtypeStruct((T, D), F32)],
        compiler_params=_cparams(("arbitrary",)))(x, target)


def small_fwd(name, fn, arrays, out_shapes):
    def body(*refs):
        res = fn(*[r[...] for r in refs[:len(arrays)]])
        for o_ref, r in zip(refs[len(arrays):], res):
            o_ref[...] = r

    return pl.pallas_call(body, name=name, out_shape=[jax.ShapeDtypeStruct(s, F32) for s in out_shapes])(*arrays)


def small_bwd(name, fn, arrays, cts):
    n = len(arrays)

    def body(*refs):
        _, vjp = jax.vjp(lambda *a: tuple(fn(*a)), *[r[...] for r in refs[:n]])
        grads = vjp(tuple(r[...] for r in refs[n:n + len(cts)]))
        for o_ref, g in zip(refs[n + len(cts):], grads):
            o_ref[...] = g

    return pl.pallas_call(body, name=name, out_shape=[jax.ShapeDtypeStruct(a.shape, F32) for a in arrays])(*arrays, *cts)


def fn_silu(x):
    return (x * jax.nn.sigmoid(x),)


FWD_NAMES = ["x", "c", "ctx", "c_ctx", "w_mod", "b_mod", "norm1_g", "norm2_g", "w_in", "ssm_conv_w", "ssm_conv_b",
             "ssm_dt_bias", "ssm_a_log", "ssm_d", "ssm_norm_g", "swa_q_norm_g", "swa_k_norm_g", "swa_sink", "mla_q_lat_g",
             "mla_kv_lat_g", "w_mla_uq", "w_mla_ukv", "mla_q_norm_g", "mla_k_norm_g", "w_p_ssm", "w_p_swa", "w_p_mla",
             "w_out", "w_ffn_in", "w_ffn_out"]
WEIGHT_NAMES = FWD_NAMES[3:]
GATHERED = ["w_in", "w_mla_uq", "w_mla_ukv", "w_p_ssm", "w_p_swa", "w_p_mla", "w_out", "w_ffn_in", "w_ffn_out"]
COLUMN_SHARDED = ("w_in", "w_mla_uq", "w_mla_ukv", "w_ffn_in")
REPLICATED = ["c_ctx", "b_mod", "norm1_g", "norm2_g", "ssm_conv_b", "ssm_dt_bias", "ssm_a_log", "ssm_d", "ssm_norm_g",
              "swa_q_norm_g", "swa_k_norm_g", "swa_sink", "mla_q_lat_g", "mla_kv_lat_g", "mla_q_norm_g", "mla_k_norm_g"]
IN_SEGS = [("xbc", SSM_CONV_DIM), ("dt", 2 * SSM_HEADS), ("ks", SWA_KV_HEADS * SWA_HEAD_DIM), ("vs", SWA_KV_HEADS * SWA_HEAD_DIM),
           ("ckv", MLA_KV_RANK), ("kr", MLA_ROPE), ("z", SSM_INNER), ("qs", SWA_Q_HEADS * SWA_HEAD_DIM), ("cq", MLA_Q_RANK),
           ("g1", None), ("g2", None), ("g3", None)]


def _pack(vectors, multiple):
    flat = jnp.concatenate([v.reshape(-1) for v in vectors])
    pad = (-flat.shape[0]) % multiple
    return jnp.pad(flat, (0, pad)).reshape(-1, LANES)


def _unpack(packed, shapes):
    flat, out, off = packed.reshape(-1), [], 0
    for s in shapes:
        n = int(np.prod(s))
        out.append(flat[off:off + n].reshape(s))
        off += n
    return out


def _rope_tables(L, T, rot_dim):
    nf = rot_dim // 4
    inv = jnp.power(ROPE_BASE, -jnp.arange(nf, dtype=F32) / nf)
    r, col = jnp.meshgrid(jnp.arange(L // GRID_W, dtype=F32), jnp.arange(GRID_W, dtype=F32), indexing="ij")
    ang = jnp.stack([r.reshape(-1)[:, None] * inv, col.reshape(-1)[:, None] * inv], axis=1)
    cos, sin = jnp.cos(ang), jnp.sin(ang)
    c = jnp.concatenate([cos[:, 0], cos[:, 0], cos[:, 1], cos[:, 1]], axis=1)
    s = jnp.concatenate([-sin[:, 0], sin[:, 0], -sin[:, 1], sin[:, 1]], axis=1)
    c = jnp.pad(c, ((0, T - L), (0, LANES - rot_dim)), constant_values=1.0)
    s = jnp.pad(s, ((0, T - L), (0, LANES - rot_dim)))
    return c, s


def _pad_rows(a, rows):
    return jnp.pad(a, ((0, rows - a.shape[0]), (0, 0)))


def _row(w, per_head=0, off=0, diff=True):
    return ("row", w, per_head, off, diff)


PAR, PAR_ND = ("par", True), ("par", False)
GRP = ("grp", True)


def kernel(*args):
    n_fwd, n_w = len(FWD_NAMES), len(WEIGHT_NAMES)
    inp = dict(zip(FWD_NAMES, args[:n_fwd]))
    loss_target = args[n_fwd]
    mom_m = dict(zip(WEIGHT_NAMES, args[n_fwd + 1:n_fwd + 1 + n_w]))
    mom_v = dict(zip(WEIGHT_NAMES, args[n_fwd + 1 + n_w:]))

    x, ctx = inp["x"][0], inp["ctx"][0]
    L, D = x.shape
    n_ctx = ctx.shape[0]
    T = L + n_ctx
    depth = inp["w_in"].shape[0]
    me = _my_index()
    in_widths = [w if w is not None else D for _, w in IN_SEGS]
    in_offs = np.concatenate([[0], np.cumsum(in_widths)]).tolist()
    ffn_h = inp["w_ffn_out"].shape[1] * N_DEV
    cfg_swa = AttnCfg(SWA_Q_HEADS, SWA_Q_HEADS // SWA_KV_HEADS, SWA_HEAD_DIM, SWA_HEAD_DIM, SWA_HEAD_DIM ** -0.5, SWA_WINDOW,
                      True, L, T, 256, 1)
    cfg_mla = AttnCfg(MLA_HEADS, 1, MLA_QK_PAD, MLA_V, MLA_QK ** -0.5, None, False, L, T, 1024, 2)
    cfg_mla_bwd = AttnCfg(MLA_HEADS, 1, MLA_QK_PAD, MLA_V, MLA_QK ** -0.5, None, False, L, T, 2048, 1)

    def rf(name, fn, descs, arrays, outs, heads=1):
        return rowop_fwd(name, fn, descs, arrays, outs, T, L, heads=heads)

    def rb(name, fn, descs, arrays, outs, cts, heads=1, add=None):
        return rowop_bwd(name, fn, descs, arrays, outs, cts, T, L, heads=heads, add=add)

    local = []
    for n in GATHERED:
        w = inp[n]
        local.append((jnp.swapaxes(w, 1, 2) if n in COLUMN_SHARDED else w).astype(BF16))
    gathered = dict(zip(GATHERED, all_gather_hbm("gather_weights", local)))

    def full(n, l):
        g = gathered[n][:, l]
        return g.reshape(g.shape[0] * g.shape[1], g.shape[2])

    def layer_weights(l):
        wt = {}
        w_in_t = full("w_in", l)
        for (sn, _), o, w in zip(IN_SEGS, in_offs, in_widths):
            seg = w_in_t[o:o + w]
            wt[sn] = _pad_rows(seg, LANES) if sn == "kr" else seg
        uq = full("w_mla_uq", l).reshape(MLA_HEADS, MLA_QK, MLA_Q_RANK)
        wt["uqn"] = uq[:, :MLA_NOPE].reshape(MLA_HEADS * MLA_NOPE, MLA_Q_RANK)
        wt["uqr"] = jnp.pad(uq[:, MLA_NOPE:], ((0, 0), (0, LANES - MLA_ROPE), (0, 0))).reshape(MLA_HEADS * LANES, MLA_Q_RANK)
        ukv = full("w_mla_ukv", l).reshape(MLA_HEADS, MLA_NOPE + MLA_V, MLA_KV_RANK)
        wt["uk"] = ukv[:, :MLA_NOPE].reshape(MLA_HEADS * MLA_NOPE, MLA_KV_RANK)
        wt["uv"] = ukv[:, MLA_NOPE:].reshape(MLA_HEADS * MLA_V, MLA_KV_RANK)
        for n in ("w_p_ssm", "w_p_swa", "w_p_mla", "w_out", "w_ffn_out"):
            wt[n] = full(n, l)
        ffn_in_t = full("w_ffn_in", l)
        wt["fg"], wt["fu"] = ffn_in_t[:ffn_h], ffn_in_t[ffn_h:]
        return wt

    def layer_params(l):
        p = {}
        for n in ("norm1_g", "norm2_g", "ssm_conv_b", "ssm_norm_g", "swa_q_norm_g", "swa_k_norm_g", "mla_q_lat_g", "mla_kv_lat_g"):
            p[n] = inp[n][l][None]
        p["dt_bias"] = inp["ssm_dt_bias"][l].reshape(1, 2 * SSM_HEADS)
        p["alog_row"] = [inp["ssm_a_log"][l][d][None] for d in range(2)]
        p["alog_col"] = [inp["ssm_a_log"][l][d][:, None] for d in range(2)]
        p["d_lane"] = jnp.repeat(inp["ssm_d"][l], SSM_HEAD_DIM)[None]
        p["sink"] = inp["swa_sink"][l].reshape(SWA_Q_HEADS, 1, 1)
        for n, key in (("mla_q_norm_g", "gq"), ("mla_k_norm_g", "gk")):
            g = inp[n][l]
            p[key + "n"] = g[:MLA_NOPE][None]
            p[key + "r"] = jnp.pad(g[MLA_NOPE:], (0, LANES - MLA_ROPE))[None]
        return p

    conv_local = _pack([inp["ssm_conv_w"]], 8 * LANES)
    conv_all = all_gather_vmem("gather_conv_w", conv_local)
    cw = inp["ssm_conv_w"].shape
    conv_full = conv_all.reshape(N_DEV, -1)[:, :cw[0] * cw[1] * cw[2]].reshape(N_DEV, cw[0], cw[1], cw[2])
    conv_full = jnp.moveaxis(conv_full, 0, 2).reshape(cw[0], cw[1], N_DEV * cw[2])
    conv_w8 = jnp.pad(conv_full, ((0, 0), (0, 8 - cw[1]), (0, 0)))

    silu_c, silu_cc = small_fwd("silu_c", lambda a, b: fn_silu(a) + fn_silu(b), [inp["c"], inp["c_ctx"][None]], [(1, D), (1, D)])
    silu_all = all_gather_vmem("gather_silu_c", silu_c.reshape(D // LANES, LANES)).reshape(N_DEV, D)
    S_rows = 2 * N_DEV
    S_mat = jnp.concatenate([silu_all, silu_cc, jnp.zeros((S_rows - N_DEV - 1, D), F32)], axis=0)
    mod_cols = inp["w_mod"].shape[2]
    mods_local = []
    for l in range(depth):
        bias = lax.dynamic_slice(inp["b_mod"][l], (me * mod_cols,), (mod_cols,))
        mods_local.append(matmul(S_mat, inp["w_mod"][l], "nn", f"mod{l}", add=jnp.broadcast_to(bias[None], (S_rows, mod_cols))))
    mods_all = all_gather_vmem("gather_mods", jnp.stack(mods_local).reshape(-1, LANES))
    mods_all = jnp.moveaxis(mods_all.reshape(N_DEV, depth, S_rows, mod_cols), 0, 2).reshape(depth, S_rows, N_DEV * mod_cols)
    mods_lat = lax.dynamic_slice(mods_all, (0, me, 0), (depth, 1, N_DEV * mod_cols))[:, 0]
    mods_ctx = mods_all[:, N_DEV]

    def layer_mods(l):
        return [jnp.stack([mods_lat[l, j * D:(j + 1) * D], mods_ctx[l, j * D:(j + 1) * D]])[:, None] for j in range(6)]

    cs_swa = _rope_tables(L, T, SWA_HEAD_DIM)
    cs_mla = _rope_tables(L, T, MLA_ROPE)
    nm_descs = [_row(D), PAR, GRP, GRP]
    resid_descs = [_row(D, diff=False), _row(D), GRP]
    tab = [_row(LANES, diff=False), _row(LANES, diff=False)]
    swaq_descs = [_row(SWA_HEAD_DIM, 1), PAR] + tab
    swakv_descs = [_row(SWA_HEAD_DIM, 1), _row(SWA_HEAD_DIM, 1), PAR] + tab
    mlaq_descs = [_row(LANES, 1), _row(LANES, 1), PAR, PAR] + tab
    mlakv_descs = [_row(LANES, 1), _row(LANES, 1), _row(LANES), PAR, PAR] + tab
    ssdout_descs = [_row(SSM_INNER), _row(SSM_INNER, diff=False), _row(SSM_INNER), _row(SSM_INNER), PAR, PAR]
    merge_descs = [_row(D)] * 6
    swiglu_descs = [_row(ffn_h), _row(ffn_h)]
    seg_names = [sn for sn, _ in IN_SEGS]
    seg_groups = [seg_names[:7], seg_names[7:]]

    def layer_fwd(l, X, wt, p, mods):
        sh1, sc1, gt1, sh2, sc2, gt2 = mods
        r = {"X": X}
        r["h1"] = rf(f"l{l}_norm1", fn_norm_mod, nm_descs, [X, p["norm1_g"], sh1, sc1], [(D, 0, BF16)])[0]
        for gi, group in enumerate(seg_groups):
            r.update(zip(group, matmul_multi(f"l{l}_in{gi}", r["h1"], [wt[sn] for sn in group])))
        r["u"] = conv_fwd(f"l{l}_conv", r["xbc"], conv_w8[l], p["ssm_conv_b"], L)
        r["dts"] = rf(f"l{l}_softplus", fn_softplus, [_row(2 * SSM_HEADS), PAR], [r["dt"], p["dt_bias"]], [(2 * SSM_HEADS, 0, F32)])[0]
        for d in range(2):
            dt_d = r["dts"][:, d * SSM_HEADS:(d + 1) * SSM_HEADS]
            r[f"dt{d}"], r[f"dtt{d}"] = dt_d, dt_d.T
            r[f"y{d}"], r[f"hin{d}"] = ssd_fwd(f"l{l}_ssd{d}", d, r["u"], dt_d, dt_d.T, p["alog_row"][d], p["alog_col"][d], L)
        r["ys"] = rf(f"l{l}_ssd_out", fn_ssd_out, ssdout_descs, [r["y0"], r["y1"], r["u"], r["z"], p["d_lane"], p["ssm_norm_g"]],
                     [(SSM_INNER, 0, F32)])[0]
        r["Qs"] = rf(f"l{l}_swa_q", fn_swa_q, swaq_descs, [r["qs"], p["swa_q_norm_g"], *cs_swa], [(SWA_HEAD_DIM, 1, BF16)], SWA_Q_HEADS)[0]
        r["Ks"], r["Vs"] = rf(f"l{l}_swa_kv", fn_swa_kv, swakv_descs, [r["ks"], r["vs"], p["swa_k_norm_g"], *cs_swa],
                              [(SWA_HEAD_DIM, 1, BF16), (SWA_HEAD_DIM, 1, BF16)], SWA_KV_HEADS)
        r["Os"], r["lse_s"] = flash_fwd(f"l{l}_swa_fwd", cfg_swa, r["Qs"], r["Ks"], r["Vs"], p["sink"])
        r["cqn"] = rf(f"l{l}_q_lat", fn_rms, [_row(MLA_Q_RANK), PAR], [r["cq"], p["mla_q_lat_g"]], [(MLA_Q_RANK, 0, BF16)])[0]
        r["qn"], r["qr"] = matmul_multi(f"l{l}_uq", r["cqn"], [wt["uqn"], wt["uqr"]])
        r["Qm"] = rf(f"l{l}_mla_q", fn_mla_q, mlaq_descs, [r["qn"], r["qr"], p["gqn"], p["gqr"], *cs_mla], [(MLA_QK_PAD, 1, BF16)], MLA_HEADS)[0]
        r["ckvn"] = rf(f"l{l}_kv_lat", fn_rms, [_row(MLA_KV_RANK), PAR], [r["ckv"], p["mla_kv_lat_g"]], [(MLA_KV_RANK, 0, BF16)])[0]
        r["kn"], r["vp"] = matmul_multi(f"l{l}_ukv", r["ckvn"], [wt["uk"], wt["uv"]])
        r["Km"], r["Vm"] = rf(f"l{l}_mla_kv", fn_mla_kv, mlakv_descs, [r["kn"], r["vp"], r["kr"], p["gkn"], p["gkr"], *cs_mla],
                              [(MLA_QK_PAD, 1, BF16), (MLA_V, 1, BF16)], MLA_HEADS)
        r["Om"], r["lse_m"] = flash_fwd(f"l{l}_mla_fwd", cfg_mla, r["Qm"], r["Km"], r["Vm"], None)
        r["P1"] = matmul(r["ys"], wt["w_p_ssm"], "nn", f"l{l}_p_ssm")
        r["P2"] = matmul(r["Os"], wt["w_p_swa"], "nn", f"l{l}_p_swa")
        r["P3"] = matmul(r["Om"], wt["w_p_mla"], "nn", f"l{l}_p_mla")
        r["mg"] = rf(f"l{l}_merge", fn_merge, merge_descs, [r["g1"], r["g2"], r["g3"], r["P1"], r["P2"], r["P3"]], [(D, 0, BF16)])[0]
        r["A"] = matmul(r["mg"], wt["w_out"], "nn", f"l{l}_out")
        r["X1"] = rf(f"l{l}_resid1", fn_resid, resid_descs, [X, r["A"], gt1], [(D, 0, F32)])[0]
        r["h2"] = rf(f"l{l}_norm2", fn_norm_mod, nm_descs, [r["X1"], p["norm2_g"], sh2, sc2], [(D, 0, BF16)])[0]
        r["Fg"] = matmul(r["h2"], wt["fg"], "nt", f"l{l}_ffn_g")
        r["Fu"] = matmul(r["h2"], wt["fu"], "nt", f"l{l}_ffn_u")
        r["sg"] = rf(f"l{l}_swiglu", fn_swiglu, swiglu_descs, [r["Fg"], r["Fu"]], [(ffn_h, 0, BF16)])[0]
        r["B"] = matmul(r["sg"], wt["w_ffn_out"], "nn", f"l{l}_ffn_out")
        X2 = rf(f"l{l}_resid2", fn_resid, resid_descs, [r["X1"], r["B"], gt2], [(D, 0, F32)])[0]
        return X2, r

    def attn_bwd(tag, cfg, q, k, v, o, lse, do, sink):
        res = attn_delta(f"{tag}_delta", cfg, o, do, lse, sink)
        delta, dob = res[0], res[1]
        dsink = res[2] if cfg.has_sink else None
        dq, dk, dv = flash_bwd_fused(f"{tag}_bwd", cfg, q, k, v, dob, lse, delta)
        return dq, dk, dv, dsink

    def layer_bwd(l, dX2, r, wt, p, mods):
        sh1, sc1, gt1, sh2, sc2, gt2 = mods
        g, gw = {}, {}
        dmod = [None] * 6
        dB, dmod[5] = rb(f"l{l}_resid2_b", fn_resid, resid_descs, [r["X1"], r["B"], gt2], [(D, 0, F32)], [dX2])
        dsg = matmul(dB, wt["w_ffn_out"], "nt", f"l{l}_ffn_out_da")
        gw["w_ffn_out"] = matmul(r["sg"], dB, "tn", f"l{l}_ffn_out_dw")
        dFg, dFu = rb(f"l{l}_swiglu_b", fn_swiglu, swiglu_descs, [r["Fg"], r["Fu"]], [(ffn_h, 0, BF16)], [dsg])
        dh2 = matmul(dFg, wt["fg"], "nn", f"l{l}_ffn_g_da")
        dh2 = matmul(dFu, wt["fu"], "nn", f"l{l}_ffn_u_da", add=dh2)
        gw["w_ffn_in"] = jnp.concatenate([matmul(r["h2"], dFg, "tn", f"l{l}_ffn_g_dw"), matmul(r["h2"], dFu, "tn", f"l{l}_ffn_u_dw")], axis=1)
        dX1, g["norm2_g"], dmod[3], dmod[4] = rb(f"l{l}_norm2_b", fn_norm_mod, nm_descs, [r["X1"], p["norm2_g"], sh2, sc2],
                                                [(D, 0, BF16)], [dh2], add={0: dX2})
        dA, dmod[2] = rb(f"l{l}_resid1_b", fn_resid, resid_descs, [r["X"], r["A"], gt1], [(D, 0, F32)], [dX1])
        dmg = matmul(dA, wt["w_out"], "nt", f"l{l}_out_da")
        gw["w_out"] = matmul(r["mg"], dA, "tn", f"l{l}_out_dw")
        dsegs = {}
        dsegs["g1"], dsegs["g2"], dsegs["g3"], dP1, dP2, dP3 = rb(
            f"l{l}_merge_b", fn_merge, merge_descs, [r["g1"], r["g2"], r["g3"], r["P1"], r["P2"], r["P3"]], [(D, 0, BF16)], [dmg])
        dys = matmul(dP1, wt["w_p_ssm"], "nt", f"l{l}_p_ssm_da")
        dOs = matmul(dP2, wt["w_p_swa"], "nt", f"l{l}_p_swa_da")
        dOm = matmul(dP3, wt["w_p_mla"], "nt", f"l{l}_p_mla_da")
        gw["w_p_ssm"] = matmul(r["ys"], dP1, "tn", f"l{l}_p_ssm_dw")
        gw["w_p_swa"] = matmul(r["Os"], dP2, "tn", f"l{l}_p_swa_dw")
        gw["w_p_mla"] = matmul(r["Om"], dP3, "tn", f"l{l}_p_mla_dw")
        dQm, dKm, dVm, _ = attn_bwd(f"l{l}_mla", cfg_mla_bwd, r["Qm"], r["Km"], r["Vm"], r["Om"], r["lse_m"], dOm, None)
        dkn, dvp, dsegs["kr"], dgkn, dgkr = rb(f"l{l}_mla_kv_b", fn_mla_kv, mlakv_descs,
                                               [r["kn"], r["vp"], r["kr"], p["gkn"], p["gkr"], *cs_mla],
                                               [(MLA_QK_PAD, 1, BF16), (MLA_V, 1, BF16)], [dKm, dVm], MLA_HEADS)
        dckvn = matmul_sum(f"l{l}_ukv_da", [(dkn, wt["uk"]), (dvp, wt["uv"])])
        dw_uk = matmul(r["ckvn"], dkn, "tn", f"l{l}_uk_dw").reshape(MLA_KV_RANK, MLA_HEADS, MLA_NOPE)
        dw_uv = matmul(r["ckvn"], dvp, "tn", f"l{l}_uv_dw").reshape(MLA_KV_RANK, MLA_HEADS, MLA_V)
        gw["w_mla_ukv"] = jnp.concatenate([dw_uk, dw_uv], axis=2).reshape(MLA_KV_RANK, -1)
        dsegs["ckv"], g["mla_kv_lat_g"] = rb(f"l{l}_kv_lat_b", fn_rms, [_row(MLA_KV_RANK), PAR], [r["ckv"], p["mla_kv_lat_g"]],
                                             [(MLA_KV_RANK, 0, BF16)], [dckvn])
        dqn, dqr, dgqn, dgqr = rb(f"l{l}_mla_q_b", fn_mla_q, mlaq_descs, [r["qn"], r["qr"], p["gqn"], p["gqr"], *cs_mla],
                                  [(MLA_QK_PAD, 1, BF16)], [dQm], MLA_HEADS)
        dcqn = matmul_sum(f"l{l}_uq_da", [(dqn, wt["uqn"]), (dqr, wt["uqr"])])
        dw_uqn = matmul(r["cqn"], dqn, "tn", f"l{l}_uqn_dw").reshape(MLA_Q_RANK, MLA_HEADS, MLA_NOPE)
        dw_uqr = matmul(r["cqn"], dqr, "tn", f"l{l}_uqr_dw").reshape(MLA_Q_RANK, MLA_HEADS, LANES)[:, :, :MLA_ROPE]
        gw["w_mla_uq"] = jnp.concatenate([dw_uqn, dw_uqr], axis=2).reshape(MLA_Q_RANK, -1)
        dsegs["cq"], g["mla_q_lat_g"] = rb(f"l{l}_q_lat_b", fn_rms, [_row(MLA_Q_RANK), PAR], [r["cq"], p["mla_q_lat_g"]],
                                           [(MLA_Q_RANK, 0, BF16)], [dcqn])
        g["mla_q_norm_g"] = jnp.concatenate([dgqn[0], dgqr[0, :MLA_ROPE]])
        g["mla_k_norm_g"] = jnp.concatenate([dgkn[0], dgkr[0, :MLA_ROPE]])
        dQs, dKs, dVs, dsink = attn_bwd(f"l{l}_swa", cfg_swa, r["Qs"], r["Ks"], r["Vs"], r["Os"], r["lse_s"], dOs, p["sink"])
        g["swa_sink"] = dsink.reshape(SWA_Q_HEADS)
        dsegs["qs"], g["swa_q_norm_g"] = rb(f"l{l}_swa_q_b", fn_swa_q, swaq_descs, [r["qs"], p["swa_q_norm_g"], *cs_swa],
                                            [(SWA_HEAD_DIM, 1, BF16)], [dQs], SWA_Q_HEADS)
        dsegs["ks"], dsegs["vs"], g["swa_k_norm_g"] = rb(f"l{l}_swa_kv_b", fn_swa_kv, swakv_descs,
                                                         [r["ks"], r["vs"], p["swa_k_norm_g"], *cs_swa],
                                                         [(SWA_HEAD_DIM, 1, BF16), (SWA_HEAD_DIM, 1, BF16)], [dKs, dVs], SWA_KV_HEADS)
        dy, dxs, dsegs["z"], dd_lane, g["ssm_norm_g"] = rb(
            f"l{l}_ssd_out_b", fn_ssd_out, ssdout_descs, [r["y0"], r["y1"], r["u"], r["z"], p["d_lane"], p["ssm_norm_g"]],
            [(SSM_INNER, 0, F32)], [dys])
        g["ssm_d"] = dd_lane.reshape(SSM_HEADS, SSM_HEAD_DIM).sum(axis=1)
        du, ddts, dalog = None, [], []
        for d in range(2):
            du, ddt, ddtt, dar, dac = ssd_bwd(f"l{l}_ssd{d}_b", d, r["u"], r[f"dt{d}"], r[f"dtt{d}"], p["alog_row"][d], p["alog_col"][d],
                                              r[f"hin{d}"], dy, L, add_x=dxs if d == 0 else None, add_u=du)
            ddts.append(ddt + ddtt.T)
            dalog.append(dar[0] + dac[:, 0])
        g["ssm_a_log"] = jnp.stack(dalog)
        dsegs["xbc"], dconv_w, g["ssm_conv_b"] = conv_bwd(f"l{l}_conv_b", r["xbc"], conv_w8[l], p["ssm_conv_b"], du, L)
        dsegs["dt"], ddt_bias = rb(f"l{l}_softplus_b", fn_softplus, [_row(2 * SSM_HEADS), PAR], [r["dt"], p["dt_bias"]],
                                   [(2 * SSM_HEADS, 0, F32)], [jnp.concatenate(ddts, axis=1)])
        g["ssm_dt_bias"] = ddt_bias.reshape(2, SSM_HEADS)
        g["ssm_conv_w"] = dconv_w[:SSM_CONV]
        dh1, dws = None, []
        for gi, group in enumerate(seg_groups):
            dh1 = matmul_sum(f"l{l}_in_da{gi}", [(dsegs[sn], wt[sn]) for sn in group], add=dh1)
        for sn, w in zip(seg_names, in_widths):
            dws.append(matmul(r["h1"], dsegs[sn], "tn", f"l{l}_in_{sn}_dw")[:, :w])
        gw["w_in"] = jnp.concatenate(dws, axis=1)
        dX, g["norm1_g"], dmod[0], dmod[1] = rb(f"l{l}_norm1_b", fn_norm_mod, nm_descs, [r["X"], p["norm1_g"], sh1, sc1],
                                               [(D, 0, BF16)], [dh1], add={0: dX1})
        for n in ("norm1_g", "norm2_g", "ssm_conv_b", "ssm_norm_g", "swa_q_norm_g", "swa_k_norm_g", "mla_q_lat_g", "mla_kv_lat_g"):
            g[n] = g[n][0]
        dmod_lat = jnp.concatenate([dm[0, 0] for dm in dmod])
        dmod_ctx = jnp.concatenate([dm[1, 0] for dm in dmod])
        return dX, g, gw, dmod_lat, dmod_ctx

    X = jnp.concatenate([x, ctx], axis=0)
    saved = []
    for l in range(depth):
        wt, p, mods = layer_weights(l), layer_params(l), layer_mods(l)
        X, r = layer_fwd(l, X, wt, p, mods)
        saved.append((r, wt, p, mods))
    loss_part, dX = loss_and_grad("loss", X, loss_target[0], L)
    loss = lax.psum(loss_part[0, 0], ("x", "y", "c"))
    small_g = [None] * depth
    big_g = [None] * depth
    dmods = [None] * depth
    for l in reversed(range(depth)):
        r, wt, p, mods = saved[l]
        dX, small_g[l], big_g[l], dm_lat, dm_ctx = layer_bwd(l, dX, r, wt, p, mods)
        dmods[l] = jnp.stack([dm_lat, dm_ctx])
    grad_x = dX[:L][None]

    dm_all = all_gather_vmem("gather_dmods", jnp.stack(dmods).reshape(-1, LANES)).reshape(N_DEV, depth, 2, N_DEV * mod_cols)
    dm_rows = jnp.concatenate([jnp.moveaxis(dm_all[:, :, 0], 0, 1), dm_all[:, :, 1].sum(axis=0)[:, None],
                               jnp.zeros((depth, S_rows - N_DEV - 1, N_DEV * mod_cols), F32)], axis=1)
    dm_mine = lax.dynamic_slice(dm_rows, (0, 0, me * mod_cols), (depth, S_rows, mod_cols))
    grads = {}
    grads["w_mod"] = jnp.stack([matmul(S_mat, dm_mine[l], "tn", f"mod{l}_dw") for l in range(depth)])
    d_silu = None
    for l in range(depth):
        d_silu = matmul(dm_mine[l], inp["w_mod"][l], "nt", f"mod{l}_da", add=d_silu)
    small = {n: jnp.stack([small_g[l][n] for l in range(depth)]) for n in small_g[0]}
    small["c_ctx"] = small_bwd("silu_c_b", fn_silu, [inp["c_ctx"][None]], [d_silu[N_DEV:N_DEV + 1]])[0][0]
    small["b_mod"] = jnp.stack(dmods).sum(axis=1)

    rep_shapes = [inp[n].shape for n in REPLICATED]
    conv_shape = (depth, SSM_CONV, SSM_CONV_DIM)
    packed = _pack([small[n] for n in REPLICATED] + [small["ssm_conv_w"]], 8 * LANES)
    small_sum = sum_parts("sum_small", all_gather_vmem("gather_small", packed))
    summed = _unpack(small_sum, rep_shapes + [conv_shape])
    for n, gsum in zip(REPLICATED, summed):
        grads[n] = gsum
    grads["ssm_conv_w"] = lax.dynamic_slice(summed[-1], (0, 0, me * cw[2]), cw)

    slabs = []
    for n in GATHERED:
        gfull = jnp.stack([big_g[l][n] for l in range(depth)])
        if n in COLUMN_SHARDED:
            k_dim, n_dim = gfull.shape[1], gfull.shape[2]
            slab = jnp.moveaxis(gfull.reshape(depth, k_dim, N_DEV, n_dim // N_DEV), 2, 0)
        else:
            k_dim, n_dim = gfull.shape[1], gfull.shape[2]
            slab = jnp.moveaxis(gfull.reshape(depth, N_DEV, k_dim // N_DEV, n_dim), 1, 0)
        slabs.append(cast_bf16(f"cast_{n}", slab.reshape(-1, slab.shape[-1])).reshape(slab.shape))
    for n, parts in zip(GATHERED, exchange_hbm("exchange_grads", slabs)):
        shp = inp[n].shape
        grads[n] = sum_parts(f"sum_{n}", parts.reshape(N_DEV, shp[0] * shp[1], shp[2])).reshape(shp)

    delta, new_m, new_v = {}, {}, {}
    rep_pack = lambda d: _pack([d[n] for n in REPLICATED], 8 * LANES)
    rep_out = adamw("adamw_small", rep_pack(inp), rep_pack(grads), rep_pack(mom_m), rep_pack(mom_v))
    for out, res in zip((delta, new_m, new_v), rep_out):
        for n, a in zip(REPLICATED, _unpack(res, rep_shapes)):
            out[n] = a
    for n in ["w_mod", "ssm_conv_w"] + GATHERED:
        shp = inp[n].shape
        two_d = (shp[0] * shp[1], shp[2])
        res = adamw(f"adamw_{n}", inp[n].reshape(two_d), grads[n].reshape(two_d), mom_m[n].reshape(two_d), mom_v[n].reshape(two_d))
        delta[n], new_m[n], new_v[n] = [a.reshape(shp) for a in res]

    return (loss, grad_x, *[grads[n] for n in WEIGHT_NAMES], *[delta[n] for n in WEIGHT_NAMES],
            *[new_m[n] for n in WEIGHT_NAMES], *[new_v[n] for n in WEIGHT_NAMES])
```

```python
import functools
import math

import numpy as np
import jax
import jax.numpy as jnp
from jax import lax
from jax.experimental import pallas as pl
from jax.experimental.pallas import tpu as pltpu

F32 = jnp.float32
BF16 = jnp.bfloat16

N_DEV = 8
V7X_VMEM_BYTES = 64 * 1024 * 1024
VMEM_LIMIT_BYTES = V7X_VMEM_BYTES - 8 * 1024 * 1024
LANES = 128

EPS = 1e-6
ROPE_BASE = 10000.0
GRID_W = 64
SSM_HEADS, SSM_HEAD_DIM, SSM_GROUPS, SSM_STATE, SSM_CONV, SSM_CHUNK = 16, 64, 2, 128, 5, 128
SSM_INNER = SSM_HEADS * SSM_HEAD_DIM
SSM_CONV_DIM = SSM_INNER + 2 * SSM_GROUPS * SSM_STATE
SWA_Q_HEADS, SWA_KV_HEADS, SWA_HEAD_DIM, SWA_WINDOW = 8, 2, 128, 128
MLA_HEADS, MLA_Q_RANK, MLA_KV_RANK, MLA_NOPE, MLA_ROPE, MLA_V = 8, 384, 256, 128, 64, 128
MLA_QK = MLA_NOPE + MLA_ROPE
MLA_QK_PAD = 2 * LANES
ADAM_LR, ADAM_B1, ADAM_B2, ADAM_EPS, ADAM_WD, ADAM_STEP = 0.001, 0.9, 0.999, 1e-08, 0.01, 10

ROW_TILE = 256


def _cparams(sem, **kw):
    return pltpu.CompilerParams(dimension_semantics=sem, vmem_limit_bytes=VMEM_LIMIT_BYTES, **kw)


def _pick(dim, prefs):
    for p in prefs:
        if dim % p == 0:
            return p
    return dim


def matmul(a, b, mode, name, out_dtype=F32, add=None):
    if mode == "nn":
        (M, K), (K2, N) = a.shape, b.shape
    elif mode == "nt":
        (M, K), (N, K2) = a.shape, b.shape
    else:
        (K, M), (K2, N) = a.shape, b.shape
    assert K == K2, (name, a.shape, b.shape)
    has_add = add is not None
    tm, tn, tk = _matmul_tiles(M, N, K, a.dtype.itemsize, b.dtype.itemsize, jnp.dtype(out_dtype).itemsize, has_add,
                                   m_on_lanes=(mode == "tn"))
    nk = K // tk
    dims = {"nn": (((1,), (0,)), ((), ())), "nt": (((1,), (1,)), ((), ())), "tn": (((0,), (0,)), ((), ()))}[mode]
    a_spec = pl.BlockSpec((tk, tm), lambda i, j, k: (k, i)) if mode == "tn" else pl.BlockSpec((tm, tk), lambda i, j, k: (i, k))
    b_spec = pl.BlockSpec((tn, tk), lambda i, j, k: (j, k)) if mode == "nt" else pl.BlockSpec((tk, tn), lambda i, j, k: (k, j))
    o_spec = pl.BlockSpec((tm, tn), lambda i, j, k: (i, j))

    def body(*refs):
        a_ref, b_ref = refs[:2]
        c_ref = refs[2] if has_add else None
        o_ref = refs[3] if has_add else refs[2]
        part = lax.dot_general(a_ref[...].astype(BF16), b_ref[...].astype(BF16), dims, preferred_element_type=F32)
        if nk == 1:
            o_ref[...] = (part + c_ref[...] if has_add else part).astype(o_ref.dtype)
            return
        acc_ref = refs[-1]
        k = pl.program_id(2)

        @pl.when(k == 0)
        def _():
            acc_ref[...] = part + c_ref[...] if has_add else part

        @pl.when(k > 0)
        def _():
            acc_ref[...] += part

        @pl.when(k == nk - 1)
        def _():
            o_ref[...] = acc_ref[...].astype(o_ref.dtype)

    ins = [a, b] + ([add] if has_add else [])
    in_specs = [a_spec, b_spec] + ([o_spec] if has_add else [])
    return pl.pallas_call(
        body, name=name, grid=(M // tm, N // tn, nk), in_specs=in_specs, out_specs=o_spec,
        out_shape=jax.ShapeDtypeStruct((M, N), out_dtype),
        scratch_shapes=[pltpu.VMEM((tm, tn), F32)] if nk > 1 else [],
        input_output_aliases=({2: 0} if has_add else {}),
        compiler_params=_cparams(("parallel", "parallel", "arbitrary")),
    )(*ins)


def matmul_sum(name, pairs, add=None):
    M, N = pairs[0][0].shape[0], pairs[0][1].shape[1]
    n = len(pairs)
    has_add = add is not None
    resident = sum(2 * b.shape[0] * N * b.dtype.itemsize for _, b in pairs)
    tm = next((t for t in (768, 512, 384, 256, 128) if M % t == 0 and resident + sum(
        2 * t * a.shape[1] * a.dtype.itemsize + t * a.shape[1] * 2 for a, _ in pairs) + 6 * t * N * 4 <= MATMUL_VMEM_BUDGET), None)
    assert tm is not None, name

    def body(*refs):
        acc = refs[2 * n][...] if has_add else None
        for s in range(n):
            part = jnp.dot(refs[2 * s][...].astype(BF16), refs[2 * s + 1][...].astype(BF16), preferred_element_type=F32)
            acc = part if acc is None else acc + part
        refs[-1][...] = acc

    o_spec = pl.BlockSpec((tm, N), lambda i: (i, 0))
    in_specs, ins = [], []
    for a, b in pairs:
        in_specs += [pl.BlockSpec((tm, a.shape[1]), lambda i: (i, 0)), pl.BlockSpec(b.shape, lambda i: (0, 0))]
        ins += [a, b]
    if has_add:
        in_specs.append(o_spec)
        ins.append(add)
    return pl.pallas_call(body, name=name, grid=(M // tm,), in_specs=in_specs, out_specs=o_spec,
                          out_shape=jax.ShapeDtypeStruct((M, N), F32), input_output_aliases=({2 * n: 0} if has_add else {}),
                          compiler_params=_cparams(("parallel",)))(*ins)


def matmul_multi(name, a, bs):
    M, K = a.shape
    n = len(bs)
    resident = sum(2 * b.shape[0] * K * b.dtype.itemsize for b in bs)
    n_total = sum(b.shape[0] for b in bs)
    tm = next((t for t in (768, 512, 384, 256, 128) if M % t == 0 and
               resident + 2 * t * K * a.dtype.itemsize + 3 * t * n_total * 4 <= MATMUL_VMEM_BUDGET), None)
    assert tm is not None, name

    def body(*refs):
        lhs = refs[0][...].astype(BF16)
        for s in range(n):
            refs[1 + n + s][...] = lax.dot_general(lhs, refs[1 + s][...].astype(BF16), NT_DIMS, preferred_element_type=F32)

    in_specs = [pl.BlockSpec((tm, K), lambda i: (i, 0))] + [pl.BlockSpec(b.shape, lambda i: (0, 0)) for b in bs]
    return pl.pallas_call(
        body, name=name, grid=(M // tm,), in_specs=in_specs,
        out_specs=[pl.BlockSpec((tm, b.shape[0]), lambda i: (i, 0)) for b in bs],
        out_shape=[jax.ShapeDtypeStruct((M, b.shape[0]), F32) for b in bs], compiler_params=_cparams(("parallel",)))(a, *bs)


MATMUL_VMEM_BUDGET = 36 * 1024 * 1024


def _matmul_tiles(M, N, K, a_bytes, b_bytes, o_bytes, has_add, m_on_lanes=False):
    tk = K if K <= 1536 else _pick(K, (1408, 1024, 768, 704, 512, 256))
    nk = K // tk
    m_cands = [t for t in (1024, 768, 512, 384, 256, 128) if M % t == 0] or [M]
    if M % 768 and M % 1024:
        m_cands += [t for t in (1408, 704, 352) if M % t == 0 and not (m_on_lanes and t % LANES)]
    n_cands = [t for t in range(LANES, min(N, 2816) + 1, LANES) if N % t == 0] or [N]
    best = None
    for tm in m_cands:
        for tn in n_cands:
            pipeline = 2 * (tm * tk * a_bytes + tk * tn * b_bytes + tm * tn * o_bytes) + (2 * tm * tn * 4 if has_add else 0)
            temps = tm * tn * 4 * (2 if nk > 1 else 1) + (tm * tk * 2 if a_bytes == 4 else 0) + (tk * tn * 2 if b_bytes == 4 else 0)
            if pipeline + temps <= MATMUL_VMEM_BUDGET:
                score = (tm * tn, tn)
                if best is None or score > best[0]:
                    best = (score, tm, tn)
    if best is None:
        return m_cands[-1], n_cands[0], tk
    return best[1], best[2], tk


def _row_specs(descs, arrays, tm, nct, heads):
    specs = []
    for d, arr in zip(descs, arrays):
        if d[0] == "row":
            _, w, per_head, off, _ = d
            specs.append(pl.BlockSpec((tm, w * (heads if per_head else 1)), lambda i, off=off: (i, off)))
        elif d[0] == "par":
            specs.append(pl.BlockSpec(arr.shape, lambda i, nd=arr.ndim: (0,) * nd))
        else:
            specs.append(pl.BlockSpec((1,) + arr.shape[1:], lambda i, nd=arr.ndim: (jnp.where(i >= nct, 1, 0),) + (0,) * (nd - 1)))
    return specs


def _load(d, ref, h):
    if d[0] == "grp":
        return ref[0]
    if d[0] == "row" and d[2]:
        return ref[:, h * d[1]:(h + 1) * d[1]]
    return ref[...]


def _out_specs(outs, tm, heads):
    return [pl.BlockSpec((tm, w * (heads if ph else 1)), lambda i: (i, 0)) for (w, ph, _) in outs]


def rowop_fwd(name, fn, descs, arrays, outs, T, n_ctx, heads=1, tm=ROW_TILE):
    nct = n_ctx // tm
    n_in = len(descs)

    def body(*refs):
        for h in range(heads):
            res = fn(*[_load(d, r, h) for d, r in zip(descs, refs[:n_in])])
            for o_ref, r, (w, ph, _) in zip(refs[n_in:], res, outs):
                if ph:
                    o_ref[:, h * w:(h + 1) * w] = r.astype(o_ref.dtype)
                else:
                    o_ref[...] = r.astype(o_ref.dtype)

    out_shape = [jax.ShapeDtypeStruct((T, w * (heads if ph else 1)), dt) for (w, ph, dt) in outs]
    return pl.pallas_call(
        body, name=name, grid=(T // tm,), in_specs=_row_specs(descs, arrays, tm, nct, heads), out_specs=_out_specs(outs, tm, heads),
        out_shape=out_shape, compiler_params=_cparams(("parallel",)),
    )(*arrays)


def rowop_bwd(name, fn, descs, arrays, outs, cts, T, n_ctx, heads=1, tm=ROW_TILE, add=None):
    nct = n_ctx // tm
    n_in, n_ct = len(descs), len(cts)
    add = add or {}
    diff_idx = [k for k, d in enumerate(descs) if d[-1]]
    add_idx = [k for k in diff_idx if k in add]

    def body(*refs):
        in_refs, ct_refs = refs[:n_in], refs[n_in:n_in + n_ct]
        add_refs = dict(zip(add_idx, refs[n_in + n_ct:n_in + n_ct + len(add_idx)]))
        g_refs = refs[n_in + n_ct + len(add_idx):]
        i = pl.program_id(0)
        shared = {}
        for h in range(heads):
            vals = [_load(d, r, h) for d, r in zip(descs, in_refs)]

            def f(*dvals, vals=vals):
                full = list(vals)
                for k, v in zip(diff_idx, dvals):
                    full[k] = v
                return tuple(fn(*full))

            _, vjp = jax.vjp(f, *[vals[k] for k in diff_idx])
            cts_h = tuple(c[:, h * w:(h + 1) * w] if ph else c[...] for c, (w, ph, _) in zip(ct_refs, outs))
            for k, g_ref, g in zip(diff_idx, g_refs, vjp(cts_h)):
                d = descs[k]
                if d[0] == "row" and d[2]:
                    g_ref[:, h * d[1]:(h + 1) * d[1]] = g.astype(g_ref.dtype)
                else:
                    shared[k] = g if k not in shared else shared[k] + g
        for k, g_ref in zip(diff_idx, g_refs):
            d = descs[k]
            if k not in shared:
                continue
            g = shared[k]
            if d[0] == "row":
                if k in add_refs:
                    g = g + add_refs[k][...]
                g_ref[...] = g.astype(g_ref.dtype)
            elif d[0] == "par":
                _accumulate(g_ref, g, i == 0)
            else:
                _accumulate(g_ref, g[None], jnp.logical_or(i == 0, i == nct))

    in_specs = _row_specs(descs, arrays, tm, nct, heads)
    g_specs, g_shape = [], []
    for k in diff_idx:
        d = descs[k]
        if d[0] == "row":
            g_specs.append(pl.BlockSpec((tm, d[1] * (heads if d[2] else 1)), lambda i: (i, 0)))
            g_shape.append(jax.ShapeDtypeStruct((T, d[1] * (heads if d[2] else 1)), F32))
        else:
            g_specs.append(in_specs[k])
            g_shape.append(jax.ShapeDtypeStruct(arrays[k].shape, F32))
    add_specs = [g_specs[diff_idx.index(k)] for k in add_idx]
    return pl.pallas_call(
        body, name=name, grid=(T // tm,), in_specs=in_specs + _out_specs(outs, tm, heads) + add_specs, out_specs=g_specs,
        out_shape=g_shape, compiler_params=_cparams(("arbitrary",)),
    )(*arrays, *cts, *[add[k] for k in add_idx])


def _accumulate(ref, val, first):
    @pl.when(first)
    def _():
        ref[...] = val.astype(ref.dtype)

    @pl.when(jnp.logical_not(first))
    def _():
        ref[...] += val.astype(ref.dtype)


def _rms(x, count=None):
    n = x.shape[-1] if count is None else count
    return x * lax.rsqrt(jnp.sum(x * x, axis=-1, keepdims=True) * (1.0 / n) + EPS)


def _swap_halves(x, nf):
    w = x.shape[-1]
    lane = lax.broadcasted_iota(jnp.int32, x.shape, x.ndim - 1)
    return jnp.where((lane % (2 * nf)) < nf, pltpu.roll(x, w - nf, x.ndim - 1), pltpu.roll(x, nf, x.ndim - 1))


def _make_rope(nf):
    @jax.custom_vjp
    def rope(x, c, s):
        return x * c + _swap_halves(x, nf) * s

    def fwd(x, c, s):
        return rope(x, c, s), (c, s)

    def bwd(res, g):
        c, s = res
        return g * c + _swap_halves(g * s, nf), jnp.zeros_like(c), jnp.zeros_like(s)

    rope.defvjp(fwd, bwd)
    return rope


_rope_swa = _make_rope(SWA_HEAD_DIM // 4)
_rope_mla = _make_rope(MLA_ROPE // 4)


@jax.custom_vjp
def _softplus(x):
    e = jnp.exp(-jnp.abs(x))
    u = 1.0 + e
    log1p_e = jnp.where(u == 1.0, e, jnp.log(u) * e / jnp.where(u == 1.0, 1.0, u - 1.0))
    return jnp.maximum(x, 0.0) + log1p_e


_softplus.defvjp(lambda x: (_softplus(x), x), lambda x, g: (g * jax.nn.sigmoid(x),))


def fn_norm_mod(x, g, shift, scale):
    return (_rms(x) * g * (1.0 + scale) + shift,)


def fn_rms(x, g):
    return (_rms(x) * g,)


def fn_resid(x, a, gate):
    return (x + gate * a,)


def fn_softplus(dt, bias):
    return (_softplus(dt + bias),)


def fn_ssd_out(yf, yb, xs, z, d_lane, g):
    y = yf + yb + d_lane * xs
    return (_rms(y * (z * jax.nn.sigmoid(z))) * g,)


def fn_swa_q(q, g, c, s):
    return (_rope_swa(_rms(q) * g, c, s),)


def fn_swa_kv(k, v, g, c, s):
    return (_rope_swa(_rms(k) * g, c, s), v)


def fn_mla_q(qn, qr, gn, gr, c, s):
    return (jnp.concatenate([_rms(qn) * gn, _rope_mla(_rms(qr, MLA_ROPE) * gr, c, s)], axis=-1),)


def fn_mla_kv(kn, v, kr, gn, gr, c, s):
    return (jnp.concatenate([_rms(kn) * gn, _rope_mla(_rms(kr, MLA_ROPE) * gr, c, s)], axis=-1), v)


def fn_merge(g1, g2, g3, p1, p2, p3):
    return (jax.nn.sigmoid(g1) * p1 + jax.nn.sigmoid(g2) * p2 + jax.nn.sigmoid(g3) * p3,)


def fn_swiglu(g, u):
    return (g * jax.nn.sigmoid(g) * u,)


ATTN_TILE = 256
NT_DIMS = (((1,), (1,)), ((), ()))


class AttnCfg:
    def __init__(self, hq, group, dq, dv, scale, window, has_sink, L, T, chunk, kv_block):
        self.hq, self.group, self.dq, self.dv, self.scale = hq, group, dq, dv, scale
        self.window, self.has_sink, self.L, self.T = window, has_sink, L, T
        self.chunk = _pick(L, (chunk, ATTN_TILE))
        self.ctx_chunk = T - L
        self.kv_block = kv_block
        self.q_block = kv_block * group
        assert L % ATTN_TILE == 0 and (T - L) % ATTN_TILE == 0 and L % self.chunk == 0
        assert (hq // group) % kv_block == 0
        if window is not None:
            assert (ATTN_TILE + 2 * window) % self.chunk == 0
            self.window_chunks = min((ATTN_TILE + 2 * window) // self.chunk, L // self.chunk)
            self.align = math.gcd(self.chunk, window)
        else:
            self.align = self.chunk


LOG2E = math.log2(math.e)


def _latent_chunks(cfg, r0):
    c = cfg.chunk
    if cfg.window is None:
        lo, n = 0, cfg.L // c
    else:
        n = cfg.window_chunks
        lo = jnp.clip(r0 - cfg.window, 0, cfg.L - n * c)
    return lo, n


def _visible(cfg, rows_q, rows_k):
    return jnp.logical_or(rows_k >= cfg.L, jnp.abs(rows_k - rows_q) <= cfg.window)


def flash_fwd(name, cfg, q, k, v, sink):
    T, tq, c = cfg.T, ATTN_TILE, cfg.chunk
    hq, g, dq, dv, hb, kb = cfg.hq, cfg.group, cfg.dq, cfg.dv, cfg.q_block, cfg.kv_block
    to_log2 = cfg.scale * LOG2E

    def body(*refs):
        if cfg.has_sink:
            q_ref, k_ref, v_ref, sink_ref, o_ref, lse_ref = refs
        else:
            q_ref, k_ref, v_ref, o_ref, lse_ref = refs
        q0 = pl.program_id(1) * tq
        qs = [q_ref[:, hh * dq:(hh + 1) * dq] for hh in range(hb)]
        lat_lo, lat_n = _latent_chunks(cfg, q0)
        n = jnp.where(q0 >= cfg.L, 0, lat_n)
        rows_q = q0 + lax.broadcasted_iota(jnp.int32, (tq, 1), 0)

        def start(t):
            return pl.multiple_of(lat_lo + jnp.minimum(t, lat_n - 1) * c, cfg.align)

        def logits(ks, size):
            return tuple(lax.dot_general(qs[hh], k_ref[pl.ds(ks, size), (hh // g) * dq:(hh // g + 1) * dq], NT_DIMS,
                                         preferred_element_type=F32) for hh in range(hb))

        def update(state, s_all, ks, size, masked):
            new_state = []
            for hh in range(hb):
                m, acc = state[hh]
                s = s_all[hh]
                if masked:
                    rows_k = ks + lax.broadcasted_iota(jnp.int32, (1, size), 1)
                    s = jnp.where(_visible(cfg, rows_q, rows_k), s, -jnp.inf)
                m_new = jnp.maximum(m, jnp.max(s, axis=-1, keepdims=True) * to_log2)
                alpha = jnp.exp2(m - m_new)
                p = jnp.exp2(s * to_log2 - m_new).astype(BF16)
                kh = hh // g
                v_ones = jnp.concatenate([v_ref[pl.ds(ks, size), kh * dv:(kh + 1) * dv], jnp.ones((size, dv), BF16)], axis=1)
                acc = alpha * acc + jnp.dot(p, v_ones, preferred_element_type=F32)
                new_state.append((m_new, acc))
            return tuple(new_state)

        def step(t, carry):
            state, s_all = carry
            s_next = logits(start(t + 1), c)
            return update(state, s_all, start(t), c, cfg.window is not None), s_next

        state = []
        for hh in range(hb):
            if cfg.has_sink:
                m0 = jnp.zeros((tq, 1), F32) + sink_ref[hh] * LOG2E
                l0 = jnp.ones((tq, dv), F32)
            else:
                m0 = jnp.full((tq, 1), -jnp.inf, F32)
                l0 = jnp.zeros((tq, dv), F32)
            state.append((m0, jnp.concatenate([jnp.zeros((tq, dv), F32), l0], axis=1)))
        state = update(tuple(state), logits(cfg.L, cfg.ctx_chunk), cfg.L, cfg.ctx_chunk, False)
        state, _ = lax.fori_loop(0, n, step, (state, logits(start(0), c)))
        for hh in range(hb):
            m, acc = state[hh]
            o_ref[:, hh * dv:(hh + 1) * dv] = acc[:, :dv] / acc[:, dv:]
            lse_ref[hh] = m + jnp.log2(acc[:, dv:dv + 1])

    in_specs = [pl.BlockSpec((tq, hb * dq), lambda h, i: (i, h)),
                pl.BlockSpec((T, kb * dq), lambda h, i: (0, h)),
                pl.BlockSpec((T, kb * dv), lambda h, i: (0, h))]
    ins = [q, k, v]
    if cfg.has_sink:
        in_specs.append(pl.BlockSpec((hb, 1, 1), lambda h, i: (h, 0, 0)))
        ins.append(sink)
    return pl.pallas_call(
        body, name=name, grid=(hq // hb, T // tq), in_specs=in_specs,
        out_specs=[pl.BlockSpec((tq, hb * dv), lambda h, i: (i, h)), pl.BlockSpec((hb, tq, 1), lambda h, i: (h, i, 0))],
        out_shape=[jax.ShapeDtypeStruct((T, hq * dv), F32), jax.ShapeDtypeStruct((hq, T, 1), F32)],
        compiler_params=_cparams(("parallel", "parallel")),
    )(*ins)


def attn_delta(name, cfg, o, do, lse, sink):
    T, tm, hq, dv = cfg.T, ATTN_TILE, cfg.hq, cfg.dv

    def body(*refs):
        if cfg.has_sink:
            o_ref, do_ref, lse_ref, sink_ref, delta_ref, dob_ref, dsink_ref = refs
        else:
            o_ref, do_ref, delta_ref, dob_ref = refs
        dob_ref[...] = do_ref[...].astype(BF16)
        parts = []
        for h in range(hq):
            delta = jnp.sum(do_ref[:, h * dv:(h + 1) * dv] * o_ref[:, h * dv:(h + 1) * dv], axis=-1, keepdims=True)
            delta_ref[h] = delta
            if cfg.has_sink:
                parts.append(-jnp.sum(jnp.exp2(sink_ref[h] * LOG2E - lse_ref[h]) * delta, axis=0, keepdims=True)[None])
        if cfg.has_sink:
            _accumulate(dsink_ref, jnp.concatenate(parts, axis=0), pl.program_id(0) == 0)

    head_tile = pl.BlockSpec((tm, hq * dv), lambda i: (i, 0))
    col = pl.BlockSpec((hq, tm, 1), lambda i: (0, i, 0))
    one = pl.BlockSpec((hq, 1, 1), lambda i: (0, 0, 0))
    in_specs, ins = [head_tile, head_tile], [o, do]
    out_specs = [col, head_tile]
    out_shape = [jax.ShapeDtypeStruct((hq, T, 1), F32), jax.ShapeDtypeStruct((T, hq * dv), BF16)]
    if cfg.has_sink:
        in_specs += [col, one]
        ins += [lse, sink]
        out_specs.append(one)
        out_shape.append(jax.ShapeDtypeStruct((hq, 1, 1), F32))
    return pl.pallas_call(body, name=name, grid=(T // tm,), in_specs=in_specs, out_specs=out_specs, out_shape=out_shape,
                          compiler_params=_cparams(("arbitrary",)))(*ins)


def flash_bwd_fused(name, cfg, q, k, v, dob, lse, delta):
    T, L, tq, c, cc = cfg.T, cfg.L, ATTN_TILE, cfg.chunk, cfg.ctx_chunk
    hq, g, dq, dv = cfg.hq, cfg.group, cfg.dq, cfg.dv
    hk = hq // g
    nq = T // tq
    to_log2 = cfg.scale * LOG2E
    masked = cfg.window is not None

    def body(q_ref, k_ref, v_ref, do_ref, lse_ref, delta_ref, dq_ref, dk_ref, dv_ref):
        i = pl.program_id(1)
        q0 = i * tq

        @pl.when(i == 0)
        def _():
            dk_ref[...] = jnp.zeros_like(dk_ref)
            dv_ref[...] = jnp.zeros_like(dv_ref)

        qs = [q_ref[:, hh * dq:(hh + 1) * dq] for hh in range(g)]
        dos = [do_ref[:, hh * dv:(hh + 1) * dv] for hh in range(g)]
        lses = [lse_ref[hh] for hh in range(g)]
        deltas = [delta_ref[hh] for hh in range(g)]
        lat_lo, lat_n = _latent_chunks(cfg, q0)
        n = jnp.where(q0 >= L, 0, lat_n)
        rows_q = q0 + lax.broadcasted_iota(jnp.int32, (tq, 1), 0)

        def start(t):
            return pl.multiple_of(lat_lo + jnp.minimum(t, lat_n - 1) * c, cfg.align)

        def products(ks, size):
            kk, vv = k_ref[pl.ds(ks, size), :], v_ref[pl.ds(ks, size), :]
            return tuple((lax.dot_general(qs[hh], kk, NT_DIMS, preferred_element_type=F32),
                          lax.dot_general(dos[hh], vv, NT_DIMS, preferred_element_type=F32)) for hh in range(g))

        def update(accs, prods, ks, size, mask_it):
            new_accs, dv_part, dk_part = [], None, None
            for hh in range(g):
                s, dp = prods[hh]
                p = jnp.exp2(s * to_log2 - lses[hh])
                if mask_it:
                    rows_k = ks + lax.broadcasted_iota(jnp.int32, (1, size), 1)
                    p = jnp.where(_visible(cfg, rows_q, rows_k), p, 0.0)
                ds = (p * (dp - deltas[hh])).astype(BF16)
                dv_h = lax.dot_general(p.astype(BF16), dos[hh], TN_DIMS, preferred_element_type=F32)
                dk_h = lax.dot_general(ds, qs[hh], TN_DIMS, preferred_element_type=F32)
                dv_part = dv_h if dv_part is None else dv_part + dv_h
                dk_part = dk_h if dk_part is None else dk_part + dk_h
                new_accs.append(accs[hh] + jnp.dot(ds, k_ref[pl.ds(ks, size), :], preferred_element_type=F32))
            dv_ref[pl.ds(ks, size), :] += dv_part
            dk_ref[pl.ds(ks, size), :] += dk_part
            return tuple(new_accs)

        def step(t, carry):
            accs, prods = carry
            nxt = products(start(t + 1), c)
            return update(accs, prods, start(t), c, masked), nxt

        accs = update(tuple(jnp.zeros((tq, dq), F32) for _ in range(g)), products(L, cc), L, cc, False)
        accs, _ = lax.fori_loop(0, n, step, (accs, products(start(0), c)))
        for hh in range(g):
            dq_ref[:, hh * dq:(hh + 1) * dq] = accs[hh] * cfg.scale

        @pl.when(i == nq - 1)
        def _():
            dk_ref[...] = dk_ref[...] * cfg.scale

    col = pl.BlockSpec((g, tq, 1), lambda h, i: (h, i, 0))
    return pl.pallas_call(
        body, name=name, grid=(hk, nq),
        in_specs=[pl.BlockSpec((tq, g * dq), lambda h, i: (i, h)),
                  pl.BlockSpec((T, dq), lambda h, i: (0, h)),
                  pl.BlockSpec((T, dv), lambda h, i: (0, h)),
                  pl.BlockSpec((tq, g * dv), lambda h, i: (i, h)), col, col],
        out_specs=[pl.BlockSpec((tq, g * dq), lambda h, i: (i, h)),
                   pl.BlockSpec((T, dq), lambda h, i: (0, h)),
                   pl.BlockSpec((T, dv), lambda h, i: (0, h))],
        out_shape=[jax.ShapeDtypeStruct((T, hq * dq), F32), jax.ShapeDtypeStruct((T, hk * dq), F32),
                   jax.ShapeDtypeStruct((T, hk * dv), F32)],
        compiler_params=_cparams(("arbitrary", "arbitrary")),
    )(q, k, v, dob, lse, delta)


HALO = 8


def _conv_specs(tm, C, T):
    nb = tm // HALO
    last = T // HALO - 1
    return [pl.BlockSpec((HALO, C), lambda i: (jnp.maximum(i * nb - 1, 0), 0)),
            pl.BlockSpec((tm, C), lambda i: (i, 0)),
            pl.BlockSpec((HALO, C), lambda i: (jnp.minimum((i + 1) * nb, last), 0))]


def _extended(prev_ref, cur_ref, next_ref, i, tm, L, T):
    r0 = i * tm
    keep_prev = jnp.logical_and(r0 != 0, r0 != L).astype(F32)
    keep_next = jnp.logical_and(r0 + tm != L, r0 + tm != T).astype(F32)
    return jnp.concatenate([prev_ref[...] * keep_prev, cur_ref[...], next_ref[...] * keep_next], axis=0)


def _shift_rows(xe, d):
    n = xe.shape[0]
    return xe if d == 0 else pltpu.roll(xe, (-d) % n, 0)


def _conv_pre(xe, w_ref, b_ref):
    acc = b_ref[...] + w_ref[SSM_CONV // 2:SSM_CONV // 2 + 1, :] * xe
    for k in range(SSM_CONV):
        if k != SSM_CONV // 2:
            acc = acc + w_ref[k:k + 1, :] * _shift_rows(xe, k - SSM_CONV // 2)
    return acc


def conv_fwd(name, x, w, b, L, tm=ROW_TILE):
    T, C = x.shape

    def body(xp, xc, xn, w_ref, b_ref, o_ref):
        xe = _extended(xp, xc, xn, pl.program_id(0), tm, L, T)
        pre = _conv_pre(xe, w_ref, b_ref)[HALO:HALO + tm]
        o_ref[...] = pre * jax.nn.sigmoid(pre)

    full = lambda a: pl.BlockSpec(a.shape, lambda i: (0, 0))
    return pl.pallas_call(body, name=name, grid=(T // tm,), in_specs=_conv_specs(tm, C, T) + [full(w), full(b)],
                          out_specs=pl.BlockSpec((tm, C), lambda i: (i, 0)), out_shape=jax.ShapeDtypeStruct((T, C), F32),
                          compiler_params=_cparams(("parallel",)))(x, x, x, w, b)


def conv_bwd(name, x, w, b, gu, L, tm=ROW_TILE):
    T, C = x.shape

    def body(xp, xc, xn, gp, gc, gn, w_ref, b_ref, dx_ref, dw_ref, db_ref):
        i = pl.program_id(0)
        xe = _extended(xp, xc, xn, i, tm, L, T)
        ge = _extended(gp, gc, gn, i, tm, L, T)
        pre = _conv_pre(xe, w_ref, b_ref)
        sg = jax.nn.sigmoid(pre)
        gpre = ge * (sg * (1.0 + pre * (1.0 - sg)))
        half = SSM_CONV // 2
        dx = jnp.zeros((tm, C), F32)
        rows = []
        for k in range(SSM_CONV):
            dx = dx + w_ref[k:k + 1, :] * _shift_rows(gpre, half - k)[HALO:HALO + tm]
            rows.append(jnp.sum(gpre[HALO:HALO + tm] * _shift_rows(xe, k - half)[HALO:HALO + tm], axis=0, keepdims=True))
        dx_ref[...] = dx
        rows += [jnp.zeros((1, C), F32)] * (8 - SSM_CONV)
        _accumulate(dw_ref, jnp.concatenate(rows, axis=0), i == 0)
        _accumulate(db_ref, jnp.sum(gpre[HALO:HALO + tm], axis=0, keepdims=True), i == 0)

    full = lambda a: pl.BlockSpec(a.shape, lambda i: (0, 0))
    return pl.pallas_call(
        body, name=name, grid=(T // tm,), in_specs=_conv_specs(tm, C, T) * 2 + [full(w), full(b)],
        out_specs=[pl.BlockSpec((tm, C), lambda i: (i, 0)), pl.BlockSpec((8, C), lambda i: (0, 0)), pl.BlockSpec((1, C), lambda i: (0, 0))],
        out_shape=[jax.ShapeDtypeStruct((T, C), F32), jax.ShapeDtypeStruct((8, C), F32), jax.ShapeDtypeStruct((1, C), F32)],
        compiler_params=_cparams(("arbitrary",)))(x, x, x, gu, gu, gu, w, b)


SSM_PAIRS = SSM_HEADS // 2
TN_DIMS = (((0,), (0,)), ((), ()))
HIGHEST = lax.Precision.HIGHEST


def _ssd_chunk(direction, xps, bs, cs, dt_col, dt_row, alog_row, alog_col, hps):
    Q = SSM_CHUNK
    da_col = dt_col * (-jnp.exp(alog_row))
    da_row = dt_row * (-jnp.exp(alog_col))
    ii = lax.broadcasted_iota(jnp.int32, (Q, Q), 0)
    jj = lax.broadcasted_iota(jnp.int32, (Q, Q), 1)
    tri = (ii >= jj) if direction == 0 else (ii <= jj)
    trif = tri.astype(F32)
    acs_col = jnp.dot(trif, da_col, precision=HIGHEST, preferred_element_type=F32)
    acs_row = lax.dot_general(da_row, trif, NT_DIMS, precision=HIGHEST, preferred_element_type=F32)
    tot_col = jnp.sum(da_col, axis=0, keepdims=True)
    lane16 = lax.broadcasted_iota(jnp.int32, (1, SSM_HEADS), 1)
    sub16 = lax.broadcasted_iota(jnp.int32, (SSM_HEADS, 1), 0)
    low = lax.broadcasted_iota(jnp.int32, (1, 2 * SSM_HEAD_DIM), 1) < SSM_HEAD_DIM

    def col(v, h):
        return jnp.sum(v * (lane16 == h).astype(F32), axis=1, keepdims=True)

    def row(v, h):
        return jnp.sum(v * (sub16 == h).astype(F32), axis=0, keepdims=True)

    ys, hos = [], []
    pairs_per_group = SSM_PAIRS // SSM_GROUPS
    for g in range(SSM_GROUPS):
        bb, cb16 = bs[g].astype(BF16), cs[g].astype(BF16)
        cb = lax.dot_general(cb16, bb, NT_DIMS, preferred_element_type=F32)
        for pp in range(pairs_per_group):
            p = g * pairs_per_group + pp
            h0, h1 = 2 * p, 2 * p + 1
            ac0, ac1 = col(acs_col, h0), col(acs_col, h1)
            seg0 = jnp.exp(jnp.where(tri, ac0 - row(acs_row, h0), -jnp.inf))
            seg1 = jnp.exp(jnp.where(tri, ac1 - row(acs_row, h1), -jnp.inf))
            dt_l = jnp.where(low, col(dt_col, h0), col(dt_col, h1))
            ac_l = jnp.where(low, ac0, ac1)
            tot_l = jnp.where(low, col(tot_col, h0), col(tot_col, h1))
            xdt = xps[p] * dt_l
            y = (jnp.dot((cb * seg0).astype(BF16), jnp.where(low, xdt, 0.0).astype(BF16), preferred_element_type=F32)
                 + jnp.dot((cb * seg1).astype(BF16), jnp.where(low, 0.0, xdt).astype(BF16), preferred_element_type=F32))
            y = y + jnp.dot(cb16, hps[p].astype(BF16), preferred_element_type=F32) * jnp.exp(ac_l)
            st = lax.dot_general(bb, (xdt * jnp.exp(tot_l - ac_l)).astype(BF16), TN_DIMS, preferred_element_type=F32)
            ys.append(y)
            hos.append(hps[p] * jnp.exp(tot_l) + st)
    return tuple(ys), tuple(hos)


def _ssd_chunk_of(direction, step, ncl, ncc):
    if direction == 0:
        return jnp.where(step < ncc, ncl + step, step - ncc)
    return jnp.where(step < ncc, ncl + ncc - 1 - step, ncl - 1 - (step - ncc))


def _ssd_load(u_ref):
    Q = SSM_CHUNK
    xps = tuple(u_ref[:, LANES * p:LANES * (p + 1)] for p in range(SSM_PAIRS))
    bs = tuple(u_ref[:, SSM_INNER + SSM_STATE * g:SSM_INNER + SSM_STATE * (g + 1)] for g in range(SSM_GROUPS))
    c0 = SSM_INNER + SSM_GROUPS * SSM_STATE
    cs = tuple(u_ref[:, c0 + SSM_STATE * g:c0 + SSM_STATE * (g + 1)] for g in range(SSM_GROUPS))
    return xps, bs, cs


def ssd_fwd(name, direction, u, dt, dt_t, alog_row, alog_col, L):
    T = u.shape[0]
    Q, N = SSM_CHUNK, SSM_STATE
    ncl, ncc = L // Q, (T - L) // Q
    nc = ncl + ncc
    cm = lambda s: _ssd_chunk_of(direction, s, ncl, ncc)

    def body(u_ref, dt_ref, dtt_ref, ar_ref, ac_ref, y_ref, hin_ref, state):
        @pl.when(pl.program_id(0) == 0)
        def _():
            state[...] = jnp.zeros_like(state)

        xps, bs, cs = _ssd_load(u_ref)
        hps = tuple(state[p] for p in range(SSM_PAIRS))
        for p in range(SSM_PAIRS):
            hin_ref[0, p] = hps[p]
        ys, hos = _ssd_chunk(direction, xps, bs, cs, dt_ref[...], dtt_ref[...], ar_ref[...], ac_ref[...], hps)
        for p in range(SSM_PAIRS):
            y_ref[:, LANES * p:LANES * (p + 1)] = ys[p]
            state[p] = hos[p]

    return pl.pallas_call(
        body, name=name, grid=(nc,),
        in_specs=[pl.BlockSpec((Q, SSM_CONV_DIM), lambda s: (cm(s), 0)),
                  pl.BlockSpec((Q, SSM_HEADS), lambda s: (cm(s), 0)),
                  pl.BlockSpec((SSM_HEADS, Q), lambda s: (0, cm(s))),
                  pl.BlockSpec((1, SSM_HEADS), lambda s: (0, 0)),
                  pl.BlockSpec((SSM_HEADS, 1), lambda s: (0, 0))],
        out_specs=[pl.BlockSpec((Q, SSM_INNER), lambda s: (cm(s), 0)),
                   pl.BlockSpec((1, SSM_PAIRS, N, LANES), lambda s: (cm(s), 0, 0, 0))],
        out_shape=[jax.ShapeDtypeStruct((T, SSM_INNER), F32), jax.ShapeDtypeStruct((nc, SSM_PAIRS, N, LANES), F32)],
        scratch_shapes=[pltpu.VMEM((SSM_PAIRS, N, LANES), F32)],
        compiler_params=_cparams(("arbitrary",)),
    )(u, dt, dt_t, alog_row, alog_col)


def ssd_bwd(name, direction, u, dt, dt_t, alog_row, alog_col, hin, dy, L, add_x=None, add_u=None):
    T = u.shape[0]
    Q, N = SSM_CHUNK, SSM_STATE
    ncl, ncc = L // Q, (T - L) // Q
    nc = ncl + ncc
    cm = lambda s: _ssd_chunk_of(direction, nc - 1 - s, ncl, ncc)
    n_add = (add_x is not None) + (add_u is not None)

    def body(*refs):
        u_ref, dt_ref, dtt_ref, ar_ref, ac_ref, hin_ref, dy_ref = refs[:7]
        add_refs = refs[7:7 + n_add]
        du_ref, ddt_ref, ddtt_ref, dar_ref, dac_ref, dstate = refs[7 + n_add:]
        first = pl.program_id(0) == 0

        @pl.when(first)
        def _():
            dstate[...] = jnp.zeros_like(dstate)

        xps, bs, cs = _ssd_load(u_ref)
        hps = tuple(hin_ref[0, p] for p in range(SSM_PAIRS))
        _, vjp = jax.vjp(functools.partial(_ssd_chunk, direction), xps, bs, cs, dt_ref[...], dtt_ref[...], ar_ref[...],
                         ac_ref[...], hps)
        dys = tuple(dy_ref[:, LANES * p:LANES * (p + 1)] for p in range(SSM_PAIRS))
        dhs = tuple(dstate[p] for p in range(SSM_PAIRS))
        gx, gb, gc, gdt, gdtt, gar, gac, ghp = vjp((dys, dhs))
        parts = list(gx) + list(gb) + list(gc)
        du = jnp.concatenate(parts, axis=1)
        k = 0
        if add_x is not None:
            du = du + jnp.concatenate([add_refs[k][...], jnp.zeros((Q, SSM_CONV_DIM - SSM_INNER), F32)], axis=1)
            k += 1
        if add_u is not None:
            du = du + add_refs[k][...]
        du_ref[...] = du
        ddt_ref[...] = gdt
        ddtt_ref[...] = gdtt
        _accumulate(dar_ref, gar, first)
        _accumulate(dac_ref, gac, first)
        for p in range(SSM_PAIRS):
            dstate[p] = ghp[p]

    in_specs = [pl.BlockSpec((Q, SSM_CONV_DIM), lambda s: (cm(s), 0)),
                pl.BlockSpec((Q, SSM_HEADS), lambda s: (cm(s), 0)),
                pl.BlockSpec((SSM_HEADS, Q), lambda s: (0, cm(s))),
                pl.BlockSpec((1, SSM_HEADS), lambda s: (0, 0)),
                pl.BlockSpec((SSM_HEADS, 1), lambda s: (0, 0)),
                pl.BlockSpec((1, SSM_PAIRS, N, LANES), lambda s: (cm(s), 0, 0, 0)),
                pl.BlockSpec((Q, SSM_INNER), lambda s: (cm(s), 0))]
    ins = [u, dt, dt_t, alog_row, alog_col, hin, dy]
    if add_x is not None:
        in_specs.append(pl.BlockSpec((Q, SSM_INNER), lambda s: (cm(s), 0)))
        ins.append(add_x)
    if add_u is not None:
        in_specs.append(pl.BlockSpec((Q, SSM_CONV_DIM), lambda s: (cm(s), 0)))
        ins.append(add_u)
    return pl.pallas_call(
        body, name=name, grid=(nc,), in_specs=in_specs,
        out_specs=[pl.BlockSpec((Q, SSM_CONV_DIM), lambda s: (cm(s), 0)),
                   pl.BlockSpec((Q, SSM_HEADS), lambda s: (cm(s), 0)),
                   pl.BlockSpec((SSM_HEADS, Q), lambda s: (0, cm(s))),
                   pl.BlockSpec((1, SSM_HEADS), lambda s: (0, 0)),
                   pl.BlockSpec((SSM_HEADS, 1), lambda s: (0, 0))],
        out_shape=[jax.ShapeDtypeStruct((T, SSM_CONV_DIM), F32), jax.ShapeDtypeStruct((T, SSM_HEADS), F32),
                   jax.ShapeDtypeStruct((SSM_HEADS, T), F32), jax.ShapeDtypeStruct((1, SSM_HEADS), F32),
                   jax.ShapeDtypeStruct((SSM_HEADS, 1), F32)],
        scratch_shapes=[pltpu.VMEM((SSM_PAIRS, N, LANES), F32)],
        compiler_params=_cparams(("arbitrary",)),
    )(*ins)


PEER_MASKS = (1, 2, 4, 3, 5, 6, 7)
N_PEERS = len(PEER_MASKS)
MESH_IDS = pl.DeviceIdType.MESH


def _my_index():
    return lax.axis_index("x") * 4 + lax.axis_index("y") * 2 + lax.axis_index("c")


def _coords(idx):
    return (idx // 4, (idx // 2) % 2, idx % 2)


def all_gather_hbm(name, arrays):
    n = len(arrays)
    chip_masks = (4, 2, 6)

    def body(*refs):
        ins, outs = refs[:n], refs[n:2 * n]
        send_sems, recv_sems, local_sems = refs[2 * n:]
        me = _my_index()
        sibling = me ^ 1

        def copy(a, k, block, to, src=None):
            return pltpu.make_async_remote_copy(
                src_ref=outs[a].at[block] if src is None else src, dst_ref=outs[a].at[block],
                send_sem=send_sems.at[a * N_PEERS + k], recv_sem=recv_sems.at[a * N_PEERS + k],
                device_id=_coords(to), device_id_type=MESH_IDS)

        started, own = [], []
        for a in range(n):
            local = pltpu.make_async_copy(ins[a], outs[a].at[me], local_sems.at[a])
            local.start()
            own.append(local)
            first = [copy(a, 0, me, sibling, src=ins[a])] + [copy(a, 1 + j, me, me ^ m, src=ins[a]) for j, m in enumerate(chip_masks)]
            for cp in first:
                cp.start()
            started += first
        for a in range(n):
            for j, m in enumerate(chip_masks):
                copy(a, 1 + j, me ^ m, me).wait_recv()
                fwd = copy(a, 4 + j, me ^ m, sibling)
                fwd.start()
                started.append(fwd)
        for a in range(n):
            copy(a, 0, sibling, me).wait_recv()
            for j, m in enumerate(chip_masks):
                copy(a, 4 + j, sibling ^ m, me).wait_recv()
        for cp in started:
            cp.wait_send()
        for cp in own:
            cp.wait()

    any_spec = pl.BlockSpec(memory_space=pl.ANY)
    return pl.pallas_call(
        body, name=name, in_specs=[any_spec] * n, out_specs=[any_spec] * n,
        out_shape=[jax.ShapeDtypeStruct((N_DEV,) + a.shape, a.dtype) for a in arrays],
        scratch_shapes=[pltpu.SemaphoreType.DMA((n * N_PEERS,)), pltpu.SemaphoreType.DMA((n * N_PEERS,)),
                        pltpu.SemaphoreType.DMA((n,))],
    )(*arrays)


N_CHIPS = N_DEV // 2


def exchange_sibling(name, arrays):
    n = len(arrays)

    def body(*refs):
        ins, outs = refs[:n], refs[n:2 * n]
        send_sems, recv_sems = refs[2 * n:]
        me = _my_index()
        other_core = 1 - me % 2
        copies = []
        for a in range(n):
            for chip in range(N_CHIPS):
                cp = pltpu.make_async_remote_copy(src_ref=ins[a].at[chip, other_core], dst_ref=outs[a].at[chip],
                                                  send_sem=send_sems.at[a * N_CHIPS + chip], recv_sem=recv_sems.at[a * N_CHIPS + chip],
                                                  device_id=_coords(me ^ 1), device_id_type=MESH_IDS)
                cp.start()
                copies.append(cp)
        for cp in copies:
            cp.wait()

    any_spec = pl.BlockSpec(memory_space=pl.ANY)
    return pl.pallas_call(
        body, name=name, in_specs=[any_spec] * n, out_specs=[any_spec] * n,
        out_shape=[jax.ShapeDtypeStruct((N_CHIPS,) + a.shape[2:], a.dtype) for a in arrays],
        scratch_shapes=[pltpu.SemaphoreType.DMA((n * N_CHIPS,)), pltpu.SemaphoreType.DMA((n * N_CHIPS,))],
    )(*arrays)


def exchange_chips(name, arrays):
    n = len(arrays)
    n_other = N_CHIPS - 1

    def body(*refs):
        ins, outs = refs[:n], refs[n:2 * n]
        send_sems, recv_sems, local_sems = refs[2 * n:]
        me = _my_index()
        chip, core = me // 2, me % 2
        copies = []
        for a in range(n):
            local = pltpu.make_async_copy(ins[a].at[chip], outs[a].at[chip], local_sems.at[a])
            local.start()
            copies.append(local)
            for k in range(n_other):
                peer_chip = chip ^ (k + 1)
                cp = pltpu.make_async_remote_copy(src_ref=ins[a].at[peer_chip], dst_ref=outs[a].at[chip],
                                                  send_sem=send_sems.at[a * n_other + k], recv_sem=recv_sems.at[a * n_other + k],
                                                  device_id=_coords(peer_chip * 2 + core), device_id_type=MESH_IDS)
                cp.start()
                copies.append(cp)
        for cp in copies:
            cp.wait()

    any_spec = pl.BlockSpec(memory_space=pl.ANY)
    return pl.pallas_call(
        body, name=name, in_specs=[any_spec] * n, out_specs=[any_spec] * n,
        out_shape=[jax.ShapeDtypeStruct(a.shape, a.dtype) for a in arrays],
        scratch_shapes=[pltpu.SemaphoreType.DMA((n * n_other,)), pltpu.SemaphoreType.DMA((n * n_other,)),
                        pltpu.SemaphoreType.DMA((n,))],
    )(*arrays)


def pair_sum(name, both, got):
    P, _, R, C = both.shape
    tr = _row_tile(R, C, 4)

    def body(a_ref, b_ref, o_ref):
        core = lax.axis_index("c")
        o_ref[...] = (a_ref[0, core].astype(F32) + b_ref[0].astype(F32)).astype(BF16)[None]

    return pl.pallas_call(
        body, name=name, grid=(P, R // tr),
        in_specs=[pl.BlockSpec((1, 2, tr, C), lambda p, i: (p, 0, i, 0)), pl.BlockSpec((1, tr, C), lambda p, i: (p, i, 0))],
        out_specs=pl.BlockSpec((1, tr, C), lambda p, i: (p, i, 0)),
        out_shape=jax.ShapeDtypeStruct((P, R, C), BF16), compiler_params=_cparams(("parallel", "parallel")))(both, got)


def all_gather_vmem(name, v):
    def body(v_ref, out_ref, send_sems, recv_sems):
        me = _my_index()
        out_ref[me] = v_ref[...]
        copies = []
        for k, mask in enumerate(PEER_MASKS):
            cp = pltpu.make_async_remote_copy(src_ref=v_ref, dst_ref=out_ref.at[me], send_sem=send_sems.at[k],
                                              recv_sem=recv_sems.at[k], device_id=_coords(me ^ mask), device_id_type=MESH_IDS)
            cp.start()
            copies.append(cp)
        for cp in copies:
            cp.wait()

    vm = pl.BlockSpec(memory_space=pltpu.VMEM)
    return pl.pallas_call(
        body, name=name, in_specs=[vm], out_specs=vm, out_shape=jax.ShapeDtypeStruct((N_DEV,) + v.shape, v.dtype),
        scratch_shapes=[pltpu.SemaphoreType.DMA((N_PEERS,)), pltpu.SemaphoreType.DMA((N_PEERS,))],
    )(v)


ROW_KERNEL_VMEM_BUDGET = 24 * 1024 * 1024


def _row_tile(rows, cols, bufs):
    budget = ROW_KERNEL_VMEM_BUDGET // (bufs * 2 * 4 * max(cols, LANES))
    if rows <= budget:
        return rows
    for t in range(budget - budget % 16, 15, -16):
        if rows % t == 0:
            return t
    return rows


def sum_parts(name, parts):
    P, R, C = parts.shape
    tr = _row_tile(R, C, P + 1)

    def body(p_ref, o_ref):
        acc = p_ref[0].astype(F32)
        for s in range(1, P):
            acc = acc + p_ref[s].astype(F32)
        o_ref[...] = acc

    return pl.pallas_call(body, name=name, grid=(R // tr,), in_specs=[pl.BlockSpec((P, tr, C), lambda i: (0, i, 0))],
                          out_specs=pl.BlockSpec((tr, C), lambda i: (i, 0)), out_shape=jax.ShapeDtypeStruct((R, C), F32),
                          compiler_params=_cparams(("parallel",)))(parts)


def cast_bf16(name, x):
    R, C = x.shape
    tr = _row_tile(R, C, 2)

    def body(x_ref, o_ref):
        o_ref[...] = x_ref[...].astype(BF16)

    spec = pl.BlockSpec((tr, C), lambda i: (i, 0))
    return pl.pallas_call(body, name=name, grid=(R // tr,), in_specs=[spec], out_specs=spec,
                          out_shape=jax.ShapeDtypeStruct((R, C), BF16), compiler_params=_cparams(("parallel",)))(x)


def adamw(name, w, g, m, v):
    R, C = w.shape
    tr = _row_tile(R, C, 7)

    def body(w_ref, g_ref, m_ref, v_ref, d_ref, nm_ref, nv_ref):
        g = g_ref[...]
        nm = ADAM_B1 * m_ref[...] + (1.0 - ADAM_B1) * g
        nv = ADAM_B2 * v_ref[...] + (1.0 - ADAM_B2) * (g * g)
        m_hat = nm / (1.0 - ADAM_B1 ** ADAM_STEP)
        v_hat = nv / (1.0 - ADAM_B2 ** ADAM_STEP)
        d_ref[...] = -ADAM_LR * (m_hat / (jnp.sqrt(v_hat) + ADAM_EPS) + ADAM_WD * w_ref[...])
        nm_ref[...] = nm
        nv_ref[...] = nv

    spec = pl.BlockSpec((tr, C), lambda i: (i, 0))
    return pl.pallas_call(body, name=name, grid=(R // tr,), in_specs=[spec] * 4, out_specs=[spec] * 3,
                          out_shape=[jax.ShapeDtypeStruct((R, C), F32)] * 3, compiler_params=_cparams(("parallel",)))(w, g, m, v)


def loss_and_grad(name, x, target, L, tm=ROW_TILE):
    T, D = x.shape
    nlt = L // tm

    def body(x_ref, t_ref, loss_ref, dx_ref):
        i = pl.program_id(0)
        err = jnp.where(i < nlt, x_ref[...] - t_ref[...], 0.0)
        dx_ref[...] = err * (1.0 / D)
        part = 0.5 * jnp.sum(jnp.sum(err * err, axis=1, keepdims=True), axis=0, keepdims=True) * (1.0 / D)
        _accumulate(loss_ref, part, i == 0)

    return pl.pallas_call(
        body, name=name, grid=(T // tm,),
        in_specs=[pl.BlockSpec((tm, D), lambda i: (i, 0)), pl.BlockSpec((tm, D), lambda i: (jnp.minimum(i, nlt - 1), 0))],
        out_specs=[pl.BlockSpec((1, 1), lambda i: (0, 0)), pl.BlockSpec((tm, D), lambda i: (i, 0))],
        out_shape=[jax.ShapeDtypeStruct((1, 1), F32), jax.ShapeDtypeStruct((T, D), F32)],
        compiler_params=_cparams(("arbitrary",)))(x, target)


def small_fwd(name, fn, arrays, out_shapes):
    def body(*refs):
        res = fn(*[r[...] for r in refs[:len(arrays)]])
        for o_ref, r in zip(refs[len(arrays):], res):
            o_ref[...] = r

    return pl.pallas_call(body, name=name, out_shape=[jax.ShapeDtypeStruct(s, F32) for s in out_shapes])(*arrays)


def small_bwd(name, fn, arrays, cts):
    n = len(arrays)

    def body(*refs):
        _, vjp = jax.vjp(lambda *a: tuple(fn(*a)), *[r[...] for r in refs[:n]])
        grads = vjp(tuple(r[...] for r in refs[n:n + len(cts)]))
        for o_ref, g in zip(refs[n + len(cts):], grads):
            o_ref[...] = g

    return pl.pallas_call(body, name=name, out_shape=[jax.ShapeDtypeStruct(a.shape, F32) for a in arrays])(*arrays, *cts)


def fn_silu(x):
    return (x * jax.nn.sigmoid(x),)


FWD_NAMES = ["x", "c", "ctx", "c_ctx", "w_mod", "b_mod", "norm1_g", "norm2_g", "w_in", "ssm_conv_w", "ssm_conv_b",
             "ssm_dt_bias", "ssm_a_log", "ssm_d", "ssm_norm_g", "swa_q_norm_g", "swa_k_norm_g", "swa_sink", "mla_q_lat_g",
             "mla_kv_lat_g", "w_mla_uq", "w_mla_ukv", "mla_q_norm_g", "mla_k_norm_g", "w_p_ssm", "w_p_swa", "w_p_mla",
             "w_out", "w_ffn_in", "w_ffn_out"]
WEIGHT_NAMES = FWD_NAMES[3:]
GATHERED = ["w_in", "w_mla_uq", "w_mla_ukv", "w_p_ssm", "w_p_swa", "w_p_mla", "w_out", "w_ffn_in", "w_ffn_out"]
COLUMN_SHARDED = ("w_in", "w_mla_uq", "w_mla_ukv", "w_ffn_in")
REPLICATED = ["c_ctx", "b_mod", "norm1_g", "norm2_g", "ssm_conv_b", "ssm_dt_bias", "ssm_a_log", "ssm_d", "ssm_norm_g",
              "swa_q_norm_g", "swa_k_norm_g", "swa_sink", "mla_q_lat_g", "mla_kv_lat_g", "mla_q_norm_g", "mla_k_norm_g"]
IN_SEGS = [("xbc", SSM_CONV_DIM), ("dt", 2 * SSM_HEADS), ("ks", SWA_KV_HEADS * SWA_HEAD_DIM), ("vs", SWA_KV_HEADS * SWA_HEAD_DIM),
           ("ckv", MLA_KV_RANK), ("kr", MLA_ROPE), ("z", SSM_INNER), ("qs", SWA_Q_HEADS * SWA_HEAD_DIM), ("cq", MLA_Q_RANK),
           ("g1", None), ("g2", None), ("g3", None)]


def _pack(vectors, multiple):
    flat = jnp.concatenate([v.reshape(-1) for v in vectors])
    pad = (-flat.shape[0]) % multiple
    return jnp.pad(flat, (0, pad)).reshape(-1, LANES)


def _unpack(packed, shapes):
    flat, out, off = packed.reshape(-1), [], 0
    for s in shapes:
        n = int(np.prod(s))
        out.append(flat[off:off + n].reshape(s))
        off += n
    return out


def _rope_tables(L, T, rot_dim):
    nf = rot_dim // 4
    inv = jnp.power(ROPE_BASE, -jnp.arange(nf, dtype=F32) / nf)
    r, col = jnp.meshgrid(jnp.arange(L // GRID_W, dtype=F32), jnp.arange(GRID_W, dtype=F32), indexing="ij")
    ang = jnp.stack([r.reshape(-1)[:, None] * inv, col.reshape(-1)[:, None] * inv], axis=1)
    cos, sin = jnp.cos(ang), jnp.sin(ang)
    c = jnp.concatenate([cos[:, 0], cos[:, 0], cos[:, 1], cos[:, 1]], axis=1)
    s = jnp.concatenate([-sin[:, 0], sin[:, 0], -sin[:, 1], sin[:, 1]], axis=1)
    c = jnp.pad(c, ((0, T - L), (0, LANES - rot_dim)), constant_values=1.0)
    s = jnp.pad(s, ((0, T - L), (0, LANES - rot_dim)))
    return c, s


def _pad_rows(a, rows):
    return jnp.pad(a, ((0, rows - a.shape[0]), (0, 0)))


def _row(w, per_head=0, off=0, diff=True):
    return ("row", w, per_head, off, diff)


PAR, PAR_ND = ("par", True), ("par", False)
GRP = ("grp", True)


def kernel(*args):
    n_fwd, n_w = len(FWD_NAMES), len(WEIGHT_NAMES)
    inp = dict(zip(FWD_NAMES, args[:n_fwd]))
    loss_target = args[n_fwd]
    mom_m = dict(zip(WEIGHT_NAMES, args[n_fwd + 1:n_fwd + 1 + n_w]))
    mom_v = dict(zip(WEIGHT_NAMES, args[n_fwd + 1 + n_w:]))

    x, ctx = inp["x"][0], inp["ctx"][0]
    L, D = x.shape
    n_ctx = ctx.shape[0]
    T = L + n_ctx
    depth = inp["w_in"].shape[0]
    me = _my_index()
    in_widths = [w if w is not None else D for _, w in IN_SEGS]
    in_offs = np.concatenate([[0], np.cumsum(in_widths)]).tolist()
    ffn_h = inp["w_ffn_out"].shape[1] * N_DEV
    cfg_swa = AttnCfg(SWA_Q_HEADS, SWA_Q_HEADS // SWA_KV_HEADS, SWA_HEAD_DIM, SWA_HEAD_DIM, SWA_HEAD_DIM ** -0.5, SWA_WINDOW,
                      True, L, T, 256, 1)
    cfg_mla = AttnCfg(MLA_HEADS, 1, MLA_QK_PAD, MLA_V, MLA_QK ** -0.5, None, False, L, T, 1024, 2)
    cfg_mla_bwd = AttnCfg(MLA_HEADS, 1, MLA_QK_PAD, MLA_V, MLA_QK ** -0.5, None, False, L, T, 2048, 1)

    def rf(name, fn, descs, arrays, outs, heads=1):
        return rowop_fwd(name, fn, descs, arrays, outs, T, L, heads=heads)

    def rb(name, fn, descs, arrays, outs, cts, heads=1, add=None):
        return rowop_bwd(name, fn, descs, arrays, outs, cts, T, L, heads=heads, add=add)

    local = []
    for n in GATHERED:
        w = inp[n]
        local.append((jnp.swapaxes(w, 1, 2) if n in COLUMN_SHARDED else w).astype(BF16))
    gathered = dict(zip(GATHERED, all_gather_hbm("gather_weights", local)))

    def full(n, l):
        g = gathered[n][:, l]
        return g.reshape(g.shape[0] * g.shape[1], g.shape[2])

    def layer_weights(l):
        wt = {}
        w_in_t = full("w_in", l)
        for (sn, _), o, w in zip(IN_SEGS, in_offs, in_widths):
            seg = w_in_t[o:o + w]
            wt[sn] = _pad_rows(seg, LANES) if sn == "kr" else seg
        uq = full("w_mla_uq", l).reshape(MLA_HEADS, MLA_QK, MLA_Q_RANK)
        wt["uqn"] = uq[:, :MLA_NOPE].reshape(MLA_HEADS * MLA_NOPE, MLA_Q_RANK)
        wt["uqr"] = jnp.pad(uq[:, MLA_NOPE:], ((0, 0), (0, LANES - MLA_ROPE), (0, 0))).reshape(MLA_HEADS * LANES, MLA_Q_RANK)
        ukv = full("w_mla_ukv", l).reshape(MLA_HEADS, MLA_NOPE + MLA_V, MLA_KV_RANK)
        wt["uk"] = ukv[:, :MLA_NOPE].reshape(MLA_HEADS * MLA_NOPE, MLA_KV_RANK)
        wt["uv"] = ukv[:, MLA_NOPE:].reshape(MLA_HEADS * MLA_V, MLA_KV_RANK)
        for n in ("w_p_ssm", "w_p_swa", "w_p_mla", "w_out", "w_ffn_out"):
            wt[n] = full(n, l)
        ffn_in_t = full("w_ffn_in", l)
        wt["fg"], wt["fu"] = ffn_in_t[:ffn_h], ffn_in_t[ffn_h:]
        return wt

    def layer_params(l):
        p = {}
        for n in ("norm1_g", "norm2_g", "ssm_conv_b", "ssm_norm_g", "swa_q_norm_g", "swa_k_norm_g", "mla_q_lat_g", "mla_kv_lat_g"):
            p[n] = inp[n][l][None]
        p["dt_bias"] = inp["ssm_dt_bias"][l].reshape(1, 2 * SSM_HEADS)
        p["alog_row"] = [inp["ssm_a_log"][l][d][None] for d in range(2)]
        p["alog_col"] = [inp["ssm_a_log"][l][d][:, None] for d in range(2)]
        p["d_lane"] = jnp.repeat(inp["ssm_d"][l], SSM_HEAD_DIM)[None]
        p["sink"] = inp["swa_sink"][l].reshape(SWA_Q_HEADS, 1, 1)
        for n, key in (("mla_q_norm_g", "gq"), ("mla_k_norm_g", "gk")):
            g = inp[n][l]
            p[key + "n"] = g[:MLA_NOPE][None]
            p[key + "r"] = jnp.pad(g[MLA_NOPE:], (0, LANES - MLA_ROPE))[None]
        return p

    conv_local = _pack([inp["ssm_conv_w"]], 8 * LANES)
    conv_all = all_gather_vmem("gather_conv_w", conv_local)
    cw = inp["ssm_conv_w"].shape
    conv_full = conv_all.reshape(N_DEV, -1)[:, :cw[0] * cw[1] * cw[2]].reshape(N_DEV, cw[0], cw[1], cw[2])
    conv_full = jnp.moveaxis(conv_full, 0, 2).reshape(cw[0], cw[1], N_DEV * cw[2])
    conv_w8 = jnp.pad(conv_full, ((0, 0), (0, 8 - cw[1]), (0, 0)))

    silu_c, silu_cc = small_fwd("silu_c", lambda a, b: fn_silu(a) + fn_silu(b), [inp["c"], inp["c_ctx"][None]], [(1, D), (1, D)])
    silu_all = all_gather_vmem("gather_silu_c", silu_c.reshape(D // LANES, LANES)).reshape(N_DEV, D)
    S_rows = 2 * N_DEV
    S_mat = jnp.concatenate([silu_all, silu_cc, jnp.zeros((S_rows - N_DEV - 1, D), F32)], axis=0)
    mod_cols = inp["w_mod"].shape[2]
    mods_local = []
    for l in range(depth):
        bias = lax.dynamic_slice(inp["b_mod"][l], (me * mod_cols,), (mod_cols,))
        mods_local.append(matmul(S_mat, inp["w_mod"][l], "nn", f"mod{l}", add=jnp.broadcast_to(bias[None], (S_rows, mod_cols))))
    mods_all = all_gather_vmem("gather_mods", jnp.stack(mods_local).reshape(-1, LANES))
    mods_all = jnp.moveaxis(mods_all.reshape(N_DEV, depth, S_rows, mod_cols), 0, 2).reshape(depth, S_rows, N_DEV * mod_cols)
    mods_lat = lax.dynamic_slice(mods_all, (0, me, 0), (depth, 1, N_DEV * mod_cols))[:, 0]
    mods_ctx = mods_all[:, N_DEV]

    def layer_mods(l):
        return [jnp.stack([mods_lat[l, j * D:(j + 1) * D], mods_ctx[l, j * D:(j + 1) * D]])[:, None] for j in range(6)]

    cs_swa = _rope_tables(L, T, SWA_HEAD_DIM)
    cs_mla = _rope_tables(L, T, MLA_ROPE)
    nm_descs = [_row(D), PAR, GRP, GRP]
    resid_descs = [_row(D, diff=False), _row(D), GRP]
    tab = [_row(LANES, diff=False), _row(LANES, diff=False)]
    swaq_descs = [_row(SWA_HEAD_DIM, 1), PAR] + tab
    swakv_descs = [_row(SWA_HEAD_DIM, 1), _row(SWA_HEAD_DIM, 1), PAR] + tab
    mlaq_descs = [_row(LANES, 1), _row(LANES, 1), PAR, PAR] + tab
    mlakv_descs = [_row(LANES, 1), _row(LANES, 1), _row(LANES), PAR, PAR] + tab
    ssdout_descs = [_row(SSM_INNER), _row(SSM_INNER, diff=False), _row(SSM_INNER), _row(SSM_INNER), PAR, PAR]
    merge_descs = [_row(D)] * 6
    swiglu_descs = [_row(ffn_h), _row(ffn_h)]
    seg_names = [sn for sn, _ in IN_SEGS]
    seg_groups = [seg_names[:7], seg_names[7:]]

    def layer_fwd(l, X, wt, p, mods):
        sh1, sc1, gt1, sh2, sc2, gt2 = mods
        r = {"X": X}
        r["h1"] = rf(f"l{l}_norm1", fn_norm_mod, nm_descs, [X, p["norm1_g"], sh1, sc1], [(D, 0, BF16)])[0]
        for gi, group in enumerate(seg_groups):
            r.update(zip(group, matmul_multi(f"l{l}_in{gi}", r["h1"], [wt[sn] for sn in group])))
        r["u"] = conv_fwd(f"l{l}_conv", r["xbc"], conv_w8[l], p["ssm_conv_b"], L)
        r["dts"] = rf(f"l{l}_softplus", fn_softplus, [_row(2 * SSM_HEADS), PAR], [r["dt"], p["dt_bias"]], [(2 * SSM_HEADS, 0, F32)])[0]
        for d in range(2):
            dt_d = r["dts"][:, d * SSM_HEADS:(d + 1) * SSM_HEADS]
            r[f"dt{d}"], r[f"dtt{d}"] = dt_d, dt_d.T
            r[f"y{d}"], r[f"hin{d}"] = ssd_fwd(f"l{l}_ssd{d}", d, r["u"], dt_d, dt_d.T, p["alog_row"][d], p["alog_col"][d], L)
        r["ys"] = rf(f"l{l}_ssd_out", fn_ssd_out, ssdout_descs, [r["y0"], r["y1"], r["u"], r["z"], p["d_lane"], p["ssm_norm_g"]],
                     [(SSM_INNER, 0, F32)])[0]
        r["Qs"] = rf(f"l{l}_swa_q", fn_swa_q, swaq_descs, [r["qs"], p["swa_q_norm_g"], *cs_swa], [(SWA_HEAD_DIM, 1, BF16)], SWA_Q_HEADS)[0]
        r["Ks"], r["Vs"] = rf(f"l{l}_swa_kv", fn_swa_kv, swakv_descs, [r["ks"], r["vs"], p["swa_k_norm_g"], *cs_swa],
                              [(SWA_HEAD_DIM, 1, BF16), (SWA_HEAD_DIM, 1, BF16)], SWA_KV_HEADS)
        r["Os"], r["lse_s"] = flash_fwd(f"l{l}_swa_fwd", cfg_swa, r["Qs"], r["Ks"], r["Vs"], p["sink"])
        r["cqn"] = rf(f"l{l}_q_lat", fn_rms, [_row(MLA_Q_RANK), PAR], [r["cq"], p["mla_q_lat_g"]], [(MLA_Q_RANK, 0, BF16)])[0]
        r["qn"], r["qr"] = matmul_multi(f"l{l}_uq", r["cqn"], [wt["uqn"], wt["uqr"]])
        r["Qm"] = rf(f"l{l}_mla_q", fn_mla_q, mlaq_descs, [r["qn"], r["qr"], p["gqn"], p["gqr"], *cs_mla], [(MLA_QK_PAD, 1, BF16)], MLA_HEADS)[0]
        r["ckvn"] = rf(f"l{l}_kv_lat", fn_rms, [_row(MLA_KV_RANK), PAR], [r["ckv"], p["mla_kv_lat_g"]], [(MLA_KV_RANK, 0, BF16)])[0]
        r["kn"], r["vp"] = matmul_multi(f"l{l}_ukv", r["ckvn"], [wt["uk"], wt["uv"]])
        r["Km"], r["Vm"] = rf(f"l{l}_mla_kv", fn_mla_kv, mlakv_descs, [r["kn"], r["vp"], r["kr"], p["gkn"], p["gkr"], *cs_mla],
                              [(MLA_QK_PAD, 1, BF16), (MLA_V, 1, BF16)], MLA_HEADS)
        r["Om"], r["lse_m"] = flash_fwd(f"l{l}_mla_fwd", cfg_mla, r["Qm"], r["Km"], r["Vm"], None)
        r["P1"] = matmul(r["ys"], wt["w_p_ssm"], "nn", f"l{l}_p_ssm")
        r["P2"] = matmul(r["Os"], wt["w_p_swa"], "nn", f"l{l}_p_swa")
        r["P3"] = matmul(r["Om"], wt["w_p_mla"], "nn", f"l{l}_p_mla")
        r["mg"] = rf(f"l{l}_merge", fn_merge, merge_descs, [r["g1"], r["g2"], r["g3"], r["P1"], r["P2"], r["P3"]], [(D, 0, BF16)])[0]
        r["A"] = matmul(r["mg"], wt["w_out"], "nn", f"l{l}_out")
        r["X1"] = rf(f"l{l}_resid1", fn_resid, resid_descs, [X, r["A"], gt1], [(D, 0, F32)])[0]
        r["h2"] = rf(f"l{l}_norm2", fn_norm_mod, nm_descs, [r["X1"], p["norm2_g"], sh2, sc2], [(D, 0, BF16)])[0]
        r["Fg"] = matmul(r["h2"], wt["fg"], "nt", f"l{l}_ffn_g")
        r["Fu"] = matmul(r["h2"], wt["fu"], "nt", f"l{l}_ffn_u")
        r["sg"] = rf(f"l{l}_swiglu", fn_swiglu, swiglu_descs, [r["Fg"], r["Fu"]], [(ffn_h, 0, BF16)])[0]
        r["B"] = matmul(r["sg"], wt["w_ffn_out"], "nn", f"l{l}_ffn_out")
        X2 = rf(f"l{l}_resid2", fn_resid, resid_descs, [r["X1"], r["B"], gt2], [(D, 0, F32)])[0]
        return X2, r

    def attn_bwd(tag, cfg, q, k, v, o, lse, do, sink):
        res = attn_delta(f"{tag}_delta", cfg, o, do, lse, sink)
        delta, dob = res[0], res[1]
        dsink = res[2] if cfg.has_sink else None
        dq, dk, dv = flash_bwd_fused(f"{tag}_bwd", cfg, q, k, v, dob, lse, delta)
        return dq, dk, dv, dsink

    def layer_bwd(l, dX2, r, wt, p, mods):
        sh1, sc1, gt1, sh2, sc2, gt2 = mods
        g, gw = {}, {}
        dmod = [None] * 6
        dB, dmod[5] = rb(f"l{l}_resid2_b", fn_resid, resid_descs, [r["X1"], r["B"], gt2], [(D, 0, F32)], [dX2])
        dsg = matmul(dB, wt["w_ffn_out"], "nt", f"l{l}_ffn_out_da")
        gw["w_ffn_out"] = matmul(r["sg"], dB, "tn", f"l{l}_ffn_out_dw")
        dFg, dFu = rb(f"l{l}_swiglu_b", fn_swiglu, swiglu_descs, [r["Fg"], r["Fu"]], [(ffn_h, 0, BF16)], [dsg])
        dh2 = matmul(dFg, wt["fg"], "nn", f"l{l}_ffn_g_da")
        dh2 = matmul(dFu, wt["fu"], "nn", f"l{l}_ffn_u_da", add=dh2)
        gw["w_ffn_in"] = jnp.concatenate([matmul(r["h2"], dFg, "tn", f"l{l}_ffn_g_dw"), matmul(r["h2"], dFu, "tn", f"l{l}_ffn_u_dw")], axis=1)
        dX1, g["norm2_g"], dmod[3], dmod[4] = rb(f"l{l}_norm2_b", fn_norm_mod, nm_descs, [r["X1"], p["norm2_g"], sh2, sc2],
                                                [(D, 0, BF16)], [dh2], add={0: dX2})
        dA, dmod[2] = rb(f"l{l}_resid1_b", fn_resid, resid_descs, [r["X"], r["A"], gt1], [(D, 0, F32)], [dX1])
        dmg = matmul(dA, wt["w_out"], "nt", f"l{l}_out_da")
        gw["w_out"] = matmul(r["mg"], dA, "tn", f"l{l}_out_dw")
        dsegs = {}
        dsegs["g1"], dsegs["g2"], dsegs["g3"], dP1, dP2, dP3 = rb(
            f"l{l}_merge_b", fn_merge, merge_descs, [r["g1"], r["g2"], r["g3"], r["P1"], r["P2"], r["P3"]], [(D, 0, BF16)], [dmg])
        dys = matmul(dP1, wt["w_p_ssm"], "nt", f"l{l}_p_ssm_da")
        dOs = matmul(dP2, wt["w_p_swa"], "nt", f"l{l}_p_swa_da")
        dOm = matmul(dP3, wt["w_p_mla"], "nt", f"l{l}_p_mla_da")
        gw["w_p_ssm"] = matmul(r["ys"], dP1, "tn", f"l{l}_p_ssm_dw")
        gw["w_p_swa"] = matmul(r["Os"], dP2, "tn", f"l{l}_p_swa_dw")
        gw["w_p_mla"] = matmul(r["Om"], dP3, "tn", f"l{l}_p_mla_dw")
        dQm, dKm, dVm, _ = attn_bwd(f"l{l}_mla", cfg_mla_bwd, r["Qm"], r["Km"], r["Vm"], r["Om"], r["lse_m"], dOm, None)
        dkn, dvp, dsegs["kr"], dgkn, dgkr = rb(f"l{l}_mla_kv_b", fn_mla_kv, mlakv_descs,
                                               [r["kn"], r["vp"], r["kr"], p["gkn"], p["gkr"], *cs_mla],
                                               [(MLA_QK_PAD, 1, BF16), (MLA_V, 1, BF16)], [dKm, dVm], MLA_HEADS)
        dckvn = matmul_sum(f"l{l}_ukv_da", [(dkn, wt["uk"]), (dvp, wt["uv"])])
        dw_uk = matmul(r["ckvn"], dkn, "tn", f"l{l}_uk_dw").reshape(MLA_KV_RANK, MLA_HEADS, MLA_NOPE)
        dw_uv = matmul(r["ckvn"], dvp, "tn", f"l{l}_uv_dw").reshape(MLA_KV_RANK, MLA_HEADS, MLA_V)
        gw["w_mla_ukv"] = jnp.concatenate([dw_uk, dw_uv], axis=2).reshape(MLA_KV_RANK, -1)
        dsegs["ckv"], g["mla_kv_lat_g"] = rb(f"l{l}_kv_lat_b", fn_rms, [_row(MLA_KV_RANK), PAR], [r["ckv"], p["mla_kv_lat_g"]],
                                             [(MLA_KV_RANK, 0, BF16)], [dckvn])
        dqn, dqr, dgqn, dgqr = rb(f"l{l}_mla_q_b", fn_mla_q, mlaq_descs, [r["qn"], r["qr"], p["gqn"], p["gqr"], *cs_mla],
                                  [(MLA_QK_PAD, 1, BF16)], [dQm], MLA_HEADS)
        dcqn = matmul_sum(f"l{l}_uq_da", [(dqn, wt["uqn"]), (dqr, wt["uqr"])])
        dw_uqn = matmul(r["cqn"], dqn, "tn", f"l{l}_uqn_dw").reshape(MLA_Q_RANK, MLA_HEADS, MLA_NOPE)
        dw_uqr = matmul(r["cqn"], dqr, "tn", f"l{l}_uqr_dw").reshape(MLA_Q_RANK, MLA_HEADS, LANES)[:, :, :MLA_ROPE]
        gw["w_mla_uq"] = jnp.concatenate([dw_uqn, dw_uqr], axis=2).reshape(MLA_Q_RANK, -1)
        dsegs["cq"], g["mla_q_lat_g"] = rb(f"l{l}_q_lat_b", fn_rms, [_row(MLA_Q_RANK), PAR], [r["cq"], p["mla_q_lat_g"]],
                                           [(MLA_Q_RANK, 0, BF16)], [dcqn])
        g["mla_q_norm_g"] = jnp.concatenate([dgqn[0], dgqr[0, :MLA_ROPE]])
        g["mla_k_norm_g"] = jnp.concatenate([dgkn[0], dgkr[0, :MLA_ROPE]])
        dQs, dKs, dVs, dsink = attn_bwd(f"l{l}_swa", cfg_swa, r["Qs"], r["Ks"], r["Vs"], r["Os"], r["lse_s"], dOs, p["sink"])
        g["swa_sink"] = dsink.reshape(SWA_Q_HEADS)
        dsegs["qs"], g["swa_q_norm_g"] = rb(f"l{l}_swa_q_b", fn_swa_q, swaq_descs, [r["qs"], p["swa_q_norm_g"], *cs_swa],
                                            [(SWA_HEAD_DIM, 1, BF16)], [dQs], SWA_Q_HEADS)
        dsegs["ks"], dsegs["vs"], g["swa_k_norm_g"] = rb(f"l{l}_swa_kv_b", fn_swa_kv, swakv_descs,
                                                         [r["ks"], r["vs"], p["swa_k_norm_g"], *cs_swa],
                                                         [(SWA_HEAD_DIM, 1, BF16), (SWA_HEAD_DIM, 1, BF16)], [dKs, dVs], SWA_KV_HEADS)
        dy, dxs, dsegs["z"], dd_lane, g["ssm_norm_g"] = rb(
            f"l{l}_ssd_out_b", fn_ssd_out, ssdout_descs, [r["y0"], r["y1"], r["u"], r["z"], p["d_lane"], p["ssm_norm_g"]],
            [(SSM_INNER, 0, F32)], [dys])
        g["ssm_d"] = dd_lane.reshape(SSM_HEADS, SSM_HEAD_DIM).sum(axis=1)
        du, ddts, dalog = None, [], []
        for d in range(2):
            du, ddt, ddtt, dar, dac = ssd_bwd(f"l{l}_ssd{d}_b", d, r["u"], r[f"dt{d}"], r[f"dtt{d}"], p["alog_row"][d], p["alog_col"][d],
                                              r[f"hin{d}"], dy, L, add_x=dxs if d == 0 else None, add_u=du)
            ddts.append(ddt + ddtt.T)
            dalog.append(dar[0] + dac[:, 0])
        g["ssm_a_log"] = jnp.stack(dalog)
        dsegs["xbc"], dconv_w, g["ssm_conv_b"] = conv_bwd(f"l{l}_conv_b", r["xbc"], conv_w8[l], p["ssm_conv_b"], du, L)
        dsegs["dt"], ddt_bias = rb(f"l{l}_softplus_b", fn_softplus, [_row(2 * SSM_HEADS), PAR], [r["dt"], p["dt_bias"]],
                                   [(2 * SSM_HEADS, 0, F32)], [jnp.concatenate(ddts, axis=1)])
        g["ssm_dt_bias"] = ddt_bias.reshape(2, SSM_HEADS)
        g["ssm_conv_w"] = dconv_w[:SSM_CONV]
        dh1, dws = None, []
        for gi, group in enumerate(seg_groups):
            dh1 = matmul_sum(f"l{l}_in_da{gi}", [(dsegs[sn], wt[sn]) for sn in group], add=dh1)
        for sn, w in zip(seg_names, in_widths):
            dws.append(matmul(r["h1"], dsegs[sn], "tn", f"l{l}_in_{sn}_dw")[:, :w])
        gw["w_in"] = jnp.concatenate(dws, axis=1)
        dX, g["norm1_g"], dmod[0], dmod[1] = rb(f"l{l}_norm1_b", fn_norm_mod, nm_descs, [r["X"], p["norm1_g"], sh1, sc1],
                                               [(D, 0, BF16)], [dh1], add={0: dX1})
        for n in ("norm1_g", "norm2_g", "ssm_conv_b", "ssm_norm_g", "swa_q_norm_g", "swa_k_norm_g", "mla_q_lat_g", "mla_kv_lat_g"):
            g[n] = g[n][0]
        dmod_lat = jnp.concatenate([dm[0, 0] for dm in dmod])
        dmod_ctx = jnp.concatenate([dm[1, 0] for dm in dmod])
        return dX, g, gw, dmod_lat, dmod_ctx

    X = jnp.concatenate([x, ctx], axis=0)
    saved = []
    for l in range(depth):
        wt, p, mods = layer_weights(l), layer_params(l), layer_mods(l)
        X, r = layer_fwd(l, X, wt, p, mods)
        saved.append((r, wt, p, mods))
    loss_part, dX = loss_and_grad("loss", X, loss_target[0], L)
    loss = lax.psum(loss_part[0, 0], ("x", "y", "c"))
    small_g = [None] * depth
    big_g = [None] * depth
    dmods = [None] * depth
    for l in reversed(range(depth)):
        r, wt, p, mods = saved[l]
        dX, small_g[l], big_g[l], dm_lat, dm_ctx = layer_bwd(l, dX, r, wt, p, mods)
        dmods[l] = jnp.stack([dm_lat, dm_ctx])
    grad_x = dX[:L][None]

    dm_all = all_gather_vmem("gather_dmods", jnp.stack(dmods).reshape(-1, LANES)).reshape(N_DEV, depth, 2, N_DEV * mod_cols)
    dm_rows = jnp.concatenate([jnp.moveaxis(dm_all[:, :, 0], 0, 1), dm_all[:, :, 1].sum(axis=0)[:, None],
                               jnp.zeros((depth, S_rows - N_DEV - 1, N_DEV * mod_cols), F32)], axis=1)
    dm_mine = lax.dynamic_slice(dm_rows, (0, 0, me * mod_cols), (depth, S_rows, mod_cols))
    grads = {}
    grads["w_mod"] = jnp.stack([matmul(S_mat, dm_mine[l], "tn", f"mod{l}_dw") for l in range(depth)])
    d_silu = None
    for l in range(depth):
        d_silu = matmul(dm_mine[l], inp["w_mod"][l], "nt", f"mod{l}_da", add=d_silu)
    small = {n: jnp.stack([small_g[l][n] for l in range(depth)]) for n in small_g[0]}
    small["c_ctx"] = small_bwd("silu_c_b", fn_silu, [inp["c_ctx"][None]], [d_silu[N_DEV:N_DEV + 1]])[0][0]
    small["b_mod"] = jnp.stack(dmods).sum(axis=1)

    rep_shapes = [inp[n].shape for n in REPLICATED]
    conv_shape = (depth, SSM_CONV, SSM_CONV_DIM)
    packed = _pack([small[n] for n in REPLICATED] + [small["ssm_conv_w"]], 8 * LANES)
    small_sum = sum_parts("sum_small", all_gather_vmem("gather_small", packed))
    summed = _unpack(small_sum, rep_shapes + [conv_shape])
    for n, gsum in zip(REPLICATED, summed):
        grads[n] = gsum
    grads["ssm_conv_w"] = lax.dynamic_slice(summed[-1], (0, 0, me * cw[2]), cw)

    slabs = []
    for n in GATHERED:
        gfull = jnp.stack([big_g[l][n] for l in range(depth)])
        if n in COLUMN_SHARDED:
            k_dim, n_dim = gfull.shape[1], gfull.shape[2]
            slab = jnp.moveaxis(gfull.reshape(depth, k_dim, N_DEV, n_dim // N_DEV), 2, 0)
        else:
            k_dim, n_dim = gfull.shape[1], gfull.shape[2]
            slab = jnp.moveaxis(gfull.reshape(depth, N_DEV, k_dim // N_DEV, n_dim), 1, 0)
        slab = cast_bf16(f"cast_{n}", slab.reshape(-1, slab.shape[-1])).reshape(slab.shape)
        slabs.append(slab.reshape((N_CHIPS, 2) + slab.shape[1:]))
    from_sibling = exchange_sibling("exchange_sibling", slabs)
    chip_sums = []
    for n, slab, got in zip(GATHERED, slabs, from_sibling):
        shp = inp[n].shape
        rows = shp[0] * shp[1]
        chip_sums.append(pair_sum(f"pair_{n}", slab.reshape(N_CHIPS, 2, rows, shp[2]), got.reshape(N_CHIPS, rows, shp[2])))
    for n, parts in zip(GATHERED, exchange_chips("exchange_chips", chip_sums)):
        grads[n] = sum_parts(f"sum_{n}", parts).reshape(inp[n].shape)

    delta, new_m, new_v = {}, {}, {}
    rep_pack = lambda d: _pack([d[n] for n in REPLICATED], 8 * LANES)
    rep_out = adamw("adamw_small", rep_pack(inp), rep_pack(grads), rep_pack(mom_m), rep_pack(mom_v))
    for out, res in zip((delta, new_m, new_v), rep_out):
        for n, a in zip(REPLICATED, _unpack(res, rep_shapes)):
            out[n] = a
    for n in ["w_mod", "ssm_conv_w"] + GATHERED:
        shp = inp[n].shape
        two_d = (shp[0] * shp[1], shp[2])
        res = adamw(f"adamw_{n}", inp[n].reshape(two_d), grads[n].reshape(two_d), mom_m[n].reshape(two_d), mom_v[n].reshape(two_d))
        delta[n], new_m[n], new_v[n] = [a.reshape(shp) for a in res]

    return (loss, grad_x, *[grads[n] for n in WEIGHT_NAMES], *[delta[n] for n in WEIGHT_NAMES],
            *[new_m[n] for n in WEIGHT_NAMES], *[new_v[n] for n in WEIGHT_NAMES])
```

```python
import functools
import math

import numpy as np
import jax
import jax.numpy as jnp
from jax import lax
from jax.experimental import pallas as pl
from jax.experimental.pallas import tpu as pltpu

F32 = jnp.float32
BF16 = jnp.bfloat16

N_DEV = 8
V7X_VMEM_BYTES = 64 * 1024 * 1024
VMEM_LIMIT_BYTES = V7X_VMEM_BYTES - 8 * 1024 * 1024
LANES = 128

EPS = 1e-6
ROPE_BASE = 10000.0
GRID_W = 64
SSM_HEADS, SSM_HEAD_DIM, SSM_GROUPS, SSM_STATE, SSM_CONV, SSM_CHUNK = 16, 64, 2, 128, 5, 128
SSM_INNER = SSM_HEADS * SSM_HEAD_DIM
SSM_CONV_DIM = SSM_INNER + 2 * SSM_GROUPS * SSM_STATE
SWA_Q_HEADS, SWA_KV_HEADS, SWA_HEAD_DIM, SWA_WINDOW = 8, 2, 128, 128
MLA_HEADS, MLA_Q_RANK, MLA_KV_RANK, MLA_NOPE, MLA_ROPE, MLA_V = 8, 384, 256, 128, 64, 128
MLA_QK = MLA_NOPE + MLA_ROPE
MLA_QK_PAD = 2 * LANES
ADAM_LR, ADAM_B1, ADAM_B2, ADAM_EPS, ADAM_WD, ADAM_STEP = 0.001, 0.9, 0.999, 1e-08, 0.01, 10

ROW_TILE = 256


def _cparams(sem, **kw):
    return pltpu.CompilerParams(dimension_semantics=sem, vmem_limit_bytes=VMEM_LIMIT_BYTES, **kw)


def _pick(dim, prefs):
    for p in prefs:
        if dim % p == 0:
            return p
    return dim


def matmul(a, b, mode, name, out_dtype=F32, add=None):
    if mode == "nn":
        (M, K), (K2, N) = a.shape, b.shape
    elif mode == "nt":
        (M, K), (N, K2) = a.shape, b.shape
    else:
        (K, M), (K2, N) = a.shape, b.shape
    assert K == K2, (name, a.shape, b.shape)
    has_add = add is not None
    tm, tn, tk = _matmul_tiles(M, N, K, a.dtype.itemsize, b.dtype.itemsize, jnp.dtype(out_dtype).itemsize, has_add,
                                   m_on_lanes=(mode == "tn"))
    nk = K // tk
    dims = {"nn": (((1,), (0,)), ((), ())), "nt": (((1,), (1,)), ((), ())), "tn": (((0,), (0,)), ((), ()))}[mode]
    a_spec = pl.BlockSpec((tk, tm), lambda i, j, k: (k, i)) if mode == "tn" else pl.BlockSpec((tm, tk), lambda i, j, k: (i, k))
    b_spec = pl.BlockSpec((tn, tk), lambda i, j, k: (j, k)) if mode == "nt" else pl.BlockSpec((tk, tn), lambda i, j, k: (k, j))
    o_spec = pl.BlockSpec((tm, tn), lambda i, j, k: (i, j))

    def body(*refs):
        a_ref, b_ref = refs[:2]
        c_ref = refs[2] if has_add else None
        o_ref = refs[3] if has_add else refs[2]
        part = lax.dot_general(a_ref[...].astype(BF16), b_ref[...].astype(BF16), dims, preferred_element_type=F32)
        if nk == 1:
            o_ref[...] = (part + c_ref[...] if has_add else part).astype(o_ref.dtype)
            return
        acc_ref = refs[-1]
        k = pl.program_id(2)

        @pl.when(k == 0)
        def _():
            acc_ref[...] = part + c_ref[...] if has_add else part

        @pl.when(k > 0)
        def _():
            acc_ref[...] += part

        @pl.when(k == nk - 1)
        def _():
            o_ref[...] = acc_ref[...].astype(o_ref.dtype)

    ins = [a, b] + ([add] if has_add else [])
    in_specs = [a_spec, b_spec] + ([o_spec] if has_add else [])
    return pl.pallas_call(
        body, name=name, grid=(M // tm, N // tn, nk), in_specs=in_specs, out_specs=o_spec,
        out_shape=jax.ShapeDtypeStruct((M, N), out_dtype),
        scratch_shapes=[pltpu.VMEM((tm, tn), F32)] if nk > 1 else [],
        input_output_aliases=({2: 0} if has_add else {}),
        compiler_params=_cparams(("parallel", "parallel", "arbitrary")),
    )(*ins)


def matmul_sum(name, pairs, add=None):
    M, N = pairs[0][0].shape[0], pairs[0][1].shape[1]
    n = len(pairs)
    has_add = add is not None
    resident = sum(2 * b.shape[0] * N * b.dtype.itemsize for _, b in pairs)
    tm = next((t for t in (768, 512, 384, 256, 128) if M % t == 0 and resident + sum(
        2 * t * a.shape[1] * a.dtype.itemsize + t * a.shape[1] * 2 for a, _ in pairs) + 6 * t * N * 4 <= MATMUL_VMEM_BUDGET), None)
    assert tm is not None, name

    def body(*refs):
        acc = refs[2 * n][...] if has_add else None
        for s in range(n):
            part = jnp.dot(refs[2 * s][...].astype(BF16), refs[2 * s + 1][...].astype(BF16), preferred_element_type=F32)
            acc = part if acc is None else acc + part
        refs[-1][...] = acc

    o_spec = pl.BlockSpec((tm, N), lambda i: (i, 0))
    in_specs, ins = [], []
    for a, b in pairs:
        in_specs += [pl.BlockSpec((tm, a.shape[1]), lambda i: (i, 0)), pl.BlockSpec(b.shape, lambda i: (0, 0))]
        ins += [a, b]
    if has_add:
        in_specs.append(o_spec)
        ins.append(add)
    return pl.pallas_call(body, name=name, grid=(M // tm,), in_specs=in_specs, out_specs=o_spec,
                          out_shape=jax.ShapeDtypeStruct((M, N), F32), input_output_aliases=({2 * n: 0} if has_add else {}),
                          compiler_params=_cparams(("parallel",)))(*ins)


def matmul_multi(name, a, bs):
    M, K = a.shape
    n = len(bs)
    resident = sum(2 * b.shape[0] * K * b.dtype.itemsize for b in bs)
    n_total = sum(b.shape[0] for b in bs)
    tm = next((t for t in (768, 512, 384, 256, 128) if M % t == 0 and
               resident + 2 * t * K * a.dtype.itemsize + 3 * t * n_total * 4 <= MATMUL_VMEM_BUDGET), None)
    assert tm is not None, name

    def body(*refs):
        lhs = refs[0][...].astype(BF16)
        for s in range(n):
            refs[1 + n + s][...] = lax.dot_general(lhs, refs[1 + s][...].astype(BF16), NT_DIMS, preferred_element_type=F32)

    in_specs = [pl.BlockSpec((tm, K), lambda i: (i, 0))] + [pl.BlockSpec(b.shape, lambda i: (0, 0)) for b in bs]
    return pl.pallas_call(
        body, name=name, grid=(M // tm,), in_specs=in_specs,
        out_specs=[pl.BlockSpec((tm, b.shape[0]), lambda i: (i, 0)) for b in bs],
        out_shape=[jax.ShapeDtypeStruct((M, b.shape[0]), F32) for b in bs], compiler_params=_cparams(("parallel",)))(a, *bs)


MATMUL_VMEM_BUDGET = 36 * 1024 * 1024


def _matmul_tiles(M, N, K, a_bytes, b_bytes, o_bytes, has_add, m_on_lanes=False):
    tk = K if K <= 1536 else _pick(K, (1408, 1024, 768, 704, 512, 256))
    nk = K // tk
    m_cands = [t for t in (1024, 768, 512, 384, 256, 128) if M % t == 0] or [M]
    if M % 768 and M % 1024:
        m_cands += [t for t in (1408, 704, 352) if M % t == 0 and not (m_on_lanes and t % LANES)]
    n_cands = [t for t in range(LANES, min(N, 2816) + 1, LANES) if N % t == 0] or [N]
    best = None
    for tm in m_cands:
        for tn in n_cands:
            pipeline = 2 * (tm * tk * a_bytes + tk * tn * b_bytes + tm * tn * o_bytes) + (2 * tm * tn * 4 if has_add else 0)
            temps = tm * tn * 4 * (2 if nk > 1 else 1) + (tm * tk * 2 if a_bytes == 4 else 0) + (tk * tn * 2 if b_bytes == 4 else 0)
            if pipeline + temps <= MATMUL_VMEM_BUDGET:
                score = (tm * tn, tn)
                if best is None or score > best[0]:
                    best = (score, tm, tn)
    if best is None:
        return m_cands[-1], n_cands[0], tk
    return best[1], best[2], tk


def _row_specs(descs, arrays, tm, nct, heads):
    specs = []
    for d, arr in zip(descs, arrays):
        if d[0] == "row":
            _, w, per_head, off, _ = d
            specs.append(pl.BlockSpec((tm, w * (heads if per_head else 1)), lambda i, off=off: (i, off)))
        elif d[0] == "par":
            specs.append(pl.BlockSpec(arr.shape, lambda i, nd=arr.ndim: (0,) * nd))
        else:
            specs.append(pl.BlockSpec((1,) + arr.shape[1:], lambda i, nd=arr.ndim: (jnp.where(i >= nct, 1, 0),) + (0,) * (nd - 1)))
    return specs


def _load(d, ref, h):
    if d[0] == "grp":
        return ref[0]
    if d[0] == "row" and d[2]:
        return ref[:, h * d[1]:(h + 1) * d[1]]
    return ref[...]


def _out_specs(outs, tm, heads):
    return [pl.BlockSpec((tm, w * (heads if ph else 1)), lambda i: (i, 0)) for (w, ph, _) in outs]


def rowop_fwd(name, fn, descs, arrays, outs, T, n_ctx, heads=1, tm=ROW_TILE):
    nct = n_ctx // tm
    n_in = len(descs)

    def body(*refs):
        for h in range(heads):
            res = fn(*[_load(d, r, h) for d, r in zip(descs, refs[:n_in])])
            for o_ref, r, (w, ph, _) in zip(refs[n_in:], res, outs):
                if ph:
                    o_ref[:, h * w:(h + 1) * w] = r.astype(o_ref.dtype)
                else:
                    o_ref[...] = r.astype(o_ref.dtype)

    out_shape = [jax.ShapeDtypeStruct((T, w * (heads if ph else 1)), dt) for (w, ph, dt) in outs]
    return pl.pallas_call(
        body, name=name, grid=(T // tm,), in_specs=_row_specs(descs, arrays, tm, nct, heads), out_specs=_out_specs(outs, tm, heads),
        out_shape=out_shape, compiler_params=_cparams(("parallel",)),
    )(*arrays)


def rowop_bwd(name, fn, descs, arrays, outs, cts, T, n_ctx, heads=1, tm=ROW_TILE, add=None, to_matmul=()):
    nct = n_ctx // tm
    n_in, n_ct = len(descs), len(cts)
    add = add or {}
    diff_idx = [k for k, d in enumerate(descs) if d[-1]]
    add_idx = [k for k in diff_idx if k in add]

    def body(*refs):
        in_refs, ct_refs = refs[:n_in], refs[n_in:n_in + n_ct]
        add_refs = dict(zip(add_idx, refs[n_in + n_ct:n_in + n_ct + len(add_idx)]))
        g_refs = refs[n_in + n_ct + len(add_idx):]
        i = pl.program_id(0)
        shared = {}
        for h in range(heads):
            vals = [_load(d, r, h) for d, r in zip(descs, in_refs)]

            def f(*dvals, vals=vals):
                full = list(vals)
                for k, v in zip(diff_idx, dvals):
                    full[k] = v
                return tuple(fn(*full))

            _, vjp = jax.vjp(f, *[vals[k] for k in diff_idx])
            cts_h = tuple(c[:, h * w:(h + 1) * w] if ph else c[...] for c, (w, ph, _) in zip(ct_refs, outs))
            for k, g_ref, g in zip(diff_idx, g_refs, vjp(cts_h)):
                d = descs[k]
                if d[0] == "row" and d[2]:
                    g_ref[:, h * d[1]:(h + 1) * d[1]] = g.astype(g_ref.dtype)
                else:
                    shared[k] = g if k not in shared else shared[k] + g
        for k, g_ref in zip(diff_idx, g_refs):
            d = descs[k]
            if k not in shared:
                continue
            g = shared[k]
            if d[0] == "row":
                if k in add_refs:
                    g = g + add_refs[k][...]
                g_ref[...] = g.astype(g_ref.dtype)
            elif d[0] == "par":
                _accumulate(g_ref, g, i == 0)
            else:
                _accumulate(g_ref, g[None], jnp.logical_or(i == 0, i == nct))

    in_specs = _row_specs(descs, arrays, tm, nct, heads)
    g_specs, g_shape = [], []
    for k in diff_idx:
        d = descs[k]
        if d[0] == "row":
            g_specs.append(pl.BlockSpec((tm, d[1] * (heads if d[2] else 1)), lambda i: (i, 0)))
            g_shape.append(jax.ShapeDtypeStruct((T, d[1] * (heads if d[2] else 1)), BF16 if k in to_matmul else F32))
        else:
            g_specs.append(in_specs[k])
            g_shape.append(jax.ShapeDtypeStruct(arrays[k].shape, F32))
    add_specs = [g_specs[diff_idx.index(k)] for k in add_idx]
    return pl.pallas_call(
        body, name=name, grid=(T // tm,), in_specs=in_specs + _out_specs(outs, tm, heads) + add_specs, out_specs=g_specs,
        out_shape=g_shape, compiler_params=_cparams(("arbitrary",)),
    )(*arrays, *cts, *[add[k] for k in add_idx])


def _accumulate(ref, val, first):
    @pl.when(first)
    def _():
        ref[...] = val.astype(ref.dtype)

    @pl.when(jnp.logical_not(first))
    def _():
        ref[...] += val.astype(ref.dtype)


def _rms(x, count=None):
    n = x.shape[-1] if count is None else count
    return x * lax.rsqrt(jnp.sum(x * x, axis=-1, keepdims=True) * (1.0 / n) + EPS)


def _swap_halves(x, nf):
    w = x.shape[-1]
    lane = lax.broadcasted_iota(jnp.int32, x.shape, x.ndim - 1)
    return jnp.where((lane % (2 * nf)) < nf, pltpu.roll(x, w - nf, x.ndim - 1), pltpu.roll(x, nf, x.ndim - 1))


def _make_rope(nf):
    @jax.custom_vjp
    def rope(x, c, s):
        return x * c + _swap_halves(x, nf) * s

    def fwd(x, c, s):
        return rope(x, c, s), (c, s)

    def bwd(res, g):
        c, s = res
        return g * c + _swap_halves(g * s, nf), jnp.zeros_like(c), jnp.zeros_like(s)

    rope.defvjp(fwd, bwd)
    return rope


_rope_swa = _make_rope(SWA_HEAD_DIM // 4)
_rope_mla = _make_rope(MLA_ROPE // 4)


@jax.custom_vjp
def _softplus(x):
    e = jnp.exp(-jnp.abs(x))
    u = 1.0 + e
    log1p_e = jnp.where(u == 1.0, e, jnp.log(u) * e / jnp.where(u == 1.0, 1.0, u - 1.0))
    return jnp.maximum(x, 0.0) + log1p_e


_softplus.defvjp(lambda x: (_softplus(x), x), lambda x, g: (g * jax.nn.sigmoid(x),))


def fn_norm_mod(x, g, shift, scale):
    return (_rms(x) * g * (1.0 + scale) + shift,)


def fn_rms(x, g):
    return (_rms(x) * g,)


def fn_resid(x, a, gate):
    return (x + gate * a,)


def fn_softplus(dt, bias):
    return (_softplus(dt + bias),)


def fn_ssd_out(yf, yb, xs, z, d_lane, g):
    y = yf + yb + d_lane * xs
    return (_rms(y * (z * jax.nn.sigmoid(z))) * g,)


def fn_swa_q(q, g, c, s):
    return (_rope_swa(_rms(q) * g, c, s),)


def fn_swa_kv(k, v, g, c, s):
    return (_rope_swa(_rms(k) * g, c, s), v)


def fn_mla_q(qn, qr, gn, gr, c, s):
    return (jnp.concatenate([_rms(qn) * gn, _rope_mla(_rms(qr, MLA_ROPE) * gr, c, s)], axis=-1),)


def fn_mla_kv(kn, v, kr, gn, gr, c, s):
    return (jnp.concatenate([_rms(kn) * gn, _rope_mla(_rms(kr, MLA_ROPE) * gr, c, s)], axis=-1), v)


def fn_merge(g1, g2, g3, p1, p2, p3):
    return (jax.nn.sigmoid(g1) * p1 + jax.nn.sigmoid(g2) * p2 + jax.nn.sigmoid(g3) * p3,)


def fn_swiglu(g, u):
    return (g * jax.nn.sigmoid(g) * u,)


ATTN_TILE = 256
NT_DIMS = (((1,), (1,)), ((), ()))


class AttnCfg:
    def __init__(self, hq, group, dq, dv, scale, window, has_sink, L, T, chunk, kv_block):
        self.hq, self.group, self.dq, self.dv, self.scale = hq, group, dq, dv, scale
        self.window, self.has_sink, self.L, self.T = window, has_sink, L, T
        self.chunk = _pick(L, (chunk, ATTN_TILE))
        self.ctx_chunk = T - L
        self.kv_block = kv_block
        self.q_block = kv_block * group
        assert L % ATTN_TILE == 0 and (T - L) % ATTN_TILE == 0 and L % self.chunk == 0
        assert (hq // group) % kv_block == 0
        if window is not None:
            assert (ATTN_TILE + 2 * window) % self.chunk == 0
            self.window_chunks = min((ATTN_TILE + 2 * window) // self.chunk, L // self.chunk)
            self.align = math.gcd(self.chunk, window)
        else:
            self.align = self.chunk


LOG2E = math.log2(math.e)


def _latent_chunks(cfg, r0):
    c = cfg.chunk
    if cfg.window is None:
        lo, n = 0, cfg.L // c
    else:
        n = cfg.window_chunks
        lo = jnp.clip(r0 - cfg.window, 0, cfg.L - n * c)
    return lo, n


def _visible(cfg, rows_q, rows_k):
    return jnp.logical_or(rows_k >= cfg.L, jnp.abs(rows_k - rows_q) <= cfg.window)


def flash_fwd(name, cfg, q, k, v, sink):
    T, tq, c = cfg.T, ATTN_TILE, cfg.chunk
    hq, g, dq, dv, hb, kb = cfg.hq, cfg.group, cfg.dq, cfg.dv, cfg.q_block, cfg.kv_block
    to_log2 = cfg.scale * LOG2E

    def body(*refs):
        if cfg.has_sink:
            q_ref, k_ref, v_ref, sink_ref, o_ref, lse_ref = refs
        else:
            q_ref, k_ref, v_ref, o_ref, lse_ref = refs
        q0 = pl.program_id(1) * tq
        qs = [q_ref[:, hh * dq:(hh + 1) * dq] for hh in range(hb)]
        lat_lo, lat_n = _latent_chunks(cfg, q0)
        n = jnp.where(q0 >= cfg.L, 0, lat_n)
        rows_q = q0 + lax.broadcasted_iota(jnp.int32, (tq, 1), 0)

        def start(t):
            return pl.multiple_of(lat_lo + jnp.minimum(t, lat_n - 1) * c, cfg.align)

        def logits(ks, size):
            return tuple(lax.dot_general(qs[hh], k_ref[pl.ds(ks, size), (hh // g) * dq:(hh // g + 1) * dq], NT_DIMS,
                                         preferred_element_type=F32) for hh in range(hb))

        def update(state, s_all, ks, size, masked):
            new_state = []
            for hh in range(hb):
                m, acc = state[hh]
                s = s_all[hh]
                if masked:
                    rows_k = ks + lax.broadcasted_iota(jnp.int32, (1, size), 1)
                    s = jnp.where(_visible(cfg, rows_q, rows_k), s, -jnp.inf)
                m_new = jnp.maximum(m, jnp.max(s, axis=-1, keepdims=True) * to_log2)
                alpha = jnp.exp2(m - m_new)
                p = jnp.exp2(s * to_log2 - m_new).astype(BF16)
                kh = hh // g
                v_ones = jnp.concatenate([v_ref[pl.ds(ks, size), kh * dv:(kh + 1) * dv], jnp.ones((size, dv), BF16)], axis=1)
                acc = alpha * acc + jnp.dot(p, v_ones, preferred_element_type=F32)
                new_state.append((m_new, acc))
            return tuple(new_state)

        def step(t, carry):
            state, s_all = carry
            s_next = logits(start(t + 1), c)
            return update(state, s_all, start(t), c, cfg.window is not None), s_next

        state = []
        for hh in range(hb):
            if cfg.has_sink:
                m0 = jnp.zeros((tq, 1), F32) + sink_ref[hh] * LOG2E
                l0 = jnp.ones((tq, dv), F32)
            else:
                m0 = jnp.full((tq, 1), -jnp.inf, F32)
                l0 = jnp.zeros((tq, dv), F32)
            state.append((m0, jnp.concatenate([jnp.zeros((tq, dv), F32), l0], axis=1)))
        state = update(tuple(state), logits(cfg.L, cfg.ctx_chunk), cfg.L, cfg.ctx_chunk, False)
        state, _ = lax.fori_loop(0, n, step, (state, logits(start(0), c)))
        for hh in range(hb):
            m, acc = state[hh]
            o_ref[:, hh * dv:(hh + 1) * dv] = acc[:, :dv] / acc[:, dv:]
            lse_ref[hh] = m + jnp.log2(acc[:, dv:dv + 1])

    in_specs = [pl.BlockSpec((tq, hb * dq), lambda h, i: (i, h)),
                pl.BlockSpec((T, kb * dq), lambda h, i: (0, h)),
                pl.BlockSpec((T, kb * dv), lambda h, i: (0, h))]
    ins = [q, k, v]
    if cfg.has_sink:
        in_specs.append(pl.BlockSpec((hb, 1, 1), lambda h, i: (h, 0, 0)))
        ins.append(sink)
    return pl.pallas_call(
        body, name=name, grid=(hq // hb, T // tq), in_specs=in_specs,
        out_specs=[pl.BlockSpec((tq, hb * dv), lambda h, i: (i, h)), pl.BlockSpec((hb, tq, 1), lambda h, i: (h, i, 0))],
        out_shape=[jax.ShapeDtypeStruct((T, hq * dv), F32), jax.ShapeDtypeStruct((hq, T, 1), F32)],
        compiler_params=_cparams(("parallel", "parallel")),
    )(*ins)


def attn_delta(name, cfg, o, do, lse, sink):
    T, tm, hq, dv = cfg.T, ATTN_TILE, cfg.hq, cfg.dv

    def body(*refs):
        if cfg.has_sink:
            o_ref, do_ref, lse_ref, sink_ref, delta_ref, dob_ref, dsink_ref = refs
        else:
            o_ref, do_ref, delta_ref, dob_ref = refs
        dob_ref[...] = do_ref[...].astype(BF16)
        parts = []
        for h in range(hq):
            delta = jnp.sum(do_ref[:, h * dv:(h + 1) * dv] * o_ref[:, h * dv:(h + 1) * dv], axis=-1, keepdims=True)
            delta_ref[h] = delta
            if cfg.has_sink:
                parts.append(-jnp.sum(jnp.exp2(sink_ref[h] * LOG2E - lse_ref[h]) * delta, axis=0, keepdims=True)[None])
        if cfg.has_sink:
            _accumulate(dsink_ref, jnp.concatenate(parts, axis=0), pl.program_id(0) == 0)

    head_tile = pl.BlockSpec((tm, hq * dv), lambda i: (i, 0))
    col = pl.BlockSpec((hq, tm, 1), lambda i: (0, i, 0))
    one = pl.BlockSpec((hq, 1, 1), lambda i: (0, 0, 0))
    in_specs, ins = [head_tile, head_tile], [o, do]
    out_specs = [col, head_tile]
    out_shape = [jax.ShapeDtypeStruct((hq, T, 1), F32), jax.ShapeDtypeStruct((T, hq * dv), BF16)]
    if cfg.has_sink:
        in_specs += [col, one]
        ins += [lse, sink]
        out_specs.append(one)
        out_shape.append(jax.ShapeDtypeStruct((hq, 1, 1), F32))
    return pl.pallas_call(body, name=name, grid=(T // tm,), in_specs=in_specs, out_specs=out_specs, out_shape=out_shape,
                          compiler_params=_cparams(("arbitrary",)))(*ins)


def flash_bwd_fused(name, cfg, q, k, v, dob, lse, delta):
    T, L, tq, c, cc = cfg.T, cfg.L, ATTN_TILE, cfg.chunk, cfg.ctx_chunk
    hq, g, dq, dv = cfg.hq, cfg.group, cfg.dq, cfg.dv
    hk = hq // g
    nq = T // tq
    to_log2 = cfg.scale * LOG2E
    masked = cfg.window is not None

    def body(q_ref, k_ref, v_ref, do_ref, lse_ref, delta_ref, dq_ref, dk_ref, dv_ref):
        i = pl.program_id(1)
        q0 = i * tq

        @pl.when(i == 0)
        def _():
            dk_ref[...] = jnp.zeros_like(dk_ref)
            dv_ref[...] = jnp.zeros_like(dv_ref)

        qs = [q_ref[:, hh * dq:(hh + 1) * dq] for hh in range(g)]
        dos = [do_ref[:, hh * dv:(hh + 1) * dv] for hh in range(g)]
        lses = [lse_ref[hh] for hh in range(g)]
        deltas = [delta_ref[hh] for hh in range(g)]
        lat_lo, lat_n = _latent_chunks(cfg, q0)
        n = jnp.where(q0 >= L, 0, lat_n)
        rows_q = q0 + lax.broadcasted_iota(jnp.int32, (tq, 1), 0)

        def start(t):
            return pl.multiple_of(lat_lo + jnp.minimum(t, lat_n - 1) * c, cfg.align)

        def products(ks, size):
            kk, vv = k_ref[pl.ds(ks, size), :], v_ref[pl.ds(ks, size), :]
            return tuple((lax.dot_general(qs[hh], kk, NT_DIMS, preferred_element_type=F32),
                          lax.dot_general(dos[hh], vv, NT_DIMS, preferred_element_type=F32)) for hh in range(g))

        def update(accs, prods, ks, size, mask_it):
            new_accs, dv_part, dk_part = [], None, None
            for hh in range(g):
                s, dp = prods[hh]
                p = jnp.exp2(s * to_log2 - lses[hh])
                if mask_it:
                    rows_k = ks + lax.broadcasted_iota(jnp.int32, (1, size), 1)
                    p = jnp.where(_visible(cfg, rows_q, rows_k), p, 0.0)
                ds = (p * (dp - deltas[hh])).astype(BF16)
                dv_h = lax.dot_general(p.astype(BF16), dos[hh], TN_DIMS, preferred_element_type=F32)
                dk_h = lax.dot_general(ds, qs[hh], TN_DIMS, preferred_element_type=F32)
                dv_part = dv_h if dv_part is None else dv_part + dv_h
                dk_part = dk_h if dk_part is None else dk_part + dk_h
                new_accs.append(accs[hh] + jnp.dot(ds, k_ref[pl.ds(ks, size), :], preferred_element_type=F32))
            dv_ref[pl.ds(ks, size), :] += dv_part
            dk_ref[pl.ds(ks, size), :] += dk_part
            return tuple(new_accs)

        def step(t, carry):
            accs, prods = carry
            nxt = products(start(t + 1), c)
            return update(accs, prods, start(t), c, masked), nxt

        accs = update(tuple(jnp.zeros((tq, dq), F32) for _ in range(g)), products(L, cc), L, cc, False)
        accs, _ = lax.fori_loop(0, n, step, (accs, products(start(0), c)))
        for hh in range(g):
            dq_ref[:, hh * dq:(hh + 1) * dq] = accs[hh] * cfg.scale

        @pl.when(i == nq - 1)
        def _():
            dk_ref[...] = dk_ref[...] * cfg.scale

    col = pl.BlockSpec((g, tq, 1), lambda h, i: (h, i, 0))
    return pl.pallas_call(
        body, name=name, grid=(hk, nq),
        in_specs=[pl.BlockSpec((tq, g * dq), lambda h, i: (i, h)),
                  pl.BlockSpec((T, dq), lambda h, i: (0, h)),
                  pl.BlockSpec((T, dv), lambda h, i: (0, h)),
                  pl.BlockSpec((tq, g * dv), lambda h, i: (i, h)), col, col],
        out_specs=[pl.BlockSpec((tq, g * dq), lambda h, i: (i, h)),
                   pl.BlockSpec((T, dq), lambda h, i: (0, h)),
                   pl.BlockSpec((T, dv), lambda h, i: (0, h))],
        out_shape=[jax.ShapeDtypeStruct((T, hq * dq), F32), jax.ShapeDtypeStruct((T, hk * dq), F32),
                   jax.ShapeDtypeStruct((T, hk * dv), F32)],
        compiler_params=_cparams(("arbitrary", "arbitrary")),
    )(q, k, v, dob, lse, delta)


HALO = 8


def _conv_specs(tm, C, T):
    nb = tm // HALO
    last = T // HALO - 1
    return [pl.BlockSpec((HALO, C), lambda i: (jnp.maximum(i * nb - 1, 0), 0)),
            pl.BlockSpec((tm, C), lambda i: (i, 0)),
            pl.BlockSpec((HALO, C), lambda i: (jnp.minimum((i + 1) * nb, last), 0))]


def _extended(prev_ref, cur_ref, next_ref, i, tm, L, T):
    r0 = i * tm
    keep_prev = jnp.logical_and(r0 != 0, r0 != L).astype(F32)
    keep_next = jnp.logical_and(r0 + tm != L, r0 + tm != T).astype(F32)
    return jnp.concatenate([prev_ref[...] * keep_prev, cur_ref[...], next_ref[...] * keep_next], axis=0)


def _shift_rows(xe, d):
    n = xe.shape[0]
    return xe if d == 0 else pltpu.roll(xe, (-d) % n, 0)


def _conv_pre(xe, w_ref, b_ref):
    acc = b_ref[...] + w_ref[SSM_CONV // 2:SSM_CONV // 2 + 1, :] * xe
    for k in range(SSM_CONV):
        if k != SSM_CONV // 2:
            acc = acc + w_ref[k:k + 1, :] * _shift_rows(xe, k - SSM_CONV // 2)
    return acc


def conv_fwd(name, x, w, b, L, tm=ROW_TILE):
    T, C = x.shape

    def body(xp, xc, xn, w_ref, b_ref, o_ref):
        xe = _extended(xp, xc, xn, pl.program_id(0), tm, L, T)
        pre = _conv_pre(xe, w_ref, b_ref)[HALO:HALO + tm]
        o_ref[...] = pre * jax.nn.sigmoid(pre)

    full = lambda a: pl.BlockSpec(a.shape, lambda i: (0, 0))
    return pl.pallas_call(body, name=name, grid=(T // tm,), in_specs=_conv_specs(tm, C, T) + [full(w), full(b)],
                          out_specs=pl.BlockSpec((tm, C), lambda i: (i, 0)), out_shape=jax.ShapeDtypeStruct((T, C), F32),
                          compiler_params=_cparams(("parallel",)))(x, x, x, w, b)


def conv_bwd(name, x, w, b, gu, L, tm=ROW_TILE):
    T, C = x.shape

    def body(xp, xc, xn, gp, gc, gn, w_ref, b_ref, dx_ref, dw_ref, db_ref):
        i = pl.program_id(0)
        xe = _extended(xp, xc, xn, i, tm, L, T)
        ge = _extended(gp, gc, gn, i, tm, L, T)
        pre = _conv_pre(xe, w_ref, b_ref)
        sg = jax.nn.sigmoid(pre)
        gpre = ge * (sg * (1.0 + pre * (1.0 - sg)))
        half = SSM_CONV // 2
        dx = jnp.zeros((tm, C), F32)
        rows = []
        for k in range(SSM_CONV):
            dx = dx + w_ref[k:k + 1, :] * _shift_rows(gpre, half - k)[HALO:HALO + tm]
            rows.append(jnp.sum(gpre[HALO:HALO + tm] * _shift_rows(xe, k - half)[HALO:HALO + tm], axis=0, keepdims=True))
        dx_ref[...] = dx.astype(dx_ref.dtype)
        rows += [jnp.zeros((1, C), F32)] * (8 - SSM_CONV)
        _accumulate(dw_ref, jnp.concatenate(rows, axis=0), i == 0)
        _accumulate(db_ref, jnp.sum(gpre[HALO:HALO + tm], axis=0, keepdims=True), i == 0)

    full = lambda a: pl.BlockSpec(a.shape, lambda i: (0, 0))
    return pl.pallas_call(
        body, name=name, grid=(T // tm,), in_specs=_conv_specs(tm, C, T) * 2 + [full(w), full(b)],
        out_specs=[pl.BlockSpec((tm, C), lambda i: (i, 0)), pl.BlockSpec((8, C), lambda i: (0, 0)), pl.BlockSpec((1, C), lambda i: (0, 0))],
        out_shape=[jax.ShapeDtypeStruct((T, C), BF16), jax.ShapeDtypeStruct((8, C), F32), jax.ShapeDtypeStruct((1, C), F32)],
        compiler_params=_cparams(("arbitrary",)))(x, x, x, gu, gu, gu, w, b)


SSM_PAIRS = SSM_HEADS // 2
TN_DIMS = (((0,), (0,)), ((), ()))
HIGHEST = lax.Precision.HIGHEST


def _ssd_chunk(direction, xps, bs, cs, dt_col, dt_row, alog_row, alog_col, hps):
    Q = SSM_CHUNK
    da_col = dt_col * (-jnp.exp(alog_row))
    da_row = dt_row * (-jnp.exp(alog_col))
    ii = lax.broadcasted_iota(jnp.int32, (Q, Q), 0)
    jj = lax.broadcasted_iota(jnp.int32, (Q, Q), 1)
    tri = (ii >= jj) if direction == 0 else (ii <= jj)
    trif = tri.astype(F32)
    acs_col = jnp.dot(trif, da_col, precision=HIGHEST, preferred_element_type=F32)
    acs_row = lax.dot_general(da_row, trif, NT_DIMS, precision=HIGHEST, preferred_element_type=F32)
    tot_col = jnp.sum(da_col, axis=0, keepdims=True)
    lane16 = lax.broadcasted_iota(jnp.int32, (1, SSM_HEADS), 1)
    sub16 = lax.broadcasted_iota(jnp.int32, (SSM_HEADS, 1), 0)
    low = lax.broadcasted_iota(jnp.int32, (1, 2 * SSM_HEAD_DIM), 1) < SSM_HEAD_DIM

    def col(v, h):
        return jnp.sum(v * (lane16 == h).astype(F32), axis=1, keepdims=True)

    def row(v, h):
        return jnp.sum(v * (sub16 == h).astype(F32), axis=0, keepdims=True)

    ys, hos = [], []
    pairs_per_group = SSM_PAIRS // SSM_GROUPS
    for g in range(SSM_GROUPS):
        bb, cb16 = bs[g].astype(BF16), cs[g].astype(BF16)
        cb = lax.dot_general(cb16, bb, NT_DIMS, preferred_element_type=F32)
        for pp in range(pairs_per_group):
            p = g * pairs_per_group + pp
            h0, h1 = 2 * p, 2 * p + 1
            ac0, ac1 = col(acs_col, h0), col(acs_col, h1)
            seg0 = jnp.exp(jnp.where(tri, ac0 - row(acs_row, h0), -jnp.inf))
            seg1 = jnp.exp(jnp.where(tri, ac1 - row(acs_row, h1), -jnp.inf))
            dt_l = jnp.where(low, col(dt_col, h0), col(dt_col, h1))
            ac_l = jnp.where(low, ac0, ac1)
            tot_l = jnp.where(low, col(tot_col, h0), col(tot_col, h1))
            xdt = xps[p] * dt_l
            y = (jnp.dot((cb * seg0).astype(BF16), jnp.where(low, xdt, 0.0).astype(BF16), preferred_element_type=F32)
                 + jnp.dot((cb * seg1).astype(BF16), jnp.where(low, 0.0, xdt).astype(BF16), preferred_element_type=F32))
            y = y + jnp.dot(cb16, hps[p].astype(BF16), preferred_element_type=F32) * jnp.exp(ac_l)
            st = lax.dot_general(bb, (xdt * jnp.exp(tot_l - ac_l)).astype(BF16), TN_DIMS, preferred_element_type=F32)
            ys.append(y)
            hos.append(hps[p] * jnp.exp(tot_l) + st)
    return tuple(ys), tuple(hos)


def _ssd_chunk_of(direction, step, ncl, ncc):
    if direction == 0:
        return jnp.where(step < ncc, ncl + step, step - ncc)
    return jnp.where(step < ncc, ncl + ncc - 1 - step, ncl - 1 - (step - ncc))


def _ssd_load(u_ref):
    Q = SSM_CHUNK
    xps = tuple(u_ref[:, LANES * p:LANES * (p + 1)] for p in range(SSM_PAIRS))
    bs = tuple(u_ref[:, SSM_INNER + SSM_STATE * g:SSM_INNER + SSM_STATE * (g + 1)] for g in range(SSM_GROUPS))
    c0 = SSM_INNER + SSM_GROUPS * SSM_STATE
    cs = tuple(u_ref[:, c0 + SSM_STATE * g:c0 + SSM_STATE * (g + 1)] for g in range(SSM_GROUPS))
    return xps, bs, cs


def ssd_fwd(name, direction, u, dt, dt_t, alog_row, alog_col, L):
    T = u.shape[0]
    Q, N = SSM_CHUNK, SSM_STATE
    ncl, ncc = L // Q, (T - L) // Q
    nc = ncl + ncc
    cm = lambda s: _ssd_chunk_of(direction, s, ncl, ncc)

    def body(u_ref, dt_ref, dtt_ref, ar_ref, ac_ref, y_ref, hin_ref, state):
        @pl.when(pl.program_id(0) == 0)
        def _():
            state[...] = jnp.zeros_like(state)

        xps, bs, cs = _ssd_load(u_ref)
        hps = tuple(state[p] for p in range(SSM_PAIRS))
        for p in range(SSM_PAIRS):
            hin_ref[0, p] = hps[p]
        ys, hos = _ssd_chunk(direction, xps, bs, cs, dt_ref[...], dtt_ref[...], ar_ref[...], ac_ref[...], hps)
        for p in range(SSM_PAIRS):
            y_ref[:, LANES * p:LANES * (p + 1)] = ys[p]
            state[p] = hos[p]

    return pl.pallas_call(
        body, name=name, grid=(nc,),
        in_specs=[pl.BlockSpec((Q, SSM_CONV_DIM), lambda s: (cm(s), 0)),
                  pl.BlockSpec((Q, SSM_HEADS), lambda s: (cm(s), 0)),
                  pl.BlockSpec((SSM_HEADS, Q), lambda s: (0, cm(s))),
                  pl.BlockSpec((1, SSM_HEADS), lambda s: (0, 0)),
                  pl.BlockSpec((SSM_HEADS, 1), lambda s: (0, 0))],
        out_specs=[pl.BlockSpec((Q, SSM_INNER), lambda s: (cm(s), 0)),
                   pl.BlockSpec((1, SSM_PAIRS, N, LANES), lambda s: (cm(s), 0, 0, 0))],
        out_shape=[jax.ShapeDtypeStruct((T, SSM_INNER), F32), jax.ShapeDtypeStruct((nc, SSM_PAIRS, N, LANES), F32)],
        scratch_shapes=[pltpu.VMEM((SSM_PAIRS, N, LANES), F32)],
        compiler_params=_cparams(("arbitrary",)),
    )(u, dt, dt_t, alog_row, alog_col)


def ssd_bwd(name, direction, u, dt, dt_t, alog_row, alog_col, hin, dy, L, add_x=None, add_u=None):
    T = u.shape[0]
    Q, N = SSM_CHUNK, SSM_STATE
    ncl, ncc = L // Q, (T - L) // Q
    nc = ncl + ncc
    cm = lambda s: _ssd_chunk_of(direction, nc - 1 - s, ncl, ncc)
    n_add = (add_x is not None) + (add_u is not None)

    def body(*refs):
        u_ref, dt_ref, dtt_ref, ar_ref, ac_ref, hin_ref, dy_ref = refs[:7]
        add_refs = refs[7:7 + n_add]
        du_ref, ddt_ref, ddtt_ref, dar_ref, dac_ref, dstate = refs[7 + n_add:]
        first = pl.program_id(0) == 0

        @pl.when(first)
        def _():
            dstate[...] = jnp.zeros_like(dstate)

        xps, bs, cs = _ssd_load(u_ref)
        hps = tuple(hin_ref[0, p] for p in range(SSM_PAIRS))
        _, vjp = jax.vjp(functools.partial(_ssd_chunk, direction), xps, bs, cs, dt_ref[...], dtt_ref[...], ar_ref[...],
                         ac_ref[...], hps)
        dys = tuple(dy_ref[:, LANES * p:LANES * (p + 1)] for p in range(SSM_PAIRS))
        dhs = tuple(dstate[p] for p in range(SSM_PAIRS))
        gx, gb, gc, gdt, gdtt, gar, gac, ghp = vjp((dys, dhs))
        parts = list(gx) + list(gb) + list(gc)
        du = jnp.concatenate(parts, axis=1)
        k = 0
        if add_x is not None:
            du = du + jnp.concatenate([add_refs[k][...], jnp.zeros((Q, SSM_CONV_DIM - SSM_INNER), F32)], axis=1)
            k += 1
        if add_u is not None:
            du = du + add_refs[k][...]
        du_ref[...] = du
        ddt_ref[...] = gdt
        ddtt_ref[...] = gdtt
        _accumulate(dar_ref, gar, first)
        _accumulate(dac_ref, gac, first)
        for p in range(SSM_PAIRS):
            dstate[p] = ghp[p]

    in_specs = [pl.BlockSpec((Q, SSM_CONV_DIM), lambda s: (cm(s), 0)),
                pl.BlockSpec((Q, SSM_HEADS), lambda s: (cm(s), 0)),
                pl.BlockSpec((SSM_HEADS, Q), lambda s: (0, cm(s))),
                pl.BlockSpec((1, SSM_HEADS), lambda s: (0, 0)),
                pl.BlockSpec((SSM_HEADS, 1), lambda s: (0, 0)),
                pl.BlockSpec((1, SSM_PAIRS, N, LANES), lambda s: (cm(s), 0, 0, 0)),
                pl.BlockSpec((Q, SSM_INNER), lambda s: (cm(s), 0))]
    ins = [u, dt, dt_t, alog_row, alog_col, hin, dy]
    if add_x is not None:
        in_specs.append(pl.BlockSpec((Q, SSM_INNER), lambda s: (cm(s), 0)))
        ins.append(add_x)
    if add_u is not None:
        in_specs.append(pl.BlockSpec((Q, SSM_CONV_DIM), lambda s: (cm(s), 0)))
        ins.append(add_u)
    return pl.pallas_call(
        body, name=name, grid=(nc,), in_specs=in_specs,
        out_specs=[pl.BlockSpec((Q, SSM_CONV_DIM), lambda s: (cm(s), 0)),
                   pl.BlockSpec((Q, SSM_HEADS), lambda s: (cm(s), 0)),
                   pl.BlockSpec((SSM_HEADS, Q), lambda s: (0, cm(s))),
                   pl.BlockSpec((1, SSM_HEADS), lambda s: (0, 0)),
                   pl.BlockSpec((SSM_HEADS, 1), lambda s: (0, 0))],
        out_shape=[jax.ShapeDtypeStruct((T, SSM_CONV_DIM), F32), jax.ShapeDtypeStruct((T, SSM_HEADS), F32),
                   jax.ShapeDtypeStruct((SSM_HEADS, T), F32), jax.ShapeDtypeStruct((1, SSM_HEADS), F32),
                   jax.ShapeDtypeStruct((SSM_HEADS, 1), F32)],
        scratch_shapes=[pltpu.VMEM((SSM_PAIRS, N, LANES), F32)],
        compiler_params=_cparams(("arbitrary",)),
    )(*ins)


PEER_MASKS = (1, 2, 4, 3, 5, 6, 7)
N_PEERS = len(PEER_MASKS)
MESH_IDS = pl.DeviceIdType.MESH


def _my_index():
    return lax.axis_index("x") * 4 + lax.axis_index("y") * 2 + lax.axis_index("c")


def _coords(idx):
    return (idx // 4, (idx // 2) % 2, idx % 2)


def all_gather_hbm(name, arrays):
    n = len(arrays)
    chip_masks = (4, 2, 6)

    def body(*refs):
        ins, outs = refs[:n], refs[n:2 * n]
        send_sems, recv_sems, local_sems = refs[2 * n:]
        me = _my_index()
        sibling = me ^ 1

        def copy(a, k, block, to, src=None):
            return pltpu.make_async_remote_copy(
                src_ref=outs[a].at[block] if src is None else src, dst_ref=outs[a].at[block],
                send_sem=send_sems.at[a * N_PEERS + k], recv_sem=recv_sems.at[a * N_PEERS + k],
                device_id=_coords(to), device_id_type=MESH_IDS)

        started, own = [], []
        for a in range(n):
            local = pltpu.make_async_copy(ins[a], outs[a].at[me], local_sems.at[a])
            local.start()
            own.append(local)
            first = [copy(a, 0, me, sibling, src=ins[a])] + [copy(a, 1 + j, me, me ^ m, src=ins[a]) for j, m in enumerate(chip_masks)]
            for cp in first:
                cp.start()
            started += first
        for a in range(n):
            for j, m in enumerate(chip_masks):
                copy(a, 1 + j, me ^ m, me).wait_recv()
                fwd = copy(a, 4 + j, me ^ m, sibling)
                fwd.start()
                started.append(fwd)
        for a in range(n):
            copy(a, 0, sibling, me).wait_recv()
            for j, m in enumerate(chip_masks):
                copy(a, 4 + j, sibling ^ m, me).wait_recv()
        for cp in started:
            cp.wait_send()
        for cp in own:
            cp.wait()

    any_spec = pl.BlockSpec(memory_space=pl.ANY)
    return pl.pallas_call(
        body, name=name, in_specs=[any_spec] * n, out_specs=[any_spec] * n,
        out_shape=[jax.ShapeDtypeStruct((N_DEV,) + a.shape, a.dtype) for a in arrays],
        scratch_shapes=[pltpu.SemaphoreType.DMA((n * N_PEERS,)), pltpu.SemaphoreType.DMA((n * N_PEERS,)),
                        pltpu.SemaphoreType.DMA((n,))],
    )(*arrays)


N_CHIPS = N_DEV // 2


def exchange_sibling(name, arrays):
    n = len(arrays)

    def body(*refs):
        ins, outs = refs[:n], refs[n:2 * n]
        send_sems, recv_sems = refs[2 * n:]
        me = _my_index()
        other_core = 1 - me % 2
        copies = []
        for a in range(n):
            for chip in range(N_CHIPS):
                cp = pltpu.make_async_remote_copy(src_ref=ins[a].at[chip, other_core], dst_ref=outs[a].at[chip],
                                                  send_sem=send_sems.at[a * N_CHIPS + chip], recv_sem=recv_sems.at[a * N_CHIPS + chip],
                                                  device_id=_coords(me ^ 1), device_id_type=MESH_IDS)
                cp.start()
                copies.append(cp)
        for cp in copies:
            cp.wait()

    any_spec = pl.BlockSpec(memory_space=pl.ANY)
    return pl.pallas_call(
        body, name=name, in_specs=[any_spec] * n, out_specs=[any_spec] * n,
        out_shape=[jax.ShapeDtypeStruct((N_CHIPS,) + a.shape[2:], a.dtype) for a in arrays],
        scratch_shapes=[pltpu.SemaphoreType.DMA((n * N_CHIPS,)), pltpu.SemaphoreType.DMA((n * N_CHIPS,))],
    )(*arrays)


def exchange_chips(name, arrays):
    n = len(arrays)
    n_other = N_CHIPS - 1

    def body(*refs):
        ins, outs = refs[:n], refs[n:2 * n]
        send_sems, recv_sems, local_sems = refs[2 * n:]
        me = _my_index()
        chip, core = me // 2, me % 2
        copies = []
        for a in range(n):
            local = pltpu.make_async_copy(ins[a].at[chip], outs[a].at[chip], local_sems.at[a])
            local.start()
            copies.append(local)
            for k in range(n_other):
                peer_chip = chip ^ (k + 1)
                cp = pltpu.make_async_remote_copy(src_ref=ins[a].at[peer_chip], dst_ref=outs[a].at[chip],
                                                  send_sem=send_sems.at[a * n_other + k], recv_sem=recv_sems.at[a * n_other + k],
                                                  device_id=_coords(peer_chip * 2 + core), device_id_type=MESH_IDS)
                cp.start()
                copies.append(cp)
        for cp in copies:
            cp.wait()

    any_spec = pl.BlockSpec(memory_space=pl.ANY)
    return pl.pallas_call(
        body, name=name, in_specs=[any_spec] * n, out_specs=[any_spec] * n,
        out_shape=[jax.ShapeDtypeStruct(a.shape, a.dtype) for a in arrays],
        scratch_shapes=[pltpu.SemaphoreType.DMA((n * n_other,)), pltpu.SemaphoreType.DMA((n * n_other,)),
                        pltpu.SemaphoreType.DMA((n,))],
    )(*arrays)


def pair_sum(name, both, got):
    P, _, R, C = both.shape
    tr = _row_tile(R, C, 4)

    def body(a_ref, b_ref, o_ref):
        core = lax.axis_index("c")
        o_ref[...] = (a_ref[0, core].astype(F32) + b_ref[0].astype(F32)).astype(BF16)[None]

    return pl.pallas_call(
        body, name=name, grid=(P, R // tr),
        in_specs=[pl.BlockSpec((1, 2, tr, C), lambda p, i: (p, 0, i, 0)), pl.BlockSpec((1, tr, C), lambda p, i: (p, i, 0))],
        out_specs=pl.BlockSpec((1, tr, C), lambda p, i: (p, i, 0)),
        out_shape=jax.ShapeDtypeStruct((P, R, C), BF16), compiler_params=_cparams(("parallel", "parallel")))(both, got)


def all_gather_vmem(name, v):
    def body(v_ref, out_ref, send_sems, recv_sems):
        me = _my_index()
        out_ref[me] = v_ref[...]
        copies = []
        for k, mask in enumerate(PEER_MASKS):
            cp = pltpu.make_async_remote_copy(src_ref=v_ref, dst_ref=out_ref.at[me], send_sem=send_sems.at[k],
                                              recv_sem=recv_sems.at[k], device_id=_coords(me ^ mask), device_id_type=MESH_IDS)
            cp.start()
            copies.append(cp)
        for cp in copies:
            cp.wait()

    vm = pl.BlockSpec(memory_space=pltpu.VMEM)
    return pl.pallas_call(
        body, name=name, in_specs=[vm], out_specs=vm, out_shape=jax.ShapeDtypeStruct((N_DEV,) + v.shape, v.dtype),
        scratch_shapes=[pltpu.SemaphoreType.DMA((N_PEERS,)), pltpu.SemaphoreType.DMA((N_PEERS,))],
    )(v)


ROW_KERNEL_VMEM_BUDGET = 24 * 1024 * 1024


def _row_tile(rows, cols, bufs):
    budget = ROW_KERNEL_VMEM_BUDGET // (bufs * 2 * 4 * max(cols, LANES))
    if rows <= budget:
        return rows
    for t in range(budget - budget % 16, 15, -16):
        if rows % t == 0:
            return t
    return rows


def sum_parts(name, parts):
    P, R, C = parts.shape
    tr = _row_tile(R, C, P + 1)

    def body(p_ref, o_ref):
        acc = p_ref[0].astype(F32)
        for s in range(1, P):
            acc = acc + p_ref[s].astype(F32)
        o_ref[...] = acc

    return pl.pallas_call(body, name=name, grid=(R // tr,), in_specs=[pl.BlockSpec((P, tr, C), lambda i: (0, i, 0))],
                          out_specs=pl.BlockSpec((tr, C), lambda i: (i, 0)), out_shape=jax.ShapeDtypeStruct((R, C), F32),
                          compiler_params=_cparams(("parallel",)))(parts)


def cast_bf16(name, x):
    R, C = x.shape
    tr = _row_tile(R, C, 2)

    def body(x_ref, o_ref):
        o_ref[...] = x_ref[...].astype(BF16)

    spec = pl.BlockSpec((tr, C), lambda i: (i, 0))
    return pl.pallas_call(body, name=name, grid=(R // tr,), in_specs=[spec], out_specs=spec,
                          out_shape=jax.ShapeDtypeStruct((R, C), BF16), compiler_params=_cparams(("parallel",)))(x)


def adamw(name, w, g, m, v):
    R, C = w.shape
    tr = _row_tile(R, C, 7)

    def body(w_ref, g_ref, m_ref, v_ref, d_ref, nm_ref, nv_ref):
        g = g_ref[...]
        nm = ADAM_B1 * m_ref[...] + (1.0 - ADAM_B1) * g
        nv = ADAM_B2 * v_ref[...] + (1.0 - ADAM_B2) * (g * g)
        m_hat = nm / (1.0 - ADAM_B1 ** ADAM_STEP)
        v_hat = nv / (1.0 - ADAM_B2 ** ADAM_STEP)
        d_ref[...] = -ADAM_LR * (m_hat / (jnp.sqrt(v_hat) + ADAM_EPS) + ADAM_WD * w_ref[...])
        nm_ref[...] = nm
        nv_ref[...] = nv

    spec = pl.BlockSpec((tr, C), lambda i: (i, 0))
    return pl.pallas_call(body, name=name, grid=(R // tr,), in_specs=[spec] * 4, out_specs=[spec] * 3,
                          out_shape=[jax.ShapeDtypeStruct((R, C), F32)] * 3, compiler_params=_cparams(("parallel",)))(w, g, m, v)


def loss_and_grad(name, x, target, L, tm=ROW_TILE):
    T, D = x.shape
    nlt = L // tm

    def body(x_ref, t_ref, loss_ref, dx_ref):
        i = pl.program_id(0)
        err = jnp.where(i < nlt, x_ref[...] - t_ref[...], 0.0)
        dx_ref[...] = err * (1.0 / D)
        part = 0.5 * jnp.sum(jnp.sum(err * err, axis=1, keepdims=True), axis=0, keepdims=True) * (1.0 / D)
        _accumulate(loss_ref, part, i == 0)

    return pl.pallas_call(
        body, name=name, grid=(T // tm,),
        in_specs=[pl.BlockSpec((tm, D), lambda i: (i, 0)), pl.BlockSpec((tm, D), lambda i: (jnp.minimum(i, nlt - 1), 0))],
        out_specs=[pl.BlockSpec((1, 1), lambda i: (0, 0)), pl.BlockSpec((tm, D), lambda i: (i, 0))],
        out_shape=[jax.ShapeDtypeStruct((1, 1), F32), jax.ShapeDtypeStruct((T, D), F32)],
        compiler_params=_cparams(("arbitrary",)))(x, target)


def small_fwd(name, fn, arrays, out_shapes):
    def body(*refs):
        res = fn(*[r[...] for r in refs[:len(arrays)]])
        for o_ref, r in zip(refs[len(arrays):], res):
            o_ref[...] = r

    return pl.pallas_call(body, name=name, out_shape=[jax.ShapeDtypeStruct(s, F32) for s in out_shapes])(*arrays)


def small_bwd(name, fn, arrays, cts):
    n = len(arrays)

    def body(*refs):
        _, vjp = jax.vjp(lambda *a: tuple(fn(*a)), *[r[...] for r in refs[:n]])
        grads = vjp(tuple(r[...] for r in refs[n:n + len(cts)]))
        for o_ref, g in zip(refs[n + len(cts):], grads):
            o_ref[...] = g

    return pl.pallas_call(body, name=name, out_shape=[jax.ShapeDtypeStruct(a.shape, F32) for a in arrays])(*arrays, *cts)


def fn_silu(x):
    return (x * jax.nn.sigmoid(x),)


FWD_NAMES = ["x", "c", "ctx", "c_ctx", "w_mod", "b_mod", "norm1_g", "norm2_g", "w_in", "ssm_conv_w", "ssm_conv_b",
             "ssm_dt_bias", "ssm_a_log", "ssm_d", "ssm_norm_g", "swa_q_norm_g", "swa_k_norm_g", "swa_sink", "mla_q_lat_g",
             "mla_kv_lat_g", "w_mla_uq", "w_mla_ukv", "mla_q_norm_g", "mla_k_norm_g", "w_p_ssm", "w_p_swa", "w_p_mla",
             "w_out", "w_ffn_in", "w_ffn_out"]
WEIGHT_NAMES = FWD_NAMES[3:]
GATHERED = ["w_in", "w_mla_uq", "w_mla_ukv", "w_p_ssm", "w_p_swa", "w_p_mla", "w_out", "w_ffn_in", "w_ffn_out"]
COLUMN_SHARDED = ("w_in", "w_mla_uq", "w_mla_ukv", "w_ffn_in")
REPLICATED = ["c_ctx", "b_mod", "norm1_g", "norm2_g", "ssm_conv_b", "ssm_dt_bias", "ssm_a_log", "ssm_d", "ssm_norm_g",
              "swa_q_norm_g", "swa_k_norm_g", "swa_sink", "mla_q_lat_g", "mla_kv_lat_g", "mla_q_norm_g", "mla_k_norm_g"]
IN_SEGS = [("xbc", SSM_CONV_DIM), ("dt", 2 * SSM_HEADS), ("ks", SWA_KV_HEADS * SWA_HEAD_DIM), ("vs", SWA_KV_HEADS * SWA_HEAD_DIM),
           ("ckv", MLA_KV_RANK), ("kr", MLA_ROPE), ("z", SSM_INNER), ("qs", SWA_Q_HEADS * SWA_HEAD_DIM), ("cq", MLA_Q_RANK),
           ("g1", None), ("g2", None), ("g3", None)]


def _pack(vectors, multiple):
    flat = jnp.concatenate([v.reshape(-1) for v in vectors])
    pad = (-flat.shape[0]) % multiple
    return jnp.pad(flat, (0, pad)).reshape(-1, LANES)


def _unpack(packed, shapes):
    flat, out, off = packed.reshape(-1), [], 0
    for s in shapes:
        n = int(np.prod(s))
        out.append(flat[off:off + n].reshape(s))
        off += n
    return out


def _rope_tables(L, T, rot_dim):
    nf = rot_dim // 4
    inv = jnp.power(ROPE_BASE, -jnp.arange(nf, dtype=F32) / nf)
    r, col = jnp.meshgrid(jnp.arange(L // GRID_W, dtype=F32), jnp.arange(GRID_W, dtype=F32), indexing="ij")
    ang = jnp.stack([r.reshape(-1)[:, None] * inv, col.reshape(-1)[:, None] * inv], axis=1)
    cos, sin = jnp.cos(ang), jnp.sin(ang)
    c = jnp.concatenate([cos[:, 0], cos[:, 0], cos[:, 1], cos[:, 1]], axis=1)
    s = jnp.concatenate([-sin[:, 0], sin[:, 0], -sin[:, 1], sin[:, 1]], axis=1)
    c = jnp.pad(c, ((0, T - L), (0, LANES - rot_dim)), constant_values=1.0)
    s = jnp.pad(s, ((0, T - L), (0, LANES - rot_dim)))
    return c, s


def _pad_rows(a, rows):
    return jnp.pad(a, ((0, rows - a.shape[0]), (0, 0)))


def _row(w, per_head=0, off=0, diff=True):
    return ("row", w, per_head, off, diff)


PAR, PAR_ND = ("par", True), ("par", False)
GRP = ("grp", True)


def kernel(*args):
    n_fwd, n_w = len(FWD_NAMES), len(WEIGHT_NAMES)
    inp = dict(zip(FWD_NAMES, args[:n_fwd]))
    loss_target = args[n_fwd]
    mom_m = dict(zip(WEIGHT_NAMES, args[n_fwd + 1:n_fwd + 1 + n_w]))
    mom_v = dict(zip(WEIGHT_NAMES, args[n_fwd + 1 + n_w:]))

    x, ctx = inp["x"][0], inp["ctx"][0]
    L, D = x.shape
    n_ctx = ctx.shape[0]
    T = L + n_ctx
    depth = inp["w_in"].shape[0]
    me = _my_index()
    in_widths = [w if w is not None else D for _, w in IN_SEGS]
    in_offs = np.concatenate([[0], np.cumsum(in_widths)]).tolist()
    ffn_h = inp["w_ffn_out"].shape[1] * N_DEV
    cfg_swa = AttnCfg(SWA_Q_HEADS, SWA_Q_HEADS // SWA_KV_HEADS, SWA_HEAD_DIM, SWA_HEAD_DIM, SWA_HEAD_DIM ** -0.5, SWA_WINDOW,
                      True, L, T, 256, 1)
    cfg_mla = AttnCfg(MLA_HEADS, 1, MLA_QK_PAD, MLA_V, MLA_QK ** -0.5, None, False, L, T, 1024, 2)
    cfg_mla_bwd = AttnCfg(MLA_HEADS, 1, MLA_QK_PAD, MLA_V, MLA_QK ** -0.5, None, False, L, T, 2048, 1)

    def rf(name, fn, descs, arrays, outs, heads=1):
        return rowop_fwd(name, fn, descs, arrays, outs, T, L, heads=heads)

    def rb(name, fn, descs, arrays, outs, cts, heads=1, add=None, to_matmul=()):
        return rowop_bwd(name, fn, descs, arrays, outs, cts, T, L, heads=heads, add=add, to_matmul=to_matmul)

    local = []
    for n in GATHERED:
        w = inp[n]
        local.append((jnp.swapaxes(w, 1, 2) if n in COLUMN_SHARDED else w).astype(BF16))
    gathered = dict(zip(GATHERED, all_gather_hbm("gather_weights", local)))

    def full(n, l):
        g = gathered[n][:, l]
        return g.reshape(g.shape[0] * g.shape[1], g.shape[2])

    def layer_weights(l):
        wt = {}
        w_in_t = full("w_in", l)
        for (sn, _), o, w in zip(IN_SEGS, in_offs, in_widths):
            seg = w_in_t[o:o + w]
            wt[sn] = _pad_rows(seg, LANES) if sn == "kr" else seg
        uq = full("w_mla_uq", l).reshape(MLA_HEADS, MLA_QK, MLA_Q_RANK)
        wt["uqn"] = uq[:, :MLA_NOPE].reshape(MLA_HEADS * MLA_NOPE, MLA_Q_RANK)
        wt["uqr"] = jnp.pad(uq[:, MLA_NOPE:], ((0, 0), (0, LANES - MLA_ROPE), (0, 0))).reshape(MLA_HEADS * LANES, MLA_Q_RANK)
        ukv = full("w_mla_ukv", l).reshape(MLA_HEADS, MLA_NOPE + MLA_V, MLA_KV_RANK)
        wt["uk"] = ukv[:, :MLA_NOPE].reshape(MLA_HEADS * MLA_NOPE, MLA_KV_RANK)
        wt["uv"] = ukv[:, MLA_NOPE:].reshape(MLA_HEADS * MLA_V, MLA_KV_RANK)
        for n in ("w_p_ssm", "w_p_swa", "w_p_mla", "w_out", "w_ffn_out"):
            wt[n] = full(n, l)
        ffn_in_t = full("w_ffn_in", l)
        wt["fg"], wt["fu"] = ffn_in_t[:ffn_h], ffn_in_t[ffn_h:]
        return wt

    def layer_params(l):
        p = {}
        for n in ("norm1_g", "norm2_g", "ssm_conv_b", "ssm_norm_g", "swa_q_norm_g", "swa_k_norm_g", "mla_q_lat_g", "mla_kv_lat_g"):
            p[n] = inp[n][l][None]
        p["dt_bias"] = inp["ssm_dt_bias"][l].reshape(1, 2 * SSM_HEADS)
        p["alog_row"] = [inp["ssm_a_log"][l][d][None] for d in range(2)]
        p["alog_col"] = [inp["ssm_a_log"][l][d][:, None] for d in range(2)]
        p["d_lane"] = jnp.repeat(inp["ssm_d"][l], SSM_HEAD_DIM)[None]
        p["sink"] = inp["swa_sink"][l].reshape(SWA_Q_HEADS, 1, 1)
        for n, key in (("mla_q_norm_g", "gq"), ("mla_k_norm_g", "gk")):
            g = inp[n][l]
            p[key + "n"] = g[:MLA_NOPE][None]
            p[key + "r"] = jnp.pad(g[MLA_NOPE:], (0, LANES - MLA_ROPE))[None]
        return p

    conv_local = _pack([inp["ssm_conv_w"]], 8 * LANES)
    conv_all = all_gather_vmem("gather_conv_w", conv_local)
    cw = inp["ssm_conv_w"].shape
    conv_full = conv_all.reshape(N_DEV, -1)[:, :cw[0] * cw[1] * cw[2]].reshape(N_DEV, cw[0], cw[1], cw[2])
    conv_full = jnp.moveaxis(conv_full, 0, 2).reshape(cw[0], cw[1], N_DEV * cw[2])
    conv_w8 = jnp.pad(conv_full, ((0, 0), (0, 8 - cw[1]), (0, 0)))

    silu_c, silu_cc = small_fwd("silu_c", lambda a, b: fn_silu(a) + fn_silu(b), [inp["c"], inp["c_ctx"][None]], [(1, D), (1, D)])
    silu_all = all_gather_vmem("gather_silu_c", silu_c.reshape(D // LANES, LANES)).reshape(N_DEV, D)
    S_rows = 2 * N_DEV
    S_mat = jnp.concatenate([silu_all, silu_cc, jnp.zeros((S_rows - N_DEV - 1, D), F32)], axis=0)
    mod_cols = inp["w_mod"].shape[2]
    mods_local = []
    for l in range(depth):
        bias = lax.dynamic_slice(inp["b_mod"][l], (me * mod_cols,), (mod_cols,))
        mods_local.append(matmul(S_mat, inp["w_mod"][l], "nn", f"mod{l}", add=jnp.broadcast_to(bias[None], (S_rows, mod_cols))))
    mods_all = all_gather_vmem("gather_mods", jnp.stack(mods_local).reshape(-1, LANES))
    mods_all = jnp.moveaxis(mods_all.reshape(N_DEV, depth, S_rows, mod_cols), 0, 2).reshape(depth, S_rows, N_DEV * mod_cols)
    mods_lat = lax.dynamic_slice(mods_all, (0, me, 0), (depth, 1, N_DEV * mod_cols))[:, 0]
    mods_ctx = mods_all[:, N_DEV]

    def layer_mods(l):
        return [jnp.stack([mods_lat[l, j * D:(j + 1) * D], mods_ctx[l, j * D:(j + 1) * D]])[:, None] for j in range(6)]

    cs_swa = _rope_tables(L, T, SWA_HEAD_DIM)
    cs_mla = _rope_tables(L, T, MLA_ROPE)
    nm_descs = [_row(D), PAR, GRP, GRP]
    resid_descs = [_row(D, diff=False), _row(D), GRP]
    tab = [_row(LANES, diff=False), _row(LANES, diff=False)]
    swaq_descs = [_row(SWA_HEAD_DIM, 1), PAR] + tab
    swakv_descs = [_row(SWA_HEAD_DIM, 1), _row(SWA_HEAD_DIM, 1), PAR] + tab
    mlaq_descs = [_row(LANES, 1), _row(LANES, 1), PAR, PAR] + tab
    mlakv_descs = [_row(LANES, 1), _row(LANES, 1), _row(LANES), PAR, PAR] + tab
    ssdout_descs = [_row(SSM_INNER), _row(SSM_INNER, diff=False), _row(SSM_INNER), _row(SSM_INNER), PAR, PAR]
    merge_descs = [_row(D)] * 6
    swiglu_descs = [_row(ffn_h), _row(ffn_h)]
    seg_names = [sn for sn, _ in IN_SEGS]
    seg_groups = [seg_names[:7], seg_names[7:]]

    def layer_fwd(l, X, wt, p, mods):
        sh1, sc1, gt1, sh2, sc2, gt2 = mods
        r = {"X": X}
        r["h1"] = rf(f"l{l}_norm1", fn_norm_mod, nm_descs, [X, p["norm1_g"], sh1, sc1], [(D, 0, BF16)])[0]
        for gi, group in enumerate(seg_groups):
            r.update(zip(group, matmul_multi(f"l{l}_in{gi}", r["h1"], [wt[sn] for sn in group])))
        r["u"] = conv_fwd(f"l{l}_conv", r["xbc"], conv_w8[l], p["ssm_conv_b"], L)
        r["dts"] = rf(f"l{l}_softplus", fn_softplus, [_row(2 * SSM_HEADS), PAR], [r["dt"], p["dt_bias"]], [(2 * SSM_HEADS, 0, F32)])[0]
        for d in range(2):
            dt_d = r["dts"][:, d * SSM_HEADS:(d + 1) * SSM_HEADS]
            r[f"dt{d}"], r[f"dtt{d}"] = dt_d, dt_d.T
            r[f"y{d}"], r[f"hin{d}"] = ssd_fwd(f"l{l}_ssd{d}", d, r["u"], dt_d, dt_d.T, p["alog_row"][d], p["alog_col"][d], L)
        r["ys"] = rf(f"l{l}_ssd_out", fn_ssd_out, ssdout_descs, [r["y0"], r["y1"], r["u"], r["z"], p["d_lane"], p["ssm_norm_g"]],
                     [(SSM_INNER, 0, BF16)])[0]
        r["Qs"] = rf(f"l{l}_swa_q", fn_swa_q, swaq_descs, [r["qs"], p["swa_q_norm_g"], *cs_swa], [(SWA_HEAD_DIM, 1, BF16)], SWA_Q_HEADS)[0]
        r["Ks"], r["Vs"] = rf(f"l{l}_swa_kv", fn_swa_kv, swakv_descs, [r["ks"], r["vs"], p["swa_k_norm_g"], *cs_swa],
                              [(SWA_HEAD_DIM, 1, BF16), (SWA_HEAD_DIM, 1, BF16)], SWA_KV_HEADS)
        r["Os"], r["lse_s"] = flash_fwd(f"l{l}_swa_fwd", cfg_swa, r["Qs"], r["Ks"], r["Vs"], p["sink"])
        r["cqn"] = rf(f"l{l}_q_lat", fn_rms, [_row(MLA_Q_RANK), PAR], [r["cq"], p["mla_q_lat_g"]], [(MLA_Q_RANK, 0, BF16)])[0]
        r["qn"], r["qr"] = matmul_multi(f"l{l}_uq", r["cqn"], [wt["uqn"], wt["uqr"]])
        r["Qm"] = rf(f"l{l}_mla_q", fn_mla_q, mlaq_descs, [r["qn"], r["qr"], p["gqn"], p["gqr"], *cs_mla], [(MLA_QK_PAD, 1, BF16)], MLA_HEADS)[0]
        r["ckvn"] = rf(f"l{l}_kv_lat", fn_rms, [_row(MLA_KV_RANK), PAR], [r["ckv"], p["mla_kv_lat_g"]], [(MLA_KV_RANK, 0, BF16)])[0]
        r["kn"], r["vp"] = matmul_multi(f"l{l}_ukv", r["ckvn"], [wt["uk"], wt["uv"]])
        r["Km"], r["Vm"] = rf(f"l{l}_mla_kv", fn_mla_kv, mlakv_descs, [r["kn"], r["vp"], r["kr"], p["gkn"], p["gkr"], *cs_mla],
                              [(MLA_QK_PAD, 1, BF16), (MLA_V, 1, BF16)], MLA_HEADS)
        r["Om"], r["lse_m"] = flash_fwd(f"l{l}_mla_fwd", cfg_mla, r["Qm"], r["Km"], r["Vm"], None)
        r["P1"] = matmul(r["ys"], wt["w_p_ssm"], "nn", f"l{l}_p_ssm")
        r["P2"] = matmul(r["Os"], wt["w_p_swa"], "nn", f"l{l}_p_swa")
        r["P3"] = matmul(r["Om"], wt["w_p_mla"], "nn", f"l{l}_p_mla")
        r["mg"] = rf(f"l{l}_merge", fn_merge, merge_descs, [r["g1"], r["g2"], r["g3"], r["P1"], r["P2"], r["P3"]], [(D, 0, BF16)])[0]
        r["A"] = matmul(r["mg"], wt["w_out"], "nn", f"l{l}_out")
        r["X1"] = rf(f"l{l}_resid1", fn_resid, resid_descs, [X, r["A"], gt1], [(D, 0, F32)])[0]
        r["h2"] = rf(f"l{l}_norm2", fn_norm_mod, nm_descs, [r["X1"], p["norm2_g"], sh2, sc2], [(D, 0, BF16)])[0]
        r["Fg"] = matmul(r["h2"], wt["fg"], "nt", f"l{l}_ffn_g")
        r["Fu"] = matmul(r["h2"], wt["fu"], "nt", f"l{l}_ffn_u")
        r["sg"] = rf(f"l{l}_swiglu", fn_swiglu, swiglu_descs, [r["Fg"], r["Fu"]], [(ffn_h, 0, BF16)])[0]
        r["B"] = matmul(r["sg"], wt["w_ffn_out"], "nn", f"l{l}_ffn_out")
        X2 = rf(f"l{l}_resid2", fn_resid, resid_descs, [r["X1"], r["B"], gt2], [(D, 0, F32)])[0]
        return X2, r

    def attn_bwd(tag, cfg, q, k, v, o, lse, do, sink):
        res = attn_delta(f"{tag}_delta", cfg, o, do, lse, sink)
        delta, dob = res[0], res[1]
        dsink = res[2] if cfg.has_sink else None
        dq, dk, dv = flash_bwd_fused(f"{tag}_bwd", cfg, q, k, v, dob, lse, delta)
        return dq, dk, dv, dsink

    def layer_bwd(l, dX2, r, wt, p, mods):
        sh1, sc1, gt1, sh2, sc2, gt2 = mods
        g, gw = {}, {}
        dmod = [None] * 6
        dB, dmod[5] = rb(f"l{l}_resid2_b", fn_resid, resid_descs, [r["X1"], r["B"], gt2], [(D, 0, F32)], [dX2], to_matmul=(1,))
        dsg = matmul(dB, wt["w_ffn_out"], "nt", f"l{l}_ffn_out_da")
        gw["w_ffn_out"] = matmul(r["sg"], dB, "tn", f"l{l}_ffn_out_dw")
        dFg, dFu = rb(f"l{l}_swiglu_b", fn_swiglu, swiglu_descs, [r["Fg"], r["Fu"]], [(ffn_h, 0, BF16)], [dsg], to_matmul=(0, 1))
        dh2 = matmul(dFg, wt["fg"], "nn", f"l{l}_ffn_g_da")
        dh2 = matmul(dFu, wt["fu"], "nn", f"l{l}_ffn_u_da", add=dh2)
        gw["w_ffn_in"] = jnp.concatenate([matmul(r["h2"], dFg, "tn", f"l{l}_ffn_g_dw"), matmul(r["h2"], dFu, "tn", f"l{l}_ffn_u_dw")], axis=1)
        dX1, g["norm2_g"], dmod[3], dmod[4] = rb(f"l{l}_norm2_b", fn_norm_mod, nm_descs, [r["X1"], p["norm2_g"], sh2, sc2],
                                                [(D, 0, BF16)], [dh2], add={0: dX2})
        dA, dmod[2] = rb(f"l{l}_resid1_b", fn_resid, resid_descs, [r["X"], r["A"], gt1], [(D, 0, F32)], [dX1], to_matmul=(1,))
        dmg = matmul(dA, wt["w_out"], "nt", f"l{l}_out_da")
        gw["w_out"] = matmul(r["mg"], dA, "tn", f"l{l}_out_dw")
        dsegs = {}
        dsegs["g1"], dsegs["g2"], dsegs["g3"], dP1, dP2, dP3 = rb(
            f"l{l}_merge_b", fn_merge, merge_descs, [r["g1"], r["g2"], r["g3"], r["P1"], r["P2"], r["P3"]], [(D, 0, BF16)], [dmg],
            to_matmul=tuple(range(6)))
        dys = matmul(dP1, wt["w_p_ssm"], "nt", f"l{l}_p_ssm_da")
        dOs = matmul(dP2, wt["w_p_swa"], "nt", f"l{l}_p_swa_da")
        dOm = matmul(dP3, wt["w_p_mla"], "nt", f"l{l}_p_mla_da")
        gw["w_p_ssm"] = matmul(r["ys"], dP1, "tn", f"l{l}_p_ssm_dw")
        gw["w_p_swa"] = matmul(r["Os"], dP2, "tn", f"l{l}_p_swa_dw")
        gw["w_p_mla"] = matmul(r["Om"], dP3, "tn", f"l{l}_p_mla_dw")
        dQm, dKm, dVm, _ = attn_bwd(f"l{l}_mla", cfg_mla_bwd, r["Qm"], r["Km"], r["Vm"], r["Om"], r["lse_m"], dOm, None)
        dkn, dvp, dsegs["kr"], dgkn, dgkr = rb(f"l{l}_mla_kv_b", fn_mla_kv, mlakv_descs,
                                               [r["kn"], r["vp"], r["kr"], p["gkn"], p["gkr"], *cs_mla],
                                               [(MLA_QK_PAD, 1, BF16), (MLA_V, 1, BF16)], [dKm, dVm], MLA_HEADS, to_matmul=(0, 1, 2))
        dckvn = matmul_sum(f"l{l}_ukv_da", [(dkn, wt["uk"]), (dvp, wt["uv"])])
        dw_uk = matmul(r["ckvn"], dkn, "tn", f"l{l}_uk_dw").reshape(MLA_KV_RANK, MLA_HEADS, MLA_NOPE)
        dw_uv = matmul(r["ckvn"], dvp, "tn", f"l{l}_uv_dw").reshape(MLA_KV_RANK, MLA_HEADS, MLA_V)
        gw["w_mla_ukv"] = jnp.concatenate([dw_uk, dw_uv], axis=2).reshape(MLA_KV_RANK, -1)
        dsegs["ckv"], g["mla_kv_lat_g"] = rb(f"l{l}_kv_lat_b", fn_rms, [_row(MLA_KV_RANK), PAR], [r["ckv"], p["mla_kv_lat_g"]],
                                             [(MLA_KV_RANK, 0, BF16)], [dckvn], to_matmul=(0,))
        dqn, dqr, dgqn, dgqr = rb(f"l{l}_mla_q_b", fn_mla_q, mlaq_descs, [r["qn"], r["qr"], p["gqn"], p["gqr"], *cs_mla],
                                  [(MLA_QK_PAD, 1, BF16)], [dQm], MLA_HEADS, to_matmul=(0, 1))
        dcqn = matmul_sum(f"l{l}_uq_da", [(dqn, wt["uqn"]), (dqr, wt["uqr"])])
        dw_uqn = matmul(r["cqn"], dqn, "tn", f"l{l}_uqn_dw").reshape(MLA_Q_RANK, MLA_HEADS, MLA_NOPE)
        dw_uqr = matmul(r["cqn"], dqr, "tn", f"l{l}_uqr_dw").reshape(MLA_Q_RANK, MLA_HEADS, LANES)[:, :, :MLA_ROPE]
        gw["w_mla_uq"] = jnp.concatenate([dw_uqn, dw_uqr], axis=2).reshape(MLA_Q_RANK, -1)
        dsegs["cq"], g["mla_q_lat_g"] = rb(f"l{l}_q_lat_b", fn_rms, [_row(MLA_Q_RANK), PAR], [r["cq"], p["mla_q_lat_g"]],
                                           [(MLA_Q_RANK, 0, BF16)], [dcqn], to_matmul=(0,))
        g["mla_q_norm_g"] = jnp.concatenate([dgqn[0], dgqr[0, :MLA_ROPE]])
        g["mla_k_norm_g"] = jnp.concatenate([dgkn[0], dgkr[0, :MLA_ROPE]])
        dQs, dKs, dVs, dsink = attn_bwd(f"l{l}_swa", cfg_swa, r["Qs"], r["Ks"], r["Vs"], r["Os"], r["lse_s"], dOs, p["sink"])
        g["swa_sink"] = dsink.reshape(SWA_Q_HEADS)
        dsegs["qs"], g["swa_q_norm_g"] = rb(f"l{l}_swa_q_b", fn_swa_q, swaq_descs, [r["qs"], p["swa_q_norm_g"], *cs_swa],
                                            [(SWA_HEAD_DIM, 1, BF16)], [dQs], SWA_Q_HEADS, to_matmul=(0,))
        dsegs["ks"], dsegs["vs"], g["swa_k_norm_g"] = rb(f"l{l}_swa_kv_b", fn_swa_kv, swakv_descs,
                                                         [r["ks"], r["vs"], p["swa_k_norm_g"], *cs_swa],
                                                         [(SWA_HEAD_DIM, 1, BF16), (SWA_HEAD_DIM, 1, BF16)], [dKs, dVs], SWA_KV_HEADS,
                                                         to_matmul=(0, 1))
        dy, dxs, dsegs["z"], dd_lane, g["ssm_norm_g"] = rb(
            f"l{l}_ssd_out_b", fn_ssd_out, ssdout_descs, [r["y0"], r["y1"], r["u"], r["z"], p["d_lane"], p["ssm_norm_g"]],
            [(SSM_INNER, 0, BF16)], [dys], to_matmul=(3,))
        g["ssm_d"] = dd_lane.reshape(SSM_HEADS, SSM_HEAD_DIM).sum(axis=1)
        du, ddts, dalog = None, [], []
        for d in range(2):
            du, ddt, ddtt, dar, dac = ssd_bwd(f"l{l}_ssd{d}_b", d, r["u"], r[f"dt{d}"], r[f"dtt{d}"], p["alog_row"][d], p["alog_col"][d],
                                              r[f"hin{d}"], dy, L, add_x=dxs if d == 0 else None, add_u=du)
            ddts.append(ddt + ddtt.T)
            dalog.append(dar[0] + dac[:, 0])
        g["ssm_a_log"] = jnp.stack(dalog)
        dsegs["xbc"], dconv_w, g["ssm_conv_b"] = conv_bwd(f"l{l}_conv_b", r["xbc"], conv_w8[l], p["ssm_conv_b"], du, L)
        dsegs["dt"], ddt_bias = rb(f"l{l}_softplus_b", fn_softplus, [_row(2 * SSM_HEADS), PAR], [r["dt"], p["dt_bias"]],
                                   [(2 * SSM_HEADS, 0, F32)], [jnp.concatenate(ddts, axis=1)], to_matmul=(0,))
        g["ssm_dt_bias"] = ddt_bias.reshape(2, SSM_HEADS)
        g["ssm_conv_w"] = dconv_w[:SSM_CONV]
        dh1, dws = None, []
        for gi, group in enumerate(seg_groups):
            dh1 = matmul_sum(f"l{l}_in_da{gi}", [(dsegs[sn], wt[sn]) for sn in group], add=dh1)
        for sn, w in zip(seg_names, in_widths):
            dws.append(matmul(r["h1"], dsegs[sn], "tn", f"l{l}_in_{sn}_dw")[:, :w])
        gw["w_in"] = jnp.concatenate(dws, axis=1)
        dX, g["norm1_g"], dmod[0], dmod[1] = rb(f"l{l}_norm1_b", fn_norm_mod, nm_descs, [r["X"], p["norm1_g"], sh1, sc1],
                                               [(D, 0, BF16)], [dh1], add={0: dX1})
        for n in ("norm1_g", "norm2_g", "ssm_conv_b", "ssm_norm_g", "swa_q_norm_g", "swa_k_norm_g", "mla_q_lat_g", "mla_kv_lat_g"):
            g[n] = g[n][0]
        dmod_lat = jnp.concatenate([dm[0, 0] for dm in dmod])
        dmod_ctx = jnp.concatenate([dm[1, 0] for dm in dmod])
        return dX, g, gw, dmod_lat, dmod_ctx

    X = jnp.concatenate([x, ctx], axis=0)
    saved = []
    for l in range(depth):
        wt, p, mods = layer_weights(l), layer_params(l), layer_mods(l)
        X, r = layer_fwd(l, X, wt, p, mods)
        saved.append((r, wt, p, mods))
    loss_part, dX = loss_and_grad("loss", X, loss_target[0], L)
    loss = lax.psum(loss_part[0, 0], ("x", "y", "c"))
    small_g = [None] * depth
    big_g = [None] * depth
    dmods = [None] * depth
    for l in reversed(range(depth)):
        r, wt, p, mods = saved[l]
        dX, small_g[l], big_g[l], dm_lat, dm_ctx = layer_bwd(l, dX, r, wt, p, mods)
        dmods[l] = jnp.stack([dm_lat, dm_ctx])
    grad_x = dX[:L][None]

    dm_all = all_gather_vmem("gather_dmods", jnp.stack(dmods).reshape(-1, LANES)).reshape(N_DEV, depth, 2, N_DEV * mod_cols)
    dm_rows = jnp.concatenate([jnp.moveaxis(dm_all[:, :, 0], 0, 1), dm_all[:, :, 1].sum(axis=0)[:, None],
                               jnp.zeros((depth, S_rows - N_DEV - 1, N_DEV * mod_cols), F32)], axis=1)
    dm_mine = lax.dynamic_slice(dm_rows, (0, 0, me * mod_cols), (depth, S_rows, mod_cols))
    grads = {}
    grads["w_mod"] = jnp.stack([matmul(S_mat, dm_mine[l], "tn", f"mod{l}_dw") for l in range(depth)])
    d_silu = None
    for l in range(depth):
        d_silu = matmul(dm_mine[l], inp["w_mod"][l], "nt", f"mod{l}_da", add=d_silu)
    small = {n: jnp.stack([small_g[l][n] for l in range(depth)]) for n in small_g[0]}
    small["c_ctx"] = small_bwd("silu_c_b", fn_silu, [inp["c_ctx"][None]], [d_silu[N_DEV:N_DEV + 1]])[0][0]
    small["b_mod"] = jnp.stack(dmods).sum(axis=1)

    rep_shapes = [inp[n].shape for n in REPLICATED]
    conv_shape = (depth, SSM_CONV, SSM_CONV_DIM)
    packed = _pack([small[n] for n in REPLICATED] + [small["ssm_conv_w"]], 8 * LANES)
    small_sum = sum_parts("sum_small", all_gather_vmem("gather_small", packed))
    summed = _unpack(small_sum, rep_shapes + [conv_shape])
    for n, gsum in zip(REPLICATED, summed):
        grads[n] = gsum
    grads["ssm_conv_w"] = lax.dynamic_slice(summed[-1], (0, 0, me * cw[2]), cw)

    slabs = []
    for n in GATHERED:
        gfull = jnp.stack([big_g[l][n] for l in range(depth)])
        if n in COLUMN_SHARDED:
            k_dim, n_dim = gfull.shape[1], gfull.shape[2]
            slab = jnp.moveaxis(gfull.reshape(depth, k_dim, N_DEV, n_dim // N_DEV), 2, 0)
        else:
            k_dim, n_dim = gfull.shape[1], gfull.shape[2]
            slab = jnp.moveaxis(gfull.reshape(depth, N_DEV, k_dim // N_DEV, n_dim), 1, 0)
        slab = cast_bf16(f"cast_{n}", slab.reshape(-1, slab.shape[-1])).reshape(slab.shape)
        slabs.append(slab.reshape((N_CHIPS, 2) + slab.shape[1:]))
    from_sibling = exchange_sibling("exchange_sibling", slabs)
    chip_sums = []
    for n, slab, got in zip(GATHERED, slabs, from_sibling):
        shp = inp[n].shape
        rows = shp[0] * shp[1]
        chip_sums.append(pair_sum(f"pair_{n}", slab.reshape(N_CHIPS, 2, rows, shp[2]), got.reshape(N_CHIPS, rows, shp[2])))
    for n, parts in zip(GATHERED, exchange_chips("exchange_chips", chip_sums)):
        grads[n] = sum_parts(f"sum_{n}", parts).reshape(inp[n].shape)

    delta, new_m, new_v = {}, {}, {}
    rep_pack = lambda d: _pack([d[n] for n in REPLICATED], 8 * LANES)
    rep_out = adamw("adamw_small", rep_pack(inp), rep_pack(grads), rep_pack(mom_m), rep_pack(mom_v))
    for out, res in zip((delta, new_m, new_v), rep_out):
        for n, a in zip(REPLICATED, _unpack(res, rep_shapes)):
            out[n] = a
    for n in ["w_mod", "ssm_conv_w"] + GATHERED:
        shp = inp[n].shape
        two_d = (shp[0] * shp[1], shp[2])
        res = adamw(f"adamw_{n}", inp[n].reshape(two_d), grads[n].reshape(two_d), mom_m[n].reshape(two_d), mom_v[n].reshape(two_d))
        delta[n], new_m[n], new_v[n] = [a.reshape(shp) for a in res]

    return (loss, grad_x, *[grads[n] for n in WEIGHT_NAMES], *[delta[n] for n in WEIGHT_NAMES],
            *[new_m[n] for n in WEIGHT_NAMES], *[new_v[n] for n in WEIGHT_NAMES])
```

```python
import functools
import math

import numpy as np
import jax
import jax.numpy as jnp
from jax import lax
from jax.experimental import pallas as pl
from jax.experimental.pallas import tpu as pltpu

F32 = jnp.float32
BF16 = jnp.bfloat16

N_DEV = 8
V7X_VMEM_BYTES = 64 * 1024 * 1024
VMEM_LIMIT_BYTES = V7X_VMEM_BYTES - 8 * 1024 * 1024
LANES = 128

EPS = 1e-6
ROPE_BASE = 10000.0
GRID_W = 64
SSM_HEADS, SSM_HEAD_DIM, SSM_GROUPS, SSM_STATE, SSM_CONV, SSM_CHUNK = 16, 64, 2, 128, 5, 128
SSM_INNER = SSM_HEADS * SSM_HEAD_DIM
SSM_CONV_DIM = SSM_INNER + 2 * SSM_GROUPS * SSM_STATE
SWA_Q_HEADS, SWA_KV_HEADS, SWA_HEAD_DIM, SWA_WINDOW = 8, 2, 128, 128
MLA_HEADS, MLA_Q_RANK, MLA_KV_RANK, MLA_NOPE, MLA_ROPE, MLA_V = 8, 384, 256, 128, 64, 128
MLA_QK = MLA_NOPE + MLA_ROPE
MLA_QK_PAD = 2 * LANES
ADAM_LR, ADAM_B1, ADAM_B2, ADAM_EPS, ADAM_WD, ADAM_STEP = 0.001, 0.9, 0.999, 1e-08, 0.01, 10

ROW_TILE = 256


def _cparams(sem, **kw):
    return pltpu.CompilerParams(dimension_semantics=sem, vmem_limit_bytes=VMEM_LIMIT_BYTES, **kw)


def _pick(dim, prefs):
    for p in prefs:
        if dim % p == 0:
            return p
    return dim


def matmul(a, b, mode, name, out_dtype=F32, add=None):
    if mode == "nn":
        (M, K), (K2, N) = a.shape, b.shape
    elif mode == "nt":
        (M, K), (N, K2) = a.shape, b.shape
    else:
        (K, M), (K2, N) = a.shape, b.shape
    assert K == K2, (name, a.shape, b.shape)
    has_add = add is not None
    tm, tn, tk = _matmul_tiles(M, N, K, a.dtype.itemsize, b.dtype.itemsize, jnp.dtype(out_dtype).itemsize, has_add,
                                   m_on_lanes=(mode == "tn"))
    nk = K // tk
    dims = {"nn": (((1,), (0,)), ((), ())), "nt": (((1,), (1,)), ((), ())), "tn": (((0,), (0,)), ((), ()))}[mode]
    a_spec = pl.BlockSpec((tk, tm), lambda i, j, k: (k, i)) if mode == "tn" else pl.BlockSpec((tm, tk), lambda i, j, k: (i, k))
    b_spec = pl.BlockSpec((tn, tk), lambda i, j, k: (j, k)) if mode == "nt" else pl.BlockSpec((tk, tn), lambda i, j, k: (k, j))
    o_spec = pl.BlockSpec((tm, tn), lambda i, j, k: (i, j))

    def body(*refs):
        a_ref, b_ref = refs[:2]
        c_ref = refs[2] if has_add else None
        o_ref = refs[3] if has_add else refs[2]
        part = lax.dot_general(a_ref[...].astype(BF16), b_ref[...].astype(BF16), dims, preferred_element_type=F32)
        if nk == 1:
            o_ref[...] = (part + c_ref[...] if has_add else part).astype(o_ref.dtype)
            return
        acc_ref = refs[-1]
        k = pl.program_id(2)

        @pl.when(k == 0)
        def _():
            acc_ref[...] = part + c_ref[...] if has_add else part

        @pl.when(k > 0)
        def _():
            acc_ref[...] += part

        @pl.when(k == nk - 1)
        def _():
            o_ref[...] = acc_ref[...].astype(o_ref.dtype)

    ins = [a, b] + ([add] if has_add else [])
    in_specs = [a_spec, b_spec] + ([o_spec] if has_add else [])
    return pl.pallas_call(
        body, name=name, grid=(M // tm, N // tn, nk), in_specs=in_specs, out_specs=o_spec,
        out_shape=jax.ShapeDtypeStruct((M, N), out_dtype),
        scratch_shapes=[pltpu.VMEM((tm, tn), F32)] if nk > 1 else [],
        input_output_aliases=({2: 0} if has_add else {}),
        compiler_params=_cparams(("parallel", "parallel", "arbitrary")),
    )(*ins)


def matmul_sum(name, pairs, add=None):
    M, N = pairs[0][0].shape[0], pairs[0][1].shape[1]
    n = len(pairs)
    has_add = add is not None
    resident = sum(2 * b.shape[0] * N * b.dtype.itemsize for _, b in pairs)
    tm = next((t for t in (768, 512, 384, 256, 128) if M % t == 0 and resident + sum(
        2 * t * a.shape[1] * a.dtype.itemsize + t * a.shape[1] * 2 for a, _ in pairs) + 6 * t * N * 4 <= MATMUL_VMEM_BUDGET), None)
    assert tm is not None, name

    def body(*refs):
        acc = refs[2 * n][...] if has_add else None
        for s in range(n):
            part = jnp.dot(refs[2 * s][...].astype(BF16), refs[2 * s + 1][...].astype(BF16), preferred_element_type=F32)
            acc = part if acc is None else acc + part
        refs[-1][...] = acc

    o_spec = pl.BlockSpec((tm, N), lambda i: (i, 0))
    in_specs, ins = [], []
    for a, b in pairs:
        in_specs += [pl.BlockSpec((tm, a.shape[1]), lambda i: (i, 0)), pl.BlockSpec(b.shape, lambda i: (0, 0))]
        ins += [a, b]
    if has_add:
        in_specs.append(o_spec)
        ins.append(add)
    return pl.pallas_call(body, name=name, grid=(M // tm,), in_specs=in_specs, out_specs=o_spec,
                          out_shape=jax.ShapeDtypeStruct((M, N), F32), input_output_aliases=({2 * n: 0} if has_add else {}),
                          compiler_params=_cparams(("parallel",)))(*ins)


def matmul_multi(name, a, bs, out_dtypes=None):
    out_dtypes = out_dtypes or [F32] * len(bs)
    M, K = a.shape
    n = len(bs)
    resident = sum(2 * b.shape[0] * K * b.dtype.itemsize for b in bs)
    n_total = sum(b.shape[0] for b in bs)
    tm = next((t for t in (768, 512, 384, 256, 128) if M % t == 0 and
               resident + 2 * t * K * a.dtype.itemsize + 3 * t * n_total * 4 <= MATMUL_VMEM_BUDGET), None)
    assert tm is not None, name

    def body(*refs):
        lhs = refs[0][...].astype(BF16)
        for s in range(n):
            out = lax.dot_general(lhs, refs[1 + s][...].astype(BF16), NT_DIMS, preferred_element_type=F32)
            refs[1 + n + s][...] = out.astype(out_dtypes[s])

    in_specs = [pl.BlockSpec((tm, K), lambda i: (i, 0))] + [pl.BlockSpec(b.shape, lambda i: (0, 0)) for b in bs]
    return pl.pallas_call(
        body, name=name, grid=(M // tm,), in_specs=in_specs,
        out_specs=[pl.BlockSpec((tm, b.shape[0]), lambda i: (i, 0)) for b in bs],
        out_shape=[jax.ShapeDtypeStruct((M, b.shape[0]), dt) for b, dt in zip(bs, out_dtypes)],
        compiler_params=_cparams(("parallel",)))(a, *bs)


MATMUL_VMEM_BUDGET = 36 * 1024 * 1024


def _matmul_tiles(M, N, K, a_bytes, b_bytes, o_bytes, has_add, m_on_lanes=False):
    tk = K if K <= 1536 else _pick(K, (1408, 1024, 768, 704, 512, 256))
    nk = K // tk
    m_cands = [t for t in (1024, 768, 512, 384, 256, 128) if M % t == 0] or [M]
    if M % 768 and M % 1024:
        m_cands += [t for t in (1408, 704, 352) if M % t == 0 and not (m_on_lanes and t % LANES)]
    n_cands = [t for t in range(LANES, min(N, 2816) + 1, LANES) if N % t == 0] or [N]
    best = None
    for tm in m_cands:
        for tn in n_cands:
            pipeline = 2 * (tm * tk * a_bytes + tk * tn * b_bytes + tm * tn * o_bytes) + (2 * tm * tn * 4 if has_add else 0)
            temps = tm * tn * 4 * (2 if nk > 1 else 1) + (tm * tk * 2 if a_bytes == 4 else 0) + (tk * tn * 2 if b_bytes == 4 else 0)
            if pipeline + temps <= MATMUL_VMEM_BUDGET:
                score = (tm * tn, tn)
                if best is None or score > best[0]:
                    best = (score, tm, tn)
    if best is None:
        return m_cands[-1], n_cands[0], tk
    return best[1], best[2], tk


def _row_specs(descs, arrays, tm, nct, heads):
    specs = []
    for d, arr in zip(descs, arrays):
        if d[0] == "row":
            _, w, per_head, off, _ = d
            specs.append(pl.BlockSpec((tm, w * (heads if per_head else 1)), lambda i, off=off: (i, off)))
        elif d[0] == "par":
            specs.append(pl.BlockSpec(arr.shape, lambda i, nd=arr.ndim: (0,) * nd))
        else:
            specs.append(pl.BlockSpec((1,) + arr.shape[1:], lambda i, nd=arr.ndim: (jnp.where(i >= nct, 1, 0),) + (0,) * (nd - 1)))
    return specs


def _load(d, ref, h):
    if d[0] == "grp":
        return ref[0]
    if d[0] == "row" and d[2]:
        return ref[:, h * d[1]:(h + 1) * d[1]].astype(F32)
    return ref[...].astype(F32) if d[0] == "row" else ref[...]


def _out_specs(outs, tm, heads):
    return [pl.BlockSpec((tm, w * (heads if ph else 1)), lambda i: (i, 0)) for (w, ph, _) in outs]


def rowop_fwd(name, fn, descs, arrays, outs, T, n_ctx, heads=1, tm=ROW_TILE):
    nct = n_ctx // tm
    n_in = len(descs)

    def body(*refs):
        for h in range(heads):
            res = fn(*[_load(d, r, h) for d, r in zip(descs, refs[:n_in])])
            for o_ref, r, (w, ph, _) in zip(refs[n_in:], res, outs):
                if ph:
                    o_ref[:, h * w:(h + 1) * w] = r.astype(o_ref.dtype)
                else:
                    o_ref[...] = r.astype(o_ref.dtype)

    out_shape = [jax.ShapeDtypeStruct((T, w * (heads if ph else 1)), dt) for (w, ph, dt) in outs]
    return pl.pallas_call(
        body, name=name, grid=(T // tm,), in_specs=_row_specs(descs, arrays, tm, nct, heads), out_specs=_out_specs(outs, tm, heads),
        out_shape=out_shape, compiler_params=_cparams(("parallel",)),
    )(*arrays)


def rowop_bwd(name, fn, descs, arrays, outs, cts, T, n_ctx, heads=1, tm=ROW_TILE, add=None, to_matmul=()):
    nct = n_ctx // tm
    n_in, n_ct = len(descs), len(cts)
    add = add or {}
    diff_idx = [k for k, d in enumerate(descs) if d[-1]]
    add_idx = [k for k in diff_idx if k in add]

    def body(*refs):
        in_refs, ct_refs = refs[:n_in], refs[n_in:n_in + n_ct]
        add_refs = dict(zip(add_idx, refs[n_in + n_ct:n_in + n_ct + len(add_idx)]))
        g_refs = refs[n_in + n_ct + len(add_idx):]
        i = pl.program_id(0)
        shared = {}
        for h in range(heads):
            vals = [_load(d, r, h) for d, r in zip(descs, in_refs)]

            def f(*dvals, vals=vals):
                full = list(vals)
                for k, v in zip(diff_idx, dvals):
                    full[k] = v
                return tuple(fn(*full))

            _, vjp = jax.vjp(f, *[vals[k] for k in diff_idx])
            cts_h = tuple(c[:, h * w:(h + 1) * w] if ph else c[...] for c, (w, ph, _) in zip(ct_refs, outs))
            for k, g_ref, g in zip(diff_idx, g_refs, vjp(cts_h)):
                d = descs[k]
                if d[0] == "row" and d[2]:
                    g_ref[:, h * d[1]:(h + 1) * d[1]] = g.astype(g_ref.dtype)
                else:
                    shared[k] = g if k not in shared else shared[k] + g
        for k, g_ref in zip(diff_idx, g_refs):
            d = descs[k]
            if k not in shared:
                continue
            g = shared[k]
            if d[0] == "row":
                if k in add_refs:
                    g = g + add_refs[k][...]
                g_ref[...] = g.astype(g_ref.dtype)
            elif d[0] == "par":
                _accumulate(g_ref, g, i == 0)
            else:
                _accumulate(g_ref, g[None], jnp.logical_or(i == 0, i == nct))

    in_specs = _row_specs(descs, arrays, tm, nct, heads)
    g_specs, g_shape = [], []
    for k in diff_idx:
        d = descs[k]
        if d[0] == "row":
            g_specs.append(pl.BlockSpec((tm, d[1] * (heads if d[2] else 1)), lambda i: (i, 0)))
            g_shape.append(jax.ShapeDtypeStruct((T, d[1] * (heads if d[2] else 1)), BF16 if k in to_matmul else F32))
        else:
            g_specs.append(in_specs[k])
            g_shape.append(jax.ShapeDtypeStruct(arrays[k].shape, F32))
    add_specs = [g_specs[diff_idx.index(k)] for k in add_idx]
    return pl.pallas_call(
        body, name=name, grid=(T // tm,), in_specs=in_specs + _out_specs(outs, tm, heads) + add_specs, out_specs=g_specs,
        out_shape=g_shape, compiler_params=_cparams(("arbitrary",)),
    )(*arrays, *cts, *[add[k] for k in add_idx])


def _accumulate(ref, val, first):
    @pl.when(first)
    def _():
        ref[...] = val.astype(ref.dtype)

    @pl.when(jnp.logical_not(first))
    def _():
        ref[...] += val.astype(ref.dtype)


def _rms(x, count=None):
    n = x.shape[-1] if count is None else count
    return x * lax.rsqrt(jnp.sum(x * x, axis=-1, keepdims=True) * (1.0 / n) + EPS)


def _swap_halves(x, nf):
    w = x.shape[-1]
    lane = lax.broadcasted_iota(jnp.int32, x.shape, x.ndim - 1)
    return jnp.where((lane % (2 * nf)) < nf, pltpu.roll(x, w - nf, x.ndim - 1), pltpu.roll(x, nf, x.ndim - 1))


def _make_rope(nf):
    @jax.custom_vjp
    def rope(x, c, s):
        return x * c + _swap_halves(x, nf) * s

    def fwd(x, c, s):
        return rope(x, c, s), (c, s)

    def bwd(res, g):
        c, s = res
        return g * c + _swap_halves(g * s, nf), jnp.zeros_like(c), jnp.zeros_like(s)

    rope.defvjp(fwd, bwd)
    return rope


_rope_swa = _make_rope(SWA_HEAD_DIM // 4)
_rope_mla = _make_rope(MLA_ROPE // 4)


@jax.custom_vjp
def _softplus(x):
    e = jnp.exp(-jnp.abs(x))
    u = 1.0 + e
    log1p_e = jnp.where(u == 1.0, e, jnp.log(u) * e / jnp.where(u == 1.0, 1.0, u - 1.0))
    return jnp.maximum(x, 0.0) + log1p_e


_softplus.defvjp(lambda x: (_softplus(x), x), lambda x, g: (g * jax.nn.sigmoid(x),))


def fn_norm_mod(x, g, shift, scale):
    return (_rms(x) * g * (1.0 + scale) + shift,)


def fn_rms(x, g):
    return (_rms(x) * g,)


def fn_resid(x, a, gate):
    return (x + gate * a,)


def fn_softplus(dt, bias):
    return (_softplus(dt + bias),)


def fn_ssd_out(yf, yb, xs, z, d_lane, g):
    y = yf + yb + d_lane * xs
    return (_rms(y * (z * jax.nn.sigmoid(z))) * g,)


def fn_swa_q(q, g, c, s):
    return (_rope_swa(_rms(q) * g, c, s),)


def fn_swa_kv(k, v, g, c, s):
    return (_rope_swa(_rms(k) * g, c, s), v)


def fn_mla_q(qn, qr, gn, gr, c, s):
    return (jnp.concatenate([_rms(qn) * gn, _rope_mla(_rms(qr, MLA_ROPE) * gr, c, s)], axis=-1),)


def fn_mla_kv(kn, v, kr, gn, gr, c, s):
    return (jnp.concatenate([_rms(kn) * gn, _rope_mla(_rms(kr, MLA_ROPE) * gr, c, s)], axis=-1), v)


def fn_merge(g1, g2, g3, p1, p2, p3):
    return (jax.nn.sigmoid(g1) * p1 + jax.nn.sigmoid(g2) * p2 + jax.nn.sigmoid(g3) * p3,)


def fn_swiglu(g, u):
    return (g * jax.nn.sigmoid(g) * u,)


ATTN_TILE = 256
NT_DIMS = (((1,), (1,)), ((), ()))


class AttnCfg:
    def __init__(self, hq, group, dq, dv, scale, window, has_sink, L, T, chunk, kv_block):
        self.hq, self.group, self.dq, self.dv, self.scale = hq, group, dq, dv, scale
        self.window, self.has_sink, self.L, self.T = window, has_sink, L, T
        self.chunk = _pick(L, (chunk, ATTN_TILE))
        self.ctx_chunk = T - L
        self.kv_block = kv_block
        self.q_block = kv_block * group
        assert L % ATTN_TILE == 0 and (T - L) % ATTN_TILE == 0 and L % self.chunk == 0
        assert (hq // group) % kv_block == 0
        if window is not None:
            assert (ATTN_TILE + 2 * window) % self.chunk == 0
            self.window_chunks = min((ATTN_TILE + 2 * window) // self.chunk, L // self.chunk)
            self.align = math.gcd(self.chunk, window)
        else:
            self.align = self.chunk


LOG2E = math.log2(math.e)


def _latent_chunks(cfg, r0):
    c = cfg.chunk
    if cfg.window is None:
        lo, n = 0, cfg.L // c
    else:
        n = cfg.window_chunks
        lo = jnp.clip(r0 - cfg.window, 0, cfg.L - n * c)
    return lo, n


def _visible(cfg, rows_q, rows_k):
    return jnp.logical_or(rows_k >= cfg.L, jnp.abs(rows_k - rows_q) <= cfg.window)


def flash_fwd(name, cfg, q, k, v, sink):
    T, tq, c = cfg.T, ATTN_TILE, cfg.chunk
    hq, g, dq, dv, hb, kb = cfg.hq, cfg.group, cfg.dq, cfg.dv, cfg.q_block, cfg.kv_block
    to_log2 = cfg.scale * LOG2E

    def body(*refs):
        if cfg.has_sink:
            q_ref, k_ref, v_ref, sink_ref, o_ref, lse_ref = refs
        else:
            q_ref, k_ref, v_ref, o_ref, lse_ref = refs
        q0 = pl.program_id(1) * tq
        qs = [q_ref[:, hh * dq:(hh + 1) * dq] for hh in range(hb)]
        lat_lo, lat_n = _latent_chunks(cfg, q0)
        n = jnp.where(q0 >= cfg.L, 0, lat_n)
        rows_q = q0 + lax.broadcasted_iota(jnp.int32, (tq, 1), 0)

        def start(t):
            return pl.multiple_of(lat_lo + jnp.minimum(t, lat_n - 1) * c, cfg.align)

        def logits(ks, size):
            return tuple(lax.dot_general(qs[hh], k_ref[pl.ds(ks, size), (hh // g) * dq:(hh // g + 1) * dq], NT_DIMS,
                                         preferred_element_type=F32) for hh in range(hb))

        def update(state, s_all, ks, size, masked):
            new_state = []
            for hh in range(hb):
                m, acc = state[hh]
                s = s_all[hh]
                if masked:
                    rows_k = ks + lax.broadcasted_iota(jnp.int32, (1, size), 1)
                    s = jnp.where(_visible(cfg, rows_q, rows_k), s, -jnp.inf)
                m_new = jnp.maximum(m, jnp.max(s, axis=-1, keepdims=True) * to_log2)
                alpha = jnp.exp2(m - m_new)
                p = jnp.exp2(s * to_log2 - m_new).astype(BF16)
                kh = hh // g
                v_ones = jnp.concatenate([v_ref[pl.ds(ks, size), kh * dv:(kh + 1) * dv], jnp.ones((size, dv), BF16)], axis=1)
                acc = alpha * acc + jnp.dot(p, v_ones, preferred_element_type=F32)
                new_state.append((m_new, acc))
            return tuple(new_state)

        def step(t, carry):
            state, s_all = carry
            s_next = logits(start(t + 1), c)
            return update(state, s_all, start(t), c, cfg.window is not None), s_next

        state = []
        for hh in range(hb):
            if cfg.has_sink:
                m0 = jnp.zeros((tq, 1), F32) + sink_ref[hh] * LOG2E
                l0 = jnp.ones((tq, dv), F32)
            else:
                m0 = jnp.full((tq, 1), -jnp.inf, F32)
                l0 = jnp.zeros((tq, dv), F32)
            state.append((m0, jnp.concatenate([jnp.zeros((tq, dv), F32), l0], axis=1)))
        state = update(tuple(state), logits(cfg.L, cfg.ctx_chunk), cfg.L, cfg.ctx_chunk, False)
        state, _ = lax.fori_loop(0, n, step, (state, logits(start(0), c)))
        for hh in range(hb):
            m, acc = state[hh]
            o_ref[:, hh * dv:(hh + 1) * dv] = acc[:, :dv] / acc[:, dv:]
            lse_ref[hh] = m + jnp.log2(acc[:, dv:dv + 1])

    in_specs = [pl.BlockSpec((tq, hb * dq), lambda h, i: (i, h)),
                pl.BlockSpec((T, kb * dq), lambda h, i: (0, h)),
                pl.BlockSpec((T, kb * dv), lambda h, i: (0, h))]
    ins = [q, k, v]
    if cfg.has_sink:
        in_specs.append(pl.BlockSpec((hb, 1, 1), lambda h, i: (h, 0, 0)))
        ins.append(sink)
    return pl.pallas_call(
        body, name=name, grid=(hq // hb, T // tq), in_specs=in_specs,
        out_specs=[pl.BlockSpec((tq, hb * dv), lambda h, i: (i, h)), pl.BlockSpec((hb, tq, 1), lambda h, i: (h, i, 0))],
        out_shape=[jax.ShapeDtypeStruct((T, hq * dv), F32), jax.ShapeDtypeStruct((hq, T, 1), F32)],
        compiler_params=_cparams(("parallel", "parallel")),
    )(*ins)


def attn_delta(name, cfg, o, do, lse, sink):
    T, tm, hq, dv = cfg.T, ATTN_TILE, cfg.hq, cfg.dv

    def body(*refs):
        if cfg.has_sink:
            o_ref, do_ref, lse_ref, sink_ref, delta_ref, dob_ref, dsink_ref = refs
        else:
            o_ref, do_ref, delta_ref, dob_ref = refs
        dob_ref[...] = do_ref[...].astype(BF16)
        parts = []
        for h in range(hq):
            delta = jnp.sum(do_ref[:, h * dv:(h + 1) * dv] * o_ref[:, h * dv:(h + 1) * dv], axis=-1, keepdims=True)
            delta_ref[h] = delta
            if cfg.has_sink:
                parts.append(-jnp.sum(jnp.exp2(sink_ref[h] * LOG2E - lse_ref[h]) * delta, axis=0, keepdims=True)[None])
        if cfg.has_sink:
            _accumulate(dsink_ref, jnp.concatenate(parts, axis=0), pl.program_id(0) == 0)

    head_tile = pl.BlockSpec((tm, hq * dv), lambda i: (i, 0))
    col = pl.BlockSpec((hq, tm, 1), lambda i: (0, i, 0))
    one = pl.BlockSpec((hq, 1, 1), lambda i: (0, 0, 0))
    in_specs, ins = [head_tile, head_tile], [o, do]
    out_specs = [col, head_tile]
    out_shape = [jax.ShapeDtypeStruct((hq, T, 1), F32), jax.ShapeDtypeStruct((T, hq * dv), BF16)]
    if cfg.has_sink:
        in_specs += [col, one]
        ins += [lse, sink]
        out_specs.append(one)
        out_shape.append(jax.ShapeDtypeStruct((hq, 1, 1), F32))
    return pl.pallas_call(body, name=name, grid=(T // tm,), in_specs=in_specs, out_specs=out_specs, out_shape=out_shape,
                          compiler_params=_cparams(("arbitrary",)))(*ins)


def flash_bwd_fused(name, cfg, q, k, v, dob, lse, delta):
    T, L, tq, c, cc = cfg.T, cfg.L, ATTN_TILE, cfg.chunk, cfg.ctx_chunk
    hq, g, dq, dv = cfg.hq, cfg.group, cfg.dq, cfg.dv
    hk = hq // g
    nq = T // tq
    to_log2 = cfg.scale * LOG2E
    masked = cfg.window is not None

    def body(q_ref, k_ref, v_ref, do_ref, lse_ref, delta_ref, dq_ref, dk_ref, dv_ref):
        i = pl.program_id(1)
        q0 = i * tq

        @pl.when(i == 0)
        def _():
            dk_ref[...] = jnp.zeros_like(dk_ref)
            dv_ref[...] = jnp.zeros_like(dv_ref)

        qs = [q_ref[:, hh * dq:(hh + 1) * dq] for hh in range(g)]
        dos = [do_ref[:, hh * dv:(hh + 1) * dv] for hh in range(g)]
        lses = [lse_ref[hh] for hh in range(g)]
        deltas = [delta_ref[hh] for hh in range(g)]
        lat_lo, lat_n = _latent_chunks(cfg, q0)
        n = jnp.where(q0 >= L, 0, lat_n)
        rows_q = q0 + lax.broadcasted_iota(jnp.int32, (tq, 1), 0)

        def start(t):
            return pl.multiple_of(lat_lo + jnp.minimum(t, lat_n - 1) * c, cfg.align)

        def products(ks, size):
            kk, vv = k_ref[pl.ds(ks, size), :], v_ref[pl.ds(ks, size), :]
            return tuple((lax.dot_general(qs[hh], kk, NT_DIMS, preferred_element_type=F32),
                          lax.dot_general(dos[hh], vv, NT_DIMS, preferred_element_type=F32)) for hh in range(g))

        def update(accs, prods, ks, size, mask_it):
            new_accs, dv_part, dk_part = [], None, None
            for hh in range(g):
                s, dp = prods[hh]
                p = jnp.exp2(s * to_log2 - lses[hh])
                if mask_it:
                    rows_k = ks + lax.broadcasted_iota(jnp.int32, (1, size), 1)
                    p = jnp.where(_visible(cfg, rows_q, rows_k), p, 0.0)
                ds = (p * (dp - deltas[hh])).astype(BF16)
                dv_h = lax.dot_general(p.astype(BF16), dos[hh], TN_DIMS, preferred_element_type=F32)
                dk_h = lax.dot_general(ds, qs[hh], TN_DIMS, preferred_element_type=F32)
                dv_part = dv_h if dv_part is None else dv_part + dv_h
                dk_part = dk_h if dk_part is None else dk_part + dk_h
                new_accs.append(accs[hh] + jnp.dot(ds, k_ref[pl.ds(ks, size), :], preferred_element_type=F32))
            dv_ref[pl.ds(ks, size), :] += dv_part
            dk_ref[pl.ds(ks, size), :] += dk_part
            return tuple(new_accs)

        def step(t, carry):
            accs, prods = carry
            nxt = products(start(t + 1), c)
            return update(accs, prods, start(t), c, masked), nxt

        accs = update(tuple(jnp.zeros((tq, dq), F32) for _ in range(g)), products(L, cc), L, cc, False)
        accs, _ = lax.fori_loop(0, n, step, (accs, products(start(0), c)))
        for hh in range(g):
            dq_ref[:, hh * dq:(hh + 1) * dq] = accs[hh] * cfg.scale

        @pl.when(i == nq - 1)
        def _():
            dk_ref[...] = dk_ref[...] * cfg.scale

    col = pl.BlockSpec((g, tq, 1), lambda h, i: (h, i, 0))
    return pl.pallas_call(
        body, name=name, grid=(hk, nq),
        in_specs=[pl.BlockSpec((tq, g * dq), lambda h, i: (i, h)),
                  pl.BlockSpec((T, dq), lambda h, i: (0, h)),
                  pl.BlockSpec((T, dv), lambda h, i: (0, h)),
                  pl.BlockSpec((tq, g * dv), lambda h, i: (i, h)), col, col],
        out_specs=[pl.BlockSpec((tq, g * dq), lambda h, i: (i, h)),
                   pl.BlockSpec((T, dq), lambda h, i: (0, h)),
                   pl.BlockSpec((T, dv), lambda h, i: (0, h))],
        out_shape=[jax.ShapeDtypeStruct((T, hq * dq), F32), jax.ShapeDtypeStruct((T, hk * dq), F32),
                   jax.ShapeDtypeStruct((T, hk * dv), F32)],
        compiler_params=_cparams(("arbitrary", "arbitrary")),
    )(q, k, v, dob, lse, delta)


HALO = 8


def _conv_specs(tm, C, T):
    nb = tm // HALO
    last = T // HALO - 1
    return [pl.BlockSpec((HALO, C), lambda i: (jnp.maximum(i * nb - 1, 0), 0)),
            pl.BlockSpec((tm, C), lambda i: (i, 0)),
            pl.BlockSpec((HALO, C), lambda i: (jnp.minimum((i + 1) * nb, last), 0))]


def _extended(prev_ref, cur_ref, next_ref, i, tm, L, T):
    r0 = i * tm
    keep_prev = jnp.logical_and(r0 != 0, r0 != L).astype(F32)
    keep_next = jnp.logical_and(r0 + tm != L, r0 + tm != T).astype(F32)
    return jnp.concatenate([prev_ref[...] * keep_prev, cur_ref[...], next_ref[...] * keep_next], axis=0)


def _shift_rows(xe, d):
    n = xe.shape[0]
    return xe if d == 0 else pltpu.roll(xe, (-d) % n, 0)


def _conv_pre(xe, w_ref, b_ref):
    acc = b_ref[...] + w_ref[SSM_CONV // 2:SSM_CONV // 2 + 1, :] * xe
    for k in range(SSM_CONV):
        if k != SSM_CONV // 2:
            acc = acc + w_ref[k:k + 1, :] * _shift_rows(xe, k - SSM_CONV // 2)
    return acc


def conv_fwd(name, x, w, b, L, tm=ROW_TILE):
    T, C = x.shape

    def body(xp, xc, xn, w_ref, b_ref, o_ref):
        xe = _extended(xp, xc, xn, pl.program_id(0), tm, L, T)
        pre = _conv_pre(xe, w_ref, b_ref)[HALO:HALO + tm]
        o_ref[...] = pre * jax.nn.sigmoid(pre)

    full = lambda a: pl.BlockSpec(a.shape, lambda i: (0, 0))
    return pl.pallas_call(body, name=name, grid=(T // tm,), in_specs=_conv_specs(tm, C, T) + [full(w), full(b)],
                          out_specs=pl.BlockSpec((tm, C), lambda i: (i, 0)), out_shape=jax.ShapeDtypeStruct((T, C), F32),
                          compiler_params=_cparams(("parallel",)))(x, x, x, w, b)


def conv_bwd(name, x, w, b, gu, L, tm=ROW_TILE):
    T, C = x.shape

    def body(xp, xc, xn, gp, gc, gn, w_ref, b_ref, dx_ref, dw_ref, db_ref):
        i = pl.program_id(0)
        xe = _extended(xp, xc, xn, i, tm, L, T)
        ge = _extended(gp, gc, gn, i, tm, L, T)
        pre = _conv_pre(xe, w_ref, b_ref)
        sg = jax.nn.sigmoid(pre)
        gpre = ge * (sg * (1.0 + pre * (1.0 - sg)))
        half = SSM_CONV // 2
        dx = jnp.zeros((tm, C), F32)
        rows = []
        for k in range(SSM_CONV):
            dx = dx + w_ref[k:k + 1, :] * _shift_rows(gpre, half - k)[HALO:HALO + tm]
            rows.append(jnp.sum(gpre[HALO:HALO + tm] * _shift_rows(xe, k - half)[HALO:HALO + tm], axis=0, keepdims=True))
        dx_ref[...] = dx.astype(dx_ref.dtype)
        rows += [jnp.zeros((1, C), F32)] * (8 - SSM_CONV)
        _accumulate(dw_ref, jnp.concatenate(rows, axis=0), i == 0)
        _accumulate(db_ref, jnp.sum(gpre[HALO:HALO + tm], axis=0, keepdims=True), i == 0)

    full = lambda a: pl.BlockSpec(a.shape, lambda i: (0, 0))
    return pl.pallas_call(
        body, name=name, grid=(T // tm,), in_specs=_conv_specs(tm, C, T) * 2 + [full(w), full(b)],
        out_specs=[pl.BlockSpec((tm, C), lambda i: (i, 0)), pl.BlockSpec((8, C), lambda i: (0, 0)), pl.BlockSpec((1, C), lambda i: (0, 0))],
        out_shape=[jax.ShapeDtypeStruct((T, C), BF16), jax.ShapeDtypeStruct((8, C), F32), jax.ShapeDtypeStruct((1, C), F32)],
        compiler_params=_cparams(("arbitrary",)))(x, x, x, gu, gu, gu, w, b)


SSM_PAIRS = SSM_HEADS // 2
TN_DIMS = (((0,), (0,)), ((), ()))
HIGHEST = lax.Precision.HIGHEST


def _ssd_chunk(direction, xps, bs, cs, dt_col, dt_row, alog_row, alog_col, hps):
    Q = SSM_CHUNK
    da_col = dt_col * (-jnp.exp(alog_row))
    da_row = dt_row * (-jnp.exp(alog_col))
    ii = lax.broadcasted_iota(jnp.int32, (Q, Q), 0)
    jj = lax.broadcasted_iota(jnp.int32, (Q, Q), 1)
    tri = (ii >= jj) if direction == 0 else (ii <= jj)
    trif = tri.astype(F32)
    acs_col = jnp.dot(trif, da_col, precision=HIGHEST, preferred_element_type=F32)
    acs_row = lax.dot_general(da_row, trif, NT_DIMS, precision=HIGHEST, preferred_element_type=F32)
    tot_col = jnp.sum(da_col, axis=0, keepdims=True)
    lane16 = lax.broadcasted_iota(jnp.int32, (1, SSM_HEADS), 1)
    sub16 = lax.broadcasted_iota(jnp.int32, (SSM_HEADS, 1), 0)
    low = lax.broadcasted_iota(jnp.int32, (1, 2 * SSM_HEAD_DIM), 1) < SSM_HEAD_DIM

    def col(v, h):
        return jnp.sum(v * (lane16 == h).astype(F32), axis=1, keepdims=True)

    def row(v, h):
        return jnp.sum(v * (sub16 == h).astype(F32), axis=0, keepdims=True)

    ys, hos = [], []
    pairs_per_group = SSM_PAIRS // SSM_GROUPS
    for g in range(SSM_GROUPS):
        bb, cb16 = bs[g].astype(BF16), cs[g].astype(BF16)
        cb = lax.dot_general(cb16, bb, NT_DIMS, preferred_element_type=F32)
        for pp in range(pairs_per_group):
            p = g * pairs_per_group + pp
            h0, h1 = 2 * p, 2 * p + 1
            ac0, ac1 = col(acs_col, h0), col(acs_col, h1)
            seg0 = jnp.exp(jnp.where(tri, ac0 - row(acs_row, h0), -jnp.inf))
            seg1 = jnp.exp(jnp.where(tri, ac1 - row(acs_row, h1), -jnp.inf))
            dt_l = jnp.where(low, col(dt_col, h0), col(dt_col, h1))
            ac_l = jnp.where(low, ac0, ac1)
            tot_l = jnp.where(low, col(tot_col, h0), col(tot_col, h1))
            xdt = xps[p] * dt_l
            y = (jnp.dot((cb * seg0).astype(BF16), jnp.where(low, xdt, 0.0).astype(BF16), preferred_element_type=F32)
                 + jnp.dot((cb * seg1).astype(BF16), jnp.where(low, 0.0, xdt).astype(BF16), preferred_element_type=F32))
            y = y + jnp.dot(cb16, hps[p].astype(BF16), preferred_element_type=F32) * jnp.exp(ac_l)
            st = lax.dot_general(bb, (xdt * jnp.exp(tot_l - ac_l)).astype(BF16), TN_DIMS, preferred_element_type=F32)
            ys.append(y)
            hos.append(hps[p] * jnp.exp(tot_l) + st)
    return tuple(ys), tuple(hos)


def _ssd_chunk_of(direction, step, ncl, ncc):
    if direction == 0:
        return jnp.where(step < ncc, ncl + step, step - ncc)
    return jnp.where(step < ncc, ncl + ncc - 1 - step, ncl - 1 - (step - ncc))


def _ssd_load(u_ref):
    Q = SSM_CHUNK
    xps = tuple(u_ref[:, LANES * p:LANES * (p + 1)] for p in range(SSM_PAIRS))
    bs = tuple(u_ref[:, SSM_INNER + SSM_STATE * g:SSM_INNER + SSM_STATE * (g + 1)] for g in range(SSM_GROUPS))
    c0 = SSM_INNER + SSM_GROUPS * SSM_STATE
    cs = tuple(u_ref[:, c0 + SSM_STATE * g:c0 + SSM_STATE * (g + 1)] for g in range(SSM_GROUPS))
    return xps, bs, cs


def ssd_fwd(name, direction, u, dt, dt_t, alog_row, alog_col, L):
    T = u.shape[0]
    Q, N = SSM_CHUNK, SSM_STATE
    ncl, ncc = L // Q, (T - L) // Q
    nc = ncl + ncc
    cm = lambda s: _ssd_chunk_of(direction, s, ncl, ncc)

    def body(u_ref, dt_ref, dtt_ref, ar_ref, ac_ref, y_ref, hin_ref, state):
        @pl.when(pl.program_id(0) == 0)
        def _():
            state[...] = jnp.zeros_like(state)

        xps, bs, cs = _ssd_load(u_ref)
        hps = tuple(state[p] for p in range(SSM_PAIRS))
        for p in range(SSM_PAIRS):
            hin_ref[0, p] = hps[p]
        ys, hos = _ssd_chunk(direction, xps, bs, cs, dt_ref[...], dtt_ref[...], ar_ref[...], ac_ref[...], hps)
        for p in range(SSM_PAIRS):
            y_ref[:, LANES * p:LANES * (p + 1)] = ys[p]
            state[p] = hos[p]

    return pl.pallas_call(
        body, name=name, grid=(nc,),
        in_specs=[pl.BlockSpec((Q, SSM_CONV_DIM), lambda s: (cm(s), 0)),
                  pl.BlockSpec((Q, SSM_HEADS), lambda s: (cm(s), 0)),
                  pl.BlockSpec((SSM_HEADS, Q), lambda s: (0, cm(s))),
                  pl.BlockSpec((1, SSM_HEADS), lambda s: (0, 0)),
                  pl.BlockSpec((SSM_HEADS, 1), lambda s: (0, 0))],
        out_specs=[pl.BlockSpec((Q, SSM_INNER), lambda s: (cm(s), 0)),
                   pl.BlockSpec((1, SSM_PAIRS, N, LANES), lambda s: (cm(s), 0, 0, 0))],
        out_shape=[jax.ShapeDtypeStruct((T, SSM_INNER), F32), jax.ShapeDtypeStruct((nc, SSM_PAIRS, N, LANES), F32)],
        scratch_shapes=[pltpu.VMEM((SSM_PAIRS, N, LANES), F32)],
        compiler_params=_cparams(("arbitrary",)),
    )(u, dt, dt_t, alog_row, alog_col)


def ssd_bwd(name, direction, u, dt, dt_t, alog_row, alog_col, hin, dy, L, add_x=None, add_u=None):
    T = u.shape[0]
    Q, N = SSM_CHUNK, SSM_STATE
    ncl, ncc = L // Q, (T - L) // Q
    nc = ncl + ncc
    cm = lambda s: _ssd_chunk_of(direction, nc - 1 - s, ncl, ncc)
    n_add = (add_x is not None) + (add_u is not None)

    def body(*refs):
        u_ref, dt_ref, dtt_ref, ar_ref, ac_ref, hin_ref, dy_ref = refs[:7]
        add_refs = refs[7:7 + n_add]
        du_ref, ddt_ref, ddtt_ref, dar_ref, dac_ref, dstate = refs[7 + n_add:]
        first = pl.program_id(0) == 0

        @pl.when(first)
        def _():
            dstate[...] = jnp.zeros_like(dstate)

        xps, bs, cs = _ssd_load(u_ref)
        hps = tuple(hin_ref[0, p] for p in range(SSM_PAIRS))
        _, vjp = jax.vjp(functools.partial(_ssd_chunk, direction), xps, bs, cs, dt_ref[...], dtt_ref[...], ar_ref[...],
                         ac_ref[...], hps)
        dys = tuple(dy_ref[:, LANES * p:LANES * (p + 1)] for p in range(SSM_PAIRS))
        dhs = tuple(dstate[p] for p in range(SSM_PAIRS))
        gx, gb, gc, gdt, gdtt, gar, gac, ghp = vjp((dys, dhs))
        parts = list(gx) + list(gb) + list(gc)
        du = jnp.concatenate(parts, axis=1)
        k = 0
        if add_x is not None:
            du = du + jnp.concatenate([add_refs[k][...], jnp.zeros((Q, SSM_CONV_DIM - SSM_INNER), F32)], axis=1)
            k += 1
        if add_u is not None:
            du = du + add_refs[k][...]
        du_ref[...] = du
        ddt_ref[...] = gdt
        ddtt_ref[...] = gdtt
        _accumulate(dar_ref, gar, first)
        _accumulate(dac_ref, gac, first)
        for p in range(SSM_PAIRS):
            dstate[p] = ghp[p]

    in_specs = [pl.BlockSpec((Q, SSM_CONV_DIM), lambda s: (cm(s), 0)),
                pl.BlockSpec((Q, SSM_HEADS), lambda s: (cm(s), 0)),
                pl.BlockSpec((SSM_HEADS, Q), lambda s: (0, cm(s))),
                pl.BlockSpec((1, SSM_HEADS), lambda s: (0, 0)),
                pl.BlockSpec((SSM_HEADS, 1), lambda s: (0, 0)),
                pl.BlockSpec((1, SSM_PAIRS, N, LANES), lambda s: (cm(s), 0, 0, 0)),
                pl.BlockSpec((Q, SSM_INNER), lambda s: (cm(s), 0))]
    ins = [u, dt, dt_t, alog_row, alog_col, hin, dy]
    if add_x is not None:
        in_specs.append(pl.BlockSpec((Q, SSM_INNER), lambda s: (cm(s), 0)))
        ins.append(add_x)
    if add_u is not None:
        in_specs.append(pl.BlockSpec((Q, SSM_CONV_DIM), lambda s: (cm(s), 0)))
        ins.append(add_u)
    return pl.pallas_call(
        body, name=name, grid=(nc,), in_specs=in_specs,
        out_specs=[pl.BlockSpec((Q, SSM_CONV_DIM), lambda s: (cm(s), 0)),
                   pl.BlockSpec((Q, SSM_HEADS), lambda s: (cm(s), 0)),
                   pl.BlockSpec((SSM_HEADS, Q), lambda s: (0, cm(s))),
                   pl.BlockSpec((1, SSM_HEADS), lambda s: (0, 0)),
                   pl.BlockSpec((SSM_HEADS, 1), lambda s: (0, 0))],
        out_shape=[jax.ShapeDtypeStruct((T, SSM_CONV_DIM), F32), jax.ShapeDtypeStruct((T, SSM_HEADS), F32),
                   jax.ShapeDtypeStruct((SSM_HEADS, T), F32), jax.ShapeDtypeStruct((1, SSM_HEADS), F32),
                   jax.ShapeDtypeStruct((SSM_HEADS, 1), F32)],
        scratch_shapes=[pltpu.VMEM((SSM_PAIRS, N, LANES), F32)],
        compiler_params=_cparams(("arbitrary",)),
    )(*ins)


PEER_MASKS = (1, 2, 4, 3, 5, 6, 7)
N_PEERS = len(PEER_MASKS)
MESH_IDS = pl.DeviceIdType.MESH


def _my_index():
    return lax.axis_index("x") * 4 + lax.axis_index("y") * 2 + lax.axis_index("c")


def _coords(idx):
    return (idx // 4, (idx // 2) % 2, idx % 2)


def all_gather_hbm(name, arrays):
    n = len(arrays)
    chip_masks = (4, 2, 6)

    def body(*refs):
        ins, outs = refs[:n], refs[n:2 * n]
        send_sems, recv_sems, local_sems = refs[2 * n:]
        me = _my_index()
        sibling = me ^ 1

        def copy(a, k, block, to, src=None):
            return pltpu.make_async_remote_copy(
                src_ref=outs[a].at[block] if src is None else src, dst_ref=outs[a].at[block],
                send_sem=send_sems.at[a * N_PEERS + k], recv_sem=recv_sems.at[a * N_PEERS + k],
                device_id=_coords(to), device_id_type=MESH_IDS)

        started, own = [], []
        for a in range(n):
            local = pltpu.make_async_copy(ins[a], outs[a].at[me], local_sems.at[a])
            local.start()
            own.append(local)
            first = [copy(a, 0, me, sibling, src=ins[a])] + [copy(a, 1 + j, me, me ^ m, src=ins[a]) for j, m in enumerate(chip_masks)]
            for cp in first:
                cp.start()
            started += first
        for a in range(n):
            for j, m in enumerate(chip_masks):
                copy(a, 1 + j, me ^ m, me).wait_recv()
                fwd = copy(a, 4 + j, me ^ m, sibling)
                fwd.start()
                started.append(fwd)
        for a in range(n):
            copy(a, 0, sibling, me).wait_recv()
            for j, m in enumerate(chip_masks):
                copy(a, 4 + j, sibling ^ m, me).wait_recv()
        for cp in started:
            cp.wait_send()
        for cp in own:
            cp.wait()

    any_spec = pl.BlockSpec(memory_space=pl.ANY)
    return pl.pallas_call(
        body, name=name, in_specs=[any_spec] * n, out_specs=[any_spec] * n,
        out_shape=[jax.ShapeDtypeStruct((N_DEV,) + a.shape, a.dtype) for a in arrays],
        scratch_shapes=[pltpu.SemaphoreType.DMA((n * N_PEERS,)), pltpu.SemaphoreType.DMA((n * N_PEERS,)),
                        pltpu.SemaphoreType.DMA((n,))],
    )(*arrays)


N_CHIPS = N_DEV // 2


def exchange_sibling(name, arrays):
    n = len(arrays)

    def body(*refs):
        ins, outs = refs[:n], refs[n:2 * n]
        send_sems, recv_sems = refs[2 * n:]
        me = _my_index()
        other_core = 1 - me % 2
        copies = []
        for a in range(n):
            for chip in range(N_CHIPS):
                cp = pltpu.make_async_remote_copy(src_ref=ins[a].at[chip, other_core], dst_ref=outs[a].at[chip],
                                                  send_sem=send_sems.at[a * N_CHIPS + chip], recv_sem=recv_sems.at[a * N_CHIPS + chip],
                                                  device_id=_coords(me ^ 1), device_id_type=MESH_IDS)
                cp.start()
                copies.append(cp)
        for cp in copies:
            cp.wait()

    any_spec = pl.BlockSpec(memory_space=pl.ANY)
    return pl.pallas_call(
        body, name=name, in_specs=[any_spec] * n, out_specs=[any_spec] * n,
        out_shape=[jax.ShapeDtypeStruct((N_CHIPS,) + a.shape[2:], a.dtype) for a in arrays],
        scratch_shapes=[pltpu.SemaphoreType.DMA((n * N_CHIPS,)), pltpu.SemaphoreType.DMA((n * N_CHIPS,))],
    )(*arrays)


def exchange_chips(name, arrays):
    n = len(arrays)
    n_other = N_CHIPS - 1

    def body(*refs):
        ins, outs = refs[:n], refs[n:2 * n]
        send_sems, recv_sems, local_sems = refs[2 * n:]
        me = _my_index()
        chip, core = me // 2, me % 2
        copies = []
        for a in range(n):
            local = pltpu.make_async_copy(ins[a].at[chip], outs[a].at[chip], local_sems.at[a])
            local.start()
            copies.append(local)
            for k in range(n_other):
                peer_chip = chip ^ (k + 1)
                cp = pltpu.make_async_remote_copy(src_ref=ins[a].at[peer_chip], dst_ref=outs[a].at[chip],
                                                  send_sem=send_sems.at[a * n_other + k], recv_sem=recv_sems.at[a * n_other + k],
                                                  device_id=_coords(peer_chip * 2 + core), device_id_type=MESH_IDS)
                cp.start()
                copies.append(cp)
        for cp in copies:
            cp.wait()

    any_spec = pl.BlockSpec(memory_space=pl.ANY)
    return pl.pallas_call(
        body, name=name, in_specs=[any_spec] * n, out_specs=[any_spec] * n,
        out_shape=[jax.ShapeDtypeStruct(a.shape, a.dtype) for a in arrays],
        scratch_shapes=[pltpu.SemaphoreType.DMA((n * n_other,)), pltpu.SemaphoreType.DMA((n * n_other,)),
                        pltpu.SemaphoreType.DMA((n,))],
    )(*arrays)


def pair_sum(name, both, got):
    P, _, R, C = both.shape
    tr = _row_tile(R, C, 4)

    def body(a_ref, b_ref, o_ref):
        core = lax.axis_index("c")
        o_ref[...] = (a_ref[0, core].astype(F32) + b_ref[0].astype(F32)).astype(BF16)[None]

    return pl.pallas_call(
        body, name=name, grid=(P, R // tr),
        in_specs=[pl.BlockSpec((1, 2, tr, C), lambda p, i: (p, 0, i, 0)), pl.BlockSpec((1, tr, C), lambda p, i: (p, i, 0))],
        out_specs=pl.BlockSpec((1, tr, C), lambda p, i: (p, i, 0)),
        out_shape=jax.ShapeDtypeStruct((P, R, C), BF16), compiler_params=_cparams(("parallel", "parallel")))(both, got)


def all_gather_vmem(name, v):
    def body(v_ref, out_ref, send_sems, recv_sems):
        me = _my_index()
        out_ref[me] = v_ref[...]
        copies = []
        for k, mask in enumerate(PEER_MASKS):
            cp = pltpu.make_async_remote_copy(src_ref=v_ref, dst_ref=out_ref.at[me], send_sem=send_sems.at[k],
                                              recv_sem=recv_sems.at[k], device_id=_coords(me ^ mask), device_id_type=MESH_IDS)
            cp.start()
            copies.append(cp)
        for cp in copies:
            cp.wait()

    vm = pl.BlockSpec(memory_space=pltpu.VMEM)
    return pl.pallas_call(
        body, name=name, in_specs=[vm], out_specs=vm, out_shape=jax.ShapeDtypeStruct((N_DEV,) + v.shape, v.dtype),
        scratch_shapes=[pltpu.SemaphoreType.DMA((N_PEERS,)), pltpu.SemaphoreType.DMA((N_PEERS,))],
    )(v)


ROW_KERNEL_VMEM_BUDGET = 24 * 1024 * 1024


def _row_tile(rows, cols, bufs):
    budget = ROW_KERNEL_VMEM_BUDGET // (bufs * 2 * 4 * max(cols, LANES))
    if rows <= budget:
        return rows
    for t in range(budget - budget % 16, 15, -16):
        if rows % t == 0:
            return t
    return rows


def sum_parts(name, parts):
    P, R, C = parts.shape
    tr = _row_tile(R, C, P + 1)

    def body(p_ref, o_ref):
        acc = p_ref[0].astype(F32)
        for s in range(1, P):
            acc = acc + p_ref[s].astype(F32)
        o_ref[...] = acc

    return pl.pallas_call(body, name=name, grid=(R // tr,), in_specs=[pl.BlockSpec((P, tr, C), lambda i: (0, i, 0))],
                          out_specs=pl.BlockSpec((tr, C), lambda i: (i, 0)), out_shape=jax.ShapeDtypeStruct((R, C), F32),
                          compiler_params=_cparams(("parallel",)))(parts)


def cast_bf16(name, x):
    R, C = x.shape
    tr = _row_tile(R, C, 2)

    def body(x_ref, o_ref):
        o_ref[...] = x_ref[...].astype(BF16)

    spec = pl.BlockSpec((tr, C), lambda i: (i, 0))
    return pl.pallas_call(body, name=name, grid=(R // tr,), in_specs=[spec], out_specs=spec,
                          out_shape=jax.ShapeDtypeStruct((R, C), BF16), compiler_params=_cparams(("parallel",)))(x)


def adamw(name, w, g, m, v):
    R, C = w.shape
    tr = _row_tile(R, C, 7)

    def body(w_ref, g_ref, m_ref, v_ref, d_ref, nm_ref, nv_ref):
        g = g_ref[...]
        nm = ADAM_B1 * m_ref[...] + (1.0 - ADAM_B1) * g
        nv = ADAM_B2 * v_ref[...] + (1.0 - ADAM_B2) * (g * g)
        m_hat = nm / (1.0 - ADAM_B1 ** ADAM_STEP)
        v_hat = nv / (1.0 - ADAM_B2 ** ADAM_STEP)
        d_ref[...] = -ADAM_LR * (m_hat / (jnp.sqrt(v_hat) + ADAM_EPS) + ADAM_WD * w_ref[...])
        nm_ref[...] = nm
        nv_ref[...] = nv

    spec = pl.BlockSpec((tr, C), lambda i: (i, 0))
    return pl.pallas_call(body, name=name, grid=(R // tr,), in_specs=[spec] * 4, out_specs=[spec] * 3,
                          out_shape=[jax.ShapeDtypeStruct((R, C), F32)] * 3, compiler_params=_cparams(("parallel",)))(w, g, m, v)


def loss_and_grad(name, x, target, L, tm=ROW_TILE):
    T, D = x.shape
    nlt = L // tm

    def body(x_ref, t_ref, loss_ref, dx_ref):
        i = pl.program_id(0)
        err = jnp.where(i < nlt, x_ref[...] - t_ref[...], 0.0)
        dx_ref[...] = err * (1.0 / D)
        part = 0.5 * jnp.sum(jnp.sum(err * err, axis=1, keepdims=True), axis=0, keepdims=True) * (1.0 / D)
        _accumulate(loss_ref, part, i == 0)

    return pl.pallas_call(
        body, name=name, grid=(T // tm,),
        in_specs=[pl.BlockSpec((tm, D), lambda i: (i, 0)), pl.BlockSpec((tm, D), lambda i: (jnp.minimum(i, nlt - 1), 0))],
        out_specs=[pl.BlockSpec((1, 1), lambda i: (0, 0)), pl.BlockSpec((tm, D), lambda i: (i, 0))],
        out_shape=[jax.ShapeDtypeStruct((1, 1), F32), jax.ShapeDtypeStruct((T, D), F32)],
        compiler_params=_cparams(("arbitrary",)))(x, target)


def small_fwd(name, fn, arrays, out_shapes):
    def body(*refs):
        res = fn(*[r[...] for r in refs[:len(arrays)]])
        for o_ref, r in zip(refs[len(arrays):], res):
            o_ref[...] = r

    return pl.pallas_call(body, name=name, out_shape=[jax.ShapeDtypeStruct(s, F32) for s in out_shapes])(*arrays)


def small_bwd(name, fn, arrays, cts):
    n = len(arrays)

    def body(*refs):
        _, vjp = jax.vjp(lambda *a: tuple(fn(*a)), *[r[...] for r in refs[:n]])
        grads = vjp(tuple(r[...] for r in refs[n:n + len(cts)]))
        for o_ref, g in zip(refs[n + len(cts):], grads):
            o_ref[...] = g

    return pl.pallas_call(body, name=name, out_shape=[jax.ShapeDtypeStruct(a.shape, F32) for a in arrays])(*arrays, *cts)


def fn_silu(x):
    return (x * jax.nn.sigmoid(x),)


FWD_NAMES = ["x", "c", "ctx", "c_ctx", "w_mod", "b_mod", "norm1_g", "norm2_g", "w_in", "ssm_conv_w", "ssm_conv_b",
             "ssm_dt_bias", "ssm_a_log", "ssm_d", "ssm_norm_g", "swa_q_norm_g", "swa_k_norm_g", "swa_sink", "mla_q_lat_g",
             "mla_kv_lat_g", "w_mla_uq", "w_mla_ukv", "mla_q_norm_g", "mla_k_norm_g", "w_p_ssm", "w_p_swa", "w_p_mla",
             "w_out", "w_ffn_in", "w_ffn_out"]
WEIGHT_NAMES = FWD_NAMES[3:]
GATHERED = ["w_in", "w_mla_uq", "w_mla_ukv", "w_p_ssm", "w_p_swa", "w_p_mla", "w_out", "w_ffn_in", "w_ffn_out"]
COLUMN_SHARDED = ("w_in", "w_mla_uq", "w_mla_ukv", "w_ffn_in")
REPLICATED = ["c_ctx", "b_mod", "norm1_g", "norm2_g", "ssm_conv_b", "ssm_dt_bias", "ssm_a_log", "ssm_d", "ssm_norm_g",
              "swa_q_norm_g", "swa_k_norm_g", "swa_sink", "mla_q_lat_g", "mla_kv_lat_g", "mla_q_norm_g", "mla_k_norm_g"]
IN_SEGS = [("xbc", SSM_CONV_DIM), ("dt", 2 * SSM_HEADS), ("ks", SWA_KV_HEADS * SWA_HEAD_DIM), ("vs", SWA_KV_HEADS * SWA_HEAD_DIM),
           ("ckv", MLA_KV_RANK), ("kr", MLA_ROPE), ("z", SSM_INNER), ("qs", SWA_Q_HEADS * SWA_HEAD_DIM), ("cq", MLA_Q_RANK),
           ("g1", None), ("g2", None), ("g3", None)]


def _pack(vectors, multiple):
    flat = jnp.concatenate([v.reshape(-1) for v in vectors])
    pad = (-flat.shape[0]) % multiple
    return jnp.pad(flat, (0, pad)).reshape(-1, LANES)


def _unpack(packed, shapes):
    flat, out, off = packed.reshape(-1), [], 0
    for s in shapes:
        n = int(np.prod(s))
        out.append(flat[off:off + n].reshape(s))
        off += n
    return out


def _rope_tables(L, T, rot_dim):
    nf = rot_dim // 4
    inv = jnp.power(ROPE_BASE, -jnp.arange(nf, dtype=F32) / nf)
    r, col = jnp.meshgrid(jnp.arange(L // GRID_W, dtype=F32), jnp.arange(GRID_W, dtype=F32), indexing="ij")
    ang = jnp.stack([r.reshape(-1)[:, None] * inv, col.reshape(-1)[:, None] * inv], axis=1)
    cos, sin = jnp.cos(ang), jnp.sin(ang)
    c = jnp.concatenate([cos[:, 0], cos[:, 0], cos[:, 1], cos[:, 1]], axis=1)
    s = jnp.concatenate([-sin[:, 0], sin[:, 0], -sin[:, 1], sin[:, 1]], axis=1)
    c = jnp.pad(c, ((0, T - L), (0, LANES - rot_dim)), constant_values=1.0)
    s = jnp.pad(s, ((0, T - L), (0, LANES - rot_dim)))
    return c, s


def _pad_rows(a, rows):
    return jnp.pad(a, ((0, rows - a.shape[0]), (0, 0)))


def _row(w, per_head=0, off=0, diff=True):
    return ("row", w, per_head, off, diff)


PAR, PAR_ND = ("par", True), ("par", False)
GRP = ("grp", True)


def kernel(*args):
    n_fwd, n_w = len(FWD_NAMES), len(WEIGHT_NAMES)
    inp = dict(zip(FWD_NAMES, args[:n_fwd]))
    loss_target = args[n_fwd]
    mom_m = dict(zip(WEIGHT_NAMES, args[n_fwd + 1:n_fwd + 1 + n_w]))
    mom_v = dict(zip(WEIGHT_NAMES, args[n_fwd + 1 + n_w:]))

    x, ctx = inp["x"][0], inp["ctx"][0]
    L, D = x.shape
    n_ctx = ctx.shape[0]
    T = L + n_ctx
    depth = inp["w_in"].shape[0]
    me = _my_index()
    in_widths = [w if w is not None else D for _, w in IN_SEGS]
    in_offs = np.concatenate([[0], np.cumsum(in_widths)]).tolist()
    ffn_h = inp["w_ffn_out"].shape[1] * N_DEV
    cfg_swa = AttnCfg(SWA_Q_HEADS, SWA_Q_HEADS // SWA_KV_HEADS, SWA_HEAD_DIM, SWA_HEAD_DIM, SWA_HEAD_DIM ** -0.5, SWA_WINDOW,
                      True, L, T, 256, 1)
    cfg_mla = AttnCfg(MLA_HEADS, 1, MLA_QK_PAD, MLA_V, MLA_QK ** -0.5, None, False, L, T, 1024, 2)
    cfg_mla_bwd = AttnCfg(MLA_HEADS, 1, MLA_QK_PAD, MLA_V, MLA_QK ** -0.5, None, False, L, T, 2048, 1)

    def rf(name, fn, descs, arrays, outs, heads=1):
        return rowop_fwd(name, fn, descs, arrays, outs, T, L, heads=heads)

    def rb(name, fn, descs, arrays, outs, cts, heads=1, add=None, to_matmul=()):
        return rowop_bwd(name, fn, descs, arrays, outs, cts, T, L, heads=heads, add=add, to_matmul=to_matmul)

    local = []
    for n in GATHERED:
        w = inp[n]
        local.append((jnp.swapaxes(w, 1, 2) if n in COLUMN_SHARDED else w).astype(BF16))
    gathered = dict(zip(GATHERED, all_gather_hbm("gather_weights", local)))

    def full(n, l):
        g = gathered[n][:, l]
        return g.reshape(g.shape[0] * g.shape[1], g.shape[2])

    def layer_weights(l):
        wt = {}
        w_in_t = full("w_in", l)
        for (sn, _), o, w in zip(IN_SEGS, in_offs, in_widths):
            seg = w_in_t[o:o + w]
            wt[sn] = _pad_rows(seg, LANES) if sn == "kr" else seg
        uq = full("w_mla_uq", l).reshape(MLA_HEADS, MLA_QK, MLA_Q_RANK)
        wt["uqn"] = uq[:, :MLA_NOPE].reshape(MLA_HEADS * MLA_NOPE, MLA_Q_RANK)
        wt["uqr"] = jnp.pad(uq[:, MLA_NOPE:], ((0, 0), (0, LANES - MLA_ROPE), (0, 0))).reshape(MLA_HEADS * LANES, MLA_Q_RANK)
        ukv = full("w_mla_ukv", l).reshape(MLA_HEADS, MLA_NOPE + MLA_V, MLA_KV_RANK)
        wt["uk"] = ukv[:, :MLA_NOPE].reshape(MLA_HEADS * MLA_NOPE, MLA_KV_RANK)
        wt["uv"] = ukv[:, MLA_NOPE:].reshape(MLA_HEADS * MLA_V, MLA_KV_RANK)
        for n in ("w_p_ssm", "w_p_swa", "w_p_mla", "w_out", "w_ffn_out"):
            wt[n] = full(n, l)
        ffn_in_t = full("w_ffn_in", l)
        wt["fg"], wt["fu"] = ffn_in_t[:ffn_h], ffn_in_t[ffn_h:]
        return wt

    def layer_params(l):
        p = {}
        for n in ("norm1_g", "norm2_g", "ssm_conv_b", "ssm_norm_g", "swa_q_norm_g", "swa_k_norm_g", "mla_q_lat_g", "mla_kv_lat_g"):
            p[n] = inp[n][l][None]
        p["dt_bias"] = inp["ssm_dt_bias"][l].reshape(1, 2 * SSM_HEADS)
        p["alog_row"] = [inp["ssm_a_log"][l][d][None] for d in range(2)]
        p["alog_col"] = [inp["ssm_a_log"][l][d][:, None] for d in range(2)]
        p["d_lane"] = jnp.repeat(inp["ssm_d"][l], SSM_HEAD_DIM)[None]
        p["sink"] = inp["swa_sink"][l].reshape(SWA_Q_HEADS, 1, 1)
        for n, key in (("mla_q_norm_g", "gq"), ("mla_k_norm_g", "gk")):
            g = inp[n][l]
            p[key + "n"] = g[:MLA_NOPE][None]
            p[key + "r"] = jnp.pad(g[MLA_NOPE:], (0, LANES - MLA_ROPE))[None]
        return p

    conv_local = _pack([inp["ssm_conv_w"]], 8 * LANES)
    conv_all = all_gather_vmem("gather_conv_w", conv_local)
    cw = inp["ssm_conv_w"].shape
    conv_full = conv_all.reshape(N_DEV, -1)[:, :cw[0] * cw[1] * cw[2]].reshape(N_DEV, cw[0], cw[1], cw[2])
    conv_full = jnp.moveaxis(conv_full, 0, 2).reshape(cw[0], cw[1], N_DEV * cw[2])
    conv_w8 = jnp.pad(conv_full, ((0, 0), (0, 8 - cw[1]), (0, 0)))

    silu_c, silu_cc = small_fwd("silu_c", lambda a, b: fn_silu(a) + fn_silu(b), [inp["c"], inp["c_ctx"][None]], [(1, D), (1, D)])
    silu_all = all_gather_vmem("gather_silu_c", silu_c.reshape(D // LANES, LANES)).reshape(N_DEV, D)
    S_rows = 2 * N_DEV
    S_mat = jnp.concatenate([silu_all, silu_cc, jnp.zeros((S_rows - N_DEV - 1, D), F32)], axis=0)
    mod_cols = inp["w_mod"].shape[2]
    mods_local = []
    for l in range(depth):
        bias = lax.dynamic_slice(inp["b_mod"][l], (me * mod_cols,), (mod_cols,))
        mods_local.append(matmul(S_mat, inp["w_mod"][l], "nn", f"mod{l}", add=jnp.broadcast_to(bias[None], (S_rows, mod_cols))))
    mods_all = all_gather_vmem("gather_mods", jnp.stack(mods_local).reshape(-1, LANES))
    mods_all = jnp.moveaxis(mods_all.reshape(N_DEV, depth, S_rows, mod_cols), 0, 2).reshape(depth, S_rows, N_DEV * mod_cols)
    mods_lat = lax.dynamic_slice(mods_all, (0, me, 0), (depth, 1, N_DEV * mod_cols))[:, 0]
    mods_ctx = mods_all[:, N_DEV]

    def layer_mods(l):
        return [jnp.stack([mods_lat[l, j * D:(j + 1) * D], mods_ctx[l, j * D:(j + 1) * D]])[:, None] for j in range(6)]

    cs_swa = _rope_tables(L, T, SWA_HEAD_DIM)
    cs_mla = _rope_tables(L, T, MLA_ROPE)
    nm_descs = [_row(D), PAR, GRP, GRP]
    resid_descs = [_row(D, diff=False), _row(D), GRP]
    tab = [_row(LANES, diff=False), _row(LANES, diff=False)]
    swaq_descs = [_row(SWA_HEAD_DIM, 1), PAR] + tab
    swakv_descs = [_row(SWA_HEAD_DIM, 1), _row(SWA_HEAD_DIM, 1), PAR] + tab
    mlaq_descs = [_row(LANES, 1), _row(LANES, 1), PAR, PAR] + tab
    mlakv_descs = [_row(LANES, 1), _row(LANES, 1), _row(LANES), PAR, PAR] + tab
    ssdout_descs = [_row(SSM_INNER), _row(SSM_INNER, diff=False), _row(SSM_INNER), _row(SSM_INNER), PAR, PAR]
    merge_descs = [_row(D)] * 6
    swiglu_descs = [_row(ffn_h), _row(ffn_h)]
    seg_names = [sn for sn, _ in IN_SEGS]
    seg_groups = [seg_names[:7], seg_names[7:]]

    def layer_fwd(l, X, wt, p, mods):
        sh1, sc1, gt1, sh2, sc2, gt2 = mods
        r = {"X": X}
        r["h1"] = rf(f"l{l}_norm1", fn_norm_mod, nm_descs, [X, p["norm1_g"], sh1, sc1], [(D, 0, BF16)])[0]
        for gi, group in enumerate(seg_groups):
            dts = [BF16 if sn in ("g1", "g2", "g3") else F32 for sn in group]
            r.update(zip(group, matmul_multi(f"l{l}_in{gi}", r["h1"], [wt[sn] for sn in group], dts)))
        r["u"] = conv_fwd(f"l{l}_conv", r["xbc"], conv_w8[l], p["ssm_conv_b"], L)
        r["dts"] = rf(f"l{l}_softplus", fn_softplus, [_row(2 * SSM_HEADS), PAR], [r["dt"], p["dt_bias"]], [(2 * SSM_HEADS, 0, F32)])[0]
        for d in range(2):
            dt_d = r["dts"][:, d * SSM_HEADS:(d + 1) * SSM_HEADS]
            r[f"dt{d}"], r[f"dtt{d}"] = dt_d, dt_d.T
            r[f"y{d}"], r[f"hin{d}"] = ssd_fwd(f"l{l}_ssd{d}", d, r["u"], dt_d, dt_d.T, p["alog_row"][d], p["alog_col"][d], L)
        r["ys"] = rf(f"l{l}_ssd_out", fn_ssd_out, ssdout_descs, [r["y0"], r["y1"], r["u"], r["z"], p["d_lane"], p["ssm_norm_g"]],
                     [(SSM_INNER, 0, BF16)])[0]
        r["Qs"] = rf(f"l{l}_swa_q", fn_swa_q, swaq_descs, [r["qs"], p["swa_q_norm_g"], *cs_swa], [(SWA_HEAD_DIM, 1, BF16)], SWA_Q_HEADS)[0]
        r["Ks"], r["Vs"] = rf(f"l{l}_swa_kv", fn_swa_kv, swakv_descs, [r["ks"], r["vs"], p["swa_k_norm_g"], *cs_swa],
                              [(SWA_HEAD_DIM, 1, BF16), (SWA_HEAD_DIM, 1, BF16)], SWA_KV_HEADS)
        r["Os"], r["lse_s"] = flash_fwd(f"l{l}_swa_fwd", cfg_swa, r["Qs"], r["Ks"], r["Vs"], p["sink"])
        r["cqn"] = rf(f"l{l}_q_lat", fn_rms, [_row(MLA_Q_RANK), PAR], [r["cq"], p["mla_q_lat_g"]], [(MLA_Q_RANK, 0, BF16)])[0]
        r["qn"], r["qr"] = matmul_multi(f"l{l}_uq", r["cqn"], [wt["uqn"], wt["uqr"]])
        r["Qm"] = rf(f"l{l}_mla_q", fn_mla_q, mlaq_descs, [r["qn"], r["qr"], p["gqn"], p["gqr"], *cs_mla], [(MLA_QK_PAD, 1, BF16)], MLA_HEADS)[0]
        r["ckvn"] = rf(f"l{l}_kv_lat", fn_rms, [_row(MLA_KV_RANK), PAR], [r["ckv"], p["mla_kv_lat_g"]], [(MLA_KV_RANK, 0, BF16)])[0]
        r["kn"], r["vp"] = matmul_multi(f"l{l}_ukv", r["ckvn"], [wt["uk"], wt["uv"]])
        r["Km"], r["Vm"] = rf(f"l{l}_mla_kv", fn_mla_kv, mlakv_descs, [r["kn"], r["vp"], r["kr"], p["gkn"], p["gkr"], *cs_mla],
                              [(MLA_QK_PAD, 1, BF16), (MLA_V, 1, BF16)], MLA_HEADS)
        r["Om"], r["lse_m"] = flash_fwd(f"l{l}_mla_fwd", cfg_mla, r["Qm"], r["Km"], r["Vm"], None)
        r["P1"] = matmul(r["ys"], wt["w_p_ssm"], "nn", f"l{l}_p_ssm", out_dtype=BF16)
        r["P2"] = matmul(r["Os"], wt["w_p_swa"], "nn", f"l{l}_p_swa", out_dtype=BF16)
        r["P3"] = matmul(r["Om"], wt["w_p_mla"], "nn", f"l{l}_p_mla", out_dtype=BF16)
        r["mg"] = rf(f"l{l}_merge", fn_merge, merge_descs, [r["g1"], r["g2"], r["g3"], r["P1"], r["P2"], r["P3"]], [(D, 0, BF16)])[0]
        r["A"] = matmul(r["mg"], wt["w_out"], "nn", f"l{l}_out")
        r["X1"] = rf(f"l{l}_resid1", fn_resid, resid_descs, [X, r["A"], gt1], [(D, 0, F32)])[0]
        r["h2"] = rf(f"l{l}_norm2", fn_norm_mod, nm_descs, [r["X1"], p["norm2_g"], sh2, sc2], [(D, 0, BF16)])[0]
        r["Fg"] = matmul(r["h2"], wt["fg"], "nt", f"l{l}_ffn_g", out_dtype=BF16)
        r["Fu"] = matmul(r["h2"], wt["fu"], "nt", f"l{l}_ffn_u", out_dtype=BF16)
        r["sg"] = rf(f"l{l}_swiglu", fn_swiglu, swiglu_descs, [r["Fg"], r["Fu"]], [(ffn_h, 0, BF16)])[0]
        r["B"] = matmul(r["sg"], wt["w_ffn_out"], "nn", f"l{l}_ffn_out")
        X2 = rf(f"l{l}_resid2", fn_resid, resid_descs, [r["X1"], r["B"], gt2], [(D, 0, F32)])[0]
        return X2, r

    def attn_bwd(tag, cfg, q, k, v, o, lse, do, sink):
        res = attn_delta(f"{tag}_delta", cfg, o, do, lse, sink)
        delta, dob = res[0], res[1]
        dsink = res[2] if cfg.has_sink else None
        dq, dk, dv = flash_bwd_fused(f"{tag}_bwd", cfg, q, k, v, dob, lse, delta)
        return dq, dk, dv, dsink

    def layer_bwd(l, dX2, r, wt, p, mods):
        sh1, sc1, gt1, sh2, sc2, gt2 = mods
        g, gw = {}, {}
        dmod = [None] * 6
        dB, dmod[5] = rb(f"l{l}_resid2_b", fn_resid, resid_descs, [r["X1"], r["B"], gt2], [(D, 0, F32)], [dX2], to_matmul=(1,))
        dsg = matmul(dB, wt["w_ffn_out"], "nt", f"l{l}_ffn_out_da")
        gw["w_ffn_out"] = matmul(r["sg"], dB, "tn", f"l{l}_ffn_out_dw")
        dFg, dFu = rb(f"l{l}_swiglu_b", fn_swiglu, swiglu_descs, [r["Fg"], r["Fu"]], [(ffn_h, 0, BF16)], [dsg], to_matmul=(0, 1))
        dh2 = matmul(dFg, wt["fg"], "nn", f"l{l}_ffn_g_da")
        dh2 = matmul(dFu, wt["fu"], "nn", f"l{l}_ffn_u_da", add=dh2)
        gw["w_ffn_in"] = jnp.concatenate([matmul(r["h2"], dFg, "tn", f"l{l}_ffn_g_dw"), matmul(r["h2"], dFu, "tn", f"l{l}_ffn_u_dw")], axis=1)
        dX1, g["norm2_g"], dmod[3], dmod[4] = rb(f"l{l}_norm2_b", fn_norm_mod, nm_descs, [r["X1"], p["norm2_g"], sh2, sc2],
                                                [(D, 0, BF16)], [dh2], add={0: dX2})
        dA, dmod[2] = rb(f"l{l}_resid1_b", fn_resid, resid_descs, [r["X"], r["A"], gt1], [(D, 0, F32)], [dX1], to_matmul=(1,))
        dmg = matmul(dA, wt["w_out"], "nt", f"l{l}_out_da")
        gw["w_out"] = matmul(r["mg"], dA, "tn", f"l{l}_out_dw")
        dsegs = {}
        dsegs["g1"], dsegs["g2"], dsegs["g3"], dP1, dP2, dP3 = rb(
            f"l{l}_merge_b", fn_merge, merge_descs, [r["g1"], r["g2"], r["g3"], r["P1"], r["P2"], r["P3"]], [(D, 0, BF16)], [dmg],
            to_matmul=tuple(range(6)))
        dys = matmul(dP1, wt["w_p_ssm"], "nt", f"l{l}_p_ssm_da")
        dOs = matmul(dP2, wt["w_p_swa"], "nt", f"l{l}_p_swa_da")
        dOm = matmul(dP3, wt["w_p_mla"], "nt", f"l{l}_p_mla_da")
        gw["w_p_ssm"] = matmul(r["ys"], dP1, "tn", f"l{l}_p_ssm_dw")
        gw["w_p_swa"] = matmul(r["Os"], dP2, "tn", f"l{l}_p_swa_dw")
        gw["w_p_mla"] = matmul(r["Om"], dP3, "tn", f"l{l}_p_mla_dw")
        dQm, dKm, dVm, _ = attn_bwd(f"l{l}_mla", cfg_mla_bwd, r["Qm"], r["Km"], r["Vm"], r["Om"], r["lse_m"], dOm, None)
        dkn, dvp, dsegs["kr"], dgkn, dgkr = rb(f"l{l}_mla_kv_b", fn_mla_kv, mlakv_descs,
                                               [r["kn"], r["vp"], r["kr"], p["gkn"], p["gkr"], *cs_mla],
                                               [(MLA_QK_PAD, 1, BF16), (MLA_V, 1, BF16)], [dKm, dVm], MLA_HEADS, to_matmul=(0, 1, 2))
        dckvn = matmul_sum(f"l{l}_ukv_da", [(dkn, wt["uk"]), (dvp, wt["uv"])])
        dw_uk = matmul(r["ckvn"], dkn, "tn", f"l{l}_uk_dw").reshape(MLA_KV_RANK, MLA_HEADS, MLA_NOPE)
        dw_uv = matmul(r["ckvn"], dvp, "tn", f"l{l}_uv_dw").reshape(MLA_KV_RANK, MLA_HEADS, MLA_V)
        gw["w_mla_ukv"] = jnp.concatenate([dw_uk, dw_uv], axis=2).reshape(MLA_KV_RANK, -1)
        dsegs["ckv"], g["mla_kv_lat_g"] = rb(f"l{l}_kv_lat_b", fn_rms, [_row(MLA_KV_RANK), PAR], [r["ckv"], p["mla_kv_lat_g"]],
                                             [(MLA_KV_RANK, 0, BF16)], [dckvn], to_matmul=(0,))
        dqn, dqr, dgqn, dgqr = rb(f"l{l}_mla_q_b", fn_mla_q, mlaq_descs, [r["qn"], r["qr"], p["gqn"], p["gqr"], *cs_mla],
                                  [(MLA_QK_PAD, 1, BF16)], [dQm], MLA_HEADS, to_matmul=(0, 1))
        dcqn = matmul_sum(f"l{l}_uq_da", [(dqn, wt["uqn"]), (dqr, wt["uqr"])])
        dw_uqn = matmul(r["cqn"], dqn, "tn", f"l{l}_uqn_dw").reshape(MLA_Q_RANK, MLA_HEADS, MLA_NOPE)
        dw_uqr = matmul(r["cqn"], dqr, "tn", f"l{l}_uqr_dw").reshape(MLA_Q_RANK, MLA_HEADS, LANES)[:, :, :MLA_ROPE]
        gw["w_mla_uq"] = jnp.concatenate([dw_uqn, dw_uqr], axis=2).reshape(MLA_Q_RANK, -1)
        dsegs["cq"], g["mla_q_lat_g"] = rb(f"l{l}_q_lat_b", fn_rms, [_row(MLA_Q_RANK), PAR], [r["cq"], p["mla_q_lat_g"]],
                                           [(MLA_Q_RANK, 0, BF16)], [dcqn], to_matmul=(0,))
        g["mla_q_norm_g"] = jnp.concatenate([dgqn[0], dgqr[0, :MLA_ROPE]])
        g["mla_k_norm_g"] = jnp.concatenate([dgkn[0], dgkr[0, :MLA_ROPE]])
        dQs, dKs, dVs, dsink = attn_bwd(f"l{l}_swa", cfg_swa, r["Qs"], r["Ks"], r["Vs"], r["Os"], r["lse_s"], dOs, p["sink"])
        g["swa_sink"] = dsink.reshape(SWA_Q_HEADS)
        dsegs["qs"], g["swa_q_norm_g"] = rb(f"l{l}_swa_q_b", fn_swa_q, swaq_descs, [r["qs"], p["swa_q_norm_g"], *cs_swa],
                                            [(SWA_HEAD_DIM, 1, BF16)], [dQs], SWA_Q_HEADS, to_matmul=(0,))
        dsegs["ks"], dsegs["vs"], g["swa_k_norm_g"] = rb(f"l{l}_swa_kv_b", fn_swa_kv, swakv_descs,
                                                         [r["ks"], r["vs"], p["swa_k_norm_g"], *cs_swa],
                                                         [(SWA_HEAD_DIM, 1, BF16), (SWA_HEAD_DIM, 1, BF16)], [dKs, dVs], SWA_KV_HEADS,
                                                         to_matmul=(0, 1))
        dy, dxs, dsegs["z"], dd_lane, g["ssm_norm_g"] = rb(
            f"l{l}_ssd_out_b", fn_ssd_out, ssdout_descs, [r["y0"], r["y1"], r["u"], r["z"], p["d_lane"], p["ssm_norm_g"]],
            [(SSM_INNER, 0, BF16)], [dys], to_matmul=(3,))
        g["ssm_d"] = dd_lane.reshape(SSM_HEADS, SSM_HEAD_DIM).sum(axis=1)
        du, ddts, dalog = None, [], []
        for d in range(2):
            du, ddt, ddtt, dar, dac = ssd_bwd(f"l{l}_ssd{d}_b", d, r["u"], r[f"dt{d}"], r[f"dtt{d}"], p["alog_row"][d], p["alog_col"][d],
                                              r[f"hin{d}"], dy, L, add_x=dxs if d == 0 else None, add_u=du)
            ddts.append(ddt + ddtt.T)
            dalog.append(dar[0] + dac[:, 0])
        g["ssm_a_log"] = jnp.stack(dalog)
        dsegs["xbc"], dconv_w, g["ssm_conv_b"] = conv_bwd(f"l{l}_conv_b", r["xbc"], conv_w8[l], p["ssm_conv_b"], du, L)
        dsegs["dt"], ddt_bias = rb(f"l{l}_softplus_b", fn_softplus, [_row(2 * SSM_HEADS), PAR], [r["dt"], p["dt_bias"]],
                                   [(2 * SSM_HEADS, 0, F32)], [jnp.concatenate(ddts, axis=1)], to_matmul=(0,))
        g["ssm_dt_bias"] = ddt_bias.reshape(2, SSM_HEADS)
        g["ssm_conv_w"] = dconv_w[:SSM_CONV]
        dh1, dws = None, []
        for gi, group in enumerate(seg_groups):
            dh1 = matmul_sum(f"l{l}_in_da{gi}", [(dsegs[sn], wt[sn]) for sn in group], add=dh1)
        for sn, w in zip(seg_names, in_widths):
            dws.append(matmul(r["h1"], dsegs[sn], "tn", f"l{l}_in_{sn}_dw")[:, :w])
        gw["w_in"] = jnp.concatenate(dws, axis=1)
        dX, g["norm1_g"], dmod[0], dmod[1] = rb(f"l{l}_norm1_b", fn_norm_mod, nm_descs, [r["X"], p["norm1_g"], sh1, sc1],
                                               [(D, 0, BF16)], [dh1], add={0: dX1})
        for n in ("norm1_g", "norm2_g", "ssm_conv_b", "ssm_norm_g", "swa_q_norm_g", "swa_k_norm_g", "mla_q_lat_g", "mla_kv_lat_g"):
            g[n] = g[n][0]
        dmod_lat = jnp.concatenate([dm[0, 0] for dm in dmod])
        dmod_ctx = jnp.concatenate([dm[1, 0] for dm in dmod])
        return dX, g, gw, dmod_lat, dmod_ctx

    X = jnp.concatenate([x, ctx], axis=0)
    saved = []
    for l in range(depth):
        wt, p, mods = layer_weights(l), layer_params(l), layer_mods(l)
        X, r = layer_fwd(l, X, wt, p, mods)
        saved.append((r, wt, p, mods))
    loss_part, dX = loss_and_grad("loss", X, loss_target[0], L)
    loss = lax.psum(loss_part[0, 0], ("x", "y", "c"))
    small_g = [None] * depth
    big_g = [None] * depth
    dmods = [None] * depth
    for l in reversed(range(depth)):
        r, wt, p, mods = saved[l]
        dX, small_g[l], big_g[l], dm_lat, dm_ctx = layer_bwd(l, dX, r, wt, p, mods)
        dmods[l] = jnp.stack([dm_lat, dm_ctx])
    grad_x = dX[:L][None]

    dm_all = all_gather_vmem("gather_dmods", jnp.stack(dmods).reshape(-1, LANES)).reshape(N_DEV, depth, 2, N_DEV * mod_cols)
    dm_rows = jnp.concatenate([jnp.moveaxis(dm_all[:, :, 0], 0, 1), dm_all[:, :, 1].sum(axis=0)[:, None],
                               jnp.zeros((depth, S_rows - N_DEV - 1, N_DEV * mod_cols), F32)], axis=1)
    dm_mine = lax.dynamic_slice(dm_rows, (0, 0, me * mod_cols), (depth, S_rows, mod_cols))
    grads = {}
    grads["w_mod"] = jnp.stack([matmul(S_mat, dm_mine[l], "tn", f"mod{l}_dw") for l in range(depth)])
    d_silu = None
    for l in range(depth):
        d_silu = matmul(dm_mine[l], inp["w_mod"][l], "nt", f"mod{l}_da", add=d_silu)
    small = {n: jnp.stack([small_g[l][n] for l in range(depth)]) for n in small_g[0]}
    small["c_ctx"] = small_bwd("silu_c_b", fn_silu, [inp["c_ctx"][None]], [d_silu[N_DEV:N_DEV + 1]])[0][0]
    small["b_mod"] = jnp.stack(dmods).sum(axis=1)

    rep_shapes = [inp[n].shape for n in REPLICATED]
    conv_shape = (depth, SSM_CONV, SSM_CONV_DIM)
    packed = _pack([small[n] for n in REPLICATED] + [small["ssm_conv_w"]], 8 * LANES)
    small_sum = sum_parts("sum_small", all_gather_vmem("gather_small", packed))
    summed = _unpack(small_sum, rep_shapes + [conv_shape])
    for n, gsum in zip(REPLICATED, summed):
        grads[n] = gsum
    grads["ssm_conv_w"] = lax.dynamic_slice(summed[-1], (0, 0, me * cw[2]), cw)

    slabs = []
    for n in GATHERED:
        gfull = jnp.stack([big_g[l][n] for l in range(depth)])
        if n in COLUMN_SHARDED:
            k_dim, n_dim = gfull.shape[1], gfull.shape[2]
            slab = jnp.moveaxis(gfull.reshape(depth, k_dim, N_DEV, n_dim // N_DEV), 2, 0)
        else:
            k_dim, n_dim = gfull.shape[1], gfull.shape[2]
            slab = jnp.moveaxis(gfull.reshape(depth, N_DEV, k_dim // N_DEV, n_dim), 1, 0)
        slab = cast_bf16(f"cast_{n}", slab.reshape(-1, slab.shape[-1])).reshape(slab.shape)
        slabs.append(slab.reshape((N_CHIPS, 2) + slab.shape[1:]))
    from_sibling = exchange_sibling("exchange_sibling", slabs)
    chip_sums = []
    for n, slab, got in zip(GATHERED, slabs, from_sibling):
        shp = inp[n].shape
        rows = shp[0] * shp[1]
        chip_sums.append(pair_sum(f"pair_{n}", slab.reshape(N_CHIPS, 2, rows, shp[2]), got.reshape(N_CHIPS, rows, shp[2])))
    for n, parts in zip(GATHERED, exchange_chips("exchange_chips", chip_sums)):
        grads[n] = sum_parts(f"sum_{n}", parts).reshape(inp[n].shape)

    delta, new_m, new_v = {}, {}, {}
    rep_pack = lambda d: _pack([d[n] for n in REPLICATED], 8 * LANES)
    rep_out = adamw("adamw_small", rep_pack(inp), rep_pack(grads), rep_pack(mom_m), rep_pack(mom_v))
    for out, res in zip((delta, new_m, new_v), rep_out):
        for n, a in zip(REPLICATED, _unpack(res, rep_shapes)):
            out[n] = a
    for n in ["w_mod", "ssm_conv_w"] + GATHERED:
        shp = inp[n].shape
        two_d = (shp[0] * shp[1], shp[2])
        res = adamw(f"adamw_{n}", inp[n].reshape(two_d), grads[n].reshape(two_d), mom_m[n].reshape(two_d), mom_v[n].reshape(two_d))
        delta[n], new_m[n], new_v[n] = [a.reshape(shp) for a in res]

    return (loss, grad_x, *[grads[n] for n in WEIGHT_NAMES], *[delta[n] for n in WEIGHT_NAMES],
            *[new_m[n] for n in WEIGHT_NAMES], *[new_v[n] for n in WEIGHT_NAMES])
```

```python
import functools
import math

import numpy as np
import jax
import jax.numpy as jnp
from jax import lax
from jax.experimental import pallas as pl
from jax.experimental.pallas import tpu as pltpu

F32 = jnp.float32
BF16 = jnp.bfloat16

N_DEV = 8
V7X_VMEM_BYTES = 64 * 1024 * 1024
VMEM_LIMIT_BYTES = V7X_VMEM_BYTES - 8 * 1024 * 1024
LANES = 128

EPS = 1e-6
ROPE_BASE = 10000.0
GRID_W = 64
SSM_HEADS, SSM_HEAD_DIM, SSM_GROUPS, SSM_STATE, SSM_CONV, SSM_CHUNK = 16, 64, 2, 128, 5, 128
SSM_INNER = SSM_HEADS * SSM_HEAD_DIM
SSM_CONV_DIM = SSM_INNER + 2 * SSM_GROUPS * SSM_STATE
SWA_Q_HEADS, SWA_KV_HEADS, SWA_HEAD_DIM, SWA_WINDOW = 8, 2, 128, 128
MLA_HEADS, MLA_Q_RANK, MLA_KV_RANK, MLA_NOPE, MLA_ROPE, MLA_V = 8, 384, 256, 128, 64, 128
MLA_QK = MLA_NOPE + MLA_ROPE
MLA_QK_PAD = 2 * LANES
ADAM_LR, ADAM_B1, ADAM_B2, ADAM_EPS, ADAM_WD, ADAM_STEP = 0.001, 0.9, 0.999, 1e-08, 0.01, 10

ROW_TILE = 256


def _cparams(sem, **kw):
    return pltpu.CompilerParams(dimension_semantics=sem, vmem_limit_bytes=VMEM_LIMIT_BYTES, **kw)


def _pick(dim, prefs):
    for p in prefs:
        if dim % p == 0:
            return p
    return dim


def matmul(a, b, mode, name, out_dtype=F32, add=None):
    if mode == "nn":
        (M, K), (K2, N) = a.shape, b.shape
    elif mode == "nt":
        (M, K), (N, K2) = a.shape, b.shape
    else:
        (K, M), (K2, N) = a.shape, b.shape
    assert K == K2, (name, a.shape, b.shape)
    has_add = add is not None
    tm, tn, tk = _matmul_tiles(M, N, K, a.dtype.itemsize, b.dtype.itemsize, jnp.dtype(out_dtype).itemsize, has_add,
                                   m_on_lanes=(mode == "tn"))
    nk = K // tk
    dims = {"nn": (((1,), (0,)), ((), ())), "nt": (((1,), (1,)), ((), ())), "tn": (((0,), (0,)), ((), ()))}[mode]
    a_spec = pl.BlockSpec((tk, tm), lambda i, j, k: (k, i)) if mode == "tn" else pl.BlockSpec((tm, tk), lambda i, j, k: (i, k))
    b_spec = pl.BlockSpec((tn, tk), lambda i, j, k: (j, k)) if mode == "nt" else pl.BlockSpec((tk, tn), lambda i, j, k: (k, j))
    o_spec = pl.BlockSpec((tm, tn), lambda i, j, k: (i, j))

    def body(*refs):
        a_ref, b_ref = refs[:2]
        c_ref = refs[2] if has_add else None
        o_ref = refs[3] if has_add else refs[2]
        part = lax.dot_general(a_ref[...].astype(BF16), b_ref[...].astype(BF16), dims, preferred_element_type=F32)
        if nk == 1:
            o_ref[...] = (part + c_ref[...] if has_add else part).astype(o_ref.dtype)
            return
        acc_ref = refs[-1]
        k = pl.program_id(2)

        @pl.when(k == 0)
        def _():
            acc_ref[...] = part + c_ref[...] if has_add else part

        @pl.when(k > 0)
        def _():
            acc_ref[...] += part

        @pl.when(k == nk - 1)
        def _():
            o_ref[...] = acc_ref[...].astype(o_ref.dtype)

    ins = [a, b] + ([add] if has_add else [])
    in_specs = [a_spec, b_spec] + ([o_spec] if has_add else [])
    return pl.pallas_call(
        body, name=name, grid=(M // tm, N // tn, nk), in_specs=in_specs, out_specs=o_spec,
        out_shape=jax.ShapeDtypeStruct((M, N), out_dtype),
        scratch_shapes=[pltpu.VMEM((tm, tn), F32)] if nk > 1 else [],
        input_output_aliases=({2: 0} if has_add else {}),
        compiler_params=_cparams(("parallel", "parallel", "arbitrary")),
    )(*ins)


def matmul_sum(name, pairs, add=None):
    M, N = pairs[0][0].shape[0], pairs[0][1].shape[1]
    n = len(pairs)
    has_add = add is not None
    resident = sum(2 * b.shape[0] * N * b.dtype.itemsize for _, b in pairs)
    tm = next((t for t in (768, 512, 384, 256, 128) if M % t == 0 and resident + sum(
        2 * t * a.shape[1] * a.dtype.itemsize + t * a.shape[1] * 2 for a, _ in pairs) + 6 * t * N * 4 <= MATMUL_VMEM_BUDGET), None)
    assert tm is not None, name

    def body(*refs):
        acc = refs[2 * n][...] if has_add else None
        for s in range(n):
            part = jnp.dot(refs[2 * s][...].astype(BF16), refs[2 * s + 1][...].astype(BF16), preferred_element_type=F32)
            acc = part if acc is None else acc + part
        refs[-1][...] = acc

    o_spec = pl.BlockSpec((tm, N), lambda i: (i, 0))
    in_specs, ins = [], []
    for a, b in pairs:
        in_specs += [pl.BlockSpec((tm, a.shape[1]), lambda i: (i, 0)), pl.BlockSpec(b.shape, lambda i: (0, 0))]
        ins += [a, b]
    if has_add:
        in_specs.append(o_spec)
        ins.append(add)
    return pl.pallas_call(body, name=name, grid=(M // tm,), in_specs=in_specs, out_specs=o_spec,
                          out_shape=jax.ShapeDtypeStruct((M, N), F32), input_output_aliases=({2 * n: 0} if has_add else {}),
                          compiler_params=_cparams(("parallel",)))(*ins)


def matmul_multi(name, a, bs, out_dtypes=None):
    out_dtypes = out_dtypes or [F32] * len(bs)
    M, K = a.shape
    n = len(bs)
    resident = sum(2 * b.shape[0] * K * b.dtype.itemsize for b in bs)
    n_total = sum(b.shape[0] for b in bs)
    tm = next((t for t in (768, 512, 384, 256, 128) if M % t == 0 and
               resident + 2 * t * K * a.dtype.itemsize + 3 * t * n_total * 4 <= MATMUL_VMEM_BUDGET), None)
    assert tm is not None, name

    def body(*refs):
        lhs = refs[0][...].astype(BF16)
        for s in range(n):
            out = lax.dot_general(lhs, refs[1 + s][...].astype(BF16), NT_DIMS, preferred_element_type=F32)
            refs[1 + n + s][...] = out.astype(out_dtypes[s])

    in_specs = [pl.BlockSpec((tm, K), lambda i: (i, 0))] + [pl.BlockSpec(b.shape, lambda i: (0, 0)) for b in bs]
    return pl.pallas_call(
        body, name=name, grid=(M // tm,), in_specs=in_specs,
        out_specs=[pl.BlockSpec((tm, b.shape[0]), lambda i: (i, 0)) for b in bs],
        out_shape=[jax.ShapeDtypeStruct((M, b.shape[0]), dt) for b, dt in zip(bs, out_dtypes)],
        compiler_params=_cparams(("parallel",)))(a, *bs)


MATMUL_VMEM_BUDGET = 36 * 1024 * 1024


def _matmul_tiles(M, N, K, a_bytes, b_bytes, o_bytes, has_add, m_on_lanes=False):
    tk = K if K <= 1536 else _pick(K, (1408, 1024, 768, 704, 512, 256))
    nk = K // tk
    m_cands = [t for t in (1024, 768, 512, 384, 256, 128) if M % t == 0] or [M]
    if M % 768 and M % 1024:
        m_cands += [t for t in (1408, 704, 352) if M % t == 0 and not (m_on_lanes and t % LANES)]
    n_cands = [t for t in range(LANES, min(N, 2816) + 1, LANES) if N % t == 0] or [N]
    best = None
    for tm in m_cands:
        for tn in n_cands:
            pipeline = 2 * (tm * tk * a_bytes + tk * tn * b_bytes + tm * tn * o_bytes) + (2 * tm * tn * 4 if has_add else 0)
            temps = tm * tn * 4 * (2 if nk > 1 else 1) + (tm * tk * 2 if a_bytes == 4 else 0) + (tk * tn * 2 if b_bytes == 4 else 0)
            if pipeline + temps <= MATMUL_VMEM_BUDGET:
                score = (tm * tn, tn)
                if best is None or score > best[0]:
                    best = (score, tm, tn)
    if best is None:
        return m_cands[-1], n_cands[0], tk
    return best[1], best[2], tk


def _row_specs(descs, arrays, tm, nct, heads):
    specs = []
    for d, arr in zip(descs, arrays):
        if d[0] == "row":
            _, w, per_head, off, _ = d
            specs.append(pl.BlockSpec((tm, w * (heads if per_head else 1)), lambda i, off=off: (i, off)))
        elif d[0] == "par":
            specs.append(pl.BlockSpec(arr.shape, lambda i, nd=arr.ndim: (0,) * nd))
        else:
            specs.append(pl.BlockSpec((1,) + arr.shape[1:], lambda i, nd=arr.ndim: (jnp.where(i >= nct, 1, 0),) + (0,) * (nd - 1)))
    return specs


def _load(d, ref, h):
    if d[0] == "grp":
        return ref[0]
    if d[0] == "row" and d[2]:
        return ref[:, h * d[1]:(h + 1) * d[1]].astype(F32)
    return ref[...].astype(F32) if d[0] == "row" else ref[...]


def _out_specs(outs, tm, heads):
    return [pl.BlockSpec((tm, w * (heads if ph else 1)), lambda i: (i, 0)) for (w, ph, _) in outs]


def rowop_fwd(name, fn, descs, arrays, outs, T, n_ctx, heads=1, tm=ROW_TILE):
    nct = n_ctx // tm
    n_in = len(descs)

    def body(*refs):
        for h in range(heads):
            res = fn(*[_load(d, r, h) for d, r in zip(descs, refs[:n_in])])
            for o_ref, r, (w, ph, _) in zip(refs[n_in:], res, outs):
                if ph:
                    o_ref[:, h * w:(h + 1) * w] = r.astype(o_ref.dtype)
                else:
                    o_ref[...] = r.astype(o_ref.dtype)

    out_shape = [jax.ShapeDtypeStruct((T, w * (heads if ph else 1)), dt) for (w, ph, dt) in outs]
    return pl.pallas_call(
        body, name=name, grid=(T // tm,), in_specs=_row_specs(descs, arrays, tm, nct, heads), out_specs=_out_specs(outs, tm, heads),
        out_shape=out_shape, compiler_params=_cparams(("parallel",)),
    )(*arrays)


def rowop_bwd(name, fn, descs, arrays, outs, cts, T, n_ctx, heads=1, tm=ROW_TILE, add=None, to_matmul=()):
    nct = n_ctx // tm
    n_in, n_ct = len(descs), len(cts)
    add = add or {}
    diff_idx = [k for k, d in enumerate(descs) if d[-1]]
    add_idx = [k for k in diff_idx if k in add]

    def body(*refs):
        in_refs, ct_refs = refs[:n_in], refs[n_in:n_in + n_ct]
        add_refs = dict(zip(add_idx, refs[n_in + n_ct:n_in + n_ct + len(add_idx)]))
        g_refs = refs[n_in + n_ct + len(add_idx):]
        i = pl.program_id(0)
        shared = {}
        for h in range(heads):
            vals = [_load(d, r, h) for d, r in zip(descs, in_refs)]

            def f(*dvals, vals=vals):
                full = list(vals)
                for k, v in zip(diff_idx, dvals):
                    full[k] = v
                return tuple(fn(*full))

            _, vjp = jax.vjp(f, *[vals[k] for k in diff_idx])
            cts_h = tuple(c[:, h * w:(h + 1) * w] if ph else c[...] for c, (w, ph, _) in zip(ct_refs, outs))
            for k, g_ref, g in zip(diff_idx, g_refs, vjp(cts_h)):
                d = descs[k]
                if d[0] == "row" and d[2]:
                    g_ref[:, h * d[1]:(h + 1) * d[1]] = g.astype(g_ref.dtype)
                else:
                    shared[k] = g if k not in shared else shared[k] + g
        for k, g_ref in zip(diff_idx, g_refs):
            d = descs[k]
            if k not in shared:
                continue
            g = shared[k]
            if d[0] == "row":
                if k in add_refs:
                    g = g + add_refs[k][...]
                g_ref[...] = g.astype(g_ref.dtype)
            elif d[0] == "par":
                _accumulate(g_ref, g, i == 0)
            else:
                _accumulate(g_ref, g[None], jnp.logical_or(i == 0, i == nct))

    in_specs = _row_specs(descs, arrays, tm, nct, heads)
    g_specs, g_shape = [], []
    for k in diff_idx:
        d = descs[k]
        if d[0] == "row":
            g_specs.append(pl.BlockSpec((tm, d[1] * (heads if d[2] else 1)), lambda i: (i, 0)))
            g_shape.append(jax.ShapeDtypeStruct((T, d[1] * (heads if d[2] else 1)), BF16 if k in to_matmul else F32))
        else:
            g_specs.append(in_specs[k])
            g_shape.append(jax.ShapeDtypeStruct(arrays[k].shape, F32))
    add_specs = [g_specs[diff_idx.index(k)] for k in add_idx]
    return pl.pallas_call(
        body, name=name, grid=(T // tm,), in_specs=in_specs + _out_specs(outs, tm, heads) + add_specs, out_specs=g_specs,
        out_shape=g_shape, compiler_params=_cparams(("arbitrary",)),
    )(*arrays, *cts, *[add[k] for k in add_idx])


def _accumulate(ref, val, first):
    @pl.when(first)
    def _():
        ref[...] = val.astype(ref.dtype)

    @pl.when(jnp.logical_not(first))
    def _():
        ref[...] += val.astype(ref.dtype)


def _rms(x, count=None):
    n = x.shape[-1] if count is None else count
    return x * lax.rsqrt(jnp.sum(x * x, axis=-1, keepdims=True) * (1.0 / n) + EPS)


def _swap_halves(x, nf):
    w = x.shape[-1]
    lane = lax.broadcasted_iota(jnp.int32, x.shape, x.ndim - 1)
    return jnp.where((lane % (2 * nf)) < nf, pltpu.roll(x, w - nf, x.ndim - 1), pltpu.roll(x, nf, x.ndim - 1))


def _make_rope(nf):
    @jax.custom_vjp
    def rope(x, c, s):
        return x * c + _swap_halves(x, nf) * s

    def fwd(x, c, s):
        return rope(x, c, s), (c, s)

    def bwd(res, g):
        c, s = res
        return g * c + _swap_halves(g * s, nf), jnp.zeros_like(c), jnp.zeros_like(s)

    rope.defvjp(fwd, bwd)
    return rope


_rope_swa = _make_rope(SWA_HEAD_DIM // 4)
_rope_mla = _make_rope(MLA_ROPE // 4)


@jax.custom_vjp
def _softplus(x):
    e = jnp.exp(-jnp.abs(x))
    u = 1.0 + e
    log1p_e = jnp.where(u == 1.0, e, jnp.log(u) * e / jnp.where(u == 1.0, 1.0, u - 1.0))
    return jnp.maximum(x, 0.0) + log1p_e


_softplus.defvjp(lambda x: (_softplus(x), x), lambda x, g: (g * jax.nn.sigmoid(x),))


def fn_norm_mod(x, g, shift, scale):
    return (_rms(x) * g * (1.0 + scale) + shift,)


def fn_rms(x, g):
    return (_rms(x) * g,)


def fn_resid(x, a, gate):
    return (x + gate * a,)


def fn_softplus(dt, bias):
    return (_softplus(dt + bias),)


def fn_ssd_out(yf, yb, xs, z, d_lane, g):
    y = yf + yb + d_lane * xs
    return (_rms(y * (z * jax.nn.sigmoid(z))) * g,)


def fn_swa_q(q, g, c, s):
    return (_rope_swa(_rms(q) * g, c, s),)


def fn_swa_kv(k, v, g, c, s):
    return (_rope_swa(_rms(k) * g, c, s), v)


def fn_mla_q(qn, qr, gn, gr, c, s):
    return (jnp.concatenate([_rms(qn) * gn, _rope_mla(_rms(qr, MLA_ROPE) * gr, c, s)], axis=-1),)


def fn_mla_kv(kn, v, kr, gn, gr, c, s):
    return (jnp.concatenate([_rms(kn) * gn, _rope_mla(_rms(kr, MLA_ROPE) * gr, c, s)], axis=-1), v)


def fn_merge(g1, g2, g3, p1, p2, p3):
    return (jax.nn.sigmoid(g1) * p1 + jax.nn.sigmoid(g2) * p2 + jax.nn.sigmoid(g3) * p3,)


def fn_swiglu(g, u):
    return (g * jax.nn.sigmoid(g) * u,)


ATTN_TILE = 256
NT_DIMS = (((1,), (1,)), ((), ()))


class AttnCfg:
    def __init__(self, hq, group, dq, dv, scale, window, has_sink, L, T, chunk, kv_block):
        self.hq, self.group, self.dq, self.dv, self.scale = hq, group, dq, dv, scale
        self.window, self.has_sink, self.L, self.T = window, has_sink, L, T
        self.chunk = _pick(L, (chunk, ATTN_TILE))
        self.ctx_chunk = T - L
        self.kv_block = kv_block
        self.q_block = kv_block * group
        assert L % ATTN_TILE == 0 and (T - L) % ATTN_TILE == 0 and L % self.chunk == 0
        assert (hq // group) % kv_block == 0
        if window is not None:
            assert (ATTN_TILE + 2 * window) % self.chunk == 0
            self.window_chunks = min((ATTN_TILE + 2 * window) // self.chunk, L // self.chunk)
            self.align = math.gcd(self.chunk, window)
        else:
            self.align = self.chunk


LOG2E = math.log2(math.e)


def _latent_chunks(cfg, r0):
    c = cfg.chunk
    if cfg.window is None:
        lo, n = 0, cfg.L // c
    else:
        n = cfg.window_chunks
        lo = jnp.clip(r0 - cfg.window, 0, cfg.L - n * c)
    return lo, n


def _visible(cfg, rows_q, rows_k):
    return jnp.logical_or(rows_k >= cfg.L, jnp.abs(rows_k - rows_q) <= cfg.window)


def flash_fwd(name, cfg, q, k, v, sink):
    T, tq, c = cfg.T, ATTN_TILE, cfg.chunk
    hq, g, dq, dv, hb, kb = cfg.hq, cfg.group, cfg.dq, cfg.dv, cfg.q_block, cfg.kv_block
    to_log2 = cfg.scale * LOG2E

    def body(*refs):
        if cfg.has_sink:
            q_ref, k_ref, v_ref, sink_ref, o_ref, lse_ref = refs
        else:
            q_ref, k_ref, v_ref, o_ref, lse_ref = refs
        q0 = pl.program_id(1) * tq
        qs = [q_ref[:, hh * dq:(hh + 1) * dq] for hh in range(hb)]
        lat_lo, lat_n = _latent_chunks(cfg, q0)
        n = jnp.where(q0 >= cfg.L, 0, lat_n)
        rows_q = q0 + lax.broadcasted_iota(jnp.int32, (tq, 1), 0)

        def start(t):
            return pl.multiple_of(lat_lo + jnp.minimum(t, lat_n - 1) * c, cfg.align)

        def logits(ks, size):
            return tuple(lax.dot_general(qs[hh], k_ref[pl.ds(ks, size), (hh // g) * dq:(hh // g + 1) * dq], NT_DIMS,
                                         preferred_element_type=F32) for hh in range(hb))

        def update(state, s_all, ks, size, masked):
            new_state = []
            for hh in range(hb):
                m, acc = state[hh]
                s = s_all[hh]
                if masked:
                    rows_k = ks + lax.broadcasted_iota(jnp.int32, (1, size), 1)
                    s = jnp.where(_visible(cfg, rows_q, rows_k), s, -jnp.inf)
                m_new = jnp.maximum(m, jnp.max(s, axis=-1, keepdims=True) * to_log2)
                alpha = jnp.exp2(m - m_new)
                p = jnp.exp2(s * to_log2 - m_new).astype(BF16)
                kh = hh // g
                v_ones = jnp.concatenate([v_ref[pl.ds(ks, size), kh * dv:(kh + 1) * dv], jnp.ones((size, dv), BF16)], axis=1)
                acc = alpha * acc + jnp.dot(p, v_ones, preferred_element_type=F32)
                new_state.append((m_new, acc))
            return tuple(new_state)

        def step(t, carry):
            state, s_all = carry
            s_next = logits(start(t + 1), c)
            return update(state, s_all, start(t), c, cfg.window is not None), s_next

        state = []
        for hh in range(hb):
            if cfg.has_sink:
                m0 = jnp.zeros((tq, 1), F32) + sink_ref[hh] * LOG2E
                l0 = jnp.ones((tq, dv), F32)
            else:
                m0 = jnp.full((tq, 1), -jnp.inf, F32)
                l0 = jnp.zeros((tq, dv), F32)
            state.append((m0, jnp.concatenate([jnp.zeros((tq, dv), F32), l0], axis=1)))
        state = update(tuple(state), logits(cfg.L, cfg.ctx_chunk), cfg.L, cfg.ctx_chunk, False)
        state, _ = lax.fori_loop(0, n, step, (state, logits(start(0), c)))
        for hh in range(hb):
            m, acc = state[hh]
            o_ref[:, hh * dv:(hh + 1) * dv] = acc[:, :dv] / acc[:, dv:]
            lse_ref[hh] = m + jnp.log2(acc[:, dv:dv + 1])

    in_specs = [pl.BlockSpec((tq, hb * dq), lambda h, i: (i, h)),
                pl.BlockSpec((T, kb * dq), lambda h, i: (0, h)),
                pl.BlockSpec((T, kb * dv), lambda h, i: (0, h))]
    ins = [q, k, v]
    if cfg.has_sink:
        in_specs.append(pl.BlockSpec((hb, 1, 1), lambda h, i: (h, 0, 0)))
        ins.append(sink)
    return pl.pallas_call(
        body, name=name, grid=(hq // hb, T // tq), in_specs=in_specs,
        out_specs=[pl.BlockSpec((tq, hb * dv), lambda h, i: (i, h)), pl.BlockSpec((hb, tq, 1), lambda h, i: (h, i, 0))],
        out_shape=[jax.ShapeDtypeStruct((T, hq * dv), F32), jax.ShapeDtypeStruct((hq, T, 1), F32)],
        compiler_params=_cparams(("parallel", "parallel")),
    )(*ins)


def attn_delta(name, cfg, o, do, lse, sink):
    T, tm, hq, dv = cfg.T, ATTN_TILE, cfg.hq, cfg.dv

    def body(*refs):
        if cfg.has_sink:
            o_ref, do_ref, lse_ref, sink_ref, delta_ref, dob_ref, dsink_ref = refs
        else:
            o_ref, do_ref, delta_ref, dob_ref = refs
        dob_ref[...] = do_ref[...].astype(BF16)
        parts = []
        for h in range(hq):
            delta = jnp.sum(do_ref[:, h * dv:(h + 1) * dv] * o_ref[:, h * dv:(h + 1) * dv], axis=-1, keepdims=True)
            delta_ref[h] = delta
            if cfg.has_sink:
                parts.append(-jnp.sum(jnp.exp2(sink_ref[h] * LOG2E - lse_ref[h]) * delta, axis=0, keepdims=True)[None])
        if cfg.has_sink:
            _accumulate(dsink_ref, jnp.concatenate(parts, axis=0), pl.program_id(0) == 0)

    head_tile = pl.BlockSpec((tm, hq * dv), lambda i: (i, 0))
    col = pl.BlockSpec((hq, tm, 1), lambda i: (0, i, 0))
    one = pl.BlockSpec((hq, 1, 1), lambda i: (0, 0, 0))
    in_specs, ins = [head_tile, head_tile], [o, do]
    out_specs = [col, head_tile]
    out_shape = [jax.ShapeDtypeStruct((hq, T, 1), F32), jax.ShapeDtypeStruct((T, hq * dv), BF16)]
    if cfg.has_sink:
        in_specs += [col, one]
        ins += [lse, sink]
        out_specs.append(one)
        out_shape.append(jax.ShapeDtypeStruct((hq, 1, 1), F32))
    return pl.pallas_call(body, name=name, grid=(T // tm,), in_specs=in_specs, out_specs=out_specs, out_shape=out_shape,
                          compiler_params=_cparams(("arbitrary",)))(*ins)


def flash_bwd_fused(name, cfg, q, k, v, dob, lse, delta):
    T, L, tq, c, cc = cfg.T, cfg.L, ATTN_TILE, cfg.chunk, cfg.ctx_chunk
    hq, g, dq, dv = cfg.hq, cfg.group, cfg.dq, cfg.dv
    hk = hq // g
    nq = T // tq
    to_log2 = cfg.scale * LOG2E
    masked = cfg.window is not None

    def body(q_ref, k_ref, v_ref, do_ref, lse_ref, delta_ref, dq_ref, dk_ref, dv_ref):
        i = pl.program_id(1)
        q0 = i * tq

        @pl.when(i == 0)
        def _():
            dk_ref[...] = jnp.zeros_like(dk_ref)
            dv_ref[...] = jnp.zeros_like(dv_ref)

        qs = [q_ref[:, hh * dq:(hh + 1) * dq] for hh in range(g)]
        dos = [do_ref[:, hh * dv:(hh + 1) * dv] for hh in range(g)]
        lses = [lse_ref[hh] for hh in range(g)]
        deltas = [delta_ref[hh] for hh in range(g)]
        lat_lo, lat_n = _latent_chunks(cfg, q0)
        n = jnp.where(q0 >= L, 0, lat_n)
        rows_q = q0 + lax.broadcasted_iota(jnp.int32, (tq, 1), 0)

        def start(t):
            return pl.multiple_of(lat_lo + jnp.minimum(t, lat_n - 1) * c, cfg.align)

        def products(ks, size):
            kk, vv = k_ref[pl.ds(ks, size), :], v_ref[pl.ds(ks, size), :]
            return tuple((lax.dot_general(qs[hh], kk, NT_DIMS, preferred_element_type=F32),
                          lax.dot_general(dos[hh], vv, NT_DIMS, preferred_element_type=F32)) for hh in range(g))

        def update(accs, prods, ks, size, mask_it):
            new_accs, dv_part, dk_part = [], None, None
            for hh in range(g):
                s, dp = prods[hh]
                p = jnp.exp2(s * to_log2 - lses[hh])
                if mask_it:
                    rows_k = ks + lax.broadcasted_iota(jnp.int32, (1, size), 1)
                    p = jnp.where(_visible(cfg, rows_q, rows_k), p, 0.0)
                ds = (p * (dp - deltas[hh])).astype(BF16)
                dv_h = lax.dot_general(p.astype(BF16), dos[hh], TN_DIMS, preferred_element_type=F32)
                dk_h = lax.dot_general(ds, qs[hh], TN_DIMS, preferred_element_type=F32)
                dv_part = dv_h if dv_part is None else dv_part + dv_h
                dk_part = dk_h if dk_part is None else dk_part + dk_h
                new_accs.append(accs[hh] + jnp.dot(ds, k_ref[pl.ds(ks, size), :], preferred_element_type=F32))
            dv_ref[pl.ds(ks, size), :] += dv_part
            dk_ref[pl.ds(ks, size), :] += dk_part
            return tuple(new_accs)

        def step(t, carry):
            accs, prods = carry
            nxt = products(start(t + 1), c)
            return update(accs, prods, start(t), c, masked), nxt

        accs = update(tuple(jnp.zeros((tq, dq), F32) for _ in range(g)), products(L, cc), L, cc, False)
        accs, _ = lax.fori_loop(0, n, step, (accs, products(start(0), c)))
        for hh in range(g):
            dq_ref[:, hh * dq:(hh + 1) * dq] = accs[hh] * cfg.scale

        @pl.when(i == nq - 1)
        def _():
            dk_ref[...] = dk_ref[...] * cfg.scale

    col = pl.BlockSpec((g, tq, 1), lambda h, i: (h, i, 0))
    return pl.pallas_call(
        body, name=name, grid=(hk, nq),
        in_specs=[pl.BlockSpec((tq, g * dq), lambda h, i: (i, h)),
                  pl.BlockSpec((T, dq), lambda h, i: (0, h)),
                  pl.BlockSpec((T, dv), lambda h, i: (0, h)),
                  pl.BlockSpec((tq, g * dv), lambda h, i: (i, h)), col, col],
        out_specs=[pl.BlockSpec((tq, g * dq), lambda h, i: (i, h)),
                   pl.BlockSpec((T, dq), lambda h, i: (0, h)),
                   pl.BlockSpec((T, dv), lambda h, i: (0, h))],
        out_shape=[jax.ShapeDtypeStruct((T, hq * dq), F32), jax.ShapeDtypeStruct((T, hk * dq), F32),
                   jax.ShapeDtypeStruct((T, hk * dv), F32)],
        compiler_params=_cparams(("arbitrary", "arbitrary")),
    )(q, k, v, dob, lse, delta)


HALO = 8


def _conv_specs(tm, C, T):
    nb = tm // HALO
    last = T // HALO - 1
    return [pl.BlockSpec((HALO, C), lambda i: (jnp.maximum(i * nb - 1, 0), 0)),
            pl.BlockSpec((tm, C), lambda i: (i, 0)),
            pl.BlockSpec((HALO, C), lambda i: (jnp.minimum((i + 1) * nb, last), 0))]


def _extended(prev_ref, cur_ref, next_ref, i, tm, L, T):
    r0 = i * tm
    keep_prev = jnp.logical_and(r0 != 0, r0 != L).astype(F32)
    keep_next = jnp.logical_and(r0 + tm != L, r0 + tm != T).astype(F32)
    return jnp.concatenate([prev_ref[...] * keep_prev, cur_ref[...], next_ref[...] * keep_next], axis=0)


def _shift_rows(xe, d):
    n = xe.shape[0]
    return xe if d == 0 else pltpu.roll(xe, (-d) % n, 0)


def _conv_pre(xe, w_ref, b_ref):
    acc = b_ref[...] + w_ref[SSM_CONV // 2:SSM_CONV // 2 + 1, :] * xe
    for k in range(SSM_CONV):
        if k != SSM_CONV // 2:
            acc = acc + w_ref[k:k + 1, :] * _shift_rows(xe, k - SSM_CONV // 2)
    return acc


def conv_fwd(name, x, w, b, L, tm=ROW_TILE):
    T, C = x.shape

    def body(xp, xc, xn, w_ref, b_ref, o_ref):
        xe = _extended(xp, xc, xn, pl.program_id(0), tm, L, T)
        pre = _conv_pre(xe, w_ref, b_ref)[HALO:HALO + tm]
        o_ref[...] = pre * jax.nn.sigmoid(pre)

    full = lambda a: pl.BlockSpec(a.shape, lambda i: (0, 0))
    return pl.pallas_call(body, name=name, grid=(T // tm,), in_specs=_conv_specs(tm, C, T) + [full(w), full(b)],
                          out_specs=pl.BlockSpec((tm, C), lambda i: (i, 0)), out_shape=jax.ShapeDtypeStruct((T, C), F32),
                          compiler_params=_cparams(("parallel",)))(x, x, x, w, b)


def conv_bwd(name, x, w, b, gu, L, tm=ROW_TILE):
    T, C = x.shape

    def body(xp, xc, xn, gp, gc, gn, w_ref, b_ref, dx_ref, dw_ref, db_ref):
        i = pl.program_id(0)
        xe = _extended(xp, xc, xn, i, tm, L, T)
        ge = _extended(gp, gc, gn, i, tm, L, T)
        pre = _conv_pre(xe, w_ref, b_ref)
        sg = jax.nn.sigmoid(pre)
        gpre = ge * (sg * (1.0 + pre * (1.0 - sg)))
        half = SSM_CONV // 2
        dx = jnp.zeros((tm, C), F32)
        rows = []
        for k in range(SSM_CONV):
            dx = dx + w_ref[k:k + 1, :] * _shift_rows(gpre, half - k)[HALO:HALO + tm]
            rows.append(jnp.sum(gpre[HALO:HALO + tm] * _shift_rows(xe, k - half)[HALO:HALO + tm], axis=0, keepdims=True))
        dx_ref[...] = dx.astype(dx_ref.dtype)
        rows += [jnp.zeros((1, C), F32)] * (8 - SSM_CONV)
        _accumulate(dw_ref, jnp.concatenate(rows, axis=0), i == 0)
        _accumulate(db_ref, jnp.sum(gpre[HALO:HALO + tm], axis=0, keepdims=True), i == 0)

    full = lambda a: pl.BlockSpec(a.shape, lambda i: (0, 0))
    return pl.pallas_call(
        body, name=name, grid=(T // tm,), in_specs=_conv_specs(tm, C, T) * 2 + [full(w), full(b)],
        out_specs=[pl.BlockSpec((tm, C), lambda i: (i, 0)), pl.BlockSpec((8, C), lambda i: (0, 0)), pl.BlockSpec((1, C), lambda i: (0, 0))],
        out_shape=[jax.ShapeDtypeStruct((T, C), BF16), jax.ShapeDtypeStruct((8, C), F32), jax.ShapeDtypeStruct((1, C), F32)],
        compiler_params=_cparams(("arbitrary",)))(x, x, x, gu, gu, gu, w, b)


SSM_PAIRS = SSM_HEADS // 2
TN_DIMS = (((0,), (0,)), ((), ()))
HIGHEST = lax.Precision.HIGHEST


def _ssd_chunk(direction, xps, bs, cs, dt_col, dt_row, alog_row, alog_col, hps):
    Q = SSM_CHUNK
    da_col = dt_col * (-jnp.exp(alog_row))
    da_row = dt_row * (-jnp.exp(alog_col))
    ii = lax.broadcasted_iota(jnp.int32, (Q, Q), 0)
    jj = lax.broadcasted_iota(jnp.int32, (Q, Q), 1)
    tri = (ii >= jj) if direction == 0 else (ii <= jj)
    trif = tri.astype(F32)
    acs_col = jnp.dot(trif, da_col, precision=HIGHEST, preferred_element_type=F32)
    acs_row = lax.dot_general(da_row, trif, NT_DIMS, precision=HIGHEST, preferred_element_type=F32)
    tot_col = jnp.sum(da_col, axis=0, keepdims=True)
    lane16 = lax.broadcasted_iota(jnp.int32, (1, SSM_HEADS), 1)
    sub16 = lax.broadcasted_iota(jnp.int32, (SSM_HEADS, 1), 0)
    low = lax.broadcasted_iota(jnp.int32, (1, 2 * SSM_HEAD_DIM), 1) < SSM_HEAD_DIM

    def col(v, h):
        return jnp.sum(v * (lane16 == h).astype(F32), axis=1, keepdims=True)

    def row(v, h):
        return jnp.sum(v * (sub16 == h).astype(F32), axis=0, keepdims=True)

    ys, hos = [], []
    pairs_per_group = SSM_PAIRS // SSM_GROUPS
    for g in range(SSM_GROUPS):
        bb, cb16 = bs[g].astype(BF16), cs[g].astype(BF16)
        cb = lax.dot_general(cb16, bb, NT_DIMS, preferred_element_type=F32)
        for pp in range(pairs_per_group):
            p = g * pairs_per_group + pp
            h0, h1 = 2 * p, 2 * p + 1
            ac0, ac1 = col(acs_col, h0), col(acs_col, h1)
            seg0 = jnp.exp(jnp.where(tri, ac0 - row(acs_row, h0), -jnp.inf))
            seg1 = jnp.exp(jnp.where(tri, ac1 - row(acs_row, h1), -jnp.inf))
            dt_l = jnp.where(low, col(dt_col, h0), col(dt_col, h1))
            ac_l = jnp.where(low, ac0, ac1)
            tot_l = jnp.where(low, col(tot_col, h0), col(tot_col, h1))
            xdt = xps[p] * dt_l
            y = (jnp.dot((cb * seg0).astype(BF16), jnp.where(low, xdt, 0.0).astype(BF16), preferred_element_type=F32)
                 + jnp.dot((cb * seg1).astype(BF16), jnp.where(low, 0.0, xdt).astype(BF16), preferred_element_type=F32))
            y = y + jnp.dot(cb16, hps[p].astype(BF16), preferred_element_type=F32) * jnp.exp(ac_l)
            st = lax.dot_general(bb, (xdt * jnp.exp(tot_l - ac_l)).astype(BF16), TN_DIMS, preferred_element_type=F32)
            ys.append(y)
            hos.append(hps[p] * jnp.exp(tot_l) + st)
    return tuple(ys), tuple(hos)


def _ssd_chunk_of(direction, step, ncl, ncc):
    if direction == 0:
        return jnp.where(step < ncc, ncl + step, step - ncc)
    return jnp.where(step < ncc, ncl + ncc - 1 - step, ncl - 1 - (step - ncc))


def _ssd_load(u_ref):
    Q = SSM_CHUNK
    xps = tuple(u_ref[:, LANES * p:LANES * (p + 1)] for p in range(SSM_PAIRS))
    bs = tuple(u_ref[:, SSM_INNER + SSM_STATE * g:SSM_INNER + SSM_STATE * (g + 1)] for g in range(SSM_GROUPS))
    c0 = SSM_INNER + SSM_GROUPS * SSM_STATE
    cs = tuple(u_ref[:, c0 + SSM_STATE * g:c0 + SSM_STATE * (g + 1)] for g in range(SSM_GROUPS))
    return xps, bs, cs


def ssd_fwd(name, direction, u, dt, dt_t, alog_row, alog_col, L):
    T = u.shape[0]
    Q, N = SSM_CHUNK, SSM_STATE
    ncl, ncc = L // Q, (T - L) // Q
    nc = ncl + ncc
    cm = lambda s: _ssd_chunk_of(direction, s, ncl, ncc)

    def body(u_ref, dt_ref, dtt_ref, ar_ref, ac_ref, y_ref, hin_ref, state):
        @pl.when(pl.program_id(0) == 0)
        def _():
            state[...] = jnp.zeros_like(state)

        xps, bs, cs = _ssd_load(u_ref)
        hps = tuple(state[p] for p in range(SSM_PAIRS))
        for p in range(SSM_PAIRS):
            hin_ref[0, p] = hps[p]
        ys, hos = _ssd_chunk(direction, xps, bs, cs, dt_ref[...], dtt_ref[...], ar_ref[...], ac_ref[...], hps)
        for p in range(SSM_PAIRS):
            y_ref[:, LANES * p:LANES * (p + 1)] = ys[p]
            state[p] = hos[p]

    return pl.pallas_call(
        body, name=name, grid=(nc,),
        in_specs=[pl.BlockSpec((Q, SSM_CONV_DIM), lambda s: (cm(s), 0)),
                  pl.BlockSpec((Q, SSM_HEADS), lambda s: (cm(s), 0)),
                  pl.BlockSpec((SSM_HEADS, Q), lambda s: (0, cm(s))),
                  pl.BlockSpec((1, SSM_HEADS), lambda s: (0, 0)),
                  pl.BlockSpec((SSM_HEADS, 1), lambda s: (0, 0))],
        out_specs=[pl.BlockSpec((Q, SSM_INNER), lambda s: (cm(s), 0)),
                   pl.BlockSpec((1, SSM_PAIRS, N, LANES), lambda s: (cm(s), 0, 0, 0))],
        out_shape=[jax.ShapeDtypeStruct((T, SSM_INNER), F32), jax.ShapeDtypeStruct((nc, SSM_PAIRS, N, LANES), F32)],
        scratch_shapes=[pltpu.VMEM((SSM_PAIRS, N, LANES), F32)],
        compiler_params=_cparams(("arbitrary",)),
    )(u, dt, dt_t, alog_row, alog_col)


def ssd_bwd(name, direction, u, dt, dt_t, alog_row, alog_col, hin, dy, L, add_x=None, add_u=None):
    T = u.shape[0]
    Q, N = SSM_CHUNK, SSM_STATE
    ncl, ncc = L // Q, (T - L) // Q
    nc = ncl + ncc
    cm = lambda s: _ssd_chunk_of(direction, nc - 1 - s, ncl, ncc)
    n_add = (add_x is not None) + (add_u is not None)

    def body(*refs):
        u_ref, dt_ref, dtt_ref, ar_ref, ac_ref, hin_ref, dy_ref = refs[:7]
        add_refs = refs[7:7 + n_add]
        du_ref, ddt_ref, ddtt_ref, dar_ref, dac_ref, dstate = refs[7 + n_add:]
        first = pl.program_id(0) == 0

        @pl.when(first)
        def _():
            dstate[...] = jnp.zeros_like(dstate)

        xps, bs, cs = _ssd_load(u_ref)
        hps = tuple(hin_ref[0, p] for p in range(SSM_PAIRS))
        _, vjp = jax.vjp(functools.partial(_ssd_chunk, direction), xps, bs, cs, dt_ref[...], dtt_ref[...], ar_ref[...],
                         ac_ref[...], hps)
        dys = tuple(dy_ref[:, LANES * p:LANES * (p + 1)] for p in range(SSM_PAIRS))
        dhs = tuple(dstate[p] for p in range(SSM_PAIRS))
        gx, gb, gc, gdt, gdtt, gar, gac, ghp = vjp((dys, dhs))
        parts = list(gx) + list(gb) + list(gc)
        du = jnp.concatenate(parts, axis=1)
        k = 0
        if add_x is not None:
            du = du + jnp.concatenate([add_refs[k][...], jnp.zeros((Q, SSM_CONV_DIM - SSM_INNER), F32)], axis=1)
            k += 1
        if add_u is not None:
            du = du + add_refs[k][...]
        du_ref[...] = du
        ddt_ref[...] = gdt
        ddtt_ref[...] = gdtt
        _accumulate(dar_ref, gar, first)
        _accumulate(dac_ref, gac, first)
        for p in range(SSM_PAIRS):
            dstate[p] = ghp[p]

    in_specs = [pl.BlockSpec((Q, SSM_CONV_DIM), lambda s: (cm(s), 0)),
                pl.BlockSpec((Q, SSM_HEADS), lambda s: (cm(s), 0)),
                pl.BlockSpec((SSM_HEADS, Q), lambda s: (0, cm(s))),
                pl.BlockSpec((1, SSM_HEADS), lambda s: (0, 0)),
                pl.BlockSpec((SSM_HEADS, 1), lambda s: (0, 0)),
                pl.BlockSpec((1, SSM_PAIRS, N, LANES), lambda s: (cm(s), 0, 0, 0)),
                pl.BlockSpec((Q, SSM_INNER), lambda s: (cm(s), 0))]
    ins = [u, dt, dt_t, alog_row, alog_col, hin, dy]
    if add_x is not None:
        in_specs.append(pl.BlockSpec((Q, SSM_INNER), lambda s: (cm(s), 0)))
        ins.append(add_x)
    if add_u is not None:
        in_specs.append(pl.BlockSpec((Q, SSM_CONV_DIM), lambda s: (cm(s), 0)))
        ins.append(add_u)
    return pl.pallas_call(
        body, name=name, grid=(nc,), in_specs=in_specs,
        out_specs=[pl.BlockSpec((Q, SSM_CONV_DIM), lambda s: (cm(s), 0)),
                   pl.BlockSpec((Q, SSM_HEADS), lambda s: (cm(s), 0)),
                   pl.BlockSpec((SSM_HEADS, Q), lambda s: (0, cm(s))),
                   pl.BlockSpec((1, SSM_HEADS), lambda s: (0, 0)),
                   pl.BlockSpec((SSM_HEADS, 1), lambda s: (0, 0))],
        out_shape=[jax.ShapeDtypeStruct((T, SSM_CONV_DIM), F32), jax.ShapeDtypeStruct((T, SSM_HEADS), F32),
                   jax.ShapeDtypeStruct((SSM_HEADS, T), F32), jax.ShapeDtypeStruct((1, SSM_HEADS), F32),
                   jax.ShapeDtypeStruct((SSM_HEADS, 1), F32)],
        scratch_shapes=[pltpu.VMEM((SSM_PAIRS, N, LANES), F32)],
        compiler_params=_cparams(("arbitrary",)),
    )(*ins)


PEER_MASKS = (1, 2, 4, 3, 5, 6, 7)
N_PEERS = len(PEER_MASKS)
MESH_IDS = pl.DeviceIdType.MESH


def _my_index():
    return lax.axis_index("x") * 4 + lax.axis_index("y") * 2 + lax.axis_index("c")


def _coords(idx):
    return (idx // 4, (idx // 2) % 2, idx % 2)


def all_gather_hbm(name, arrays):
    n = len(arrays)
    chip_masks = (4, 2, 6)

    def body(*refs):
        ins, outs = refs[:n], refs[n:2 * n]
        send_sems, recv_sems, local_sems = refs[2 * n:]
        me = _my_index()
        sibling = me ^ 1

        def copy(a, k, block, to, src=None):
            return pltpu.make_async_remote_copy(
                src_ref=outs[a].at[block] if src is None else src, dst_ref=outs[a].at[block],
                send_sem=send_sems.at[a * N_PEERS + k], recv_sem=recv_sems.at[a * N_PEERS + k],
                device_id=_coords(to), device_id_type=MESH_IDS)

        started, own = [], []
        for a in range(n):
            local = pltpu.make_async_copy(ins[a], outs[a].at[me], local_sems.at[a])
            local.start()
            own.append(local)
            first = [copy(a, 0, me, sibling, src=ins[a])] + [copy(a, 1 + j, me, me ^ m, src=ins[a]) for j, m in enumerate(chip_masks)]
            for cp in first:
                cp.start()
            started += first
        for a in range(n):
            for j, m in enumerate(chip_masks):
                copy(a, 1 + j, me ^ m, me).wait_recv()
                fwd = copy(a, 4 + j, me ^ m, sibling)
                fwd.start()
                started.append(fwd)
        for a in range(n):
            copy(a, 0, sibling, me).wait_recv()
            for j, m in enumerate(chip_masks):
                copy(a, 4 + j, sibling ^ m, me).wait_recv()
        for cp in started:
            cp.wait_send()
        for cp in own:
            cp.wait()

    any_spec = pl.BlockSpec(memory_space=pl.ANY)
    return pl.pallas_call(
        body, name=name, in_specs=[any_spec] * n, out_specs=[any_spec] * n,
        out_shape=[jax.ShapeDtypeStruct((N_DEV,) + a.shape, a.dtype) for a in arrays],
        scratch_shapes=[pltpu.SemaphoreType.DMA((n * N_PEERS,)), pltpu.SemaphoreType.DMA((n * N_PEERS,)),
                        pltpu.SemaphoreType.DMA((n,))],
    )(*arrays)


N_CHIPS = N_DEV // 2


def exchange_sibling(name, arrays):
    n = len(arrays)

    def body(*refs):
        ins, outs = refs[:n], refs[n:2 * n]
        send_sems, recv_sems = refs[2 * n:]
        me = _my_index()
        other_core = 1 - me % 2
        copies = []
        for a in range(n):
            for chip in range(N_CHIPS):
                cp = pltpu.make_async_remote_copy(src_ref=ins[a].at[chip, other_core], dst_ref=outs[a].at[chip],
                                                  send_sem=send_sems.at[a * N_CHIPS + chip], recv_sem=recv_sems.at[a * N_CHIPS + chip],
                                                  device_id=_coords(me ^ 1), device_id_type=MESH_IDS)
                cp.start()
                copies.append(cp)
        for cp in copies:
            cp.wait()

    any_spec = pl.BlockSpec(memory_space=pl.ANY)
    return pl.pallas_call(
        body, name=name, in_specs=[any_spec] * n, out_specs=[any_spec] * n,
        out_shape=[jax.ShapeDtypeStruct((N_CHIPS,) + a.shape[2:], a.dtype) for a in arrays],
        scratch_shapes=[pltpu.SemaphoreType.DMA((n * N_CHIPS,)), pltpu.SemaphoreType.DMA((n * N_CHIPS,))],
    )(*arrays)


def exchange_chips(name, arrays):
    n = len(arrays)
    n_other = N_CHIPS - 1

    def body(*refs):
        ins, outs = refs[:n], refs[n:2 * n]
        send_sems, recv_sems, local_sems = refs[2 * n:]
        me = _my_index()
        chip, core = me // 2, me % 2
        copies = []
        for a in range(n):
            local = pltpu.make_async_copy(ins[a].at[chip], outs[a].at[chip], local_sems.at[a])
            local.start()
            copies.append(local)
            for k in range(n_other):
                peer_chip = chip ^ (k + 1)
                cp = pltpu.make_async_remote_copy(src_ref=ins[a].at[peer_chip], dst_ref=outs[a].at[chip],
                                                  send_sem=send_sems.at[a * n_other + k], recv_sem=recv_sems.at[a * n_other + k],
                                                  device_id=_coords(peer_chip * 2 + core), device_id_type=MESH_IDS)
                cp.start()
                copies.append(cp)
        for cp in copies:
            cp.wait()

    any_spec = pl.BlockSpec(memory_space=pl.ANY)
    return pl.pallas_call(
        body, name=name, in_specs=[any_spec] * n, out_specs=[any_spec] * n,
        out_shape=[jax.ShapeDtypeStruct(a.shape, a.dtype) for a in arrays],
        scratch_shapes=[pltpu.SemaphoreType.DMA((n * n_other,)), pltpu.SemaphoreType.DMA((n * n_other,)),
                        pltpu.SemaphoreType.DMA((n,))],
    )(*arrays)


def pair_sum(name, both, got):
    P, _, R, C = both.shape
    tr = _row_tile(R, C, 4)

    def body(a_ref, b_ref, o_ref):
        core = lax.axis_index("c")
        o_ref[...] = (a_ref[0, core].astype(F32) + b_ref[0].astype(F32)).astype(BF16)[None]

    return pl.pallas_call(
        body, name=name, grid=(P, R // tr),
        in_specs=[pl.BlockSpec((1, 2, tr, C), lambda p, i: (p, 0, i, 0)), pl.BlockSpec((1, tr, C), lambda p, i: (p, i, 0))],
        out_specs=pl.BlockSpec((1, tr, C), lambda p, i: (p, i, 0)),
        out_shape=jax.ShapeDtypeStruct((P, R, C), BF16), compiler_params=_cparams(("parallel", "parallel")))(both, got)


def all_gather_vmem(name, v):
    def body(v_ref, out_ref, send_sems, recv_sems):
        me = _my_index()
        out_ref[me] = v_ref[...]
        copies = []
        for k, mask in enumerate(PEER_MASKS):
            cp = pltpu.make_async_remote_copy(src_ref=v_ref, dst_ref=out_ref.at[me], send_sem=send_sems.at[k],
                                              recv_sem=recv_sems.at[k], device_id=_coords(me ^ mask), device_id_type=MESH_IDS)
            cp.start()
            copies.append(cp)
        for cp in copies:
            cp.wait()

    vm = pl.BlockSpec(memory_space=pltpu.VMEM)
    return pl.pallas_call(
        body, name=name, in_specs=[vm], out_specs=vm, out_shape=jax.ShapeDtypeStruct((N_DEV,) + v.shape, v.dtype),
        scratch_shapes=[pltpu.SemaphoreType.DMA((N_PEERS,)), pltpu.SemaphoreType.DMA((N_PEERS,))],
    )(v)


ROW_KERNEL_VMEM_BUDGET = 24 * 1024 * 1024


def _row_tile(rows, cols, bufs):
    budget = ROW_KERNEL_VMEM_BUDGET // (bufs * 2 * 4 * max(cols, LANES))
    if rows <= budget:
        return rows
    for t in range(budget - budget % 16, 15, -16):
        if rows % t == 0:
            return t
    return rows


def sum_parts(name, parts):
    P, R, C = parts.shape
    tr = _row_tile(R, C, P + 1)

    def body(p_ref, o_ref):
        acc = p_ref[0].astype(F32)
        for s in range(1, P):
            acc = acc + p_ref[s].astype(F32)
        o_ref[...] = acc

    return pl.pallas_call(body, name=name, grid=(R // tr,), in_specs=[pl.BlockSpec((P, tr, C), lambda i: (0, i, 0))],
                          out_specs=pl.BlockSpec((tr, C), lambda i: (i, 0)), out_shape=jax.ShapeDtypeStruct((R, C), F32),
                          compiler_params=_cparams(("parallel",)))(parts)


def adamw(name, w, g, m, v):
    R, C = w.shape
    tr = _row_tile(R, C, 7)

    def body(w_ref, g_ref, m_ref, v_ref, d_ref, nm_ref, nv_ref):
        g = g_ref[...]
        nm = ADAM_B1 * m_ref[...] + (1.0 - ADAM_B1) * g
        nv = ADAM_B2 * v_ref[...] + (1.0 - ADAM_B2) * (g * g)
        m_hat = nm / (1.0 - ADAM_B1 ** ADAM_STEP)
        v_hat = nv / (1.0 - ADAM_B2 ** ADAM_STEP)
        d_ref[...] = -ADAM_LR * (m_hat / (jnp.sqrt(v_hat) + ADAM_EPS) + ADAM_WD * w_ref[...])
        nm_ref[...] = nm
        nv_ref[...] = nv

    spec = pl.BlockSpec((tr, C), lambda i: (i, 0))
    return pl.pallas_call(body, name=name, grid=(R // tr,), in_specs=[spec] * 4, out_specs=[spec] * 3,
                          out_shape=[jax.ShapeDtypeStruct((R, C), F32)] * 3, compiler_params=_cparams(("parallel",)))(w, g, m, v)


def loss_and_grad(name, x, target, L, tm=ROW_TILE):
    T, D = x.shape
    nlt = L // tm

    def body(x_ref, t_ref, loss_ref, dx_ref):
        i = pl.program_id(0)
        err = jnp.where(i < nlt, x_ref[...] - t_ref[...], 0.0)
        dx_ref[...] = err * (1.0 / D)
        part = 0.5 * jnp.sum(jnp.sum(err * err, axis=1, keepdims=True), axis=0, keepdims=True) * (1.0 / D)
        _accumulate(loss_ref, part, i == 0)

    return pl.pallas_call(
        body, name=name, grid=(T // tm,),
        in_specs=[pl.BlockSpec((tm, D), lambda i: (i, 0)), pl.BlockSpec((tm, D), lambda i: (jnp.minimum(i, nlt - 1), 0))],
        out_specs=[pl.BlockSpec((1, 1), lambda i: (0, 0)), pl.BlockSpec((tm, D), lambda i: (i, 0))],
        out_shape=[jax.ShapeDtypeStruct((1, 1), F32), jax.ShapeDtypeStruct((T, D), F32)],
        compiler_params=_cparams(("arbitrary",)))(x, target)


def small_fwd(name, fn, arrays, out_shapes):
    def body(*refs):
        res = fn(*[r[...] for r in refs[:len(arrays)]])
        for o_ref, r in zip(refs[len(arrays):], res):
            o_ref[...] = r

    return pl.pallas_call(body, name=name, out_shape=[jax.ShapeDtypeStruct(s, F32) for s in out_shapes])(*arrays)


def small_bwd(name, fn, arrays, cts):
    n = len(arrays)

    def body(*refs):
        _, vjp = jax.vjp(lambda *a: tuple(fn(*a)), *[r[...] for r in refs[:n]])
        grads = vjp(tuple(r[...] for r in refs[n:n + len(cts)]))
        for o_ref, g in zip(refs[n + len(cts):], grads):
            o_ref[...] = g

    return pl.pallas_call(body, name=name, out_shape=[jax.ShapeDtypeStruct(a.shape, F32) for a in arrays])(*arrays, *cts)


def fn_silu(x):
    return (x * jax.nn.sigmoid(x),)


FWD_NAMES = ["x", "c", "ctx", "c_ctx", "w_mod", "b_mod", "norm1_g", "norm2_g", "w_in", "ssm_conv_w", "ssm_conv_b",
             "ssm_dt_bias", "ssm_a_log", "ssm_d", "ssm_norm_g", "swa_q_norm_g", "swa_k_norm_g", "swa_sink", "mla_q_lat_g",
             "mla_kv_lat_g", "w_mla_uq", "w_mla_ukv", "mla_q_norm_g", "mla_k_norm_g", "w_p_ssm", "w_p_swa", "w_p_mla",
             "w_out", "w_ffn_in", "w_ffn_out"]
WEIGHT_NAMES = FWD_NAMES[3:]
GATHERED = ["w_in", "w_mla_uq", "w_mla_ukv", "w_p_ssm", "w_p_swa", "w_p_mla", "w_out", "w_ffn_in", "w_ffn_out"]
COLUMN_SHARDED = ("w_in", "w_mla_uq", "w_mla_ukv", "w_ffn_in")
REPLICATED = ["c_ctx", "b_mod", "norm1_g", "norm2_g", "ssm_conv_b", "ssm_dt_bias", "ssm_a_log", "ssm_d", "ssm_norm_g",
              "swa_q_norm_g", "swa_k_norm_g", "swa_sink", "mla_q_lat_g", "mla_kv_lat_g", "mla_q_norm_g", "mla_k_norm_g"]
IN_SEGS = [("xbc", SSM_CONV_DIM), ("dt", 2 * SSM_HEADS), ("ks", SWA_KV_HEADS * SWA_HEAD_DIM), ("vs", SWA_KV_HEADS * SWA_HEAD_DIM),
           ("ckv", MLA_KV_RANK), ("kr", MLA_ROPE), ("z", SSM_INNER), ("qs", SWA_Q_HEADS * SWA_HEAD_DIM), ("cq", MLA_Q_RANK),
           ("g1", None), ("g2", None), ("g3", None)]


def _pack(vectors, multiple):
    flat = jnp.concatenate([v.reshape(-1) for v in vectors])
    pad = (-flat.shape[0]) % multiple
    return jnp.pad(flat, (0, pad)).reshape(-1, LANES)


def _unpack(packed, shapes):
    flat, out, off = packed.reshape(-1), [], 0
    for s in shapes:
        n = int(np.prod(s))
        out.append(flat[off:off + n].reshape(s))
        off += n
    return out


def _rope_tables(L, T, rot_dim):
    nf = rot_dim // 4
    inv = jnp.power(ROPE_BASE, -jnp.arange(nf, dtype=F32) / nf)
    r, col = jnp.meshgrid(jnp.arange(L // GRID_W, dtype=F32), jnp.arange(GRID_W, dtype=F32), indexing="ij")
    ang = jnp.stack([r.reshape(-1)[:, None] * inv, col.reshape(-1)[:, None] * inv], axis=1)
    cos, sin = jnp.cos(ang), jnp.sin(ang)
    c = jnp.concatenate([cos[:, 0], cos[:, 0], cos[:, 1], cos[:, 1]], axis=1)
    s = jnp.concatenate([-sin[:, 0], sin[:, 0], -sin[:, 1], sin[:, 1]], axis=1)
    c = jnp.pad(c, ((0, T - L), (0, LANES - rot_dim)), constant_values=1.0)
    s = jnp.pad(s, ((0, T - L), (0, LANES - rot_dim)))
    return c, s


def _pad_rows(a, rows):
    return jnp.pad(a, ((0, rows - a.shape[0]), (0, 0)))


def _row(w, per_head=0, off=0, diff=True):
    return ("row", w, per_head, off, diff)


PAR, PAR_ND = ("par", True), ("par", False)
GRP = ("grp", True)


def kernel(*args):
    n_fwd, n_w = len(FWD_NAMES), len(WEIGHT_NAMES)
    inp = dict(zip(FWD_NAMES, args[:n_fwd]))
    loss_target = args[n_fwd]
    mom_m = dict(zip(WEIGHT_NAMES, args[n_fwd + 1:n_fwd + 1 + n_w]))
    mom_v = dict(zip(WEIGHT_NAMES, args[n_fwd + 1 + n_w:]))

    x, ctx = inp["x"][0], inp["ctx"][0]
    L, D = x.shape
    n_ctx = ctx.shape[0]
    T = L + n_ctx
    depth = inp["w_in"].shape[0]
    me = _my_index()
    in_widths = [w if w is not None else D for _, w in IN_SEGS]
    in_offs = np.concatenate([[0], np.cumsum(in_widths)]).tolist()
    ffn_h = inp["w_ffn_out"].shape[1] * N_DEV
    cfg_swa = AttnCfg(SWA_Q_HEADS, SWA_Q_HEADS // SWA_KV_HEADS, SWA_HEAD_DIM, SWA_HEAD_DIM, SWA_HEAD_DIM ** -0.5, SWA_WINDOW,
                      True, L, T, 256, 1)
    cfg_mla = AttnCfg(MLA_HEADS, 1, MLA_QK_PAD, MLA_V, MLA_QK ** -0.5, None, False, L, T, 1024, 2)
    cfg_mla_bwd = AttnCfg(MLA_HEADS, 1, MLA_QK_PAD, MLA_V, MLA_QK ** -0.5, None, False, L, T, 2048, 1)

    def rf(name, fn, descs, arrays, outs, heads=1):
        return rowop_fwd(name, fn, descs, arrays, outs, T, L, heads=heads)

    def rb(name, fn, descs, arrays, outs, cts, heads=1, add=None, to_matmul=()):
        return rowop_bwd(name, fn, descs, arrays, outs, cts, T, L, heads=heads, add=add, to_matmul=to_matmul)

    local = []
    for n in GATHERED:
        w = inp[n]
        local.append((jnp.swapaxes(w, 1, 2) if n in COLUMN_SHARDED else w).astype(BF16))
    gathered = dict(zip(GATHERED, all_gather_hbm("gather_weights", local)))

    def full(n, l):
        g = gathered[n][:, l]
        return g.reshape(g.shape[0] * g.shape[1], g.shape[2])

    def layer_weights(l):
        wt = {}
        w_in_t = full("w_in", l)
        for (sn, _), o, w in zip(IN_SEGS, in_offs, in_widths):
            seg = w_in_t[o:o + w]
            wt[sn] = _pad_rows(seg, LANES) if sn == "kr" else seg
        uq = full("w_mla_uq", l).reshape(MLA_HEADS, MLA_QK, MLA_Q_RANK)
        wt["uqn"] = uq[:, :MLA_NOPE].reshape(MLA_HEADS * MLA_NOPE, MLA_Q_RANK)
        wt["uqr"] = jnp.pad(uq[:, MLA_NOPE:], ((0, 0), (0, LANES - MLA_ROPE), (0, 0))).reshape(MLA_HEADS * LANES, MLA_Q_RANK)
        ukv = full("w_mla_ukv", l).reshape(MLA_HEADS, MLA_NOPE + MLA_V, MLA_KV_RANK)
        wt["uk"] = ukv[:, :MLA_NOPE].reshape(MLA_HEADS * MLA_NOPE, MLA_KV_RANK)
        wt["uv"] = ukv[:, MLA_NOPE:].reshape(MLA_HEADS * MLA_V, MLA_KV_RANK)
        for n in ("w_p_ssm", "w_p_swa", "w_p_mla", "w_out", "w_ffn_out"):
            wt[n] = full(n, l)
        ffn_in_t = full("w_ffn_in", l)
        wt["fg"], wt["fu"] = ffn_in_t[:ffn_h], ffn_in_t[ffn_h:]
        return wt

    def layer_params(l):
        p = {}
        for n in ("norm1_g", "norm2_g", "ssm_conv_b", "ssm_norm_g", "swa_q_norm_g", "swa_k_norm_g", "mla_q_lat_g", "mla_kv_lat_g"):
            p[n] = inp[n][l][None]
        p["dt_bias"] = inp["ssm_dt_bias"][l].reshape(1, 2 * SSM_HEADS)
        p["alog_row"] = [inp["ssm_a_log"][l][d][None] for d in range(2)]
        p["alog_col"] = [inp["ssm_a_log"][l][d][:, None] for d in range(2)]
        p["d_lane"] = jnp.repeat(inp["ssm_d"][l], SSM_HEAD_DIM)[None]
        p["sink"] = inp["swa_sink"][l].reshape(SWA_Q_HEADS, 1, 1)
        for n, key in (("mla_q_norm_g", "gq"), ("mla_k_norm_g", "gk")):
            g = inp[n][l]
            p[key + "n"] = g[:MLA_NOPE][None]
            p[key + "r"] = jnp.pad(g[MLA_NOPE:], (0, LANES - MLA_ROPE))[None]
        return p

    conv_local = _pack([inp["ssm_conv_w"]], 8 * LANES)
    conv_all = all_gather_vmem("gather_conv_w", conv_local)
    cw = inp["ssm_conv_w"].shape
    conv_full = conv_all.reshape(N_DEV, -1)[:, :cw[0] * cw[1] * cw[2]].reshape(N_DEV, cw[0], cw[1], cw[2])
    conv_full = jnp.moveaxis(conv_full, 0, 2).reshape(cw[0], cw[1], N_DEV * cw[2])
    conv_w8 = jnp.pad(conv_full, ((0, 0), (0, 8 - cw[1]), (0, 0)))

    silu_c, silu_cc = small_fwd("silu_c", lambda a, b: fn_silu(a) + fn_silu(b), [inp["c"], inp["c_ctx"][None]], [(1, D), (1, D)])
    silu_all = all_gather_vmem("gather_silu_c", silu_c.reshape(D // LANES, LANES)).reshape(N_DEV, D)
    S_rows = 2 * N_DEV
    S_mat = jnp.concatenate([silu_all, silu_cc, jnp.zeros((S_rows - N_DEV - 1, D), F32)], axis=0)
    mod_cols = inp["w_mod"].shape[2]
    mods_local = []
    for l in range(depth):
        bias = lax.dynamic_slice(inp["b_mod"][l], (me * mod_cols,), (mod_cols,))
        mods_local.append(matmul(S_mat, inp["w_mod"][l], "nn", f"mod{l}", add=jnp.broadcast_to(bias[None], (S_rows, mod_cols))))
    mods_all = all_gather_vmem("gather_mods", jnp.stack(mods_local).reshape(-1, LANES))
    mods_all = jnp.moveaxis(mods_all.reshape(N_DEV, depth, S_rows, mod_cols), 0, 2).reshape(depth, S_rows, N_DEV * mod_cols)
    mods_lat = lax.dynamic_slice(mods_all, (0, me, 0), (depth, 1, N_DEV * mod_cols))[:, 0]
    mods_ctx = mods_all[:, N_DEV]

    def layer_mods(l):
        return [jnp.stack([mods_lat[l, j * D:(j + 1) * D], mods_ctx[l, j * D:(j + 1) * D]])[:, None] for j in range(6)]

    cs_swa = _rope_tables(L, T, SWA_HEAD_DIM)
    cs_mla = _rope_tables(L, T, MLA_ROPE)
    nm_descs = [_row(D), PAR, GRP, GRP]
    resid_descs = [_row(D, diff=False), _row(D), GRP]
    tab = [_row(LANES, diff=False), _row(LANES, diff=False)]
    swaq_descs = [_row(SWA_HEAD_DIM, 1), PAR] + tab
    swakv_descs = [_row(SWA_HEAD_DIM, 1), _row(SWA_HEAD_DIM, 1), PAR] + tab
    mlaq_descs = [_row(LANES, 1), _row(LANES, 1), PAR, PAR] + tab
    mlakv_descs = [_row(LANES, 1), _row(LANES, 1), _row(LANES), PAR, PAR] + tab
    ssdout_descs = [_row(SSM_INNER), _row(SSM_INNER, diff=False), _row(SSM_INNER), _row(SSM_INNER), PAR, PAR]
    merge_descs = [_row(D)] * 6
    swiglu_descs = [_row(ffn_h), _row(ffn_h)]
    seg_names = [sn for sn, _ in IN_SEGS]
    seg_groups = [seg_names[:7], seg_names[7:]]

    def layer_fwd(l, X, wt, p, mods):
        sh1, sc1, gt1, sh2, sc2, gt2 = mods
        r = {"X": X}
        r["h1"] = rf(f"l{l}_norm1", fn_norm_mod, nm_descs, [X, p["norm1_g"], sh1, sc1], [(D, 0, BF16)])[0]
        for gi, group in enumerate(seg_groups):
            dts = [BF16 if sn in ("g1", "g2", "g3") else F32 for sn in group]
            r.update(zip(group, matmul_multi(f"l{l}_in{gi}", r["h1"], [wt[sn] for sn in group], dts)))
        r["u"] = conv_fwd(f"l{l}_conv", r["xbc"], conv_w8[l], p["ssm_conv_b"], L)
        r["dts"] = rf(f"l{l}_softplus", fn_softplus, [_row(2 * SSM_HEADS), PAR], [r["dt"], p["dt_bias"]], [(2 * SSM_HEADS, 0, F32)])[0]
        for d in range(2):
            dt_d = r["dts"][:, d * SSM_HEADS:(d + 1) * SSM_HEADS]
            r[f"dt{d}"], r[f"dtt{d}"] = dt_d, dt_d.T
            r[f"y{d}"], r[f"hin{d}"] = ssd_fwd(f"l{l}_ssd{d}", d, r["u"], dt_d, dt_d.T, p["alog_row"][d], p["alog_col"][d], L)
        r["ys"] = rf(f"l{l}_ssd_out", fn_ssd_out, ssdout_descs, [r["y0"], r["y1"], r["u"], r["z"], p["d_lane"], p["ssm_norm_g"]],
                     [(SSM_INNER, 0, BF16)])[0]
        r["Qs"] = rf(f"l{l}_swa_q", fn_swa_q, swaq_descs, [r["qs"], p["swa_q_norm_g"], *cs_swa], [(SWA_HEAD_DIM, 1, BF16)], SWA_Q_HEADS)[0]
        r["Ks"], r["Vs"] = rf(f"l{l}_swa_kv", fn_swa_kv, swakv_descs, [r["ks"], r["vs"], p["swa_k_norm_g"], *cs_swa],
                              [(SWA_HEAD_DIM, 1, BF16), (SWA_HEAD_DIM, 1, BF16)], SWA_KV_HEADS)
        r["Os"], r["lse_s"] = flash_fwd(f"l{l}_swa_fwd", cfg_swa, r["Qs"], r["Ks"], r["Vs"], p["sink"])
        r["cqn"] = rf(f"l{l}_q_lat", fn_rms, [_row(MLA_Q_RANK), PAR], [r["cq"], p["mla_q_lat_g"]], [(MLA_Q_RANK, 0, BF16)])[0]
        r["qn"], r["qr"] = matmul_multi(f"l{l}_uq", r["cqn"], [wt["uqn"], wt["uqr"]])
        r["Qm"] = rf(f"l{l}_mla_q", fn_mla_q, mlaq_descs, [r["qn"], r["qr"], p["gqn"], p["gqr"], *cs_mla], [(MLA_QK_PAD, 1, BF16)], MLA_HEADS)[0]
        r["ckvn"] = rf(f"l{l}_kv_lat", fn_rms, [_row(MLA_KV_RANK), PAR], [r["ckv"], p["mla_kv_lat_g"]], [(MLA_KV_RANK, 0, BF16)])[0]
        r["kn"], r["vp"] = matmul_multi(f"l{l}_ukv", r["ckvn"], [wt["uk"], wt["uv"]])
        r["Km"], r["Vm"] = rf(f"l{l}_mla_kv", fn_mla_kv, mlakv_descs, [r["kn"], r["vp"], r["kr"], p["gkn"], p["gkr"], *cs_mla],
                              [(MLA_QK_PAD, 1, BF16), (MLA_V, 1, BF16)], MLA_HEADS)
        r["Om"], r["lse_m"] = flash_fwd(f"l{l}_mla_fwd", cfg_mla, r["Qm"], r["Km"], r["Vm"], None)
        r["P1"] = matmul(r["ys"], wt["w_p_ssm"], "nn", f"l{l}_p_ssm", out_dtype=BF16)
        r["P2"] = matmul(r["Os"], wt["w_p_swa"], "nn", f"l{l}_p_swa", out_dtype=BF16)
        r["P3"] = matmul(r["Om"], wt["w_p_mla"], "nn", f"l{l}_p_mla", out_dtype=BF16)
        r["mg"] = rf(f"l{l}_merge", fn_merge, merge_descs, [r["g1"], r["g2"], r["g3"], r["P1"], r["P2"], r["P3"]], [(D, 0, BF16)])[0]
        r["A"] = matmul(r["mg"], wt["w_out"], "nn", f"l{l}_out")
        r["X1"] = rf(f"l{l}_resid1", fn_resid, resid_descs, [X, r["A"], gt1], [(D, 0, F32)])[0]
        r["h2"] = rf(f"l{l}_norm2", fn_norm_mod, nm_descs, [r["X1"], p["norm2_g"], sh2, sc2], [(D, 0, BF16)])[0]
        r["Fg"] = matmul(r["h2"], wt["fg"], "nt", f"l{l}_ffn_g", out_dtype=BF16)
        r["Fu"] = matmul(r["h2"], wt["fu"], "nt", f"l{l}_ffn_u", out_dtype=BF16)
        r["sg"] = rf(f"l{l}_swiglu", fn_swiglu, swiglu_descs, [r["Fg"], r["Fu"]], [(ffn_h, 0, BF16)])[0]
        r["B"] = matmul(r["sg"], wt["w_ffn_out"], "nn", f"l{l}_ffn_out")
        X2 = rf(f"l{l}_resid2", fn_resid, resid_descs, [r["X1"], r["B"], gt2], [(D, 0, F32)])[0]
        return X2, r

    def attn_bwd(tag, cfg, q, k, v, o, lse, do, sink):
        res = attn_delta(f"{tag}_delta", cfg, o, do, lse, sink)
        delta, dob = res[0], res[1]
        dsink = res[2] if cfg.has_sink else None
        dq, dk, dv = flash_bwd_fused(f"{tag}_bwd", cfg, q, k, v, dob, lse, delta)
        return dq, dk, dv, dsink

    def layer_bwd(l, dX2, r, wt, p, mods):
        sh1, sc1, gt1, sh2, sc2, gt2 = mods
        g, gw = {}, {}
        dmod = [None] * 6
        dB, dmod[5] = rb(f"l{l}_resid2_b", fn_resid, resid_descs, [r["X1"], r["B"], gt2], [(D, 0, F32)], [dX2], to_matmul=(1,))
        dsg = matmul(dB, wt["w_ffn_out"], "nt", f"l{l}_ffn_out_da")
        gw["w_ffn_out"] = matmul(r["sg"], dB, "tn", f"l{l}_ffn_out_dw", out_dtype=BF16)
        dFg, dFu = rb(f"l{l}_swiglu_b", fn_swiglu, swiglu_descs, [r["Fg"], r["Fu"]], [(ffn_h, 0, BF16)], [dsg], to_matmul=(0, 1))
        dh2 = matmul(dFg, wt["fg"], "nn", f"l{l}_ffn_g_da")
        dh2 = matmul(dFu, wt["fu"], "nn", f"l{l}_ffn_u_da", add=dh2)
        gw["w_ffn_in"] = jnp.concatenate([matmul(r["h2"], dFg, "tn", f"l{l}_ffn_g_dw", out_dtype=BF16),
                                          matmul(r["h2"], dFu, "tn", f"l{l}_ffn_u_dw", out_dtype=BF16)], axis=1)
        dX1, g["norm2_g"], dmod[3], dmod[4] = rb(f"l{l}_norm2_b", fn_norm_mod, nm_descs, [r["X1"], p["norm2_g"], sh2, sc2],
                                                [(D, 0, BF16)], [dh2], add={0: dX2})
        dA, dmod[2] = rb(f"l{l}_resid1_b", fn_resid, resid_descs, [r["X"], r["A"], gt1], [(D, 0, F32)], [dX1], to_matmul=(1,))
        dmg = matmul(dA, wt["w_out"], "nt", f"l{l}_out_da")
        gw["w_out"] = matmul(r["mg"], dA, "tn", f"l{l}_out_dw", out_dtype=BF16)
        dsegs = {}
        dsegs["g1"], dsegs["g2"], dsegs["g3"], dP1, dP2, dP3 = rb(
            f"l{l}_merge_b", fn_merge, merge_descs, [r["g1"], r["g2"], r["g3"], r["P1"], r["P2"], r["P3"]], [(D, 0, BF16)], [dmg],
            to_matmul=tuple(range(6)))
        dys = matmul(dP1, wt["w_p_ssm"], "nt", f"l{l}_p_ssm_da")
        dOs = matmul(dP2, wt["w_p_swa"], "nt", f"l{l}_p_swa_da")
        dOm = matmul(dP3, wt["w_p_mla"], "nt", f"l{l}_p_mla_da")
        gw["w_p_ssm"] = matmul(r["ys"], dP1, "tn", f"l{l}_p_ssm_dw", out_dtype=BF16)
        gw["w_p_swa"] = matmul(r["Os"], dP2, "tn", f"l{l}_p_swa_dw", out_dtype=BF16)
        gw["w_p_mla"] = matmul(r["Om"], dP3, "tn", f"l{l}_p_mla_dw", out_dtype=BF16)
        dQm, dKm, dVm, _ = attn_bwd(f"l{l}_mla", cfg_mla_bwd, r["Qm"], r["Km"], r["Vm"], r["Om"], r["lse_m"], dOm, None)
        dkn, dvp, dsegs["kr"], dgkn, dgkr = rb(f"l{l}_mla_kv_b", fn_mla_kv, mlakv_descs,
                                               [r["kn"], r["vp"], r["kr"], p["gkn"], p["gkr"], *cs_mla],
                                               [(MLA_QK_PAD, 1, BF16), (MLA_V, 1, BF16)], [dKm, dVm], MLA_HEADS, to_matmul=(0, 1, 2))
        dckvn = matmul_sum(f"l{l}_ukv_da", [(dkn, wt["uk"]), (dvp, wt["uv"])])
        dw_uk = matmul(r["ckvn"], dkn, "tn", f"l{l}_uk_dw", out_dtype=BF16).reshape(MLA_KV_RANK, MLA_HEADS, MLA_NOPE)
        dw_uv = matmul(r["ckvn"], dvp, "tn", f"l{l}_uv_dw", out_dtype=BF16).reshape(MLA_KV_RANK, MLA_HEADS, MLA_V)
        gw["w_mla_ukv"] = jnp.concatenate([dw_uk, dw_uv], axis=2).reshape(MLA_KV_RANK, -1)
        dsegs["ckv"], g["mla_kv_lat_g"] = rb(f"l{l}_kv_lat_b", fn_rms, [_row(MLA_KV_RANK), PAR], [r["ckv"], p["mla_kv_lat_g"]],
                                             [(MLA_KV_RANK, 0, BF16)], [dckvn], to_matmul=(0,))
        dqn, dqr, dgqn, dgqr = rb(f"l{l}_mla_q_b", fn_mla_q, mlaq_descs, [r["qn"], r["qr"], p["gqn"], p["gqr"], *cs_mla],
                                  [(MLA_QK_PAD, 1, BF16)], [dQm], MLA_HEADS, to_matmul=(0, 1))
        dcqn = matmul_sum(f"l{l}_uq_da", [(dqn, wt["uqn"]), (dqr, wt["uqr"])])
        dw_uqn = matmul(r["cqn"], dqn, "tn", f"l{l}_uqn_dw", out_dtype=BF16).reshape(MLA_Q_RANK, MLA_HEADS, MLA_NOPE)
        dw_uqr = matmul(r["cqn"], dqr, "tn", f"l{l}_uqr_dw", out_dtype=BF16).reshape(MLA_Q_RANK, MLA_HEADS, LANES)[:, :, :MLA_ROPE]
        gw["w_mla_uq"] = jnp.concatenate([dw_uqn, dw_uqr], axis=2).reshape(MLA_Q_RANK, -1)
        dsegs["cq"], g["mla_q_lat_g"] = rb(f"l{l}_q_lat_b", fn_rms, [_row(MLA_Q_RANK), PAR], [r["cq"], p["mla_q_lat_g"]],
                                           [(MLA_Q_RANK, 0, BF16)], [dcqn], to_matmul=(0,))
        g["mla_q_norm_g"] = jnp.concatenate([dgqn[0], dgqr[0, :MLA_ROPE]])
        g["mla_k_norm_g"] = jnp.concatenate([dgkn[0], dgkr[0, :MLA_ROPE]])
        dQs, dKs, dVs, dsink = attn_bwd(f"l{l}_swa", cfg_swa, r["Qs"], r["Ks"], r["Vs"], r["Os"], r["lse_s"], dOs, p["sink"])
        g["swa_sink"] = dsink.reshape(SWA_Q_HEADS)
        dsegs["qs"], g["swa_q_norm_g"] = rb(f"l{l}_swa_q_b", fn_swa_q, swaq_descs, [r["qs"], p["swa_q_norm_g"], *cs_swa],
                                            [(SWA_HEAD_DIM, 1, BF16)], [dQs], SWA_Q_HEADS, to_matmul=(0,))
        dsegs["ks"], dsegs["vs"], g["swa_k_norm_g"] = rb(f"l{l}_swa_kv_b", fn_swa_kv, swakv_descs,
                                                         [r["ks"], r["vs"], p["swa_k_norm_g"], *cs_swa],
                                                         [(SWA_HEAD_DIM, 1, BF16), (SWA_HEAD_DIM, 1, BF16)], [dKs, dVs], SWA_KV_HEADS,
                                                         to_matmul=(0, 1))
        dy, dxs, dsegs["z"], dd_lane, g["ssm_norm_g"] = rb(
            f"l{l}_ssd_out_b", fn_ssd_out, ssdout_descs, [r["y0"], r["y1"], r["u"], r["z"], p["d_lane"], p["ssm_norm_g"]],
            [(SSM_INNER, 0, BF16)], [dys], to_matmul=(3,))
        g["ssm_d"] = dd_lane.reshape(SSM_HEADS, SSM_HEAD_DIM).sum(axis=1)
        du, ddts, dalog = None, [], []
        for d in range(2):
            du, ddt, ddtt, dar, dac = ssd_bwd(f"l{l}_ssd{d}_b", d, r["u"], r[f"dt{d}"], r[f"dtt{d}"], p["alog_row"][d], p["alog_col"][d],
                                              r[f"hin{d}"], dy, L, add_x=dxs if d == 0 else None, add_u=du)
            ddts.append(ddt + ddtt.T)
            dalog.append(dar[0] + dac[:, 0])
        g["ssm_a_log"] = jnp.stack(dalog)
        dsegs["xbc"], dconv_w, g["ssm_conv_b"] = conv_bwd(f"l{l}_conv_b", r["xbc"], conv_w8[l], p["ssm_conv_b"], du, L)
        dsegs["dt"], ddt_bias = rb(f"l{l}_softplus_b", fn_softplus, [_row(2 * SSM_HEADS), PAR], [r["dt"], p["dt_bias"]],
                                   [(2 * SSM_HEADS, 0, F32)], [jnp.concatenate(ddts, axis=1)], to_matmul=(0,))
        g["ssm_dt_bias"] = ddt_bias.reshape(2, SSM_HEADS)
        g["ssm_conv_w"] = dconv_w[:SSM_CONV]
        dh1, dws = None, []
        for gi, group in enumerate(seg_groups):
            dh1 = matmul_sum(f"l{l}_in_da{gi}", [(dsegs[sn], wt[sn]) for sn in group], add=dh1)
        for sn, w in zip(seg_names, in_widths):
            dws.append(matmul(r["h1"], dsegs[sn], "tn", f"l{l}_in_{sn}_dw", out_dtype=BF16)[:, :w])
        gw["w_in"] = jnp.concatenate(dws, axis=1)
        dX, g["norm1_g"], dmod[0], dmod[1] = rb(f"l{l}_norm1_b", fn_norm_mod, nm_descs, [r["X"], p["norm1_g"], sh1, sc1],
                                               [(D, 0, BF16)], [dh1], add={0: dX1})
        for n in ("norm1_g", "norm2_g", "ssm_conv_b", "ssm_norm_g", "swa_q_norm_g", "swa_k_norm_g", "mla_q_lat_g", "mla_kv_lat_g"):
            g[n] = g[n][0]
        dmod_lat = jnp.concatenate([dm[0, 0] for dm in dmod])
        dmod_ctx = jnp.concatenate([dm[1, 0] for dm in dmod])
        return dX, g, gw, dmod_lat, dmod_ctx

    X = jnp.concatenate([x, ctx], axis=0)
    saved = []
    for l in range(depth):
        wt, p, mods = layer_weights(l), layer_params(l), layer_mods(l)
        X, r = layer_fwd(l, X, wt, p, mods)
        saved.append((r, wt, p, mods))
    loss_part, dX = loss_and_grad("loss", X, loss_target[0], L)
    loss = lax.psum(loss_part[0, 0], ("x", "y", "c"))
    small_g = [None] * depth
    big_g = [None] * depth
    dmods = [None] * depth
    for l in reversed(range(depth)):
        r, wt, p, mods = saved[l]
        dX, small_g[l], big_g[l], dm_lat, dm_ctx = layer_bwd(l, dX, r, wt, p, mods)
        dmods[l] = jnp.stack([dm_lat, dm_ctx])
    grad_x = dX[:L][None]

    dm_all = all_gather_vmem("gather_dmods", jnp.stack(dmods).reshape(-1, LANES)).reshape(N_DEV, depth, 2, N_DEV * mod_cols)
    dm_rows = jnp.concatenate([jnp.moveaxis(dm_all[:, :, 0], 0, 1), dm_all[:, :, 1].sum(axis=0)[:, None],
                               jnp.zeros((depth, S_rows - N_DEV - 1, N_DEV * mod_cols), F32)], axis=1)
    dm_mine = lax.dynamic_slice(dm_rows, (0, 0, me * mod_cols), (depth, S_rows, mod_cols))
    grads = {}
    grads["w_mod"] = jnp.stack([matmul(S_mat, dm_mine[l], "tn", f"mod{l}_dw") for l in range(depth)])
    d_silu = None
    for l in range(depth):
        d_silu = matmul(dm_mine[l], inp["w_mod"][l], "nt", f"mod{l}_da", add=d_silu)
    small = {n: jnp.stack([small_g[l][n] for l in range(depth)]) for n in small_g[0]}
    small["c_ctx"] = small_bwd("silu_c_b", fn_silu, [inp["c_ctx"][None]], [d_silu[N_DEV:N_DEV + 1]])[0][0]
    small["b_mod"] = jnp.stack(dmods).sum(axis=1)

    rep_shapes = [inp[n].shape for n in REPLICATED]
    conv_shape = (depth, SSM_CONV, SSM_CONV_DIM)
    packed = _pack([small[n] for n in REPLICATED] + [small["ssm_conv_w"]], 8 * LANES)
    small_sum = sum_parts("sum_small", all_gather_vmem("gather_small", packed))
    summed = _unpack(small_sum, rep_shapes + [conv_shape])
    for n, gsum in zip(REPLICATED, summed):
        grads[n] = gsum
    grads["ssm_conv_w"] = lax.dynamic_slice(summed[-1], (0, 0, me * cw[2]), cw)

    slabs = []
    for n in GATHERED:
        gfull = jnp.stack([big_g[l][n] for l in range(depth)])
        if n in COLUMN_SHARDED:
            k_dim, n_dim = gfull.shape[1], gfull.shape[2]
            slab = jnp.moveaxis(gfull.reshape(depth, k_dim, N_DEV, n_dim // N_DEV), 2, 0)
        else:
            k_dim, n_dim = gfull.shape[1], gfull.shape[2]
            slab = jnp.moveaxis(gfull.reshape(depth, N_DEV, k_dim // N_DEV, n_dim), 1, 0)
        slabs.append(slab.reshape((N_CHIPS, 2) + slab.shape[1:]))
    from_sibling = exchange_sibling("exchange_sibling", slabs)
    chip_sums = []
    for n, slab, got in zip(GATHERED, slabs, from_sibling):
        shp = inp[n].shape
        rows = shp[0] * shp[1]
        chip_sums.append(pair_sum(f"pair_{n}", slab.reshape(N_CHIPS, 2, rows, shp[2]), got.reshape(N_CHIPS, rows, shp[2])))
    for n, parts in zip(GATHERED, exchange_chips("exchange_chips", chip_sums)):
        grads[n] = sum_parts(f"sum_{n}", parts).reshape(inp[n].shape)

    delta, new_m, new_v = {}, {}, {}
    rep_pack = lambda d: _pack([d[n] for n in REPLICATED], 8 * LANES)
    rep_out = adamw("adamw_small", rep_pack(inp), rep_pack(grads), rep_pack(mom_m), rep_pack(mom_v))
    for out, res in zip((delta, new_m, new_v), rep_out):
        for n, a in zip(REPLICATED, _unpack(res, rep_shapes)):
            out[n] = a
    for n in ["w_mod", "ssm_conv_w"] + GATHERED:
        shp = inp[n].shape
        two_d = (shp[0] * shp[1], shp[2])
        res = adamw(f"adamw_{n}", inp[n].reshape(two_d), grads[n].reshape(two_d), mom_m[n].reshape(two_d), mom_v[n].reshape(two_d))
        delta[n], new_m[n], new_v[n] = [a.reshape(shp) for a in res]

    return (loss, grad_x, *[grads[n] for n in WEIGHT_NAMES], *[delta[n] for n in WEIGHT_NAMES],
            *[new_m[n] for n in WEIGHT_NAMES], *[new_v[n] for n in WEIGHT_NAMES])
```

```python
import functools
import math

import numpy as np
import jax
import jax.numpy as jnp
from jax import lax
from jax.experimental import pallas as pl
from jax.experimental.pallas import tpu as pltpu

F32 = jnp.float32
BF16 = jnp.bfloat16

N_DEV = 8
V7X_VMEM_BYTES = 64 * 1024 * 1024
VMEM_LIMIT_BYTES = V7X_VMEM_BYTES - 8 * 1024 * 1024
LANES = 128

EPS = 1e-6
ROPE_BASE = 10000.0
GRID_W = 64
SSM_HEADS, SSM_HEAD_DIM, SSM_GROUPS, SSM_STATE, SSM_CONV, SSM_CHUNK = 16, 64, 2, 128, 5, 128
SSM_INNER = SSM_HEADS * SSM_HEAD_DIM
SSM_CONV_DIM = SSM_INNER + 2 * SSM_GROUPS * SSM_STATE
SWA_Q_HEADS, SWA_KV_HEADS, SWA_HEAD_DIM, SWA_WINDOW = 8, 2, 128, 128
MLA_HEADS, MLA_Q_RANK, MLA_KV_RANK, MLA_NOPE, MLA_ROPE, MLA_V = 8, 384, 256, 128, 64, 128
MLA_QK = MLA_NOPE + MLA_ROPE
MLA_QK_PAD = 2 * LANES
ADAM_LR, ADAM_B1, ADAM_B2, ADAM_EPS, ADAM_WD, ADAM_STEP = 0.001, 0.9, 0.999, 1e-08, 0.01, 10

ROW_TILE = 256


def _cparams(sem, **kw):
    return pltpu.CompilerParams(dimension_semantics=sem, vmem_limit_bytes=VMEM_LIMIT_BYTES, **kw)


def _pick(dim, prefs):
    for p in prefs:
        if dim % p == 0:
            return p
    return dim


def matmul(a, b, mode, name, out_dtype=F32, add=None):
    if mode == "nn":
        (M, K), (K2, N) = a.shape, b.shape
    elif mode == "nt":
        (M, K), (N, K2) = a.shape, b.shape
    else:
        (K, M), (K2, N) = a.shape, b.shape
    assert K == K2, (name, a.shape, b.shape)
    has_add = add is not None
    tm, tn, tk = _matmul_tiles(M, N, K, a.dtype.itemsize, b.dtype.itemsize, jnp.dtype(out_dtype).itemsize, has_add,
                                   m_on_lanes=(mode == "tn"))
    nk = K // tk
    dims = {"nn": (((1,), (0,)), ((), ())), "nt": (((1,), (1,)), ((), ())), "tn": (((0,), (0,)), ((), ()))}[mode]
    a_spec = pl.BlockSpec((tk, tm), lambda i, j, k: (k, i)) if mode == "tn" else pl.BlockSpec((tm, tk), lambda i, j, k: (i, k))
    b_spec = pl.BlockSpec((tn, tk), lambda i, j, k: (j, k)) if mode == "nt" else pl.BlockSpec((tk, tn), lambda i, j, k: (k, j))
    o_spec = pl.BlockSpec((tm, tn), lambda i, j, k: (i, j))

    def body(*refs):
        a_ref, b_ref = refs[:2]
        c_ref = refs[2] if has_add else None
        o_ref = refs[3] if has_add else refs[2]
        part = lax.dot_general(a_ref[...].astype(BF16), b_ref[...].astype(BF16), dims, preferred_element_type=F32)
        if nk == 1:
            o_ref[...] = (part + c_ref[...] if has_add else part).astype(o_ref.dtype)
            return
        acc_ref = refs[-1]
        k = pl.program_id(2)

        @pl.when(k == 0)
        def _():
            acc_ref[...] = part + c_ref[...] if has_add else part

        @pl.when(k > 0)
        def _():
            acc_ref[...] += part

        @pl.when(k == nk - 1)
        def _():
            o_ref[...] = acc_ref[...].astype(o_ref.dtype)

    ins = [a, b] + ([add] if has_add else [])
    in_specs = [a_spec, b_spec] + ([o_spec] if has_add else [])
    return pl.pallas_call(
        body, name=name, grid=(M // tm, N // tn, nk), in_specs=in_specs, out_specs=o_spec,
        out_shape=jax.ShapeDtypeStruct((M, N), out_dtype),
        scratch_shapes=[pltpu.VMEM((tm, tn), F32)] if nk > 1 else [],
        input_output_aliases=({2: 0} if has_add else {}),
        compiler_params=_cparams(("parallel", "parallel", "arbitrary")),
    )(*ins)


def matmul_sum(name, pairs, add=None):
    M, N = pairs[0][0].shape[0], pairs[0][1].shape[1]
    n = len(pairs)
    has_add = add is not None
    resident = sum(2 * b.shape[0] * N * b.dtype.itemsize for _, b in pairs)
    tm = next((t for t in (768, 512, 384, 256, 128) if M % t == 0 and resident + sum(
        2 * t * a.shape[1] * a.dtype.itemsize + t * a.shape[1] * 2 for a, _ in pairs) + 6 * t * N * 4 <= MATMUL_VMEM_BUDGET), None)
    assert tm is not None, name

    def body(*refs):
        acc = refs[2 * n][...] if has_add else None
        for s in range(n):
            part = jnp.dot(refs[2 * s][...].astype(BF16), refs[2 * s + 1][...].astype(BF16), preferred_element_type=F32)
            acc = part if acc is None else acc + part
        refs[-1][...] = acc

    o_spec = pl.BlockSpec((tm, N), lambda i: (i, 0))
    in_specs, ins = [], []
    for a, b in pairs:
        in_specs += [pl.BlockSpec((tm, a.shape[1]), lambda i: (i, 0)), pl.BlockSpec(b.shape, lambda i: (0, 0))]
        ins += [a, b]
    if has_add:
        in_specs.append(o_spec)
        ins.append(add)
    return pl.pallas_call(body, name=name, grid=(M // tm,), in_specs=in_specs, out_specs=o_spec,
                          out_shape=jax.ShapeDtypeStruct((M, N), F32), input_output_aliases=({2 * n: 0} if has_add else {}),
                          compiler_params=_cparams(("parallel",)))(*ins)


def matmul_multi(name, a, bs, out_dtypes=None):
    out_dtypes = out_dtypes or [F32] * len(bs)
    M, K = a.shape
    n = len(bs)
    resident = sum(2 * b.shape[0] * K * b.dtype.itemsize for b in bs)
    n_total = sum(b.shape[0] for b in bs)
    tm = next((t for t in (768, 512, 384, 256, 128) if M % t == 0 and
               resident + 2 * t * K * a.dtype.itemsize + 3 * t * n_total * 4 <= MATMUL_VMEM_BUDGET), None)
    assert tm is not None, name

    def body(*refs):
        lhs = refs[0][...].astype(BF16)
        for s in range(n):
            out = lax.dot_general(lhs, refs[1 + s][...].astype(BF16), NT_DIMS, preferred_element_type=F32)
            refs[1 + n + s][...] = out.astype(out_dtypes[s])

    in_specs = [pl.BlockSpec((tm, K), lambda i: (i, 0))] + [pl.BlockSpec(b.shape, lambda i: (0, 0)) for b in bs]
    return pl.pallas_call(
        body, name=name, grid=(M // tm,), in_specs=in_specs,
        out_specs=[pl.BlockSpec((tm, b.shape[0]), lambda i: (i, 0)) for b in bs],
        out_shape=[jax.ShapeDtypeStruct((M, b.shape[0]), dt) for b, dt in zip(bs, out_dtypes)],
        compiler_params=_cparams(("parallel",)))(a, *bs)


MATMUL_VMEM_BUDGET = 36 * 1024 * 1024


def _matmul_tiles(M, N, K, a_bytes, b_bytes, o_bytes, has_add, m_on_lanes=False):
    tk = K if K <= 1536 else _pick(K, (1408, 1024, 768, 704, 512, 256))
    nk = K // tk
    m_cands = [t for t in (1024, 768, 512, 384, 256, 128) if M % t == 0] or [M]
    if M % 768 and M % 1024:
        m_cands += [t for t in (1408, 704, 352) if M % t == 0 and not (m_on_lanes and t % LANES)]
    n_cands = [t for t in range(LANES, min(N, 2816) + 1, LANES) if N % t == 0] or [N]
    best = None
    for tm in m_cands:
        for tn in n_cands:
            pipeline = 2 * (tm * tk * a_bytes + tk * tn * b_bytes + tm * tn * o_bytes) + (2 * tm * tn * 4 if has_add else 0)
            temps = tm * tn * 4 * (2 if nk > 1 else 1) + (tm * tk * 2 if a_bytes == 4 else 0) + (tk * tn * 2 if b_bytes == 4 else 0)
            if pipeline + temps <= MATMUL_VMEM_BUDGET:
                score = (tm * tn, tn)
                if best is None or score > best[0]:
                    best = (score, tm, tn)
    if best is None:
        return m_cands[-1], n_cands[0], tk
    return best[1], best[2], tk


def _row_specs(descs, arrays, tm, nct, heads):
    specs = []
    for d, arr in zip(descs, arrays):
        if d[0] == "row":
            _, w, per_head, off, _ = d
            specs.append(pl.BlockSpec((tm, w * (heads if per_head else 1)), lambda i, off=off: (i, off)))
        elif d[0] == "par":
            specs.append(pl.BlockSpec(arr.shape, lambda i, nd=arr.ndim: (0,) * nd))
        else:
            specs.append(pl.BlockSpec((1,) + arr.shape[1:], lambda i, nd=arr.ndim: (jnp.where(i >= nct, 1, 0),) + (0,) * (nd - 1)))
    return specs


def _load(d, ref, h):
    if d[0] == "grp":
        return ref[0]
    if d[0] == "row" and d[2]:
        return ref[:, h * d[1]:(h + 1) * d[1]].astype(F32)
    return ref[...].astype(F32) if d[0] == "row" else ref[...]


def _out_specs(outs, tm, heads):
    return [pl.BlockSpec((tm, w * (heads if ph else 1)), lambda i: (i, 0)) for (w, ph, _) in outs]


def rowop_fwd(name, fn, descs, arrays, outs, T, n_ctx, heads=1, tm=ROW_TILE):
    nct = n_ctx // tm
    n_in = len(descs)

    def body(*refs):
        for h in range(heads):
            res = fn(*[_load(d, r, h) for d, r in zip(descs, refs[:n_in])])
            for o_ref, r, (w, ph, _) in zip(refs[n_in:], res, outs):
                if ph:
                    o_ref[:, h * w:(h + 1) * w] = r.astype(o_ref.dtype)
                else:
                    o_ref[...] = r.astype(o_ref.dtype)

    out_shape = [jax.ShapeDtypeStruct((T, w * (heads if ph else 1)), dt) for (w, ph, dt) in outs]
    return pl.pallas_call(
        body, name=name, grid=(T // tm,), in_specs=_row_specs(descs, arrays, tm, nct, heads), out_specs=_out_specs(outs, tm, heads),
        out_shape=out_shape, compiler_params=_cparams(("parallel",)),
    )(*arrays)


def rowop_bwd(name, fn, descs, arrays, outs, cts, T, n_ctx, heads=1, tm=ROW_TILE, add=None, to_matmul=()):
    nct = n_ctx // tm
    n_in, n_ct = len(descs), len(cts)
    add = add or {}
    diff_idx = [k for k, d in enumerate(descs) if d[-1]]
    add_idx = [k for k in diff_idx if k in add]

    def body(*refs):
        in_refs, ct_refs = refs[:n_in], refs[n_in:n_in + n_ct]
        add_refs = dict(zip(add_idx, refs[n_in + n_ct:n_in + n_ct + len(add_idx)]))
        g_refs = refs[n_in + n_ct + len(add_idx):]
        i = pl.program_id(0)
        shared = {}
        for h in range(heads):
            vals = [_load(d, r, h) for d, r in zip(descs, in_refs)]

            def f(*dvals, vals=vals):
                full = list(vals)
                for k, v in zip(diff_idx, dvals):
                    full[k] = v
                return tuple(fn(*full))

            _, vjp = jax.vjp(f, *[vals[k] for k in diff_idx])
            cts_h = tuple(c[:, h * w:(h + 1) * w] if ph else c[...] for c, (w, ph, _) in zip(ct_refs, outs))
            for k, g_ref, g in zip(diff_idx, g_refs, vjp(cts_h)):
                d = descs[k]
                if d[0] == "row" and d[2]:
                    g_ref[:, h * d[1]:(h + 1) * d[1]] = g.astype(g_ref.dtype)
                else:
                    shared[k] = g if k not in shared else shared[k] + g
        for k, g_ref in zip(diff_idx, g_refs):
            d = descs[k]
            if k not in shared:
                continue
            g = shared[k]
            if d[0] == "row":
                if k in add_refs:
                    g = g + add_refs[k][...]
                g_ref[...] = g.astype(g_ref.dtype)
            elif d[0] == "par":
                _accumulate(g_ref, g, i == 0)
            else:
                _accumulate(g_ref, g[None], jnp.logical_or(i == 0, i == nct))

    in_specs = _row_specs(descs, arrays, tm, nct, heads)
    g_specs, g_shape = [], []
    for k in diff_idx:
        d = descs[k]
        if d[0] == "row":
            g_specs.append(pl.BlockSpec((tm, d[1] * (heads if d[2] else 1)), lambda i: (i, 0)))
            g_shape.append(jax.ShapeDtypeStruct((T, d[1] * (heads if d[2] else 1)), BF16 if k in to_matmul else F32))
        else:
            g_specs.append(in_specs[k])
            g_shape.append(jax.ShapeDtypeStruct(arrays[k].shape, F32))
    add_specs = [g_specs[diff_idx.index(k)] for k in add_idx]
    return pl.pallas_call(
        body, name=name, grid=(T // tm,), in_specs=in_specs + _out_specs(outs, tm, heads) + add_specs, out_specs=g_specs,
        out_shape=g_shape, compiler_params=_cparams(("arbitrary",)),
    )(*arrays, *cts, *[add[k] for k in add_idx])


def _accumulate(ref, val, first):
    @pl.when(first)
    def _():
        ref[...] = val.astype(ref.dtype)

    @pl.when(jnp.logical_not(first))
    def _():
        ref[...] += val.astype(ref.dtype)


def _rms(x, count=None):
    n = x.shape[-1] if count is None else count
    return x * lax.rsqrt(jnp.sum(x * x, axis=-1, keepdims=True) * (1.0 / n) + EPS)


def _swap_halves(x, nf):
    w = x.shape[-1]
    lane = lax.broadcasted_iota(jnp.int32, x.shape, x.ndim - 1)
    return jnp.where((lane % (2 * nf)) < nf, pltpu.roll(x, w - nf, x.ndim - 1), pltpu.roll(x, nf, x.ndim - 1))


def _make_rope(nf):
    @jax.custom_vjp
    def rope(x, c, s):
        return x * c + _swap_halves(x, nf) * s

    def fwd(x, c, s):
        return rope(x, c, s), (c, s)

    def bwd(res, g):
        c, s = res
        return g * c + _swap_halves(g * s, nf), jnp.zeros_like(c), jnp.zeros_like(s)

    rope.defvjp(fwd, bwd)
    return rope


_rope_swa = _make_rope(SWA_HEAD_DIM // 4)
_rope_mla = _make_rope(MLA_ROPE // 4)


@jax.custom_vjp
def _softplus(x):
    e = jnp.exp(-jnp.abs(x))
    u = 1.0 + e
    log1p_e = jnp.where(u == 1.0, e, jnp.log(u) * e / jnp.where(u == 1.0, 1.0, u - 1.0))
    return jnp.maximum(x, 0.0) + log1p_e


_softplus.defvjp(lambda x: (_softplus(x), x), lambda x, g: (g * jax.nn.sigmoid(x),))


def fn_norm_mod(x, g, shift, scale):
    return (_rms(x) * g * (1.0 + scale) + shift,)


def fn_rms(x, g):
    return (_rms(x) * g,)


def fn_resid(x, a, gate):
    return (x + gate * a,)


def fn_softplus(dt, bias):
    return (_softplus(dt + bias),)


def fn_ssd_out(yf, yb, xs, z, d_lane, g):
    y = yf + yb + d_lane * xs
    return (_rms(y * (z * jax.nn.sigmoid(z))) * g,)


def fn_swa_q(q, g, c, s):
    return (_rope_swa(_rms(q) * g, c, s),)


def fn_swa_kv(k, v, g, c, s):
    return (_rope_swa(_rms(k) * g, c, s), v)


def fn_mla_q(qn, qr, gn, gr, c, s):
    return (jnp.concatenate([_rms(qn) * gn, _rope_mla(_rms(qr, MLA_ROPE) * gr, c, s)], axis=-1),)


def fn_mla_kv(kn, v, kr, gn, gr, c, s):
    return (jnp.concatenate([_rms(kn) * gn, _rope_mla(_rms(kr, MLA_ROPE) * gr, c, s)], axis=-1), v)


def fn_merge(g1, g2, g3, p1, p2, p3):
    return (jax.nn.sigmoid(g1) * p1 + jax.nn.sigmoid(g2) * p2 + jax.nn.sigmoid(g3) * p3,)


def fn_swiglu(g, u):
    return (g * jax.nn.sigmoid(g) * u,)


ATTN_TILE = 256
NT_DIMS = (((1,), (1,)), ((), ()))


class AttnCfg:
    def __init__(self, hq, group, dq, dv, scale, window, has_sink, L, T, chunk, kv_block):
        self.hq, self.group, self.dq, self.dv, self.scale = hq, group, dq, dv, scale
        self.window, self.has_sink, self.L, self.T = window, has_sink, L, T
        self.chunk = _pick(L, (chunk, ATTN_TILE))
        self.ctx_chunk = T - L
        self.kv_block = kv_block
        self.q_block = kv_block * group
        assert L % ATTN_TILE == 0 and (T - L) % ATTN_TILE == 0 and L % self.chunk == 0
        assert (hq // group) % kv_block == 0
        if window is not None:
            assert (ATTN_TILE + 2 * window) % self.chunk == 0
            self.window_chunks = min((ATTN_TILE + 2 * window) // self.chunk, L // self.chunk)
            self.align = math.gcd(self.chunk, window)
        else:
            self.align = self.chunk


LOG2E = math.log2(math.e)


def _latent_chunks(cfg, r0):
    c = cfg.chunk
    if cfg.window is None:
        lo, n = 0, cfg.L // c
    else:
        n = cfg.window_chunks
        lo = jnp.clip(r0 - cfg.window, 0, cfg.L - n * c)
    return lo, n


def _visible(cfg, rows_q, rows_k):
    return jnp.logical_or(rows_k >= cfg.L, jnp.abs(rows_k - rows_q) <= cfg.window)


def flash_fwd(name, cfg, q, k, v, sink):
    T, tq, c = cfg.T, ATTN_TILE, cfg.chunk
    hq, g, dq, dv, hb, kb = cfg.hq, cfg.group, cfg.dq, cfg.dv, cfg.q_block, cfg.kv_block
    to_log2 = cfg.scale * LOG2E

    def body(*refs):
        if cfg.has_sink:
            q_ref, k_ref, v_ref, sink_ref, o_ref, lse_ref = refs
        else:
            q_ref, k_ref, v_ref, o_ref, lse_ref = refs
        q0 = pl.program_id(1) * tq
        qs = [q_ref[:, hh * dq:(hh + 1) * dq] for hh in range(hb)]
        lat_lo, lat_n = _latent_chunks(cfg, q0)
        n = jnp.where(q0 >= cfg.L, 0, lat_n)
        rows_q = q0 + lax.broadcasted_iota(jnp.int32, (tq, 1), 0)

        def start(t):
            return pl.multiple_of(lat_lo + jnp.minimum(t, lat_n - 1) * c, cfg.align)

        def logits(ks, size):
            return tuple(lax.dot_general(qs[hh], k_ref[pl.ds(ks, size), (hh // g) * dq:(hh // g + 1) * dq], NT_DIMS,
                                         preferred_element_type=F32) for hh in range(hb))

        def update(state, s_all, ks, size, masked):
            new_state = []
            for hh in range(hb):
                m, acc = state[hh]
                s = s_all[hh]
                if masked:
                    rows_k = ks + lax.broadcasted_iota(jnp.int32, (1, size), 1)
                    s = jnp.where(_visible(cfg, rows_q, rows_k), s, -jnp.inf)
                m_new = jnp.maximum(m, jnp.max(s, axis=-1, keepdims=True) * to_log2)
                alpha = jnp.exp2(m - m_new)
                p = jnp.exp2(s * to_log2 - m_new).astype(BF16)
                kh = hh // g
                v_ones = jnp.concatenate([v_ref[pl.ds(ks, size), kh * dv:(kh + 1) * dv], jnp.ones((size, dv), BF16)], axis=1)
                acc = alpha * acc + jnp.dot(p, v_ones, preferred_element_type=F32)
                new_state.append((m_new, acc))
            return tuple(new_state)

        def step(t, carry):
            state, s_all = carry
            s_next = logits(start(t + 1), c)
            return update(state, s_all, start(t), c, cfg.window is not None), s_next

        state = []
        for hh in range(hb):
            if cfg.has_sink:
                m0 = jnp.zeros((tq, 1), F32) + sink_ref[hh] * LOG2E
                l0 = jnp.ones((tq, dv), F32)
            else:
                m0 = jnp.full((tq, 1), -jnp.inf, F32)
                l0 = jnp.zeros((tq, dv), F32)
            state.append((m0, jnp.concatenate([jnp.zeros((tq, dv), F32), l0], axis=1)))
        state = update(tuple(state), logits(cfg.L, cfg.ctx_chunk), cfg.L, cfg.ctx_chunk, False)
        state, _ = lax.fori_loop(0, n, step, (state, logits(start(0), c)))
        for hh in range(hb):
            m, acc = state[hh]
            o_ref[:, hh * dv:(hh + 1) * dv] = acc[:, :dv] / acc[:, dv:]
            lse_ref[hh] = m + jnp.log2(acc[:, dv:dv + 1])

    in_specs = [pl.BlockSpec((tq, hb * dq), lambda h, i: (i, h)),
                pl.BlockSpec((T, kb * dq), lambda h, i: (0, h)),
                pl.BlockSpec((T, kb * dv), lambda h, i: (0, h))]
    ins = [q, k, v]
    if cfg.has_sink:
        in_specs.append(pl.BlockSpec((hb, 1, 1), lambda h, i: (h, 0, 0)))
        ins.append(sink)
    return pl.pallas_call(
        body, name=name, grid=(hq // hb, T // tq), in_specs=in_specs,
        out_specs=[pl.BlockSpec((tq, hb * dv), lambda h, i: (i, h)), pl.BlockSpec((hb, tq, 1), lambda h, i: (h, i, 0))],
        out_shape=[jax.ShapeDtypeStruct((T, hq * dv), F32), jax.ShapeDtypeStruct((hq, T, 1), F32)],
        compiler_params=_cparams(("parallel", "parallel")),
    )(*ins)


def flash_bwd_fused(name, cfg, q, k, v, o, do, lse, sink):
    T, L, tq, c, cc = cfg.T, cfg.L, ATTN_TILE, cfg.chunk, cfg.ctx_chunk
    hq, g, dq, dv = cfg.hq, cfg.group, cfg.dq, cfg.dv
    hk = hq // g
    nq = T // tq
    to_log2 = cfg.scale * LOG2E
    masked = cfg.window is not None

    def body(*refs):
        if cfg.has_sink:
            q_ref, k_ref, v_ref, o_ref, do_ref, lse_ref, sink_ref, dq_ref, dk_ref, dv_ref, dsink_ref = refs
        else:
            q_ref, k_ref, v_ref, o_ref, do_ref, lse_ref, dq_ref, dk_ref, dv_ref = refs
        i = pl.program_id(1)
        q0 = i * tq

        @pl.when(i == 0)
        def _():
            dk_ref[...] = jnp.zeros_like(dk_ref)
            dv_ref[...] = jnp.zeros_like(dv_ref)

        qs = [q_ref[:, hh * dq:(hh + 1) * dq] for hh in range(g)]
        dos = [do_ref[:, hh * dv:(hh + 1) * dv].astype(BF16) for hh in range(g)]
        lses = [lse_ref[hh] for hh in range(g)]
        deltas = [jnp.sum(do_ref[:, hh * dv:(hh + 1) * dv] * o_ref[:, hh * dv:(hh + 1) * dv], axis=-1, keepdims=True)
                  for hh in range(g)]
        if cfg.has_sink:
            parts = [-jnp.sum(jnp.exp2(sink_ref[hh] * LOG2E - lses[hh]) * deltas[hh], axis=0, keepdims=True)[None]
                     for hh in range(g)]
            _accumulate(dsink_ref, jnp.concatenate(parts, axis=0), i == 0)
        lat_lo, lat_n = _latent_chunks(cfg, q0)
        n = jnp.where(q0 >= L, 0, lat_n)
        rows_q = q0 + lax.broadcasted_iota(jnp.int32, (tq, 1), 0)

        def start(t):
            return pl.multiple_of(lat_lo + jnp.minimum(t, lat_n - 1) * c, cfg.align)

        def products(ks, size):
            kk, vv = k_ref[pl.ds(ks, size), :], v_ref[pl.ds(ks, size), :]
            return tuple((lax.dot_general(qs[hh], kk, NT_DIMS, preferred_element_type=F32),
                          lax.dot_general(dos[hh], vv, NT_DIMS, preferred_element_type=F32)) for hh in range(g))

        def update(accs, prods, ks, size, mask_it):
            new_accs, dv_part, dk_part = [], None, None
            for hh in range(g):
                s, dp = prods[hh]
                p = jnp.exp2(s * to_log2 - lses[hh])
                if mask_it:
                    rows_k = ks + lax.broadcasted_iota(jnp.int32, (1, size), 1)
                    p = jnp.where(_visible(cfg, rows_q, rows_k), p, 0.0)
                ds = (p * (dp - deltas[hh])).astype(BF16)
                dv_h = lax.dot_general(p.astype(BF16), dos[hh], TN_DIMS, preferred_element_type=F32)
                dk_h = lax.dot_general(ds, qs[hh], TN_DIMS, preferred_element_type=F32)
                dv_part = dv_h if dv_part is None else dv_part + dv_h
                dk_part = dk_h if dk_part is None else dk_part + dk_h
                new_accs.append(accs[hh] + jnp.dot(ds, k_ref[pl.ds(ks, size), :], preferred_element_type=F32))
            dv_ref[pl.ds(ks, size), :] += dv_part
            dk_ref[pl.ds(ks, size), :] += dk_part
            return tuple(new_accs)

        def step(t, carry):
            accs, prods = carry
            nxt = products(start(t + 1), c)
            return update(accs, prods, start(t), c, masked), nxt

        accs = update(tuple(jnp.zeros((tq, dq), F32) for _ in range(g)), products(L, cc), L, cc, False)
        accs, _ = lax.fori_loop(0, n, step, (accs, products(start(0), c)))
        for hh in range(g):
            dq_ref[:, hh * dq:(hh + 1) * dq] = accs[hh] * cfg.scale

        @pl.when(i == nq - 1)
        def _():
            dk_ref[...] = dk_ref[...] * cfg.scale

    head_tile = pl.BlockSpec((tq, g * dv), lambda h, i: (i, h))
    one = pl.BlockSpec((g, 1, 1), lambda h, i: (h, 0, 0))
    in_specs = [pl.BlockSpec((tq, g * dq), lambda h, i: (i, h)),
                pl.BlockSpec((T, dq), lambda h, i: (0, h)),
                pl.BlockSpec((T, dv), lambda h, i: (0, h)),
                head_tile, head_tile, pl.BlockSpec((g, tq, 1), lambda h, i: (h, i, 0))]
    ins = [q, k, v, o, do, lse]
    out_specs = [pl.BlockSpec((tq, g * dq), lambda h, i: (i, h)),
                 pl.BlockSpec((T, dq), lambda h, i: (0, h)),
                 pl.BlockSpec((T, dv), lambda h, i: (0, h))]
    out_shape = [jax.ShapeDtypeStruct((T, hq * dq), F32), jax.ShapeDtypeStruct((T, hk * dq), F32),
                 jax.ShapeDtypeStruct((T, hk * dv), F32)]
    if cfg.has_sink:
        in_specs.append(one)
        ins.append(sink)
        out_specs.append(one)
        out_shape.append(jax.ShapeDtypeStruct((hq, 1, 1), F32))
    return pl.pallas_call(body, name=name, grid=(hk, nq), in_specs=in_specs, out_specs=out_specs, out_shape=out_shape,
                          compiler_params=_cparams(("arbitrary", "arbitrary")))(*ins)


HALO = 8


def _conv_specs(tm, C, T):
    nb = tm // HALO
    last = T // HALO - 1
    return [pl.BlockSpec((HALO, C), lambda i: (jnp.maximum(i * nb - 1, 0), 0)),
            pl.BlockSpec((tm, C), lambda i: (i, 0)),
            pl.BlockSpec((HALO, C), lambda i: (jnp.minimum((i + 1) * nb, last), 0))]


def _extended(prev_ref, cur_ref, next_ref, i, tm, L, T):
    r0 = i * tm
    keep_prev = jnp.logical_and(r0 != 0, r0 != L).astype(F32)
    keep_next = jnp.logical_and(r0 + tm != L, r0 + tm != T).astype(F32)
    return jnp.concatenate([prev_ref[...] * keep_prev, cur_ref[...], next_ref[...] * keep_next], axis=0)


def _shift_rows(xe, d):
    n = xe.shape[0]
    return xe if d == 0 else pltpu.roll(xe, (-d) % n, 0)


def _conv_pre(xe, w_ref, b_ref):
    acc = b_ref[...] + w_ref[SSM_CONV // 2:SSM_CONV // 2 + 1, :] * xe
    for k in range(SSM_CONV):
        if k != SSM_CONV // 2:
            acc = acc + w_ref[k:k + 1, :] * _shift_rows(xe, k - SSM_CONV // 2)
    return acc


def conv_fwd(name, x, w, b, L, tm=ROW_TILE):
    T, C = x.shape

    def body(xp, xc, xn, w_ref, b_ref, o_ref):
        xe = _extended(xp, xc, xn, pl.program_id(0), tm, L, T)
        pre = _conv_pre(xe, w_ref, b_ref)[HALO:HALO + tm]
        o_ref[...] = pre * jax.nn.sigmoid(pre)

    full = lambda a: pl.BlockSpec(a.shape, lambda i: (0, 0))
    return pl.pallas_call(body, name=name, grid=(T // tm,), in_specs=_conv_specs(tm, C, T) + [full(w), full(b)],
                          out_specs=pl.BlockSpec((tm, C), lambda i: (i, 0)), out_shape=jax.ShapeDtypeStruct((T, C), F32),
                          compiler_params=_cparams(("parallel",)))(x, x, x, w, b)


def conv_bwd(name, x, w, b, gu, L, tm=ROW_TILE):
    T, C = x.shape

    def body(xp, xc, xn, gp, gc, gn, w_ref, b_ref, dx_ref, dw_ref, db_ref):
        i = pl.program_id(0)
        xe = _extended(xp, xc, xn, i, tm, L, T)
        ge = _extended(gp, gc, gn, i, tm, L, T)
        pre = _conv_pre(xe, w_ref, b_ref)
        sg = jax.nn.sigmoid(pre)
        gpre = ge * (sg * (1.0 + pre * (1.0 - sg)))
        half = SSM_CONV // 2
        dx = jnp.zeros((tm, C), F32)
        rows = []
        for k in range(SSM_CONV):
            dx = dx + w_ref[k:k + 1, :] * _shift_rows(gpre, half - k)[HALO:HALO + tm]
            rows.append(jnp.sum(gpre[HALO:HALO + tm] * _shift_rows(xe, k - half)[HALO:HALO + tm], axis=0, keepdims=True))
        dx_ref[...] = dx.astype(dx_ref.dtype)
        rows += [jnp.zeros((1, C), F32)] * (8 - SSM_CONV)
        _accumulate(dw_ref, jnp.concatenate(rows, axis=0), i == 0)
        _accumulate(db_ref, jnp.sum(gpre[HALO:HALO + tm], axis=0, keepdims=True), i == 0)

    full = lambda a: pl.BlockSpec(a.shape, lambda i: (0, 0))
    return pl.pallas_call(
        body, name=name, grid=(T // tm,), in_specs=_conv_specs(tm, C, T) * 2 + [full(w), full(b)],
        out_specs=[pl.BlockSpec((tm, C), lambda i: (i, 0)), pl.BlockSpec((8, C), lambda i: (0, 0)), pl.BlockSpec((1, C), lambda i: (0, 0))],
        out_shape=[jax.ShapeDtypeStruct((T, C), BF16), jax.ShapeDtypeStruct((8, C), F32), jax.ShapeDtypeStruct((1, C), F32)],
        compiler_params=_cparams(("arbitrary",)))(x, x, x, gu, gu, gu, w, b)


SSM_PAIRS = SSM_HEADS // 2
TN_DIMS = (((0,), (0,)), ((), ()))
HIGHEST = lax.Precision.HIGHEST


def _ssd_chunk(direction, xps, bs, cs, dt_col, dt_row, alog_row, alog_col, hps):
    Q = SSM_CHUNK
    da_col = dt_col * (-jnp.exp(alog_row))
    da_row = dt_row * (-jnp.exp(alog_col))
    ii = lax.broadcasted_iota(jnp.int32, (Q, Q), 0)
    jj = lax.broadcasted_iota(jnp.int32, (Q, Q), 1)
    tri = (ii >= jj) if direction == 0 else (ii <= jj)
    trif = tri.astype(F32)
    acs_col = jnp.dot(trif, da_col, precision=HIGHEST, preferred_element_type=F32)
    acs_row = lax.dot_general(da_row, trif, NT_DIMS, precision=HIGHEST, preferred_element_type=F32)
    tot_col = jnp.sum(da_col, axis=0, keepdims=True)
    lane16 = lax.broadcasted_iota(jnp.int32, (1, SSM_HEADS), 1)
    sub16 = lax.broadcasted_iota(jnp.int32, (SSM_HEADS, 1), 0)
    low = lax.broadcasted_iota(jnp.int32, (1, 2 * SSM_HEAD_DIM), 1) < SSM_HEAD_DIM

    def col(v, h):
        return jnp.sum(v * (lane16 == h).astype(F32), axis=1, keepdims=True)

    def row(v, h):
        return jnp.sum(v * (sub16 == h).astype(F32), axis=0, keepdims=True)

    ys, hos = [], []
    pairs_per_group = SSM_PAIRS // SSM_GROUPS
    for g in range(SSM_GROUPS):
        bb, cb16 = bs[g].astype(BF16), cs[g].astype(BF16)
        cb = lax.dot_general(cb16, bb, NT_DIMS, preferred_element_type=F32)
        for pp in range(pairs_per_group):
            p = g * pairs_per_group + pp
            h0, h1 = 2 * p, 2 * p + 1
            ac0, ac1 = col(acs_col, h0), col(acs_col, h1)
            seg0 = jnp.exp(jnp.where(tri, ac0 - row(acs_row, h0), -jnp.inf))
            seg1 = jnp.exp(jnp.where(tri, ac1 - row(acs_row, h1), -jnp.inf))
            dt_l = jnp.where(low, col(dt_col, h0), col(dt_col, h1))
            ac_l = jnp.where(low, ac0, ac1)
            tot_l = jnp.where(low, col(tot_col, h0), col(tot_col, h1))
            xdt = xps[p] * dt_l
            y = (jnp.dot((cb * seg0).astype(BF16), jnp.where(low, xdt, 0.0).astype(BF16), preferred_element_type=F32)
                 + jnp.dot((cb * seg1).astype(BF16), jnp.where(low, 0.0, xdt).astype(BF16), preferred_element_type=F32))
            y = y + jnp.dot(cb16, hps[p].astype(BF16), preferred_element_type=F32) * jnp.exp(ac_l)
            st = lax.dot_general(bb, (xdt * jnp.exp(tot_l - ac_l)).astype(BF16), TN_DIMS, preferred_element_type=F32)
            ys.append(y)
            hos.append(hps[p] * jnp.exp(tot_l) + st)
    return tuple(ys), tuple(hos)


def _ssd_chunk_of(direction, step, ncl, ncc):
    if direction == 0:
        return jnp.where(step < ncc, ncl + step, step - ncc)
    return jnp.where(step < ncc, ncl + ncc - 1 - step, ncl - 1 - (step - ncc))


def _ssd_load(u_ref):
    Q = SSM_CHUNK
    xps = tuple(u_ref[:, LANES * p:LANES * (p + 1)] for p in range(SSM_PAIRS))
    bs = tuple(u_ref[:, SSM_INNER + SSM_STATE * g:SSM_INNER + SSM_STATE * (g + 1)] for g in range(SSM_GROUPS))
    c0 = SSM_INNER + SSM_GROUPS * SSM_STATE
    cs = tuple(u_ref[:, c0 + SSM_STATE * g:c0 + SSM_STATE * (g + 1)] for g in range(SSM_GROUPS))
    return xps, bs, cs


def ssd_fwd(name, direction, u, dt, dt_t, alog_row, alog_col, L):
    T = u.shape[0]
    Q, N = SSM_CHUNK, SSM_STATE
    ncl, ncc = L // Q, (T - L) // Q
    nc = ncl + ncc
    cm = lambda s: _ssd_chunk_of(direction, s, ncl, ncc)

    def body(u_ref, dt_ref, dtt_ref, ar_ref, ac_ref, y_ref, hin_ref, state):
        @pl.when(pl.program_id(0) == 0)
        def _():
            state[...] = jnp.zeros_like(state)

        xps, bs, cs = _ssd_load(u_ref)
        hps = tuple(state[p] for p in range(SSM_PAIRS))
        for p in range(SSM_PAIRS):
            hin_ref[0, p] = hps[p]
        ys, hos = _ssd_chunk(direction, xps, bs, cs, dt_ref[...], dtt_ref[...], ar_ref[...], ac_ref[...], hps)
        for p in range(SSM_PAIRS):
            y_ref[:, LANES * p:LANES * (p + 1)] = ys[p]
            state[p] = hos[p]

    return pl.pallas_call(
        body, name=name, grid=(nc,),
        in_specs=[pl.BlockSpec((Q, SSM_CONV_DIM), lambda s: (cm(s), 0)),
                  pl.BlockSpec((Q, SSM_HEADS), lambda s: (cm(s), 0)),
                  pl.BlockSpec((SSM_HEADS, Q), lambda s: (0, cm(s))),
                  pl.BlockSpec((1, SSM_HEADS), lambda s: (0, 0)),
                  pl.BlockSpec((SSM_HEADS, 1), lambda s: (0, 0))],
        out_specs=[pl.BlockSpec((Q, SSM_INNER), lambda s: (cm(s), 0)),
                   pl.BlockSpec((1, SSM_PAIRS, N, LANES), lambda s: (cm(s), 0, 0, 0))],
        out_shape=[jax.ShapeDtypeStruct((T, SSM_INNER), F32), jax.ShapeDtypeStruct((nc, SSM_PAIRS, N, LANES), F32)],
        scratch_shapes=[pltpu.VMEM((SSM_PAIRS, N, LANES), F32)],
        compiler_params=_cparams(("arbitrary",)),
    )(u, dt, dt_t, alog_row, alog_col)


def ssd_bwd(name, direction, u, dt, dt_t, alog_row, alog_col, hin, dy, L, add_x=None, add_u=None):
    T = u.shape[0]
    Q, N = SSM_CHUNK, SSM_STATE
    ncl, ncc = L // Q, (T - L) // Q
    nc = ncl + ncc
    cm = lambda s: _ssd_chunk_of(direction, nc - 1 - s, ncl, ncc)
    n_add = (add_x is not None) + (add_u is not None)

    def body(*refs):
        u_ref, dt_ref, dtt_ref, ar_ref, ac_ref, hin_ref, dy_ref = refs[:7]
        add_refs = refs[7:7 + n_add]
        du_ref, ddt_ref, ddtt_ref, dar_ref, dac_ref, dstate = refs[7 + n_add:]
        first = pl.program_id(0) == 0

        @pl.when(first)
        def _():
            dstate[...] = jnp.zeros_like(dstate)

        xps, bs, cs = _ssd_load(u_ref)
        hps = tuple(hin_ref[0, p] for p in range(SSM_PAIRS))
        _, vjp = jax.vjp(functools.partial(_ssd_chunk, direction), xps, bs, cs, dt_ref[...], dtt_ref[...], ar_ref[...],
                         ac_ref[...], hps)
        dys = tuple(dy_ref[:, LANES * p:LANES * (p + 1)] for p in range(SSM_PAIRS))
        dhs = tuple(dstate[p] for p in range(SSM_PAIRS))
        gx, gb, gc, gdt, gdtt, gar, gac, ghp = vjp((dys, dhs))
        parts = list(gx) + list(gb) + list(gc)
        du = jnp.concatenate(parts, axis=1)
        k = 0
        if add_x is not None:
            du = du + jnp.concatenate([add_refs[k][...], jnp.zeros((Q, SSM_CONV_DIM - SSM_INNER), F32)], axis=1)
            k += 1
        if add_u is not None:
            du = du + add_refs[k][...]
        du_ref[...] = du
        ddt_ref[...] = gdt
        ddtt_ref[...] = gdtt
        _accumulate(dar_ref, gar, first)
        _accumulate(dac_ref, gac, first)
        for p in range(SSM_PAIRS):
            dstate[p] = ghp[p]

    in_specs = [pl.BlockSpec((Q, SSM_CONV_DIM), lambda s: (cm(s), 0)),
                pl.BlockSpec((Q, SSM_HEADS), lambda s: (cm(s), 0)),
                pl.BlockSpec((SSM_HEADS, Q), lambda s: (0, cm(s))),
                pl.BlockSpec((1, SSM_HEADS), lambda s: (0, 0)),
                pl.BlockSpec((SSM_HEADS, 1), lambda s: (0, 0)),
                pl.BlockSpec((1, SSM_PAIRS, N, LANES), lambda s: (cm(s), 0, 0, 0)),
                pl.BlockSpec((Q, SSM_INNER), lambda s: (cm(s), 0))]
    ins = [u, dt, dt_t, alog_row, alog_col, hin, dy]
    if add_x is not None:
        in_specs.append(pl.BlockSpec((Q, SSM_INNER), lambda s: (cm(s), 0)))
        ins.append(add_x)
    if add_u is not None:
        in_specs.append(pl.BlockSpec((Q, SSM_CONV_DIM), lambda s: (cm(s), 0)))
        ins.append(add_u)
    return pl.pallas_call(
        body, name=name, grid=(nc,), in_specs=in_specs,
        out_specs=[pl.BlockSpec((Q, SSM_CONV_DIM), lambda s: (cm(s), 0)),
                   pl.BlockSpec((Q, SSM_HEADS), lambda s: (cm(s), 0)),
                   pl.BlockSpec((SSM_HEADS, Q), lambda s: (0, cm(s))),
                   pl.BlockSpec((1, SSM_HEADS), lambda s: (0, 0)),
                   pl.BlockSpec((SSM_HEADS, 1), lambda s: (0, 0))],
        out_shape=[jax.ShapeDtypeStruct((T, SSM_CONV_DIM), F32), jax.ShapeDtypeStruct((T, SSM_HEADS), F32),
                   jax.ShapeDtypeStruct((SSM_HEADS, T), F32), jax.ShapeDtypeStruct((1, SSM_HEADS), F32),
                   jax.ShapeDtypeStruct((SSM_HEADS, 1), F32)],
        scratch_shapes=[pltpu.VMEM((SSM_PAIRS, N, LANES), F32)],
        compiler_params=_cparams(("arbitrary",)),
    )(*ins)


PEER_MASKS = (1, 2, 4, 3, 5, 6, 7)
N_PEERS = len(PEER_MASKS)
MESH_IDS = pl.DeviceIdType.MESH


def _my_index():
    return lax.axis_index("x") * 4 + lax.axis_index("y") * 2 + lax.axis_index("c")


def _coords(idx):
    return (idx // 4, (idx // 2) % 2, idx % 2)


def all_gather_hbm(name, arrays):
    n = len(arrays)
    chip_masks = (4, 2, 6)

    def body(*refs):
        ins, outs = refs[:n], refs[n:2 * n]
        send_sems, recv_sems, local_sems = refs[2 * n:]
        me = _my_index()
        sibling = me ^ 1

        def copy(a, k, block, to, src=None):
            return pltpu.make_async_remote_copy(
                src_ref=outs[a].at[block] if src is None else src, dst_ref=outs[a].at[block],
                send_sem=send_sems.at[a * N_PEERS + k], recv_sem=recv_sems.at[a * N_PEERS + k],
                device_id=_coords(to), device_id_type=MESH_IDS)

        started, own = [], []
        for a in range(n):
            local = pltpu.make_async_copy(ins[a], outs[a].at[me], local_sems.at[a])
            local.start()
            own.append(local)
            first = [copy(a, 0, me, sibling, src=ins[a])] + [copy(a, 1 + j, me, me ^ m, src=ins[a]) for j, m in enumerate(chip_masks)]
            for cp in first:
                cp.start()
            started += first
        for a in range(n):
            for j, m in enumerate(chip_masks):
                copy(a, 1 + j, me ^ m, me).wait_recv()
                fwd = copy(a, 4 + j, me ^ m, sibling)
                fwd.start()
                started.append(fwd)
        for a in range(n):
            copy(a, 0, sibling, me).wait_recv()
            for j, m in enumerate(chip_masks):
                copy(a, 4 + j, sibling ^ m, me).wait_recv()
        for cp in started:
            cp.wait_send()
        for cp in own:
            cp.wait()

    any_spec = pl.BlockSpec(memory_space=pl.ANY)
    return pl.pallas_call(
        body, name=name, in_specs=[any_spec] * n, out_specs=[any_spec] * n,
        out_shape=[jax.ShapeDtypeStruct((N_DEV,) + a.shape, a.dtype) for a in arrays],
        scratch_shapes=[pltpu.SemaphoreType.DMA((n * N_PEERS,)), pltpu.SemaphoreType.DMA((n * N_PEERS,)),
                        pltpu.SemaphoreType.DMA((n,))],
    )(*arrays)


N_CHIPS = N_DEV // 2


def exchange_sibling(name, arrays):
    n = len(arrays)

    def body(*refs):
        ins, outs = refs[:n], refs[n:2 * n]
        send_sems, recv_sems = refs[2 * n:]
        me = _my_index()
        other_core = 1 - me % 2
        copies = []
        for a in range(n):
            for chip in range(N_CHIPS):
                cp = pltpu.make_async_remote_copy(src_ref=ins[a].at[chip, other_core], dst_ref=outs[a].at[chip],
                                                  send_sem=send_sems.at[a * N_CHIPS + chip], recv_sem=recv_sems.at[a * N_CHIPS + chip],
                                                  device_id=_coords(me ^ 1), device_id_type=MESH_IDS)
                cp.start()
                copies.append(cp)
        for cp in copies:
            cp.wait()

    any_spec = pl.BlockSpec(memory_space=pl.ANY)
    return pl.pallas_call(
        body, name=name, in_specs=[any_spec] * n, out_specs=[any_spec] * n,
        out_shape=[jax.ShapeDtypeStruct((N_CHIPS,) + a.shape[2:], a.dtype) for a in arrays],
        scratch_shapes=[pltpu.SemaphoreType.DMA((n * N_CHIPS,)), pltpu.SemaphoreType.DMA((n * N_CHIPS,))],
    )(*arrays)


def exchange_chips(name, arrays):
    n = len(arrays)
    n_other = N_CHIPS - 1

    def body(*refs):
        ins, outs = refs[:n], refs[n:2 * n]
        send_sems, recv_sems, local_sems = refs[2 * n:]
        me = _my_index()
        chip, core = me // 2, me % 2
        copies = []
        for a in range(n):
            local = pltpu.make_async_copy(ins[a].at[chip], outs[a].at[chip], local_sems.at[a])
            local.start()
            copies.append(local)
            for k in range(n_other):
                peer_chip = chip ^ (k + 1)
                cp = pltpu.make_async_remote_copy(src_ref=ins[a].at[peer_chip], dst_ref=outs[a].at[chip],
                                                  send_sem=send_sems.at[a * n_other + k], recv_sem=recv_sems.at[a * n_other + k],
                                                  device_id=_coords(peer_chip * 2 + core), device_id_type=MESH_IDS)
                cp.start()
                copies.append(cp)
        for cp in copies:
            cp.wait()

    any_spec = pl.BlockSpec(memory_space=pl.ANY)
    return pl.pallas_call(
        body, name=name, in_specs=[any_spec] * n, out_specs=[any_spec] * n,
        out_shape=[jax.ShapeDtypeStruct(a.shape, a.dtype) for a in arrays],
        scratch_shapes=[pltpu.SemaphoreType.DMA((n * n_other,)), pltpu.SemaphoreType.DMA((n * n_other,)),
                        pltpu.SemaphoreType.DMA((n,))],
    )(*arrays)


def pair_sum(name, both, got):
    P, _, R, C = both.shape
    tr = _row_tile(R, C, 4)

    def body(a_ref, b_ref, o_ref):
        core = lax.axis_index("c")
        o_ref[...] = (a_ref[0, core].astype(F32) + b_ref[0].astype(F32)).astype(BF16)[None]

    return pl.pallas_call(
        body, name=name, grid=(P, R // tr),
        in_specs=[pl.BlockSpec((1, 2, tr, C), lambda p, i: (p, 0, i, 0)), pl.BlockSpec((1, tr, C), lambda p, i: (p, i, 0))],
        out_specs=pl.BlockSpec((1, tr, C), lambda p, i: (p, i, 0)),
        out_shape=jax.ShapeDtypeStruct((P, R, C), BF16), compiler_params=_cparams(("parallel", "parallel")))(both, got)


def all_gather_vmem(name, v):
    def body(v_ref, out_ref, send_sems, recv_sems):
        me = _my_index()
        out_ref[me] = v_ref[...]
        copies = []
        for k, mask in enumerate(PEER_MASKS):
            cp = pltpu.make_async_remote_copy(src_ref=v_ref, dst_ref=out_ref.at[me], send_sem=send_sems.at[k],
                                              recv_sem=recv_sems.at[k], device_id=_coords(me ^ mask), device_id_type=MESH_IDS)
            cp.start()
            copies.append(cp)
        for cp in copies:
            cp.wait()

    vm = pl.BlockSpec(memory_space=pltpu.VMEM)
    return pl.pallas_call(
        body, name=name, in_specs=[vm], out_specs=vm, out_shape=jax.ShapeDtypeStruct((N_DEV,) + v.shape, v.dtype),
        scratch_shapes=[pltpu.SemaphoreType.DMA((N_PEERS,)), pltpu.SemaphoreType.DMA((N_PEERS,))],
    )(v)


ROW_KERNEL_VMEM_BUDGET = 24 * 1024 * 1024


def _row_tile(rows, cols, bufs):
    budget = ROW_KERNEL_VMEM_BUDGET // (bufs * 2 * 4 * max(cols, LANES))
    if rows <= budget:
        return rows
    for t in range(budget - budget % 16, 15, -16):
        if rows % t == 0:
            return t
    return rows


def sum_parts(name, parts):
    P, R, C = parts.shape
    tr = _row_tile(R, C, P + 1)

    def body(p_ref, o_ref):
        acc = p_ref[0].astype(F32)
        for s in range(1, P):
            acc = acc + p_ref[s].astype(F32)
        o_ref[...] = acc

    return pl.pallas_call(body, name=name, grid=(R // tr,), in_specs=[pl.BlockSpec((P, tr, C), lambda i: (0, i, 0))],
                          out_specs=pl.BlockSpec((tr, C), lambda i: (i, 0)), out_shape=jax.ShapeDtypeStruct((R, C), F32),
                          compiler_params=_cparams(("parallel",)))(parts)


def adamw(name, w, g, m, v):
    R, C = w.shape
    tr = _row_tile(R, C, 7)

    def body(w_ref, g_ref, m_ref, v_ref, d_ref, nm_ref, nv_ref):
        g = g_ref[...]
        nm = ADAM_B1 * m_ref[...] + (1.0 - ADAM_B1) * g
        nv = ADAM_B2 * v_ref[...] + (1.0 - ADAM_B2) * (g * g)
        m_hat = nm / (1.0 - ADAM_B1 ** ADAM_STEP)
        v_hat = nv / (1.0 - ADAM_B2 ** ADAM_STEP)
        d_ref[...] = -ADAM_LR * (m_hat / (jnp.sqrt(v_hat) + ADAM_EPS) + ADAM_WD * w_ref[...])
        nm_ref[...] = nm
        nv_ref[...] = nv

    spec = pl.BlockSpec((tr, C), lambda i: (i, 0))
    return pl.pallas_call(body, name=name, grid=(R // tr,), in_specs=[spec] * 4, out_specs=[spec] * 3,
                          out_shape=[jax.ShapeDtypeStruct((R, C), F32)] * 3, compiler_params=_cparams(("parallel",)))(w, g, m, v)


def loss_and_grad(name, x, target, L, tm=ROW_TILE):
    T, D = x.shape
    nlt = L // tm

    def body(x_ref, t_ref, loss_ref, dx_ref):
        i = pl.program_id(0)
        err = jnp.where(i < nlt, x_ref[...] - t_ref[...], 0.0)
        dx_ref[...] = err * (1.0 / D)
        part = 0.5 * jnp.sum(jnp.sum(err * err, axis=1, keepdims=True), axis=0, keepdims=True) * (1.0 / D)
        _accumulate(loss_ref, part, i == 0)

    return pl.pallas_call(
        body, name=name, grid=(T // tm,),
        in_specs=[pl.BlockSpec((tm, D), lambda i: (i, 0)), pl.BlockSpec((tm, D), lambda i: (jnp.minimum(i, nlt - 1), 0))],
        out_specs=[pl.BlockSpec((1, 1), lambda i: (0, 0)), pl.BlockSpec((tm, D), lambda i: (i, 0))],
        out_shape=[jax.ShapeDtypeStruct((1, 1), F32), jax.ShapeDtypeStruct((T, D), F32)],
        compiler_params=_cparams(("arbitrary",)))(x, target)


def small_fwd(name, fn, arrays, out_shapes):
    def body(*refs):
        res = fn(*[r[...] for r in refs[:len(arrays)]])
        for o_ref, r in zip(refs[len(arrays):], res):
            o_ref[...] = r

    return pl.pallas_call(body, name=name, out_shape=[jax.ShapeDtypeStruct(s, F32) for s in out_shapes])(*arrays)


def small_bwd(name, fn, arrays, cts):
    n = len(arrays)

    def body(*refs):
        _, vjp = jax.vjp(lambda *a: tuple(fn(*a)), *[r[...] for r in refs[:n]])
        grads = vjp(tuple(r[...] for r in refs[n:n + len(cts)]))
        for o_ref, g in zip(refs[n + len(cts):], grads):
            o_ref[...] = g

    return pl.pallas_call(body, name=name, out_shape=[jax.ShapeDtypeStruct(a.shape, F32) for a in arrays])(*arrays, *cts)


def fn_silu(x):
    return (x * jax.nn.sigmoid(x),)


FWD_NAMES = ["x", "c", "ctx", "c_ctx", "w_mod", "b_mod", "norm1_g", "norm2_g", "w_in", "ssm_conv_w", "ssm_conv_b",
             "ssm_dt_bias", "ssm_a_log", "ssm_d", "ssm_norm_g", "swa_q_norm_g", "swa_k_norm_g", "swa_sink", "mla_q_lat_g",
             "mla_kv_lat_g", "w_mla_uq", "w_mla_ukv", "mla_q_norm_g", "mla_k_norm_g", "w_p_ssm", "w_p_swa", "w_p_mla",
             "w_out", "w_ffn_in", "w_ffn_out"]
WEIGHT_NAMES = FWD_NAMES[3:]
GATHERED = ["w_in", "w_mla_uq", "w_mla_ukv", "w_p_ssm", "w_p_swa", "w_p_mla", "w_out", "w_ffn_in", "w_ffn_out"]
COLUMN_SHARDED = ("w_in", "w_mla_uq", "w_mla_ukv", "w_ffn_in")
REPLICATED = ["c_ctx", "b_mod", "norm1_g", "norm2_g", "ssm_conv_b", "ssm_dt_bias", "ssm_a_log", "ssm_d", "ssm_norm_g",
              "swa_q_norm_g", "swa_k_norm_g", "swa_sink", "mla_q_lat_g", "mla_kv_lat_g", "mla_q_norm_g", "mla_k_norm_g"]
IN_SEGS = [("xbc", SSM_CONV_DIM), ("dt", 2 * SSM_HEADS), ("ks", SWA_KV_HEADS * SWA_HEAD_DIM), ("vs", SWA_KV_HEADS * SWA_HEAD_DIM),
           ("ckv", MLA_KV_RANK), ("kr", MLA_ROPE), ("z", SSM_INNER), ("qs", SWA_Q_HEADS * SWA_HEAD_DIM), ("cq", MLA_Q_RANK),
           ("g1", None), ("g2", None), ("g3", None)]


def _pack(vectors, multiple):
    flat = jnp.concatenate([v.reshape(-1) for v in vectors])
    pad = (-flat.shape[0]) % multiple
    return jnp.pad(flat, (0, pad)).reshape(-1, LANES)


def _unpack(packed, shapes):
    flat, out, off = packed.reshape(-1), [], 0
    for s in shapes:
        n = int(np.prod(s))
        out.append(flat[off:off + n].reshape(s))
        off += n
    return out


def _rope_tables(L, T, rot_dim):
    nf = rot_dim // 4
    inv = jnp.power(ROPE_BASE, -jnp.arange(nf, dtype=F32) / nf)
    r, col = jnp.meshgrid(jnp.arange(L // GRID_W, dtype=F32), jnp.arange(GRID_W, dtype=F32), indexing="ij")
    ang = jnp.stack([r.reshape(-1)[:, None] * inv, col.reshape(-1)[:, None] * inv], axis=1)
    cos, sin = jnp.cos(ang), jnp.sin(ang)
    c = jnp.concatenate([cos[:, 0], cos[:, 0], cos[:, 1], cos[:, 1]], axis=1)
    s = jnp.concatenate([-sin[:, 0], sin[:, 0], -sin[:, 1], sin[:, 1]], axis=1)
    c = jnp.pad(c, ((0, T - L), (0, LANES - rot_dim)), constant_values=1.0)
    s = jnp.pad(s, ((0, T - L), (0, LANES - rot_dim)))
    return c, s


def _pad_rows(a, rows):
    return jnp.pad(a, ((0, rows - a.shape[0]), (0, 0)))


def _row(w, per_head=0, off=0, diff=True):
    return ("row", w, per_head, off, diff)


PAR, PAR_ND = ("par", True), ("par", False)
GRP = ("grp", True)


def kernel(*args):
    n_fwd, n_w = len(FWD_NAMES), len(WEIGHT_NAMES)
    inp = dict(zip(FWD_NAMES, args[:n_fwd]))
    loss_target = args[n_fwd]
    mom_m = dict(zip(WEIGHT_NAMES, args[n_fwd + 1:n_fwd + 1 + n_w]))
    mom_v = dict(zip(WEIGHT_NAMES, args[n_fwd + 1 + n_w:]))

    x, ctx = inp["x"][0], inp["ctx"][0]
    L, D = x.shape
    n_ctx = ctx.shape[0]
    T = L + n_ctx
    depth = inp["w_in"].shape[0]
    me = _my_index()
    in_widths = [w if w is not None else D for _, w in IN_SEGS]
    in_offs = np.concatenate([[0], np.cumsum(in_widths)]).tolist()
    ffn_h = inp["w_ffn_out"].shape[1] * N_DEV
    cfg_swa = AttnCfg(SWA_Q_HEADS, SWA_Q_HEADS // SWA_KV_HEADS, SWA_HEAD_DIM, SWA_HEAD_DIM, SWA_HEAD_DIM ** -0.5, SWA_WINDOW,
                      True, L, T, 256, 1)
    cfg_mla = AttnCfg(MLA_HEADS, 1, MLA_QK_PAD, MLA_V, MLA_QK ** -0.5, None, False, L, T, 1024, 2)
    cfg_mla_bwd = AttnCfg(MLA_HEADS, 1, MLA_QK_PAD, MLA_V, MLA_QK ** -0.5, None, False, L, T, 2048, 1)

    def rf(name, fn, descs, arrays, outs, heads=1):
        return rowop_fwd(name, fn, descs, arrays, outs, T, L, heads=heads)

    def rb(name, fn, descs, arrays, outs, cts, heads=1, add=None, to_matmul=()):
        return rowop_bwd(name, fn, descs, arrays, outs, cts, T, L, heads=heads, add=add, to_matmul=to_matmul)

    local = []
    for n in GATHERED:
        w = inp[n]
        local.append((jnp.swapaxes(w, 1, 2) if n in COLUMN_SHARDED else w).astype(BF16))
    gathered = dict(zip(GATHERED, all_gather_hbm("gather_weights", local)))

    def full(n, l):
        g = gathered[n][:, l]
        return g.reshape(g.shape[0] * g.shape[1], g.shape[2])

    def layer_weights(l):
        wt = {}
        w_in_t = full("w_in", l)
        for (sn, _), o, w in zip(IN_SEGS, in_offs, in_widths):
            seg = w_in_t[o:o + w]
            wt[sn] = _pad_rows(seg, LANES) if sn == "kr" else seg
        uq = full("w_mla_uq", l).reshape(MLA_HEADS, MLA_QK, MLA_Q_RANK)
        wt["uqn"] = uq[:, :MLA_NOPE].reshape(MLA_HEADS * MLA_NOPE, MLA_Q_RANK)
        wt["uqr"] = jnp.pad(uq[:, MLA_NOPE:], ((0, 0), (0, LANES - MLA_ROPE), (0, 0))).reshape(MLA_HEADS * LANES, MLA_Q_RANK)
        ukv = full("w_mla_ukv", l).reshape(MLA_HEADS, MLA_NOPE + MLA_V, MLA_KV_RANK)
        wt["uk"] = ukv[:, :MLA_NOPE].reshape(MLA_HEADS * MLA_NOPE, MLA_KV_RANK)
        wt["uv"] = ukv[:, MLA_NOPE:].reshape(MLA_HEADS * MLA_V, MLA_KV_RANK)
        for n in ("w_p_ssm", "w_p_swa", "w_p_mla", "w_out", "w_ffn_out"):
            wt[n] = full(n, l)
        ffn_in_t = full("w_ffn_in", l)
        wt["fg"], wt["fu"] = ffn_in_t[:ffn_h], ffn_in_t[ffn_h:]
        return wt

    def layer_params(l):
        p = {}
        for n in ("norm1_g", "norm2_g", "ssm_conv_b", "ssm_norm_g", "swa_q_norm_g", "swa_k_norm_g", "mla_q_lat_g", "mla_kv_lat_g"):
            p[n] = inp[n][l][None]
        p["dt_bias"] = inp["ssm_dt_bias"][l].reshape(1, 2 * SSM_HEADS)
        p["alog_row"] = [inp["ssm_a_log"][l][d][None] for d in range(2)]
        p["alog_col"] = [inp["ssm_a_log"][l][d][:, None] for d in range(2)]
        p["d_lane"] = jnp.repeat(inp["ssm_d"][l], SSM_HEAD_DIM)[None]
        p["sink"] = inp["swa_sink"][l].reshape(SWA_Q_HEADS, 1, 1)
        for n, key in (("mla_q_norm_g", "gq"), ("mla_k_norm_g", "gk")):
            g = inp[n][l]
            p[key + "n"] = g[:MLA_NOPE][None]
            p[key + "r"] = jnp.pad(g[MLA_NOPE:], (0, LANES - MLA_ROPE))[None]
        return p

    conv_local = _pack([inp["ssm_conv_w"]], 8 * LANES)
    conv_all = all_gather_vmem("gather_conv_w", conv_local)
    cw = inp["ssm_conv_w"].shape
    conv_full = conv_all.reshape(N_DEV, -1)[:, :cw[0] * cw[1] * cw[2]].reshape(N_DEV, cw[0], cw[1], cw[2])
    conv_full = jnp.moveaxis(conv_full, 0, 2).reshape(cw[0], cw[1], N_DEV * cw[2])
    conv_w8 = jnp.pad(conv_full, ((0, 0), (0, 8 - cw[1]), (0, 0)))

    silu_c, silu_cc = small_fwd("silu_c", lambda a, b: fn_silu(a) + fn_silu(b), [inp["c"], inp["c_ctx"][None]], [(1, D), (1, D)])
    silu_all = all_gather_vmem("gather_silu_c", silu_c.reshape(D // LANES, LANES)).reshape(N_DEV, D)
    S_rows = 2 * N_DEV
    S_mat = jnp.concatenate([silu_all, silu_cc, jnp.zeros((S_rows - N_DEV - 1, D), F32)], axis=0)
    mod_cols = inp["w_mod"].shape[2]
    mods_local = []
    for l in range(depth):
        bias = lax.dynamic_slice(inp["b_mod"][l], (me * mod_cols,), (mod_cols,))
        mods_local.append(matmul(S_mat, inp["w_mod"][l], "nn", f"mod{l}", add=jnp.broadcast_to(bias[None], (S_rows, mod_cols))))
    mods_all = all_gather_vmem("gather_mods", jnp.stack(mods_local).reshape(-1, LANES))
    mods_all = jnp.moveaxis(mods_all.reshape(N_DEV, depth, S_rows, mod_cols), 0, 2).reshape(depth, S_rows, N_DEV * mod_cols)
    mods_lat = lax.dynamic_slice(mods_all, (0, me, 0), (depth, 1, N_DEV * mod_cols))[:, 0]
    mods_ctx = mods_all[:, N_DEV]

    def layer_mods(l):
        return [jnp.stack([mods_lat[l, j * D:(j + 1) * D], mods_ctx[l, j * D:(j + 1) * D]])[:, None] for j in range(6)]

    cs_swa = _rope_tables(L, T, SWA_HEAD_DIM)
    cs_mla = _rope_tables(L, T, MLA_ROPE)
    nm_descs = [_row(D), PAR, GRP, GRP]
    resid_descs = [_row(D, diff=False), _row(D), GRP]
    tab = [_row(LANES, diff=False), _row(LANES, diff=False)]
    swaq_descs = [_row(SWA_HEAD_DIM, 1), PAR] + tab
    swakv_descs = [_row(SWA_HEAD_DIM, 1), _row(SWA_HEAD_DIM, 1), PAR] + tab
    mlaq_descs = [_row(LANES, 1), _row(LANES, 1), PAR, PAR] + tab
    mlakv_descs = [_row(LANES, 1), _row(LANES, 1), _row(LANES), PAR, PAR] + tab
    ssdout_descs = [_row(SSM_INNER), _row(SSM_INNER, diff=False), _row(SSM_INNER), _row(SSM_INNER), PAR, PAR]
    merge_descs = [_row(D)] * 6
    swiglu_descs = [_row(ffn_h), _row(ffn_h)]
    seg_names = [sn for sn, _ in IN_SEGS]
    seg_groups = [seg_names[:7], seg_names[7:]]

    def layer_fwd(l, X, wt, p, mods):
        sh1, sc1, gt1, sh2, sc2, gt2 = mods
        r = {"X": X}
        r["h1"] = rf(f"l{l}_norm1", fn_norm_mod, nm_descs, [X, p["norm1_g"], sh1, sc1], [(D, 0, BF16)])[0]
        for gi, group in enumerate(seg_groups):
            dts = [BF16 if sn in ("g1", "g2", "g3") else F32 for sn in group]
            r.update(zip(group, matmul_multi(f"l{l}_in{gi}", r["h1"], [wt[sn] for sn in group], dts)))
        r["u"] = conv_fwd(f"l{l}_conv", r["xbc"], conv_w8[l], p["ssm_conv_b"], L)
        r["dts"] = rf(f"l{l}_softplus", fn_softplus, [_row(2 * SSM_HEADS), PAR], [r["dt"], p["dt_bias"]], [(2 * SSM_HEADS, 0, F32)])[0]
        for d in range(2):
            dt_d = r["dts"][:, d * SSM_HEADS:(d + 1) * SSM_HEADS]
            r[f"dt{d}"], r[f"dtt{d}"] = dt_d, dt_d.T
            r[f"y{d}"], r[f"hin{d}"] = ssd_fwd(f"l{l}_ssd{d}", d, r["u"], dt_d, dt_d.T, p["alog_row"][d], p["alog_col"][d], L)
        r["ys"] = rf(f"l{l}_ssd_out", fn_ssd_out, ssdout_descs, [r["y0"], r["y1"], r["u"], r["z"], p["d_lane"], p["ssm_norm_g"]],
                     [(SSM_INNER, 0, BF16)])[0]
        r["Qs"] = rf(f"l{l}_swa_q", fn_swa_q, swaq_descs, [r["qs"], p["swa_q_norm_g"], *cs_swa], [(SWA_HEAD_DIM, 1, BF16)], SWA_Q_HEADS)[0]
        r["Ks"], r["Vs"] = rf(f"l{l}_swa_kv", fn_swa_kv, swakv_descs, [r["ks"], r["vs"], p["swa_k_norm_g"], *cs_swa],
                              [(SWA_HEAD_DIM, 1, BF16), (SWA_HEAD_DIM, 1, BF16)], SWA_KV_HEADS)
        r["Os"], r["lse_s"] = flash_fwd(f"l{l}_swa_fwd", cfg_swa, r["Qs"], r["Ks"], r["Vs"], p["sink"])
        r["cqn"] = rf(f"l{l}_q_lat", fn_rms, [_row(MLA_Q_RANK), PAR], [r["cq"], p["mla_q_lat_g"]], [(MLA_Q_RANK, 0, BF16)])[0]
        r["qn"], r["qr"] = matmul_multi(f"l{l}_uq", r["cqn"], [wt["uqn"], wt["uqr"]])
        r["Qm"] = rf(f"l{l}_mla_q", fn_mla_q, mlaq_descs, [r["qn"], r["qr"], p["gqn"], p["gqr"], *cs_mla], [(MLA_QK_PAD, 1, BF16)], MLA_HEADS)[0]
        r["ckvn"] = rf(f"l{l}_kv_lat", fn_rms, [_row(MLA_KV_RANK), PAR], [r["ckv"], p["mla_kv_lat_g"]], [(MLA_KV_RANK, 0, BF16)])[0]
        r["kn"], r["vp"] = matmul_multi(f"l{l}_ukv", r["ckvn"], [wt["uk"], wt["uv"]])
        r["Km"], r["Vm"] = rf(f"l{l}_mla_kv", fn_mla_kv, mlakv_descs, [r["kn"], r["vp"], r["kr"], p["gkn"], p["gkr"], *cs_mla],
                              [(MLA_QK_PAD, 1, BF16), (MLA_V, 1, BF16)], MLA_HEADS)
        r["Om"], r["lse_m"] = flash_fwd(f"l{l}_mla_fwd", cfg_mla, r["Qm"], r["Km"], r["Vm"], None)
        r["P1"] = matmul(r["ys"], wt["w_p_ssm"], "nn", f"l{l}_p_ssm", out_dtype=BF16)
        r["P2"] = matmul(r["Os"], wt["w_p_swa"], "nn", f"l{l}_p_swa", out_dtype=BF16)
        r["P3"] = matmul(r["Om"], wt["w_p_mla"], "nn", f"l{l}_p_mla", out_dtype=BF16)
        r["mg"] = rf(f"l{l}_merge", fn_merge, merge_descs, [r["g1"], r["g2"], r["g3"], r["P1"], r["P2"], r["P3"]], [(D, 0, BF16)])[0]
        r["A"] = matmul(r["mg"], wt["w_out"], "nn", f"l{l}_out")
        r["X1"] = rf(f"l{l}_resid1", fn_resid, resid_descs, [X, r["A"], gt1], [(D, 0, F32)])[0]
        r["h2"] = rf(f"l{l}_norm2", fn_norm_mod, nm_descs, [r["X1"], p["norm2_g"], sh2, sc2], [(D, 0, BF16)])[0]
        r["Fg"] = matmul(r["h2"], wt["fg"], "nt", f"l{l}_ffn_g", out_dtype=BF16)
        r["Fu"] = matmul(r["h2"], wt["fu"], "nt", f"l{l}_ffn_u", out_dtype=BF16)
        r["sg"] = rf(f"l{l}_swiglu", fn_swiglu, swiglu_descs, [r["Fg"], r["Fu"]], [(ffn_h, 0, BF16)])[0]
        r["B"] = matmul(r["sg"], wt["w_ffn_out"], "nn", f"l{l}_ffn_out")
        X2 = rf(f"l{l}_resid2", fn_resid, resid_descs, [r["X1"], r["B"], gt2], [(D, 0, F32)])[0]
        return X2, r

    def attn_bwd(tag, cfg, q, k, v, o, lse, do, sink):
        res = flash_bwd_fused(f"{tag}_bwd", cfg, q, k, v, o, do, lse, sink)
        return res[0], res[1], res[2], (res[3] if cfg.has_sink else None)

    def layer_bwd(l, dX2, r, wt, p, mods):
        sh1, sc1, gt1, sh2, sc2, gt2 = mods
        g, gw = {}, {}
        dmod = [None] * 6
        dB, dmod[5] = rb(f"l{l}_resid2_b", fn_resid, resid_descs, [r["X1"], r["B"], gt2], [(D, 0, F32)], [dX2], to_matmul=(1,))
        dsg = matmul(dB, wt["w_ffn_out"], "nt", f"l{l}_ffn_out_da")
        gw["w_ffn_out"] = matmul(r["sg"], dB, "tn", f"l{l}_ffn_out_dw", out_dtype=BF16)
        dFg, dFu = rb(f"l{l}_swiglu_b", fn_swiglu, swiglu_descs, [r["Fg"], r["Fu"]], [(ffn_h, 0, BF16)], [dsg], to_matmul=(0, 1))
        dh2 = matmul(dFg, wt["fg"], "nn", f"l{l}_ffn_g_da")
        dh2 = matmul(dFu, wt["fu"], "nn", f"l{l}_ffn_u_da", add=dh2)
        gw["w_ffn_in"] = jnp.concatenate([matmul(r["h2"], dFg, "tn", f"l{l}_ffn_g_dw", out_dtype=BF16),
                                          matmul(r["h2"], dFu, "tn", f"l{l}_ffn_u_dw", out_dtype=BF16)], axis=1)
        dX1, g["norm2_g"], dmod[3], dmod[4] = rb(f"l{l}_norm2_b", fn_norm_mod, nm_descs, [r["X1"], p["norm2_g"], sh2, sc2],
                                                [(D, 0, BF16)], [dh2], add={0: dX2})
        dA, dmod[2] = rb(f"l{l}_resid1_b", fn_resid, resid_descs, [r["X"], r["A"], gt1], [(D, 0, F32)], [dX1], to_matmul=(1,))
        dmg = matmul(dA, wt["w_out"], "nt", f"l{l}_out_da")
        gw["w_out"] = matmul(r["mg"], dA, "tn", f"l{l}_out_dw", out_dtype=BF16)
        dsegs = {}
        dsegs["g1"], dsegs["g2"], dsegs["g3"], dP1, dP2, dP3 = rb(
            f"l{l}_merge_b", fn_merge, merge_descs, [r["g1"], r["g2"], r["g3"], r["P1"], r["P2"], r["P3"]], [(D, 0, BF16)], [dmg],
            to_matmul=tuple(range(6)))
        dys = matmul(dP1, wt["w_p_ssm"], "nt", f"l{l}_p_ssm_da")
        dOs = matmul(dP2, wt["w_p_swa"], "nt", f"l{l}_p_swa_da")
        dOm = matmul(dP3, wt["w_p_mla"], "nt", f"l{l}_p_mla_da")
        gw["w_p_ssm"] = matmul(r["ys"], dP1, "tn", f"l{l}_p_ssm_dw", out_dtype=BF16)
        gw["w_p_swa"] = matmul(r["Os"], dP2, "tn", f"l{l}_p_swa_dw", out_dtype=BF16)
        gw["w_p_mla"] = matmul(r["Om"], dP3, "tn", f"l{l}_p_mla_dw", out_dtype=BF16)
        dQm, dKm, dVm, _ = attn_bwd(f"l{l}_mla", cfg_mla_bwd, r["Qm"], r["Km"], r["Vm"], r["Om"], r["lse_m"], dOm, None)
        dkn, dvp, dsegs["kr"], dgkn, dgkr = rb(f"l{l}_mla_kv_b", fn_mla_kv, mlakv_descs,
                                               [r["kn"], r["vp"], r["kr"], p["gkn"], p["gkr"], *cs_mla],
                                               [(MLA_QK_PAD, 1, BF16), (MLA_V, 1, BF16)], [dKm, dVm], MLA_HEADS, to_matmul=(0, 1, 2))
        dckvn = matmul_sum(f"l{l}_ukv_da", [(dkn, wt["uk"]), (dvp, wt["uv"])])
        dw_uk = matmul(r["ckvn"], dkn, "tn", f"l{l}_uk_dw", out_dtype=BF16).reshape(MLA_KV_RANK, MLA_HEADS, MLA_NOPE)
        dw_uv = matmul(r["ckvn"], dvp, "tn", f"l{l}_uv_dw", out_dtype=BF16).reshape(MLA_KV_RANK, MLA_HEADS, MLA_V)
        gw["w_mla_ukv"] = jnp.concatenate([dw_uk, dw_uv], axis=2).reshape(MLA_KV_RANK, -1)
        dsegs["ckv"], g["mla_kv_lat_g"] = rb(f"l{l}_kv_lat_b", fn_rms, [_row(MLA_KV_RANK), PAR], [r["ckv"], p["mla_kv_lat_g"]],
                                             [(MLA_KV_RANK, 0, BF16)], [dckvn], to_matmul=(0,))
        dqn, dqr, dgqn, dgqr = rb(f"l{l}_mla_q_b", fn_mla_q, mlaq_descs, [r["qn"], r["qr"], p["gqn"], p["gqr"], *cs_mla],
                                  [(MLA_QK_PAD, 1, BF16)], [dQm], MLA_HEADS, to_matmul=(0, 1))
        dcqn = matmul_sum(f"l{l}_uq_da", [(dqn, wt["uqn"]), (dqr, wt["uqr"])])
        dw_uqn = matmul(r["cqn"], dqn, "tn", f"l{l}_uqn_dw", out_dtype=BF16).reshape(MLA_Q_RANK, MLA_HEADS, MLA_NOPE)
        dw_uqr = matmul(r["cqn"], dqr, "tn", f"l{l}_uqr_dw", out_dtype=BF16).reshape(MLA_Q_RANK, MLA_HEADS, LANES)[:, :, :MLA_ROPE]
        gw["w_mla_uq"] = jnp.concatenate([dw_uqn, dw_uqr], axis=2).reshape(MLA_Q_RANK, -1)
        dsegs["cq"], g["mla_q_lat_g"] = rb(f"l{l}_q_lat_b", fn_rms, [_row(MLA_Q_RANK), PAR], [r["cq"], p["mla_q_lat_g"]],
                                           [(MLA_Q_RANK, 0, BF16)], [dcqn], to_matmul=(0,))
        g["mla_q_norm_g"] = jnp.concatenate([dgqn[0], dgqr[0, :MLA_ROPE]])
        g["mla_k_norm_g"] = jnp.concatenate([dgkn[0], dgkr[0, :MLA_ROPE]])
        dQs, dKs, dVs, dsink = attn_bwd(f"l{l}_swa", cfg_swa, r["Qs"], r["Ks"], r["Vs"], r["Os"], r["lse_s"], dOs, p["sink"])
        g["swa_sink"] = dsink.reshape(SWA_Q_HEADS)
        dsegs["qs"], g["swa_q_norm_g"] = rb(f"l{l}_swa_q_b", fn_swa_q, swaq_descs, [r["qs"], p["swa_q_norm_g"], *cs_swa],
                                            [(SWA_HEAD_DIM, 1, BF16)], [dQs], SWA_Q_HEADS, to_matmul=(0,))
        dsegs["ks"], dsegs["vs"], g["swa_k_norm_g"] = rb(f"l{l}_swa_kv_b", fn_swa_kv, swakv_descs,
                                                         [r["ks"], r["vs"], p["swa_k_norm_g"], *cs_swa],
                                                         [(SWA_HEAD_DIM, 1, BF16), (SWA_HEAD_DIM, 1, BF16)], [dKs, dVs], SWA_KV_HEADS,
                                                         to_matmul=(0, 1))
        dy, dxs, dsegs["z"], dd_lane, g["ssm_norm_g"] = rb(
            f"l{l}_ssd_out_b", fn_ssd_out, ssdout_descs, [r["y0"], r["y1"], r["u"], r["z"], p["d_lane"], p["ssm_norm_g"]],
            [(SSM_INNER, 0, BF16)], [dys], to_matmul=(3,))
        g["ssm_d"] = dd_lane.reshape(SSM_HEADS, SSM_HEAD_DIM).sum(axis=1)
        du, ddts, dalog = None, [], []
        for d in range(2):
            du, ddt, ddtt, dar, dac = ssd_bwd(f"l{l}_ssd{d}_b", d, r["u"], r[f"dt{d}"], r[f"dtt{d}"], p["alog_row"][d], p["alog_col"][d],
                                              r[f"hin{d}"], dy, L, add_x=dxs if d == 0 else None, add_u=du)
            ddts.append(ddt + ddtt.T)
            dalog.append(dar[0] + dac[:, 0])
        g["ssm_a_log"] = jnp.stack(dalog)
        dsegs["xbc"], dconv_w, g["ssm_conv_b"] = conv_bwd(f"l{l}_conv_b", r["xbc"], conv_w8[l], p["ssm_conv_b"], du, L)
        dsegs["dt"], ddt_bias = rb(f"l{l}_softplus_b", fn_softplus, [_row(2 * SSM_HEADS), PAR], [r["dt"], p["dt_bias"]],
                                   [(2 * SSM_HEADS, 0, F32)], [jnp.concatenate(ddts, axis=1)], to_matmul=(0,))
        g["ssm_dt_bias"] = ddt_bias.reshape(2, SSM_HEADS)
        g["ssm_conv_w"] = dconv_w[:SSM_CONV]
        dh1, dws = None, []
        for gi, group in enumerate(seg_groups):
            dh1 = matmul_sum(f"l{l}_in_da{gi}", [(dsegs[sn], wt[sn]) for sn in group], add=dh1)
        for sn, w in zip(seg_names, in_widths):
            dws.append(matmul(r["h1"], dsegs[sn], "tn", f"l{l}_in_{sn}_dw", out_dtype=BF16)[:, :w])
        gw["w_in"] = jnp.concatenate(dws, axis=1)
        dX, g["norm1_g"], dmod[0], dmod[1] = rb(f"l{l}_norm1_b", fn_norm_mod, nm_descs, [r["X"], p["norm1_g"], sh1, sc1],
                                               [(D, 0, BF16)], [dh1], add={0: dX1})
        for n in ("norm1_g", "norm2_g", "ssm_conv_b", "ssm_norm_g", "swa_q_norm_g", "swa_k_norm_g", "mla_q_lat_g", "mla_kv_lat_g"):
            g[n] = g[n][0]
        dmod_lat = jnp.concatenate([dm[0, 0] for dm in dmod])
        dmod_ctx = jnp.concatenate([dm[1, 0] for dm in dmod])
        return dX, g, gw, dmod_lat, dmod_ctx

    X = jnp.concatenate([x, ctx], axis=0)
    saved = []
    for l in range(depth):
        wt, p, mods = layer_weights(l), layer_params(l), layer_mods(l)
        X, r = layer_fwd(l, X, wt, p, mods)
        saved.append((r, wt, p, mods))
    loss_part, dX = loss_and_grad("loss", X, loss_target[0], L)
    loss = lax.psum(loss_part[0, 0], ("x", "y", "c"))
    small_g = [None] * depth
    big_g = [None] * depth
    dmods = [None] * depth
    for l in reversed(range(depth)):
        r, wt, p, mods = saved[l]
        dX, small_g[l], big_g[l], dm_lat, dm_ctx = layer_bwd(l, dX, r, wt, p, mods)
        dmods[l] = jnp.stack([dm_lat, dm_ctx])
    grad_x = dX[:L][None]

    dm_all = all_gather_vmem("gather_dmods", jnp.stack(dmods).reshape(-1, LANES)).reshape(N_DEV, depth, 2, N_DEV * mod_cols)
    dm_rows = jnp.concatenate([jnp.moveaxis(dm_all[:, :, 0], 0, 1), dm_all[:, :, 1].sum(axis=0)[:, None],
                               jnp.zeros((depth, S_rows - N_DEV - 1, N_DEV * mod_cols), F32)], axis=1)
    dm_mine = lax.dynamic_slice(dm_rows, (0, 0, me * mod_cols), (depth, S_rows, mod_cols))
    grads = {}
    grads["w_mod"] = jnp.stack([matmul(S_mat, dm_mine[l], "tn", f"mod{l}_dw") for l in range(depth)])
    d_silu = None
    for l in range(depth):
        d_silu = matmul(dm_mine[l], inp["w_mod"][l], "nt", f"mod{l}_da", add=d_silu)
    small = {n: jnp.stack([small_g[l][n] for l in range(depth)]) for n in small_g[0]}
    small["c_ctx"] = small_bwd("silu_c_b", fn_silu, [inp["c_ctx"][None]], [d_silu[N_DEV:N_DEV + 1]])[0][0]
    small["b_mod"] = jnp.stack(dmods).sum(axis=1)

    rep_shapes = [inp[n].shape for n in REPLICATED]
    conv_shape = (depth, SSM_CONV, SSM_CONV_DIM)
    packed = _pack([small[n] for n in REPLICATED] + [small["ssm_conv_w"]], 8 * LANES)
    small_sum = sum_parts("sum_small", all_gather_vmem("gather_small", packed))
    summed = _unpack(small_sum, rep_shapes + [conv_shape])
    for n, gsum in zip(REPLICATED, summed):
        grads[n] = gsum
    grads["ssm_conv_w"] = lax.dynamic_slice(summed[-1], (0, 0, me * cw[2]), cw)

    slabs = []
    for n in GATHERED:
        gfull = jnp.stack([big_g[l][n] for l in range(depth)])
        if n in COLUMN_SHARDED:
            k_dim, n_dim = gfull.shape[1], gfull.shape[2]
            slab = jnp.moveaxis(gfull.reshape(depth, k_dim, N_DEV, n_dim // N_DEV), 2, 0)
        else:
            k_dim, n_dim = gfull.shape[1], gfull.shape[2]
            slab = jnp.moveaxis(gfull.reshape(depth, N_DEV, k_dim // N_DEV, n_dim), 1, 0)
        slabs.append(slab.reshape((N_CHIPS, 2) + slab.shape[1:]))
    from_sibling = exchange_sibling("exchange_sibling", slabs)
    chip_sums = []
    for n, slab, got in zip(GATHERED, slabs, from_sibling):
        shp = inp[n].shape
        rows = shp[0] * shp[1]
        chip_sums.append(pair_sum(f"pair_{n}", slab.reshape(N_CHIPS, 2, rows, shp[2]), got.reshape(N_CHIPS, rows, shp[2])))
    for n, parts in zip(GATHERED, exchange_chips("exchange_chips", chip_sums)):
        grads[n] = sum_parts(f"sum_{n}", parts).reshape(inp[n].shape)

    delta, new_m, new_v = {}, {}, {}
    rep_pack = lambda d: _pack([d[n] for n in REPLICATED], 8 * LANES)
    rep_out = adamw("adamw_small", rep_pack(inp), rep_pack(grads), rep_pack(mom_m), rep_pack(mom_v))
    for out, res in zip((delta, new_m, new_v), rep_out):
        for n, a in zip(REPLICATED, _unpack(res, rep_shapes)):
            out[n] = a
    for n in ["w_mod", "ssm_conv_w"] + GATHERED:
        shp = inp[n].shape
        two_d = (shp[0] * shp[1], shp[2])
        res = adamw(f"adamw_{n}", inp[n].reshape(two_d), grads[n].reshape(two_d), mom_m[n].reshape(two_d), mom_v[n].reshape(two_d))
        delta[n], new_m[n], new_v[n] = [a.reshape(shp) for a in res]

    return (loss, grad_x, *[grads[n] for n in WEIGHT_NAMES], *[delta[n] for n in WEIGHT_NAMES],
            *[new_m[n] for n in WEIGHT_NAMES], *[new_v[n] for n in WEIGHT_NAMES])
```

```python
import functools
import math

import numpy as np
import jax
import jax.numpy as jnp
from jax import lax
from jax.experimental import pallas as pl
from jax.experimental.pallas import tpu as pltpu

F32 = jnp.float32
BF16 = jnp.bfloat16

N_DEV = 8
V7X_VMEM_BYTES = 64 * 1024 * 1024
VMEM_LIMIT_BYTES = V7X_VMEM_BYTES - 8 * 1024 * 1024
LANES = 128

EPS = 1e-6
ROPE_BASE = 10000.0
GRID_W = 64
SSM_HEADS, SSM_HEAD_DIM, SSM_GROUPS, SSM_STATE, SSM_CONV, SSM_CHUNK = 16, 64, 2, 128, 5, 128
SSM_INNER = SSM_HEADS * SSM_HEAD_DIM
SSM_CONV_DIM = SSM_INNER + 2 * SSM_GROUPS * SSM_STATE
SWA_Q_HEADS, SWA_KV_HEADS, SWA_HEAD_DIM, SWA_WINDOW = 8, 2, 128, 128
MLA_HEADS, MLA_Q_RANK, MLA_KV_RANK, MLA_NOPE, MLA_ROPE, MLA_V = 8, 384, 256, 128, 64, 128
MLA_QK = MLA_NOPE + MLA_ROPE
MLA_QK_PAD = 2 * LANES
ADAM_LR, ADAM_B1, ADAM_B2, ADAM_EPS, ADAM_WD, ADAM_STEP = 0.001, 0.9, 0.999, 1e-08, 0.01, 10

ROW_TILE = 256


def _cparams(sem, **kw):
    return pltpu.CompilerParams(dimension_semantics=sem, vmem_limit_bytes=VMEM_LIMIT_BYTES, **kw)


def _pick(dim, prefs):
    for p in prefs:
        if dim % p == 0:
            return p
    return dim


def matmul(a, b, mode, name, out_dtype=F32, add=None):
    if mode == "nn":
        (M, K), (K2, N) = a.shape, b.shape
    elif mode == "nt":
        (M, K), (N, K2) = a.shape, b.shape
    else:
        (K, M), (K2, N) = a.shape, b.shape
    assert K == K2, (name, a.shape, b.shape)
    has_add = add is not None
    tm, tn, tk = _matmul_tiles(M, N, K, a.dtype.itemsize, b.dtype.itemsize, jnp.dtype(out_dtype).itemsize, has_add,
                                   m_on_lanes=(mode == "tn"))
    nk = K // tk
    dims = {"nn": (((1,), (0,)), ((), ())), "nt": (((1,), (1,)), ((), ())), "tn": (((0,), (0,)), ((), ()))}[mode]
    a_spec = pl.BlockSpec((tk, tm), lambda i, j, k: (k, i)) if mode == "tn" else pl.BlockSpec((tm, tk), lambda i, j, k: (i, k))
    b_spec = pl.BlockSpec((tn, tk), lambda i, j, k: (j, k)) if mode == "nt" else pl.BlockSpec((tk, tn), lambda i, j, k: (k, j))
    o_spec = pl.BlockSpec((tm, tn), lambda i, j, k: (i, j))

    def body(*refs):
        a_ref, b_ref = refs[:2]
        c_ref = refs[2] if has_add else None
        o_ref = refs[3] if has_add else refs[2]
        part = lax.dot_general(a_ref[...].astype(BF16), b_ref[...].astype(BF16), dims, preferred_element_type=F32)
        if nk == 1:
            o_ref[...] = (part + c_ref[...] if has_add else part).astype(o_ref.dtype)
            return
        acc_ref = refs[-1]
        k = pl.program_id(2)

        @pl.when(k == 0)
        def _():
            acc_ref[...] = part + c_ref[...] if has_add else part

        @pl.when(k > 0)
        def _():
            acc_ref[...] += part

        @pl.when(k == nk - 1)
        def _():
            o_ref[...] = acc_ref[...].astype(o_ref.dtype)

    ins = [a, b] + ([add] if has_add else [])
    in_specs = [a_spec, b_spec] + ([o_spec] if has_add else [])
    return pl.pallas_call(
        body, name=name, grid=(M // tm, N // tn, nk), in_specs=in_specs, out_specs=o_spec,
        out_shape=jax.ShapeDtypeStruct((M, N), out_dtype),
        scratch_shapes=[pltpu.VMEM((tm, tn), F32)] if nk > 1 else [],
        input_output_aliases=({2: 0} if has_add else {}),
        compiler_params=_cparams(("parallel", "parallel", "arbitrary")),
    )(*ins)


def matmul_sum(name, pairs, add=None):
    M, N = pairs[0][0].shape[0], pairs[0][1].shape[1]
    n = len(pairs)
    has_add = add is not None
    resident = sum(2 * b.shape[0] * N * b.dtype.itemsize for _, b in pairs)
    tm = next((t for t in (768, 512, 384, 256, 128) if M % t == 0 and resident + sum(
        2 * t * a.shape[1] * a.dtype.itemsize + t * a.shape[1] * 2 for a, _ in pairs) + 6 * t * N * 4 <= MATMUL_VMEM_BUDGET), None)
    assert tm is not None, name

    def body(*refs):
        acc = refs[2 * n][...] if has_add else None
        for s in range(n):
            part = jnp.dot(refs[2 * s][...].astype(BF16), refs[2 * s + 1][...].astype(BF16), preferred_element_type=F32)
            acc = part if acc is None else acc + part
        refs[-1][...] = acc

    o_spec = pl.BlockSpec((tm, N), lambda i: (i, 0))
    in_specs, ins = [], []
    for a, b in pairs:
        in_specs += [pl.BlockSpec((tm, a.shape[1]), lambda i: (i, 0)), pl.BlockSpec(b.shape, lambda i: (0, 0))]
        ins += [a, b]
    if has_add:
        in_specs.append(o_spec)
        ins.append(add)
    return pl.pallas_call(body, name=name, grid=(M // tm,), in_specs=in_specs, out_specs=o_spec,
                          out_shape=jax.ShapeDtypeStruct((M, N), F32), input_output_aliases=({2 * n: 0} if has_add else {}),
                          compiler_params=_cparams(("parallel",)))(*ins)


def matmul_multi(name, a, bs, out_dtypes=None):
    out_dtypes = out_dtypes or [F32] * len(bs)
    M, K = a.shape
    n = len(bs)
    resident = sum(2 * b.shape[0] * K * b.dtype.itemsize for b in bs)
    n_total = sum(b.shape[0] for b in bs)
    tm = next((t for t in (768, 512, 384, 256, 128) if M % t == 0 and
               resident + 2 * t * K * a.dtype.itemsize + 3 * t * n_total * 4 <= MATMUL_VMEM_BUDGET), None)
    assert tm is not None, name

    def body(*refs):
        lhs = refs[0][...].astype(BF16)
        for s in range(n):
            out = lax.dot_general(lhs, refs[1 + s][...].astype(BF16), NT_DIMS, preferred_element_type=F32)
            refs[1 + n + s][...] = out.astype(out_dtypes[s])

    in_specs = [pl.BlockSpec((tm, K), lambda i: (i, 0))] + [pl.BlockSpec(b.shape, lambda i: (0, 0)) for b in bs]
    return pl.pallas_call(
        body, name=name, grid=(M // tm,), in_specs=in_specs,
        out_specs=[pl.BlockSpec((tm, b.shape[0]), lambda i: (i, 0)) for b in bs],
        out_shape=[jax.ShapeDtypeStruct((M, b.shape[0]), dt) for b, dt in zip(bs, out_dtypes)],
        compiler_params=_cparams(("parallel",)))(a, *bs)


MATMUL_VMEM_BUDGET = 36 * 1024 * 1024


def _matmul_tiles(M, N, K, a_bytes, b_bytes, o_bytes, has_add, m_on_lanes=False):
    tk = K if K <= 1536 else _pick(K, (1408, 1024, 768, 704, 512, 256))
    nk = K // tk
    m_cands = [t for t in (1024, 768, 512, 384, 256, 128) if M % t == 0] or [M]
    if M % 768 and M % 1024:
        m_cands += [t for t in (1408, 704, 352) if M % t == 0 and not (m_on_lanes and t % LANES)]
    n_cands = [t for t in range(LANES, min(N, 2816) + 1, LANES) if N % t == 0] or [N]
    best = None
    for tm in m_cands:
        for tn in n_cands:
            pipeline = 2 * (tm * tk * a_bytes + tk * tn * b_bytes + tm * tn * o_bytes) + (2 * tm * tn * 4 if has_add else 0)
            temps = tm * tn * 4 * (2 if nk > 1 else 1) + (tm * tk * 2 if a_bytes == 4 else 0) + (tk * tn * 2 if b_bytes == 4 else 0)
            if pipeline + temps <= MATMUL_VMEM_BUDGET:
                score = (tm * tn, tn)
                if best is None or score > best[0]:
                    best = (score, tm, tn)
    if best is None:
        return m_cands[-1], n_cands[0], tk
    return best[1], best[2], tk


def _row_specs(descs, arrays, tm, nct, heads):
    specs = []
    for d, arr in zip(descs, arrays):
        if d[0] == "row":
            _, w, per_head, off, _ = d
            specs.append(pl.BlockSpec((tm, w * (heads if per_head else 1)), lambda i, off=off: (i, off)))
        elif d[0] == "par":
            specs.append(pl.BlockSpec(arr.shape, lambda i, nd=arr.ndim: (0,) * nd))
        else:
            specs.append(pl.BlockSpec((1,) + arr.shape[1:], lambda i, nd=arr.ndim: (jnp.where(i >= nct, 1, 0),) + (0,) * (nd - 1)))
    return specs


def _load(d, ref, h):
    if d[0] == "grp":
        return ref[0]
    if d[0] == "row" and d[2]:
        return ref[:, h * d[1]:(h + 1) * d[1]].astype(F32)
    return ref[...].astype(F32) if d[0] == "row" else ref[...]


def _out_specs(outs, tm, heads):
    return [pl.BlockSpec((tm, w * (heads if ph else 1)), lambda i: (i, 0)) for (w, ph, _) in outs]


def rowop_fwd(name, fn, descs, arrays, outs, T, n_ctx, heads=1, tm=ROW_TILE):
    nct = n_ctx // tm
    n_in = len(descs)

    def body(*refs):
        for h in range(heads):
            res = fn(*[_load(d, r, h) for d, r in zip(descs, refs[:n_in])])
            for o_ref, r, (w, ph, _) in zip(refs[n_in:], res, outs):
                if ph:
                    o_ref[:, h * w:(h + 1) * w] = r.astype(o_ref.dtype)
                else:
                    o_ref[...] = r.astype(o_ref.dtype)

    out_shape = [jax.ShapeDtypeStruct((T, w * (heads if ph else 1)), dt) for (w, ph, dt) in outs]
    return pl.pallas_call(
        body, name=name, grid=(T // tm,), in_specs=_row_specs(descs, arrays, tm, nct, heads), out_specs=_out_specs(outs, tm, heads),
        out_shape=out_shape, compiler_params=_cparams(("parallel",)),
    )(*arrays)


def rowop_bwd(name, fn, descs, arrays, outs, cts, T, n_ctx, heads=1, tm=ROW_TILE, add=None, to_matmul=()):
    nct = n_ctx // tm
    n_in, n_ct = len(descs), len(cts)
    add = add or {}
    diff_idx = [k for k, d in enumerate(descs) if d[-1]]
    add_idx = [k for k in diff_idx if k in add]

    def body(*refs):
        in_refs, ct_refs = refs[:n_in], refs[n_in:n_in + n_ct]
        add_refs = dict(zip(add_idx, refs[n_in + n_ct:n_in + n_ct + len(add_idx)]))
        g_refs = refs[n_in + n_ct + len(add_idx):]
        i = pl.program_id(0)
        shared = {}
        for h in range(heads):
            vals = [_load(d, r, h) for d, r in zip(descs, in_refs)]

            def f(*dvals, vals=vals):
                full = list(vals)
                for k, v in zip(diff_idx, dvals):
                    full[k] = v
                return tuple(fn(*full))

            _, vjp = jax.vjp(f, *[vals[k] for k in diff_idx])
            cts_h = tuple(c[:, h * w:(h + 1) * w] if ph else c[...] for c, (w, ph, _) in zip(ct_refs, outs))
            for k, g_ref, g in zip(diff_idx, g_refs, vjp(cts_h)):
                d = descs[k]
                if d[0] == "row" and d[2]:
                    g_ref[:, h * d[1]:(h + 1) * d[1]] = g.astype(g_ref.dtype)
                else:
                    shared[k] = g if k not in shared else shared[k] + g
        for k, g_ref in zip(diff_idx, g_refs):
            d = descs[k]
            if k not in shared:
                continue
            g = shared[k]
            if d[0] == "row":
                if k in add_refs:
                    g = g + add_refs[k][...]
                g_ref[...] = g.astype(g_ref.dtype)
            elif d[0] == "par":
                _accumulate(g_ref, g, i == 0)
            else:
                _accumulate(g_ref, g[None], jnp.logical_or(i == 0, i == nct))

    in_specs = _row_specs(descs, arrays, tm, nct, heads)
    g_specs, g_shape = [], []
    for k in diff_idx:
        d = descs[k]
        if d[0] == "row":
            g_specs.append(pl.BlockSpec((tm, d[1] * (heads if d[2] else 1)), lambda i: (i, 0)))
            g_shape.append(jax.ShapeDtypeStruct((T, d[1] * (heads if d[2] else 1)), BF16 if k in to_matmul else F32))
        else:
            g_specs.append(in_specs[k])
            g_shape.append(jax.ShapeDtypeStruct(arrays[k].shape, F32))
    add_specs = [g_specs[diff_idx.index(k)] for k in add_idx]
    return pl.pallas_call(
        body, name=name, grid=(T // tm,), in_specs=in_specs + _out_specs(outs, tm, heads) + add_specs, out_specs=g_specs,
        out_shape=g_shape, compiler_params=_cparams(("arbitrary",)),
    )(*arrays, *cts, *[add[k] for k in add_idx])


def _accumulate(ref, val, first):
    @pl.when(first)
    def _():
        ref[...] = val.astype(ref.dtype)

    @pl.when(jnp.logical_not(first))
    def _():
        ref[...] += val.astype(ref.dtype)


def _rms(x, count=None):
    n = x.shape[-1] if count is None else count
    return x * lax.rsqrt(jnp.sum(x * x, axis=-1, keepdims=True) * (1.0 / n) + EPS)


def _swap_halves(x, nf):
    w = x.shape[-1]
    lane = lax.broadcasted_iota(jnp.int32, x.shape, x.ndim - 1)
    return jnp.where((lane % (2 * nf)) < nf, pltpu.roll(x, w - nf, x.ndim - 1), pltpu.roll(x, nf, x.ndim - 1))


def _make_rope(nf):
    @jax.custom_vjp
    def rope(x, c, s):
        return x * c + _swap_halves(x, nf) * s

    def fwd(x, c, s):
        return rope(x, c, s), (c, s)

    def bwd(res, g):
        c, s = res
        return g * c + _swap_halves(g * s, nf), jnp.zeros_like(c), jnp.zeros_like(s)

    rope.defvjp(fwd, bwd)
    return rope


_rope_swa = _make_rope(SWA_HEAD_DIM // 4)
_rope_mla = _make_rope(MLA_ROPE // 4)


@jax.custom_vjp
def _softplus(x):
    e = jnp.exp(-jnp.abs(x))
    u = 1.0 + e
    log1p_e = jnp.where(u == 1.0, e, jnp.log(u) * e / jnp.where(u == 1.0, 1.0, u - 1.0))
    return jnp.maximum(x, 0.0) + log1p_e


_softplus.defvjp(lambda x: (_softplus(x), x), lambda x, g: (g * jax.nn.sigmoid(x),))


def fn_norm_mod(x, g, shift, scale):
    return (_rms(x) * g * (1.0 + scale) + shift,)


def fn_rms(x, g):
    return (_rms(x) * g,)


def fn_resid(x, a, gate):
    return (x + gate * a,)


def fn_softplus(dt, bias):
    return (_softplus(dt + bias),)


def fn_ssd_out(yf, yb, xs, z, d_lane, g):
    y = yf + yb + d_lane * xs
    return (_rms(y * (z * jax.nn.sigmoid(z))) * g,)


def fn_swa_q(q, g, c, s):
    return (_rope_swa(_rms(q) * g, c, s),)


def fn_swa_kv(k, v, g, c, s):
    return (_rope_swa(_rms(k) * g, c, s), v)


def fn_mla_q(qn, qr, gn, gr, c, s):
    return (jnp.concatenate([_rms(qn) * gn, _rope_mla(_rms(qr, MLA_ROPE) * gr, c, s)], axis=-1),)


def fn_mla_kv(kn, v, kr, gn, gr, c, s):
    return (jnp.concatenate([_rms(kn) * gn, _rope_mla(_rms(kr, MLA_ROPE) * gr, c, s)], axis=-1), v)


def fn_merge(g1, g2, g3, p1, p2, p3):
    return (jax.nn.sigmoid(g1) * p1 + jax.nn.sigmoid(g2) * p2 + jax.nn.sigmoid(g3) * p3,)


def fn_swiglu(g, u):
    return (g * jax.nn.sigmoid(g) * u,)


ATTN_TILE = 256
NT_DIMS = (((1,), (1,)), ((), ()))


class AttnCfg:
    def __init__(self, hq, group, dq, dv, scale, window, has_sink, L, T, chunk, kv_block):
        self.hq, self.group, self.dq, self.dv, self.scale = hq, group, dq, dv, scale
        self.window, self.has_sink, self.L, self.T = window, has_sink, L, T
        self.chunk = _pick(L, (chunk, ATTN_TILE))
        self.ctx_chunk = T - L
        self.kv_block = kv_block
        self.q_block = kv_block * group
        assert L % ATTN_TILE == 0 and (T - L) % ATTN_TILE == 0 and L % self.chunk == 0
        assert (hq // group) % kv_block == 0
        if window is not None:
            assert (ATTN_TILE + 2 * window) % self.chunk == 0
            self.window_chunks = min((ATTN_TILE + 2 * window) // self.chunk, L // self.chunk)
            self.align = math.gcd(self.chunk, window)
        else:
            self.align = self.chunk


LOG2E = math.log2(math.e)


def _latent_chunks(cfg, r0):
    c = cfg.chunk
    if cfg.window is None:
        lo, n = 0, cfg.L // c
    else:
        n = cfg.window_chunks
        lo = jnp.clip(r0 - cfg.window, 0, cfg.L - n * c)
    return lo, n


def _visible(cfg, rows_q, rows_k):
    return jnp.logical_or(rows_k >= cfg.L, jnp.abs(rows_k - rows_q) <= cfg.window)


def flash_fwd(name, cfg, q, k, v, sink):
    T, tq, c = cfg.T, ATTN_TILE, cfg.chunk
    hq, g, dq, dv, hb, kb = cfg.hq, cfg.group, cfg.dq, cfg.dv, cfg.q_block, cfg.kv_block
    to_log2 = cfg.scale * LOG2E

    def body(*refs):
        if cfg.has_sink:
            q_ref, k_ref, v_ref, sink_ref, o_ref, lse_ref = refs
        else:
            q_ref, k_ref, v_ref, o_ref, lse_ref = refs
        q0 = pl.program_id(1) * tq
        qs = [q_ref[:, hh * dq:(hh + 1) * dq] for hh in range(hb)]
        lat_lo, lat_n = _latent_chunks(cfg, q0)
        n = jnp.where(q0 >= cfg.L, 0, lat_n)
        rows_q = q0 + lax.broadcasted_iota(jnp.int32, (tq, 1), 0)

        def start(t):
            return pl.multiple_of(lat_lo + jnp.minimum(t, lat_n - 1) * c, cfg.align)

        def logits(ks, size):
            return tuple(lax.dot_general(qs[hh], k_ref[pl.ds(ks, size), (hh // g) * dq:(hh // g + 1) * dq], NT_DIMS,
                                         preferred_element_type=F32) for hh in range(hb))

        def update(state, s_all, ks, size, masked):
            new_state = []
            for hh in range(hb):
                m, acc = state[hh]
                s = s_all[hh]
                if masked:
                    rows_k = ks + lax.broadcasted_iota(jnp.int32, (1, size), 1)
                    s = jnp.where(_visible(cfg, rows_q, rows_k), s, -jnp.inf)
                m_new = jnp.maximum(m, jnp.max(s, axis=-1, keepdims=True) * to_log2)
                alpha = jnp.exp2(m - m_new)
                p = jnp.exp2(s * to_log2 - m_new).astype(BF16)
                kh = hh // g
                v_ones = jnp.concatenate([v_ref[pl.ds(ks, size), kh * dv:(kh + 1) * dv], jnp.ones((size, dv), BF16)], axis=1)
                acc = alpha * acc + jnp.dot(p, v_ones, preferred_element_type=F32)
                new_state.append((m_new, acc))
            return tuple(new_state)

        def step(t, carry):
            state, s_all = carry
            s_next = logits(start(t + 1), c)
            return update(state, s_all, start(t), c, cfg.window is not None), s_next

        state = []
        for hh in range(hb):
            if cfg.has_sink:
                m0 = jnp.zeros((tq, 1), F32) + sink_ref[hh] * LOG2E
                l0 = jnp.ones((tq, dv), F32)
            else:
                m0 = jnp.full((tq, 1), -jnp.inf, F32)
                l0 = jnp.zeros((tq, dv), F32)
            state.append((m0, jnp.concatenate([jnp.zeros((tq, dv), F32), l0], axis=1)))
        state = update(tuple(state), logits(cfg.L, cfg.ctx_chunk), cfg.L, cfg.ctx_chunk, False)
        state, _ = lax.fori_loop(0, n, step, (state, logits(start(0), c)))
        for hh in range(hb):
            m, acc = state[hh]
            o_ref[:, hh * dv:(hh + 1) * dv] = acc[:, :dv] / acc[:, dv:]
            lse_ref[hh] = m + jnp.log2(acc[:, dv:dv + 1])

    in_specs = [pl.BlockSpec((tq, hb * dq), lambda h, i: (i, h)),
                pl.BlockSpec((T, kb * dq), lambda h, i: (0, h)),
                pl.BlockSpec((T, kb * dv), lambda h, i: (0, h))]
    ins = [q, k, v]
    if cfg.has_sink:
        in_specs.append(pl.BlockSpec((hb, 1, 1), lambda h, i: (h, 0, 0)))
        ins.append(sink)
    return pl.pallas_call(
        body, name=name, grid=(hq // hb, T // tq), in_specs=in_specs,
        out_specs=[pl.BlockSpec((tq, hb * dv), lambda h, i: (i, h)), pl.BlockSpec((hb, tq, 1), lambda h, i: (h, i, 0))],
        out_shape=[jax.ShapeDtypeStruct((T, hq * dv), F32), jax.ShapeDtypeStruct((hq, T, 1), F32)],
        compiler_params=_cparams(("parallel", "parallel")),
    )(*ins)


def flash_bwd_fused(name, cfg, q, k, v, o, do, lse, sink):
    T, L, tq, c, cc = cfg.T, cfg.L, ATTN_TILE, cfg.chunk, cfg.ctx_chunk
    hq, g, dq, dv = cfg.hq, cfg.group, cfg.dq, cfg.dv
    hk = hq // g
    nq = T // tq
    to_log2 = cfg.scale * LOG2E
    masked = cfg.window is not None

    def body(*refs):
        if cfg.has_sink:
            q_ref, k_ref, v_ref, o_ref, do_ref, lse_ref, sink_ref, dq_ref, dk_ref, dv_ref, dsink_ref = refs
        else:
            q_ref, k_ref, v_ref, o_ref, do_ref, lse_ref, dq_ref, dk_ref, dv_ref = refs
        i = pl.program_id(1)
        q0 = i * tq

        @pl.when(i == 0)
        def _():
            dk_ref[...] = jnp.zeros_like(dk_ref)
            dv_ref[...] = jnp.zeros_like(dv_ref)

        qs = [q_ref[:, hh * dq:(hh + 1) * dq] for hh in range(g)]
        dos = [do_ref[:, hh * dv:(hh + 1) * dv].astype(BF16) for hh in range(g)]
        lses = [lse_ref[hh] for hh in range(g)]
        deltas = [jnp.sum(do_ref[:, hh * dv:(hh + 1) * dv] * o_ref[:, hh * dv:(hh + 1) * dv], axis=-1, keepdims=True)
                  for hh in range(g)]
        if cfg.has_sink:
            parts = [-jnp.sum(jnp.exp2(sink_ref[hh] * LOG2E - lses[hh]) * deltas[hh], axis=0, keepdims=True)[None]
                     for hh in range(g)]
            _accumulate(dsink_ref, jnp.concatenate(parts, axis=0), i == 0)
        lat_lo, lat_n = _latent_chunks(cfg, q0)
        n = jnp.where(q0 >= L, 0, lat_n)
        rows_q = q0 + lax.broadcasted_iota(jnp.int32, (tq, 1), 0)

        def start(t):
            return pl.multiple_of(lat_lo + jnp.minimum(t, lat_n - 1) * c, cfg.align)

        def products(ks, size):
            kk, vv = k_ref[pl.ds(ks, size), :], v_ref[pl.ds(ks, size), :]
            return tuple((lax.dot_general(qs[hh], kk, NT_DIMS, preferred_element_type=F32),
                          lax.dot_general(dos[hh], vv, NT_DIMS, preferred_element_type=F32)) for hh in range(g))

        def update(accs, prods, ks, size, mask_it):
            new_accs, dv_part, dk_part = [], None, None
            for hh in range(g):
                s, dp = prods[hh]
                p = jnp.exp2(s * to_log2 - lses[hh])
                if mask_it:
                    rows_k = ks + lax.broadcasted_iota(jnp.int32, (1, size), 1)
                    p = jnp.where(_visible(cfg, rows_q, rows_k), p, 0.0)
                ds = (p * (dp - deltas[hh])).astype(BF16)
                dv_h = lax.dot_general(p.astype(BF16), dos[hh], TN_DIMS, preferred_element_type=F32)
                dk_h = lax.dot_general(ds, qs[hh], TN_DIMS, preferred_element_type=F32)
                dv_part = dv_h if dv_part is None else dv_part + dv_h
                dk_part = dk_h if dk_part is None else dk_part + dk_h
                new_accs.append(accs[hh] + jnp.dot(ds, k_ref[pl.ds(ks, size), :], preferred_element_type=F32))
            dv_ref[pl.ds(ks, size), :] += dv_part
            dk_ref[pl.ds(ks, size), :] += dk_part
            return tuple(new_accs)

        def step(t, carry):
            accs, prods = carry
            nxt = products(start(t + 1), c)
            return update(accs, prods, start(t), c, masked), nxt

        accs = update(tuple(jnp.zeros((tq, dq), F32) for _ in range(g)), products(L, cc), L, cc, False)
        accs, _ = lax.fori_loop(0, n, step, (accs, products(start(0), c)))
        for hh in range(g):
            dq_ref[:, hh * dq:(hh + 1) * dq] = accs[hh] * cfg.scale

        @pl.when(i == nq - 1)
        def _():
            dk_ref[...] = dk_ref[...] * cfg.scale

    head_tile = pl.BlockSpec((tq, g * dv), lambda h, i: (i, h))
    one = pl.BlockSpec((g, 1, 1), lambda h, i: (h, 0, 0))
    in_specs = [pl.BlockSpec((tq, g * dq), lambda h, i: (i, h)),
                pl.BlockSpec((T, dq), lambda h, i: (0, h)),
                pl.BlockSpec((T, dv), lambda h, i: (0, h)),
                head_tile, head_tile, pl.BlockSpec((g, tq, 1), lambda h, i: (h, i, 0))]
    ins = [q, k, v, o, do, lse]
    out_specs = [pl.BlockSpec((tq, g * dq), lambda h, i: (i, h)),
                 pl.BlockSpec((T, dq), lambda h, i: (0, h)),
                 pl.BlockSpec((T, dv), lambda h, i: (0, h))]
    out_shape = [jax.ShapeDtypeStruct((T, hq * dq), F32), jax.ShapeDtypeStruct((T, hk * dq), F32),
                 jax.ShapeDtypeStruct((T, hk * dv), F32)]
    if cfg.has_sink:
        in_specs.append(one)
        ins.append(sink)
        out_specs.append(one)
        out_shape.append(jax.ShapeDtypeStruct((hq, 1, 1), F32))
    return pl.pallas_call(body, name=name, grid=(hk, nq), in_specs=in_specs, out_specs=out_specs, out_shape=out_shape,
                          compiler_params=_cparams(("arbitrary", "arbitrary")))(*ins)


HALO = 8


def _conv_specs(tm, C, T):
    nb = tm // HALO
    last = T // HALO - 1
    return [pl.BlockSpec((HALO, C), lambda i: (jnp.maximum(i * nb - 1, 0), 0)),
            pl.BlockSpec((tm, C), lambda i: (i, 0)),
            pl.BlockSpec((HALO, C), lambda i: (jnp.minimum((i + 1) * nb, last), 0))]


def _extended(prev_ref, cur_ref, next_ref, i, tm, L, T):
    r0 = i * tm
    keep_prev = jnp.logical_and(r0 != 0, r0 != L).astype(F32)
    keep_next = jnp.logical_and(r0 + tm != L, r0 + tm != T).astype(F32)
    return jnp.concatenate([prev_ref[...] * keep_prev, cur_ref[...], next_ref[...] * keep_next], axis=0)


def _shift_rows(xe, d):
    n = xe.shape[0]
    return xe if d == 0 else pltpu.roll(xe, (-d) % n, 0)


def _conv_pre(xe, w_ref, b_ref):
    acc = b_ref[...] + w_ref[SSM_CONV // 2:SSM_CONV // 2 + 1, :] * xe
    for k in range(SSM_CONV):
        if k != SSM_CONV // 2:
            acc = acc + w_ref[k:k + 1, :] * _shift_rows(xe, k - SSM_CONV // 2)
    return acc


def conv_fwd(name, x, w, b, L, tm=ROW_TILE):
    T, C = x.shape

    def body(xp, xc, xn, w_ref, b_ref, o_ref):
        xe = _extended(xp, xc, xn, pl.program_id(0), tm, L, T)
        pre = _conv_pre(xe, w_ref, b_ref)[HALO:HALO + tm]
        o_ref[...] = pre * jax.nn.sigmoid(pre)

    full = lambda a: pl.BlockSpec(a.shape, lambda i: (0, 0))
    return pl.pallas_call(body, name=name, grid=(T // tm,), in_specs=_conv_specs(tm, C, T) + [full(w), full(b)],
                          out_specs=pl.BlockSpec((tm, C), lambda i: (i, 0)), out_shape=jax.ShapeDtypeStruct((T, C), F32),
                          compiler_params=_cparams(("parallel",)))(x, x, x, w, b)


def conv_bwd(name, x, w, b, gu, L, tm=ROW_TILE):
    T, C = x.shape

    def body(xp, xc, xn, gp, gc, gn, w_ref, b_ref, dx_ref, dw_ref, db_ref):
        i = pl.program_id(0)
        xe = _extended(xp, xc, xn, i, tm, L, T)
        ge = _extended(gp, gc, gn, i, tm, L, T)
        pre = _conv_pre(xe, w_ref, b_ref)
        sg = jax.nn.sigmoid(pre)
        gpre = ge * (sg * (1.0 + pre * (1.0 - sg)))
        half = SSM_CONV // 2
        dx = jnp.zeros((tm, C), F32)
        rows = []
        x_rows = xe[HALO:HALO + tm]
        for k in range(SSM_CONV):
            g_k = _shift_rows(gpre, half - k)[HALO:HALO + tm]
            dx = dx + w_ref[k:k + 1, :] * g_k
            rows.append(jnp.sum(g_k * x_rows, axis=0, keepdims=True))
        dx_ref[...] = dx.astype(dx_ref.dtype)
        rows += [jnp.zeros((1, C), F32)] * (8 - SSM_CONV)
        _accumulate(dw_ref, jnp.concatenate(rows, axis=0), i == 0)
        _accumulate(db_ref, jnp.sum(gpre[HALO:HALO + tm], axis=0, keepdims=True), i == 0)

    full = lambda a: pl.BlockSpec(a.shape, lambda i: (0, 0))
    return pl.pallas_call(
        body, name=name, grid=(T // tm,), in_specs=_conv_specs(tm, C, T) * 2 + [full(w), full(b)],
        out_specs=[pl.BlockSpec((tm, C), lambda i: (i, 0)), pl.BlockSpec((8, C), lambda i: (0, 0)), pl.BlockSpec((1, C), lambda i: (0, 0))],
        out_shape=[jax.ShapeDtypeStruct((T, C), BF16), jax.ShapeDtypeStruct((8, C), F32), jax.ShapeDtypeStruct((1, C), F32)],
        compiler_params=_cparams(("arbitrary",)))(x, x, x, gu, gu, gu, w, b)


SSM_PAIRS = SSM_HEADS // 2
TN_DIMS = (((0,), (0,)), ((), ()))
HIGHEST = lax.Precision.HIGHEST


def _ssd_chunk(direction, xps, bs, cs, dt_col, dt_row, alog_row, alog_col, hps):
    Q = SSM_CHUNK
    da_col = dt_col * (-jnp.exp(alog_row))
    da_row = dt_row * (-jnp.exp(alog_col))
    ii = lax.broadcasted_iota(jnp.int32, (Q, Q), 0)
    jj = lax.broadcasted_iota(jnp.int32, (Q, Q), 1)
    tri = (ii >= jj) if direction == 0 else (ii <= jj)
    trif = tri.astype(F32)
    acs_col = jnp.dot(trif, da_col, precision=HIGHEST, preferred_element_type=F32)
    acs_row = lax.dot_general(da_row, trif, NT_DIMS, precision=HIGHEST, preferred_element_type=F32)
    tot_col = jnp.sum(da_col, axis=0, keepdims=True)
    lane16 = lax.broadcasted_iota(jnp.int32, (1, SSM_HEADS), 1)
    sub16 = lax.broadcasted_iota(jnp.int32, (SSM_HEADS, 1), 0)
    low = lax.broadcasted_iota(jnp.int32, (1, 2 * SSM_HEAD_DIM), 1) < SSM_HEAD_DIM

    def col(v, h):
        return jnp.sum(v * (lane16 == h).astype(F32), axis=1, keepdims=True)

    def row(v, h):
        return jnp.sum(v * (sub16 == h).astype(F32), axis=0, keepdims=True)

    ys, hos = [], []
    pairs_per_group = SSM_PAIRS // SSM_GROUPS
    for g in range(SSM_GROUPS):
        bb, cb16 = bs[g].astype(BF16), cs[g].astype(BF16)
        cb = lax.dot_general(cb16, bb, NT_DIMS, preferred_element_type=F32)
        for pp in range(pairs_per_group):
            p = g * pairs_per_group + pp
            h0, h1 = 2 * p, 2 * p + 1
            ac0, ac1 = col(acs_col, h0), col(acs_col, h1)
            seg0 = jnp.exp(jnp.where(tri, ac0 - row(acs_row, h0), -jnp.inf))
            seg1 = jnp.exp(jnp.where(tri, ac1 - row(acs_row, h1), -jnp.inf))
            dt_l = jnp.where(low, col(dt_col, h0), col(dt_col, h1))
            ac_l = jnp.where(low, ac0, ac1)
            tot_l = jnp.where(low, col(tot_col, h0), col(tot_col, h1))
            xdt = xps[p] * dt_l
            y = (jnp.dot((cb * seg0).astype(BF16), jnp.where(low, xdt, 0.0).astype(BF16), preferred_element_type=F32)
                 + jnp.dot((cb * seg1).astype(BF16), jnp.where(low, 0.0, xdt).astype(BF16), preferred_element_type=F32))
            y = y + jnp.dot(cb16, hps[p].astype(BF16), preferred_element_type=F32) * jnp.exp(ac_l)
            st = lax.dot_general(bb, (xdt * jnp.exp(tot_l - ac_l)).astype(BF16), TN_DIMS, preferred_element_type=F32)
            ys.append(y)
            hos.append(hps[p] * jnp.exp(tot_l) + st)
    return tuple(ys), tuple(hos)


def _ssd_chunk_of(direction, step, ncl, ncc):
    if direction == 0:
        return jnp.where(step < ncc, ncl + step, step - ncc)
    return jnp.where(step < ncc, ncl + ncc - 1 - step, ncl - 1 - (step - ncc))


def _ssd_load(u_ref):
    Q = SSM_CHUNK
    xps = tuple(u_ref[:, LANES * p:LANES * (p + 1)] for p in range(SSM_PAIRS))
    bs = tuple(u_ref[:, SSM_INNER + SSM_STATE * g:SSM_INNER + SSM_STATE * (g + 1)] for g in range(SSM_GROUPS))
    c0 = SSM_INNER + SSM_GROUPS * SSM_STATE
    cs = tuple(u_ref[:, c0 + SSM_STATE * g:c0 + SSM_STATE * (g + 1)] for g in range(SSM_GROUPS))
    return xps, bs, cs


def ssd_fwd(name, direction, u, dt, dt_t, alog_row, alog_col, L):
    T = u.shape[0]
    Q, N = SSM_CHUNK, SSM_STATE
    ncl, ncc = L // Q, (T - L) // Q
    nc = ncl + ncc
    cm = lambda s: _ssd_chunk_of(direction, s, ncl, ncc)

    def body(u_ref, dt_ref, dtt_ref, ar_ref, ac_ref, y_ref, hin_ref, state):
        @pl.when(pl.program_id(0) == 0)
        def _():
            state[...] = jnp.zeros_like(state)

        xps, bs, cs = _ssd_load(u_ref)
        hps = tuple(state[p] for p in range(SSM_PAIRS))
        for p in range(SSM_PAIRS):
            hin_ref[0, p] = hps[p]
        ys, hos = _ssd_chunk(direction, xps, bs, cs, dt_ref[...], dtt_ref[...], ar_ref[...], ac_ref[...], hps)
        for p in range(SSM_PAIRS):
            y_ref[:, LANES * p:LANES * (p + 1)] = ys[p]
            state[p] = hos[p]

    return pl.pallas_call(
        body, name=name, grid=(nc,),
        in_specs=[pl.BlockSpec((Q, SSM_CONV_DIM), lambda s: (cm(s), 0)),
                  pl.BlockSpec((Q, SSM_HEADS), lambda s: (cm(s), 0)),
                  pl.BlockSpec((SSM_HEADS, Q), lambda s: (0, cm(s))),
                  pl.BlockSpec((1, SSM_HEADS), lambda s: (0, 0)),
                  pl.BlockSpec((SSM_HEADS, 1), lambda s: (0, 0))],
        out_specs=[pl.BlockSpec((Q, SSM_INNER), lambda s: (cm(s), 0)),
                   pl.BlockSpec((1, SSM_PAIRS, N, LANES), lambda s: (cm(s), 0, 0, 0))],
        out_shape=[jax.ShapeDtypeStruct((T, SSM_INNER), F32), jax.ShapeDtypeStruct((nc, SSM_PAIRS, N, LANES), F32)],
        scratch_shapes=[pltpu.VMEM((SSM_PAIRS, N, LANES), F32)],
        compiler_params=_cparams(("arbitrary",)),
    )(u, dt, dt_t, alog_row, alog_col)


def ssd_bwd(name, direction, u, dt, dt_t, alog_row, alog_col, hin, dy, L, add_x=None, add_u=None):
    T = u.shape[0]
    Q, N = SSM_CHUNK, SSM_STATE
    ncl, ncc = L // Q, (T - L) // Q
    nc = ncl + ncc
    cm = lambda s: _ssd_chunk_of(direction, nc - 1 - s, ncl, ncc)
    n_add = (add_x is not None) + (add_u is not None)

    def body(*refs):
        u_ref, dt_ref, dtt_ref, ar_ref, ac_ref, hin_ref, dy_ref = refs[:7]
        add_refs = refs[7:7 + n_add]
        du_ref, ddt_ref, ddtt_ref, dar_ref, dac_ref, dstate = refs[7 + n_add:]
        first = pl.program_id(0) == 0

        @pl.when(first)
        def _():
            dstate[...] = jnp.zeros_like(dstate)

        xps, bs, cs = _ssd_load(u_ref)
        hps = tuple(hin_ref[0, p] for p in range(SSM_PAIRS))
        _, vjp = jax.vjp(functools.partial(_ssd_chunk, direction), xps, bs, cs, dt_ref[...], dtt_ref[...], ar_ref[...],
                         ac_ref[...], hps)
        dys = tuple(dy_ref[:, LANES * p:LANES * (p + 1)] for p in range(SSM_PAIRS))
        dhs = tuple(dstate[p] for p in range(SSM_PAIRS))
        gx, gb, gc, gdt, gdtt, gar, gac, ghp = vjp((dys, dhs))
        parts = list(gx) + list(gb) + list(gc)
        du = jnp.concatenate(parts, axis=1)
        k = 0
        if add_x is not None:
            du = du + jnp.concatenate([add_refs[k][...], jnp.zeros((Q, SSM_CONV_DIM - SSM_INNER), F32)], axis=1)
            k += 1
        if add_u is not None:
            du = du + add_refs[k][...]
        du_ref[...] = du
        ddt_ref[...] = gdt
        ddtt_ref[...] = gdtt
        _accumulate(dar_ref, gar, first)
        _accumulate(dac_ref, gac, first)
        for p in range(SSM_PAIRS):
            dstate[p] = ghp[p]

    in_specs = [pl.BlockSpec((Q, SSM_CONV_DIM), lambda s: (cm(s), 0)),
                pl.BlockSpec((Q, SSM_HEADS), lambda s: (cm(s), 0)),
                pl.BlockSpec((SSM_HEADS, Q), lambda s: (0, cm(s))),
                pl.BlockSpec((1, SSM_HEADS), lambda s: (0, 0)),
                pl.BlockSpec((SSM_HEADS, 1), lambda s: (0, 0)),
                pl.BlockSpec((1, SSM_PAIRS, N, LANES), lambda s: (cm(s), 0, 0, 0)),
                pl.BlockSpec((Q, SSM_INNER), lambda s: (cm(s), 0))]
    ins = [u, dt, dt_t, alog_row, alog_col, hin, dy]
    if add_x is not None:
        in_specs.append(pl.BlockSpec((Q, SSM_INNER), lambda s: (cm(s), 0)))
        ins.append(add_x)
    if add_u is not None:
        in_specs.append(pl.BlockSpec((Q, SSM_CONV_DIM), lambda s: (cm(s), 0)))
        ins.append(add_u)
    return pl.pallas_call(
        body, name=name, grid=(nc,), in_specs=in_specs,
        out_specs=[pl.BlockSpec((Q, SSM_CONV_DIM), lambda s: (cm(s), 0)),
                   pl.BlockSpec((Q, SSM_HEADS), lambda s: (cm(s), 0)),
                   pl.BlockSpec((SSM_HEADS, Q), lambda s: (0, cm(s))),
                   pl.BlockSpec((1, SSM_HEADS), lambda s: (0, 0)),
                   pl.BlockSpec((SSM_HEADS, 1), lambda s: (0, 0))],
        out_shape=[jax.ShapeDtypeStruct((T, SSM_CONV_DIM), F32), jax.ShapeDtypeStruct((T, SSM_HEADS), F32),
                   jax.ShapeDtypeStruct((SSM_HEADS, T), F32), jax.ShapeDtypeStruct((1, SSM_HEADS), F32),
                   jax.ShapeDtypeStruct((SSM_HEADS, 1), F32)],
        scratch_shapes=[pltpu.VMEM((SSM_PAIRS, N, LANES), F32)],
        compiler_params=_cparams(("arbitrary",)),
    )(*ins)


PEER_MASKS = (1, 2, 4, 3, 5, 6, 7)
N_PEERS = len(PEER_MASKS)
MESH_IDS = pl.DeviceIdType.MESH


def _my_index():
    return lax.axis_index("x") * 4 + lax.axis_index("y") * 2 + lax.axis_index("c")


def _coords(idx):
    return (idx // 4, (idx // 2) % 2, idx % 2)


def all_gather_hbm(name, arrays):
    n = len(arrays)
    chip_masks = (4, 2, 6)

    def body(*refs):
        ins, outs = refs[:n], refs[n:2 * n]
        send_sems, recv_sems, local_sems = refs[2 * n:]
        me = _my_index()
        sibling = me ^ 1

        def copy(a, k, block, to, src=None):
            return pltpu.make_async_remote_copy(
                src_ref=outs[a].at[block] if src is None else src, dst_ref=outs[a].at[block],
                send_sem=send_sems.at[a * N_PEERS + k], recv_sem=recv_sems.at[a * N_PEERS + k],
                device_id=_coords(to), device_id_type=MESH_IDS)

        started, own = [], []
        for a in range(n):
            local = pltpu.make_async_copy(ins[a], outs[a].at[me], local_sems.at[a])
            local.start()
            own.append(local)
            first = [copy(a, 0, me, sibling, src=ins[a])] + [copy(a, 1 + j, me, me ^ m, src=ins[a]) for j, m in enumerate(chip_masks)]
            for cp in first:
                cp.start()
            started += first
        for a in range(n):
            for j, m in enumerate(chip_masks):
                copy(a, 1 + j, me ^ m, me).wait_recv()
                fwd = copy(a, 4 + j, me ^ m, sibling)
                fwd.start()
                started.append(fwd)
        for a in range(n):
            copy(a, 0, sibling, me).wait_recv()
            for j, m in enumerate(chip_masks):
                copy(a, 4 + j, sibling ^ m, me).wait_recv()
        for cp in started:
            cp.wait_send()
        for cp in own:
            cp.wait()

    any_spec = pl.BlockSpec(memory_space=pl.ANY)
    return pl.pallas_call(
        body, name=name, in_specs=[any_spec] * n, out_specs=[any_spec] * n,
        out_shape=[jax.ShapeDtypeStruct((N_DEV,) + a.shape, a.dtype) for a in arrays],
        scratch_shapes=[pltpu.SemaphoreType.DMA((n * N_PEERS,)), pltpu.SemaphoreType.DMA((n * N_PEERS,)),
                        pltpu.SemaphoreType.DMA((n,))],
    )(*arrays)


N_CHIPS = N_DEV // 2


def exchange_sibling(name, arrays):
    n = len(arrays)

    def body(*refs):
        ins, outs = refs[:n], refs[n:2 * n]
        send_sems, recv_sems = refs[2 * n:]
        me = _my_index()
        other_core = 1 - me % 2
        copies = []
        for a in range(n):
            for chip in range(N_CHIPS):
                cp = pltpu.make_async_remote_copy(src_ref=ins[a].at[chip, other_core], dst_ref=outs[a].at[chip],
                                                  send_sem=send_sems.at[a * N_CHIPS + chip], recv_sem=recv_sems.at[a * N_CHIPS + chip],
                                                  device_id=_coords(me ^ 1), device_id_type=MESH_IDS)
                cp.start()
                copies.append(cp)
        for cp in copies:
            cp.wait()

    any_spec = pl.BlockSpec(memory_space=pl.ANY)
    return pl.pallas_call(
        body, name=name, in_specs=[any_spec] * n, out_specs=[any_spec] * n,
        out_shape=[jax.ShapeDtypeStruct((N_CHIPS,) + a.shape[2:], a.dtype) for a in arrays],
        scratch_shapes=[pltpu.SemaphoreType.DMA((n * N_CHIPS,)), pltpu.SemaphoreType.DMA((n * N_CHIPS,))],
    )(*arrays)


def exchange_chips(name, arrays):
    n = len(arrays)
    n_other = N_CHIPS - 1

    def body(*refs):
        ins, outs = refs[:n], refs[n:2 * n]
        send_sems, recv_sems, local_sems = refs[2 * n:]
        me = _my_index()
        chip, core = me // 2, me % 2
        copies = []
        for a in range(n):
            local = pltpu.make_async_copy(ins[a].at[chip], outs[a].at[chip], local_sems.at[a])
            local.start()
            copies.append(local)
            for k in range(n_other):
                peer_chip = chip ^ (k + 1)
                cp = pltpu.make_async_remote_copy(src_ref=ins[a].at[peer_chip], dst_ref=outs[a].at[chip],
                                                  send_sem=send_sems.at[a * n_other + k], recv_sem=recv_sems.at[a * n_other + k],
                                                  device_id=_coords(peer_chip * 2 + core), device_id_type=MESH_IDS)
                cp.start()
                copies.append(cp)
        for cp in copies:
            cp.wait()

    any_spec = pl.BlockSpec(memory_space=pl.ANY)
    return pl.pallas_call(
        body, name=name, in_specs=[any_spec] * n, out_specs=[any_spec] * n,
        out_shape=[jax.ShapeDtypeStruct(a.shape, a.dtype) for a in arrays],
        scratch_shapes=[pltpu.SemaphoreType.DMA((n * n_other,)), pltpu.SemaphoreType.DMA((n * n_other,)),
                        pltpu.SemaphoreType.DMA((n,))],
    )(*arrays)


def pair_sum(name, both, got):
    P, _, R, C = both.shape
    tr = _row_tile(R, C, 4)

    def body(a_ref, b_ref, o_ref):
        core = lax.axis_index("c")
        o_ref[...] = (a_ref[0, core].astype(F32) + b_ref[0].astype(F32)).astype(BF16)[None]

    return pl.pallas_call(
        body, name=name, grid=(P, R // tr),
        in_specs=[pl.BlockSpec((1, 2, tr, C), lambda p, i: (p, 0, i, 0)), pl.BlockSpec((1, tr, C), lambda p, i: (p, i, 0))],
        out_specs=pl.BlockSpec((1, tr, C), lambda p, i: (p, i, 0)),
        out_shape=jax.ShapeDtypeStruct((P, R, C), BF16), compiler_params=_cparams(("parallel", "parallel")))(both, got)


def all_gather_vmem(name, v):
    def body(v_ref, out_ref, send_sems, recv_sems):
        me = _my_index()
        out_ref[me] = v_ref[...]
        copies = []
        for k, mask in enumerate(PEER_MASKS):
            cp = pltpu.make_async_remote_copy(src_ref=v_ref, dst_ref=out_ref.at[me], send_sem=send_sems.at[k],
                                              recv_sem=recv_sems.at[k], device_id=_coords(me ^ mask), device_id_type=MESH_IDS)
            cp.start()
            copies.append(cp)
        for cp in copies:
            cp.wait()

    vm = pl.BlockSpec(memory_space=pltpu.VMEM)
    return pl.pallas_call(
        body, name=name, in_specs=[vm], out_specs=vm, out_shape=jax.ShapeDtypeStruct((N_DEV,) + v.shape, v.dtype),
        scratch_shapes=[pltpu.SemaphoreType.DMA((N_PEERS,)), pltpu.SemaphoreType.DMA((N_PEERS,))],
    )(v)


ROW_KERNEL_VMEM_BUDGET = 24 * 1024 * 1024


def _row_tile(rows, cols, bufs):
    budget = ROW_KERNEL_VMEM_BUDGET // (bufs * 2 * 4 * max(cols, LANES))
    if rows <= budget:
        return rows
    for t in range(budget - budget % 16, 15, -16):
        if rows % t == 0:
            return t
    return rows


def sum_parts(name, parts):
    P, R, C = parts.shape
    tr = _row_tile(R, C, P + 1)

    def body(p_ref, o_ref):
        acc = p_ref[0].astype(F32)
        for s in range(1, P):
            acc = acc + p_ref[s].astype(F32)
        o_ref[...] = acc

    return pl.pallas_call(body, name=name, grid=(R // tr,), in_specs=[pl.BlockSpec((P, tr, C), lambda i: (0, i, 0))],
                          out_specs=pl.BlockSpec((tr, C), lambda i: (i, 0)), out_shape=jax.ShapeDtypeStruct((R, C), F32),
                          compiler_params=_cparams(("parallel",)))(parts)


def adamw(name, w, g, m, v):
    R, C = w.shape
    tr = _row_tile(R, C, 7)

    def body(w_ref, g_ref, m_ref, v_ref, d_ref, nm_ref, nv_ref):
        g = g_ref[...]
        nm = ADAM_B1 * m_ref[...] + (1.0 - ADAM_B1) * g
        nv = ADAM_B2 * v_ref[...] + (1.0 - ADAM_B2) * (g * g)
        m_hat = nm / (1.0 - ADAM_B1 ** ADAM_STEP)
        v_hat = nv / (1.0 - ADAM_B2 ** ADAM_STEP)
        d_ref[...] = -ADAM_LR * (m_hat / (jnp.sqrt(v_hat) + ADAM_EPS) + ADAM_WD * w_ref[...])
        nm_ref[...] = nm
        nv_ref[...] = nv

    spec = pl.BlockSpec((tr, C), lambda i: (i, 0))
    return pl.pallas_call(body, name=name, grid=(R // tr,), in_specs=[spec] * 4, out_specs=[spec] * 3,
                          out_shape=[jax.ShapeDtypeStruct((R, C), F32)] * 3, compiler_params=_cparams(("parallel",)))(w, g, m, v)


def loss_and_grad(name, x, target, L, tm=ROW_TILE):
    T, D = x.shape
    nlt = L // tm

    def body(x_ref, t_ref, loss_ref, dx_ref):
        i = pl.program_id(0)
        err = jnp.where(i < nlt, x_ref[...] - t_ref[...], 0.0)
        dx_ref[...] = err * (1.0 / D)
        part = 0.5 * jnp.sum(jnp.sum(err * err, axis=1, keepdims=True), axis=0, keepdims=True) * (1.0 / D)
        _accumulate(loss_ref, part, i == 0)

    return pl.pallas_call(
        body, name=name, grid=(T // tm,),
        in_specs=[pl.BlockSpec((tm, D), lambda i: (i, 0)), pl.BlockSpec((tm, D), lambda i: (jnp.minimum(i, nlt - 1), 0))],
        out_specs=[pl.BlockSpec((1, 1), lambda i: (0, 0)), pl.BlockSpec((tm, D), lambda i: (i, 0))],
        out_shape=[jax.ShapeDtypeStruct((1, 1), F32), jax.ShapeDtypeStruct((T, D), F32)],
        compiler_params=_cparams(("arbitrary",)))(x, target)


def small_fwd(name, fn, arrays, out_shapes):
    def body(*refs):
        res = fn(*[r[...] for r in refs[:len(arrays)]])
        for o_ref, r in zip(refs[len(arrays):], res):
            o_ref[...] = r

    return pl.pallas_call(body, name=name, out_shape=[jax.ShapeDtypeStruct(s, F32) for s in out_shapes])(*arrays)


def small_bwd(name, fn, arrays, cts):
    n = len(arrays)

    def body(*refs):
        _, vjp = jax.vjp(lambda *a: tuple(fn(*a)), *[r[...] for r in refs[:n]])
        grads = vjp(tuple(r[...] for r in refs[n:n + len(cts)]))
        for o_ref, g in zip(refs[n + len(cts):], grads):
            o_ref[...] = g

    return pl.pallas_call(body, name=name, out_shape=[jax.ShapeDtypeStruct(a.shape, F32) for a in arrays])(*arrays, *cts)


def fn_silu(x):
    return (x * jax.nn.sigmoid(x),)


FWD_NAMES = ["x", "c", "ctx", "c_ctx", "w_mod", "b_mod", "norm1_g", "norm2_g", "w_in", "ssm_conv_w", "ssm_conv_b",
             "ssm_dt_bias", "ssm_a_log", "ssm_d", "ssm_norm_g", "swa_q_norm_g", "swa_k_norm_g", "swa_sink", "mla_q_lat_g",
             "mla_kv_lat_g", "w_mla_uq", "w_mla_ukv", "mla_q_norm_g", "mla_k_norm_g", "w_p_ssm", "w_p_swa", "w_p_mla",
             "w_out", "w_ffn_in", "w_ffn_out"]
WEIGHT_NAMES = FWD_NAMES[3:]
GATHERED = ["w_in", "w_mla_uq", "w_mla_ukv", "w_p_ssm", "w_p_swa", "w_p_mla", "w_out", "w_ffn_in", "w_ffn_out"]
COLUMN_SHARDED = ("w_in", "w_mla_uq", "w_mla_ukv", "w_ffn_in")
REPLICATED = ["c_ctx", "b_mod", "norm1_g", "norm2_g", "ssm_conv_b", "ssm_dt_bias", "ssm_a_log", "ssm_d", "ssm_norm_g",
              "swa_q_norm_g", "swa_k_norm_g", "swa_sink", "mla_q_lat_g", "mla_kv_lat_g", "mla_q_norm_g", "mla_k_norm_g"]
IN_SEGS = [("xbc", SSM_CONV_DIM), ("dt", 2 * SSM_HEADS), ("ks", SWA_KV_HEADS * SWA_HEAD_DIM), ("vs", SWA_KV_HEADS * SWA_HEAD_DIM),
           ("ckv", MLA_KV_RANK), ("kr", MLA_ROPE), ("z", SSM_INNER), ("qs", SWA_Q_HEADS * SWA_HEAD_DIM), ("cq", MLA_Q_RANK),
           ("g1", None), ("g2", None), ("g3", None)]


def _pack(vectors, multiple):
    flat = jnp.concatenate([v.reshape(-1) for v in vectors])
    pad = (-flat.shape[0]) % multiple
    return jnp.pad(flat, (0, pad)).reshape(-1, LANES)


def _unpack(packed, shapes):
    flat, out, off = packed.reshape(-1), [], 0
    for s in shapes:
        n = int(np.prod(s))
        out.append(flat[off:off + n].reshape(s))
        off += n
    return out


def _rope_tables(L, T, rot_dim):
    nf = rot_dim // 4
    inv = jnp.power(ROPE_BASE, -jnp.arange(nf, dtype=F32) / nf)
    r, col = jnp.meshgrid(jnp.arange(L // GRID_W, dtype=F32), jnp.arange(GRID_W, dtype=F32), indexing="ij")
    ang = jnp.stack([r.reshape(-1)[:, None] * inv, col.reshape(-1)[:, None] * inv], axis=1)
    cos, sin = jnp.cos(ang), jnp.sin(ang)
    c = jnp.concatenate([cos[:, 0], cos[:, 0], cos[:, 1], cos[:, 1]], axis=1)
    s = jnp.concatenate([-sin[:, 0], sin[:, 0], -sin[:, 1], sin[:, 1]], axis=1)
    c = jnp.pad(c, ((0, T - L), (0, LANES - rot_dim)), constant_values=1.0)
    s = jnp.pad(s, ((0, T - L), (0, LANES - rot_dim)))
    return c, s


def _pad_rows(a, rows):
    return jnp.pad(a, ((0, rows - a.shape[0]), (0, 0)))


def _row(w, per_head=0, off=0, diff=True):
    return ("row", w, per_head, off, diff)


PAR, PAR_ND = ("par", True), ("par", False)
GRP = ("grp", True)


def kernel(*args):
    n_fwd, n_w = len(FWD_NAMES), len(WEIGHT_NAMES)
    inp = dict(zip(FWD_NAMES, args[:n_fwd]))
    loss_target = args[n_fwd]
    mom_m = dict(zip(WEIGHT_NAMES, args[n_fwd + 1:n_fwd + 1 + n_w]))
    mom_v = dict(zip(WEIGHT_NAMES, args[n_fwd + 1 + n_w:]))

    x, ctx = inp["x"][0], inp["ctx"][0]
    L, D = x.shape
    n_ctx = ctx.shape[0]
    T = L + n_ctx
    depth = inp["w_in"].shape[0]
    me = _my_index()
    in_widths = [w if w is not None else D for _, w in IN_SEGS]
    in_offs = np.concatenate([[0], np.cumsum(in_widths)]).tolist()
    ffn_h = inp["w_ffn_out"].shape[1] * N_DEV
    cfg_swa = AttnCfg(SWA_Q_HEADS, SWA_Q_HEADS // SWA_KV_HEADS, SWA_HEAD_DIM, SWA_HEAD_DIM, SWA_HEAD_DIM ** -0.5, SWA_WINDOW,
                      True, L, T, 256, 1)
    cfg_mla = AttnCfg(MLA_HEADS, 1, MLA_QK_PAD, MLA_V, MLA_QK ** -0.5, None, False, L, T, 1024, 2)
    cfg_mla_bwd = AttnCfg(MLA_HEADS, 1, MLA_QK_PAD, MLA_V, MLA_QK ** -0.5, None, False, L, T, 2048, 1)

    def rf(name, fn, descs, arrays, outs, heads=1):
        return rowop_fwd(name, fn, descs, arrays, outs, T, L, heads=heads)

    def rb(name, fn, descs, arrays, outs, cts, heads=1, add=None, to_matmul=()):
        return rowop_bwd(name, fn, descs, arrays, outs, cts, T, L, heads=heads, add=add, to_matmul=to_matmul)

    local = []
    for n in GATHERED:
        w = inp[n]
        local.append((jnp.swapaxes(w, 1, 2) if n in COLUMN_SHARDED else w).astype(BF16))
    gathered = dict(zip(GATHERED, all_gather_hbm("gather_weights", local)))

    def full(n, l):
        g = gathered[n][:, l]
        return g.reshape(g.shape[0] * g.shape[1], g.shape[2])

    def layer_weights(l):
        wt = {}
        w_in_t = full("w_in", l)
        for (sn, _), o, w in zip(IN_SEGS, in_offs, in_widths):
            seg = w_in_t[o:o + w]
            wt[sn] = _pad_rows(seg, LANES) if sn == "kr" else seg
        uq = full("w_mla_uq", l).reshape(MLA_HEADS, MLA_QK, MLA_Q_RANK)
        wt["uqn"] = uq[:, :MLA_NOPE].reshape(MLA_HEADS * MLA_NOPE, MLA_Q_RANK)
        wt["uqr"] = jnp.pad(uq[:, MLA_NOPE:], ((0, 0), (0, LANES - MLA_ROPE), (0, 0))).reshape(MLA_HEADS * LANES, MLA_Q_RANK)
        ukv = full("w_mla_ukv", l).reshape(MLA_HEADS, MLA_NOPE + MLA_V, MLA_KV_RANK)
        wt["uk"] = ukv[:, :MLA_NOPE].reshape(MLA_HEADS * MLA_NOPE, MLA_KV_RANK)
        wt["uv"] = ukv[:, MLA_NOPE:].reshape(MLA_HEADS * MLA_V, MLA_KV_RANK)
        for n in ("w_p_ssm", "w_p_swa", "w_p_mla", "w_out", "w_ffn_out"):
            wt[n] = full(n, l)
        ffn_in_t = full("w_ffn_in", l)
        wt["fg"], wt["fu"] = ffn_in_t[:ffn_h], ffn_in_t[ffn_h:]
        return wt

    def layer_params(l):
        p = {}
        for n in ("norm1_g", "norm2_g", "ssm_conv_b", "ssm_norm_g", "swa_q_norm_g", "swa_k_norm_g", "mla_q_lat_g", "mla_kv_lat_g"):
            p[n] = inp[n][l][None]
        p["dt_bias"] = inp["ssm_dt_bias"][l].reshape(1, 2 * SSM_HEADS)
        p["alog_row"] = [inp["ssm_a_log"][l][d][None] for d in range(2)]
        p["alog_col"] = [inp["ssm_a_log"][l][d][:, None] for d in range(2)]
        p["d_lane"] = jnp.repeat(inp["ssm_d"][l], SSM_HEAD_DIM)[None]
        p["sink"] = inp["swa_sink"][l].reshape(SWA_Q_HEADS, 1, 1)
        for n, key in (("mla_q_norm_g", "gq"), ("mla_k_norm_g", "gk")):
            g = inp[n][l]
            p[key + "n"] = g[:MLA_NOPE][None]
            p[key + "r"] = jnp.pad(g[MLA_NOPE:], (0, LANES - MLA_ROPE))[None]
        return p

    conv_local = _pack([inp["ssm_conv_w"]], 8 * LANES)
    conv_all = all_gather_vmem("gather_conv_w", conv_local)
    cw = inp["ssm_conv_w"].shape
    conv_full = conv_all.reshape(N_DEV, -1)[:, :cw[0] * cw[1] * cw[2]].reshape(N_DEV, cw[0], cw[1], cw[2])
    conv_full = jnp.moveaxis(conv_full, 0, 2).reshape(cw[0], cw[1], N_DEV * cw[2])
    conv_w8 = jnp.pad(conv_full, ((0, 0), (0, 8 - cw[1]), (0, 0)))

    silu_c, silu_cc = small_fwd("silu_c", lambda a, b: fn_silu(a) + fn_silu(b), [inp["c"], inp["c_ctx"][None]], [(1, D), (1, D)])
    silu_all = all_gather_vmem("gather_silu_c", silu_c.reshape(D // LANES, LANES)).reshape(N_DEV, D)
    S_rows = 2 * N_DEV
    S_mat = jnp.concatenate([silu_all, silu_cc, jnp.zeros((S_rows - N_DEV - 1, D), F32)], axis=0)
    mod_cols = inp["w_mod"].shape[2]
    mods_local = []
    for l in range(depth):
        bias = lax.dynamic_slice(inp["b_mod"][l], (me * mod_cols,), (mod_cols,))
        mods_local.append(matmul(S_mat, inp["w_mod"][l], "nn", f"mod{l}", add=jnp.broadcast_to(bias[None], (S_rows, mod_cols))))
    mods_all = all_gather_vmem("gather_mods", jnp.stack(mods_local).reshape(-1, LANES))
    mods_all = jnp.moveaxis(mods_all.reshape(N_DEV, depth, S_rows, mod_cols), 0, 2).reshape(depth, S_rows, N_DEV * mod_cols)
    mods_lat = lax.dynamic_slice(mods_all, (0, me, 0), (depth, 1, N_DEV * mod_cols))[:, 0]
    mods_ctx = mods_all[:, N_DEV]

    def layer_mods(l):
        return [jnp.stack([mods_lat[l, j * D:(j + 1) * D], mods_ctx[l, j * D:(j + 1) * D]])[:, None] for j in range(6)]

    cs_swa = _rope_tables(L, T, SWA_HEAD_DIM)
    cs_mla = _rope_tables(L, T, MLA_ROPE)
    nm_descs = [_row(D), PAR, GRP, GRP]
    resid_descs = [_row(D, diff=False), _row(D), GRP]
    tab = [_row(LANES, diff=False), _row(LANES, diff=False)]
    swaq_descs = [_row(SWA_HEAD_DIM, 1), PAR] + tab
    swakv_descs = [_row(SWA_HEAD_DIM, 1), _row(SWA_HEAD_DIM, 1), PAR] + tab
    mlaq_descs = [_row(LANES, 1), _row(LANES, 1), PAR, PAR] + tab
    mlakv_descs = [_row(LANES, 1), _row(LANES, 1), _row(LANES), PAR, PAR] + tab
    ssdout_descs = [_row(SSM_INNER), _row(SSM_INNER, diff=False), _row(SSM_INNER), _row(SSM_INNER), PAR, PAR]
    merge_descs = [_row(D)] * 6
    swiglu_descs = [_row(ffn_h), _row(ffn_h)]
    seg_names = [sn for sn, _ in IN_SEGS]
    seg_groups = [seg_names[:7], seg_names[7:]]

    def layer_fwd(l, X, wt, p, mods):
        sh1, sc1, gt1, sh2, sc2, gt2 = mods
        r = {"X": X}
        r["h1"] = rf(f"l{l}_norm1", fn_norm_mod, nm_descs, [X, p["norm1_g"], sh1, sc1], [(D, 0, BF16)])[0]
        for gi, group in enumerate(seg_groups):
            dts = [BF16 if sn in ("g1", "g2", "g3") else F32 for sn in group]
            r.update(zip(group, matmul_multi(f"l{l}_in{gi}", r["h1"], [wt[sn] for sn in group], dts)))
        r["u"] = conv_fwd(f"l{l}_conv", r["xbc"], conv_w8[l], p["ssm_conv_b"], L)
        r["dts"] = rf(f"l{l}_softplus", fn_softplus, [_row(2 * SSM_HEADS), PAR], [r["dt"], p["dt_bias"]], [(2 * SSM_HEADS, 0, F32)])[0]
        for d in range(2):
            dt_d = r["dts"][:, d * SSM_HEADS:(d + 1) * SSM_HEADS]
            r[f"dt{d}"], r[f"dtt{d}"] = dt_d, dt_d.T
            r[f"y{d}"], r[f"hin{d}"] = ssd_fwd(f"l{l}_ssd{d}", d, r["u"], dt_d, dt_d.T, p["alog_row"][d], p["alog_col"][d], L)
        r["ys"] = rf(f"l{l}_ssd_out", fn_ssd_out, ssdout_descs, [r["y0"], r["y1"], r["u"], r["z"], p["d_lane"], p["ssm_norm_g"]],
                     [(SSM_INNER, 0, BF16)])[0]
        r["Qs"] = rf(f"l{l}_swa_q", fn_swa_q, swaq_descs, [r["qs"], p["swa_q_norm_g"], *cs_swa], [(SWA_HEAD_DIM, 1, BF16)], SWA_Q_HEADS)[0]
        r["Ks"], r["Vs"] = rf(f"l{l}_swa_kv", fn_swa_kv, swakv_descs, [r["ks"], r["vs"], p["swa_k_norm_g"], *cs_swa],
                              [(SWA_HEAD_DIM, 1, BF16), (SWA_HEAD_DIM, 1, BF16)], SWA_KV_HEADS)
        r["Os"], r["lse_s"] = flash_fwd(f"l{l}_swa_fwd", cfg_swa, r["Qs"], r["Ks"], r["Vs"], p["sink"])
        r["cqn"] = rf(f"l{l}_q_lat", fn_rms, [_row(MLA_Q_RANK), PAR], [r["cq"], p["mla_q_lat_g"]], [(MLA_Q_RANK, 0, BF16)])[0]
        r["qn"], r["qr"] = matmul_multi(f"l{l}_uq", r["cqn"], [wt["uqn"], wt["uqr"]])
        r["Qm"] = rf(f"l{l}_mla_q", fn_mla_q, mlaq_descs, [r["qn"], r["qr"], p["gqn"], p["gqr"], *cs_mla], [(MLA_QK_PAD, 1, BF16)], MLA_HEADS)[0]
        r["ckvn"] = rf(f"l{l}_kv_lat", fn_rms, [_row(MLA_KV_RANK), PAR], [r["ckv"], p["mla_kv_lat_g"]], [(MLA_KV_RANK, 0, BF16)])[0]
        r["kn"], r["vp"] = matmul_multi(f"l{l}_ukv", r["ckvn"], [wt["uk"], wt["uv"]])
        r["Km"], r["Vm"] = rf(f"l{l}_mla_kv", fn_mla_kv, mlakv_descs, [r["kn"], r["vp"], r["kr"], p["gkn"], p["gkr"], *cs_mla],
                              [(MLA_QK_PAD, 1, BF16), (MLA_V, 1, BF16)], MLA_HEADS)
        r["Om"], r["lse_m"] = flash_fwd(f"l{l}_mla_fwd", cfg_mla, r["Qm"], r["Km"], r["Vm"], None)
        r["P1"] = matmul(r["ys"], wt["w_p_ssm"], "nn", f"l{l}_p_ssm", out_dtype=BF16)
        r["P2"] = matmul(r["Os"], wt["w_p_swa"], "nn", f"l{l}_p_swa", out_dtype=BF16)
        r["P3"] = matmul(r["Om"], wt["w_p_mla"], "nn", f"l{l}_p_mla", out_dtype=BF16)
        r["mg"] = rf(f"l{l}_merge", fn_merge, merge_descs, [r["g1"], r["g2"], r["g3"], r["P1"], r["P2"], r["P3"]], [(D, 0, BF16)])[0]
        r["A"] = matmul(r["mg"], wt["w_out"], "nn", f"l{l}_out")
        r["X1"] = rf(f"l{l}_resid1", fn_resid, resid_descs, [X, r["A"], gt1], [(D, 0, F32)])[0]
        r["h2"] = rf(f"l{l}_norm2", fn_norm_mod, nm_descs, [r["X1"], p["norm2_g"], sh2, sc2], [(D, 0, BF16)])[0]
        r["Fg"] = matmul(r["h2"], wt["fg"], "nt", f"l{l}_ffn_g", out_dtype=BF16)
        r["Fu"] = matmul(r["h2"], wt["fu"], "nt", f"l{l}_ffn_u", out_dtype=BF16)
        r["sg"] = rf(f"l{l}_swiglu", fn_swiglu, swiglu_descs, [r["Fg"], r["Fu"]], [(ffn_h, 0, BF16)])[0]
        r["B"] = matmul(r["sg"], wt["w_ffn_out"], "nn", f"l{l}_ffn_out")
        X2 = rf(f"l{l}_resid2", fn_resid, resid_descs, [r["X1"], r["B"], gt2], [(D, 0, F32)])[0]
        return X2, r

    def attn_bwd(tag, cfg, q, k, v, o, lse, do, sink):
        res = flash_bwd_fused(f"{tag}_bwd", cfg, q, k, v, o, do, lse, sink)
        return res[0], res[1], res[2], (res[3] if cfg.has_sink else None)

    def layer_bwd(l, dX2, r, wt, p, mods):
        sh1, sc1, gt1, sh2, sc2, gt2 = mods
        g, gw = {}, {}
        dmod = [None] * 6
        dB, dmod[5] = rb(f"l{l}_resid2_b", fn_resid, resid_descs, [r["X1"], r["B"], gt2], [(D, 0, F32)], [dX2], to_matmul=(1,))
        dsg = matmul(dB, wt["w_ffn_out"], "nt", f"l{l}_ffn_out_da")
        gw["w_ffn_out"] = matmul(r["sg"], dB, "tn", f"l{l}_ffn_out_dw", out_dtype=BF16)
        dFg, dFu = rb(f"l{l}_swiglu_b", fn_swiglu, swiglu_descs, [r["Fg"], r["Fu"]], [(ffn_h, 0, BF16)], [dsg], to_matmul=(0, 1))
        dh2 = matmul(dFg, wt["fg"], "nn", f"l{l}_ffn_g_da")
        dh2 = matmul(dFu, wt["fu"], "nn", f"l{l}_ffn_u_da", add=dh2)
        gw["w_ffn_in"] = jnp.concatenate([matmul(r["h2"], dFg, "tn", f"l{l}_ffn_g_dw", out_dtype=BF16),
                                          matmul(r["h2"], dFu, "tn", f"l{l}_ffn_u_dw", out_dtype=BF16)], axis=1)
        dX1, g["norm2_g"], dmod[3], dmod[4] = rb(f"l{l}_norm2_b", fn_norm_mod, nm_descs, [r["X1"], p["norm2_g"], sh2, sc2],
                                                [(D, 0, BF16)], [dh2], add={0: dX2})
        dA, dmod[2] = rb(f"l{l}_resid1_b", fn_resid, resid_descs, [r["X"], r["A"], gt1], [(D, 0, F32)], [dX1], to_matmul=(1,))
        dmg = matmul(dA, wt["w_out"], "nt", f"l{l}_out_da")
        gw["w_out"] = matmul(r["mg"], dA, "tn", f"l{l}_out_dw", out_dtype=BF16)
        dsegs = {}
        dsegs["g1"], dsegs["g2"], dsegs["g3"], dP1, dP2, dP3 = rb(
            f"l{l}_merge_b", fn_merge, merge_descs, [r["g1"], r["g2"], r["g3"], r["P1"], r["P2"], r["P3"]], [(D, 0, BF16)], [dmg],
            to_matmul=tuple(range(6)))
        dys = matmul(dP1, wt["w_p_ssm"], "nt", f"l{l}_p_ssm_da")
        dOs = matmul(dP2, wt["w_p_swa"], "nt", f"l{l}_p_swa_da")
        dOm = matmul(dP3, wt["w_p_mla"], "nt", f"l{l}_p_mla_da")
        gw["w_p_ssm"] = matmul(r["ys"], dP1, "tn", f"l{l}_p_ssm_dw", out_dtype=BF16)
        gw["w_p_swa"] = matmul(r["Os"], dP2, "tn", f"l{l}_p_swa_dw", out_dtype=BF16)
        gw["w_p_mla"] = matmul(r["Om"], dP3, "tn", f"l{l}_p_mla_dw", out_dtype=BF16)
        dQm, dKm, dVm, _ = attn_bwd(f"l{l}_mla", cfg_mla_bwd, r["Qm"], r["Km"], r["Vm"], r["Om"], r["lse_m"], dOm, None)
        dkn, dvp, dsegs["kr"], dgkn, dgkr = rb(f"l{l}_mla_kv_b", fn_mla_kv, mlakv_descs,
                                               [r["kn"], r["vp"], r["kr"], p["gkn"], p["gkr"], *cs_mla],
                                               [(MLA_QK_PAD, 1, BF16), (MLA_V, 1, BF16)], [dKm, dVm], MLA_HEADS, to_matmul=(0, 1, 2))
        dckvn = matmul_sum(f"l{l}_ukv_da", [(dkn, wt["uk"]), (dvp, wt["uv"])])
        dw_uk = matmul(r["ckvn"], dkn, "tn", f"l{l}_uk_dw", out_dtype=BF16).reshape(MLA_KV_RANK, MLA_HEADS, MLA_NOPE)
        dw_uv = matmul(r["ckvn"], dvp, "tn", f"l{l}_uv_dw", out_dtype=BF16).reshape(MLA_KV_RANK, MLA_HEADS, MLA_V)
        gw["w_mla_ukv"] = jnp.concatenate([dw_uk, dw_uv], axis=2).reshape(MLA_KV_RANK, -1)
        dsegs["ckv"], g["mla_kv_lat_g"] = rb(f"l{l}_kv_lat_b", fn_rms, [_row(MLA_KV_RANK), PAR], [r["ckv"], p["mla_kv_lat_g"]],
                                             [(MLA_KV_RANK, 0, BF16)], [dckvn], to_matmul=(0,))
        dqn, dqr, dgqn, dgqr = rb(f"l{l}_mla_q_b", fn_mla_q, mlaq_descs, [r["qn"], r["qr"], p["gqn"], p["gqr"], *cs_mla],
                                  [(MLA_QK_PAD, 1, BF16)], [dQm], MLA_HEADS, to_matmul=(0, 1))
        dcqn = matmul_sum(f"l{l}_uq_da", [(dqn, wt["uqn"]), (dqr, wt["uqr"])])
        dw_uqn = matmul(r["cqn"], dqn, "tn", f"l{l}_uqn_dw", out_dtype=BF16).reshape(MLA_Q_RANK, MLA_HEADS, MLA_NOPE)
        dw_uqr = matmul(r["cqn"], dqr, "tn", f"l{l}_uqr_dw", out_dtype=BF16).reshape(MLA_Q_RANK, MLA_HEADS, LANES)[:, :, :MLA_ROPE]
        gw["w_mla_uq"] = jnp.concatenate([dw_uqn, dw_uqr], axis=2).reshape(MLA_Q_RANK, -1)
        dsegs["cq"], g["mla_q_lat_g"] = rb(f"l{l}_q_lat_b", fn_rms, [_row(MLA_Q_RANK), PAR], [r["cq"], p["mla_q_lat_g"]],
                                           [(MLA_Q_RANK, 0, BF16)], [dcqn], to_matmul=(0,))
        g["mla_q_norm_g"] = jnp.concatenate([dgqn[0], dgqr[0, :MLA_ROPE]])
        g["mla_k_norm_g"] = jnp.concatenate([dgkn[0], dgkr[0, :MLA_ROPE]])
        dQs, dKs, dVs, dsink = attn_bwd(f"l{l}_swa", cfg_swa, r["Qs"], r["Ks"], r["Vs"], r["Os"], r["lse_s"], dOs, p["sink"])
        g["swa_sink"] = dsink.reshape(SWA_Q_HEADS)
        dsegs["qs"], g["swa_q_norm_g"] = rb(f"l{l}_swa_q_b", fn_swa_q, swaq_descs, [r["qs"], p["swa_q_norm_g"], *cs_swa],
                                            [(SWA_HEAD_DIM, 1, BF16)], [dQs], SWA_Q_HEADS, to_matmul=(0,))
        dsegs["ks"], dsegs["vs"], g["swa_k_norm_g"] = rb(f"l{l}_swa_kv_b", fn_swa_kv, swakv_descs,
                                                         [r["ks"], r["vs"], p["swa_k_norm_g"], *cs_swa],
                                                         [(SWA_HEAD_DIM, 1, BF16), (SWA_HEAD_DIM, 1, BF16)], [dKs, dVs], SWA_KV_HEADS,
                                                         to_matmul=(0, 1))
        dy, dxs, dsegs["z"], dd_lane, g["ssm_norm_g"] = rb(
            f"l{l}_ssd_out_b", fn_ssd_out, ssdout_descs, [r["y0"], r["y1"], r["u"], r["z"], p["d_lane"], p["ssm_norm_g"]],
            [(SSM_INNER, 0, BF16)], [dys], to_matmul=(3,))
        g["ssm_d"] = dd_lane.reshape(SSM_HEADS, SSM_HEAD_DIM).sum(axis=1)
        du, ddts, dalog = None, [], []
        for d in range(2):
            du, ddt, ddtt, dar, dac = ssd_bwd(f"l{l}_ssd{d}_b", d, r["u"], r[f"dt{d}"], r[f"dtt{d}"], p["alog_row"][d], p["alog_col"][d],
                                              r[f"hin{d}"], dy, L, add_x=dxs if d == 0 else None, add_u=du)
            ddts.append(ddt + ddtt.T)
            dalog.append(dar[0] + dac[:, 0])
        g["ssm_a_log"] = jnp.stack(dalog)
        dsegs["xbc"], dconv_w, g["ssm_conv_b"] = conv_bwd(f"l{l}_conv_b", r["xbc"], conv_w8[l], p["ssm_conv_b"], du, L)
        dsegs["dt"], ddt_bias = rb(f"l{l}_softplus_b", fn_softplus, [_row(2 * SSM_HEADS), PAR], [r["dt"], p["dt_bias"]],
                                   [(2 * SSM_HEADS, 0, F32)], [jnp.concatenate(ddts, axis=1)], to_matmul=(0,))
        g["ssm_dt_bias"] = ddt_bias.reshape(2, SSM_HEADS)
        g["ssm_conv_w"] = dconv_w[:SSM_CONV]
        dh1, dws = None, []
        for gi, group in enumerate(seg_groups):
            dh1 = matmul_sum(f"l{l}_in_da{gi}", [(dsegs[sn], wt[sn]) for sn in group], add=dh1)
        for sn, w in zip(seg_names, in_widths):
            dws.append(matmul(r["h1"], dsegs[sn], "tn", f"l{l}_in_{sn}_dw", out_dtype=BF16)[:, :w])
        gw["w_in"] = jnp.concatenate(dws, axis=1)
        dX, g["norm1_g"], dmod[0], dmod[1] = rb(f"l{l}_norm1_b", fn_norm_mod, nm_descs, [r["X"], p["norm1_g"], sh1, sc1],
                                               [(D, 0, BF16)], [dh1], add={0: dX1})
        for n in ("norm1_g", "norm2_g", "ssm_conv_b", "ssm_norm_g", "swa_q_norm_g", "swa_k_norm_g", "mla_q_lat_g", "mla_kv_lat_g"):
            g[n] = g[n][0]
        dmod_lat = jnp.concatenate([dm[0, 0] for dm in dmod])
        dmod_ctx = jnp.concatenate([dm[1, 0] for dm in dmod])
        return dX, g, gw, dmod_lat, dmod_ctx

    X = jnp.concatenate([x, ctx], axis=0)
    saved = []
    for l in range(depth):
        wt, p, mods = layer_weights(l), layer_params(l), layer_mods(l)
        X, r = layer_fwd(l, X, wt, p, mods)
        saved.append((r, wt, p, mods))
    loss_part, dX = loss_and_grad("loss", X, loss_target[0], L)
    loss = lax.psum(loss_part[0, 0], ("x", "y", "c"))
    small_g = [None] * depth
    big_g = [None] * depth
    dmods = [None] * depth
    for l in reversed(range(depth)):
        r, wt, p, mods = saved[l]
        dX, small_g[l], big_g[l], dm_lat, dm_ctx = layer_bwd(l, dX, r, wt, p, mods)
        dmods[l] = jnp.stack([dm_lat, dm_ctx])
    grad_x = dX[:L][None]

    dm_all = all_gather_vmem("gather_dmods", jnp.stack(dmods).reshape(-1, LANES)).reshape(N_DEV, depth, 2, N_DEV * mod_cols)
    dm_rows = jnp.concatenate([jnp.moveaxis(dm_all[:, :, 0], 0, 1), dm_all[:, :, 1].sum(axis=0)[:, None],
                               jnp.zeros((depth, S_rows - N_DEV - 1, N_DEV * mod_cols), F32)], axis=1)
    dm_mine = lax.dynamic_slice(dm_rows, (0, 0, me * mod_cols), (depth, S_rows, mod_cols))
    grads = {}
    grads["w_mod"] = jnp.stack([matmul(S_mat, dm_mine[l], "tn", f"mod{l}_dw") for l in range(depth)])
    d_silu = None
    for l in range(depth):
        d_silu = matmul(dm_mine[l], inp["w_mod"][l], "nt", f"mod{l}_da", add=d_silu)
    small = {n: jnp.stack([small_g[l][n] for l in range(depth)]) for n in small_g[0]}
    small["c_ctx"] = small_bwd("silu_c_b", fn_silu, [inp["c_ctx"][None]], [d_silu[N_DEV:N_DEV + 1]])[0][0]
    small["b_mod"] = jnp.stack(dmods).sum(axis=1)

    rep_shapes = [inp[n].shape for n in REPLICATED]
    conv_shape = (depth, SSM_CONV, SSM_CONV_DIM)
    packed = _pack([small[n] for n in REPLICATED] + [small["ssm_conv_w"]], 8 * LANES)
    small_sum = sum_parts("sum_small", all_gather_vmem("gather_small", packed))
    summed = _unpack(small_sum, rep_shapes + [conv_shape])
    for n, gsum in zip(REPLICATED, summed):
        grads[n] = gsum
    grads["ssm_conv_w"] = lax.dynamic_slice(summed[-1], (0, 0, me * cw[2]), cw)

    slabs = []
    for n in GATHERED:
        gfull = jnp.stack([big_g[l][n] for l in range(depth)])
        if n in COLUMN_SHARDED:
            k_dim, n_dim = gfull.shape[1], gfull.shape[2]
            slab = jnp.moveaxis(gfull.reshape(depth, k_dim, N_DEV, n_dim // N_DEV), 2, 0)
        else:
            k_dim, n_dim = gfull.shape[1], gfull.shape[2]
            slab = jnp.moveaxis(gfull.reshape(depth, N_DEV, k_dim // N_DEV, n_dim), 1, 0)
        slabs.append(slab.reshape((N_CHIPS, 2) + slab.shape[1:]))
    from_sibling = exchange_sibling("exchange_sibling", slabs)
    chip_sums = []
    for n, slab, got in zip(GATHERED, slabs, from_sibling):
        shp = inp[n].shape
        rows = shp[0] * shp[1]
        chip_sums.append(pair_sum(f"pair_{n}", slab.reshape(N_CHIPS, 2, rows, shp[2]), got.reshape(N_CHIPS, rows, shp[2])))
    for n, parts in zip(GATHERED, exchange_chips("exchange_chips", chip_sums)):
        grads[n] = sum_parts(f"sum_{n}", parts).reshape(inp[n].shape)

    delta, new_m, new_v = {}, {}, {}
    rep_pack = lambda d: _pack([d[n] for n in REPLICATED], 8 * LANES)
    rep_out = adamw("adamw_small", rep_pack(inp), rep_pack(grads), rep_pack(mom_m), rep_pack(mom_v))
    for out, res in zip((delta, new_m, new_v), rep_out):
        for n, a in zip(REPLICATED, _unpack(res, rep_shapes)):
            out[n] = a
    for n in ["w_mod", "ssm_conv_w"] + GATHERED:
        shp = inp[n].shape
        two_d = (shp[0] * shp[1], shp[2])
        res = adamw(f"adamw_{n}", inp[n].reshape(two_d), grads[n].reshape(two_d), mom_m[n].reshape(two_d), mom_v[n].reshape(two_d))
        delta[n], new_m[n], new_v[n] = [a.reshape(shp) for a in res]

    return (loss, grad_x, *[grads[n] for n in WEIGHT_NAMES], *[delta[n] for n in WEIGHT_NAMES],
            *[new_m[n] for n in WEIGHT_NAMES], *[new_v[n] for n in WEIGHT_NAMES])
```
